```python
import math
import jax, jax.numpy as jnp
from jax import lax
import numpy as np

D_MODEL = 1024
BATCH = 8
SEQ = 4096
DEPTH = 1

SB_HEADS = 8
SB_HEAD_DIM = 64
SB_WIDTH = SB_HEADS * SB_HEAD_DIM
Q_BLOCK = 128
CONV_WIDTH = D_MODEL // 2
CONV_K = 3
MEM_LEN = 256
MEM_HEADS = 4
MEM_HEAD_DIM = D_MODEL // MEM_HEADS
FFN_HIDDEN = -(-8 * D_MODEL // (3 * 256)) * 256
EPS = 1e-6

IN_SPLITS = (SB_WIDTH, SB_WIDTH, SB_WIDTH, CONV_WIDTH, CONV_WIDTH, CONV_WIDTH, D_MODEL, D_MODEL)
IN_WIDTH = sum(IN_SPLITS)

kernel_name = "hybrid_stickbreak_shortconv_memxattn_swiglu"


def rms_norm(x, g):
    xf = x.astype(jnp.float32)
    var = jnp.mean(xf * xf, axis=-1, keepdims=True)
    return (xf * lax.rsqrt(var + EPS) * g.astype(jnp.float32)).astype(x.dtype)


def stick_breaking_attention(q, k, v):
    seq = q.shape[2]
    scale = 1.0 / math.sqrt(q.shape[-1])
    outs = []
    for blk in range(seq // Q_BLOCK):
        t0 = blk * Q_BLOCK
        n_keys = t0 + Q_BLOCK
        qb = q[:, :, t0:n_keys].astype(jnp.float32)
        kb = k[:, :, :n_keys].astype(jnp.float32)
        vb = v[:, :, :n_keys].astype(jnp.float32)
        z = jnp.einsum('bhqd,bhkd->bhqk', qb, kb) * scale
        q_pos = t0 + jnp.arange(Q_BLOCK)[:, None]
        k_pos = jnp.arange(n_keys)[None, :]
        mask = k_pos < q_pos
        log_fail = jnp.where(mask, jax.nn.log_sigmoid(-z), 0.0)
        log_later = lax.cumsum(log_fail, axis=3, reverse=True) - log_fail
        weights = jnp.where(mask, jnp.exp(jax.nn.log_sigmoid(z) + log_later), 0.0)
        outs.append(jnp.einsum('bhqk,bhkd->bhqd', weights, vb))
    return jnp.concatenate(outs, axis=2).astype(v.dtype)


def causal_depthwise_conv(u, w):
    c = u.shape[-1]
    return lax.conv_general_dilated(
        u, w[:, None, :].astype(u.dtype), window_strides=(1,), padding=((CONV_K - 1, 0),),
        dimension_numbers=('NWC', 'WIO', 'NWC'), feature_group_count=c)


def memory_cross_attention(h, m, w_q, w_kv, w_o):
    b, s, _ = h.shape
    mlen = m.shape[1]
    q = (h @ w_q).reshape(b, s, MEM_HEADS, MEM_HEAD_DIM)
    kv = (m @ w_kv).reshape(b, mlen, 2, MEM_HEADS, MEM_HEAD_DIM)
    k, v = kv[:, :, 0], kv[:, :, 1]
    scores = jnp.einsum('bshd,bmhd->bhsm', q.astype(jnp.float32), k.astype(jnp.float32))
    probs = jax.nn.softmax(scores / math.sqrt(MEM_HEAD_DIM), axis=-1)
    o = jnp.einsum('bhsm,bmhd->bshd', probs, v.astype(jnp.float32)).astype(h.dtype)
    return o.reshape(b, s, D_MODEL) @ w_o


def _fwd_setup_inputs(seed: int = 0) -> dict:
    key = jax.random.key(seed)
    ks = jax.random.split(key, 20)
    f32 = jnp.float32

    def w(k, shape, fan_in):
        return jax.random.normal(k, shape, f32) * (fan_in ** -0.5)

    def gain(k, shape):
        return 1.0 + 0.05 * jax.random.normal(k, shape, f32)

    return {
        "x": jax.random.normal(ks[0], (BATCH, SEQ, D_MODEL), f32),
        "mem": jax.random.normal(ks[1], (BATCH, MEM_LEN, D_MODEL), f32),
        "norm_mix": gain(ks[2], (DEPTH, D_MODEL)),
        "w_in": w(ks[3], (DEPTH, D_MODEL, IN_WIDTH), D_MODEL),
        "conv_w": w(ks[4], (DEPTH, CONV_K, CONV_WIDTH), CONV_K),
        "w_branch_a": w(ks[5], (DEPTH, SB_WIDTH, D_MODEL), SB_WIDTH),
        "w_branch_b": w(ks[6], (DEPTH, CONV_WIDTH, D_MODEL), CONV_WIDTH),
        "w_mix_out": w(ks[7], (DEPTH, D_MODEL, D_MODEL), D_MODEL),
        "norm_mem_q": gain(ks[8], (DEPTH, D_MODEL)),
        "norm_mem_kv": gain(ks[9], (DEPTH, D_MODEL)),
        "w_mem_q": w(ks[10], (DEPTH, D_MODEL, D_MODEL), D_MODEL),
        "w_mem_kv": w(ks[11], (DEPTH, D_MODEL, 2 * D_MODEL), D_MODEL),
        "w_mem_o": w(ks[12], (DEPTH, D_MODEL, D_MODEL), D_MODEL),
        "norm_ffn": gain(ks[13], (DEPTH, D_MODEL)),
        "w_ffn_in": w(ks[14], (DEPTH, D_MODEL, 2 * FFN_HIDDEN), D_MODEL),
        "w_ffn_out": w(ks[15], (DEPTH, FFN_HIDDEN, D_MODEL), FFN_HIDDEN),
        "norm_final": gain(ks[16], (D_MODEL,)),
    }


def _fwd_reference(x, mem, norm_mix, w_in, conv_w, w_branch_a, w_branch_b, w_mix_out,
              norm_mem_q, norm_mem_kv, w_mem_q, w_mem_kv, w_mem_o,
              norm_ffn, w_ffn_in, w_ffn_out, norm_final):
    b, s, _ = x.shape
    split_points = list(np.cumsum(IN_SPLITS)[:-1])
    for l in range(DEPTH):
        h = rms_norm(x, norm_mix[l])
        proj = h @ w_in[l]
        q_a, k_a, v_a, u_b, gate_b, gate_c, g_a, g_b = jnp.split(proj, split_points, axis=-1)

        def heads(t):
            return t.reshape(b, s, SB_HEADS, SB_HEAD_DIM).transpose(0, 2, 1, 3)

        o_a = stick_breaking_attention(heads(q_a), heads(k_a), heads(v_a))
        o_a = o_a.transpose(0, 2, 1, 3).reshape(b, s, SB_WIDTH)
        y_b = gate_b * causal_depthwise_conv(gate_c * u_b, conv_w[l])

        branch_a = o_a @ w_branch_a[l]
        branch_b = y_b @ w_branch_b[l]
        merged = jax.nn.sigmoid(g_a) * branch_a + jax.nn.sigmoid(g_b) * branch_b
        x = x + merged @ w_mix_out[l]

        x = x + memory_cross_attention(rms_norm(x, norm_mem_q[l]), rms_norm(mem, norm_mem_kv[l]),
                                       w_mem_q[l], w_mem_kv[l], w_mem_o[l])

        hf = rms_norm(x, norm_ffn[l])
        gate, up = jnp.split(hf @ w_ffn_in[l], 2, axis=-1)
        x = x + (jax.nn.silu(gate) * up) @ w_ffn_out[l]
    return rms_norm(x, norm_final)


import jax as _jax
import jax.numpy as _jnp

TWIN_FORMAT = 'train_step'
FWD_PARAMS = ['x', 'mem', 'norm_mix', 'w_in', 'conv_w', 'w_branch_a', 'w_branch_b', 'w_mix_out', 'norm_mem_q', 'norm_mem_kv', 'w_mem_q', 'w_mem_kv', 'w_mem_o', 'norm_ffn', 'w_ffn_in', 'w_ffn_out', 'norm_final']
TWIN_WEIGHTS = ['norm_mix', 'w_in', 'conv_w', 'w_branch_a', 'w_branch_b', 'w_mix_out', 'norm_mem_q', 'norm_mem_kv', 'w_mem_q', 'w_mem_kv', 'w_mem_o', 'norm_ffn', 'w_ffn_in', 'w_ffn_out', 'norm_final']
TWIN_DIFF_INPUT = 'x'
TWIN_INPUTS = ['x', 'mem', 'norm_mix', 'w_in', 'conv_w', 'w_branch_a', 'w_branch_b', 'w_mix_out', 'norm_mem_q', 'norm_mem_kv', 'w_mem_q', 'w_mem_kv', 'w_mem_o', 'norm_ffn', 'w_ffn_in', 'w_ffn_out', 'norm_final', 'loss_target', 'm_norm_mix', 'm_w_in', 'm_conv_w', 'm_w_branch_a', 'm_w_branch_b', 'm_w_mix_out', 'm_norm_mem_q', 'm_norm_mem_kv', 'm_w_mem_q', 'm_w_mem_kv', 'm_w_mem_o', 'm_norm_ffn', 'm_w_ffn_in', 'm_w_ffn_out', 'm_norm_final', 'v_norm_mix', 'v_w_in', 'v_conv_w', 'v_w_branch_a', 'v_w_branch_b', 'v_w_mix_out', 'v_norm_mem_q', 'v_norm_mem_kv', 'v_w_mem_q', 'v_w_mem_kv', 'v_w_mem_o', 'v_norm_ffn', 'v_w_ffn_in', 'v_w_ffn_out', 'v_norm_final']
TWIN_OUTPUTS = ['loss', 'grad_x', 'grad_norm_mix', 'grad_w_in', 'grad_conv_w', 'grad_w_branch_a', 'grad_w_branch_b', 'grad_w_mix_out', 'grad_norm_mem_q', 'grad_norm_mem_kv', 'grad_w_mem_q', 'grad_w_mem_kv', 'grad_w_mem_o', 'grad_norm_ffn', 'grad_w_ffn_in', 'grad_w_ffn_out', 'grad_norm_final', 'delta_norm_mix', 'delta_w_in', 'delta_conv_w', 'delta_w_branch_a', 'delta_w_branch_b', 'delta_w_mix_out', 'delta_norm_mem_q', 'delta_norm_mem_kv', 'delta_w_mem_q', 'delta_w_mem_kv', 'delta_w_mem_o', 'delta_norm_ffn', 'delta_w_ffn_in', 'delta_w_ffn_out', 'delta_norm_final', 'new_m_norm_mix', 'new_m_w_in', 'new_m_conv_w', 'new_m_w_branch_a', 'new_m_w_branch_b', 'new_m_w_mix_out', 'new_m_norm_mem_q', 'new_m_norm_mem_kv', 'new_m_w_mem_q', 'new_m_w_mem_kv', 'new_m_w_mem_o', 'new_m_norm_ffn', 'new_m_w_ffn_in', 'new_m_w_ffn_out', 'new_m_norm_final', 'new_v_norm_mix', 'new_v_w_in', 'new_v_conv_w', 'new_v_w_branch_a', 'new_v_w_branch_b', 'new_v_w_mix_out', 'new_v_norm_mem_q', 'new_v_norm_mem_kv', 'new_v_w_mem_q', 'new_v_w_mem_kv', 'new_v_w_mem_o', 'new_v_norm_ffn', 'new_v_w_ffn_in', 'new_v_w_ffn_out', 'new_v_norm_final']
TWIN_LEAF_KINDS = {'loss': 'loss', 'grad_x': 'grad_x', 'grad_norm_mix': 'grad_w', 'grad_w_in': 'grad_w', 'grad_conv_w': 'grad_w', 'grad_w_branch_a': 'grad_w', 'grad_w_branch_b': 'grad_w', 'grad_w_mix_out': 'grad_w', 'grad_norm_mem_q': 'grad_w', 'grad_norm_mem_kv': 'grad_w', 'grad_w_mem_q': 'grad_w', 'grad_w_mem_kv': 'grad_w', 'grad_w_mem_o': 'grad_w', 'grad_norm_ffn': 'grad_w', 'grad_w_ffn_in': 'grad_w', 'grad_w_ffn_out': 'grad_w', 'grad_norm_final': 'grad_w', 'delta_norm_mix': 'delta_w', 'delta_w_in': 'delta_w', 'delta_conv_w': 'delta_w', 'delta_w_branch_a': 'delta_w', 'delta_w_branch_b': 'delta_w', 'delta_w_mix_out': 'delta_w', 'delta_norm_mem_q': 'delta_w', 'delta_norm_mem_kv': 'delta_w', 'delta_w_mem_q': 'delta_w', 'delta_w_mem_kv': 'delta_w', 'delta_w_mem_o': 'delta_w', 'delta_norm_ffn': 'delta_w', 'delta_w_ffn_in': 'delta_w', 'delta_w_ffn_out': 'delta_w', 'delta_norm_final': 'delta_w', 'new_m_norm_mix': 'new_m', 'new_m_w_in': 'new_m', 'new_m_conv_w': 'new_m', 'new_m_w_branch_a': 'new_m', 'new_m_w_branch_b': 'new_m', 'new_m_w_mix_out': 'new_m', 'new_m_norm_mem_q': 'new_m', 'new_m_norm_mem_kv': 'new_m', 'new_m_w_mem_q': 'new_m', 'new_m_w_mem_kv': 'new_m', 'new_m_w_mem_o': 'new_m', 'new_m_norm_ffn': 'new_m', 'new_m_w_ffn_in': 'new_m', 'new_m_w_ffn_out': 'new_m', 'new_m_norm_final': 'new_m', 'new_v_norm_mix': 'new_v', 'new_v_w_in': 'new_v', 'new_v_conv_w': 'new_v', 'new_v_w_branch_a': 'new_v', 'new_v_w_branch_b': 'new_v', 'new_v_w_mix_out': 'new_v', 'new_v_norm_mem_q': 'new_v', 'new_v_norm_mem_kv': 'new_v', 'new_v_w_mem_q': 'new_v', 'new_v_w_mem_kv': 'new_v', 'new_v_w_mem_o': 'new_v', 'new_v_norm_ffn': 'new_v', 'new_v_w_ffn_in': 'new_v', 'new_v_w_ffn_out': 'new_v', 'new_v_norm_final': 'new_v'}


def _forward(args):
    return _fwd_reference(*[args[k] for k in FWD_PARAMS])


def _output_shape():
    def fwd():
        inp = _fwd_setup_inputs(0)
        return _fwd_reference(*[inp[k] for k in FWD_PARAMS])
    out = _jax.eval_shape(fwd)
    return out.shape, out.dtype

N_MICROBATCH = 1
ADAM_LR = 0.001
ADAM_B1 = 0.9
ADAM_B2 = 0.999
ADAM_EPS = 1e-08
ADAM_WD = 0.01
ADAM_STEP = 10
PER_EXAMPLE_BATCH_AXIS = {'x': 0, 'mem': 0, 'loss_target': 0}
SHARED_INPUTS = []
_WEIGHT_DTYPES = {'norm_mix': _jnp.float32, 'w_in': _jnp.float32, 'conv_w': _jnp.float32, 'w_branch_a': _jnp.float32, 'w_branch_b': _jnp.float32, 'w_mix_out': _jnp.float32, 'norm_mem_q': _jnp.float32, 'norm_mem_kv': _jnp.float32, 'w_mem_q': _jnp.float32, 'w_mem_kv': _jnp.float32, 'w_mem_o': _jnp.float32, 'norm_ffn': _jnp.float32, 'w_ffn_in': _jnp.float32, 'w_ffn_out': _jnp.float32, 'norm_final': _jnp.float32}
MOMENT_SCALE = {'norm_mix': 1.742411e-01, 'w_in': 7.784543e-02, 'conv_w': 1.263474e-01, 'w_branch_a': 5.877875e-02, 'w_branch_b': 8.955358e-02, 'w_mix_out': 1.081019e-01, 'norm_mem_q': 1.777432e-02, 'norm_mem_kv': 2.587272e-02, 'w_mem_q': 1.736425e-02, 'w_mem_kv': 1.759471e-02, 'w_mem_o': 1.782132e-02, 'norm_ffn': 1.227492e-01, 'w_ffn_in': 4.941001e-02, 'w_ffn_out': 8.111990e-02, 'norm_final': 3.210093e+01}


def _to_microbatches(a, axis):
    t = _jnp.moveaxis(a, axis, 0)
    t = t.reshape((N_MICROBATCH, t.shape[0] // N_MICROBATCH) + t.shape[1:])
    return _jnp.moveaxis(t, 1, axis + 1)


def setup_inputs(seed: int = 0) -> dict:
    inp = _fwd_setup_inputs(seed)
    key = _jax.random.fold_in(_jax.random.key(seed), 7919)
    shape, _ = _output_shape()
    out = dict(inp)
    out["loss_target"] = _jax.random.normal(_jax.random.fold_in(key, 0), shape, _jnp.float32)
    for i, name in enumerate(TWIN_WEIGHTS):
        w = inp[name].astype(_jnp.float32)
        if MOMENT_SCALE is None:
            s = _jnp.sqrt(_jnp.mean(_jnp.square(w)) + 1e-30)
        else:
            s = MOMENT_SCALE[name]
        km, kv = _jax.random.split(_jax.random.fold_in(key, i + 1))
        out[name] = w
        out["m_" + name] = s * _jax.random.normal(km, w.shape, _jnp.float32)
        out["v_" + name] = (s * s) * _jax.random.uniform(kv, w.shape, _jnp.float32, 0.5, 1.5)
    if N_MICROBATCH > 1:
        for name, axis in PER_EXAMPLE_BATCH_AXIS.items():
            out[name] = _to_microbatches(out[name], axis)
    return {'x': out['x'], 'mem': out['mem'], 'norm_mix': out['norm_mix'], 'w_in': out['w_in'], 'conv_w': out['conv_w'], 'w_branch_a': out['w_branch_a'], 'w_branch_b': out['w_branch_b'], 'w_mix_out': out['w_mix_out'], 'norm_mem_q': out['norm_mem_q'], 'norm_mem_kv': out['norm_mem_kv'], 'w_mem_q': out['w_mem_q'], 'w_mem_kv': out['w_mem_kv'], 'w_mem_o': out['w_mem_o'], 'norm_ffn': out['norm_ffn'], 'w_ffn_in': out['w_ffn_in'], 'w_ffn_out': out['w_ffn_out'], 'norm_final': out['norm_final'], 'loss_target': out['loss_target'], 'm_norm_mix': out['m_norm_mix'], 'm_w_in': out['m_w_in'], 'm_conv_w': out['m_conv_w'], 'm_w_branch_a': out['m_w_branch_a'], 'm_w_branch_b': out['m_w_branch_b'], 'm_w_mix_out': out['m_w_mix_out'], 'm_norm_mem_q': out['m_norm_mem_q'], 'm_norm_mem_kv': out['m_norm_mem_kv'], 'm_w_mem_q': out['m_w_mem_q'], 'm_w_mem_kv': out['m_w_mem_kv'], 'm_w_mem_o': out['m_w_mem_o'], 'm_norm_ffn': out['m_norm_ffn'], 'm_w_ffn_in': out['m_w_ffn_in'], 'm_w_ffn_out': out['m_w_ffn_out'], 'm_norm_final': out['m_norm_final'], 'v_norm_mix': out['v_norm_mix'], 'v_w_in': out['v_w_in'], 'v_conv_w': out['v_conv_w'], 'v_w_branch_a': out['v_w_branch_a'], 'v_w_branch_b': out['v_w_branch_b'], 'v_w_mix_out': out['v_w_mix_out'], 'v_norm_mem_q': out['v_norm_mem_q'], 'v_norm_mem_kv': out['v_norm_mem_kv'], 'v_w_mem_q': out['v_w_mem_q'], 'v_w_mem_kv': out['v_w_mem_kv'], 'v_w_mem_o': out['v_w_mem_o'], 'v_norm_ffn': out['v_norm_ffn'], 'v_w_ffn_in': out['v_w_ffn_in'], 'v_w_ffn_out': out['v_w_ffn_out'], 'v_norm_final': out['v_norm_final']}


def _loss(weights, diff, rest, loss_target):
    with _jax.named_scope("forward"):
        args = {**rest, TWIN_DIFF_INPUT: diff, **{k: w.astype(_WEIGHT_DTYPES[k]) for k, w in weights.items()}}
        y = _forward(args)
    with _jax.named_scope("loss_head"):
        err = _jnp.square(y.astype(_jnp.float32) - loss_target)
        return 0.5 * _jnp.sum(_jnp.mean(err, axis=-1)) if err.ndim else 0.5 * err


def _adamw(w, g, m, v):
    m = ADAM_B1 * m + (1.0 - ADAM_B1) * g
    v = ADAM_B2 * v + (1.0 - ADAM_B2) * _jnp.square(g)
    m_hat = m / (1.0 - ADAM_B1 ** ADAM_STEP)
    v_hat = v / (1.0 - ADAM_B2 ** ADAM_STEP)
    delta = -ADAM_LR * (m_hat / (_jnp.sqrt(v_hat) + ADAM_EPS) + ADAM_WD * w)
    return delta, m, v


def reference(x, mem, norm_mix, w_in, conv_w, w_branch_a, w_branch_b, w_mix_out, norm_mem_q, norm_mem_kv, w_mem_q, w_mem_kv, w_mem_o, norm_ffn, w_ffn_in, w_ffn_out, norm_final, loss_target, m_norm_mix, m_w_in, m_conv_w, m_w_branch_a, m_w_branch_b, m_w_mix_out, m_norm_mem_q, m_norm_mem_kv, m_w_mem_q, m_w_mem_kv, m_w_mem_o, m_norm_ffn, m_w_ffn_in, m_w_ffn_out, m_norm_final, v_norm_mix, v_w_in, v_conv_w, v_w_branch_a, v_w_branch_b, v_w_mix_out, v_norm_mem_q, v_norm_mem_kv, v_w_mem_q, v_w_mem_kv, v_w_mem_o, v_norm_ffn, v_w_ffn_in, v_w_ffn_out, v_norm_final):
    given = dict(x=x, mem=mem, norm_mix=norm_mix, w_in=w_in, conv_w=conv_w, w_branch_a=w_branch_a, w_branch_b=w_branch_b, w_mix_out=w_mix_out, norm_mem_q=norm_mem_q, norm_mem_kv=norm_mem_kv, w_mem_q=w_mem_q, w_mem_kv=w_mem_kv, w_mem_o=w_mem_o, norm_ffn=norm_ffn, w_ffn_in=w_ffn_in, w_ffn_out=w_ffn_out, norm_final=norm_final, loss_target=loss_target, m_norm_mix=m_norm_mix, m_w_in=m_w_in, m_conv_w=m_conv_w, m_w_branch_a=m_w_branch_a, m_w_branch_b=m_w_branch_b, m_w_mix_out=m_w_mix_out, m_norm_mem_q=m_norm_mem_q, m_norm_mem_kv=m_norm_mem_kv, m_w_mem_q=m_w_mem_q, m_w_mem_kv=m_w_mem_kv, m_w_mem_o=m_w_mem_o, m_norm_ffn=m_norm_ffn, m_w_ffn_in=m_w_ffn_in, m_w_ffn_out=m_w_ffn_out, m_norm_final=m_norm_final, v_norm_mix=v_norm_mix, v_w_in=v_w_in, v_conv_w=v_conv_w, v_w_branch_a=v_w_branch_a, v_w_branch_b=v_w_branch_b, v_w_mix_out=v_w_mix_out, v_norm_mem_q=v_norm_mem_q, v_norm_mem_kv=v_norm_mem_kv, v_w_mem_q=v_w_mem_q, v_w_mem_kv=v_w_mem_kv, v_w_mem_o=v_w_mem_o, v_norm_ffn=v_norm_ffn, v_w_ffn_in=v_w_ffn_in, v_w_ffn_out=v_w_ffn_out, v_norm_final=v_norm_final)
    weights = {n: given[n] for n in TWIN_WEIGHTS}
    shared = {n: given[n] for n in SHARED_INPUTS}
    per_example = {n: given[n] for n in ['x', 'mem']}
    grad_fn = _jax.value_and_grad(_loss, argnums=(0, 1))

    def one_microbatch(ex, loss_target):
        ex = dict(ex)
        diff = ex.pop(TWIN_DIFF_INPUT)
        return grad_fn(weights, diff, {**shared, **ex}, loss_target)

    if N_MICROBATCH == 1:
        loss, (grad_w, grad_x) = one_microbatch(per_example, given["loss_target"])
    else:
        def body(carry, xs):
            loss_sum, grad_sum = carry
            l_k, (gw_k, gx_k) = one_microbatch(xs[0], xs[1])
            with _jax.named_scope("update"):
                return (loss_sum + l_k, _jax.tree.map(_jnp.add, grad_sum, gw_k)), gx_k

        init = (_jnp.zeros((), _jnp.float32), _jax.tree.map(_jnp.zeros_like, weights))
        (loss, grad_w), grad_x = _jax.lax.scan(body, init, (per_example, given["loss_target"]))
    with _jax.named_scope("update"):
        delta_w, new_m, new_v = {}, {}, {}
        for n in TWIN_WEIGHTS:
            delta_w[n], new_m[n], new_v[n] = _adamw(weights[n], grad_w[n], given["m_" + n], given["v_" + n])
    return (loss, grad_x, *[grad_w[n] for n in TWIN_WEIGHTS], *[delta_w[n] for n in TWIN_WEIGHTS],
            *[new_m[n] for n in TWIN_WEIGHTS], *[new_v[n] for n in TWIN_WEIGHTS])
```

```python
import functools
import math

import jax
import jax.numpy as jnp
from jax import lax
from jax.experimental import pallas as pl
from jax.experimental.pallas import tpu as pltpu

BF = jnp.bfloat16
F32 = jnp.float32
MESH = pl.DeviceIdType.MESH

SB_HEAD_DIM = 64
LANES = 128
MEM_HEADS = 4
CONV_K = 3
CONV_ROWS = 8
EPS = 1e-6
N_CHIPS = 4
N_DEV = 8
VMEM_LIMIT = 56 * 1024 * 1024

ADAM_LR = 0.001
ADAM_B1 = 0.9
ADAM_B2 = 0.999
ADAM_EPS = 1e-08
ADAM_WD = 0.01
ADAM_STEP = 10

SMALL_ROWS = 16
ROW_CONV = 5
ROW_LOSS = 8


def _params(sem=None):
    return pltpu.CompilerParams(dimension_semantics=sem, vmem_limit_bytes=VMEM_LIMIT)


def _tile(dim, pref):
    if dim <= pref:
        return dim
    for step in (LANES, 8):
        t = (pref // step) * step
        while t >= step:
            if dim % t == 0:
                return t
            t -= step
    raise ValueError(f"no tile of {dim} under {pref}")


def _matmul(a, b, *, ta=False, tb=False, tm=1024, tn=512, tk=None, out_dtype=BF, resid=None, name):
    if ta:
        kdim, m = a.shape
    else:
        m, kdim = a.shape
    n = b.shape[0] if tb else b.shape[1]
    tm, tn = _tile(m, tm), _tile(n, tn)
    tk = _tile(kdim, tk or kdim)
    nk = kdim // tk
    a_spec = pl.BlockSpec((tk, tm), lambda i, j, k: (k, i)) if ta else pl.BlockSpec((tm, tk), lambda i, j, k: (i, k))
    b_spec = pl.BlockSpec((tn, tk), lambda i, j, k: (j, k)) if tb else pl.BlockSpec((tk, tn), lambda i, j, k: (k, j))
    o_spec = pl.BlockSpec((tm, tn), lambda i, j, k: (i, j))
    dims = (((0 if ta else 1,), (1 if tb else 0,)), ((), ()))
    has_res = resid is not None

    def body(*refs):
        a_ref, b_ref = refs[0], refs[1]
        o_ref = refs[2 + has_res]
        av, bv = a_ref[...], b_ref[...]
        if av.dtype != BF:
            av = av.astype(BF)
        if bv.dtype != BF:
            bv = bv.astype(BF)
        p = lax.dot_general(av, bv, dims, preferred_element_type=F32)

        def finish(acc):
            if has_res:
                acc = refs[2][...] + acc
            o_ref[...] = acc.astype(o_ref.dtype)

        if nk == 1:
            finish(p)
        else:
            acc_ref = refs[-1]
            k = pl.program_id(2)

            @pl.when(k == 0)
            def _():
                acc_ref[...] = p

            @pl.when(k > 0)
            def _():
                acc_ref[...] += p

            @pl.when(k == nk - 1)
            def _():
                finish(acc_ref[...])

    return pl.pallas_call(
        body, name=name, grid=(m // tm, n // tn, nk),
        in_specs=[a_spec, b_spec] + ([o_spec] if has_res else []),
        out_specs=o_spec, out_shape=jax.ShapeDtypeStruct((m, n), out_dtype),
        scratch_shapes=[pltpu.VMEM((tm, tn), F32)] if nk > 1 else [],
        compiler_params=_params(("parallel", "parallel", "arbitrary")),
    )(*([a, b] + ([resid] if has_res else [])))


def _rowwise(fn, ins, outs, *, rows, tm, name, accs=()):
    tm = _tile(rows, tm)
    in_specs, args = [], []
    for arr, cols, cb in ins:
        if cols is None:
            in_specs.append(pl.BlockSpec(arr.shape, lambda i, nd=arr.ndim: (0,) * nd))
        else:
            in_specs.append(pl.BlockSpec((tm, cols), lambda i, cb=cb: (i, cb)))
        args.append(arr)
    out_specs = [pl.BlockSpec((tm, cols), lambda i: (i, 0)) for cols, _ in outs]
    out_shape = [jax.ShapeDtypeStruct((rows, cols), dt) for cols, dt in outs]
    for r, c in accs:
        out_specs.append(pl.BlockSpec((r, c), lambda i: (0, 0)))
        out_shape.append(jax.ShapeDtypeStruct((r, c), F32))
    n_in, n_out = len(ins), len(outs)

    def body(*refs):
        res = fn(*[r[...] for r in refs[:n_in]])
        if not isinstance(res, (tuple, list)):
            res = (res,)
        for o_ref, val in zip(refs[n_in:n_in + n_out], res[:n_out]):
            o_ref[...] = val.astype(o_ref.dtype)
        first = pl.program_id(0) == 0
        for a_ref, val in zip(refs[n_in + n_out:], res[n_out:]):
            @pl.when(first)
            def _(a_ref=a_ref, val=val):
                a_ref[...] = val

            @pl.when(jnp.logical_not(first))
            def _(a_ref=a_ref, val=val):
                a_ref[...] += val

    res = pl.pallas_call(
        body, name=name, grid=(rows // tm,), in_specs=in_specs, out_specs=out_specs, out_shape=out_shape,
        compiler_params=_params(("arbitrary",) if accs else ("parallel",)),
    )(*args)
    return res


def _rstd(xf):
    return lax.rsqrt(jnp.mean(xf * xf, axis=-1, keepdims=True) + EPS)


def _rmsnorm(x, g, name):
    rows, d = x.shape
    return _rowwise(lambda xv, gv: xv * _rstd(xv) * gv, [(x, d, 0), (g, None, None)], [(d, BF)],
                    rows=rows, tm=512, name=name)[0]


def _rmsnorm_bwd(x, g, dy, resid, name):
    rows, d = x.shape

    def fn(xv, gv, dyv, *rest):
        dyv = dyv.astype(F32)
        r = _rstd(xv)
        xn = xv * r
        dxn = dyv * gv
        dx = r * (dxn - xn * jnp.mean(dxn * xn, axis=-1, keepdims=True))
        if rest:
            dx = rest[0] + dx
        return dx, jnp.sum(dyv * xn, axis=0, keepdims=True)

    ins = [(x, d, 0), (g, None, None), (dy, d, 0)] + ([(resid, d, 0)] if resid is not None else [])
    outs = [(d, F32)]
    return _rowwise(fn, ins, outs, rows=rows, tm=512, name=name, accs=[(1, d)])


def _loss_head(x, g, target, name):
    rows, d = x.shape

    def fn(xv, gv, tv):
        r = _rstd(xv)
        xn = xv * r
        err = xn * gv - tv
        per_tok = jnp.mean(err * err, axis=-1, keepdims=True)
        loss = 0.5 * jnp.sum(per_tok, axis=0, keepdims=True)
        dyv = err * (1.0 / d)
        dxn = dyv * gv
        dx = r * (dxn - xn * jnp.mean(dxn * xn, axis=-1, keepdims=True))
        return dx, jnp.sum(dyv * xn, axis=0, keepdims=True), jnp.broadcast_to(loss, (1, LANES))

    return _rowwise(fn, [(x, d, 0), (g, None, None), (target, d, 0)], [(d, F32)],
                    rows=rows, tm=512, name=name, accs=[(1, d), (1, LANES)])


def _sb_consts(tq):
    row = lax.broadcasted_iota(jnp.int32, (tq, tq), 0)
    col = lax.broadcasted_iota(jnp.int32, (tq, tq), 1)
    causal = col < row
    rj = lax.broadcasted_iota(jnp.int32, (tq, 2 * tq), 0)
    cj = lax.broadcasted_iota(jnp.int32, (tq, 2 * tq), 1)
    ones_half = cj >= tq
    later = jnp.where((rj > cj) | ones_half, 1.0, 0.0).astype(BF)
    later_incl = jnp.where((rj >= cj) | ones_half, 1.0, 0.0).astype(BF)
    return causal, later, later_incl


def _split_dot(val, rhs):
    hi = val.astype(BF)
    lo = (val - hi.astype(F32)).astype(BF)
    return jnp.dot(hi, rhs, preferred_element_type=F32) + jnp.dot(lo, rhs, preferred_element_type=F32)


def _log_terms(z):
    sp = jnp.maximum(z, 0.0) + jnp.log(1.0 + jnp.exp(-jnp.abs(z)))
    return z - sp, -sp


NT = (((1,), (1,)), ((), ()))
TN = (((0,), (0,)), ((), ()))


def _head_masks(tq):
    lane = lax.broadcasted_iota(jnp.int32, (tq, LANES), 1)
    return [lane < SB_HEAD_DIM, lane >= SB_HEAD_DIM]


def _sb_fwd(proj, tq=128):
    s = proj.shape[0]
    tq = _tile(s, tq)
    n_pairs = 4
    scale = 1.0 / math.sqrt(SB_HEAD_DIM)

    def body(q_ref, k_ref, v_ref, o_ref, o32_ref):
        i = pl.program_id(1)
        causal, later, _ = _sb_consts(tq)
        q2 = q_ref[...]
        masks = _head_masks(tq)
        accs = []
        for hm in masks:
            qh = (jnp.where(hm, q2.astype(F32), 0.0) * scale).astype(BF)

            def tile(j, carry, acc, masked, qh=qh):
                rows = pl.ds(pl.multiple_of(j * tq, tq), tq)
                k2, v2 = k_ref[rows, :], v_ref[rows, :]
                z = lax.dot_general(qh, k2, NT, preferred_element_type=F32)
                log_b, log_f = _log_terms(z)
                if masked:
                    log_f = jnp.where(causal, log_f, 0.0)
                sums = _split_dot(log_f, later)
                w = jnp.exp(log_b + sums[:, :tq] + carry)
                if masked:
                    w = jnp.where(causal, w, 0.0)
                acc = acc + jnp.dot(w.astype(BF), v2, preferred_element_type=F32)
                return carry + sums[:, tq:], acc

            carry, acc = tile(i, jnp.zeros((tq, tq), F32), jnp.zeros((tq, LANES), F32), True)
            carry, acc = lax.fori_loop(
                0, i, lambda jj, ca: tile(i - 1 - jj, ca[0], ca[1], False), (carry, acc))
            accs.append(acc)
        out = jnp.where(masks[0], accs[0], accs[1])
        o_ref[...] = out.astype(o_ref.dtype)
        o32_ref[...] = out

    tok = pl.BlockSpec((tq, LANES), lambda p, i: (i, p))
    return pl.pallas_call(
        body, name="sb_attn_fwd", grid=(n_pairs, s // tq),
        in_specs=[tok,
                  pl.BlockSpec((s, LANES), lambda p, i: (0, n_pairs + p)),
                  pl.BlockSpec((s, LANES), lambda p, i: (0, 2 * n_pairs + p))],
        out_specs=[tok, tok],
        out_shape=[jax.ShapeDtypeStruct((s, n_pairs * LANES), BF), jax.ShapeDtypeStruct((s, n_pairs * LANES), F32)],
        compiler_params=_params(("parallel", "arbitrary")),
    )(proj, proj, proj)


def _sb_bwd(proj, o32, do_a, tq=128):
    s = proj.shape[0]
    tq = _tile(s, tq)
    n_pairs = 4
    scale = 1.0 / math.sqrt(SB_HEAD_DIM)

    def body(q_ref, k_ref, v_ref, o_ref, do_ref, dq_ref, dk_ref, dv_ref):
        i = pl.program_id(1)

        @pl.when(i == 0)
        def _():
            dk_ref[...] = jnp.zeros_like(dk_ref)
            dv_ref[...] = jnp.zeros_like(dv_ref)

        causal, later, later_incl = _sb_consts(tq)
        q2, do2 = q_ref[...], do_ref[...]
        prod = do2.astype(F32) * o_ref[...]
        masks = _head_masks(tq)
        dqs = []
        for hm in masks:
            qh = (jnp.where(hm, q2.astype(F32), 0.0) * scale).astype(BF)
            doh = jnp.where(hm, do2.astype(F32), 0.0).astype(BF)
            total = jnp.broadcast_to(jnp.sum(jnp.where(hm, prod, 0.0), axis=-1, keepdims=True), (tq, tq))

            def tile(j, c_log, c_g, dq, masked, qh=qh, doh=doh, total=total):
                rows = pl.ds(pl.multiple_of(j * tq, tq), tq)
                k2, v2 = k_ref[rows, :], v_ref[rows, :]
                z = lax.dot_general(qh, k2, NT, preferred_element_type=F32)
                log_b, log_f = _log_terms(z)
                beta, fail = jnp.exp(log_b), jnp.exp(log_f)
                if masked:
                    log_f = jnp.where(causal, log_f, 0.0)
                sums = _split_dot(log_f, later)
                w = jnp.exp(log_b + sums[:, :tq] + c_log)
                if masked:
                    w = jnp.where(causal, w, 0.0)
                wb = w.astype(BF)
                dw = lax.dot_general(doh, v2, NT, preferred_element_type=F32)
                g = wb.astype(F32) * dw
                gsum = _split_dot(g, later_incl)
                earlier = total - (gsum[:, :tq] + c_g)
                dz = g * fail - earlier * beta
                if masked:
                    dz = jnp.where(causal, dz, 0.0)
                dzb = dz.astype(BF)
                dq = dq + jnp.dot(dzb, k2, preferred_element_type=F32)
                dk_ref[rows, :] += lax.dot_general(dzb, qh, TN, preferred_element_type=F32)
                dv_ref[rows, :] += lax.dot_general(wb, doh, TN, preferred_element_type=F32)
                return c_log + sums[:, tq:], c_g + gsum[:, tq:], dq

            zero = jnp.zeros((tq, tq), F32)
            state = tile(i, zero, zero, jnp.zeros((tq, LANES), F32), True)
            state = lax.fori_loop(0, i, lambda jj, st: tile(i - 1 - jj, st[0], st[1], st[2], False), state)
            dqs.append(state[2])
        dq_ref[...] = (jnp.where(masks[0], dqs[0], dqs[1]) * scale).astype(dq_ref.dtype)

    width = n_pairs * LANES
    return pl.pallas_call(
        body, name="sb_attn_bwd", grid=(n_pairs, s // tq),
        in_specs=[pl.BlockSpec((tq, LANES), lambda p, i: (i, p)),
                  pl.BlockSpec((s, LANES), lambda p, i: (0, n_pairs + p)),
                  pl.BlockSpec((s, LANES), lambda p, i: (0, 2 * n_pairs + p)),
                  pl.BlockSpec((tq, LANES), lambda p, i: (i, p)),
                  pl.BlockSpec((tq, LANES), lambda p, i: (i, p))],
        out_specs=[pl.BlockSpec((tq, LANES), lambda p, i: (i, p)),
                   pl.BlockSpec((s, LANES), lambda p, i: (0, p)),
                   pl.BlockSpec((s, LANES), lambda p, i: (0, p))],
        out_shape=[jax.ShapeDtypeStruct((s, width), BF), jax.ShapeDtypeStruct((s, width), F32),
                   jax.ShapeDtypeStruct((s, width), F32)],
        compiler_params=_params(("parallel", "arbitrary")),
    )(proj, proj, proj, o32, do_a)


CONV_COL0 = 12


def _shift_rows(v, k):
    n = v.shape[0]
    row = lax.broadcasted_iota(jnp.int32, v.shape, 0)
    rolled = pltpu.roll(v, k % n, axis=0)
    keep = row >= k if k > 0 else row < n + k
    return jnp.where(keep, rolled, 0.0)


def _conv_specs(s):
    return [pl.BlockSpec((s, LANES), lambda cb: (0, CONV_COL0 + cb)),
            pl.BlockSpec((s, LANES), lambda cb: (0, CONV_COL0 + 4 + cb)),
            pl.BlockSpec((s, LANES), lambda cb: (0, CONV_COL0 + 8 + cb)),
            pl.BlockSpec((CONV_ROWS, LANES), lambda cb: (0, cb))]


def _conv_fwd(proj, conv_w):
    s = proj.shape[0]

    def body(u_ref, gb_ref, gc_ref, w_ref, y_ref):
        cu = gc_ref[...].astype(F32) * u_ref[...].astype(F32)
        w = w_ref[...]
        y = w[0:1] * _shift_rows(cu, 2) + w[1:2] * _shift_rows(cu, 1) + w[2:3] * cu
        y_ref[...] = (gb_ref[...].astype(F32) * y).astype(y_ref.dtype)

    return pl.pallas_call(
        body, name="conv_fwd", grid=(4,), in_specs=_conv_specs(s),
        out_specs=pl.BlockSpec((s, LANES), lambda cb: (0, cb)),
        out_shape=jax.ShapeDtypeStruct((s, 4 * LANES), BF),
        compiler_params=_params(("parallel",)),
    )(proj, proj, proj, conv_w)


def _conv_bwd(proj, conv_w, dy):
    s = proj.shape[0]

    def body(u_ref, gb_ref, gc_ref, w_ref, dy_ref, du_ref, dgb_ref, dgc_ref, dw_ref):
        u, gc = u_ref[...].astype(F32), gc_ref[...].astype(F32)
        dyv = dy_ref[...].astype(F32)
        w = w_ref[...]
        cu = gc * u
        cu1, cu2 = _shift_rows(cu, 1), _shift_rows(cu, 2)
        conv = w[0:1] * cu2 + w[1:2] * cu1 + w[2:3] * cu
        dgb_ref[...] = (dyv * conv).astype(dgb_ref.dtype)
        dc = dyv * gb_ref[...].astype(F32)
        dcu = w[2:3] * dc + w[1:2] * _shift_rows(dc, -1) + w[0:1] * _shift_rows(dc, -2)
        dgc_ref[...] = (dcu * u).astype(dgc_ref.dtype)
        du_ref[...] = (dcu * gc).astype(du_ref.dtype)
        tap_row = lax.broadcasted_iota(jnp.int32, (CONV_ROWS, LANES), 0)
        dw = jnp.zeros((CONV_ROWS, LANES), F32)
        for t, shifted in enumerate((cu2, cu1, cu)):
            dw = jnp.where(tap_row == t, jnp.sum(dc * shifted, axis=0, keepdims=True), dw)
        dw_ref[...] = dw

    col = pl.BlockSpec((s, LANES), lambda cb: (0, cb))
    act = jax.ShapeDtypeStruct((s, 4 * LANES), BF)
    return pl.pallas_call(
        body, name="conv_bwd", grid=(4,), in_specs=_conv_specs(s) + [col],
        out_specs=[col, col, col, pl.BlockSpec((CONV_ROWS, LANES), lambda cb: (0, cb))],
        out_shape=[act, act, act, jax.ShapeDtypeStruct((CONV_ROWS, 4 * LANES), F32)],
        compiler_params=_params(("parallel",)),
    )(proj, proj, proj, conv_w, dy)


def _mem_probs(q, k, scale):
    sc = lax.dot_general(q, k, NT, preferred_element_type=F32) * scale
    p = jnp.exp(sc - jnp.max(sc, axis=-1, keepdims=True))
    return p / jnp.sum(p, axis=-1, keepdims=True)


def _mem_fwd(q_m, kv, tq=512):
    s, d = q_m.shape
    mlen = kv.shape[0]
    hd = d // MEM_HEADS
    tq = _tile(s, tq)
    scale = 1.0 / math.sqrt(hd)

    def body(q_ref, k_ref, v_ref, o_ref):
        p = _mem_probs(q_ref[...], k_ref[...], scale)
        o_ref[...] = jnp.dot(p.astype(BF), v_ref[...], preferred_element_type=F32).astype(o_ref.dtype)

    return pl.pallas_call(
        body, name="mem_attn_fwd", grid=(MEM_HEADS, s // tq),
        in_specs=[pl.BlockSpec((tq, hd), lambda h, i: (i, h)),
                  pl.BlockSpec((mlen, hd), lambda h, i: (0, h)),
                  pl.BlockSpec((mlen, hd), lambda h, i: (0, MEM_HEADS + h))],
        out_specs=pl.BlockSpec((tq, hd), lambda h, i: (i, h)),
        out_shape=jax.ShapeDtypeStruct((s, d), BF),
        compiler_params=_params(("parallel", "parallel")),
    )(q_m, kv, kv)


def _mem_bwd(q_m, kv, do_m, tq=512):
    s, d = q_m.shape
    mlen = kv.shape[0]
    hd = d // MEM_HEADS
    tq = _tile(s, tq)
    scale = 1.0 / math.sqrt(hd)

    def body(q_ref, k_ref, v_ref, do_ref, dq_ref, dk_ref, dv_ref):
        q, k, v, do = q_ref[...], k_ref[...], v_ref[...], do_ref[...]
        p = _mem_probs(q, k, scale)
        dp = lax.dot_general(do, v, NT, preferred_element_type=F32)
        ds = p * (dp - jnp.sum(dp * p, axis=-1, keepdims=True)) * scale
        dsb = ds.astype(BF)
        dq_ref[...] = jnp.dot(dsb, k, preferred_element_type=F32).astype(dq_ref.dtype)
        dk = lax.dot_general(dsb, q, TN, preferred_element_type=F32)
        dv = lax.dot_general(p.astype(BF), do, TN, preferred_element_type=F32)
        first = pl.program_id(1) == 0

        @pl.when(first)
        def _():
            dk_ref[...] = dk
            dv_ref[...] = dv

        @pl.when(jnp.logical_not(first))
        def _():
            dk_ref[...] += dk
            dv_ref[...] += dv

    tok = pl.BlockSpec((tq, hd), lambda h, i: (i, h))
    memb = pl.BlockSpec((mlen, hd), lambda h, i: (0, h))
    return pl.pallas_call(
        body, name="mem_attn_bwd", grid=(MEM_HEADS, s // tq),
        in_specs=[tok, memb, pl.BlockSpec((mlen, hd), lambda h, i: (0, MEM_HEADS + h)), tok],
        out_specs=[tok, memb, memb],
        out_shape=[jax.ShapeDtypeStruct((s, d), BF), jax.ShapeDtypeStruct((mlen, d), F32),
                   jax.ShapeDtypeStruct((mlen, d), F32)],
        compiler_params=_params(("parallel", "arbitrary")),
    )(q_m, kv, kv, do_m)


def _place():
    x, y, c = lax.axis_index("x"), lax.axis_index("y"), lax.axis_index("c")
    other_chips = [(1 - x, y), (x, 1 - y), (1 - x, 1 - y)]
    return x, y, c, other_chips


def _chip_no(cx, cy):
    return 2 * cx + cy


HBM_SPEC = pl.BlockSpec(memory_space=pl.ANY)


def _shard_view(ref, axis, j, width):
    start = pl.multiple_of(j * width, width)
    return ref.at[pl.ds(start, width), :] if axis == 0 else ref.at[:, pl.ds(start, width)]


def _gather_weights(shards, axes):
    n = len(shards)
    full_shapes = [(sh.shape[0] * (N_CHIPS if ax == 0 else 1), sh.shape[1] * (N_CHIPS if ax == 1 else 1))
                   for sh, ax in zip(shards, axes)]

    def body(*refs):
        ins, outs = refs[:n], refs[n:2 * n]
        send, recv, local = refs[2 * n:]
        x, y, c, others = _place()
        me = _chip_no(x, y)
        copies = []
        for w in range(n):
            width = ins[w].shape[axes[w]]
            mine = _shard_view(outs[w], axes[w], me, width)
            cp = pltpu.make_async_copy(ins[w], mine, local.at[w])
            cp.start()
            copies.append(cp)
            for p, chip in enumerate(others):
                pltpu.make_async_remote_copy(
                    src_ref=ins[w], dst_ref=mine, send_sem=send.at[w, p], recv_sem=recv.at[w, p],
                    device_id=(chip[0], chip[1], c), device_id_type=MESH).start()
        for w in range(n):
            width = ins[w].shape[axes[w]]
            for p, chip in enumerate(others):
                theirs = _shard_view(outs[w], axes[w], _chip_no(*chip), width)
                arrival = pltpu.make_async_remote_copy(
                    src_ref=ins[w], dst_ref=theirs, send_sem=send.at[w, p], recv_sem=recv.at[w, p],
                    device_id=(chip[0], chip[1], c), device_id_type=MESH)
                arrival.wait_recv()
                arrival.wait_send()
            copies[w].wait()

    return pl.pallas_call(
        body, name="gather_weights",
        in_specs=[HBM_SPEC] * n, out_specs=[HBM_SPEC] * n,
        out_shape=[jax.ShapeDtypeStruct(fs, sh.dtype) for fs, sh in zip(full_shapes, shards)],
        scratch_shapes=[pltpu.SemaphoreType.DMA((n, 3)), pltpu.SemaphoreType.DMA((n, 3)),
                        pltpu.SemaphoreType.DMA((n,))],
        compiler_params=pltpu.CompilerParams(has_side_effects=True),
    )(*shards)


def _pair_exchange(grads):
    n = len(grads)

    def body(*refs):
        ins, outs = refs[:n], refs[n:2 * n]
        send, recv = refs[2 * n:]
        x, y, c, _ = _place()
        cps = []
        for w in range(n):
            cp = pltpu.make_async_remote_copy(
                src_ref=ins[w].at[:, 1 - c], dst_ref=outs[w], send_sem=send.at[w], recv_sem=recv.at[w],
                device_id=(x, y, 1 - c), device_id_type=MESH)
            cp.start()
            cps.append(cp)
        for cp in cps:
            cp.wait()

    return pl.pallas_call(
        body, name="grad_pair_exchange",
        in_specs=[HBM_SPEC] * n, out_specs=[HBM_SPEC] * n,
        out_shape=[jax.ShapeDtypeStruct((g.shape[0],) + g.shape[2:], g.dtype) for g in grads],
        scratch_shapes=[pltpu.SemaphoreType.DMA((n,)), pltpu.SemaphoreType.DMA((n,))],
        compiler_params=pltpu.CompilerParams(has_side_effects=True),
    )(*grads)


def _pair_add(g4, got, core, name):
    nj, _, hr, cdim = g4.shape
    tr = _tile(hr, max(8, 524288 // cdim))

    def body(core_ref, a_ref, b_ref, o_ref):
        o_ref[...] = a_ref[...] + b_ref[...]

    return pl.pallas_call(
        body, name=name,
        grid_spec=pltpu.PrefetchScalarGridSpec(
            num_scalar_prefetch=1, grid=(nj, hr // tr),
            in_specs=[pl.BlockSpec((1, None, tr, cdim), lambda j, i, core_ref: (j, core_ref[0], i, 0)),
                      pl.BlockSpec((1, tr, cdim), lambda j, i, core_ref: (j, i, 0))],
            out_specs=pl.BlockSpec((1, tr, cdim), lambda j, i, core_ref: (j, i, 0))),
        out_shape=jax.ShapeDtypeStruct((nj, hr, cdim), F32),
        compiler_params=_params(("parallel", "parallel")),
    )(core, g4, got)


def _piece(ref, axis, j, hc):
    if axis == 0:
        return ref.at[j]
    return ref.at[0, :, pl.ds(pl.multiple_of(j * hc, hc), hc)]


def _chip_exchange(sums, axes):
    n = len(sums)
    shapes = []
    for sm, ax in zip(sums, axes):
        shapes.append((N_CHIPS, sm.shape[1], sm.shape[2] // (1 if ax == 0 else N_CHIPS)))

    def body(*refs):
        ins, outs = refs[:n], refs[n:2 * n]
        send, recv, local = refs[2 * n:]
        x, y, c, others = _place()
        me = _chip_no(x, y)
        copies = []
        for w in range(n):
            hc = shapes[w][2]
            cp = pltpu.make_async_copy(_piece(ins[w], axes[w], me, hc), outs[w].at[me], local.at[w])
            cp.start()
            copies.append(cp)
            for p, chip in enumerate(others):
                pltpu.make_async_remote_copy(
                    src_ref=_piece(ins[w], axes[w], _chip_no(*chip), hc), dst_ref=outs[w].at[me],
                    send_sem=send.at[w, p], recv_sem=recv.at[w, p],
                    device_id=(chip[0], chip[1], c), device_id_type=MESH).start()
        for w in range(n):
            hc = shapes[w][2]
            for p, chip in enumerate(others):
                arrival = pltpu.make_async_remote_copy(
                    src_ref=_piece(ins[w], axes[w], _chip_no(*chip), hc), dst_ref=outs[w].at[_chip_no(*chip)],
                    send_sem=send.at[w, p], recv_sem=recv.at[w, p],
                    device_id=(chip[0], chip[1], c), device_id_type=MESH)
                arrival.wait_recv()
                arrival.wait_send()
            copies[w].wait()

    return pl.pallas_call(
        body, name="grad_chip_exchange",
        in_specs=[HBM_SPEC] * n, out_specs=[HBM_SPEC] * n,
        out_shape=[jax.ShapeDtypeStruct(sh, sm.dtype) for sh, sm in zip(shapes, sums)],
        scratch_shapes=[pltpu.SemaphoreType.DMA((n, 3)), pltpu.SemaphoreType.DMA((n, 3)),
                        pltpu.SemaphoreType.DMA((n,))],
        compiler_params=pltpu.CompilerParams(has_side_effects=True),
    )(*sums)


def _chip_sum(slots, name):
    _, hr, hc = slots.shape
    tr = _tile(hr, 256)

    def body(s_ref, o_ref):
        o_ref[...] = ((s_ref[0].astype(F32) + s_ref[1].astype(F32)) + s_ref[2].astype(F32)) + s_ref[3].astype(F32)

    return pl.pallas_call(
        body, name=name, grid=(hr // tr,),
        in_specs=[pl.BlockSpec((N_CHIPS, tr, hc), lambda i: (0, i, 0))],
        out_specs=pl.BlockSpec((tr, hc), lambda i: (i, 0)),
        out_shape=jax.ShapeDtypeStruct((hr, hc), F32),
        compiler_params=_params(("parallel",)),
    )(slots)


def _half_swap(halves):
    n = len(halves)

    def body(*refs):
        ins, outs = refs[:n], refs[n:2 * n]
        send, recv, local = refs[2 * n:]
        x, y, c, _ = _place()
        cps, lcs = [], []
        for w in range(n):
            lc = pltpu.make_async_copy(ins[w], outs[w].at[c], local.at[w])
            lc.start()
            lcs.append(lc)
            cp = pltpu.make_async_remote_copy(
                src_ref=ins[w], dst_ref=outs[w].at[c], send_sem=send.at[w], recv_sem=recv.at[w],
                device_id=(x, y, 1 - c), device_id_type=MESH)
            cp.start()
            cps.append(cp)
        for w in range(n):
            arrival = pltpu.make_async_remote_copy(
                src_ref=ins[w], dst_ref=outs[w].at[1 - c], send_sem=send.at[w], recv_sem=recv.at[w],
                device_id=(x, y, 1 - c), device_id_type=MESH)
            arrival.wait_recv()
            arrival.wait_send()
            lcs[w].wait()

    return pl.pallas_call(
        body, name="grad_half_swap",
        in_specs=[HBM_SPEC] * n, out_specs=[HBM_SPEC] * n,
        out_shape=[jax.ShapeDtypeStruct((2,) + h.shape, h.dtype) for h in halves],
        scratch_shapes=[pltpu.SemaphoreType.DMA((n,)), pltpu.SemaphoreType.DMA((n,)),
                        pltpu.SemaphoreType.DMA((n,))],
        compiler_params=pltpu.CompilerParams(has_side_effects=True),
    )(*halves)


def _allreduce_small(pack):
    rows, d = pack.shape

    def body(p_ref, o_ref, slots, send, recv):
        x, y, c, _ = _place()
        me = 4 * x + 2 * y + c
        slots[me] = p_ref[...]
        cps = []
        for k in range(1, N_DEV):
            px, py, pc = x ^ (k >> 2), y ^ ((k >> 1) & 1), c ^ (k & 1)
            cp = pltpu.make_async_remote_copy(
                src_ref=p_ref, dst_ref=slots.at[me], send_sem=send.at[k - 1], recv_sem=recv.at[k - 1],
                device_id=(px, py, pc), device_id_type=MESH)
            cp.start()
            cps.append(cp)
        for k in range(1, N_DEV):
            px, py, pc = x ^ (k >> 2), y ^ ((k >> 1) & 1), c ^ (k & 1)
            arrival = pltpu.make_async_remote_copy(
                src_ref=p_ref, dst_ref=slots.at[4 * px + 2 * py + pc], send_sem=send.at[k - 1],
                recv_sem=recv.at[k - 1], device_id=(px, py, pc), device_id_type=MESH)
            arrival.wait_recv()
            arrival.wait_send()
        acc = slots[0]
        for k in range(1, N_DEV):
            acc = acc + slots[k]
        o_ref[...] = acc

    vm = pl.BlockSpec(memory_space=pltpu.VMEM)
    return pl.pallas_call(
        body, name="allreduce_small", in_specs=[vm], out_specs=vm,
        out_shape=jax.ShapeDtypeStruct((rows, d), F32),
        scratch_shapes=[pltpu.VMEM((N_DEV, rows, d), F32), pltpu.SemaphoreType.DMA((N_DEV - 1,)),
                        pltpu.SemaphoreType.DMA((N_DEV - 1,))],
        compiler_params=pltpu.CompilerParams(has_side_effects=True),
    )(pack)


def _adamw(w, g, m, v, name):
    rows, cols = w.shape

    def fn(wv, gv, mv, vv):
        m2 = ADAM_B1 * mv + (1.0 - ADAM_B1) * gv
        v2 = ADAM_B2 * vv + (1.0 - ADAM_B2) * (gv * gv)
        m_hat = m2 / (1.0 - ADAM_B1 ** ADAM_STEP)
        v_hat = v2 / (1.0 - ADAM_B2 ** ADAM_STEP)
        delta = -ADAM_LR * (m_hat / (jnp.sqrt(v_hat) + ADAM_EPS) + ADAM_WD * wv)
        return delta, m2, v2

    ins = [(a, cols, 0) for a in (w, g, m, v)]
    return _rowwise(fn, ins, [(cols, F32)] * 3, rows=rows, tm=_tile(rows, max(8, 262144 // cols)), name=name)


BIG = ["w_in", "w_branch_a", "w_branch_b", "w_mix_out", "w_mem_q", "w_mem_kv", "w_mem_o", "w_ffn_in", "w_ffn_out"]
BIG_AXIS = {"w_in": 1, "w_branch_a": 1, "w_branch_b": 1, "w_mix_out": 0, "w_mem_q": 0, "w_mem_kv": 1,
            "w_mem_o": 0, "w_ffn_in": 1, "w_ffn_out": 0}
NORMS = ["norm_mix", "norm_mem_q", "norm_mem_kv", "norm_ffn", "norm_final"]
ORDER = ["norm_mix", "w_in", "conv_w", "w_branch_a", "w_branch_b", "w_mix_out", "norm_mem_q", "norm_mem_kv",
         "w_mem_q", "w_mem_kv", "w_mem_o", "norm_ffn", "w_ffn_in", "w_ffn_out", "norm_final"]


def _pack_small(vals, conv):
    d = vals[0].shape[-1]
    rows = [v.reshape(1, d) for v in vals]
    conv = jnp.pad(conv, ((0, 0), (0, d - conv.shape[1])))
    pad = jnp.zeros((SMALL_ROWS - len(rows) - CONV_K, d), F32)
    return jnp.concatenate(rows + [conv, pad], axis=0)


def kernel(x, mem, norm_mix, w_in, conv_w, w_branch_a, w_branch_b, w_mix_out, norm_mem_q, norm_mem_kv, w_mem_q, w_mem_kv, w_mem_o, norm_ffn, w_ffn_in, w_ffn_out, norm_final, loss_target, m_norm_mix, m_w_in, m_conv_w, m_w_branch_a, m_w_branch_b, m_w_mix_out, m_norm_mem_q, m_norm_mem_kv, m_w_mem_q, m_w_mem_kv, m_w_mem_o, m_norm_ffn, m_w_ffn_in, m_w_ffn_out, m_norm_final, v_norm_mix, v_w_in, v_conv_w, v_w_branch_a, v_w_branch_b, v_w_mix_out, v_norm_mem_q, v_norm_mem_kv, v_w_mem_q, v_w_mem_kv, v_w_mem_o, v_norm_ffn, v_w_ffn_in, v_w_ffn_out, v_norm_final):
    args = dict(locals())
    wts = {n: args[n] for n in ORDER}
    mom = {n: args["m_" + n] for n in ORDER}
    var = {n: args["v_" + n] for n in ORDER}
    x = x[0]
    mem = mem[0]
    target = loss_target[0]
    s, d = x.shape
    gains = {n: wts[n].reshape(1, d) for n in NORMS}
    chip = 2 * lax.axis_index("x") + lax.axis_index("y")
    core = lax.axis_index("c").astype(jnp.int32).reshape(1)

    shards = [wts[n][0].astype(BF) for n in BIG]
    axes = [BIG_AXIS[n] for n in BIG]
    conv_shard = jnp.pad(conv_w[0], ((0, CONV_ROWS - CONV_K), (0, 0)))
    full = _gather_weights(shards + [conv_shard], axes + [1])
    W = dict(zip(BIG, full[:-1]))
    conv_full = full[-1]

    h1 = _rmsnorm(x, gains["norm_mix"], "norm_mix_fwd")
    proj = _matmul(h1, W["w_in"], name="in_proj")
    o_a, o_a32 = _sb_fwd(proj)
    y_b = _conv_fwd(proj, conv_full)
    br_a = _matmul(o_a, W["w_branch_a"], name="branch_a")
    br_b = _matmul(y_b, W["w_branch_b"], name="branch_b")
    ga_blk, gb_blk = 3, 4

    def merge(ga, gb, a, b):
        return jax.nn.sigmoid(ga.astype(F32)) * a.astype(F32) + jax.nn.sigmoid(gb.astype(F32)) * b.astype(F32)

    merged = _rowwise(merge, [(proj, d, ga_blk), (proj, d, gb_blk), (br_a, d, 0), (br_b, d, 0)], [(d, BF)],
                      rows=s, tm=512, name="merge_fwd")[0]
    x1 = _matmul(merged, W["w_mix_out"], out_dtype=F32, resid=x, name="mix_out")

    hq = _rmsnorm(x1, gains["norm_mem_q"], "norm_mem_q_fwd")
    mn = _rmsnorm(mem, gains["norm_mem_kv"], "norm_mem_kv_fwd")
    q_m = _matmul(hq, W["w_mem_q"], name="mem_q")
    kv = _matmul(mn, W["w_mem_kv"], name="mem_kv")
    o_m = _mem_fwd(q_m, kv)
    x2 = _matmul(o_m, W["w_mem_o"], out_dtype=F32, resid=x1, name="mem_o")

    hf = _rmsnorm(x2, gains["norm_ffn"], "norm_ffn_fwd")
    gu = _matmul(hf, W["w_ffn_in"], name="ffn_in")
    f = gu.shape[1] // 2

    def swiglu(gate, up):
        gate = gate.astype(F32)
        return gate * jax.nn.sigmoid(gate) * up.astype(F32)

    act = _rowwise(swiglu, [(gu, f, 0), (gu, f, 1)], [(f, BF)], rows=s, tm=512, name="swiglu_fwd")[0]
    x3 = _matmul(act, W["w_ffn_out"], out_dtype=F32, resid=x2, name="ffn_out")

    dx3, dg_final, loss_part = _loss_head(x3, gains["norm_final"], target, "loss_head")

    dact = _matmul(dx3, W["w_ffn_out"], tb=True, name="d_act")
    gw = {"w_ffn_out": _matmul(act, dx3, ta=True, tm=1408, tn=1024, tk=512, out_dtype=F32, name="gw_ffn_out")}

    def swiglu_bwd(gate, up, da):
        gate, up, da = gate.astype(F32), up.astype(F32), da.astype(F32)
        sg = jax.nn.sigmoid(gate)
        dgate = da * up * (sg * (1.0 + gate * (1.0 - sg)))
        return jnp.concatenate([dgate, da * (gate * sg)], axis=-1)

    dgu = _rowwise(swiglu_bwd, [(gu, f, 0), (gu, f, 1), (dact, f, 0)], [(2 * f, BF)], rows=s, tm=256,
                   name="swiglu_bwd")[0]
    dhf = _matmul(dgu, W["w_ffn_in"], tb=True, tk=1408, out_dtype=F32, name="d_hf")
    gw["w_ffn_in"] = _matmul(hf, dgu, ta=True, tm=1024, tn=1408, tk=512, out_dtype=F32, name="gw_ffn_in")
    dx2, dg_ffn = _rmsnorm_bwd(x2, gains["norm_ffn"], dhf, dx3, "norm_ffn_bwd")

    do_m = _matmul(dx2, W["w_mem_o"], tb=True, name="d_o_m")
    gw["w_mem_o"] = _matmul(o_m, dx2, ta=True, tk=512, tn=1024, out_dtype=F32, name="gw_mem_o")
    dq_m, dk_m, dv_m = _mem_bwd(q_m, kv, do_m)
    dkv = jnp.concatenate([dk_m, dv_m], axis=-1)
    dhq = _matmul(dq_m, W["w_mem_q"], tb=True, out_dtype=F32, name="d_hq")
    gw["w_mem_q"] = _matmul(hq, dq_m, ta=True, tk=512, tn=1024, out_dtype=F32, name="gw_mem_q")
    dmn = _matmul(dkv, W["w_mem_kv"], tb=True, out_dtype=F32, name="d_mn")
    gw["w_mem_kv"] = _matmul(mn, dkv, ta=True, tn=1024, out_dtype=F32, name="gw_mem_kv")
    _, dg_kv = _rmsnorm_bwd(mem, gains["norm_mem_kv"], dmn, None, "norm_mem_kv_bwd")
    dx1, dg_q = _rmsnorm_bwd(x1, gains["norm_mem_q"], dhq, dx2, "norm_mem_q_bwd")

    dmerged = _matmul(dx1, W["w_mix_out"], tb=True, name="d_merged")
    gw["w_mix_out"] = _matmul(merged, dx1, ta=True, tk=512, tn=1024, out_dtype=F32, name="gw_mix_out")

    def merge_bwd(ga, gb, a, b, dm):
        dm = dm.astype(F32)
        sa, sb = jax.nn.sigmoid(ga.astype(F32)), jax.nn.sigmoid(gb.astype(F32))
        a, b = a.astype(F32), b.astype(F32)
        return dm * sa, dm * sb, dm * a * (sa * (1.0 - sa)), dm * b * (sb * (1.0 - sb))

    dbr_a, dbr_b, dga, dgb = _rowwise(
        merge_bwd, [(proj, d, ga_blk), (proj, d, gb_blk), (br_a, d, 0), (br_b, d, 0), (dmerged, d, 0)],
        [(d, BF)] * 4, rows=s, tm=256, name="merge_bwd")
    do_a = _matmul(dbr_a, W["w_branch_a"], tb=True, name="d_o_a")
    gw["w_branch_a"] = _matmul(o_a, dbr_a, ta=True, tk=512, tn=1024, out_dtype=F32, name="gw_branch_a")
    dy_b = _matmul(dbr_b, W["w_branch_b"], tb=True, name="d_y_b")
    gw["w_branch_b"] = _matmul(y_b, dbr_b, ta=True, tk=512, tn=1024, out_dtype=F32, name="gw_branch_b")
    du, dgate_b, dgate_c, dconv = _conv_bwd(proj, conv_full, dy_b)
    dq, dk, dv = _sb_bwd(proj, o_a32, do_a)

    def assemble(*parts):
        return jnp.concatenate([p.astype(BF) for p in parts], axis=-1)

    hw = dq.shape[1]
    dproj = _rowwise(assemble, [(t, hw, 0) for t in (dq, dk, dv, du, dgate_b, dgate_c)] + [(dga, d, 0), (dgb, d, 0)],
                     [(proj.shape[1], BF)], rows=s, tm=256, name="assemble_dproj")[0]
    dh1 = _matmul(dproj, W["w_in"], tb=True, tk=1280, out_dtype=F32, name="d_h1")
    gw["w_in"] = _matmul(h1, dproj, ta=True, tk=512, tn=1280, out_dtype=F32, name="gw_in")
    grad_x, dg_mix = _rmsnorm_bwd(x, gains["norm_mix"], dh1, dx1, "norm_mix_bwd")

    views = []
    for n in BIG:
        r, cdim = gw[n].shape
        views.append(gw[n].reshape(1, 2, r // 2, cdim) if BIG_AXIS[n] == 1
                     else gw[n].reshape(N_CHIPS, 2, r // (2 * N_CHIPS), cdim))
    got = _pair_exchange(views)
    sums = [_pair_add(v, g, core, "pair_add_" + n) for n, v, g in zip(BIG, views, got)]
    slots = _chip_exchange(sums, axes)
    halves = [_chip_sum(sl, "chip_sum_" + n) for n, sl in zip(BIG, slots)]
    both = _half_swap(halves)
    grads = {n: b.reshape(wts[n].shape[1:]) for n, b in zip(BIG, both)}

    small_g = [dg_mix, dg_q, dg_kv, dg_ffn, dg_final]
    pack = _pack_small(small_g, dconv[:CONV_K])
    pack = pack.at[ROW_LOSS].set(jnp.broadcast_to(loss_part[0, :1], (d,)))
    red = _allreduce_small(pack)
    loss = red[ROW_LOSS, 0]
    cw = conv_w.shape[2]
    conv_g = lax.dynamic_slice(red, (ROW_CONV, chip * cw), (CONV_K, cw))
    small_grad = _pack_small([red[i] for i in range(len(NORMS))], conv_g)
    small = [_pack_small([t[n] for n in NORMS], t["conv_w"][0]) for t in (wts, mom, var)]
    s_delta, s_m, s_v = _adamw(small[0], small_grad, small[1], small[2], "adamw_small")

    out = {"grad": {}, "delta": {}, "new_m": {}, "new_v": {}}
    for n in BIG:
        shp = wts[n].shape
        dl, m2, v2 = _adamw(wts[n][0], grads[n], mom[n][0], var[n][0], "adamw_" + n)
        out["grad"][n] = grads[n].reshape(shp)
        out["delta"][n], out["new_m"][n], out["new_v"][n] = dl.reshape(shp), m2.reshape(shp), v2.reshape(shp)
    for key, blk in (("grad", small_grad), ("delta", s_delta), ("new_m", s_m), ("new_v", s_v)):
        for i, n in enumerate(NORMS):
            out[key][n] = blk[i].reshape(wts[n].shape)
        out[key]["conv_w"] = blk[ROW_CONV:ROW_CONV + CONV_K, :cw].reshape(conv_w.shape)

    return (loss, grad_x[None], *[out["grad"][n] for n in ORDER], *[out["delta"][n] for n in ORDER],
            *[out["new_m"][n] for n in ORDER], *[out["new_v"][n] for n in ORDER])
```

```python
import functools
import math

import jax
import jax.numpy as jnp
from jax import lax
from jax.experimental import pallas as pl
from jax.experimental.pallas import tpu as pltpu

BF = jnp.bfloat16
F32 = jnp.float32
MESH = pl.DeviceIdType.MESH

SB_HEAD_DIM = 64
LANES = 128
MEM_HEADS = 4
CONV_K = 3
CONV_ROWS = 8
EPS = 1e-6
N_CHIPS = 4
N_DEV = 8
VMEM_LIMIT = 56 * 1024 * 1024

ADAM_LR = 0.001
ADAM_B1 = 0.9
ADAM_B2 = 0.999
ADAM_EPS = 1e-08
ADAM_WD = 0.01
ADAM_STEP = 10

SMALL_ROWS = 16
ROW_CONV = 5
ROW_LOSS = 8


def _params(sem=None):
    return pltpu.CompilerParams(dimension_semantics=sem, vmem_limit_bytes=VMEM_LIMIT)


def _tile(dim, pref):
    if dim <= pref:
        return dim
    for step in (LANES, 8):
        t = (pref // step) * step
        while t >= step:
            if dim % t == 0:
                return t
            t -= step
    raise ValueError(f"no tile of {dim} under {pref}")


def _matmul(a, b, *, ta=False, tb=False, tm=1024, tn=512, tk=None, out_dtype=BF, resid=None, name):
    if ta:
        kdim, m = a.shape
    else:
        m, kdim = a.shape
    n = b.shape[0] if tb else b.shape[1]
    tm, tn = _tile(m, tm), _tile(n, tn)
    tk = _tile(kdim, tk or kdim)
    nk = kdim // tk
    a_spec = pl.BlockSpec((tk, tm), lambda i, j, k: (k, i)) if ta else pl.BlockSpec((tm, tk), lambda i, j, k: (i, k))
    b_spec = pl.BlockSpec((tn, tk), lambda i, j, k: (j, k)) if tb else pl.BlockSpec((tk, tn), lambda i, j, k: (k, j))
    o_spec = pl.BlockSpec((tm, tn), lambda i, j, k: (i, j))
    dims = (((0 if ta else 1,), (1 if tb else 0,)), ((), ()))
    has_res = resid is not None

    def body(*refs):
        a_ref, b_ref = refs[0], refs[1]
        o_ref = refs[2 + has_res]
        av, bv = a_ref[...], b_ref[...]
        if av.dtype != BF:
            av = av.astype(BF)
        if bv.dtype != BF:
            bv = bv.astype(BF)
        p = lax.dot_general(av, bv, dims, preferred_element_type=F32)

        def finish(acc):
            if has_res:
                acc = refs[2][...] + acc
            o_ref[...] = acc.astype(o_ref.dtype)

        if nk == 1:
            finish(p)
        else:
            acc_ref = refs[-1]
            k = pl.program_id(2)

            @pl.when(k == 0)
            def _():
                acc_ref[...] = p

            @pl.when(k > 0)
            def _():
                acc_ref[...] += p

            @pl.when(k == nk - 1)
            def _():
                finish(acc_ref[...])

    return pl.pallas_call(
        body, name=name, grid=(m // tm, n // tn, nk),
        in_specs=[a_spec, b_spec] + ([o_spec] if has_res else []),
        out_specs=o_spec, out_shape=jax.ShapeDtypeStruct((m, n), out_dtype),
        scratch_shapes=[pltpu.VMEM((tm, tn), F32)] if nk > 1 else [],
        compiler_params=_params(("parallel", "parallel", "arbitrary")),
    )(*([a, b] + ([resid] if has_res else [])))


def _rowwise(fn, ins, outs, *, rows, tm, name, accs=()):
    tm = _tile(rows, tm)
    in_specs, args = [], []
    for arr, cols, cb in ins:
        if cols is None:
            in_specs.append(pl.BlockSpec(arr.shape, lambda i, nd=arr.ndim: (0,) * nd))
        else:
            in_specs.append(pl.BlockSpec((tm, cols), lambda i, cb=cb: (i, cb)))
        args.append(arr)
    out_specs = [pl.BlockSpec((tm, cols), lambda i: (i, 0)) for cols, _ in outs]
    out_shape = [jax.ShapeDtypeStruct((rows, cols), dt) for cols, dt in outs]
    for r, c in accs:
        out_specs.append(pl.BlockSpec((r, c), lambda i: (0, 0)))
        out_shape.append(jax.ShapeDtypeStruct((r, c), F32))
    n_in, n_out = len(ins), len(outs)

    def body(*refs):
        res = fn(*[r[...] for r in refs[:n_in]])
        if not isinstance(res, (tuple, list)):
            res = (res,)
        for o_ref, val in zip(refs[n_in:n_in + n_out], res[:n_out]):
            o_ref[...] = val.astype(o_ref.dtype)
        first = pl.program_id(0) == 0
        for a_ref, val in zip(refs[n_in + n_out:], res[n_out:]):
            @pl.when(first)
            def _(a_ref=a_ref, val=val):
                a_ref[...] = val

            @pl.when(jnp.logical_not(first))
            def _(a_ref=a_ref, val=val):
                a_ref[...] += val

    res = pl.pallas_call(
        body, name=name, grid=(rows // tm,), in_specs=in_specs, out_specs=out_specs, out_shape=out_shape,
        compiler_params=_params(("arbitrary",) if accs else ("parallel",)),
    )(*args)
    return res


def _rstd(xf):
    return lax.rsqrt(jnp.mean(xf * xf, axis=-1, keepdims=True) + EPS)


def _rmsnorm(x, g, name):
    rows, d = x.shape
    return _rowwise(lambda xv, gv: xv * _rstd(xv) * gv, [(x, d, 0), (g, None, None)], [(d, BF)],
                    rows=rows, tm=512, name=name)[0]


def _rmsnorm_bwd(x, g, dy, resid, name):
    rows, d = x.shape

    def fn(xv, gv, dyv, *rest):
        dyv = dyv.astype(F32)
        r = _rstd(xv)
        xn = xv * r
        dxn = dyv * gv
        dx = r * (dxn - xn * jnp.mean(dxn * xn, axis=-1, keepdims=True))
        if rest:
            dx = rest[0] + dx
        return dx, jnp.sum(dyv * xn, axis=0, keepdims=True)

    ins = [(x, d, 0), (g, None, None), (dy, d, 0)] + ([(resid, d, 0)] if resid is not None else [])
    outs = [(d, F32)]
    return _rowwise(fn, ins, outs, rows=rows, tm=512, name=name, accs=[(1, d)])


def _loss_head(x, g, target, name):
    rows, d = x.shape

    def fn(xv, gv, tv):
        r = _rstd(xv)
        xn = xv * r
        err = xn * gv - tv
        per_tok = jnp.mean(err * err, axis=-1, keepdims=True)
        loss = 0.5 * jnp.sum(per_tok, axis=0, keepdims=True)
        dyv = err * (1.0 / d)
        dxn = dyv * gv
        dx = r * (dxn - xn * jnp.mean(dxn * xn, axis=-1, keepdims=True))
        return dx, jnp.sum(dyv * xn, axis=0, keepdims=True), jnp.broadcast_to(loss, (1, LANES))

    return _rowwise(fn, [(x, d, 0), (g, None, None), (target, d, 0)], [(d, F32)],
                    rows=rows, tm=512, name=name, accs=[(1, d), (1, LANES)])


SB_TK = 128
SB_KT = 2


def _sb_consts(tq):
    tk = SB_TK
    diff = lax.broadcasted_iota(jnp.int32, (tq, tk), 1) - lax.broadcasted_iota(jnp.int32, (tq, tk), 0)
    rj = lax.broadcasted_iota(jnp.int32, (2 * tk, 2 * tk), 0) & (tk - 1)
    cj = lax.broadcasted_iota(jnp.int32, (2 * tk, 2 * tk), 1)
    ones_half = cj >= tk
    later = jnp.where((rj > cj) | ones_half, 1.0, 0.0).astype(BF)
    later_incl = jnp.where((rj >= cj) | ones_half, 1.0, 0.0).astype(BF)
    return diff, later, later_incl


def _split_dot(val, rhs_twice):
    hi = val.astype(BF)
    lo = (val - hi.astype(F32)).astype(BF)
    return jnp.dot(jnp.concatenate([hi, lo], axis=1), rhs_twice, preferred_element_type=F32)


def _log_terms(z):
    sp = jnp.maximum(z, 0.0) + jnp.log(1.0 + jnp.exp(-jnp.abs(z)))
    return z - sp, sp


NT = (((1,), (1,)), ((), ()))
TN = (((0,), (0,)), ((), ()))


def _head_lane_masks(rows):
    lane = lax.broadcasted_iota(jnp.int32, (rows, LANES), 1)
    first = jnp.where(lane < SB_HEAD_DIM, 1.0, 0.0)
    return first.astype(BF), (1.0 - first).astype(BF)


def _both_heads(tile, masks):
    return jnp.concatenate([tile * masks[0], tile * masks[1]], axis=0)


def _sb_fwd(proj):
    s = proj.shape[0]
    tk, tq = SB_TK, SB_KT * SB_TK
    n_pairs = 4
    scale = 1.0 / math.sqrt(SB_HEAD_DIM)

    def body(q_ref, k_ref, v_ref, o_ref, o32_ref):
        i = pl.program_id(1)
        diff, later, _ = _sb_consts(tq)
        qs = (q_ref[...].astype(F32) * scale).astype(BF)
        lane_masks = _head_lane_masks(tk)

        def step(g, state, masked):
            tiles = list(reversed(range(SB_KT)))
            chains = [(t, h) for t in tiles for h in range(2)]
            rows = {t: pl.ds(pl.multiple_of((g * SB_KT + t) * tk, tk), tk) for t in tiles}
            ks = {t: _both_heads(k_ref[rows[t], :], lane_masks) for t in tiles}
            vs = {t: _both_heads(v_ref[rows[t], :], lane_masks) for t in tiles}
            allowed = {t: diff < -t * tk for t in tiles}
            zs = {t: lax.dot_general(qs, ks[t], NT, preferred_element_type=F32) for t in tiles}
            logs = {}
            for t, h in chains:
                log_b, sp = _log_terms(zs[t][:, h * tk:(h + 1) * tk])
                logs[t, h] = (log_b, jnp.where(allowed[t], sp, 0.0) if masked else sp)
            sums = {c: _split_dot(logs[c][1], later) for c in chains}
            carries = list(state[0])
            ws = {}
            for t, h in chains:
                w = jnp.exp(logs[t, h][0] - (sums[t, h][:, :tk] + carries[h]))
                ws[t, h] = (jnp.where(allowed[t], w, 0.0) if masked else w).astype(BF)
                carries[h] = carries[h] + sums[t, h][:, tk:]
            acc = state[1]
            for t in tiles:
                acc = acc + jnp.dot(jnp.concatenate([ws[t, 0], ws[t, 1]], axis=1), vs[t],
                                    preferred_element_type=F32)
            return tuple(carries), acc

        zero = jnp.zeros((tq, LANES), F32)
        state = step(i, ((zero, zero), zero), True)
        state = lax.fori_loop(0, i, lambda gg, st: step(i - 1 - gg, st, False), state)
        o_ref[...] = state[1].astype(o_ref.dtype)
        o32_ref[...] = state[1]

    tok = pl.BlockSpec((tq, LANES), lambda p, i: (i, p))
    return pl.pallas_call(
        body, name="sb_attn_fwd", grid=(n_pairs, s // tq),
        in_specs=[tok,
                  pl.BlockSpec((s, LANES), lambda p, i: (0, n_pairs + p)),
                  pl.BlockSpec((s, LANES), lambda p, i: (0, 2 * n_pairs + p))],
        out_specs=[tok, tok],
        out_shape=[jax.ShapeDtypeStruct((s, n_pairs * LANES), BF), jax.ShapeDtypeStruct((s, n_pairs * LANES), F32)],
        compiler_params=_params(("parallel", "arbitrary")),
    )(proj, proj, proj)


def _sb_bwd(proj, o32, do_a):
    s = proj.shape[0]
    tk, tq = SB_TK, SB_KT * SB_TK
    n_pairs = 4
    scale = 1.0 / math.sqrt(SB_HEAD_DIM)

    def body(q_ref, k_ref, v_ref, o_ref, do_ref, dq_ref, dk_ref, dv_ref):
        i = pl.program_id(1)

        @pl.when(i == 0)
        def _():
            dk_ref[...] = jnp.zeros_like(dk_ref)
            dv_ref[...] = jnp.zeros_like(dv_ref)

        diff, later, later_incl = _sb_consts(tq)
        qs = (q_ref[...].astype(F32) * scale).astype(BF)
        do2 = do_ref[...]
        prod = do2.astype(F32) * o_ref[...]
        lane_masks = _head_lane_masks(tk)
        first_head = lax.broadcasted_iota(jnp.int32, (tq, LANES), 1) < SB_HEAD_DIM
        totals = [jnp.broadcast_to(jnp.sum(jnp.where(keep, prod, 0.0), axis=-1, keepdims=True), (tq, tk))
                  for keep in (first_head, jnp.logical_not(first_head))]
        first_head_k = first_head[:tk]

        def step(g_idx, state, masked):
            tiles = list(reversed(range(SB_KT)))
            chains = [(t, h) for t in tiles for h in range(2)]
            rows = {t: pl.ds(pl.multiple_of((g_idx * SB_KT + t) * tk, tk), tk) for t in tiles}
            ks = {t: _both_heads(k_ref[rows[t], :], lane_masks) for t in tiles}
            vs = {t: _both_heads(v_ref[rows[t], :], lane_masks) for t in tiles}
            allowed = {t: diff < -t * tk for t in tiles}
            zs = {t: lax.dot_general(qs, ks[t], NT, preferred_element_type=F32) for t in tiles}
            dws = {t: lax.dot_general(do2, vs[t], NT, preferred_element_type=F32) for t in tiles}
            logs = {}
            for t, h in chains:
                log_b, sp = _log_terms(zs[t][:, h * tk:(h + 1) * tk])
                logs[t, h] = (log_b, jnp.where(allowed[t], sp, 0.0) if masked else sp)
            sums = {c: _split_dot(logs[c][1], later) for c in chains}
            c_log, c_g = list(state[0]), list(state[1])
            ws, gs = {}, {}
            for t, h in chains:
                w = jnp.exp(logs[t, h][0] - (sums[t, h][:, :tk] + c_log[h]))
                ws[t, h] = (jnp.where(allowed[t], w, 0.0) if masked else w).astype(BF)
                c_log[h] = c_log[h] + sums[t, h][:, tk:]
                gs[t, h] = ws[t, h].astype(F32) * dws[t][:, h * tk:(h + 1) * tk]
            gsums = {c: _split_dot(gs[c], later_incl) for c in chains}
            dzs = {}
            for t, h in chains:
                beta = jnp.exp(logs[t, h][0])
                earlier = totals[h] - (gsums[t, h][:, :tk] + c_g[h])
                dz = gs[t, h] * (1.0 - beta) - earlier * beta
                dzs[t, h] = (jnp.where(allowed[t], dz, 0.0) if masked else dz).astype(BF)
                c_g[h] = c_g[h] + gsums[t, h][:, tk:]
            dq = state[2]
            for t in tiles:
                dz_both = jnp.concatenate([dzs[t, 0], dzs[t, 1]], axis=1)
                w_both = jnp.concatenate([ws[t, 0], ws[t, 1]], axis=1)
                dq = dq + jnp.dot(dz_both, ks[t], preferred_element_type=F32)
                dk2 = lax.dot_general(dz_both, qs, TN, preferred_element_type=F32)
                dv2 = lax.dot_general(w_both, do2, TN, preferred_element_type=F32)
                dk_ref[rows[t], :] += jnp.where(first_head_k, dk2[:tk], dk2[tk:])
                dv_ref[rows[t], :] += jnp.where(first_head_k, dv2[:tk], dv2[tk:])
            return tuple(c_log), tuple(c_g), dq

        zero = jnp.zeros((tq, LANES), F32)
        state = step(i, ((zero, zero), (zero, zero), zero), True)
        state = lax.fori_loop(0, i, lambda gg, st: step(i - 1 - gg, st, False), state)
        dq_ref[...] = (state[2] * scale).astype(dq_ref.dtype)

    width = n_pairs * LANES
    return pl.pallas_call(
        body, name="sb_attn_bwd", grid=(n_pairs, s // tq),
        in_specs=[pl.BlockSpec((tq, LANES), lambda p, i: (i, p)),
                  pl.BlockSpec((s, LANES), lambda p, i: (0, n_pairs + p)),
                  pl.BlockSpec((s, LANES), lambda p, i: (0, 2 * n_pairs + p)),
                  pl.BlockSpec((tq, LANES), lambda p, i: (i, p)),
                  pl.BlockSpec((tq, LANES), lambda p, i: (i, p))],
        out_specs=[pl.BlockSpec((tq, LANES), lambda p, i: (i, p)),
                   pl.BlockSpec((s, LANES), lambda p, i: (0, p)),
                   pl.BlockSpec((s, LANES), lambda p, i: (0, p))],
        out_shape=[jax.ShapeDtypeStruct((s, width), BF), jax.ShapeDtypeStruct((s, width), F32),
                   jax.ShapeDtypeStruct((s, width), F32)],
        compiler_params=_params(("parallel", "arbitrary")),
    )(proj, proj, proj, o32, do_a)


CONV_COL0 = 12


def _shift_rows(v, k):
    n = v.shape[0]
    row = lax.broadcasted_iota(jnp.int32, v.shape, 0)
    rolled = pltpu.roll(v, k % n, axis=0)
    keep = row >= k if k > 0 else row < n + k
    return jnp.where(keep, rolled, 0.0)


def _conv_specs(s):
    return [pl.BlockSpec((s, LANES), lambda cb: (0, CONV_COL0 + cb)),
            pl.BlockSpec((s, LANES), lambda cb: (0, CONV_COL0 + 4 + cb)),
            pl.BlockSpec((s, LANES), lambda cb: (0, CONV_COL0 + 8 + cb)),
            pl.BlockSpec((CONV_ROWS, LANES), lambda cb: (0, cb))]


def _conv_fwd(proj, conv_w):
    s = proj.shape[0]

    def body(u_ref, gb_ref, gc_ref, w_ref, y_ref):
        cu = gc_ref[...].astype(F32) * u_ref[...].astype(F32)
        w = w_ref[...]
        y = w[0:1] * _shift_rows(cu, 2) + w[1:2] * _shift_rows(cu, 1) + w[2:3] * cu
        y_ref[...] = (gb_ref[...].astype(F32) * y).astype(y_ref.dtype)

    return pl.pallas_call(
        body, name="conv_fwd", grid=(4,), in_specs=_conv_specs(s),
        out_specs=pl.BlockSpec((s, LANES), lambda cb: (0, cb)),
        out_shape=jax.ShapeDtypeStruct((s, 4 * LANES), BF),
        compiler_params=_params(("parallel",)),
    )(proj, proj, proj, conv_w)


def _conv_bwd(proj, conv_w, dy):
    s = proj.shape[0]

    def body(u_ref, gb_ref, gc_ref, w_ref, dy_ref, du_ref, dgb_ref, dgc_ref, dw_ref):
        u, gc = u_ref[...].astype(F32), gc_ref[...].astype(F32)
        dyv = dy_ref[...].astype(F32)
        w = w_ref[...]
        cu = gc * u
        cu1, cu2 = _shift_rows(cu, 1), _shift_rows(cu, 2)
        conv = w[0:1] * cu2 + w[1:2] * cu1 + w[2:3] * cu
        dgb_ref[...] = (dyv * conv).astype(dgb_ref.dtype)
        dc = dyv * gb_ref[...].astype(F32)
        dcu = w[2:3] * dc + w[1:2] * _shift_rows(dc, -1) + w[0:1] * _shift_rows(dc, -2)
        dgc_ref[...] = (dcu * u).astype(dgc_ref.dtype)
        du_ref[...] = (dcu * gc).astype(du_ref.dtype)
        tap_row = lax.broadcasted_iota(jnp.int32, (CONV_ROWS, LANES), 0)
        dw = jnp.zeros((CONV_ROWS, LANES), F32)
        for t, shifted in enumerate((cu2, cu1, cu)):
            dw = jnp.where(tap_row == t, jnp.sum(dc * shifted, axis=0, keepdims=True), dw)
        dw_ref[...] = dw

    col = pl.BlockSpec((s, LANES), lambda cb: (0, cb))
    act = jax.ShapeDtypeStruct((s, 4 * LANES), BF)
    return pl.pallas_call(
        body, name="conv_bwd", grid=(4,), in_specs=_conv_specs(s) + [col],
        out_specs=[col, col, col, pl.BlockSpec((CONV_ROWS, LANES), lambda cb: (0, cb))],
        out_shape=[act, act, act, jax.ShapeDtypeStruct((CONV_ROWS, 4 * LANES), F32)],
        compiler_params=_params(("parallel",)),
    )(proj, proj, proj, conv_w, dy)


def _mem_probs(q, k, scale):
    sc = lax.dot_general(q, k, NT, preferred_element_type=F32) * scale
    p = jnp.exp(sc - jnp.max(sc, axis=-1, keepdims=True))
    return p / jnp.sum(p, axis=-1, keepdims=True)


def _mem_fwd(q_m, kv, tq=512):
    s, d = q_m.shape
    mlen = kv.shape[0]
    hd = d // MEM_HEADS
    tq = _tile(s, tq)
    scale = 1.0 / math.sqrt(hd)

    def body(q_ref, k_ref, v_ref, o_ref):
        p = _mem_probs(q_ref[...], k_ref[...], scale)
        o_ref[...] = jnp.dot(p.astype(BF), v_ref[...], preferred_element_type=F32).astype(o_ref.dtype)

    return pl.pallas_call(
        body, name="mem_attn_fwd", grid=(MEM_HEADS, s // tq),
        in_specs=[pl.BlockSpec((tq, hd), lambda h, i: (i, h)),
                  pl.BlockSpec((mlen, hd), lambda h, i: (0, h)),
                  pl.BlockSpec((mlen, hd), lambda h, i: (0, MEM_HEADS + h))],
        out_specs=pl.BlockSpec((tq, hd), lambda h, i: (i, h)),
        out_shape=jax.ShapeDtypeStruct((s, d), BF),
        compiler_params=_params(("parallel", "parallel")),
    )(q_m, kv, kv)


def _mem_bwd(q_m, kv, do_m, tq=512):
    s, d = q_m.shape
    mlen = kv.shape[0]
    hd = d // MEM_HEADS
    tq = _tile(s, tq)
    scale = 1.0 / math.sqrt(hd)

    def body(q_ref, k_ref, v_ref, do_ref, dq_ref, dk_ref, dv_ref):
        q, k, v, do = q_ref[...], k_ref[...], v_ref[...], do_ref[...]
        p = _mem_probs(q, k, scale)
        dp = lax.dot_general(do, v, NT, preferred_element_type=F32)
        ds = p * (dp - jnp.sum(dp * p, axis=-1, keepdims=True)) * scale
        dsb = ds.astype(BF)
        dq_ref[...] = jnp.dot(dsb, k, preferred_element_type=F32).astype(dq_ref.dtype)
        dk = lax.dot_general(dsb, q, TN, preferred_element_type=F32)
        dv = lax.dot_general(p.astype(BF), do, TN, preferred_element_type=F32)
        first = pl.program_id(1) == 0

        @pl.when(first)
        def _():
            dk_ref[...] = dk
            dv_ref[...] = dv

        @pl.when(jnp.logical_not(first))
        def _():
            dk_ref[...] += dk
            dv_ref[...] += dv

    tok = pl.BlockSpec((tq, hd), lambda h, i: (i, h))
    memb = pl.BlockSpec((mlen, hd), lambda h, i: (0, h))
    return pl.pallas_call(
        body, name="mem_attn_bwd", grid=(MEM_HEADS, s // tq),
        in_specs=[tok, memb, pl.BlockSpec((mlen, hd), lambda h, i: (0, MEM_HEADS + h)), tok],
        out_specs=[tok, memb, memb],
        out_shape=[jax.ShapeDtypeStruct((s, d), BF), jax.ShapeDtypeStruct((mlen, d), F32),
                   jax.ShapeDtypeStruct((mlen, d), F32)],
        compiler_params=_params(("parallel", "arbitrary")),
    )(q_m, kv, kv, do_m)


def _place():
    x, y, c = lax.axis_index("x"), lax.axis_index("y"), lax.axis_index("c")
    other_chips = [(1 - x, y), (x, 1 - y), (1 - x, 1 - y)]
    return x, y, c, other_chips


def _chip_no(cx, cy):
    return 2 * cx + cy


HBM_SPEC = pl.BlockSpec(memory_space=pl.ANY)


def _shard_view(ref, axis, j, width):
    start = pl.multiple_of(j * width, width)
    return ref.at[pl.ds(start, width), :] if axis == 0 else ref.at[:, pl.ds(start, width)]


def _gather_weights(shards, axes):
    n = len(shards)
    full_shapes = [(sh.shape[0] * (N_CHIPS if ax == 0 else 1), sh.shape[1] * (N_CHIPS if ax == 1 else 1))
                   for sh, ax in zip(shards, axes)]

    def body(*refs):
        ins, outs = refs[:n], refs[n:2 * n]
        send, recv, local = refs[2 * n:]
        x, y, c, others = _place()
        me = _chip_no(x, y)
        copies = []
        for w in range(n):
            width = ins[w].shape[axes[w]]
            mine = _shard_view(outs[w], axes[w], me, width)
            cp = pltpu.make_async_copy(ins[w], mine, local.at[w])
            cp.start()
            copies.append(cp)
            for p, chip in enumerate(others):
                pltpu.make_async_remote_copy(
                    src_ref=ins[w], dst_ref=mine, send_sem=send.at[w, p], recv_sem=recv.at[w, p],
                    device_id=(chip[0], chip[1], c), device_id_type=MESH).start()
        for w in range(n):
            width = ins[w].shape[axes[w]]
            for p, chip in enumerate(others):
                theirs = _shard_view(outs[w], axes[w], _chip_no(*chip), width)
                arrival = pltpu.make_async_remote_copy(
                    src_ref=ins[w], dst_ref=theirs, send_sem=send.at[w, p], recv_sem=recv.at[w, p],
                    device_id=(chip[0], chip[1], c), device_id_type=MESH)
                arrival.wait_recv()
                arrival.wait_send()
            copies[w].wait()

    return pl.pallas_call(
        body, name="gather_weights",
        in_specs=[HBM_SPEC] * n, out_specs=[HBM_SPEC] * n,
        out_shape=[jax.ShapeDtypeStruct(fs, sh.dtype) for fs, sh in zip(full_shapes, shards)],
        scratch_shapes=[pltpu.SemaphoreType.DMA((n, 3)), pltpu.SemaphoreType.DMA((n, 3)),
                        pltpu.SemaphoreType.DMA((n,))],
        compiler_params=pltpu.CompilerParams(has_side_effects=True),
    )(*shards)


def _pair_exchange(grads):
    n = len(grads)

    def body(*refs):
        ins, outs = refs[:n], refs[n:2 * n]
        send, recv = refs[2 * n:]
        x, y, c, _ = _place()
        cps = []
        for w in range(n):
            cp = pltpu.make_async_remote_copy(
                src_ref=ins[w].at[:, 1 - c], dst_ref=outs[w], send_sem=send.at[w], recv_sem=recv.at[w],
                device_id=(x, y, 1 - c), device_id_type=MESH)
            cp.start()
            cps.append(cp)
        for cp in cps:
            cp.wait()

    return pl.pallas_call(
        body, name="grad_pair_exchange",
        in_specs=[HBM_SPEC] * n, out_specs=[HBM_SPEC] * n,
        out_shape=[jax.ShapeDtypeStruct((g.shape[0],) + g.shape[2:], g.dtype) for g in grads],
        scratch_shapes=[pltpu.SemaphoreType.DMA((n,)), pltpu.SemaphoreType.DMA((n,))],
        compiler_params=pltpu.CompilerParams(has_side_effects=True),
    )(*grads)


def _pair_add(g4, got, core, name):
    nj, _, hr, cdim = g4.shape
    tr = _tile(hr, max(8, 524288 // cdim))

    def body(core_ref, a_ref, b_ref, o_ref):
        o_ref[...] = a_ref[...] + b_ref[...]

    return pl.pallas_call(
        body, name=name,
        grid_spec=pltpu.PrefetchScalarGridSpec(
            num_scalar_prefetch=1, grid=(nj, hr // tr),
            in_specs=[pl.BlockSpec((1, None, tr, cdim), lambda j, i, core_ref: (j, core_ref[0], i, 0)),
                      pl.BlockSpec((1, tr, cdim), lambda j, i, core_ref: (j, i, 0))],
            out_specs=pl.BlockSpec((1, tr, cdim), lambda j, i, core_ref: (j, i, 0))),
        out_shape=jax.ShapeDtypeStruct((nj, hr, cdim), F32),
        compiler_params=_params(("parallel", "parallel")),
    )(core, g4, got)


def _piece(ref, axis, j, hc):
    if axis == 0:
        return ref.at[j]
    return ref.at[0, :, pl.ds(pl.multiple_of(j * hc, hc), hc)]


def _chip_exchange(sums, axes):
    n = len(sums)
    shapes = []
    for sm, ax in zip(sums, axes):
        shapes.append((N_CHIPS, sm.shape[1], sm.shape[2] // (1 if ax == 0 else N_CHIPS)))

    def body(*refs):
        ins, outs = refs[:n], refs[n:2 * n]
        send, recv, local = refs[2 * n:]
        x, y, c, others = _place()
        me = _chip_no(x, y)
        copies = []
        for w in range(n):
            hc = shapes[w][2]
            cp = pltpu.make_async_copy(_piece(ins[w], axes[w], me, hc), outs[w].at[me], local.at[w])
            cp.start()
            copies.append(cp)
            for p, chip in enumerate(others):
                pltpu.make_async_remote_copy(
                    src_ref=_piece(ins[w], axes[w], _chip_no(*chip), hc), dst_ref=outs[w].at[me],
                    send_sem=send.at[w, p], recv_sem=recv.at[w, p],
                    device_id=(chip[0], chip[1], c), device_id_type=MESH).start()
        for w in range(n):
            hc = shapes[w][2]
            for p, chip in enumerate(others):
                arrival = pltpu.make_async_remote_copy(
                    src_ref=_piece(ins[w], axes[w], _chip_no(*chip), hc), dst_ref=outs[w].at[_chip_no(*chip)],
                    send_sem=send.at[w, p], recv_sem=recv.at[w, p],
                    device_id=(chip[0], chip[1], c), device_id_type=MESH)
                arrival.wait_recv()
                arrival.wait_send()
            copies[w].wait()

    return pl.pallas_call(
        body, name="grad_chip_exchange",
        in_specs=[HBM_SPEC] * n, out_specs=[HBM_SPEC] * n,
        out_shape=[jax.ShapeDtypeStruct(sh, sm.dtype) for sh, sm in zip(shapes, sums)],
        scratch_shapes=[pltpu.SemaphoreType.DMA((n, 3)), pltpu.SemaphoreType.DMA((n, 3)),
                        pltpu.SemaphoreType.DMA((n,))],
        compiler_params=pltpu.CompilerParams(has_side_effects=True),
    )(*sums)


def _chip_sum(slots, name):
    _, hr, hc = slots.shape
    tr = _tile(hr, 256)

    def body(s_ref, o_ref):
        o_ref[...] = ((s_ref[0].astype(F32) + s_ref[1].astype(F32)) + s_ref[2].astype(F32)) + s_ref[3].astype(F32)

    return pl.pallas_call(
        body, name=name, grid=(hr // tr,),
        in_specs=[pl.BlockSpec((N_CHIPS, tr, hc), lambda i: (0, i, 0))],
        out_specs=pl.BlockSpec((tr, hc), lambda i: (i, 0)),
        out_shape=jax.ShapeDtypeStruct((hr, hc), F32),
        compiler_params=_params(("parallel",)),
    )(slots)


def _half_swap(halves):
    n = len(halves)

    def body(*refs):
        ins, outs = refs[:n], refs[n:2 * n]
        send, recv, local = refs[2 * n:]
        x, y, c, _ = _place()
        cps, lcs = [], []
        for w in range(n):
            lc = pltpu.make_async_copy(ins[w], outs[w].at[c], local.at[w])
            lc.start()
            lcs.append(lc)
            cp = pltpu.make_async_remote_copy(
                src_ref=ins[w], dst_ref=outs[w].at[c], send_sem=send.at[w], recv_sem=recv.at[w],
                device_id=(x, y, 1 - c), device_id_type=MESH)
            cp.start()
            cps.append(cp)
        for w in range(n):
            arrival = pltpu.make_async_remote_copy(
                src_ref=ins[w], dst_ref=outs[w].at[1 - c], send_sem=send.at[w], recv_sem=recv.at[w],
                device_id=(x, y, 1 - c), device_id_type=MESH)
            arrival.wait_recv()
            arrival.wait_send()
            lcs[w].wait()

    return pl.pallas_call(
        body, name="grad_half_swap",
        in_specs=[HBM_SPEC] * n, out_specs=[HBM_SPEC] * n,
        out_shape=[jax.ShapeDtypeStruct((2,) + h.shape, h.dtype) for h in halves],
        scratch_shapes=[pltpu.SemaphoreType.DMA((n,)), pltpu.SemaphoreType.DMA((n,)),
                        pltpu.SemaphoreType.DMA((n,))],
        compiler_params=pltpu.CompilerParams(has_side_effects=True),
    )(*halves)


def _allreduce_small(pack):
    rows, d = pack.shape

    def body(p_ref, o_ref, slots, send, recv):
        x, y, c, _ = _place()
        me = 4 * x + 2 * y + c
        slots[me] = p_ref[...]
        cps = []
        for k in range(1, N_DEV):
            px, py, pc = x ^ (k >> 2), y ^ ((k >> 1) & 1), c ^ (k & 1)
            cp = pltpu.make_async_remote_copy(
                src_ref=p_ref, dst_ref=slots.at[me], send_sem=send.at[k - 1], recv_sem=recv.at[k - 1],
                device_id=(px, py, pc), device_id_type=MESH)
            cp.start()
            cps.append(cp)
        for k in range(1, N_DEV):
            px, py, pc = x ^ (k >> 2), y ^ ((k >> 1) & 1), c ^ (k & 1)
            arrival = pltpu.make_async_remote_copy(
                src_ref=p_ref, dst_ref=slots.at[4 * px + 2 * py + pc], send_sem=send.at[k - 1],
                recv_sem=recv.at[k - 1], device_id=(px, py, pc), device_id_type=MESH)
            arrival.wait_recv()
            arrival.wait_send()
        acc = slots[0]
        for k in range(1, N_DEV):
            acc = acc + slots[k]
        o_ref[...] = acc

    vm = pl.BlockSpec(memory_space=pltpu.VMEM)
    return pl.pallas_call(
        body, name="allreduce_small", in_specs=[vm], out_specs=vm,
        out_shape=jax.ShapeDtypeStruct((rows, d), F32),
        scratch_shapes=[pltpu.VMEM((N_DEV, rows, d), F32), pltpu.SemaphoreType.DMA((N_DEV - 1,)),
                        pltpu.SemaphoreType.DMA((N_DEV - 1,))],
        compiler_params=pltpu.CompilerParams(has_side_effects=True),
    )(pack)


def _adamw(w, g, m, v, name):
    rows, cols = w.shape

    def fn(wv, gv, mv, vv):
        m2 = ADAM_B1 * mv + (1.0 - ADAM_B1) * gv
        v2 = ADAM_B2 * vv + (1.0 - ADAM_B2) * (gv * gv)
        m_hat = m2 / (1.0 - ADAM_B1 ** ADAM_STEP)
        v_hat = v2 / (1.0 - ADAM_B2 ** ADAM_STEP)
        delta = -ADAM_LR * (m_hat / (jnp.sqrt(v_hat) + ADAM_EPS) + ADAM_WD * wv)
        return delta, m2, v2

    ins = [(a, cols, 0) for a in (w, g, m, v)]
    return _rowwise(fn, ins, [(cols, F32)] * 3, rows=rows, tm=_tile(rows, max(8, 262144 // cols)), name=name)


BIG = ["w_in", "w_branch_a", "w_branch_b", "w_mix_out", "w_mem_q", "w_mem_kv", "w_mem_o", "w_ffn_in", "w_ffn_out"]
BIG_AXIS = {"w_in": 1, "w_branch_a": 1, "w_branch_b": 1, "w_mix_out": 0, "w_mem_q": 0, "w_mem_kv": 1,
            "w_mem_o": 0, "w_ffn_in": 1, "w_ffn_out": 0}
NORMS = ["norm_mix", "norm_mem_q", "norm_mem_kv", "norm_ffn", "norm_final"]
ORDER = ["norm_mix", "w_in", "conv_w", "w_branch_a", "w_branch_b", "w_mix_out", "norm_mem_q", "norm_mem_kv",
         "w_mem_q", "w_mem_kv", "w_mem_o", "norm_ffn", "w_ffn_in", "w_ffn_out", "norm_final"]


def _pack_small(vals, conv):
    d = vals[0].shape[-1]
    rows = [v.reshape(1, d) for v in vals]
    conv = jnp.pad(conv, ((0, 0), (0, d - conv.shape[1])))
    pad = jnp.zeros((SMALL_ROWS - len(rows) - CONV_K, d), F32)
    return jnp.concatenate(rows + [conv, pad], axis=0)


def kernel(x, mem, norm_mix, w_in, conv_w, w_branch_a, w_branch_b, w_mix_out, norm_mem_q, norm_mem_kv, w_mem_q, w_mem_kv, w_mem_o, norm_ffn, w_ffn_in, w_ffn_out, norm_final, loss_target, m_norm_mix, m_w_in, m_conv_w, m_w_branch_a, m_w_branch_b, m_w_mix_out, m_norm_mem_q, m_norm_mem_kv, m_w_mem_q, m_w_mem_kv, m_w_mem_o, m_norm_ffn, m_w_ffn_in, m_w_ffn_out, m_norm_final, v_norm_mix, v_w_in, v_conv_w, v_w_branch_a, v_w_branch_b, v_w_mix_out, v_norm_mem_q, v_norm_mem_kv, v_w_mem_q, v_w_mem_kv, v_w_mem_o, v_norm_ffn, v_w_ffn_in, v_w_ffn_out, v_norm_final):
    args = dict(locals())
    wts = {n: args[n] for n in ORDER}
    mom = {n: args["m_" + n] for n in ORDER}
    var = {n: args["v_" + n] for n in ORDER}
    x = x[0]
    mem = mem[0]
    target = loss_target[0]
    s, d = x.shape
    gains = {n: wts[n].reshape(1, d) for n in NORMS}
    chip = 2 * lax.axis_index("x") + lax.axis_index("y")
    core = lax.axis_index("c").astype(jnp.int32).reshape(1)

    shards = [wts[n][0].astype(BF) for n in BIG]
    axes = [BIG_AXIS[n] for n in BIG]
    conv_shard = jnp.pad(conv_w[0], ((0, CONV_ROWS - CONV_K), (0, 0)))
    full = _gather_weights(shards + [conv_shard], axes + [1])
    W = dict(zip(BIG, full[:-1]))
    conv_full = full[-1]

    h1 = _rmsnorm(x, gains["norm_mix"], "norm_mix_fwd")
    proj = _matmul(h1, W["w_in"], name="in_proj")
    o_a, o_a32 = _sb_fwd(proj)
    y_b = _conv_fwd(proj, conv_full)
    br_a = _matmul(o_a, W["w_branch_a"], name="branch_a")
    br_b = _matmul(y_b, W["w_branch_b"], name="branch_b")
    ga_blk, gb_blk = 3, 4

    def merge(ga, gb, a, b):
        return jax.nn.sigmoid(ga.astype(F32)) * a.astype(F32) + jax.nn.sigmoid(gb.astype(F32)) * b.astype(F32)

    merged = _rowwise(merge, [(proj, d, ga_blk), (proj, d, gb_blk), (br_a, d, 0), (br_b, d, 0)], [(d, BF)],
                      rows=s, tm=512, name="merge_fwd")[0]
    x1 = _matmul(merged, W["w_mix_out"], out_dtype=F32, resid=x, name="mix_out")

    hq = _rmsnorm(x1, gains["norm_mem_q"], "norm_mem_q_fwd")
    mn = _rmsnorm(mem, gains["norm_mem_kv"], "norm_mem_kv_fwd")
    q_m = _matmul(hq, W["w_mem_q"], name="mem_q")
    kv = _matmul(mn, W["w_mem_kv"], name="mem_kv")
    o_m = _mem_fwd(q_m, kv)
    x2 = _matmul(o_m, W["w_mem_o"], out_dtype=F32, resid=x1, name="mem_o")

    hf = _rmsnorm(x2, gains["norm_ffn"], "norm_ffn_fwd")
    gu = _matmul(hf, W["w_ffn_in"], name="ffn_in")
    f = gu.shape[1] // 2

    def swiglu(gate, up):
        gate = gate.astype(F32)
        return gate * jax.nn.sigmoid(gate) * up.astype(F32)

    act = _rowwise(swiglu, [(gu, f, 0), (gu, f, 1)], [(f, BF)], rows=s, tm=512, name="swiglu_fwd")[0]
    x3 = _matmul(act, W["w_ffn_out"], out_dtype=F32, resid=x2, name="ffn_out")

    dx3, dg_final, loss_part = _loss_head(x3, gains["norm_final"], target, "loss_head")

    dact = _matmul(dx3, W["w_ffn_out"], tb=True, name="d_act")
    gw = {"w_ffn_out": _matmul(act, dx3, ta=True, tm=1408, tn=1024, tk=512, out_dtype=F32, name="gw_ffn_out")}

    def swiglu_bwd(gate, up, da):
        gate, up, da = gate.astype(F32), up.astype(F32), da.astype(F32)
        sg = jax.nn.sigmoid(gate)
        dgate = da * up * (sg * (1.0 + gate * (1.0 - sg)))
        return jnp.concatenate([dgate, da * (gate * sg)], axis=-1)

    dgu = _rowwise(swiglu_bwd, [(gu, f, 0), (gu, f, 1), (dact, f, 0)], [(2 * f, BF)], rows=s, tm=256,
                   name="swiglu_bwd")[0]
    dhf = _matmul(dgu, W["w_ffn_in"], tb=True, tk=1408, out_dtype=F32, name="d_hf")
    gw["w_ffn_in"] = _matmul(hf, dgu, ta=True, tm=1024, tn=1408, tk=512, out_dtype=F32, name="gw_ffn_in")
    dx2, dg_ffn = _rmsnorm_bwd(x2, gains["norm_ffn"], dhf, dx3, "norm_ffn_bwd")

    do_m = _matmul(dx2, W["w_mem_o"], tb=True, name="d_o_m")
    gw["w_mem_o"] = _matmul(o_m, dx2, ta=True, tk=512, tn=1024, out_dtype=F32, name="gw_mem_o")
    dq_m, dk_m, dv_m = _mem_bwd(q_m, kv, do_m)
    dkv = jnp.concatenate([dk_m, dv_m], axis=-1)
    dhq = _matmul(dq_m, W["w_mem_q"], tb=True, out_dtype=F32, name="d_hq")
    gw["w_mem_q"] = _matmul(hq, dq_m, ta=True, tk=512, tn=1024, out_dtype=F32, name="gw_mem_q")
    dmn = _matmul(dkv, W["w_mem_kv"], tb=True, out_dtype=F32, name="d_mn")
    gw["w_mem_kv"] = _matmul(mn, dkv, ta=True, tn=1024, out_dtype=F32, name="gw_mem_kv")
    _, dg_kv = _rmsnorm_bwd(mem, gains["norm_mem_kv"], dmn, None, "norm_mem_kv_bwd")
    dx1, dg_q = _rmsnorm_bwd(x1, gains["norm_mem_q"], dhq, dx2, "norm_mem_q_bwd")

    dmerged = _matmul(dx1, W["w_mix_out"], tb=True, name="d_merged")
    gw["w_mix_out"] = _matmul(merged, dx1, ta=True, tk=512, tn=1024, out_dtype=F32, name="gw_mix_out")

    def merge_bwd(ga, gb, a, b, dm):
        dm = dm.astype(F32)
        sa, sb = jax.nn.sigmoid(ga.astype(F32)), jax.nn.sigmoid(gb.astype(F32))
        a, b = a.astype(F32), b.astype(F32)
        return dm * sa, dm * sb, dm * a * (sa * (1.0 - sa)), dm * b * (sb * (1.0 - sb))

    dbr_a, dbr_b, dga, dgb = _rowwise(
        merge_bwd, [(proj, d, ga_blk), (proj, d, gb_blk), (br_a, d, 0), (br_b, d, 0), (dmerged, d, 0)],
        [(d, BF)] * 4, rows=s, tm=256, name="merge_bwd")
    do_a = _matmul(dbr_a, W["w_branch_a"], tb=True, name="d_o_a")
    gw["w_branch_a"] = _matmul(o_a, dbr_a, ta=True, tk=512, tn=1024, out_dtype=F32, name="gw_branch_a")
    dy_b = _matmul(dbr_b, W["w_branch_b"], tb=True, name="d_y_b")
    gw["w_branch_b"] = _matmul(y_b, dbr_b, ta=True, tk=512, tn=1024, out_dtype=F32, name="gw_branch_b")
    du, dgate_b, dgate_c, dconv = _conv_bwd(proj, conv_full, dy_b)
    dq, dk, dv = _sb_bwd(proj, o_a32, do_a)

    def assemble(*parts):
        return jnp.concatenate([p.astype(BF) for p in parts], axis=-1)

    hw = dq.shape[1]
    dproj = _rowwise(assemble, [(t, hw, 0) for t in (dq, dk, dv, du, dgate_b, dgate_c)] + [(dga, d, 0), (dgb, d, 0)],
                     [(proj.shape[1], BF)], rows=s, tm=256, name="assemble_dproj")[0]
    dh1 = _matmul(dproj, W["w_in"], tb=True, tk=1280, out_dtype=F32, name="d_h1")
    gw["w_in"] = _matmul(h1, dproj, ta=True, tk=512, tn=1280, out_dtype=F32, name="gw_in")
    grad_x, dg_mix = _rmsnorm_bwd(x, gains["norm_mix"], dh1, dx1, "norm_mix_bwd")

    views = []
    for n in BIG:
        r, cdim = gw[n].shape
        views.append(gw[n].reshape(1, 2, r // 2, cdim) if BIG_AXIS[n] == 1
                     else gw[n].reshape(N_CHIPS, 2, r // (2 * N_CHIPS), cdim))
    got = _pair_exchange(views)
    sums = [_pair_add(v, g, core, "pair_add_" + n) for n, v, g in zip(BIG, views, got)]
    slots = _chip_exchange(sums, axes)
    halves = [_chip_sum(sl, "chip_sum_" + n) for n, sl in zip(BIG, slots)]
    both = _half_swap(halves)
    grads = {n: b.reshape(wts[n].shape[1:]) for n, b in zip(BIG, both)}

    small_g = [dg_mix, dg_q, dg_kv, dg_ffn, dg_final]
    pack = _pack_small(small_g, dconv[:CONV_K])
    pack = pack.at[ROW_LOSS].set(jnp.broadcast_to(loss_part[0, :1], (d,)))
    red = _allreduce_small(pack)
    loss = red[ROW_LOSS, 0]
    cw = conv_w.shape[2]
    conv_g = lax.dynamic_slice(red, (ROW_CONV, chip * cw), (CONV_K, cw))
    small_grad = _pack_small([red[i] for i in range(len(NORMS))], conv_g)
    small = [_pack_small([t[n] for n in NORMS], t["conv_w"][0]) for t in (wts, mom, var)]
    s_delta, s_m, s_v = _adamw(small[0], small_grad, small[1], small[2], "adamw_small")

    out = {"grad": {}, "delta": {}, "new_m": {}, "new_v": {}}
    for n in BIG:
        shp = wts[n].shape
        dl, m2, v2 = _adamw(wts[n][0], grads[n], mom[n][0], var[n][0], "adamw_" + n)
        out["grad"][n] = grads[n].reshape(shp)
        out["delta"][n], out["new_m"][n], out["new_v"][n] = dl.reshape(shp), m2.reshape(shp), v2.reshape(shp)
    for key, blk in (("grad", small_grad), ("delta", s_delta), ("new_m", s_m), ("new_v", s_v)):
        for i, n in enumerate(NORMS):
            out[key][n] = blk[i].reshape(wts[n].shape)
        out[key]["conv_w"] = blk[ROW_CONV:ROW_CONV + CONV_K, :cw].reshape(conv_w.shape)

    return (loss, grad_x[None], *[out["grad"][n] for n in ORDER], *[out["delta"][n] for n in ORDER],
            *[out["new_m"][n] for n in ORDER], *[out["new_v"][n] for n in ORDER])
```

```python
import functools
import math

import jax
import jax.numpy as jnp
from jax import lax
from jax.experimental import pallas as pl
from jax.experimental.pallas import tpu as pltpu

BF = jnp.bfloat16
F32 = jnp.float32
MESH = pl.DeviceIdType.MESH

SB_HEAD_DIM = 64
LANES = 128
MEM_HEADS = 4
CONV_K = 3
CONV_ROWS = 8
EPS = 1e-6
N_CHIPS = 4
N_DEV = 8
VMEM_LIMIT = 56 * 1024 * 1024

ADAM_LR = 0.001
ADAM_B1 = 0.9
ADAM_B2 = 0.999
ADAM_EPS = 1e-08
ADAM_WD = 0.01
ADAM_STEP = 10

SMALL_ROWS = 16
ROW_CONV = 5
ROW_LOSS = 8


def _params(sem=None):
    return pltpu.CompilerParams(dimension_semantics=sem, vmem_limit_bytes=VMEM_LIMIT)


def _tile(dim, pref):
    if dim <= pref:
        return dim
    for step in (LANES, 8):
        t = (pref // step) * step
        while t >= step:
            if dim % t == 0:
                return t
            t -= step
    raise ValueError(f"no tile of {dim} under {pref}")


def _matmul(a, b, *, ta=False, tb=False, tm=1024, tn=512, tk=None, out_dtype=BF, resid=None, name):
    if ta:
        kdim, m = a.shape
    else:
        m, kdim = a.shape
    n = b.shape[0] if tb else b.shape[1]
    tm, tn = _tile(m, tm), _tile(n, tn)
    tk = _tile(kdim, tk or kdim)
    nk = kdim // tk
    a_spec = pl.BlockSpec((tk, tm), lambda i, j, k: (k, i)) if ta else pl.BlockSpec((tm, tk), lambda i, j, k: (i, k))
    b_spec = pl.BlockSpec((tn, tk), lambda i, j, k: (j, k)) if tb else pl.BlockSpec((tk, tn), lambda i, j, k: (k, j))
    o_spec = pl.BlockSpec((tm, tn), lambda i, j, k: (i, j))
    dims = (((0 if ta else 1,), (1 if tb else 0,)), ((), ()))
    has_res = resid is not None

    def body(*refs):
        a_ref, b_ref = refs[0], refs[1]
        o_ref = refs[2 + has_res]
        av, bv = a_ref[...], b_ref[...]
        if av.dtype != BF:
            av = av.astype(BF)
        if bv.dtype != BF:
            bv = bv.astype(BF)
        p = lax.dot_general(av, bv, dims, preferred_element_type=F32)

        def finish(acc):
            if has_res:
                acc = refs[2][...] + acc
            o_ref[...] = acc.astype(o_ref.dtype)

        if nk == 1:
            finish(p)
        else:
            acc_ref = refs[-1]
            k = pl.program_id(2)

            @pl.when(k == 0)
            def _():
                acc_ref[...] = p

            @pl.when(k > 0)
            def _():
                acc_ref[...] += p

            @pl.when(k == nk - 1)
            def _():
                finish(acc_ref[...])

    return pl.pallas_call(
        body, name=name, grid=(m // tm, n // tn, nk),
        in_specs=[a_spec, b_spec] + ([o_spec] if has_res else []),
        out_specs=o_spec, out_shape=jax.ShapeDtypeStruct((m, n), out_dtype),
        scratch_shapes=[pltpu.VMEM((tm, tn), F32)] if nk > 1 else [],
        compiler_params=_params(("parallel", "parallel", "arbitrary")),
    )(*([a, b] + ([resid] if has_res else [])))


def _rowwise(fn, ins, outs, *, rows, tm, name, accs=()):
    tm = _tile(rows, tm)
    in_specs, args = [], []
    for arr, cols, cb in ins:
        if cols is None:
            in_specs.append(pl.BlockSpec(arr.shape, lambda i, nd=arr.ndim: (0,) * nd))
        else:
            in_specs.append(pl.BlockSpec((tm, cols), lambda i, cb=cb: (i, cb)))
        args.append(arr)
    out_specs = [pl.BlockSpec((tm, cols), lambda i: (i, 0)) for cols, _ in outs]
    out_shape = [jax.ShapeDtypeStruct((rows, cols), dt) for cols, dt in outs]
    for r, c in accs:
        out_specs.append(pl.BlockSpec((r, c), lambda i: (0, 0)))
        out_shape.append(jax.ShapeDtypeStruct((r, c), F32))
    n_in, n_out = len(ins), len(outs)

    def body(*refs):
        res = fn(*[r[...] for r in refs[:n_in]])
        if not isinstance(res, (tuple, list)):
            res = (res,)
        for o_ref, val in zip(refs[n_in:n_in + n_out], res[:n_out]):
            o_ref[...] = val.astype(o_ref.dtype)
        first = pl.program_id(0) == 0
        for a_ref, val in zip(refs[n_in + n_out:], res[n_out:]):
            @pl.when(first)
            def _(a_ref=a_ref, val=val):
                a_ref[...] = val

            @pl.when(jnp.logical_not(first))
            def _(a_ref=a_ref, val=val):
                a_ref[...] += val

    res = pl.pallas_call(
        body, name=name, grid=(rows // tm,), in_specs=in_specs, out_specs=out_specs, out_shape=out_shape,
        compiler_params=_params(("arbitrary",) if accs else ("parallel",)),
    )(*args)
    return res


def _rstd(xf):
    return lax.rsqrt(jnp.mean(xf * xf, axis=-1, keepdims=True) + EPS)


def _rmsnorm(x, g, name):
    rows, d = x.shape
    return _rowwise(lambda xv, gv: xv * _rstd(xv) * gv, [(x, d, 0), (g, None, None)], [(d, BF)],
                    rows=rows, tm=512, name=name)[0]


def _rmsnorm_bwd(x, g, dy, resid, name):
    rows, d = x.shape

    def fn(xv, gv, dyv, *rest):
        dyv = dyv.astype(F32)
        r = _rstd(xv)
        xn = xv * r
        dxn = dyv * gv
        dx = r * (dxn - xn * jnp.mean(dxn * xn, axis=-1, keepdims=True))
        if rest:
            dx = rest[0] + dx
        return dx, jnp.sum(dyv * xn, axis=0, keepdims=True)

    ins = [(x, d, 0), (g, None, None), (dy, d, 0)] + ([(resid, d, 0)] if resid is not None else [])
    outs = [(d, F32)]
    return _rowwise(fn, ins, outs, rows=rows, tm=512, name=name, accs=[(1, d)])


def _loss_head(x, g, target, name):
    rows, d = x.shape

    def fn(xv, gv, tv):
        r = _rstd(xv)
        xn = xv * r
        err = xn * gv - tv
        per_tok = jnp.mean(err * err, axis=-1, keepdims=True)
        loss = 0.5 * jnp.sum(per_tok, axis=0, keepdims=True)
        dyv = err * (1.0 / d)
        dxn = dyv * gv
        dx = r * (dxn - xn * jnp.mean(dxn * xn, axis=-1, keepdims=True))
        return dx, jnp.sum(dyv * xn, axis=0, keepdims=True), jnp.broadcast_to(loss, (1, LANES))

    return _rowwise(fn, [(x, d, 0), (g, None, None), (target, d, 0)], [(d, F32)],
                    rows=rows, tm=512, name=name, accs=[(1, d), (1, LANES)])


SB_TK = 128
SB_KT = 2


def _sb_consts(tq):
    tk = SB_TK
    diff = lax.broadcasted_iota(jnp.int32, (tq, tk), 1) - lax.broadcasted_iota(jnp.int32, (tq, tk), 0)
    rj = lax.broadcasted_iota(jnp.int32, (2 * tk, 2 * tk), 0) & (tk - 1)
    cj = lax.broadcasted_iota(jnp.int32, (2 * tk, 2 * tk), 1)
    ones_half = cj >= tk
    later = jnp.where((rj > cj) | ones_half, 1.0, 0.0).astype(BF)
    later_incl = jnp.where((rj >= cj) | ones_half, 1.0, 0.0).astype(BF)
    return diff, later, later_incl


def _split_dot(val, rhs_twice):
    hi = val.astype(BF)
    lo = (val - hi.astype(F32)).astype(BF)
    return jnp.dot(jnp.concatenate([hi, lo], axis=1), rhs_twice, preferred_element_type=F32)


def _log_terms(z):
    sp = jnp.maximum(z, 0.0) + jnp.log(1.0 + jnp.exp(-jnp.abs(z)))
    return z - sp, sp


NT = (((1,), (1,)), ((), ()))
TN = (((0,), (0,)), ((), ()))


def _head_lane_masks(rows):
    lane = lax.broadcasted_iota(jnp.int32, (rows, LANES), 1)
    first = jnp.where(lane < SB_HEAD_DIM, 1.0, 0.0)
    return first.astype(BF), (1.0 - first).astype(BF)


def _both_heads(tile, masks):
    return jnp.concatenate([tile * masks[0], tile * masks[1]], axis=0)


def _sb_fwd(proj):
    s = proj.shape[0]
    tk, tq = SB_TK, SB_KT * SB_TK
    n_pairs = 4
    scale = 1.0 / math.sqrt(SB_HEAD_DIM)

    def body(q_ref, k_ref, v_ref, o_ref, o32_ref):
        i = pl.program_id(1)
        diff, later, _ = _sb_consts(tq)
        qs = (q_ref[...].astype(F32) * scale).astype(BF)
        lane_masks = _head_lane_masks(tk)

        def step(g, state, masked):
            tiles = list(reversed(range(SB_KT)))
            chains = [(t, h) for t in tiles for h in range(2)]
            rows = {t: pl.ds(pl.multiple_of((g * SB_KT + t) * tk, tk), tk) for t in tiles}
            ks = {t: _both_heads(k_ref[rows[t], :], lane_masks) for t in tiles}
            vs = {t: _both_heads(v_ref[rows[t], :], lane_masks) for t in tiles}
            allowed = {t: diff < -t * tk for t in tiles}
            zs = {t: lax.dot_general(qs, ks[t], NT, preferred_element_type=F32) for t in tiles}
            logs = {}
            for t, h in chains:
                log_b, sp = _log_terms(zs[t][:, h * tk:(h + 1) * tk])
                logs[t, h] = (log_b, jnp.where(allowed[t], sp, 0.0) if masked else sp)
            sums = {c: _split_dot(logs[c][1], later) for c in chains}
            carries = list(state[0])
            ws = {}
            for t, h in chains:
                w = jnp.exp(logs[t, h][0] - (sums[t, h][:, :tk] + carries[h]))
                ws[t, h] = (jnp.where(allowed[t], w, 0.0) if masked else w).astype(BF)
                carries[h] = carries[h] + sums[t, h][:, tk:]
            acc = state[1]
            for t in tiles:
                acc = acc + jnp.dot(jnp.concatenate([ws[t, 0], ws[t, 1]], axis=1), vs[t],
                                    preferred_element_type=F32)
            return tuple(carries), acc

        zero = jnp.zeros((tq, LANES), F32)
        state = step(i, ((zero, zero), zero), True)
        state = lax.fori_loop(0, i, lambda gg, st: step(i - 1 - gg, st, False), state)
        o_ref[...] = state[1].astype(o_ref.dtype)
        o32_ref[...] = state[1]

    tok = pl.BlockSpec((tq, LANES), lambda p, i: (i, p))
    return pl.pallas_call(
        body, name="sb_attn_fwd", grid=(n_pairs, s // tq),
        in_specs=[tok,
                  pl.BlockSpec((s, LANES), lambda p, i: (0, n_pairs + p)),
                  pl.BlockSpec((s, LANES), lambda p, i: (0, 2 * n_pairs + p))],
        out_specs=[tok, tok],
        out_shape=[jax.ShapeDtypeStruct((s, n_pairs * LANES), BF), jax.ShapeDtypeStruct((s, n_pairs * LANES), F32)],
        compiler_params=_params(("parallel", "arbitrary")),
    )(proj, proj, proj)


def _sb_bwd(proj, o32, do_a):
    s = proj.shape[0]
    tk, tq = SB_TK, SB_KT * SB_TK
    n_pairs = 4
    scale = 1.0 / math.sqrt(SB_HEAD_DIM)

    def body(q_ref, k_ref, v_ref, o_ref, do_ref, dq_ref, dk_ref, dv_ref):
        i = pl.program_id(1)

        @pl.when(i == 0)
        def _():
            dk_ref[...] = jnp.zeros_like(dk_ref)
            dv_ref[...] = jnp.zeros_like(dv_ref)

        diff, later, later_incl = _sb_consts(tq)
        qs = (q_ref[...].astype(F32) * scale).astype(BF)
        do2 = do_ref[...]
        prod = do2.astype(F32) * o_ref[...]
        lane_masks = _head_lane_masks(tk)
        first_head = lax.broadcasted_iota(jnp.int32, (tq, LANES), 1) < SB_HEAD_DIM
        totals = [jnp.broadcast_to(jnp.sum(jnp.where(keep, prod, 0.0), axis=-1, keepdims=True), (tq, tk))
                  for keep in (first_head, jnp.logical_not(first_head))]
        first_head_k = first_head[:tk]

        def step(g_idx, state, masked):
            tiles = list(reversed(range(SB_KT)))
            chains = [(t, h) for t in tiles for h in range(2)]
            rows = {t: pl.ds(pl.multiple_of((g_idx * SB_KT + t) * tk, tk), tk) for t in tiles}
            ks = {t: _both_heads(k_ref[rows[t], :], lane_masks) for t in tiles}
            vs = {t: _both_heads(v_ref[rows[t], :], lane_masks) for t in tiles}
            allowed = {t: diff < -t * tk for t in tiles}
            zs = {t: lax.dot_general(qs, ks[t], NT, preferred_element_type=F32) for t in tiles}
            dws = {t: lax.dot_general(do2, vs[t], NT, preferred_element_type=F32) for t in tiles}
            logs = {}
            for t, h in chains:
                log_b, sp = _log_terms(zs[t][:, h * tk:(h + 1) * tk])
                logs[t, h] = (log_b, jnp.where(allowed[t], sp, 0.0) if masked else sp)
            sums = {c: _split_dot(logs[c][1], later) for c in chains}
            c_log, c_g = list(state[0]), list(state[1])
            ws, gs = {}, {}
            for t, h in chains:
                w = jnp.exp(logs[t, h][0] - (sums[t, h][:, :tk] + c_log[h]))
                ws[t, h] = (jnp.where(allowed[t], w, 0.0) if masked else w).astype(BF)
                c_log[h] = c_log[h] + sums[t, h][:, tk:]
                gs[t, h] = ws[t, h].astype(F32) * dws[t][:, h * tk:(h + 1) * tk]
            gsums = {c: _split_dot(gs[c], later_incl) for c in chains}
            dzs = {}
            for t, h in chains:
                beta = jnp.exp(logs[t, h][0])
                earlier = totals[h] - (gsums[t, h][:, :tk] + c_g[h])
                dz = gs[t, h] * (1.0 - beta) - earlier * beta
                dzs[t, h] = (jnp.where(allowed[t], dz, 0.0) if masked else dz).astype(BF)
                c_g[h] = c_g[h] + gsums[t, h][:, tk:]
            dq = state[2]
            for t in tiles:
                dz_both = jnp.concatenate([dzs[t, 0], dzs[t, 1]], axis=1)
                w_both = jnp.concatenate([ws[t, 0], ws[t, 1]], axis=1)
                dq = dq + jnp.dot(dz_both, ks[t], preferred_element_type=F32)
                dk2 = lax.dot_general(dz_both, qs, TN, preferred_element_type=F32)
                dv2 = lax.dot_general(w_both, do2, TN, preferred_element_type=F32)
                dk_ref[rows[t], :] += jnp.where(first_head_k, dk2[:tk], dk2[tk:])
                dv_ref[rows[t], :] += jnp.where(first_head_k, dv2[:tk], dv2[tk:])
            return tuple(c_log), tuple(c_g), dq

        zero = jnp.zeros((tq, LANES), F32)
        state = step(i, ((zero, zero), (zero, zero), zero), True)
        state = lax.fori_loop(0, i, lambda gg, st: step(i - 1 - gg, st, False), state)
        dq_ref[...] = (state[2] * scale).astype(dq_ref.dtype)

    width = n_pairs * LANES
    return pl.pallas_call(
        body, name="sb_attn_bwd", grid=(n_pairs, s // tq),
        in_specs=[pl.BlockSpec((tq, LANES), lambda p, i: (i, p)),
                  pl.BlockSpec((s, LANES), lambda p, i: (0, n_pairs + p)),
                  pl.BlockSpec((s, LANES), lambda p, i: (0, 2 * n_pairs + p)),
                  pl.BlockSpec((tq, LANES), lambda p, i: (i, p)),
                  pl.BlockSpec((tq, LANES), lambda p, i: (i, p))],
        out_specs=[pl.BlockSpec((tq, LANES), lambda p, i: (i, p)),
                   pl.BlockSpec((s, LANES), lambda p, i: (0, p)),
                   pl.BlockSpec((s, LANES), lambda p, i: (0, p))],
        out_shape=[jax.ShapeDtypeStruct((s, width), BF), jax.ShapeDtypeStruct((s, width), F32),
                   jax.ShapeDtypeStruct((s, width), F32)],
        compiler_params=_params(("parallel", "arbitrary")),
    )(proj, proj, proj, o32, do_a)


CONV_COL0 = 12


def _shift_rows(v, k):
    n = v.shape[0]
    row = lax.broadcasted_iota(jnp.int32, v.shape, 0)
    rolled = pltpu.roll(v, k % n, axis=0)
    keep = row >= k if k > 0 else row < n + k
    return jnp.where(keep, rolled, 0.0)


def _conv_specs(s):
    return [pl.BlockSpec((s, LANES), lambda cb: (0, CONV_COL0 + cb)),
            pl.BlockSpec((s, LANES), lambda cb: (0, CONV_COL0 + 4 + cb)),
            pl.BlockSpec((s, LANES), lambda cb: (0, CONV_COL0 + 8 + cb)),
            pl.BlockSpec((CONV_ROWS, LANES), lambda cb: (0, cb))]


def _conv_fwd(proj, conv_w):
    s = proj.shape[0]

    def body(u_ref, gb_ref, gc_ref, w_ref, y_ref):
        cu = gc_ref[...].astype(F32) * u_ref[...].astype(F32)
        w = w_ref[...]
        y = w[0:1] * _shift_rows(cu, 2) + w[1:2] * _shift_rows(cu, 1) + w[2:3] * cu
        y_ref[...] = (gb_ref[...].astype(F32) * y).astype(y_ref.dtype)

    return pl.pallas_call(
        body, name="conv_fwd", grid=(4,), in_specs=_conv_specs(s),
        out_specs=pl.BlockSpec((s, LANES), lambda cb: (0, cb)),
        out_shape=jax.ShapeDtypeStruct((s, 4 * LANES), BF),
        compiler_params=_params(("parallel",)),
    )(proj, proj, proj, conv_w)


def _conv_bwd(proj, conv_w, dy):
    s = proj.shape[0]

    def body(u_ref, gb_ref, gc_ref, w_ref, dy_ref, du_ref, dgb_ref, dgc_ref, dw_ref):
        u, gc = u_ref[...].astype(F32), gc_ref[...].astype(F32)
        dyv = dy_ref[...].astype(F32)
        w = w_ref[...]
        cu = gc * u
        cu1, cu2 = _shift_rows(cu, 1), _shift_rows(cu, 2)
        conv = w[0:1] * cu2 + w[1:2] * cu1 + w[2:3] * cu
        dgb_ref[...] = (dyv * conv).astype(dgb_ref.dtype)
        dc = dyv * gb_ref[...].astype(F32)
        dcu = w[2:3] * dc + w[1:2] * _shift_rows(dc, -1) + w[0:1] * _shift_rows(dc, -2)
        dgc_ref[...] = (dcu * u).astype(dgc_ref.dtype)
        du_ref[...] = (dcu * gc).astype(du_ref.dtype)
        tap_row = lax.broadcasted_iota(jnp.int32, (CONV_ROWS, LANES), 0)
        dw = jnp.zeros((CONV_ROWS, LANES), F32)
        for t, shifted in enumerate((cu2, cu1, cu)):
            dw = jnp.where(tap_row == t, jnp.sum(dc * shifted, axis=0, keepdims=True), dw)
        dw_ref[...] = dw

    col = pl.BlockSpec((s, LANES), lambda cb: (0, cb))
    act = jax.ShapeDtypeStruct((s, 4 * LANES), BF)
    return pl.pallas_call(
        body, name="conv_bwd", grid=(4,), in_specs=_conv_specs(s) + [col],
        out_specs=[col, col, col, pl.BlockSpec((CONV_ROWS, LANES), lambda cb: (0, cb))],
        out_shape=[act, act, act, jax.ShapeDtypeStruct((CONV_ROWS, 4 * LANES), F32)],
        compiler_params=_params(("parallel",)),
    )(proj, proj, proj, conv_w, dy)


def _mem_probs(q, k, scale):
    sc = lax.dot_general(q, k, NT, preferred_element_type=F32) * scale
    p = jnp.exp(sc - jnp.max(sc, axis=-1, keepdims=True))
    return p / jnp.sum(p, axis=-1, keepdims=True)


def _mem_fwd(q_m, kv, tq=512):
    s, d = q_m.shape
    mlen = kv.shape[0]
    hd = d // MEM_HEADS
    tq = _tile(s, tq)
    scale = 1.0 / math.sqrt(hd)

    def body(q_ref, k_ref, v_ref, o_ref):
        p = _mem_probs(q_ref[...], k_ref[...], scale)
        o_ref[...] = jnp.dot(p.astype(BF), v_ref[...], preferred_element_type=F32).astype(o_ref.dtype)

    return pl.pallas_call(
        body, name="mem_attn_fwd", grid=(MEM_HEADS, s // tq),
        in_specs=[pl.BlockSpec((tq, hd), lambda h, i: (i, h)),
                  pl.BlockSpec((mlen, hd), lambda h, i: (0, h)),
                  pl.BlockSpec((mlen, hd), lambda h, i: (0, MEM_HEADS + h))],
        out_specs=pl.BlockSpec((tq, hd), lambda h, i: (i, h)),
        out_shape=jax.ShapeDtypeStruct((s, d), BF),
        compiler_params=_params(("parallel", "parallel")),
    )(q_m, kv, kv)


def _mem_bwd(q_m, kv, do_m, tq=512):
    s, d = q_m.shape
    mlen = kv.shape[0]
    hd = d // MEM_HEADS
    tq = _tile(s, tq)
    scale = 1.0 / math.sqrt(hd)

    def body(q_ref, k_ref, v_ref, do_ref, dq_ref, dk_ref, dv_ref):
        q, k, v, do = q_ref[...], k_ref[...], v_ref[...], do_ref[...]
        p = _mem_probs(q, k, scale)
        dp = lax.dot_general(do, v, NT, preferred_element_type=F32)
        ds = p * (dp - jnp.sum(dp * p, axis=-1, keepdims=True)) * scale
        dsb = ds.astype(BF)
        dq_ref[...] = jnp.dot(dsb, k, preferred_element_type=F32).astype(dq_ref.dtype)
        dk = lax.dot_general(dsb, q, TN, preferred_element_type=F32)
        dv = lax.dot_general(p.astype(BF), do, TN, preferred_element_type=F32)
        first = pl.program_id(1) == 0

        @pl.when(first)
        def _():
            dk_ref[...] = dk
            dv_ref[...] = dv

        @pl.when(jnp.logical_not(first))
        def _():
            dk_ref[...] += dk
            dv_ref[...] += dv

    tok = pl.BlockSpec((tq, hd), lambda h, i: (i, h))
    memb = pl.BlockSpec((mlen, hd), lambda h, i: (0, h))
    return pl.pallas_call(
        body, name="mem_attn_bwd", grid=(MEM_HEADS, s // tq),
        in_specs=[tok, memb, pl.BlockSpec((mlen, hd), lambda h, i: (0, MEM_HEADS + h)), tok],
        out_specs=[tok, memb, memb],
        out_shape=[jax.ShapeDtypeStruct((s, d), BF), jax.ShapeDtypeStruct((mlen, d), F32),
                   jax.ShapeDtypeStruct((mlen, d), F32)],
        compiler_params=_params(("parallel", "arbitrary")),
    )(q_m, kv, kv, do_m)


def _place():
    x, y, c = lax.axis_index("x"), lax.axis_index("y"), lax.axis_index("c")
    other_chips = [(1 - x, y), (x, 1 - y), (1 - x, 1 - y)]
    return x, y, c, other_chips


def _chip_no(cx, cy):
    return 2 * cx + cy


HBM_SPEC = pl.BlockSpec(memory_space=pl.ANY)


def _cast_place(shard, axis, place, dtype, name):
    r, c = shard.shape
    tr = _tile(r, max(16, 1048576 // c))
    nblk = r // tr
    if axis == 1:
        full, out_map = (r, N_CHIPS * c), lambda i, pref: (i, pref[0])
    else:
        full, out_map = (N_CHIPS * r, c), lambda i, pref: (pref[0] * nblk + i, 0)

    def body(pref, s_ref, o_ref):
        o_ref[...] = s_ref[...].astype(o_ref.dtype)

    return pl.pallas_call(
        body, name=name,
        grid_spec=pltpu.PrefetchScalarGridSpec(
            num_scalar_prefetch=1, grid=(nblk,),
            in_specs=[pl.BlockSpec((tr, c), lambda i, pref: (i, 0))],
            out_specs=pl.BlockSpec((tr, c), out_map)),
        out_shape=jax.ShapeDtypeStruct(full, dtype),
        compiler_params=_params(("parallel",)),
    )(place, shard)


def _region(ref, axis, chip_no, half):
    width = ref.shape[axis] // N_CHIPS
    start = pl.multiple_of(chip_no * width, width)
    if axis == 1:
        if half is None:
            return ref.at[:, pl.ds(start, width)]
        hr = ref.shape[0] // 2
        return ref.at[pl.ds(pl.multiple_of(half * hr, hr), hr), pl.ds(start, width)]
    if half is None:
        return ref.at[pl.ds(start, width), :]
    hr = width // 2
    return ref.at[pl.ds(pl.multiple_of(start + half * hr, hr), hr), :]


def _gather_weights(fulls, axes, split):
    n = len(fulls)

    def body(*refs):
        outs = refs[n:2 * n]
        send, recv, fsend, frecv = refs[2 * n:]
        x, y, c, others = _place()
        me = _chip_no(x, y)
        sibling = (x, y, 1 - c)

        def copy(w, chip_no, half, sems, p, to):
            reg = _region(outs[w], axes[w], chip_no, half)
            return pltpu.make_async_remote_copy(
                src_ref=reg, dst_ref=reg, send_sem=sems[0].at[w, p], recv_sem=sems[1].at[w, p],
                device_id=to, device_id_type=MESH)

        for w in range(n):
            for p, chip in enumerate(others):
                copy(w, me, c if split[w] else None, (send, recv), p, (chip[0], chip[1], c)).start()
        for w in range(n):
            for p, chip in enumerate(others):
                half = c if split[w] else None
                copy(w, _chip_no(*chip), half, (send, recv), p, (chip[0], chip[1], c)).wait_recv()
                if split[w]:
                    copy(w, _chip_no(*chip), c, (fsend, frecv), p, sibling).start()
        for w in range(n):
            for p, chip in enumerate(others):
                copy(w, me, c if split[w] else None, (send, recv), p, (chip[0], chip[1], c)).wait_send()
                if split[w]:
                    handed = copy(w, _chip_no(*chip), 1 - c, (fsend, frecv), p, sibling)
                    handed.wait_recv()
                    handed.wait_send()

    return pl.pallas_call(
        body, name="gather_weights",
        in_specs=[HBM_SPEC] * n, out_specs=[HBM_SPEC] * n,
        out_shape=[jax.ShapeDtypeStruct(f.shape, f.dtype) for f in fulls],
        input_output_aliases={i: i for i in range(n)},
        scratch_shapes=[pltpu.SemaphoreType.DMA((n, 3))] * 4,
        compiler_params=pltpu.CompilerParams(has_side_effects=True),
    )(*fulls)


def _pair_exchange(grads):
    n = len(grads)

    def body(*refs):
        ins, outs = refs[:n], refs[n:2 * n]
        send, recv = refs[2 * n:]
        x, y, c, _ = _place()
        cps = []
        for w in range(n):
            cp = pltpu.make_async_remote_copy(
                src_ref=ins[w].at[:, 1 - c], dst_ref=outs[w], send_sem=send.at[w], recv_sem=recv.at[w],
                device_id=(x, y, 1 - c), device_id_type=MESH)
            cp.start()
            cps.append(cp)
        for cp in cps:
            cp.wait()

    return pl.pallas_call(
        body, name="grad_pair_exchange",
        in_specs=[HBM_SPEC] * n, out_specs=[HBM_SPEC] * n,
        out_shape=[jax.ShapeDtypeStruct((g.shape[0],) + g.shape[2:], g.dtype) for g in grads],
        scratch_shapes=[pltpu.SemaphoreType.DMA((n,)), pltpu.SemaphoreType.DMA((n,))],
        compiler_params=pltpu.CompilerParams(has_side_effects=True),
    )(*grads)


def _pair_add(g4, got, core, name):
    nj, _, hr, cdim = g4.shape
    tr = _tile(hr, max(8, 524288 // cdim))

    def body(core_ref, a_ref, b_ref, o_ref):
        o_ref[...] = (a_ref[...] + b_ref[...]).astype(o_ref.dtype)

    return pl.pallas_call(
        body, name=name,
        grid_spec=pltpu.PrefetchScalarGridSpec(
            num_scalar_prefetch=1, grid=(nj, hr // tr),
            in_specs=[pl.BlockSpec((1, None, tr, cdim), lambda j, i, core_ref: (j, core_ref[0], i, 0)),
                      pl.BlockSpec((1, tr, cdim), lambda j, i, core_ref: (j, i, 0))],
            out_specs=pl.BlockSpec((1, tr, cdim), lambda j, i, core_ref: (j, i, 0))),
        out_shape=jax.ShapeDtypeStruct((nj, hr, cdim), BF),
        compiler_params=_params(("parallel", "parallel")),
    )(core, g4, got)


def _piece(ref, axis, j, hc):
    if axis == 0:
        return ref.at[j]
    return ref.at[0, :, pl.ds(pl.multiple_of(j * hc, hc), hc)]


def _chip_exchange(sums, axes):
    n = len(sums)
    shapes = []
    for sm, ax in zip(sums, axes):
        shapes.append((N_CHIPS - 1, sm.shape[1], sm.shape[2] // (1 if ax == 0 else N_CHIPS)))

    def body(*refs):
        ins, outs = refs[:n], refs[n:2 * n]
        send, recv = refs[2 * n:]
        x, y, c, others = _place()

        def copy(w, p, chip):
            return pltpu.make_async_remote_copy(
                src_ref=_piece(ins[w], axes[w], _chip_no(*chip), shapes[w][2]), dst_ref=outs[w].at[p],
                send_sem=send.at[w, p], recv_sem=recv.at[w, p],
                device_id=(chip[0], chip[1], c), device_id_type=MESH)

        for w in range(n):
            for p, chip in enumerate(others):
                copy(w, p, chip).start()
        for w in range(n):
            for p, chip in enumerate(others):
                copy(w, p, chip).wait()

    return pl.pallas_call(
        body, name="grad_chip_exchange",
        in_specs=[HBM_SPEC] * n, out_specs=[HBM_SPEC] * n,
        out_shape=[jax.ShapeDtypeStruct(sh, sm.dtype) for sh, sm in zip(shapes, sums)],
        scratch_shapes=[pltpu.SemaphoreType.DMA((n, 3)), pltpu.SemaphoreType.DMA((n, 3))],
        compiler_params=pltpu.CompilerParams(has_side_effects=True),
    )(*sums)


def _chip_sum(psum, slots, axis, place, name):
    _, hr, hc = slots.shape
    tr = _tile(hr, 256)
    own_map = (lambda i, pref: (0, i, pref[0])) if axis == 1 else (lambda i, pref: (pref[0], i, 0))

    def body(pref, own_ref, s_ref, o_ref):
        o_ref[...] = ((own_ref[...].astype(F32) + s_ref[0].astype(F32)) + s_ref[1].astype(F32)) + s_ref[2].astype(F32)

    return pl.pallas_call(
        body, name=name,
        grid_spec=pltpu.PrefetchScalarGridSpec(
            num_scalar_prefetch=1, grid=(hr // tr,),
            in_specs=[pl.BlockSpec((None, tr, hc), own_map),
                      pl.BlockSpec((N_CHIPS - 1, tr, hc), lambda i, pref: (0, i, 0))],
            out_specs=pl.BlockSpec((None, tr, hc), lambda i, pref: (pref[1], i, 0))),
        out_shape=jax.ShapeDtypeStruct((2, hr, hc), F32),
        compiler_params=_params(("parallel",)),
    )(place, psum, slots)


def _half_swap(both):
    n = len(both)

    def body(*refs):
        outs = refs[n:2 * n]
        send, recv = refs[2 * n:]
        x, y, c, _ = _place()

        def copy(w, half):
            return pltpu.make_async_remote_copy(
                src_ref=outs[w].at[half], dst_ref=outs[w].at[half], send_sem=send.at[w], recv_sem=recv.at[w],
                device_id=(x, y, 1 - c), device_id_type=MESH)

        for w in range(n):
            copy(w, c).start()
        for w in range(n):
            copy(w, 1 - c).wait()

    return pl.pallas_call(
        body, name="grad_half_swap",
        in_specs=[HBM_SPEC] * n, out_specs=[HBM_SPEC] * n,
        out_shape=[jax.ShapeDtypeStruct(b.shape, b.dtype) for b in both],
        input_output_aliases={i: i for i in range(n)},
        scratch_shapes=[pltpu.SemaphoreType.DMA((n,)), pltpu.SemaphoreType.DMA((n,))],
        compiler_params=pltpu.CompilerParams(has_side_effects=True),
    )(*both)


def _allreduce_small(pack):
    rows, d = pack.shape

    def body(p_ref, o_ref, slots, send, recv):
        x, y, c, _ = _place()
        me = 4 * x + 2 * y + c
        slots[me] = p_ref[...]
        cps = []
        for k in range(1, N_DEV):
            px, py, pc = x ^ (k >> 2), y ^ ((k >> 1) & 1), c ^ (k & 1)
            cp = pltpu.make_async_remote_copy(
                src_ref=p_ref, dst_ref=slots.at[me], send_sem=send.at[k - 1], recv_sem=recv.at[k - 1],
                device_id=(px, py, pc), device_id_type=MESH)
            cp.start()
            cps.append(cp)
        for k in range(1, N_DEV):
            px, py, pc = x ^ (k >> 2), y ^ ((k >> 1) & 1), c ^ (k & 1)
            arrival = pltpu.make_async_remote_copy(
                src_ref=p_ref, dst_ref=slots.at[4 * px + 2 * py + pc], send_sem=send.at[k - 1],
                recv_sem=recv.at[k - 1], device_id=(px, py, pc), device_id_type=MESH)
            arrival.wait_recv()
            arrival.wait_send()
        acc = slots[0]
        for k in range(1, N_DEV):
            acc = acc + slots[k]
        o_ref[...] = acc

    vm = pl.BlockSpec(memory_space=pltpu.VMEM)
    return pl.pallas_call(
        body, name="allreduce_small", in_specs=[vm], out_specs=vm,
        out_shape=jax.ShapeDtypeStruct((rows, d), F32),
        scratch_shapes=[pltpu.VMEM((N_DEV, rows, d), F32), pltpu.SemaphoreType.DMA((N_DEV - 1,)),
                        pltpu.SemaphoreType.DMA((N_DEV - 1,))],
        compiler_params=pltpu.CompilerParams(has_side_effects=True),
    )(pack)


def _adamw(w, g, m, v, name):
    rows, cols = w.shape

    def fn(wv, gv, mv, vv):
        m2 = ADAM_B1 * mv + (1.0 - ADAM_B1) * gv
        v2 = ADAM_B2 * vv + (1.0 - ADAM_B2) * (gv * gv)
        m_hat = m2 / (1.0 - ADAM_B1 ** ADAM_STEP)
        v_hat = v2 / (1.0 - ADAM_B2 ** ADAM_STEP)
        delta = -ADAM_LR * (m_hat / (jnp.sqrt(v_hat) + ADAM_EPS) + ADAM_WD * wv)
        return delta, m2, v2

    ins = [(a, cols, 0) for a in (w, g, m, v)]
    return _rowwise(fn, ins, [(cols, F32)] * 3, rows=rows, tm=_tile(rows, max(8, 262144 // cols)), name=name)


BIG = ["w_in", "w_branch_a", "w_branch_b", "w_mix_out", "w_mem_q", "w_mem_kv", "w_mem_o", "w_ffn_in", "w_ffn_out"]
BIG_AXIS = {"w_in": 1, "w_branch_a": 1, "w_branch_b": 1, "w_mix_out": 0, "w_mem_q": 0, "w_mem_kv": 1,
            "w_mem_o": 0, "w_ffn_in": 1, "w_ffn_out": 0}
NORMS = ["norm_mix", "norm_mem_q", "norm_mem_kv", "norm_ffn", "norm_final"]
ORDER = ["norm_mix", "w_in", "conv_w", "w_branch_a", "w_branch_b", "w_mix_out", "norm_mem_q", "norm_mem_kv",
         "w_mem_q", "w_mem_kv", "w_mem_o", "norm_ffn", "w_ffn_in", "w_ffn_out", "norm_final"]


def _pack_small(vals, conv):
    d = vals[0].shape[-1]
    rows = [v.reshape(1, d) for v in vals]
    conv = jnp.pad(conv, ((0, 0), (0, d - conv.shape[1])))
    pad = jnp.zeros((SMALL_ROWS - len(rows) - CONV_K, d), F32)
    return jnp.concatenate(rows + [conv, pad], axis=0)


def kernel(x, mem, norm_mix, w_in, conv_w, w_branch_a, w_branch_b, w_mix_out, norm_mem_q, norm_mem_kv, w_mem_q, w_mem_kv, w_mem_o, norm_ffn, w_ffn_in, w_ffn_out, norm_final, loss_target, m_norm_mix, m_w_in, m_conv_w, m_w_branch_a, m_w_branch_b, m_w_mix_out, m_norm_mem_q, m_norm_mem_kv, m_w_mem_q, m_w_mem_kv, m_w_mem_o, m_norm_ffn, m_w_ffn_in, m_w_ffn_out, m_norm_final, v_norm_mix, v_w_in, v_conv_w, v_w_branch_a, v_w_branch_b, v_w_mix_out, v_norm_mem_q, v_norm_mem_kv, v_w_mem_q, v_w_mem_kv, v_w_mem_o, v_norm_ffn, v_w_ffn_in, v_w_ffn_out, v_norm_final):
    args = dict(locals())
    wts = {n: args[n] for n in ORDER}
    mom = {n: args["m_" + n] for n in ORDER}
    var = {n: args["v_" + n] for n in ORDER}
    x = x[0]
    mem = mem[0]
    target = loss_target[0]
    s, d = x.shape
    gains = {n: wts[n].reshape(1, d) for n in NORMS}
    chip = 2 * lax.axis_index("x") + lax.axis_index("y")
    core = lax.axis_index("c").astype(jnp.int32).reshape(1)
    place = jnp.stack([chip, lax.axis_index("c")]).astype(jnp.int32)

    axes = [BIG_AXIS[n] for n in BIG]
    conv_shard = jnp.pad(conv_w[0], ((0, CONV_ROWS - CONV_K), (0, 0)))
    placed = [_cast_place(wts[n][0], BIG_AXIS[n], place, BF, "place_" + n) for n in BIG]
    placed.append(_cast_place(conv_shard, 1, place, F32, "place_conv_w"))
    full = _gather_weights(placed, axes + [1], [True] * len(BIG) + [False])
    W = dict(zip(BIG, full[:-1]))
    conv_full = full[-1]

    h1 = _rmsnorm(x, gains["norm_mix"], "norm_mix_fwd")
    proj = _matmul(h1, W["w_in"], name="in_proj")
    o_a, o_a32 = _sb_fwd(proj)
    y_b = _conv_fwd(proj, conv_full)
    br_a = _matmul(o_a, W["w_branch_a"], name="branch_a")
    br_b = _matmul(y_b, W["w_branch_b"], name="branch_b")
    ga_blk, gb_blk = 3, 4

    def merge(ga, gb, a, b):
        return jax.nn.sigmoid(ga.astype(F32)) * a.astype(F32) + jax.nn.sigmoid(gb.astype(F32)) * b.astype(F32)

    merged = _rowwise(merge, [(proj, d, ga_blk), (proj, d, gb_blk), (br_a, d, 0), (br_b, d, 0)], [(d, BF)],
                      rows=s, tm=512, name="merge_fwd")[0]
    x1 = _matmul(merged, W["w_mix_out"], out_dtype=F32, resid=x, name="mix_out")

    hq = _rmsnorm(x1, gains["norm_mem_q"], "norm_mem_q_fwd")
    mn = _rmsnorm(mem, gains["norm_mem_kv"], "norm_mem_kv_fwd")
    q_m = _matmul(hq, W["w_mem_q"], name="mem_q")
    kv = _matmul(mn, W["w_mem_kv"], name="mem_kv")
    o_m = _mem_fwd(q_m, kv)
    x2 = _matmul(o_m, W["w_mem_o"], out_dtype=F32, resid=x1, name="mem_o")

    hf = _rmsnorm(x2, gains["norm_ffn"], "norm_ffn_fwd")
    gu = _matmul(hf, W["w_ffn_in"], name="ffn_in")
    f = gu.shape[1] // 2

    def swiglu(gate, up):
        gate = gate.astype(F32)
        return gate * jax.nn.sigmoid(gate) * up.astype(F32)

    act = _rowwise(swiglu, [(gu, f, 0), (gu, f, 1)], [(f, BF)], rows=s, tm=512, name="swiglu_fwd")[0]
    x3 = _matmul(act, W["w_ffn_out"], out_dtype=F32, resid=x2, name="ffn_out")

    dx3, dg_final, loss_part = _loss_head(x3, gains["norm_final"], target, "loss_head")

    dact = _matmul(dx3, W["w_ffn_out"], tb=True, name="d_act")
    gw = {"w_ffn_out": _matmul(act, dx3, ta=True, tm=1408, tn=1024, tk=512, out_dtype=F32, name="gw_ffn_out")}

    def swiglu_bwd(gate, up, da):
        gate, up, da = gate.astype(F32), up.astype(F32), da.astype(F32)
        sg = jax.nn.sigmoid(gate)
        dgate = da * up * (sg * (1.0 + gate * (1.0 - sg)))
        return jnp.concatenate([dgate, da * (gate * sg)], axis=-1)

    dgu = _rowwise(swiglu_bwd, [(gu, f, 0), (gu, f, 1), (dact, f, 0)], [(2 * f, BF)], rows=s, tm=256,
                   name="swiglu_bwd")[0]
    dhf = _matmul(dgu, W["w_ffn_in"], tb=True, tk=1408, out_dtype=F32, name="d_hf")
    gw["w_ffn_in"] = _matmul(hf, dgu, ta=True, tm=1024, tn=1408, tk=512, out_dtype=F32, name="gw_ffn_in")
    dx2, dg_ffn = _rmsnorm_bwd(x2, gains["norm_ffn"], dhf, dx3, "norm_ffn_bwd")

    do_m = _matmul(dx2, W["w_mem_o"], tb=True, name="d_o_m")
    gw["w_mem_o"] = _matmul(o_m, dx2, ta=True, tk=512, tn=1024, out_dtype=F32, name="gw_mem_o")
    dq_m, dk_m, dv_m = _mem_bwd(q_m, kv, do_m)
    dkv = jnp.concatenate([dk_m, dv_m], axis=-1)
    dhq = _matmul(dq_m, W["w_mem_q"], tb=True, out_dtype=F32, name="d_hq")
    gw["w_mem_q"] = _matmul(hq, dq_m, ta=True, tk=512, tn=1024, out_dtype=F32, name="gw_mem_q")
    dmn = _matmul(dkv, W["w_mem_kv"], tb=True, out_dtype=F32, name="d_mn")
    gw["w_mem_kv"] = _matmul(mn, dkv, ta=True, tn=1024, out_dtype=F32, name="gw_mem_kv")
    _, dg_kv = _rmsnorm_bwd(mem, gains["norm_mem_kv"], dmn, None, "norm_mem_kv_bwd")
    dx1, dg_q = _rmsnorm_bwd(x1, gains["norm_mem_q"], dhq, dx2, "norm_mem_q_bwd")

    dmerged = _matmul(dx1, W["w_mix_out"], tb=True, name="d_merged")
    gw["w_mix_out"] = _matmul(merged, dx1, ta=True, tk=512, tn=1024, out_dtype=F32, name="gw_mix_out")

    def merge_bwd(ga, gb, a, b, dm):
        dm = dm.astype(F32)
        sa, sb = jax.nn.sigmoid(ga.astype(F32)), jax.nn.sigmoid(gb.astype(F32))
        a, b = a.astype(F32), b.astype(F32)
        return dm * sa, dm * sb, dm * a * (sa * (1.0 - sa)), dm * b * (sb * (1.0 - sb))

    dbr_a, dbr_b, dga, dgb = _rowwise(
        merge_bwd, [(proj, d, ga_blk), (proj, d, gb_blk), (br_a, d, 0), (br_b, d, 0), (dmerged, d, 0)],
        [(d, BF)] * 4, rows=s, tm=256, name="merge_bwd")
    do_a = _matmul(dbr_a, W["w_branch_a"], tb=True, name="d_o_a")
    gw["w_branch_a"] = _matmul(o_a, dbr_a, ta=True, tk=512, tn=1024, out_dtype=F32, name="gw_branch_a")
    dy_b = _matmul(dbr_b, W["w_branch_b"], tb=True, name="d_y_b")
    gw["w_branch_b"] = _matmul(y_b, dbr_b, ta=True, tk=512, tn=1024, out_dtype=F32, name="gw_branch_b")
    du, dgate_b, dgate_c, dconv = _conv_bwd(proj, conv_full, dy_b)
    dq, dk, dv = _sb_bwd(proj, o_a32, do_a)

    def assemble(*parts):
        return jnp.concatenate([p.astype(BF) for p in parts], axis=-1)

    hw = dq.shape[1]
    dproj = _rowwise(assemble, [(t, hw, 0) for t in (dq, dk, dv, du, dgate_b, dgate_c)] + [(dga, d, 0), (dgb, d, 0)],
                     [(proj.shape[1], BF)], rows=s, tm=256, name="assemble_dproj")[0]
    dh1 = _matmul(dproj, W["w_in"], tb=True, tk=1280, out_dtype=F32, name="d_h1")
    gw["w_in"] = _matmul(h1, dproj, ta=True, tk=512, tn=1280, out_dtype=F32, name="gw_in")
    grad_x, dg_mix = _rmsnorm_bwd(x, gains["norm_mix"], dh1, dx1, "norm_mix_bwd")

    views = []
    for n in BIG:
        r, cdim = gw[n].shape
        views.append(gw[n].reshape(1, 2, r // 2, cdim) if BIG_AXIS[n] == 1
                     else gw[n].reshape(N_CHIPS, 2, r // (2 * N_CHIPS), cdim))
    got = _pair_exchange(views)
    sums = [_pair_add(v, g, core, "pair_add_" + n) for n, v, g in zip(BIG, views, got)]
    slots = _chip_exchange(sums, axes)
    halves = [_chip_sum(sm, sl, BIG_AXIS[n], place, "chip_sum_" + n) for n, sm, sl in zip(BIG, sums, slots)]
    both = _half_swap(halves)
    grads = {n: b.reshape(wts[n].shape[1:]) for n, b in zip(BIG, both)}

    small_g = [dg_mix, dg_q, dg_kv, dg_ffn, dg_final]
    pack = _pack_small(small_g, dconv[:CONV_K])
    pack = pack.at[ROW_LOSS].set(jnp.broadcast_to(loss_part[0, :1], (d,)))
    red = _allreduce_small(pack)
    loss = red[ROW_LOSS, 0]
    cw = conv_w.shape[2]
    conv_g = lax.dynamic_slice(red, (ROW_CONV, chip * cw), (CONV_K, cw))
    small_grad = _pack_small([red[i] for i in range(len(NORMS))], conv_g)
    small = [_pack_small([t[n] for n in NORMS], t["conv_w"][0]) for t in (wts, mom, var)]
    s_delta, s_m, s_v = _adamw(small[0], small_grad, small[1], small[2], "adamw_small")

    out = {"grad": {}, "delta": {}, "new_m": {}, "new_v": {}}
    for n in BIG:
        shp = wts[n].shape
        dl, m2, v2 = _adamw(wts[n][0], grads[n], mom[n][0], var[n][0], "adamw_" + n)
        out["grad"][n] = grads[n].reshape(shp)
        out["delta"][n], out["new_m"][n], out["new_v"][n] = dl.reshape(shp), m2.reshape(shp), v2.reshape(shp)
    for key, blk in (("grad", small_grad), ("delta", s_delta), ("new_m", s_m), ("new_v", s_v)):
        for i, n in enumerate(NORMS):
            out[key][n] = blk[i].reshape(wts[n].shape)
        out[key]["conv_w"] = blk[ROW_CONV:ROW_CONV + CONV_K, :cw].reshape(conv_w.shape)

    return (loss, grad_x[None], *[out["grad"][n] for n in ORDER], *[out["delta"][n] for n in ORDER],
            *[out["new_m"][n] for n in ORDER], *[out["new_v"][n] for n in ORDER])
```

```python
import functools
import math

import jax
import jax.numpy as jnp
from jax import lax
from jax.experimental import pallas as pl
from jax.experimental.pallas import tpu as pltpu

BF = jnp.bfloat16
F32 = jnp.float32
MESH = pl.DeviceIdType.MESH

SB_HEAD_DIM = 64
LANES = 128
MEM_HEADS = 4
CONV_K = 3
CONV_ROWS = 8
EPS = 1e-6
N_CHIPS = 4
N_DEV = 8
VMEM_LIMIT = 56 * 1024 * 1024

ADAM_LR = 0.001
ADAM_B1 = 0.9
ADAM_B2 = 0.999
ADAM_EPS = 1e-08
ADAM_WD = 0.01
ADAM_STEP = 10

SMALL_ROWS = 16
ROW_CONV = 5
ROW_LOSS = 8


def _params(sem=None):
    return pltpu.CompilerParams(dimension_semantics=sem, vmem_limit_bytes=VMEM_LIMIT)


def _tile(dim, pref):
    if dim <= pref:
        return dim
    for step in (LANES, 8):
        t = (pref // step) * step
        while t >= step:
            if dim % t == 0:
                return t
            t -= step
    raise ValueError(f"no tile of {dim} under {pref}")


def _matmul(a, b, *, ta=False, tb=False, tm=1024, tn=512, tk=None, out_dtype=BF, resid=None, after=(), name):
    if ta:
        kdim, m = a.shape
    else:
        m, kdim = a.shape
    n = b.shape[0] if tb else b.shape[1]
    tm, tn = _tile(m, tm), _tile(n, tn)
    tk = _tile(kdim, tk or kdim)
    nk = kdim // tk
    a_spec = pl.BlockSpec((tk, tm), lambda i, j, k: (k, i)) if ta else pl.BlockSpec((tm, tk), lambda i, j, k: (i, k))
    b_spec = pl.BlockSpec((tn, tk), lambda i, j, k: (j, k)) if tb else pl.BlockSpec((tk, tn), lambda i, j, k: (k, j))
    o_spec = pl.BlockSpec((tm, tn), lambda i, j, k: (i, j))
    dims = (((0 if ta else 1,), (1 if tb else 0,)), ((), ()))
    has_res = resid is not None

    def body(*refs):
        a_ref, b_ref = refs[0], refs[1]
        o_ref = refs[2 + has_res + len(after)]
        av, bv = a_ref[...], b_ref[...]
        if av.dtype != BF:
            av = av.astype(BF)
        if bv.dtype != BF:
            bv = bv.astype(BF)
        p = lax.dot_general(av, bv, dims, preferred_element_type=F32)

        def finish(acc):
            if has_res:
                acc = refs[2][...] + acc
            o_ref[...] = acc.astype(o_ref.dtype)

        if nk == 1:
            finish(p)
        else:
            acc_ref = refs[-1]
            k = pl.program_id(2)

            @pl.when(k == 0)
            def _():
                acc_ref[...] = p

            @pl.when(k > 0)
            def _():
                acc_ref[...] += p

            @pl.when(k == nk - 1)
            def _():
                finish(acc_ref[...])

    return pl.pallas_call(
        body, name=name, grid=(m // tm, n // tn, nk),
        in_specs=[a_spec, b_spec] + ([o_spec] if has_res else []) + [HBM_SPEC] * len(after),
        out_specs=o_spec, out_shape=jax.ShapeDtypeStruct((m, n), out_dtype),
        scratch_shapes=[pltpu.VMEM((tm, tn), F32)] if nk > 1 else [],
        compiler_params=_params(("parallel", "parallel", "arbitrary")),
    )(*([a, b] + ([resid] if has_res else []) + list(after)))


def _rowwise(fn, ins, outs, *, rows, tm, name, accs=(), after=()):
    tm = _tile(rows, tm)
    in_specs, args = [], []
    for arr, cols, cb in ins:
        if cols is None:
            in_specs.append(pl.BlockSpec(arr.shape, lambda i, nd=arr.ndim: (0,) * nd))
        else:
            in_specs.append(pl.BlockSpec((tm, cols), lambda i, cb=cb: (i, cb)))
        args.append(arr)
    out_specs = [pl.BlockSpec((tm, cols), lambda i: (i, 0)) for cols, _ in outs]
    out_shape = [jax.ShapeDtypeStruct((rows, cols), dt) for cols, dt in outs]
    for r, c in accs:
        out_specs.append(pl.BlockSpec((r, c), lambda i: (0, 0)))
        out_shape.append(jax.ShapeDtypeStruct((r, c), F32))
    n_in, n_out = len(ins), len(outs)
    in_specs += [HBM_SPEC] * len(after)
    args += list(after)

    def body(*refs):
        res = fn(*[r[...] for r in refs[:n_in]])
        if not isinstance(res, (tuple, list)):
            res = (res,)
        refs = refs[n_in + len(after):]
        for o_ref, val in zip(refs[:n_out], res[:n_out]):
            o_ref[...] = val.astype(o_ref.dtype)
        first = pl.program_id(0) == 0
        for a_ref, val in zip(refs[n_out:], res[n_out:]):
            @pl.when(first)
            def _(a_ref=a_ref, val=val):
                a_ref[...] = val

            @pl.when(jnp.logical_not(first))
            def _(a_ref=a_ref, val=val):
                a_ref[...] += val

    res = pl.pallas_call(
        body, name=name, grid=(rows // tm,), in_specs=in_specs, out_specs=out_specs, out_shape=out_shape,
        compiler_params=_params(("arbitrary",) if accs else ("parallel",)),
    )(*args)
    return res


def _rstd(xf):
    return lax.rsqrt(jnp.mean(xf * xf, axis=-1, keepdims=True) + EPS)


def _rmsnorm(x, g, name, after=()):
    rows, d = x.shape
    return _rowwise(lambda xv, gv: xv * _rstd(xv) * gv, [(x, d, 0), (g, None, None)], [(d, BF)],
                    rows=rows, tm=512, name=name, after=after)[0]


def _rmsnorm_bwd(x, g, dy, resid, name):
    rows, d = x.shape

    def fn(xv, gv, dyv, *rest):
        dyv = dyv.astype(F32)
        r = _rstd(xv)
        xn = xv * r
        dxn = dyv * gv
        dx = r * (dxn - xn * jnp.mean(dxn * xn, axis=-1, keepdims=True))
        if rest:
            dx = rest[0] + dx
        return dx, jnp.sum(dyv * xn, axis=0, keepdims=True)

    ins = [(x, d, 0), (g, None, None), (dy, d, 0)] + ([(resid, d, 0)] if resid is not None else [])
    outs = [(d, F32)]
    return _rowwise(fn, ins, outs, rows=rows, tm=512, name=name, accs=[(1, d)])


def _loss_head(x, g, target, name):
    rows, d = x.shape

    def fn(xv, gv, tv):
        r = _rstd(xv)
        xn = xv * r
        err = xn * gv - tv
        per_tok = jnp.mean(err * err, axis=-1, keepdims=True)
        loss = 0.5 * jnp.sum(per_tok, axis=0, keepdims=True)
        dyv = err * (1.0 / d)
        dxn = dyv * gv
        dx = r * (dxn - xn * jnp.mean(dxn * xn, axis=-1, keepdims=True))
        return dx, jnp.sum(dyv * xn, axis=0, keepdims=True), jnp.broadcast_to(loss, (1, LANES))

    return _rowwise(fn, [(x, d, 0), (g, None, None), (target, d, 0)], [(d, F32)],
                    rows=rows, tm=512, name=name, accs=[(1, d), (1, LANES)])


SB_TK = 128
SB_KT = 2


def _sb_consts(tq):
    tk = SB_TK
    diff = lax.broadcasted_iota(jnp.int32, (tq, tk), 1) - lax.broadcasted_iota(jnp.int32, (tq, tk), 0)
    rj = lax.broadcasted_iota(jnp.int32, (2 * tk, 2 * tk), 0) & (tk - 1)
    cj = lax.broadcasted_iota(jnp.int32, (2 * tk, 2 * tk), 1)
    ones_half = cj >= tk
    later = jnp.where((rj > cj) | ones_half, 1.0, 0.0).astype(BF)
    later_incl = jnp.where((rj >= cj) | ones_half, 1.0, 0.0).astype(BF)
    return diff, later, later_incl


def _split_dot(val, rhs_twice):
    hi = val.astype(BF)
    lo = (val - hi.astype(F32)).astype(BF)
    return jnp.dot(jnp.concatenate([hi, lo], axis=1), rhs_twice, preferred_element_type=F32)


def _log_terms(z):
    sp = jnp.maximum(z, 0.0) + jnp.log(1.0 + jnp.exp(-jnp.abs(z)))
    return z - sp, sp


NT = (((1,), (1,)), ((), ()))
TN = (((0,), (0,)), ((), ()))


def _head_lane_masks(rows):
    lane = lax.broadcasted_iota(jnp.int32, (rows, LANES), 1)
    first = jnp.where(lane < SB_HEAD_DIM, 1.0, 0.0)
    return first.astype(BF), (1.0 - first).astype(BF)


def _both_heads(tile, masks):
    return jnp.concatenate([tile * masks[0], tile * masks[1]], axis=0)


def _sb_fwd(proj):
    s = proj.shape[0]
    tk, tq = SB_TK, SB_KT * SB_TK
    n_pairs = 4
    scale = 1.0 / math.sqrt(SB_HEAD_DIM)

    def body(q_ref, k_ref, v_ref, o_ref, o32_ref):
        i = pl.program_id(1)
        diff, later, _ = _sb_consts(tq)
        qs = (q_ref[...].astype(F32) * scale).astype(BF)
        lane_masks = _head_lane_masks(tk)

        def step(g, state, masked):
            tiles = list(reversed(range(SB_KT)))
            chains = [(t, h) for t in tiles for h in range(2)]
            rows = {t: pl.ds(pl.multiple_of((g * SB_KT + t) * tk, tk), tk) for t in tiles}
            ks = {t: _both_heads(k_ref[rows[t], :], lane_masks) for t in tiles}
            vs = {t: _both_heads(v_ref[rows[t], :], lane_masks) for t in tiles}
            allowed = {t: diff < -t * tk for t in tiles}
            zs = {t: lax.dot_general(qs, ks[t], NT, preferred_element_type=F32) for t in tiles}
            logs = {}
            for t, h in chains:
                log_b, sp = _log_terms(zs[t][:, h * tk:(h + 1) * tk])
                logs[t, h] = (log_b, jnp.where(allowed[t], sp, 0.0) if masked else sp)
            sums = {c: _split_dot(logs[c][1], later) for c in chains}
            carries = list(state[0])
            ws = {}
            for t, h in chains:
                w = jnp.exp(logs[t, h][0] - (sums[t, h][:, :tk] + carries[h]))
                ws[t, h] = (jnp.where(allowed[t], w, 0.0) if masked else w).astype(BF)
                carries[h] = carries[h] + sums[t, h][:, tk:]
            acc = state[1]
            for t in tiles:
                acc = acc + jnp.dot(jnp.concatenate([ws[t, 0], ws[t, 1]], axis=1), vs[t],
                                    preferred_element_type=F32)
            return tuple(carries), acc

        zero = jnp.zeros((tq, LANES), F32)
        state = step(i, ((zero, zero), zero), True)
        state = lax.fori_loop(0, i, lambda gg, st: step(i - 1 - gg, st, False), state)
        o_ref[...] = state[1].astype(o_ref.dtype)
        o32_ref[...] = state[1]

    tok = pl.BlockSpec((tq, LANES), lambda p, i: (i, p))
    return pl.pallas_call(
        body, name="sb_attn_fwd", grid=(n_pairs, s // tq),
        in_specs=[tok,
                  pl.BlockSpec((s, LANES), lambda p, i: (0, n_pairs + p)),
                  pl.BlockSpec((s, LANES), lambda p, i: (0, 2 * n_pairs + p))],
        out_specs=[tok, tok],
        out_shape=[jax.ShapeDtypeStruct((s, n_pairs * LANES), BF), jax.ShapeDtypeStruct((s, n_pairs * LANES), F32)],
        compiler_params=_params(("parallel", "arbitrary")),
    )(proj, proj, proj)


def _sb_bwd(proj, o32, do_a, after=()):
    s = proj.shape[0]
    tk, tq = SB_TK, SB_KT * SB_TK
    n_pairs = 4
    scale = 1.0 / math.sqrt(SB_HEAD_DIM)

    def body(q_ref, k_ref, v_ref, o_ref, do_ref, *rest):
        dq_ref, dk_ref, dv_ref = rest[len(after):]
        i = pl.program_id(1)

        @pl.when(i == 0)
        def _():
            dk_ref[...] = jnp.zeros_like(dk_ref)
            dv_ref[...] = jnp.zeros_like(dv_ref)

        diff, later, later_incl = _sb_consts(tq)
        qs = (q_ref[...].astype(F32) * scale).astype(BF)
        do2 = do_ref[...]
        prod = do2.astype(F32) * o_ref[...]
        lane_masks = _head_lane_masks(tk)
        first_head = lax.broadcasted_iota(jnp.int32, (tq, LANES), 1) < SB_HEAD_DIM
        totals = [jnp.broadcast_to(jnp.sum(jnp.where(keep, prod, 0.0), axis=-1, keepdims=True), (tq, tk))
                  for keep in (first_head, jnp.logical_not(first_head))]
        first_head_k = first_head[:tk]

        def step(g_idx, state, masked):
            tiles = list(reversed(range(SB_KT)))
            chains = [(t, h) for t in tiles for h in range(2)]
            rows = {t: pl.ds(pl.multiple_of((g_idx * SB_KT + t) * tk, tk), tk) for t in tiles}
            ks = {t: _both_heads(k_ref[rows[t], :], lane_masks) for t in tiles}
            vs = {t: _both_heads(v_ref[rows[t], :], lane_masks) for t in tiles}
            allowed = {t: diff < -t * tk for t in tiles}
            zs = {t: lax.dot_general(qs, ks[t], NT, preferred_element_type=F32) for t in tiles}
            dws = {t: lax.dot_general(do2, vs[t], NT, preferred_element_type=F32) for t in tiles}
            logs = {}
            for t, h in chains:
                log_b, sp = _log_terms(zs[t][:, h * tk:(h + 1) * tk])
                logs[t, h] = (log_b, jnp.where(allowed[t], sp, 0.0) if masked else sp)
            sums = {c: _split_dot(logs[c][1], later) for c in chains}
            c_log, c_g = list(state[0]), list(state[1])
            ws, gs = {}, {}
            for t, h in chains:
                w = jnp.exp(logs[t, h][0] - (sums[t, h][:, :tk] + c_log[h]))
                ws[t, h] = (jnp.where(allowed[t], w, 0.0) if masked else w).astype(BF)
                c_log[h] = c_log[h] + sums[t, h][:, tk:]
                gs[t, h] = ws[t, h].astype(F32) * dws[t][:, h * tk:(h + 1) * tk]
            gsums = {c: _split_dot(gs[c], later_incl) for c in chains}
            dzs = {}
            for t, h in chains:
                beta = jnp.exp(logs[t, h][0])
                earlier = totals[h] - (gsums[t, h][:, :tk] + c_g[h])
                dz = gs[t, h] * (1.0 - beta) - earlier * beta
                dzs[t, h] = (jnp.where(allowed[t], dz, 0.0) if masked else dz).astype(BF)
                c_g[h] = c_g[h] + gsums[t, h][:, tk:]
            dq = state[2]
            for t in tiles:
                dz_both = jnp.concatenate([dzs[t, 0], dzs[t, 1]], axis=1)
                w_both = jnp.concatenate([ws[t, 0], ws[t, 1]], axis=1)
                dq = dq + jnp.dot(dz_both, ks[t], preferred_element_type=F32)
                dk2 = lax.dot_general(dz_both, qs, TN, preferred_element_type=F32)
                dv2 = lax.dot_general(w_both, do2, TN, preferred_element_type=F32)
                dk_ref[rows[t], :] += jnp.where(first_head_k, dk2[:tk], dk2[tk:])
                dv_ref[rows[t], :] += jnp.where(first_head_k, dv2[:tk], dv2[tk:])
            return tuple(c_log), tuple(c_g), dq

        zero = jnp.zeros((tq, LANES), F32)
        state = step(i, ((zero, zero), (zero, zero), zero), True)
        state = lax.fori_loop(0, i, lambda gg, st: step(i - 1 - gg, st, False), state)
        dq_ref[...] = (state[2] * scale).astype(dq_ref.dtype)

    width = n_pairs * LANES
    return pl.pallas_call(
        body, name="sb_attn_bwd", grid=(n_pairs, s // tq),
        in_specs=[pl.BlockSpec((tq, LANES), lambda p, i: (i, p)),
                  pl.BlockSpec((s, LANES), lambda p, i: (0, n_pairs + p)),
                  pl.BlockSpec((s, LANES), lambda p, i: (0, 2 * n_pairs + p)),
                  pl.BlockSpec((tq, LANES), lambda p, i: (i, p)),
                  pl.BlockSpec((tq, LANES), lambda p, i: (i, p))] + [HBM_SPEC] * len(after),
        out_specs=[pl.BlockSpec((tq, LANES), lambda p, i: (i, p)),
                   pl.BlockSpec((s, LANES), lambda p, i: (0, p)),
                   pl.BlockSpec((s, LANES), lambda p, i: (0, p))],
        out_shape=[jax.ShapeDtypeStruct((s, width), BF), jax.ShapeDtypeStruct((s, width), F32),
                   jax.ShapeDtypeStruct((s, width), F32)],
        compiler_params=_params(("parallel", "arbitrary")),
    )(proj, proj, proj, o32, do_a, *after)


CONV_COL0 = 12


def _shift_rows(v, k):
    n = v.shape[0]
    row = lax.broadcasted_iota(jnp.int32, v.shape, 0)
    rolled = pltpu.roll(v, k % n, axis=0)
    keep = row >= k if k > 0 else row < n + k
    return jnp.where(keep, rolled, 0.0)


def _conv_specs(s):
    return [pl.BlockSpec((s, LANES), lambda cb: (0, CONV_COL0 + cb)),
            pl.BlockSpec((s, LANES), lambda cb: (0, CONV_COL0 + 4 + cb)),
            pl.BlockSpec((s, LANES), lambda cb: (0, CONV_COL0 + 8 + cb)),
            pl.BlockSpec((CONV_ROWS, LANES), lambda cb: (0, cb))]


def _conv_fwd(proj, conv_w):
    s = proj.shape[0]

    def body(u_ref, gb_ref, gc_ref, w_ref, y_ref):
        cu = gc_ref[...].astype(F32) * u_ref[...].astype(F32)
        w = w_ref[...]
        y = w[0:1] * _shift_rows(cu, 2) + w[1:2] * _shift_rows(cu, 1) + w[2:3] * cu
        y_ref[...] = (gb_ref[...].astype(F32) * y).astype(y_ref.dtype)

    return pl.pallas_call(
        body, name="conv_fwd", grid=(4,), in_specs=_conv_specs(s),
        out_specs=pl.BlockSpec((s, LANES), lambda cb: (0, cb)),
        out_shape=jax.ShapeDtypeStruct((s, 4 * LANES), BF),
        compiler_params=_params(("parallel",)),
    )(proj, proj, proj, conv_w)


def _conv_bwd(proj, conv_w, dy):
    s = proj.shape[0]

    def body(u_ref, gb_ref, gc_ref, w_ref, dy_ref, du_ref, dgb_ref, dgc_ref, dw_ref):
        u, gc = u_ref[...].astype(F32), gc_ref[...].astype(F32)
        dyv = dy_ref[...].astype(F32)
        w = w_ref[...]
        cu = gc * u
        cu1, cu2 = _shift_rows(cu, 1), _shift_rows(cu, 2)
        conv = w[0:1] * cu2 + w[1:2] * cu1 + w[2:3] * cu
        dgb_ref[...] = (dyv * conv).astype(dgb_ref.dtype)
        dc = dyv * gb_ref[...].astype(F32)
        dcu = w[2:3] * dc + w[1:2] * _shift_rows(dc, -1) + w[0:1] * _shift_rows(dc, -2)
        dgc_ref[...] = (dcu * u).astype(dgc_ref.dtype)
        du_ref[...] = (dcu * gc).astype(du_ref.dtype)
        tap_row = lax.broadcasted_iota(jnp.int32, (CONV_ROWS, LANES), 0)
        dw = jnp.zeros((CONV_ROWS, LANES), F32)
        for t, shifted in enumerate((cu2, cu1, cu)):
            dw = jnp.where(tap_row == t, jnp.sum(dc * shifted, axis=0, keepdims=True), dw)
        dw_ref[...] = dw

    col = pl.BlockSpec((s, LANES), lambda cb: (0, cb))
    act = jax.ShapeDtypeStruct((s, 4 * LANES), BF)
    return pl.pallas_call(
        body, name="conv_bwd", grid=(4,), in_specs=_conv_specs(s) + [col],
        out_specs=[col, col, col, pl.BlockSpec((CONV_ROWS, LANES), lambda cb: (0, cb))],
        out_shape=[act, act, act, jax.ShapeDtypeStruct((CONV_ROWS, 4 * LANES), F32)],
        compiler_params=_params(("parallel",)),
    )(proj, proj, proj, conv_w, dy)


def _mem_probs(q, k, scale):
    sc = lax.dot_general(q, k, NT, preferred_element_type=F32) * scale
    p = jnp.exp(sc - jnp.max(sc, axis=-1, keepdims=True))
    return p / jnp.sum(p, axis=-1, keepdims=True)


def _mem_fwd(q_m, kv, tq=512):
    s, d = q_m.shape
    mlen = kv.shape[0]
    hd = d // MEM_HEADS
    tq = _tile(s, tq)
    scale = 1.0 / math.sqrt(hd)

    def body(q_ref, k_ref, v_ref, o_ref):
        p = _mem_probs(q_ref[...], k_ref[...], scale)
        o_ref[...] = jnp.dot(p.astype(BF), v_ref[...], preferred_element_type=F32).astype(o_ref.dtype)

    return pl.pallas_call(
        body, name="mem_attn_fwd", grid=(MEM_HEADS, s // tq),
        in_specs=[pl.BlockSpec((tq, hd), lambda h, i: (i, h)),
                  pl.BlockSpec((mlen, hd), lambda h, i: (0, h)),
                  pl.BlockSpec((mlen, hd), lambda h, i: (0, MEM_HEADS + h))],
        out_specs=pl.BlockSpec((tq, hd), lambda h, i: (i, h)),
        out_shape=jax.ShapeDtypeStruct((s, d), BF),
        compiler_params=_params(("parallel", "parallel")),
    )(q_m, kv, kv)


def _mem_bwd(q_m, kv, do_m, tq=512):
    s, d = q_m.shape
    mlen = kv.shape[0]
    hd = d // MEM_HEADS
    tq = _tile(s, tq)
    scale = 1.0 / math.sqrt(hd)

    def body(q_ref, k_ref, v_ref, do_ref, dq_ref, dk_ref, dv_ref):
        q, k, v, do = q_ref[...], k_ref[...], v_ref[...], do_ref[...]
        p = _mem_probs(q, k, scale)
        dp = lax.dot_general(do, v, NT, preferred_element_type=F32)
        ds = p * (dp - jnp.sum(dp * p, axis=-1, keepdims=True)) * scale
        dsb = ds.astype(BF)
        dq_ref[...] = jnp.dot(dsb, k, preferred_element_type=F32).astype(dq_ref.dtype)
        dk = lax.dot_general(dsb, q, TN, preferred_element_type=F32)
        dv = lax.dot_general(p.astype(BF), do, TN, preferred_element_type=F32)
        first = pl.program_id(1) == 0

        @pl.when(first)
        def _():
            dk_ref[...] = dk
            dv_ref[...] = dv

        @pl.when(jnp.logical_not(first))
        def _():
            dk_ref[...] += dk
            dv_ref[...] += dv

    tok = pl.BlockSpec((tq, hd), lambda h, i: (i, h))
    memb = pl.BlockSpec((mlen, hd), lambda h, i: (0, h))
    return pl.pallas_call(
        body, name="mem_attn_bwd", grid=(MEM_HEADS, s // tq),
        in_specs=[tok, memb, pl.BlockSpec((mlen, hd), lambda h, i: (0, MEM_HEADS + h)), tok],
        out_specs=[tok, memb, memb],
        out_shape=[jax.ShapeDtypeStruct((s, d), BF), jax.ShapeDtypeStruct((mlen, d), F32),
                   jax.ShapeDtypeStruct((mlen, d), F32)],
        compiler_params=_params(("parallel", "arbitrary")),
    )(q_m, kv, kv, do_m)


def _place():
    x, y, c = lax.axis_index("x"), lax.axis_index("y"), lax.axis_index("c")
    other_chips = [(1 - x, y), (x, 1 - y), (1 - x, 1 - y)]
    return x, y, c, other_chips


def _chip_no(cx, cy):
    return 2 * cx + cy


HBM_SPEC = pl.BlockSpec(memory_space=pl.ANY)


def _cast_place(shard, axis, place, dtype, name):
    r, c = shard.shape
    tr = _tile(r, max(16, 1048576 // c))
    nblk = r // tr
    if axis == 1:
        full, out_map = (r, N_CHIPS * c), lambda i, pref: (i, pref[0])
    else:
        full, out_map = (N_CHIPS * r, c), lambda i, pref: (pref[0] * nblk + i, 0)

    def body(pref, s_ref, o_ref):
        o_ref[...] = s_ref[...].astype(o_ref.dtype)

    return pl.pallas_call(
        body, name=name,
        grid_spec=pltpu.PrefetchScalarGridSpec(
            num_scalar_prefetch=1, grid=(nblk,),
            in_specs=[pl.BlockSpec((tr, c), lambda i, pref: (i, 0))],
            out_specs=pl.BlockSpec((tr, c), out_map)),
        out_shape=jax.ShapeDtypeStruct(full, dtype),
        compiler_params=_params(("parallel",)),
    )(place, shard)


def _region(ref, axis, chip_no, half):
    width = ref.shape[axis] // N_CHIPS
    start = pl.multiple_of(chip_no * width, width)
    if axis == 1:
        if half is None:
            return ref.at[:, pl.ds(start, width)]
        hr = ref.shape[0] // 2
        return ref.at[pl.ds(pl.multiple_of(half * hr, hr), hr), pl.ds(start, width)]
    if half is None:
        return ref.at[pl.ds(start, width), :]
    hr = width // 2
    return ref.at[pl.ds(pl.multiple_of(start + half * hr, hr), hr), :]


def _gather_weights(fulls, axes, split):
    n = len(fulls)

    def body(*refs):
        outs = refs[n:2 * n]
        send, recv, fsend, frecv = refs[2 * n:]
        x, y, c, others = _place()
        me = _chip_no(x, y)
        sibling = (x, y, 1 - c)

        def copy(w, chip_no, half, sems, p, to):
            reg = _region(outs[w], axes[w], chip_no, half)
            return pltpu.make_async_remote_copy(
                src_ref=reg, dst_ref=reg, send_sem=sems[0].at[w, p], recv_sem=sems[1].at[w, p],
                device_id=to, device_id_type=MESH)

        for w in range(n):
            for p, chip in enumerate(others):
                copy(w, me, c if split[w] else None, (send, recv), p, (chip[0], chip[1], c)).start()
        for w in range(n):
            for p, chip in enumerate(others):
                half = c if split[w] else None
                copy(w, _chip_no(*chip), half, (send, recv), p, (chip[0], chip[1], c)).wait_recv()
                if split[w]:
                    copy(w, _chip_no(*chip), c, (fsend, frecv), p, sibling).start()
        for w in range(n):
            for p, chip in enumerate(others):
                copy(w, me, c if split[w] else None, (send, recv), p, (chip[0], chip[1], c)).wait_send()
                if split[w]:
                    handed = copy(w, _chip_no(*chip), 1 - c, (fsend, frecv), p, sibling)
                    handed.wait_recv()
                    handed.wait_send()

    return pl.pallas_call(
        body, name="gather_weights",
        in_specs=[HBM_SPEC] * n, out_specs=[HBM_SPEC] * n,
        out_shape=[jax.ShapeDtypeStruct(f.shape, f.dtype) for f in fulls],
        input_output_aliases={i: i for i in range(n)},
        scratch_shapes=[pltpu.SemaphoreType.DMA((n, 3))] * 4,
        compiler_params=pltpu.CompilerParams(has_side_effects=True),
    )(*fulls)


SEM_SPEC = pl.BlockSpec(memory_space=pltpu.SEMAPHORE)
IN_HBM = pl.BlockSpec(memory_space=pltpu.HBM)
FLOWS = pltpu.CompilerParams(has_side_effects=pltpu.SideEffectType.DATAFLOW_SIDE_EFFECTING)
TOKEN = jax.ShapeDtypeStruct((8, LANES), F32)


def _in_hbm(arrays):
    return [pltpu.with_memory_space_constraint(a, pltpu.HBM) for a in arrays]


def _hbm_like(arrays):
    return [pltpu.HBM(a.shape, a.dtype) for a in arrays]


def _fetch_copy(refs, axes, send, recv, w, p, chip, c, arriving):
    owner = _chip_no(*chip) if arriving else _chip_no(lax.axis_index("x"), lax.axis_index("y"))
    reg = _region(refs[w], axes[w], owner, c)
    return pltpu.make_async_remote_copy(
        src_ref=reg, dst_ref=reg, send_sem=send[p], recv_sem=recv[p],
        device_id=(chip[0], chip[1], c), device_id_type=MESH)


N_PEERS = N_CHIPS - 1
PEER_SEMS = [pltpu.SemaphoreType.DMA(())] * (2 * N_PEERS)


def _fetch_start(fulls, axes, name):
    n = len(fulls)

    def body(*refs):
        ins = refs[:n]
        send, recv = refs[n:n + N_PEERS], refs[n + N_PEERS:n + 2 * N_PEERS]
        token = refs[-1]
        _, _, c, others = _place()
        for w in range(n):
            for p, chip in enumerate(others):
                _fetch_copy(ins, axes, send, recv, w, p, chip, c, False).start()
        token[...] = jnp.zeros_like(token)

    res = pl.pallas_call(
        body, name=name,
        in_specs=[IN_HBM] * n,
        out_specs=[SEM_SPEC] * len(PEER_SEMS) + [IN_HBM] * n + [pl.BlockSpec(memory_space=pltpu.VMEM)],
        out_shape=PEER_SEMS + _hbm_like(fulls) + [TOKEN],
        input_output_aliases={i: len(PEER_SEMS) + i for i in range(n)},
        compiler_params=FLOWS,
    )(*_in_hbm(fulls))
    k = len(PEER_SEMS)
    return list(res[:k]), list(res[k:k + n]), res[-1]


def _fetch_wait(sems, fulls, axes, after, name):
    n = len(fulls)

    def body(*refs):
        ins = refs[:n]
        send, recv = refs[n:n + N_PEERS], refs[n + N_PEERS:n + 2 * N_PEERS]
        _, _, c, others = _place()
        for w in range(n):
            for p, chip in enumerate(others):
                _fetch_copy(ins, axes, send, recv, w, p, chip, c, False).wait_send()
                _fetch_copy(ins, axes, send, recv, w, p, chip, c, True).wait_recv()

    res = pl.pallas_call(
        body, name=name,
        in_specs=[IN_HBM] * n + [SEM_SPEC] * len(sems) + [HBM_SPEC] * len(after),
        out_specs=[IN_HBM] * n,
        out_shape=_hbm_like(fulls),
        input_output_aliases={i: i for i in range(n)},
        compiler_params=FLOWS,
    )(*fulls, *sems, *after)
    return list(res)


def _hand_on(fulls, axes):
    n = len(fulls)

    def body(*refs):
        outs = refs[n:2 * n]
        send, recv = refs[2 * n:]
        x, y, c, others = _place()

        def copy(w, p, chip, half):
            reg = _region(outs[w], axes[w], _chip_no(*chip), half)
            return pltpu.make_async_remote_copy(
                src_ref=reg, dst_ref=reg, send_sem=send.at[w, p], recv_sem=recv.at[w, p],
                device_id=(x, y, 1 - c), device_id_type=MESH)

        for w in range(n):
            for p, chip in enumerate(others):
                copy(w, p, chip, c).start()
        for w in range(n):
            for p, chip in enumerate(others):
                copy(w, p, chip, 1 - c).wait()

    return pl.pallas_call(
        body, name="gather_hand_on",
        in_specs=[HBM_SPEC] * n, out_specs=[HBM_SPEC] * n,
        out_shape=[jax.ShapeDtypeStruct(f.shape, f.dtype) for f in fulls],
        input_output_aliases={i: i for i in range(n)},
        scratch_shapes=[pltpu.SemaphoreType.DMA((n, 3)), pltpu.SemaphoreType.DMA((n, 3))],
        compiler_params=pltpu.CompilerParams(has_side_effects=True),
    )(*fulls)


def _pair_exchange(grads, name):
    n = len(grads)

    def body(*refs):
        ins, outs = refs[:n], refs[n:2 * n]
        send, recv = refs[2 * n:]
        x, y, c, _ = _place()
        cps = []
        for w in range(n):
            cp = pltpu.make_async_remote_copy(
                src_ref=ins[w].at[:, 1 - c], dst_ref=outs[w], send_sem=send.at[w], recv_sem=recv.at[w],
                device_id=(x, y, 1 - c), device_id_type=MESH)
            cp.start()
            cps.append(cp)
        for cp in cps:
            cp.wait()

    return pl.pallas_call(
        body, name=name,
        in_specs=[HBM_SPEC] * n, out_specs=[HBM_SPEC] * n,
        out_shape=[jax.ShapeDtypeStruct((g.shape[0],) + g.shape[2:], g.dtype) for g in grads],
        scratch_shapes=[pltpu.SemaphoreType.DMA((n,)), pltpu.SemaphoreType.DMA((n,))],
        compiler_params=pltpu.CompilerParams(has_side_effects=True),
    )(*grads)


def _pair_add(g4, got, core, name):
    nj, _, hr, cdim = g4.shape
    tr = _tile(hr, max(8, 524288 // cdim))

    def body(core_ref, a_ref, b_ref, o_ref):
        o_ref[...] = (a_ref[...] + b_ref[...]).astype(o_ref.dtype)

    return pl.pallas_call(
        body, name=name,
        grid_spec=pltpu.PrefetchScalarGridSpec(
            num_scalar_prefetch=1, grid=(nj, hr // tr),
            in_specs=[pl.BlockSpec((1, None, tr, cdim), lambda j, i, core_ref: (j, core_ref[0], i, 0)),
                      pl.BlockSpec((1, tr, cdim), lambda j, i, core_ref: (j, i, 0))],
            out_specs=pl.BlockSpec((1, tr, cdim), lambda j, i, core_ref: (j, i, 0))),
        out_shape=jax.ShapeDtypeStruct((nj, hr, cdim), BF),
        compiler_params=_params(("parallel", "parallel")),
    )(core, g4, got)


def _piece(ref, axis, j, hc):
    if axis == 0:
        return ref.at[j]
    return ref.at[0, :, pl.ds(pl.multiple_of(j * hc, hc), hc)]


def _slot_shapes(sums, axes):
    return [(N_CHIPS - 1, sm.shape[1], sm.shape[2] // (1 if ax == 0 else N_CHIPS)) for sm, ax in zip(sums, axes)]


def _slot_copy(sums, lands, axes, send, recv, w, p, chip, c):
    return pltpu.make_async_remote_copy(
        src_ref=_piece(sums[w], axes[w], _chip_no(*chip), lands[w].shape[2]), dst_ref=lands[w].at[p],
        send_sem=send[p], recv_sem=recv[p],
        device_id=(chip[0], chip[1], c), device_id_type=MESH)


def _chip_exchange_start(sums, axes, name):
    n = len(sums)
    shapes = _slot_shapes(sums, axes)
    lands = [lax.empty(sh, sm.dtype) for sh, sm in zip(shapes, sums)]

    def body(*refs):
        ins, land_refs = refs[:n], refs[n:2 * n]
        send, recv = refs[2 * n:2 * n + N_PEERS], refs[2 * n + N_PEERS:2 * n + 2 * N_PEERS]
        token = refs[-1]
        _, _, c, others = _place()
        for w in range(n):
            for p, chip in enumerate(others):
                _slot_copy(ins, land_refs, axes, send, recv, w, p, chip, c).start()
        token[...] = jnp.zeros_like(token)

    k = len(PEER_SEMS)
    res = pl.pallas_call(
        body, name=name,
        in_specs=[IN_HBM] * (2 * n),
        out_specs=[SEM_SPEC] * k + [IN_HBM] * (2 * n) + [pl.BlockSpec(memory_space=pltpu.VMEM)],
        out_shape=PEER_SEMS + _hbm_like(list(sums) + lands) + [TOKEN],
        input_output_aliases={i: k + i for i in range(2 * n)},
        compiler_params=FLOWS,
    )(*_in_hbm(list(sums) + lands))
    return list(res[:k]), list(res[k:k + n]), list(res[k + n:k + 2 * n]), res[-1]


def _chip_exchange_wait(sems, sums, lands, axes, after, name):
    n = len(sums)

    def body(*refs):
        ins, land_refs = refs[:n], refs[n:2 * n]
        send, recv = refs[2 * n:2 * n + N_PEERS], refs[2 * n + N_PEERS:2 * n + 2 * N_PEERS]
        _, _, c, others = _place()
        for w in range(n):
            for p, chip in enumerate(others):
                cp = _slot_copy(ins, land_refs, axes, send, recv, w, p, chip, c)
                cp.wait_send()
                cp.wait_recv()

    res = pl.pallas_call(
        body, name=name,
        in_specs=[IN_HBM] * (2 * n) + [SEM_SPEC] * len(sems) + [HBM_SPEC] * len(after),
        out_specs=[IN_HBM] * (2 * n), out_shape=_hbm_like(list(sums) + list(lands)),
        input_output_aliases={i: i for i in range(2 * n)},
        compiler_params=FLOWS,
    )(*sums, *lands, *sems, *after)
    return list(res[:n]), list(res[n:])


def _chip_sum(psum, slots, axis, place, name):
    _, hr, hc = slots.shape
    tr = _tile(hr, 256)
    own_map = (lambda i, pref: (0, i, pref[0])) if axis == 1 else (lambda i, pref: (pref[0], i, 0))

    def body(pref, own_ref, s_ref, o_ref):
        o_ref[...] = ((own_ref[...].astype(F32) + s_ref[0].astype(F32)) + s_ref[1].astype(F32)) + s_ref[2].astype(F32)

    return pl.pallas_call(
        body, name=name,
        grid_spec=pltpu.PrefetchScalarGridSpec(
            num_scalar_prefetch=1, grid=(hr // tr,),
            in_specs=[pl.BlockSpec((None, tr, hc), own_map),
                      pl.BlockSpec((N_CHIPS - 1, tr, hc), lambda i, pref: (0, i, 0))],
            out_specs=pl.BlockSpec((None, tr, hc), lambda i, pref: (pref[1], i, 0))),
        out_shape=jax.ShapeDtypeStruct((2, hr, hc), F32),
        compiler_params=_params(("parallel",)),
    )(place, psum, slots)


def _half_swap(both):
    n = len(both)

    def body(*refs):
        outs = refs[n:2 * n]
        send, recv = refs[2 * n:]
        x, y, c, _ = _place()

        def copy(w, half):
            return pltpu.make_async_remote_copy(
                src_ref=outs[w].at[half], dst_ref=outs[w].at[half], send_sem=send.at[w], recv_sem=recv.at[w],
                device_id=(x, y, 1 - c), device_id_type=MESH)

        for w in range(n):
            copy(w, c).start()
        for w in range(n):
            copy(w, 1 - c).wait()

    return pl.pallas_call(
        body, name="grad_half_swap",
        in_specs=[HBM_SPEC] * n, out_specs=[HBM_SPEC] * n,
        out_shape=[jax.ShapeDtypeStruct(b.shape, b.dtype) for b in both],
        input_output_aliases={i: i for i in range(n)},
        scratch_shapes=[pltpu.SemaphoreType.DMA((n,)), pltpu.SemaphoreType.DMA((n,))],
        compiler_params=pltpu.CompilerParams(has_side_effects=True),
    )(*both)


def _allreduce_small(pack):
    rows, d = pack.shape

    def body(p_ref, o_ref, slots, send, recv):
        x, y, c, _ = _place()
        me = 4 * x + 2 * y + c
        slots[me] = p_ref[...]
        cps = []
        for k in range(1, N_DEV):
            px, py, pc = x ^ (k >> 2), y ^ ((k >> 1) & 1), c ^ (k & 1)
            cp = pltpu.make_async_remote_copy(
                src_ref=p_ref, dst_ref=slots.at[me], send_sem=send.at[k - 1], recv_sem=recv.at[k - 1],
                device_id=(px, py, pc), device_id_type=MESH)
            cp.start()
            cps.append(cp)
        for k in range(1, N_DEV):
            px, py, pc = x ^ (k >> 2), y ^ ((k >> 1) & 1), c ^ (k & 1)
            arrival = pltpu.make_async_remote_copy(
                src_ref=p_ref, dst_ref=slots.at[4 * px + 2 * py + pc], send_sem=send.at[k - 1],
                recv_sem=recv.at[k - 1], device_id=(px, py, pc), device_id_type=MESH)
            arrival.wait_recv()
            arrival.wait_send()
        acc = slots[0]
        for k in range(1, N_DEV):
            acc = acc + slots[k]
        o_ref[...] = acc

    vm = pl.BlockSpec(memory_space=pltpu.VMEM)
    return pl.pallas_call(
        body, name="allreduce_small", in_specs=[vm], out_specs=vm,
        out_shape=jax.ShapeDtypeStruct((rows, d), F32),
        scratch_shapes=[pltpu.VMEM((N_DEV, rows, d), F32), pltpu.SemaphoreType.DMA((N_DEV - 1,)),
                        pltpu.SemaphoreType.DMA((N_DEV - 1,))],
        compiler_params=pltpu.CompilerParams(has_side_effects=True),
    )(pack)


def _adamw(w, g, m, v, name):
    rows, cols = w.shape

    def fn(wv, gv, mv, vv):
        m2 = ADAM_B1 * mv + (1.0 - ADAM_B1) * gv
        v2 = ADAM_B2 * vv + (1.0 - ADAM_B2) * (gv * gv)
        m_hat = m2 / (1.0 - ADAM_B1 ** ADAM_STEP)
        v_hat = v2 / (1.0 - ADAM_B2 ** ADAM_STEP)
        delta = -ADAM_LR * (m_hat / (jnp.sqrt(v_hat) + ADAM_EPS) + ADAM_WD * wv)
        return delta, m2, v2

    ins = [(a, cols, 0) for a in (w, g, m, v)]
    return _rowwise(fn, ins, [(cols, F32)] * 3, rows=rows, tm=_tile(rows, max(8, 262144 // cols)), name=name)


BIG = ["w_in", "w_branch_a", "w_branch_b", "w_mix_out", "w_mem_q", "w_mem_kv", "w_mem_o", "w_ffn_in", "w_ffn_out"]
BIG_AXIS = {"w_in": 1, "w_branch_a": 1, "w_branch_b": 1, "w_mix_out": 0, "w_mem_q": 0, "w_mem_kv": 1,
            "w_mem_o": 0, "w_ffn_in": 1, "w_ffn_out": 0}
NORMS = ["norm_mix", "norm_mem_q", "norm_mem_kv", "norm_ffn", "norm_final"]
ORDER = ["norm_mix", "w_in", "conv_w", "w_branch_a", "w_branch_b", "w_mix_out", "norm_mem_q", "norm_mem_kv",
         "w_mem_q", "w_mem_kv", "w_mem_o", "norm_ffn", "w_ffn_in", "w_ffn_out", "norm_final"]


def _pack_small(vals, conv):
    d = vals[0].shape[-1]
    rows = [v.reshape(1, d) for v in vals]
    conv = jnp.pad(conv, ((0, 0), (0, d - conv.shape[1])))
    pad = jnp.zeros((SMALL_ROWS - len(rows) - CONV_K, d), F32)
    return jnp.concatenate(rows + [conv, pad], axis=0)


def kernel(x, mem, norm_mix, w_in, conv_w, w_branch_a, w_branch_b, w_mix_out, norm_mem_q, norm_mem_kv, w_mem_q, w_mem_kv, w_mem_o, norm_ffn, w_ffn_in, w_ffn_out, norm_final, loss_target, m_norm_mix, m_w_in, m_conv_w, m_w_branch_a, m_w_branch_b, m_w_mix_out, m_norm_mem_q, m_norm_mem_kv, m_w_mem_q, m_w_mem_kv, m_w_mem_o, m_norm_ffn, m_w_ffn_in, m_w_ffn_out, m_norm_final, v_norm_mix, v_w_in, v_conv_w, v_w_branch_a, v_w_branch_b, v_w_mix_out, v_norm_mem_q, v_norm_mem_kv, v_w_mem_q, v_w_mem_kv, v_w_mem_o, v_norm_ffn, v_w_ffn_in, v_w_ffn_out, v_norm_final):
    args = dict(locals())
    wts = {n: args[n] for n in ORDER}
    mom = {n: args["m_" + n] for n in ORDER}
    var = {n: args["v_" + n] for n in ORDER}
    x = x[0]
    mem = mem[0]
    target = loss_target[0]
    s, d = x.shape
    gains = {n: wts[n].reshape(1, d) for n in NORMS}
    chip = 2 * lax.axis_index("x") + lax.axis_index("y")
    core = lax.axis_index("c").astype(jnp.int32).reshape(1)
    place = jnp.stack([chip, lax.axis_index("c")]).astype(jnp.int32)

    conv_shard = jnp.pad(conv_w[0], ((0, CONV_ROWS - CONV_K), (0, 0)))
    placed = {n: _cast_place(wts[n][0], BIG_AXIS[n], place, BF, "place_" + n) for n in BIG}
    conv_placed = _cast_place(conv_shard, 1, place, F32, "place_conv_w")
    w_in_full, conv_full = _gather_weights([placed["w_in"], conv_placed], [BIG_AXIS["w_in"], 1], [True, False])
    W = {"w_in": w_in_full}
    later_w = [n for n in BIG if n != "w_in"]
    later_axes = [BIG_AXIS[n] for n in later_w]
    f_sems, later_bufs, f_token = _fetch_start([placed[n] for n in later_w], later_axes, "fetch_start")

    h1 = _rmsnorm(x, gains["norm_mix"], "norm_mix_fwd", after=(f_token,))
    proj = _matmul(h1, W["w_in"], name="in_proj")
    o_a, o_a32 = _sb_fwd(proj)
    y_b = _conv_fwd(proj, conv_full)
    later_bufs = _fetch_wait(f_sems, later_bufs, later_axes, (o_a, y_b), "fetch_wait")
    W.update(zip(later_w, _hand_on(later_bufs, later_axes)))
    br_a = _matmul(o_a, W["w_branch_a"], name="branch_a")
    br_b = _matmul(y_b, W["w_branch_b"], name="branch_b")
    ga_blk, gb_blk = 3, 4

    def merge(ga, gb, a, b):
        return jax.nn.sigmoid(ga.astype(F32)) * a.astype(F32) + jax.nn.sigmoid(gb.astype(F32)) * b.astype(F32)

    merged = _rowwise(merge, [(proj, d, ga_blk), (proj, d, gb_blk), (br_a, d, 0), (br_b, d, 0)], [(d, BF)],
                      rows=s, tm=512, name="merge_fwd")[0]
    x1 = _matmul(merged, W["w_mix_out"], out_dtype=F32, resid=x, name="mix_out")

    hq = _rmsnorm(x1, gains["norm_mem_q"], "norm_mem_q_fwd")
    mn = _rmsnorm(mem, gains["norm_mem_kv"], "norm_mem_kv_fwd")
    q_m = _matmul(hq, W["w_mem_q"], name="mem_q")
    kv = _matmul(mn, W["w_mem_kv"], name="mem_kv")
    o_m = _mem_fwd(q_m, kv)
    x2 = _matmul(o_m, W["w_mem_o"], out_dtype=F32, resid=x1, name="mem_o")

    hf = _rmsnorm(x2, gains["norm_ffn"], "norm_ffn_fwd")
    gu = _matmul(hf, W["w_ffn_in"], name="ffn_in")
    f = gu.shape[1] // 2

    def swiglu(gate, up):
        gate = gate.astype(F32)
        return gate * jax.nn.sigmoid(gate) * up.astype(F32)

    act = _rowwise(swiglu, [(gu, f, 0), (gu, f, 1)], [(f, BF)], rows=s, tm=512, name="swiglu_fwd")[0]
    x3 = _matmul(act, W["w_ffn_out"], out_dtype=F32, resid=x2, name="ffn_out")

    dx3, dg_final, loss_part = _loss_head(x3, gains["norm_final"], target, "loss_head")

    dact = _matmul(dx3, W["w_ffn_out"], tb=True, name="d_act")
    gw = {"w_ffn_out": _matmul(act, dx3, ta=True, tm=1408, tn=1024, tk=512, out_dtype=F32, name="gw_ffn_out")}

    def swiglu_bwd(gate, up, da):
        gate, up, da = gate.astype(F32), up.astype(F32), da.astype(F32)
        sg = jax.nn.sigmoid(gate)
        dgate = da * up * (sg * (1.0 + gate * (1.0 - sg)))
        return jnp.concatenate([dgate, da * (gate * sg)], axis=-1)

    dgu = _rowwise(swiglu_bwd, [(gu, f, 0), (gu, f, 1), (dact, f, 0)], [(2 * f, BF)], rows=s, tm=256,
                   name="swiglu_bwd")[0]
    dhf = _matmul(dgu, W["w_ffn_in"], tb=True, tk=1408, out_dtype=F32, name="d_hf")
    gw["w_ffn_in"] = _matmul(hf, dgu, ta=True, tm=1024, tn=1408, tk=512, out_dtype=F32, name="gw_ffn_in")
    dx2, dg_ffn = _rmsnorm_bwd(x2, gains["norm_ffn"], dhf, dx3, "norm_ffn_bwd")

    def reduce_start(names, tag):
        views = []
        for n in names:
            r, cdim = gw[n].shape
            views.append(gw[n].reshape(1, 2, r // 2, cdim) if BIG_AXIS[n] == 1
                         else gw[n].reshape(N_CHIPS, 2, r // (2 * N_CHIPS), cdim))
        got = _pair_exchange(views, "grad_pair_exchange_" + tag)
        sums = [_pair_add(v, g, core, "pair_add_" + n) for n, v, g in zip(names, views, got)]
        group_axes = [BIG_AXIS[n] for n in names]
        sems, sums, lands, token = _chip_exchange_start(sums, group_axes, "grad_chip_start_" + tag)
        return (names, group_axes, sems, sums, lands), token

    def reduce_finish(group, after, tag):
        names, group_axes, sems, sums, lands = group
        sums, slots = _chip_exchange_wait(sems, sums, lands, group_axes, after, "grad_chip_wait_" + tag)
        return [_chip_sum(sm, sl, BIG_AXIS[n], place, "chip_sum_" + n) for n, sm, sl in zip(names, sums, slots)]

    group_ffn, token_ffn = reduce_start(["w_ffn_in", "w_ffn_out"], "ffn")

    do_m = _matmul(dx2, W["w_mem_o"], tb=True, after=(token_ffn,), name="d_o_m")
    gw["w_mem_o"] = _matmul(o_m, dx2, ta=True, tk=512, tn=1024, out_dtype=F32, name="gw_mem_o")
    dq_m, dk_m, dv_m = _mem_bwd(q_m, kv, do_m)
    dkv = jnp.concatenate([dk_m, dv_m], axis=-1)
    dhq = _matmul(dq_m, W["w_mem_q"], tb=True, out_dtype=F32, name="d_hq")
    gw["w_mem_q"] = _matmul(hq, dq_m, ta=True, tk=512, tn=1024, out_dtype=F32, name="gw_mem_q")
    dmn = _matmul(dkv, W["w_mem_kv"], tb=True, out_dtype=F32, name="d_mn")
    gw["w_mem_kv"] = _matmul(mn, dkv, ta=True, tn=1024, out_dtype=F32, name="gw_mem_kv")
    _, dg_kv = _rmsnorm_bwd(mem, gains["norm_mem_kv"], dmn, None, "norm_mem_kv_bwd")
    dx1, dg_q = _rmsnorm_bwd(x1, gains["norm_mem_q"], dhq, dx2, "norm_mem_q_bwd")

    dmerged = _matmul(dx1, W["w_mix_out"], tb=True, name="d_merged")
    gw["w_mix_out"] = _matmul(merged, dx1, ta=True, tk=512, tn=1024, out_dtype=F32, name="gw_mix_out")

    def merge_bwd(ga, gb, a, b, dm):
        dm = dm.astype(F32)
        sa, sb = jax.nn.sigmoid(ga.astype(F32)), jax.nn.sigmoid(gb.astype(F32))
        a, b = a.astype(F32), b.astype(F32)
        return dm * sa, dm * sb, dm * a * (sa * (1.0 - sa)), dm * b * (sb * (1.0 - sb))

    dbr_a, dbr_b, dga, dgb = _rowwise(
        merge_bwd, [(proj, d, ga_blk), (proj, d, gb_blk), (br_a, d, 0), (br_b, d, 0), (dmerged, d, 0)],
        [(d, BF)] * 4, rows=s, tm=256, name="merge_bwd")
    do_a = _matmul(dbr_a, W["w_branch_a"], tb=True, name="d_o_a")
    gw["w_branch_a"] = _matmul(o_a, dbr_a, ta=True, tk=512, tn=1024, out_dtype=F32, name="gw_branch_a")
    dy_b = _matmul(dbr_b, W["w_branch_b"], tb=True, name="d_y_b")
    gw["w_branch_b"] = _matmul(y_b, dbr_b, ta=True, tk=512, tn=1024, out_dtype=F32, name="gw_branch_b")
    group_mid, token_mid = reduce_start(
        ["w_mem_o", "w_mem_q", "w_mem_kv", "w_mix_out", "w_branch_a", "w_branch_b"], "mid")
    du, dgate_b, dgate_c, dconv = _conv_bwd(proj, conv_full, dy_b)
    dq, dk, dv = _sb_bwd(proj, o_a32, do_a, after=(token_mid,))

    def assemble(*parts):
        return jnp.concatenate([p.astype(BF) for p in parts], axis=-1)

    hw = dq.shape[1]
    dproj = _rowwise(assemble, [(t, hw, 0) for t in (dq, dk, dv, du, dgate_b, dgate_c)] + [(dga, d, 0), (dgb, d, 0)],
                     [(proj.shape[1], BF)], rows=s, tm=256, name="assemble_dproj")[0]
    dh1 = _matmul(dproj, W["w_in"], tb=True, tk=1280, out_dtype=F32, name="d_h1")
    gw["w_in"] = _matmul(h1, dproj, ta=True, tk=512, tn=1280, out_dtype=F32, name="gw_in")
    grad_x, dg_mix = _rmsnorm_bwd(x, gains["norm_mix"], dh1, dx1, "norm_mix_bwd")

    group_in, token_in = reduce_start(["w_in"], "in")
    halves = {}
    for group, after, tag in ((group_ffn, (grad_x,), "ffn"), (group_mid, (grad_x,), "mid"),
                              (group_in, (token_in,), "in")):
        halves.update(zip(group[0], reduce_finish(group, after, tag)))
    both = _half_swap([halves[n] for n in BIG])
    grads = {n: b.reshape(wts[n].shape[1:]) for n, b in zip(BIG, both)}

    small_g = [dg_mix, dg_q, dg_kv, dg_ffn, dg_final]
    pack = _pack_small(small_g, dconv[:CONV_K])
    pack = pack.at[ROW_LOSS].set(jnp.broadcast_to(loss_part[0, :1], (d,)))
    red = _allreduce_small(pack)
    loss = red[ROW_LOSS, 0]
    cw = conv_w.shape[2]
    conv_g = lax.dynamic_slice(red, (ROW_CONV, chip * cw), (CONV_K, cw))
    small_grad = _pack_small([red[i] for i in range(len(NORMS))], conv_g)
    small = [_pack_small([t[n] for n in NORMS], t["conv_w"][0]) for t in (wts, mom, var)]
    s_delta, s_m, s_v = _adamw(small[0], small_grad, small[1], small[2], "adamw_small")

    out = {"grad": {}, "delta": {}, "new_m": {}, "new_v": {}}
    for n in BIG:
        shp = wts[n].shape
        dl, m2, v2 = _adamw(wts[n][0], grads[n], mom[n][0], var[n][0], "adamw_" + n)
        out["grad"][n] = grads[n].reshape(shp)
        out["delta"][n], out["new_m"][n], out["new_v"][n] = dl.reshape(shp), m2.reshape(shp), v2.reshape(shp)
    for key, blk in (("grad", small_grad), ("delta", s_delta), ("new_m", s_m), ("new_v", s_v)):
        for i, n in enumerate(NORMS):
            out[key][n] = blk[i].reshape(wts[n].shape)
        out[key]["conv_w"] = blk[ROW_CONV:ROW_CONV + CONV_K, :cw].reshape(conv_w.shape)

    return (loss, grad_x[None], *[out["grad"][n] for n in ORDER], *[out["delta"][n] for n in ORDER],
            *[out["new_m"][n] for n in ORDER], *[out["new_v"][n] for n in ORDER])
```

```python
import functools
import math

import jax
import jax.numpy as jnp
from jax import lax
from jax.experimental import pallas as pl
from jax.experimental.pallas import tpu as pltpu

BF = jnp.bfloat16
F32 = jnp.float32
MESH = pl.DeviceIdType.MESH

SB_HEAD_DIM = 64
LANES = 128
MEM_HEADS = 4
CONV_K = 3
CONV_ROWS = 8
EPS = 1e-6
N_CHIPS = 4
N_DEV = 8
VMEM_LIMIT = 56 * 1024 * 1024

ADAM_LR = 0.001
ADAM_B1 = 0.9
ADAM_B2 = 0.999
ADAM_EPS = 1e-08
ADAM_WD = 0.01
ADAM_STEP = 10

SMALL_ROWS = 16
ROW_CONV = 5
ROW_LOSS = 8


def _params(sem=None):
    return pltpu.CompilerParams(dimension_semantics=sem, vmem_limit_bytes=VMEM_LIMIT)


def _tile(dim, pref):
    if dim <= pref:
        return dim
    for step in (LANES, 8):
        t = (pref // step) * step
        while t >= step:
            if dim % t == 0:
                return t
            t -= step
    raise ValueError(f"no tile of {dim} under {pref}")


def _matmul(a, b, *, ta=False, tb=False, tm=1024, tn=512, tk=None, out_dtype=BF, resid=None, after=(), name):
    if ta:
        kdim, m = a.shape
    else:
        m, kdim = a.shape
    n = b.shape[0] if tb else b.shape[1]
    tm, tn = _tile(m, tm), _tile(n, tn)
    tk = _tile(kdim, tk or kdim)
    nk = kdim // tk
    a_spec = pl.BlockSpec((tk, tm), lambda i, j, k: (k, i)) if ta else pl.BlockSpec((tm, tk), lambda i, j, k: (i, k))
    b_spec = pl.BlockSpec((tn, tk), lambda i, j, k: (j, k)) if tb else pl.BlockSpec((tk, tn), lambda i, j, k: (k, j))
    o_spec = pl.BlockSpec((tm, tn), lambda i, j, k: (i, j))
    dims = (((0 if ta else 1,), (1 if tb else 0,)), ((), ()))
    has_res = resid is not None

    def body(*refs):
        a_ref, b_ref = refs[0], refs[1]
        o_ref = refs[2 + has_res + len(after)]
        av, bv = a_ref[...], b_ref[...]
        if av.dtype != BF:
            av = av.astype(BF)
        if bv.dtype != BF:
            bv = bv.astype(BF)
        p = lax.dot_general(av, bv, dims, preferred_element_type=F32)

        def finish(acc):
            if has_res:
                acc = refs[2][...] + acc
            o_ref[...] = acc.astype(o_ref.dtype)

        if nk == 1:
            finish(p)
        else:
            acc_ref = refs[-1]
            k = pl.program_id(2)

            @pl.when(k == 0)
            def _():
                acc_ref[...] = p

            @pl.when(k > 0)
            def _():
                acc_ref[...] += p

            @pl.when(k == nk - 1)
            def _():
                finish(acc_ref[...])

    return pl.pallas_call(
        body, name=name, grid=(m // tm, n // tn, nk),
        in_specs=[a_spec, b_spec] + ([o_spec] if has_res else []) + [HBM_SPEC] * len(after),
        out_specs=o_spec, out_shape=jax.ShapeDtypeStruct((m, n), out_dtype),
        scratch_shapes=[pltpu.VMEM((tm, tn), F32)] if nk > 1 else [],
        compiler_params=_params(("parallel", "parallel", "arbitrary")),
    )(*([a, b] + ([resid] if has_res else []) + list(after)))


def _rowwise(fn, ins, outs, *, rows, tm, name, accs=(), after=()):
    tm = _tile(rows, tm)
    in_specs, args = [], []
    for arr, cols, cb in ins:
        if cols is None:
            in_specs.append(pl.BlockSpec(arr.shape, lambda i, nd=arr.ndim: (0,) * nd))
        else:
            in_specs.append(pl.BlockSpec((tm, cols), lambda i, cb=cb: (i, cb)))
        args.append(arr)
    out_specs = [pl.BlockSpec((tm, cols), lambda i: (i, 0)) for cols, _ in outs]
    out_shape = [jax.ShapeDtypeStruct((rows, cols), dt) for cols, dt in outs]
    for r, c in accs:
        out_specs.append(pl.BlockSpec((r, c), lambda i: (0, 0)))
        out_shape.append(jax.ShapeDtypeStruct((r, c), F32))
    n_in, n_out = len(ins), len(outs)
    in_specs += [HBM_SPEC] * len(after)
    args += list(after)

    def body(*refs):
        res = fn(*[r[...] for r in refs[:n_in]])
        if not isinstance(res, (tuple, list)):
            res = (res,)
        refs = refs[n_in + len(after):]
        for o_ref, val in zip(refs[:n_out], res[:n_out]):
            o_ref[...] = val.astype(o_ref.dtype)
        first = pl.program_id(0) == 0
        for a_ref, val in zip(refs[n_out:], res[n_out:]):
            @pl.when(first)
            def _(a_ref=a_ref, val=val):
                a_ref[...] = val

            @pl.when(jnp.logical_not(first))
            def _(a_ref=a_ref, val=val):
                a_ref[...] += val

    res = pl.pallas_call(
        body, name=name, grid=(rows // tm,), in_specs=in_specs, out_specs=out_specs, out_shape=out_shape,
        compiler_params=_params(("arbitrary",) if accs else ("parallel",)),
    )(*args)
    return res


def _rstd(xf):
    return lax.rsqrt(jnp.mean(xf * xf, axis=-1, keepdims=True) + EPS)


def _rmsnorm(x, g, name, after=()):
    rows, d = x.shape
    return _rowwise(lambda xv, gv: xv * _rstd(xv) * gv, [(x, d, 0), (g, None, None)], [(d, BF)],
                    rows=rows, tm=512, name=name, after=after)[0]


def _rmsnorm_bwd(x, g, dy, resid, name):
    rows, d = x.shape

    def fn(xv, gv, dyv, *rest):
        dyv = dyv.astype(F32)
        r = _rstd(xv)
        xn = xv * r
        dxn = dyv * gv
        dx = r * (dxn - xn * jnp.mean(dxn * xn, axis=-1, keepdims=True))
        if rest:
            dx = rest[0] + dx
        return dx, jnp.sum(dyv * xn, axis=0, keepdims=True)

    ins = [(x, d, 0), (g, None, None), (dy, d, 0)] + ([(resid, d, 0)] if resid is not None else [])
    outs = [(d, F32)]
    return _rowwise(fn, ins, outs, rows=rows, tm=512, name=name, accs=[(1, d)])


def _loss_head(x, g, target, name):
    rows, d = x.shape

    def fn(xv, gv, tv):
        r = _rstd(xv)
        xn = xv * r
        err = xn * gv - tv
        per_tok = jnp.mean(err * err, axis=-1, keepdims=True)
        loss = 0.5 * jnp.sum(per_tok, axis=0, keepdims=True)
        dyv = err * (1.0 / d)
        dxn = dyv * gv
        dx = r * (dxn - xn * jnp.mean(dxn * xn, axis=-1, keepdims=True))
        return dx, jnp.sum(dyv * xn, axis=0, keepdims=True), jnp.broadcast_to(loss, (1, LANES))

    return _rowwise(fn, [(x, d, 0), (g, None, None), (target, d, 0)], [(d, F32)],
                    rows=rows, tm=512, name=name, accs=[(1, d), (1, LANES)])


SB_TK = 128
SB_KT = 4


def _sb_consts(tq):
    tk = SB_TK
    diff = lax.broadcasted_iota(jnp.int32, (tq, tk), 1) - lax.broadcasted_iota(jnp.int32, (tq, tk), 0)
    rj = lax.broadcasted_iota(jnp.int32, (2 * tk, 2 * tk), 0) & (tk - 1)
    cj = lax.broadcasted_iota(jnp.int32, (2 * tk, 2 * tk), 1)
    ones_half = cj >= tk
    later = jnp.where((rj > cj) | ones_half, 1.0, 0.0).astype(BF)
    later_incl = jnp.where((rj >= cj) | ones_half, 1.0, 0.0).astype(BF)
    return diff, later, later_incl


def _split_dot(val, rhs_twice):
    hi = val.astype(BF)
    lo = (val - hi.astype(F32)).astype(BF)
    return jnp.dot(jnp.concatenate([hi, lo], axis=1), rhs_twice, preferred_element_type=F32)


def _log_terms(z):
    sp = jnp.maximum(z, 0.0) + jnp.log(1.0 + jnp.exp(-jnp.abs(z)))
    return z - sp, sp


NT = (((1,), (1,)), ((), ()))
TN = (((0,), (0,)), ((), ()))


def _head_lane_masks(rows):
    lane = lax.broadcasted_iota(jnp.int32, (rows, LANES), 1)
    first = jnp.where(lane < SB_HEAD_DIM, 1.0, 0.0)
    return first.astype(BF), (1.0 - first).astype(BF)


def _both_heads(tile, masks):
    return jnp.concatenate([tile * masks[0], tile * masks[1]], axis=0)


def _sb_fwd(proj):
    s = proj.shape[0]
    tk, tq = SB_TK, SB_KT * SB_TK
    n_pairs = 4
    scale = 1.0 / math.sqrt(SB_HEAD_DIM)

    def body(q_ref, k_ref, v_ref, o_ref, o32_ref):
        i = pl.program_id(1)
        diff, later, _ = _sb_consts(tq)
        qs = (q_ref[...].astype(F32) * scale).astype(BF)
        lane_masks = _head_lane_masks(tk)

        def step(g, state, masked):
            tiles = list(reversed(range(SB_KT)))
            chains = [(t, h) for t in tiles for h in range(2)]
            rows = {t: pl.ds(pl.multiple_of((g * SB_KT + t) * tk, tk), tk) for t in tiles}
            ks = {t: _both_heads(k_ref[rows[t], :], lane_masks) for t in tiles}
            vs = {t: _both_heads(v_ref[rows[t], :], lane_masks) for t in tiles}
            allowed = {t: diff < -t * tk for t in tiles}
            zs = {t: lax.dot_general(qs, ks[t], NT, preferred_element_type=F32) for t in tiles}
            logs = {}
            for t, h in chains:
                log_b, sp = _log_terms(zs[t][:, h * tk:(h + 1) * tk])
                logs[t, h] = (log_b, jnp.where(allowed[t], sp, 0.0) if masked else sp)
            sums = {c: _split_dot(logs[c][1], later) for c in chains}
            carries = list(state[0])
            ws = {}
            for t, h in chains:
                w = jnp.exp(logs[t, h][0] - (sums[t, h][:, :tk] + carries[h]))
                ws[t, h] = (jnp.where(allowed[t], w, 0.0) if masked else w).astype(BF)
                carries[h] = carries[h] + sums[t, h][:, tk:]
            acc = state[1]
            for t in tiles:
                acc = acc + jnp.dot(jnp.concatenate([ws[t, 0], ws[t, 1]], axis=1), vs[t],
                                    preferred_element_type=F32)
            return tuple(carries), acc

        zero = jnp.zeros((tq, LANES), F32)
        state = step(i, ((zero, zero), zero), True)
        state = lax.fori_loop(0, i, lambda gg, st: step(i - 1 - gg, st, False), state)
        o_ref[...] = state[1].astype(o_ref.dtype)
        o32_ref[...] = state[1]

    tok = pl.BlockSpec((tq, LANES), lambda p, i: (i, p))
    return pl.pallas_call(
        body, name="sb_attn_fwd", grid=(n_pairs, s // tq),
        in_specs=[tok,
                  pl.BlockSpec((s, LANES), lambda p, i: (0, n_pairs + p)),
                  pl.BlockSpec((s, LANES), lambda p, i: (0, 2 * n_pairs + p))],
        out_specs=[tok, tok],
        out_shape=[jax.ShapeDtypeStruct((s, n_pairs * LANES), BF), jax.ShapeDtypeStruct((s, n_pairs * LANES), F32)],
        compiler_params=_params(("parallel", "arbitrary")),
    )(proj, proj, proj)


def _sb_bwd(proj, o32, do_a, after=()):
    s = proj.shape[0]
    tk, tq = SB_TK, SB_KT * SB_TK
    n_pairs = 4
    scale = 1.0 / math.sqrt(SB_HEAD_DIM)

    def body(q_ref, k_ref, v_ref, o_ref, do_ref, *rest):
        dq_ref, dk_ref, dv_ref = rest[len(after):]
        i = pl.program_id(1)

        @pl.when(i == 0)
        def _():
            dk_ref[...] = jnp.zeros_like(dk_ref)
            dv_ref[...] = jnp.zeros_like(dv_ref)

        diff, later, later_incl = _sb_consts(tq)
        qs = (q_ref[...].astype(F32) * scale).astype(BF)
        do2 = do_ref[...]
        prod = do2.astype(F32) * o_ref[...]
        lane_masks = _head_lane_masks(tk)
        first_head = lax.broadcasted_iota(jnp.int32, (tq, LANES), 1) < SB_HEAD_DIM
        totals = [jnp.broadcast_to(jnp.sum(jnp.where(keep, prod, 0.0), axis=-1, keepdims=True), (tq, tk))
                  for keep in (first_head, jnp.logical_not(first_head))]
        first_head_k = first_head[:tk]

        def step(g_idx, state, masked):
            tiles = list(reversed(range(SB_KT)))
            chains = [(t, h) for t in tiles for h in range(2)]
            rows = {t: pl.ds(pl.multiple_of((g_idx * SB_KT + t) * tk, tk), tk) for t in tiles}
            ks = {t: _both_heads(k_ref[rows[t], :], lane_masks) for t in tiles}
            vs = {t: _both_heads(v_ref[rows[t], :], lane_masks) for t in tiles}
            allowed = {t: diff < -t * tk for t in tiles}
            zs = {t: lax.dot_general(qs, ks[t], NT, preferred_element_type=F32) for t in tiles}
            dws = {t: lax.dot_general(do2, vs[t], NT, preferred_element_type=F32) for t in tiles}
            logs = {}
            for t, h in chains:
                log_b, sp = _log_terms(zs[t][:, h * tk:(h + 1) * tk])
                logs[t, h] = (log_b, jnp.where(allowed[t], sp, 0.0) if masked else sp)
            sums = {c: _split_dot(logs[c][1], later) for c in chains}
            c_log, c_g = list(state[0]), list(state[1])
            ws, gs = {}, {}
            for t, h in chains:
                w = jnp.exp(logs[t, h][0] - (sums[t, h][:, :tk] + c_log[h]))
                ws[t, h] = (jnp.where(allowed[t], w, 0.0) if masked else w).astype(BF)
                c_log[h] = c_log[h] + sums[t, h][:, tk:]
                gs[t, h] = ws[t, h].astype(F32) * dws[t][:, h * tk:(h + 1) * tk]
            gsums = {c: _split_dot(gs[c], later_incl) for c in chains}
            dzs = {}
            for t, h in chains:
                beta = jnp.exp(logs[t, h][0])
                earlier = totals[h] - (gsums[t, h][:, :tk] + c_g[h])
                dz = gs[t, h] * (1.0 - beta) - earlier * beta
                dzs[t, h] = (jnp.where(allowed[t], dz, 0.0) if masked else dz).astype(BF)
                c_g[h] = c_g[h] + gsums[t, h][:, tk:]
            dq = state[2]
            for t in tiles:
                dz_both = jnp.concatenate([dzs[t, 0], dzs[t, 1]], axis=1)
                w_both = jnp.concatenate([ws[t, 0], ws[t, 1]], axis=1)
                dq = dq + jnp.dot(dz_both, ks[t], preferred_element_type=F32)
                dk2 = lax.dot_general(dz_both, qs, TN, preferred_element_type=F32)
                dv2 = lax.dot_general(w_both, do2, TN, preferred_element_type=F32)
                dk_ref[rows[t], :] += jnp.where(first_head_k, dk2[:tk], dk2[tk:])
                dv_ref[rows[t], :] += jnp.where(first_head_k, dv2[:tk], dv2[tk:])
            return tuple(c_log), tuple(c_g), dq

        zero = jnp.zeros((tq, LANES), F32)
        state = step(i, ((zero, zero), (zero, zero), zero), True)
        state = lax.fori_loop(0, i, lambda gg, st: step(i - 1 - gg, st, False), state)
        dq_ref[...] = (state[2] * scale).astype(dq_ref.dtype)

    width = n_pairs * LANES
    return pl.pallas_call(
        body, name="sb_attn_bwd", grid=(n_pairs, s // tq),
        in_specs=[pl.BlockSpec((tq, LANES), lambda p, i: (i, p)),
                  pl.BlockSpec((s, LANES), lambda p, i: (0, n_pairs + p)),
                  pl.BlockSpec((s, LANES), lambda p, i: (0, 2 * n_pairs + p)),
                  pl.BlockSpec((tq, LANES), lambda p, i: (i, p)),
                  pl.BlockSpec((tq, LANES), lambda p, i: (i, p))] + [HBM_SPEC] * len(after),
        out_specs=[pl.BlockSpec((tq, LANES), lambda p, i: (i, p)),
                   pl.BlockSpec((s, LANES), lambda p, i: (0, p)),
                   pl.BlockSpec((s, LANES), lambda p, i: (0, p))],
        out_shape=[jax.ShapeDtypeStruct((s, width), BF), jax.ShapeDtypeStruct((s, width), F32),
                   jax.ShapeDtypeStruct((s, width), F32)],
        compiler_params=_params(("parallel", "arbitrary")),
    )(proj, proj, proj, o32, do_a, *after)


CONV_COL0 = 12


def _shift_rows(v, k):
    n = v.shape[0]
    row = lax.broadcasted_iota(jnp.int32, v.shape, 0)
    rolled = pltpu.roll(v, k % n, axis=0)
    keep = row >= k if k > 0 else row < n + k
    return jnp.where(keep, rolled, 0.0)


def _conv_specs(s):
    return [pl.BlockSpec((s, LANES), lambda cb: (0, CONV_COL0 + cb)),
            pl.BlockSpec((s, LANES), lambda cb: (0, CONV_COL0 + 4 + cb)),
            pl.BlockSpec((s, LANES), lambda cb: (0, CONV_COL0 + 8 + cb)),
            pl.BlockSpec((CONV_ROWS, LANES), lambda cb: (0, cb))]


def _conv_fwd(proj, conv_w):
    s = proj.shape[0]

    def body(u_ref, gb_ref, gc_ref, w_ref, y_ref):
        cu = gc_ref[...].astype(F32) * u_ref[...].astype(F32)
        w = w_ref[...]
        y = w[0:1] * _shift_rows(cu, 2) + w[1:2] * _shift_rows(cu, 1) + w[2:3] * cu
        y_ref[...] = (gb_ref[...].astype(F32) * y).astype(y_ref.dtype)

    return pl.pallas_call(
        body, name="conv_fwd", grid=(4,), in_specs=_conv_specs(s),
        out_specs=pl.BlockSpec((s, LANES), lambda cb: (0, cb)),
        out_shape=jax.ShapeDtypeStruct((s, 4 * LANES), BF),
        compiler_params=_params(("parallel",)),
    )(proj, proj, proj, conv_w)


def _conv_bwd(proj, conv_w, dy):
    s = proj.shape[0]

    def body(u_ref, gb_ref, gc_ref, w_ref, dy_ref, du_ref, dgb_ref, dgc_ref, dw_ref):
        u, gc = u_ref[...].astype(F32), gc_ref[...].astype(F32)
        dyv = dy_ref[...].astype(F32)
        w = w_ref[...]
        cu = gc * u
        cu1, cu2 = _shift_rows(cu, 1), _shift_rows(cu, 2)
        conv = w[0:1] * cu2 + w[1:2] * cu1 + w[2:3] * cu
        dgb_ref[...] = (dyv * conv).astype(dgb_ref.dtype)
        dc = dyv * gb_ref[...].astype(F32)
        dcu = w[2:3] * dc + w[1:2] * _shift_rows(dc, -1) + w[0:1] * _shift_rows(dc, -2)
        dgc_ref[...] = (dcu * u).astype(dgc_ref.dtype)
        du_ref[...] = (dcu * gc).astype(du_ref.dtype)
        tap_row = lax.broadcasted_iota(jnp.int32, (CONV_ROWS, LANES), 0)
        dw = jnp.zeros((CONV_ROWS, LANES), F32)
        for t, shifted in enumerate((cu2, cu1, cu)):
            dw = jnp.where(tap_row == t, jnp.sum(dc * shifted, axis=0, keepdims=True), dw)
        dw_ref[...] = dw

    col = pl.BlockSpec((s, LANES), lambda cb: (0, cb))
    act = jax.ShapeDtypeStruct((s, 4 * LANES), BF)
    return pl.pallas_call(
        body, name="conv_bwd", grid=(4,), in_specs=_conv_specs(s) + [col],
        out_specs=[col, col, col, pl.BlockSpec((CONV_ROWS, LANES), lambda cb: (0, cb))],
        out_shape=[act, act, act, jax.ShapeDtypeStruct((CONV_ROWS, 4 * LANES), F32)],
        compiler_params=_params(("parallel",)),
    )(proj, proj, proj, conv_w, dy)


def _mem_probs(q, k, scale):
    sc = lax.dot_general(q, k, NT, preferred_element_type=F32) * scale
    p = jnp.exp(sc - jnp.max(sc, axis=-1, keepdims=True))
    return p / jnp.sum(p, axis=-1, keepdims=True)


def _mem_fwd(q_m, kv, tq=512):
    s, d = q_m.shape
    mlen = kv.shape[0]
    hd = d // MEM_HEADS
    tq = _tile(s, tq)
    scale = 1.0 / math.sqrt(hd)

    def body(q_ref, k_ref, v_ref, o_ref):
        p = _mem_probs(q_ref[...], k_ref[...], scale)
        o_ref[...] = jnp.dot(p.astype(BF), v_ref[...], preferred_element_type=F32).astype(o_ref.dtype)

    return pl.pallas_call(
        body, name="mem_attn_fwd", grid=(MEM_HEADS, s // tq),
        in_specs=[pl.BlockSpec((tq, hd), lambda h, i: (i, h)),
                  pl.BlockSpec((mlen, hd), lambda h, i: (0, h)),
                  pl.BlockSpec((mlen, hd), lambda h, i: (0, MEM_HEADS + h))],
        out_specs=pl.BlockSpec((tq, hd), lambda h, i: (i, h)),
        out_shape=jax.ShapeDtypeStruct((s, d), BF),
        compiler_params=_params(("parallel", "parallel")),
    )(q_m, kv, kv)


def _mem_bwd(q_m, kv, do_m, tq=512):
    s, d = q_m.shape
    mlen = kv.shape[0]
    hd = d // MEM_HEADS
    tq = _tile(s, tq)
    scale = 1.0 / math.sqrt(hd)

    def body(q_ref, k_ref, v_ref, do_ref, dq_ref, dk_ref, dv_ref):
        q, k, v, do = q_ref[...], k_ref[...], v_ref[...], do_ref[...]
        p = _mem_probs(q, k, scale)
        dp = lax.dot_general(do, v, NT, preferred_element_type=F32)
        ds = p * (dp - jnp.sum(dp * p, axis=-1, keepdims=True)) * scale
        dsb = ds.astype(BF)
        dq_ref[...] = jnp.dot(dsb, k, preferred_element_type=F32).astype(dq_ref.dtype)
        dk = lax.dot_general(dsb, q, TN, preferred_element_type=F32)
        dv = lax.dot_general(p.astype(BF), do, TN, preferred_element_type=F32)
        first = pl.program_id(1) == 0

        @pl.when(first)
        def _():
            dk_ref[...] = dk
            dv_ref[...] = dv

        @pl.when(jnp.logical_not(first))
        def _():
            dk_ref[...] += dk
            dv_ref[...] += dv

    tok = pl.BlockSpec((tq, hd), lambda h, i: (i, h))
    memb = pl.BlockSpec((mlen, hd), lambda h, i: (0, h))
    return pl.pallas_call(
        body, name="mem_attn_bwd", grid=(MEM_HEADS, s // tq),
        in_specs=[tok, memb, pl.BlockSpec((mlen, hd), lambda h, i: (0, MEM_HEADS + h)), tok],
        out_specs=[tok, memb, memb],
        out_shape=[jax.ShapeDtypeStruct((s, d), BF), jax.ShapeDtypeStruct((mlen, d), F32),
                   jax.ShapeDtypeStruct((mlen, d), F32)],
        compiler_params=_params(("parallel", "arbitrary")),
    )(q_m, kv, kv, do_m)


def _place():
    x, y, c = lax.axis_index("x"), lax.axis_index("y"), lax.axis_index("c")
    other_chips = [(1 - x, y), (x, 1 - y), (1 - x, 1 - y)]
    return x, y, c, other_chips


def _chip_no(cx, cy):
    return 2 * cx + cy


HBM_SPEC = pl.BlockSpec(memory_space=pl.ANY)


def _cast_place(shard, axis, place, dtype, name):
    r, c = shard.shape
    tr = _tile(r, max(16, 1048576 // c))
    nblk = r // tr
    if axis == 1:
        full, out_map = (r, N_CHIPS * c), lambda i, pref: (i, pref[0])
    else:
        full, out_map = (N_CHIPS * r, c), lambda i, pref: (pref[0] * nblk + i, 0)

    def body(pref, s_ref, o_ref):
        o_ref[...] = s_ref[...].astype(o_ref.dtype)

    return pl.pallas_call(
        body, name=name,
        grid_spec=pltpu.PrefetchScalarGridSpec(
            num_scalar_prefetch=1, grid=(nblk,),
            in_specs=[pl.BlockSpec((tr, c), lambda i, pref: (i, 0))],
            out_specs=pl.BlockSpec((tr, c), out_map)),
        out_shape=jax.ShapeDtypeStruct(full, dtype),
        compiler_params=_params(("parallel",)),
    )(place, shard)


def _region(ref, axis, chip_no, half):
    width = ref.shape[axis] // N_CHIPS
    start = pl.multiple_of(chip_no * width, width)
    if axis == 1:
        if half is None:
            return ref.at[:, pl.ds(start, width)]
        hr = ref.shape[0] // 2
        return ref.at[pl.ds(pl.multiple_of(half * hr, hr), hr), pl.ds(start, width)]
    if half is None:
        return ref.at[pl.ds(start, width), :]
    hr = width // 2
    return ref.at[pl.ds(pl.multiple_of(start + half * hr, hr), hr), :]


def _gather_weights(fulls, axes, split):
    n = len(fulls)

    def body(*refs):
        outs = refs[n:2 * n]
        send, recv, fsend, frecv = refs[2 * n:]
        x, y, c, others = _place()
        me = _chip_no(x, y)
        sibling = (x, y, 1 - c)

        def copy(w, chip_no, half, sems, p, to):
            reg = _region(outs[w], axes[w], chip_no, half)
            return pltpu.make_async_remote_copy(
                src_ref=reg, dst_ref=reg, send_sem=sems[0].at[w, p], recv_sem=sems[1].at[w, p],
                device_id=to, device_id_type=MESH)

        for w in range(n):
            for p, chip in enumerate(others):
                copy(w, me, c if split[w] else None, (send, recv), p, (chip[0], chip[1], c)).start()
        for w in range(n):
            for p, chip in enumerate(others):
                half = c if split[w] else None
                copy(w, _chip_no(*chip), half, (send, recv), p, (chip[0], chip[1], c)).wait_recv()
                if split[w]:
                    copy(w, _chip_no(*chip), c, (fsend, frecv), p, sibling).start()
        for w in range(n):
            for p, chip in enumerate(others):
                copy(w, me, c if split[w] else None, (send, recv), p, (chip[0], chip[1], c)).wait_send()
                if split[w]:
                    handed = copy(w, _chip_no(*chip), 1 - c, (fsend, frecv), p, sibling)
                    handed.wait_recv()
                    handed.wait_send()

    return pl.pallas_call(
        body, name="gather_weights",
        in_specs=[HBM_SPEC] * n, out_specs=[HBM_SPEC] * n,
        out_shape=[jax.ShapeDtypeStruct(f.shape, f.dtype) for f in fulls],
        input_output_aliases={i: i for i in range(n)},
        scratch_shapes=[pltpu.SemaphoreType.DMA((n, 3))] * 4,
        compiler_params=pltpu.CompilerParams(has_side_effects=True),
    )(*fulls)


SEM_SPEC = pl.BlockSpec(memory_space=pltpu.SEMAPHORE)
IN_HBM = pl.BlockSpec(memory_space=pltpu.HBM)
FLOWS = pltpu.CompilerParams(has_side_effects=pltpu.SideEffectType.DATAFLOW_SIDE_EFFECTING)
TOKEN = jax.ShapeDtypeStruct((8, LANES), F32)


def _in_hbm(arrays):
    return [pltpu.with_memory_space_constraint(a, pltpu.HBM) for a in arrays]


def _hbm_like(arrays):
    return [pltpu.HBM(a.shape, a.dtype) for a in arrays]


def _fetch_copy(refs, axes, send, recv, w, p, chip, c, arriving):
    owner = _chip_no(*chip) if arriving else _chip_no(lax.axis_index("x"), lax.axis_index("y"))
    reg = _region(refs[w], axes[w], owner, c)
    return pltpu.make_async_remote_copy(
        src_ref=reg, dst_ref=reg, send_sem=send[p], recv_sem=recv[p],
        device_id=(chip[0], chip[1], c), device_id_type=MESH)


N_PEERS = N_CHIPS - 1
PEER_SEMS = [pltpu.SemaphoreType.DMA(())] * (2 * N_PEERS)


def _fetch_start(fulls, axes, after, name):
    n = len(fulls)

    def body(*refs):
        ins = refs[:n]
        sems = refs[n + len(after):]
        send, recv = sems[:N_PEERS], sems[N_PEERS:2 * N_PEERS]
        token = refs[-1]
        _, _, c, others = _place()
        for w in range(n):
            for p, chip in enumerate(others):
                _fetch_copy(ins, axes, send, recv, w, p, chip, c, False).start()
        token[...] = jnp.zeros_like(token)

    res = pl.pallas_call(
        body, name=name,
        in_specs=[IN_HBM] * n + [HBM_SPEC] * len(after),
        out_specs=[SEM_SPEC] * len(PEER_SEMS) + [IN_HBM] * n + [pl.BlockSpec(memory_space=pltpu.VMEM)],
        out_shape=PEER_SEMS + _hbm_like(fulls) + [TOKEN],
        input_output_aliases={i: len(PEER_SEMS) + i for i in range(n)},
        compiler_params=FLOWS,
    )(*_in_hbm(fulls), *after)
    k = len(PEER_SEMS)
    return list(res[:k]), list(res[k:k + n]), res[-1]


def _fetch_wait(sems, fulls, axes, after, name):
    n = len(fulls)

    def body(*refs):
        ins = refs[:n]
        send, recv = refs[n:n + N_PEERS], refs[n + N_PEERS:n + 2 * N_PEERS]
        _, _, c, others = _place()
        for w in range(n):
            for p, chip in enumerate(others):
                _fetch_copy(ins, axes, send, recv, w, p, chip, c, False).wait_send()
                _fetch_copy(ins, axes, send, recv, w, p, chip, c, True).wait_recv()

    res = pl.pallas_call(
        body, name=name,
        in_specs=[IN_HBM] * n + [SEM_SPEC] * len(sems) + [HBM_SPEC] * len(after),
        out_specs=[IN_HBM] * n,
        out_shape=_hbm_like(fulls),
        input_output_aliases={i: i for i in range(n)},
        compiler_params=FLOWS,
    )(*fulls, *sems, *after)
    return list(res)


def _hand_on(fulls, axes):
    n = len(fulls)

    def body(*refs):
        outs = refs[n:2 * n]
        send, recv = refs[2 * n:]
        x, y, c, others = _place()

        def copy(w, p, chip, half):
            reg = _region(outs[w], axes[w], _chip_no(*chip), half)
            return pltpu.make_async_remote_copy(
                src_ref=reg, dst_ref=reg, send_sem=send.at[w, p], recv_sem=recv.at[w, p],
                device_id=(x, y, 1 - c), device_id_type=MESH)

        for w in range(n):
            for p, chip in enumerate(others):
                copy(w, p, chip, c).start()
        for w in range(n):
            for p, chip in enumerate(others):
                copy(w, p, chip, 1 - c).wait()

    return pl.pallas_call(
        body, name="gather_hand_on",
        in_specs=[HBM_SPEC] * n, out_specs=[HBM_SPEC] * n,
        out_shape=[jax.ShapeDtypeStruct(f.shape, f.dtype) for f in fulls],
        input_output_aliases={i: i for i in range(n)},
        scratch_shapes=[pltpu.SemaphoreType.DMA((n, 3)), pltpu.SemaphoreType.DMA((n, 3))],
        compiler_params=pltpu.CompilerParams(has_side_effects=True),
    )(*fulls)


def _pair_exchange(grads, name):
    n = len(grads)

    def body(*refs):
        ins, outs = refs[:n], refs[n:2 * n]
        send, recv = refs[2 * n:]
        x, y, c, _ = _place()
        cps = []
        for w in range(n):
            cp = pltpu.make_async_remote_copy(
                src_ref=ins[w].at[:, 1 - c], dst_ref=outs[w], send_sem=send.at[w], recv_sem=recv.at[w],
                device_id=(x, y, 1 - c), device_id_type=MESH)
            cp.start()
            cps.append(cp)
        for cp in cps:
            cp.wait()

    return pl.pallas_call(
        body, name=name,
        in_specs=[HBM_SPEC] * n, out_specs=[HBM_SPEC] * n,
        out_shape=[jax.ShapeDtypeStruct((g.shape[0],) + g.shape[2:], g.dtype) for g in grads],
        scratch_shapes=[pltpu.SemaphoreType.DMA((n,)), pltpu.SemaphoreType.DMA((n,))],
        compiler_params=pltpu.CompilerParams(has_side_effects=True),
    )(*grads)


def _pair_add(g4, got, core, name):
    nj, _, hr, cdim = g4.shape
    tr = _tile(hr, max(8, 524288 // cdim))

    def body(core_ref, a_ref, b_ref, o_ref):
        o_ref[...] = (a_ref[...] + b_ref[...]).astype(o_ref.dtype)

    return pl.pallas_call(
        body, name=name,
        grid_spec=pltpu.PrefetchScalarGridSpec(
            num_scalar_prefetch=1, grid=(nj, hr // tr),
            in_specs=[pl.BlockSpec((1, None, tr, cdim), lambda j, i, core_ref: (j, core_ref[0], i, 0)),
                      pl.BlockSpec((1, tr, cdim), lambda j, i, core_ref: (j, i, 0))],
            out_specs=pl.BlockSpec((1, tr, cdim), lambda j, i, core_ref: (j, i, 0))),
        out_shape=jax.ShapeDtypeStruct((nj, hr, cdim), BF),
        compiler_params=_params(("parallel", "parallel")),
    )(core, g4, got)


def _piece(ref, axis, j, hc):
    if axis == 0:
        return ref.at[j]
    return ref.at[0, :, pl.ds(pl.multiple_of(j * hc, hc), hc)]


def _slot_shapes(sums, axes):
    return [(N_CHIPS - 1, sm.shape[1], sm.shape[2] // (1 if ax == 0 else N_CHIPS)) for sm, ax in zip(sums, axes)]


def _slot_copy(sums, lands, axes, send, recv, w, p, chip, c):
    return pltpu.make_async_remote_copy(
        src_ref=_piece(sums[w], axes[w], _chip_no(*chip), lands[w].shape[2]), dst_ref=lands[w].at[p],
        send_sem=send[p], recv_sem=recv[p],
        device_id=(chip[0], chip[1], c), device_id_type=MESH)


def _chip_exchange_start(sums, axes, name):
    n = len(sums)
    shapes = _slot_shapes(sums, axes)
    lands = [lax.empty(sh, sm.dtype) for sh, sm in zip(shapes, sums)]

    def body(*refs):
        ins, land_refs = refs[:n], refs[n:2 * n]
        send, recv = refs[2 * n:2 * n + N_PEERS], refs[2 * n + N_PEERS:2 * n + 2 * N_PEERS]
        token = refs[-1]
        _, _, c, others = _place()
        for w in range(n):
            for p, chip in enumerate(others):
                _slot_copy(ins, land_refs, axes, send, recv, w, p, chip, c).start()
        token[...] = jnp.zeros_like(token)

    k = len(PEER_SEMS)
    res = pl.pallas_call(
        body, name=name,
        in_specs=[IN_HBM] * (2 * n),
        out_specs=[SEM_SPEC] * k + [IN_HBM] * (2 * n) + [pl.BlockSpec(memory_space=pltpu.VMEM)],
        out_shape=PEER_SEMS + _hbm_like(list(sums) + lands) + [TOKEN],
        input_output_aliases={i: k + i for i in range(2 * n)},
        compiler_params=FLOWS,
    )(*_in_hbm(list(sums) + lands))
    return list(res[:k]), list(res[k:k + n]), list(res[k + n:k + 2 * n]), res[-1]


def _chip_exchange_wait(sems, sums, lands, axes, after, name):
    n = len(sums)

    def body(*refs):
        ins, land_refs = refs[:n], refs[n:2 * n]
        send, recv = refs[2 * n:2 * n + N_PEERS], refs[2 * n + N_PEERS:2 * n + 2 * N_PEERS]
        _, _, c, others = _place()
        for w in range(n):
            for p, chip in enumerate(others):
                cp = _slot_copy(ins, land_refs, axes, send, recv, w, p, chip, c)
                cp.wait_send()
                cp.wait_recv()

    res = pl.pallas_call(
        body, name=name,
        in_specs=[IN_HBM] * (2 * n) + [SEM_SPEC] * len(sems) + [HBM_SPEC] * len(after),
        out_specs=[IN_HBM] * (2 * n), out_shape=_hbm_like(list(sums) + list(lands)),
        input_output_aliases={i: i for i in range(2 * n)},
        compiler_params=FLOWS,
    )(*sums, *lands, *sems, *after)
    return list(res[:n]), list(res[n:])


def _chip_sum(psum, slots, axis, place, name):
    _, hr, hc = slots.shape
    tr = _tile(hr, 256)
    own_map = (lambda i, pref: (0, i, pref[0])) if axis == 1 else (lambda i, pref: (pref[0], i, 0))

    def body(pref, own_ref, s_ref, o_ref):
        o_ref[...] = ((own_ref[...].astype(F32) + s_ref[0].astype(F32)) + s_ref[1].astype(F32)) + s_ref[2].astype(F32)

    return pl.pallas_call(
        body, name=name,
        grid_spec=pltpu.PrefetchScalarGridSpec(
            num_scalar_prefetch=1, grid=(hr // tr,),
            in_specs=[pl.BlockSpec((None, tr, hc), own_map),
                      pl.BlockSpec((N_CHIPS - 1, tr, hc), lambda i, pref: (0, i, 0))],
            out_specs=pl.BlockSpec((None, tr, hc), lambda i, pref: (pref[1], i, 0))),
        out_shape=jax.ShapeDtypeStruct((2, hr, hc), F32),
        compiler_params=_params(("parallel",)),
    )(place, psum, slots)


def _half_swap(both):
    n = len(both)

    def body(*refs):
        outs = refs[n:2 * n]
        send, recv = refs[2 * n:]
        x, y, c, _ = _place()

        def copy(w, half):
            return pltpu.make_async_remote_copy(
                src_ref=outs[w].at[half], dst_ref=outs[w].at[half], send_sem=send.at[w], recv_sem=recv.at[w],
                device_id=(x, y, 1 - c), device_id_type=MESH)

        for w in range(n):
            copy(w, c).start()
        for w in range(n):
            copy(w, 1 - c).wait()

    return pl.pallas_call(
        body, name="grad_half_swap",
        in_specs=[HBM_SPEC] * n, out_specs=[HBM_SPEC] * n,
        out_shape=[jax.ShapeDtypeStruct(b.shape, b.dtype) for b in both],
        input_output_aliases={i: i for i in range(n)},
        scratch_shapes=[pltpu.SemaphoreType.DMA((n,)), pltpu.SemaphoreType.DMA((n,))],
        compiler_params=pltpu.CompilerParams(has_side_effects=True),
    )(*both)


def _allreduce_small(pack):
    rows, d = pack.shape

    def body(p_ref, o_ref, slots, send, recv):
        x, y, c, _ = _place()
        me = 4 * x + 2 * y + c
        slots[me] = p_ref[...]
        cps = []
        for k in range(1, N_DEV):
            px, py, pc = x ^ (k >> 2), y ^ ((k >> 1) & 1), c ^ (k & 1)
            cp = pltpu.make_async_remote_copy(
                src_ref=p_ref, dst_ref=slots.at[me], send_sem=send.at[k - 1], recv_sem=recv.at[k - 1],
                device_id=(px, py, pc), device_id_type=MESH)
            cp.start()
            cps.append(cp)
        for k in range(1, N_DEV):
            px, py, pc = x ^ (k >> 2), y ^ ((k >> 1) & 1), c ^ (k & 1)
            arrival = pltpu.make_async_remote_copy(
                src_ref=p_ref, dst_ref=slots.at[4 * px + 2 * py + pc], send_sem=send.at[k - 1],
                recv_sem=recv.at[k - 1], device_id=(px, py, pc), device_id_type=MESH)
            arrival.wait_recv()
            arrival.wait_send()
        acc = slots[0]
        for k in range(1, N_DEV):
            acc = acc + slots[k]
        o_ref[...] = acc

    vm = pl.BlockSpec(memory_space=pltpu.VMEM)
    return pl.pallas_call(
        body, name="allreduce_small", in_specs=[vm], out_specs=vm,
        out_shape=jax.ShapeDtypeStruct((rows, d), F32),
        scratch_shapes=[pltpu.VMEM((N_DEV, rows, d), F32), pltpu.SemaphoreType.DMA((N_DEV - 1,)),
                        pltpu.SemaphoreType.DMA((N_DEV - 1,))],
        compiler_params=pltpu.CompilerParams(has_side_effects=True),
    )(pack)


def _adamw(w, g, m, v, name):
    rows, cols = w.shape

    def fn(wv, gv, mv, vv):
        m2 = ADAM_B1 * mv + (1.0 - ADAM_B1) * gv
        v2 = ADAM_B2 * vv + (1.0 - ADAM_B2) * (gv * gv)
        m_hat = m2 / (1.0 - ADAM_B1 ** ADAM_STEP)
        v_hat = v2 / (1.0 - ADAM_B2 ** ADAM_STEP)
        delta = -ADAM_LR * (m_hat / (jnp.sqrt(v_hat) + ADAM_EPS) + ADAM_WD * wv)
        return delta, m2, v2

    ins = [(a, cols, 0) for a in (w, g, m, v)]
    return _rowwise(fn, ins, [(cols, F32)] * 3, rows=rows, tm=_tile(rows, max(8, 262144 // cols)), name=name)


BIG = ["w_in", "w_branch_a", "w_branch_b", "w_mix_out", "w_mem_q", "w_mem_kv", "w_mem_o", "w_ffn_in", "w_ffn_out"]
BIG_AXIS = {"w_in": 1, "w_branch_a": 1, "w_branch_b": 1, "w_mix_out": 0, "w_mem_q": 0, "w_mem_kv": 1,
            "w_mem_o": 0, "w_ffn_in": 1, "w_ffn_out": 0}
NORMS = ["norm_mix", "norm_mem_q", "norm_mem_kv", "norm_ffn", "norm_final"]
ORDER = ["norm_mix", "w_in", "conv_w", "w_branch_a", "w_branch_b", "w_mix_out", "norm_mem_q", "norm_mem_kv",
         "w_mem_q", "w_mem_kv", "w_mem_o", "norm_ffn", "w_ffn_in", "w_ffn_out", "norm_final"]


def _pack_small(vals, conv):
    d = vals[0].shape[-1]
    rows = [v.reshape(1, d) for v in vals]
    conv = jnp.pad(conv, ((0, 0), (0, d - conv.shape[1])))
    pad = jnp.zeros((SMALL_ROWS - len(rows) - CONV_K, d), F32)
    return jnp.concatenate(rows + [conv, pad], axis=0)


def kernel(x, mem, norm_mix, w_in, conv_w, w_branch_a, w_branch_b, w_mix_out, norm_mem_q, norm_mem_kv, w_mem_q, w_mem_kv, w_mem_o, norm_ffn, w_ffn_in, w_ffn_out, norm_final, loss_target, m_norm_mix, m_w_in, m_conv_w, m_w_branch_a, m_w_branch_b, m_w_mix_out, m_norm_mem_q, m_norm_mem_kv, m_w_mem_q, m_w_mem_kv, m_w_mem_o, m_norm_ffn, m_w_ffn_in, m_w_ffn_out, m_norm_final, v_norm_mix, v_w_in, v_conv_w, v_w_branch_a, v_w_branch_b, v_w_mix_out, v_norm_mem_q, v_norm_mem_kv, v_w_mem_q, v_w_mem_kv, v_w_mem_o, v_norm_ffn, v_w_ffn_in, v_w_ffn_out, v_norm_final):
    args = dict(locals())
    wts = {n: args[n] for n in ORDER}
    mom = {n: args["m_" + n] for n in ORDER}
    var = {n: args["v_" + n] for n in ORDER}
    x = x[0]
    mem = mem[0]
    target = loss_target[0]
    s, d = x.shape
    gains = {n: wts[n].reshape(1, d) for n in NORMS}
    chip = 2 * lax.axis_index("x") + lax.axis_index("y")
    core = lax.axis_index("c").astype(jnp.int32).reshape(1)
    place = jnp.stack([chip, lax.axis_index("c")]).astype(jnp.int32)

    conv_shard = jnp.pad(conv_w[0], ((0, CONV_ROWS - CONV_K), (0, 0)))
    placed = {n: _cast_place(wts[n][0], BIG_AXIS[n], place, BF, "place_" + n) for n in BIG}
    conv_placed = _cast_place(conv_shard, 1, place, F32, "place_conv_w")
    w_in_full, conv_full = _gather_weights([placed["w_in"], conv_placed], [BIG_AXIS["w_in"], 1], [True, False])
    W = {"w_in": w_in_full}
    later_w = [n for n in BIG if n != "w_in"]
    later_axes = [BIG_AXIS[n] for n in later_w]
    f_sems, later_bufs, f_token = _fetch_start([placed[n] for n in later_w], later_axes, (w_in_full,), "fetch_start")

    h1 = _rmsnorm(x, gains["norm_mix"], "norm_mix_fwd", after=(f_token,))
    proj = _matmul(h1, W["w_in"], name="in_proj")
    o_a, o_a32 = _sb_fwd(proj)
    y_b = _conv_fwd(proj, conv_full)
    later_bufs = _fetch_wait(f_sems, later_bufs, later_axes, (o_a, y_b), "fetch_wait")
    W.update(zip(later_w, _hand_on(later_bufs, later_axes)))
    br_a = _matmul(o_a, W["w_branch_a"], name="branch_a")
    br_b = _matmul(y_b, W["w_branch_b"], name="branch_b")
    ga_blk, gb_blk = 3, 4

    def merge(ga, gb, a, b):
        return jax.nn.sigmoid(ga.astype(F32)) * a.astype(F32) + jax.nn.sigmoid(gb.astype(F32)) * b.astype(F32)

    merged = _rowwise(merge, [(proj, d, ga_blk), (proj, d, gb_blk), (br_a, d, 0), (br_b, d, 0)], [(d, BF)],
                      rows=s, tm=512, name="merge_fwd")[0]
    x1 = _matmul(merged, W["w_mix_out"], out_dtype=F32, resid=x, name="mix_out")

    hq = _rmsnorm(x1, gains["norm_mem_q"], "norm_mem_q_fwd")
    mn = _rmsnorm(mem, gains["norm_mem_kv"], "norm_mem_kv_fwd")
    q_m = _matmul(hq, W["w_mem_q"], name="mem_q")
    kv = _matmul(mn, W["w_mem_kv"], name="mem_kv")
    o_m = _mem_fwd(q_m, kv)
    x2 = _matmul(o_m, W["w_mem_o"], out_dtype=F32, resid=x1, name="mem_o")

    hf = _rmsnorm(x2, gains["norm_ffn"], "norm_ffn_fwd")
    gu = _matmul(hf, W["w_ffn_in"], name="ffn_in")
    f = gu.shape[1] // 2

    def swiglu(gate, up):
        gate = gate.astype(F32)
        return gate * jax.nn.sigmoid(gate) * up.astype(F32)

    act = _rowwise(swiglu, [(gu, f, 0), (gu, f, 1)], [(f, BF)], rows=s, tm=512, name="swiglu_fwd")[0]
    x3 = _matmul(act, W["w_ffn_out"], out_dtype=F32, resid=x2, name="ffn_out")

    dx3, dg_final, loss_part = _loss_head(x3, gains["norm_final"], target, "loss_head")

    dact = _matmul(dx3, W["w_ffn_out"], tb=True, name="d_act")
    gw = {"w_ffn_out": _matmul(act, dx3, ta=True, tm=1408, tn=1024, tk=512, out_dtype=F32, name="gw_ffn_out")}

    def swiglu_bwd(gate, up, da):
        gate, up, da = gate.astype(F32), up.astype(F32), da.astype(F32)
        sg = jax.nn.sigmoid(gate)
        dgate = da * up * (sg * (1.0 + gate * (1.0 - sg)))
        return jnp.concatenate([dgate, da * (gate * sg)], axis=-1)

    dgu = _rowwise(swiglu_bwd, [(gu, f, 0), (gu, f, 1), (dact, f, 0)], [(2 * f, BF)], rows=s, tm=256,
                   name="swiglu_bwd")[0]
    dhf = _matmul(dgu, W["w_ffn_in"], tb=True, tk=1408, out_dtype=F32, name="d_hf")
    gw["w_ffn_in"] = _matmul(hf, dgu, ta=True, tm=1024, tn=1408, tk=512, out_dtype=F32, name="gw_ffn_in")
    dx2, dg_ffn = _rmsnorm_bwd(x2, gains["norm_ffn"], dhf, dx3, "norm_ffn_bwd")

    def reduce_start(names, tag):
        views = []
        for n in names:
            r, cdim = gw[n].shape
            views.append(gw[n].reshape(1, 2, r // 2, cdim) if BIG_AXIS[n] == 1
                         else gw[n].reshape(N_CHIPS, 2, r // (2 * N_CHIPS), cdim))
        got = _pair_exchange(views, "grad_pair_exchange_" + tag)
        sums = [_pair_add(v, g, core, "pair_add_" + n) for n, v, g in zip(names, views, got)]
        group_axes = [BIG_AXIS[n] for n in names]
        sems, sums, lands, token = _chip_exchange_start(sums, group_axes, "grad_chip_start_" + tag)
        return (names, group_axes, sems, sums, lands), token

    def reduce_finish(group, after, tag):
        names, group_axes, sems, sums, lands = group
        sums, slots = _chip_exchange_wait(sems, sums, lands, group_axes, after, "grad_chip_wait_" + tag)
        return [_chip_sum(sm, sl, BIG_AXIS[n], place, "chip_sum_" + n) for n, sm, sl in zip(names, sums, slots)]

    group_ffn, token_ffn = reduce_start(["w_ffn_in", "w_ffn_out"], "ffn")

    do_m = _matmul(dx2, W["w_mem_o"], tb=True, after=(token_ffn,), name="d_o_m")
    gw["w_mem_o"] = _matmul(o_m, dx2, ta=True, tk=512, tn=1024, out_dtype=F32, name="gw_mem_o")
    dq_m, dk_m, dv_m = _mem_bwd(q_m, kv, do_m)
    dkv = jnp.concatenate([dk_m, dv_m], axis=-1)
    dhq = _matmul(dq_m, W["w_mem_q"], tb=True, out_dtype=F32, name="d_hq")
    gw["w_mem_q"] = _matmul(hq, dq_m, ta=True, tk=512, tn=1024, out_dtype=F32, name="gw_mem_q")
    dmn = _matmul(dkv, W["w_mem_kv"], tb=True, out_dtype=F32, name="d_mn")
    gw["w_mem_kv"] = _matmul(mn, dkv, ta=True, tn=1024, out_dtype=F32, name="gw_mem_kv")
    _, dg_kv = _rmsnorm_bwd(mem, gains["norm_mem_kv"], dmn, None, "norm_mem_kv_bwd")
    dx1, dg_q = _rmsnorm_bwd(x1, gains["norm_mem_q"], dhq, dx2, "norm_mem_q_bwd")

    dmerged = _matmul(dx1, W["w_mix_out"], tb=True, name="d_merged")
    gw["w_mix_out"] = _matmul(merged, dx1, ta=True, tk=512, tn=1024, out_dtype=F32, name="gw_mix_out")

    def merge_bwd(ga, gb, a, b, dm):
        dm = dm.astype(F32)
        sa, sb = jax.nn.sigmoid(ga.astype(F32)), jax.nn.sigmoid(gb.astype(F32))
        a, b = a.astype(F32), b.astype(F32)
        return dm * sa, dm * sb, dm * a * (sa * (1.0 - sa)), dm * b * (sb * (1.0 - sb))

    dbr_a, dbr_b, dga, dgb = _rowwise(
        merge_bwd, [(proj, d, ga_blk), (proj, d, gb_blk), (br_a, d, 0), (br_b, d, 0), (dmerged, d, 0)],
        [(d, BF)] * 4, rows=s, tm=256, name="merge_bwd")
    do_a = _matmul(dbr_a, W["w_branch_a"], tb=True, name="d_o_a")
    gw["w_branch_a"] = _matmul(o_a, dbr_a, ta=True, tk=512, tn=1024, out_dtype=F32, name="gw_branch_a")
    dy_b = _matmul(dbr_b, W["w_branch_b"], tb=True, name="d_y_b")
    gw["w_branch_b"] = _matmul(y_b, dbr_b, ta=True, tk=512, tn=1024, out_dtype=F32, name="gw_branch_b")
    group_mid, token_mid = reduce_start(
        ["w_mem_o", "w_mem_q", "w_mem_kv", "w_mix_out", "w_branch_a", "w_branch_b"], "mid")
    du, dgate_b, dgate_c, dconv = _conv_bwd(proj, conv_full, dy_b)
    dq, dk, dv = _sb_bwd(proj, o_a32, do_a, after=(token_mid,))

    def assemble(*parts):
        return jnp.concatenate([p.astype(BF) for p in parts], axis=-1)

    hw = dq.shape[1]
    dproj = _rowwise(assemble, [(t, hw, 0) for t in (dq, dk, dv, du, dgate_b, dgate_c)] + [(dga, d, 0), (dgb, d, 0)],
                     [(proj.shape[1], BF)], rows=s, tm=256, name="assemble_dproj")[0]
    gw["w_in"] = _matmul(h1, dproj, ta=True, tk=512, tn=1280, out_dtype=F32, name="gw_in")
    group_in, token_in = reduce_start(["w_in"], "in")
    dh1 = _matmul(dproj, W["w_in"], tb=True, tk=1280, out_dtype=F32, after=(token_in,), name="d_h1")
    grad_x, dg_mix = _rmsnorm_bwd(x, gains["norm_mix"], dh1, dx1, "norm_mix_bwd")

    halves = {}
    for group, tag in ((group_ffn, "ffn"), (group_mid, "mid"), (group_in, "in")):
        halves.update(zip(group[0], reduce_finish(group, (grad_x,), tag)))
    both = _half_swap([halves[n] for n in BIG])
    grads = {n: b.reshape(wts[n].shape[1:]) for n, b in zip(BIG, both)}

    small_g = [dg_mix, dg_q, dg_kv, dg_ffn, dg_final]
    pack = _pack_small(small_g, dconv[:CONV_K])
    pack = pack.at[ROW_LOSS].set(jnp.broadcast_to(loss_part[0, :1], (d,)))
    red = _allreduce_small(pack)
    loss = red[ROW_LOSS, 0]
    cw = conv_w.shape[2]
    conv_g = lax.dynamic_slice(red, (ROW_CONV, chip * cw), (CONV_K, cw))
    small_grad = _pack_small([red[i] for i in range(len(NORMS))], conv_g)
    small = [_pack_small([t[n] for n in NORMS], t["conv_w"][0]) for t in (wts, mom, var)]
    s_delta, s_m, s_v = _adamw(small[0], small_grad, small[1], small[2], "adamw_small")

    out = {"grad": {}, "delta": {}, "new_m": {}, "new_v": {}}
    for n in BIG:
        shp = wts[n].shape
        dl, m2, v2 = _adamw(wts[n][0], grads[n], mom[n][0], var[n][0], "adamw_" + n)
        out["grad"][n] = grads[n].reshape(shp)
        out["delta"][n], out["new_m"][n], out["new_v"][n] = dl.reshape(shp), m2.reshape(shp), v2.reshape(shp)
    for key, blk in (("grad", small_grad), ("delta", s_delta), ("new_m", s_m), ("new_v", s_v)):
        for i, n in enumerate(NORMS):
            out[key][n] = blk[i].reshape(wts[n].shape)
        out[key]["conv_w"] = blk[ROW_CONV:ROW_CONV + CONV_K, :cw].reshape(conv_w.shape)

    return (loss, grad_x[None], *[out["grad"][n] for n in ORDER], *[out["delta"][n] for n in ORDER],
            *[out["new_m"][n] for n in ORDER], *[out["new_v"][n] for n in ORDER])
```

```python
import functools
import math

import jax
import jax.numpy as jnp
from jax import lax
from jax.experimental import pallas as pl
from jax.experimental.pallas import tpu as pltpu

BF = jnp.bfloat16
F32 = jnp.float32
MESH = pl.DeviceIdType.MESH

SB_HEAD_DIM = 64
LANES = 128
MEM_HEADS = 4
CONV_K = 3
CONV_ROWS = 8
EPS = 1e-6
N_CHIPS = 4
N_DEV = 8
VMEM_LIMIT = 56 * 1024 * 1024

ADAM_LR = 0.001
ADAM_B1 = 0.9
ADAM_B2 = 0.999
ADAM_EPS = 1e-08
ADAM_WD = 0.01
ADAM_STEP = 10

SMALL_ROWS = 16
ROW_CONV = 5
ROW_LOSS = 8


def _params(sem=None, **kw):
    return pltpu.CompilerParams(dimension_semantics=sem, vmem_limit_bytes=VMEM_LIMIT, **kw)


def _tile(dim, pref):
    if dim <= pref:
        return dim
    for step in (LANES, 8):
        t = (pref // step) * step
        while t >= step:
            if dim % t == 0:
                return t
            t -= step
    raise ValueError(f"no tile of {dim} under {pref}")


def _matmul(a, b, *, ta=False, tb=False, tm=1024, tn=512, tk=None, out_dtype=BF, resid=None, after=(), name):
    if ta:
        kdim, m = a.shape
    else:
        m, kdim = a.shape
    n = b.shape[0] if tb else b.shape[1]
    tm, tn = _tile(m, tm), _tile(n, tn)
    tk = _tile(kdim, tk or kdim)
    nk = kdim // tk
    a_spec = pl.BlockSpec((tk, tm), lambda i, j, k: (k, i)) if ta else pl.BlockSpec((tm, tk), lambda i, j, k: (i, k))
    b_spec = pl.BlockSpec((tn, tk), lambda i, j, k: (j, k)) if tb else pl.BlockSpec((tk, tn), lambda i, j, k: (k, j))
    o_spec = pl.BlockSpec((tm, tn), lambda i, j, k: (i, j))
    dims = (((0 if ta else 1,), (1 if tb else 0,)), ((), ()))
    has_res = resid is not None

    def body(*refs):
        a_ref, b_ref = refs[0], refs[1]
        o_ref = refs[2 + has_res + len(after)]
        av, bv = a_ref[...], b_ref[...]
        if av.dtype != BF:
            av = av.astype(BF)
        if bv.dtype != BF:
            bv = bv.astype(BF)
        p = lax.dot_general(av, bv, dims, preferred_element_type=F32)

        def finish(acc):
            if has_res:
                acc = refs[2][...] + acc
            o_ref[...] = acc.astype(o_ref.dtype)

        if nk == 1:
            finish(p)
        else:
            acc_ref = refs[-1]
            k = pl.program_id(2)

            @pl.when(k == 0)
            def _():
                acc_ref[...] = p

            @pl.when(k > 0)
            def _():
                acc_ref[...] += p

            @pl.when(k == nk - 1)
            def _():
                finish(acc_ref[...])

    return pl.pallas_call(
        body, name=name, grid=(m // tm, n // tn, nk),
        in_specs=[a_spec, b_spec] + ([o_spec] if has_res else []) + [HBM_SPEC] * len(after),
        out_specs=o_spec, out_shape=jax.ShapeDtypeStruct((m, n), out_dtype),
        scratch_shapes=[pltpu.VMEM((tm, tn), F32)] if nk > 1 else [],
        compiler_params=_params(("parallel", "parallel", "arbitrary")),
    )(*([a, b] + ([resid] if has_res else []) + list(after)))


def _rowwise(fn, ins, outs, *, rows, tm, name, accs=(), after=()):
    tm = _tile(rows, tm)
    in_specs, args = [], []
    for arr, cols, cb in ins:
        if cols is None:
            in_specs.append(pl.BlockSpec(arr.shape, lambda i, nd=arr.ndim: (0,) * nd))
        else:
            in_specs.append(pl.BlockSpec((tm, cols), lambda i, cb=cb: (i, cb)))
        args.append(arr)
    out_specs = [pl.BlockSpec((tm, cols), lambda i: (i, 0)) for cols, _ in outs]
    out_shape = [jax.ShapeDtypeStruct((rows, cols), dt) for cols, dt in outs]
    for r, c in accs:
        out_specs.append(pl.BlockSpec((r, c), lambda i: (0, 0)))
        out_shape.append(jax.ShapeDtypeStruct((r, c), F32))
    n_in, n_out = len(ins), len(outs)
    in_specs += [HBM_SPEC] * len(after)
    args += list(after)

    def body(*refs):
        res = fn(*[r[...] for r in refs[:n_in]])
        if not isinstance(res, (tuple, list)):
            res = (res,)
        refs = refs[n_in + len(after):]
        for o_ref, val in zip(refs[:n_out], res[:n_out]):
            o_ref[...] = val.astype(o_ref.dtype)
        first = pl.program_id(0) == 0
        for a_ref, val in zip(refs[n_out:], res[n_out:]):
            @pl.when(first)
            def _(a_ref=a_ref, val=val):
                a_ref[...] = val

            @pl.when(jnp.logical_not(first))
            def _(a_ref=a_ref, val=val):
                a_ref[...] += val

    res = pl.pallas_call(
        body, name=name, grid=(rows // tm,), in_specs=in_specs, out_specs=out_specs, out_shape=out_shape,
        compiler_params=_params(("arbitrary",) if accs else ("parallel",)),
    )(*args)
    return res


def _rstd(xf):
    return lax.rsqrt(jnp.mean(xf * xf, axis=-1, keepdims=True) + EPS)


def _rmsnorm(x, g, name, after=()):
    rows, d = x.shape
    return _rowwise(lambda xv, gv: xv * _rstd(xv) * gv, [(x, d, 0), (g, None, None)], [(d, BF)],
                    rows=rows, tm=512, name=name, after=after)[0]


def _rmsnorm_bwd(x, g, dy, resid, name, bf_copy=False):
    rows, d = x.shape

    def fn(xv, gv, dyv, *rest):
        dyv = dyv.astype(F32)
        r = _rstd(xv)
        xn = xv * r
        dxn = dyv * gv
        dx = r * (dxn - xn * jnp.mean(dxn * xn, axis=-1, keepdims=True))
        if rest:
            dx = rest[0] + dx
        return (dx,) * (1 + bf_copy) + (jnp.sum(dyv * xn, axis=0, keepdims=True),)

    ins = [(x, d, 0), (g, None, None), (dy, d, 0)] + ([(resid, d, 0)] if resid is not None else [])
    outs = [(d, F32)] + ([(d, BF)] if bf_copy else [])
    return _rowwise(fn, ins, outs, rows=rows, tm=512, name=name, accs=[(1, d)])


def _loss_head(x, g, target, name):
    rows, d = x.shape

    def fn(xv, gv, tv):
        r = _rstd(xv)
        xn = xv * r
        err = xn * gv - tv
        per_tok = jnp.mean(err * err, axis=-1, keepdims=True)
        loss = 0.5 * jnp.sum(per_tok, axis=0, keepdims=True)
        dyv = err * (1.0 / d)
        dxn = dyv * gv
        dx = r * (dxn - xn * jnp.mean(dxn * xn, axis=-1, keepdims=True))
        return dx, dx, jnp.sum(dyv * xn, axis=0, keepdims=True), jnp.broadcast_to(loss, (1, LANES))

    return _rowwise(fn, [(x, d, 0), (g, None, None), (target, d, 0)], [(d, F32), (d, BF)],
                    rows=rows, tm=512, name=name, accs=[(1, d), (1, LANES)])


SB_TK = 128
SB_KT = 4


def _sb_consts(tq):
    tk = SB_TK
    diff = lax.broadcasted_iota(jnp.int32, (tq, tk), 1) - lax.broadcasted_iota(jnp.int32, (tq, tk), 0)
    rj = lax.broadcasted_iota(jnp.int32, (2 * tk, 2 * tk), 0) & (tk - 1)
    cj = lax.broadcasted_iota(jnp.int32, (2 * tk, 2 * tk), 1)
    ones_half = cj >= tk
    later = jnp.where((rj > cj) | ones_half, 1.0, 0.0).astype(BF)
    later_incl = jnp.where((rj >= cj) | ones_half, 1.0, 0.0).astype(BF)
    return diff, later, later_incl


def _split_dot(val, rhs_twice):
    hi = val.astype(BF)
    lo = (val - hi.astype(F32)).astype(BF)
    return jnp.dot(jnp.concatenate([hi, lo], axis=1), rhs_twice, preferred_element_type=F32)


def _log_terms(z):
    sp = jnp.maximum(z, 0.0) + jnp.log(1.0 + jnp.exp(-jnp.abs(z)))
    return z - sp, sp


NT = (((1,), (1,)), ((), ()))
TN = (((0,), (0,)), ((), ()))


def _head_lane_masks(rows):
    lane = lax.broadcasted_iota(jnp.int32, (rows, LANES), 1)
    first = jnp.where(lane < SB_HEAD_DIM, 1.0, 0.0)
    return first.astype(BF), (1.0 - first).astype(BF)


def _both_heads(tile, masks):
    return jnp.concatenate([tile * masks[0], tile * masks[1]], axis=0)


def _sb_fwd(proj):
    s = proj.shape[0]
    tk, tq = SB_TK, SB_KT * SB_TK
    n_pairs = 4
    scale = 1.0 / math.sqrt(SB_HEAD_DIM)

    def body(q_ref, k_ref, v_ref, o_ref, o32_ref):
        i = pl.program_id(1)
        diff, later, _ = _sb_consts(tq)
        qs = (q_ref[...].astype(F32) * scale).astype(BF)
        lane_masks = _head_lane_masks(tk)

        def step(g, state, masked):
            tiles = list(reversed(range(SB_KT)))
            chains = [(t, h) for t in tiles for h in range(2)]
            rows = {t: pl.ds(pl.multiple_of((g * SB_KT + t) * tk, tk), tk) for t in tiles}
            ks = {t: _both_heads(k_ref[rows[t], :], lane_masks) for t in tiles}
            vs = {t: _both_heads(v_ref[rows[t], :], lane_masks) for t in tiles}
            allowed = {t: diff < -t * tk for t in tiles}
            zs = {t: lax.dot_general(qs, ks[t], NT, preferred_element_type=F32) for t in tiles}
            logs = {}
            for t, h in chains:
                log_b, sp = _log_terms(zs[t][:, h * tk:(h + 1) * tk])
                logs[t, h] = (log_b, jnp.where(allowed[t], sp, 0.0) if masked else sp)
            sums = {c: _split_dot(logs[c][1], later) for c in chains}
            carries = list(state[0])
            ws = {}
            for t, h in chains:
                w = jnp.exp(logs[t, h][0] - (sums[t, h][:, :tk] + carries[h]))
                ws[t, h] = (jnp.where(allowed[t], w, 0.0) if masked else w).astype(BF)
                carries[h] = carries[h] + sums[t, h][:, tk:]
            acc = state[1]
            for t in tiles:
                acc = acc + jnp.dot(jnp.concatenate([ws[t, 0], ws[t, 1]], axis=1), vs[t],
                                    preferred_element_type=F32)
            return tuple(carries), acc

        zero = jnp.zeros((tq, LANES), F32)
        state = step(i, ((zero, zero), zero), True)
        state = lax.fori_loop(0, i, lambda gg, st: step(i - 1 - gg, st, False), state)
        o_ref[...] = state[1].astype(o_ref.dtype)
        o32_ref[...] = state[1]

    tok = pl.BlockSpec((tq, LANES), lambda p, i: (i, p))
    return pl.pallas_call(
        body, name="sb_attn_fwd", grid=(n_pairs, s // tq),
        in_specs=[tok,
                  pl.BlockSpec((s, LANES), lambda p, i: (0, n_pairs + p)),
                  pl.BlockSpec((s, LANES), lambda p, i: (0, 2 * n_pairs + p))],
        out_specs=[tok, tok],
        out_shape=[jax.ShapeDtypeStruct((s, n_pairs * LANES), BF), jax.ShapeDtypeStruct((s, n_pairs * LANES), F32)],
        compiler_params=_params(("parallel", "arbitrary")),
    )(proj, proj, proj)


def _sb_bwd(proj, o32, do_a, after=()):
    s = proj.shape[0]
    tk, tq = SB_TK, SB_KT * SB_TK
    n_pairs = 4
    scale = 1.0 / math.sqrt(SB_HEAD_DIM)

    def body(q_ref, k_ref, v_ref, o_ref, do_ref, *rest):
        dq_ref, dk_ref, dv_ref = rest[len(after):]
        i = pl.program_id(1)

        @pl.when(i == 0)
        def _():
            dk_ref[...] = jnp.zeros_like(dk_ref)
            dv_ref[...] = jnp.zeros_like(dv_ref)

        diff, later, later_incl = _sb_consts(tq)
        qs = (q_ref[...].astype(F32) * scale).astype(BF)
        do2 = do_ref[...]
        prod = do2.astype(F32) * o_ref[...]
        lane_masks = _head_lane_masks(tk)
        first_head = lax.broadcasted_iota(jnp.int32, (tq, LANES), 1) < SB_HEAD_DIM
        totals = [jnp.broadcast_to(jnp.sum(jnp.where(keep, prod, 0.0), axis=-1, keepdims=True), (tq, tk))
                  for keep in (first_head, jnp.logical_not(first_head))]
        first_head_k = first_head[:tk]

        def step(g_idx, state, masked):
            tiles = list(reversed(range(SB_KT)))
            chains = [(t, h) for t in tiles for h in range(2)]
            rows = {t: pl.ds(pl.multiple_of((g_idx * SB_KT + t) * tk, tk), tk) for t in tiles}
            ks = {t: _both_heads(k_ref[rows[t], :], lane_masks) for t in tiles}
            vs = {t: _both_heads(v_ref[rows[t], :], lane_masks) for t in tiles}
            allowed = {t: diff < -t * tk for t in tiles}
            zs = {t: lax.dot_general(qs, ks[t], NT, preferred_element_type=F32) for t in tiles}
            dws = {t: lax.dot_general(do2, vs[t], NT, preferred_element_type=F32) for t in tiles}
            logs = {}
            for t, h in chains:
                log_b, sp = _log_terms(zs[t][:, h * tk:(h + 1) * tk])
                logs[t, h] = (log_b, jnp.where(allowed[t], sp, 0.0) if masked else sp)
            sums = {c: _split_dot(logs[c][1], later) for c in chains}
            c_log, c_g = list(state[0]), list(state[1])
            ws, gs = {}, {}
            for t, h in chains:
                w = jnp.exp(logs[t, h][0] - (sums[t, h][:, :tk] + c_log[h]))
                ws[t, h] = (jnp.where(allowed[t], w, 0.0) if masked else w).astype(BF)
                c_log[h] = c_log[h] + sums[t, h][:, tk:]
                gs[t, h] = ws[t, h].astype(F32) * dws[t][:, h * tk:(h + 1) * tk]
            gsums = {c: _split_dot(gs[c], later_incl) for c in chains}
            dzs = {}
            for t, h in chains:
                beta = jnp.exp(logs[t, h][0])
                earlier = totals[h] - (gsums[t, h][:, :tk] + c_g[h])
                dz = gs[t, h] * (1.0 - beta) - earlier * beta
                dzs[t, h] = (jnp.where(allowed[t], dz, 0.0) if masked else dz).astype(BF)
                c_g[h] = c_g[h] + gsums[t, h][:, tk:]
            dq = state[2]
            for t in tiles:
                dz_both = jnp.concatenate([dzs[t, 0], dzs[t, 1]], axis=1)
                w_both = jnp.concatenate([ws[t, 0], ws[t, 1]], axis=1)
                dq = dq + jnp.dot(dz_both, ks[t], preferred_element_type=F32)
                dk2 = lax.dot_general(dz_both, qs, TN, preferred_element_type=F32)
                dv2 = lax.dot_general(w_both, do2, TN, preferred_element_type=F32)
                dk_ref[rows[t], :] += jnp.where(first_head_k, dk2[:tk], dk2[tk:])
                dv_ref[rows[t], :] += jnp.where(first_head_k, dv2[:tk], dv2[tk:])
            return tuple(c_log), tuple(c_g), dq

        zero = jnp.zeros((tq, LANES), F32)
        state = step(i, ((zero, zero), (zero, zero), zero), True)
        state = lax.fori_loop(0, i, lambda gg, st: step(i - 1 - gg, st, False), state)
        dq_ref[...] = (state[2] * scale).astype(dq_ref.dtype)

    width = n_pairs * LANES
    return pl.pallas_call(
        body, name="sb_attn_bwd", grid=(n_pairs, s // tq),
        in_specs=[pl.BlockSpec((tq, LANES), lambda p, i: (i, p)),
                  pl.BlockSpec((s, LANES), lambda p, i: (0, n_pairs + p)),
                  pl.BlockSpec((s, LANES), lambda p, i: (0, 2 * n_pairs + p)),
                  pl.BlockSpec((tq, LANES), lambda p, i: (i, p)),
                  pl.BlockSpec((tq, LANES), lambda p, i: (i, p))] + [HBM_SPEC] * len(after),
        out_specs=[pl.BlockSpec((tq, LANES), lambda p, i: (i, p)),
                   pl.BlockSpec((s, LANES), lambda p, i: (0, p)),
                   pl.BlockSpec((s, LANES), lambda p, i: (0, p))],
        out_shape=[jax.ShapeDtypeStruct((s, width), BF), jax.ShapeDtypeStruct((s, width), F32),
                   jax.ShapeDtypeStruct((s, width), F32)],
        compiler_params=_params(("parallel", "arbitrary")),
    )(proj, proj, proj, o32, do_a, *after)


CONV_COL0 = 12


def _shift_rows(v, k):
    n = v.shape[0]
    row = lax.broadcasted_iota(jnp.int32, v.shape, 0)
    rolled = pltpu.roll(v, k % n, axis=0)
    keep = row >= k if k > 0 else row < n + k
    return jnp.where(keep, rolled, 0.0)


def _conv_specs(s):
    return [pl.BlockSpec((s, LANES), lambda cb: (0, CONV_COL0 + cb)),
            pl.BlockSpec((s, LANES), lambda cb: (0, CONV_COL0 + 4 + cb)),
            pl.BlockSpec((s, LANES), lambda cb: (0, CONV_COL0 + 8 + cb)),
            pl.BlockSpec((CONV_ROWS, LANES), lambda cb: (0, cb))]


def _conv_fwd(proj, conv_w):
    s = proj.shape[0]

    def body(u_ref, gb_ref, gc_ref, w_ref, y_ref):
        cu = gc_ref[...].astype(F32) * u_ref[...].astype(F32)
        w = w_ref[...]
        y = w[0:1] * _shift_rows(cu, 2) + w[1:2] * _shift_rows(cu, 1) + w[2:3] * cu
        y_ref[...] = (gb_ref[...].astype(F32) * y).astype(y_ref.dtype)

    return pl.pallas_call(
        body, name="conv_fwd", grid=(4,), in_specs=_conv_specs(s),
        out_specs=pl.BlockSpec((s, LANES), lambda cb: (0, cb)),
        out_shape=jax.ShapeDtypeStruct((s, 4 * LANES), BF),
        compiler_params=_params(("parallel",)),
    )(proj, proj, proj, conv_w)


def _conv_bwd(proj, conv_w, dy):
    s = proj.shape[0]

    def body(u_ref, gb_ref, gc_ref, w_ref, dy_ref, du_ref, dgb_ref, dgc_ref, dw_ref):
        u, gc = u_ref[...].astype(F32), gc_ref[...].astype(F32)
        dyv = dy_ref[...].astype(F32)
        w = w_ref[...]
        cu = gc * u
        cu1, cu2 = _shift_rows(cu, 1), _shift_rows(cu, 2)
        conv = w[0:1] * cu2 + w[1:2] * cu1 + w[2:3] * cu
        dgb_ref[...] = (dyv * conv).astype(dgb_ref.dtype)
        dc = dyv * gb_ref[...].astype(F32)
        dcu = w[2:3] * dc + w[1:2] * _shift_rows(dc, -1) + w[0:1] * _shift_rows(dc, -2)
        dgc_ref[...] = (dcu * u).astype(dgc_ref.dtype)
        du_ref[...] = (dcu * gc).astype(du_ref.dtype)
        tap_row = lax.broadcasted_iota(jnp.int32, (CONV_ROWS, LANES), 0)
        dw = jnp.zeros((CONV_ROWS, LANES), F32)
        for t, shifted in enumerate((cu2, cu1, cu)):
            dw = jnp.where(tap_row == t, jnp.sum(dc * shifted, axis=0, keepdims=True), dw)
        dw_ref[...] = dw

    col = pl.BlockSpec((s, LANES), lambda cb: (0, cb))
    act = jax.ShapeDtypeStruct((s, 4 * LANES), BF)
    return pl.pallas_call(
        body, name="conv_bwd", grid=(4,), in_specs=_conv_specs(s) + [col],
        out_specs=[col, col, col, pl.BlockSpec((CONV_ROWS, LANES), lambda cb: (0, cb))],
        out_shape=[act, act, act, jax.ShapeDtypeStruct((CONV_ROWS, 4 * LANES), F32)],
        compiler_params=_params(("parallel",)),
    )(proj, proj, proj, conv_w, dy)


def _mem_probs(q, k, scale):
    sc = lax.dot_general(q, k, NT, preferred_element_type=F32) * scale
    p = jnp.exp(sc - jnp.max(sc, axis=-1, keepdims=True))
    return p / jnp.sum(p, axis=-1, keepdims=True)


def _mem_fwd(q_m, kv, tq=512):
    s, d = q_m.shape
    mlen = kv.shape[0]
    hd = d // MEM_HEADS
    tq = _tile(s, tq)
    scale = 1.0 / math.sqrt(hd)

    def body(q_ref, k_ref, v_ref, o_ref):
        p = _mem_probs(q_ref[...], k_ref[...], scale)
        o_ref[...] = jnp.dot(p.astype(BF), v_ref[...], preferred_element_type=F32).astype(o_ref.dtype)

    return pl.pallas_call(
        body, name="mem_attn_fwd", grid=(MEM_HEADS, s // tq),
        in_specs=[pl.BlockSpec((tq, hd), lambda h, i: (i, h)),
                  pl.BlockSpec((mlen, hd), lambda h, i: (0, h)),
                  pl.BlockSpec((mlen, hd), lambda h, i: (0, MEM_HEADS + h))],
        out_specs=pl.BlockSpec((tq, hd), lambda h, i: (i, h)),
        out_shape=jax.ShapeDtypeStruct((s, d), BF),
        compiler_params=_params(("parallel", "parallel")),
    )(q_m, kv, kv)


def _mem_bwd(q_m, kv, do_m, tq=512):
    s, d = q_m.shape
    mlen = kv.shape[0]
    hd = d // MEM_HEADS
    tq = _tile(s, tq)
    scale = 1.0 / math.sqrt(hd)

    def body(q_ref, k_ref, v_ref, do_ref, dq_ref, dk_ref, dv_ref):
        q, k, v, do = q_ref[...], k_ref[...], v_ref[...], do_ref[...]
        p = _mem_probs(q, k, scale)
        dp = lax.dot_general(do, v, NT, preferred_element_type=F32)
        ds = p * (dp - jnp.sum(dp * p, axis=-1, keepdims=True)) * scale
        dsb = ds.astype(BF)
        dq_ref[...] = jnp.dot(dsb, k, preferred_element_type=F32).astype(dq_ref.dtype)
        dk = lax.dot_general(dsb, q, TN, preferred_element_type=F32)
        dv = lax.dot_general(p.astype(BF), do, TN, preferred_element_type=F32)
        first = pl.program_id(1) == 0

        @pl.when(first)
        def _():
            dk_ref[...] = dk
            dv_ref[...] = dv

        @pl.when(jnp.logical_not(first))
        def _():
            dk_ref[...] += dk
            dv_ref[...] += dv

    tok = pl.BlockSpec((tq, hd), lambda h, i: (i, h))
    memb = pl.BlockSpec((mlen, hd), lambda h, i: (0, h))
    return pl.pallas_call(
        body, name="mem_attn_bwd", grid=(MEM_HEADS, s // tq),
        in_specs=[tok, memb, pl.BlockSpec((mlen, hd), lambda h, i: (0, MEM_HEADS + h)), tok],
        out_specs=[tok, memb, memb],
        out_shape=[jax.ShapeDtypeStruct((s, d), BF), jax.ShapeDtypeStruct((mlen, d), F32),
                   jax.ShapeDtypeStruct((mlen, d), F32)],
        compiler_params=_params(("parallel", "arbitrary")),
    )(q_m, kv, kv, do_m)


def _place():
    x, y, c = lax.axis_index("x"), lax.axis_index("y"), lax.axis_index("c")
    other_chips = [(1 - x, y), (x, 1 - y), (1 - x, 1 - y)]
    return x, y, c, other_chips


def _chip_no(cx, cy):
    return 2 * cx + cy


HBM_SPEC = pl.BlockSpec(memory_space=pl.ANY)


def _cast_place(shard, axis, place, dtype, name):
    r, c = shard.shape
    tr = _tile(r, max(16, 1048576 // c))
    nblk = r // tr
    if axis == 1:
        full, out_map = (r, N_CHIPS * c), lambda i, pref: (i, pref[0])
    else:
        full, out_map = (N_CHIPS * r, c), lambda i, pref: (pref[0] * nblk + i, 0)

    def body(pref, s_ref, o_ref):
        o_ref[...] = s_ref[...].astype(o_ref.dtype)

    return pl.pallas_call(
        body, name=name,
        grid_spec=pltpu.PrefetchScalarGridSpec(
            num_scalar_prefetch=1, grid=(nblk,),
            in_specs=[pl.BlockSpec((tr, c), lambda i, pref: (i, 0))],
            out_specs=pl.BlockSpec((tr, c), out_map)),
        out_shape=jax.ShapeDtypeStruct(full, dtype),
        compiler_params=_params(("parallel",)),
    )(place, shard)


def _region(ref, axis, chip_no, half):
    width = ref.shape[axis] // N_CHIPS
    start = pl.multiple_of(chip_no * width, width)
    if axis == 1:
        if half is None:
            return ref.at[:, pl.ds(start, width)]
        hr = ref.shape[0] // 2
        return ref.at[pl.ds(pl.multiple_of(half * hr, hr), hr), pl.ds(start, width)]
    if half is None:
        return ref.at[pl.ds(start, width), :]
    hr = width // 2
    return ref.at[pl.ds(pl.multiple_of(start + half * hr, hr), hr), :]


def _gather_weights(fulls, axes, split):
    n = len(fulls)

    def body(*refs):
        outs = refs[n:2 * n]
        send, recv, fsend, frecv = refs[2 * n:]
        x, y, c, others = _place()
        me = _chip_no(x, y)
        sibling = (x, y, 1 - c)

        def copy(w, chip_no, half, sems, p, to):
            reg = _region(outs[w], axes[w], chip_no, half)
            return pltpu.make_async_remote_copy(
                src_ref=reg, dst_ref=reg, send_sem=sems[0].at[w, p], recv_sem=sems[1].at[w, p],
                device_id=to, device_id_type=MESH)

        for w in range(n):
            for p, chip in enumerate(others):
                copy(w, me, c if split[w] else None, (send, recv), p, (chip[0], chip[1], c)).start()
        for w in range(n):
            for p, chip in enumerate(others):
                half = c if split[w] else None
                copy(w, _chip_no(*chip), half, (send, recv), p, (chip[0], chip[1], c)).wait_recv()
                if split[w]:
                    copy(w, _chip_no(*chip), c, (fsend, frecv), p, sibling).start()
        for w in range(n):
            for p, chip in enumerate(others):
                copy(w, me, c if split[w] else None, (send, recv), p, (chip[0], chip[1], c)).wait_send()
                if split[w]:
                    handed = copy(w, _chip_no(*chip), 1 - c, (fsend, frecv), p, sibling)
                    handed.wait_recv()
                    handed.wait_send()

    return pl.pallas_call(
        body, name="gather_weights",
        in_specs=[HBM_SPEC] * n, out_specs=[HBM_SPEC] * n,
        out_shape=[jax.ShapeDtypeStruct(f.shape, f.dtype) for f in fulls],
        input_output_aliases={i: i for i in range(n)},
        scratch_shapes=[pltpu.SemaphoreType.DMA((n, 3))] * 4,
        compiler_params=pltpu.CompilerParams(has_side_effects=True),
    )(*fulls)


SEM_SPEC = pl.BlockSpec(memory_space=pltpu.SEMAPHORE)
IN_HBM = pl.BlockSpec(memory_space=pltpu.HBM)
FLOWS = pltpu.CompilerParams(has_side_effects=pltpu.SideEffectType.DATAFLOW_SIDE_EFFECTING)
TOKEN = jax.ShapeDtypeStruct((8, LANES), F32)


def _in_hbm(arrays):
    return [pltpu.with_memory_space_constraint(a, pltpu.HBM) for a in arrays]


def _hbm_like(arrays):
    return [pltpu.HBM(a.shape, a.dtype) for a in arrays]


def _fetch_copy(refs, axes, send, recv, w, p, chip, c, arriving):
    owner = _chip_no(*chip) if arriving else _chip_no(lax.axis_index("x"), lax.axis_index("y"))
    reg = _region(refs[w], axes[w], owner, c)
    return pltpu.make_async_remote_copy(
        src_ref=reg, dst_ref=reg, send_sem=send[p], recv_sem=recv[p],
        device_id=(chip[0], chip[1], c), device_id_type=MESH)


N_PEERS = N_CHIPS - 1
PEER_SEMS = [pltpu.SemaphoreType.DMA(())] * (2 * N_PEERS)


def _fetch_start(fulls, axes, after, name):
    n = len(fulls)

    def body(*refs):
        ins = refs[:n]
        sems = refs[n + len(after):]
        send, recv = sems[:N_PEERS], sems[N_PEERS:2 * N_PEERS]
        token = refs[-1]
        _, _, c, others = _place()
        for w in range(n):
            for p, chip in enumerate(others):
                _fetch_copy(ins, axes, send, recv, w, p, chip, c, False).start()
        token[...] = jnp.zeros_like(token)

    res = pl.pallas_call(
        body, name=name,
        in_specs=[IN_HBM] * n + [HBM_SPEC] * len(after),
        out_specs=[SEM_SPEC] * len(PEER_SEMS) + [IN_HBM] * n + [pl.BlockSpec(memory_space=pltpu.VMEM)],
        out_shape=PEER_SEMS + _hbm_like(fulls) + [TOKEN],
        input_output_aliases={i: len(PEER_SEMS) + i for i in range(n)},
        compiler_params=FLOWS,
    )(*_in_hbm(fulls), *after)
    k = len(PEER_SEMS)
    return list(res[:k]), list(res[k:k + n]), res[-1]


def _fetch_wait(sems, fulls, axes, after, name):
    n = len(fulls)

    def body(*refs):
        ins = refs[:n]
        send, recv = refs[n:n + N_PEERS], refs[n + N_PEERS:n + 2 * N_PEERS]
        _, _, c, others = _place()
        for w in range(n):
            for p, chip in enumerate(others):
                _fetch_copy(ins, axes, send, recv, w, p, chip, c, False).wait_send()
                _fetch_copy(ins, axes, send, recv, w, p, chip, c, True).wait_recv()

    res = pl.pallas_call(
        body, name=name,
        in_specs=[IN_HBM] * n + [SEM_SPEC] * len(sems) + [HBM_SPEC] * len(after),
        out_specs=[IN_HBM] * n,
        out_shape=_hbm_like(fulls),
        input_output_aliases={i: i for i in range(n)},
        compiler_params=FLOWS,
    )(*fulls, *sems, *after)
    return list(res)


def _hand_on(fulls, axes):
    n = len(fulls)

    def body(*refs):
        outs = refs[n:2 * n]
        send, recv = refs[2 * n:]
        x, y, c, others = _place()

        def copy(w, p, chip, half):
            reg = _region(outs[w], axes[w], _chip_no(*chip), half)
            return pltpu.make_async_remote_copy(
                src_ref=reg, dst_ref=reg, send_sem=send.at[w, p], recv_sem=recv.at[w, p],
                device_id=(x, y, 1 - c), device_id_type=MESH)

        for w in range(n):
            for p, chip in enumerate(others):
                copy(w, p, chip, c).start()
        for w in range(n):
            for p, chip in enumerate(others):
                copy(w, p, chip, 1 - c).wait()

    return pl.pallas_call(
        body, name="gather_hand_on",
        in_specs=[HBM_SPEC] * n, out_specs=[HBM_SPEC] * n,
        out_shape=[jax.ShapeDtypeStruct(f.shape, f.dtype) for f in fulls],
        input_output_aliases={i: i for i in range(n)},
        scratch_shapes=[pltpu.SemaphoreType.DMA((n, 3)), pltpu.SemaphoreType.DMA((n, 3))],
        compiler_params=pltpu.CompilerParams(has_side_effects=True),
    )(*fulls)


def _pair_exchange(grads, name):
    n = len(grads)

    def body(*refs):
        ins, outs = refs[:n], refs[n:2 * n]
        send, recv = refs[2 * n:]
        x, y, c, _ = _place()
        cps = []
        for w in range(n):
            cp = pltpu.make_async_remote_copy(
                src_ref=ins[w].at[:, 1 - c], dst_ref=outs[w], send_sem=send.at[w], recv_sem=recv.at[w],
                device_id=(x, y, 1 - c), device_id_type=MESH)
            cp.start()
            cps.append(cp)
        for cp in cps:
            cp.wait()

    return pl.pallas_call(
        body, name=name,
        in_specs=[HBM_SPEC] * n, out_specs=[HBM_SPEC] * n,
        out_shape=[jax.ShapeDtypeStruct((g.shape[0],) + g.shape[2:], g.dtype) for g in grads],
        scratch_shapes=[pltpu.SemaphoreType.DMA((n,)), pltpu.SemaphoreType.DMA((n,))],
        compiler_params=pltpu.CompilerParams(has_side_effects=True),
    )(*grads)


def _pair_add(g4, got, core, name):
    nj, _, hr, cdim = g4.shape
    tr = _tile(hr, max(8, 524288 // cdim))

    def body(core_ref, a_ref, b_ref, o_ref):
        o_ref[...] = (a_ref[...].astype(F32) + b_ref[...].astype(F32)).astype(o_ref.dtype)

    return pl.pallas_call(
        body, name=name,
        grid_spec=pltpu.PrefetchScalarGridSpec(
            num_scalar_prefetch=1, grid=(nj, hr // tr),
            in_specs=[pl.BlockSpec((1, None, tr, cdim), lambda j, i, core_ref: (j, core_ref[0], i, 0)),
                      pl.BlockSpec((1, tr, cdim), lambda j, i, core_ref: (j, i, 0))],
            out_specs=pl.BlockSpec((1, tr, cdim), lambda j, i, core_ref: (j, i, 0))),
        out_shape=jax.ShapeDtypeStruct((nj, hr, cdim), BF),
        compiler_params=_params(("parallel", "parallel")),
    )(core, g4, got)


def _piece(ref, axis, j, hc):
    if axis == 0:
        return ref.at[j]
    return ref.at[0, :, pl.ds(pl.multiple_of(j * hc, hc), hc)]


def _slot_shapes(sums, axes):
    return [(N_CHIPS - 1, sm.shape[1], sm.shape[2] // (1 if ax == 0 else N_CHIPS)) for sm, ax in zip(sums, axes)]


def _slot_copy(sums, lands, axes, send, recv, w, p, chip, c):
    return pltpu.make_async_remote_copy(
        src_ref=_piece(sums[w], axes[w], _chip_no(*chip), lands[w].shape[2]), dst_ref=lands[w].at[p],
        send_sem=send[p], recv_sem=recv[p],
        device_id=(chip[0], chip[1], c), device_id_type=MESH)


def _chip_exchange_start(sums, axes, name):
    n = len(sums)
    shapes = _slot_shapes(sums, axes)
    lands = [lax.empty(sh, sm.dtype) for sh, sm in zip(shapes, sums)]

    def body(*refs):
        ins, land_refs = refs[:n], refs[n:2 * n]
        send, recv = refs[2 * n:2 * n + N_PEERS], refs[2 * n + N_PEERS:2 * n + 2 * N_PEERS]
        token = refs[-1]
        _, _, c, others = _place()
        for w in range(n):
            for p, chip in enumerate(others):
                _slot_copy(ins, land_refs, axes, send, recv, w, p, chip, c).start()
        token[...] = jnp.zeros_like(token)

    k = len(PEER_SEMS)
    res = pl.pallas_call(
        body, name=name,
        in_specs=[IN_HBM] * (2 * n),
        out_specs=[SEM_SPEC] * k + [IN_HBM] * (2 * n) + [pl.BlockSpec(memory_space=pltpu.VMEM)],
        out_shape=PEER_SEMS + _hbm_like(list(sums) + lands) + [TOKEN],
        input_output_aliases={i: k + i for i in range(2 * n)},
        compiler_params=FLOWS,
    )(*_in_hbm(list(sums) + lands))
    return list(res[:k]), list(res[k:k + n]), list(res[k + n:k + 2 * n]), res[-1]


def _chip_exchange_wait(sems, sums, lands, axes, after, name):
    n = len(sums)

    def body(*refs):
        ins, land_refs = refs[:n], refs[n:2 * n]
        send, recv = refs[2 * n:2 * n + N_PEERS], refs[2 * n + N_PEERS:2 * n + 2 * N_PEERS]
        _, _, c, others = _place()
        for w in range(n):
            for p, chip in enumerate(others):
                cp = _slot_copy(ins, land_refs, axes, send, recv, w, p, chip, c)
                cp.wait_send()
                cp.wait_recv()

    res = pl.pallas_call(
        body, name=name,
        in_specs=[IN_HBM] * (2 * n) + [SEM_SPEC] * len(sems) + [HBM_SPEC] * len(after),
        out_specs=[IN_HBM] * (2 * n), out_shape=_hbm_like(list(sums) + list(lands)),
        input_output_aliases={i: i for i in range(2 * n)},
        compiler_params=FLOWS,
    )(*sums, *lands, *sems, *after)
    return list(res[:n]), list(res[n:])


def _chip_sum(psum, slots, axis, place, name):
    _, hr, hc = slots.shape
    tr = _tile(hr, 256)
    own_map = (lambda i, pref: (0, i, pref[0])) if axis == 1 else (lambda i, pref: (pref[0], i, 0))

    def body(pref, own_ref, s_ref, o_ref):
        o_ref[...] = ((own_ref[...].astype(F32) + s_ref[0].astype(F32)) + s_ref[1].astype(F32)) + s_ref[2].astype(F32)

    return pl.pallas_call(
        body, name=name,
        grid_spec=pltpu.PrefetchScalarGridSpec(
            num_scalar_prefetch=1, grid=(hr // tr,),
            in_specs=[pl.BlockSpec((None, tr, hc), own_map),
                      pl.BlockSpec((N_CHIPS - 1, tr, hc), lambda i, pref: (0, i, 0))],
            out_specs=pl.BlockSpec((None, tr, hc), lambda i, pref: (pref[1], i, 0))),
        out_shape=jax.ShapeDtypeStruct((2, hr, hc), F32),
        compiler_params=_params(("parallel",)),
    )(place, psum, slots)


def _half_swap(both):
    n = len(both)

    def body(*refs):
        outs = refs[n:2 * n]
        send, recv = refs[2 * n:]
        x, y, c, _ = _place()

        def copy(w, half):
            return pltpu.make_async_remote_copy(
                src_ref=outs[w].at[half], dst_ref=outs[w].at[half], send_sem=send.at[w], recv_sem=recv.at[w],
                device_id=(x, y, 1 - c), device_id_type=MESH)

        for w in range(n):
            copy(w, c).start()
        for w in range(n):
            copy(w, 1 - c).wait()

    return pl.pallas_call(
        body, name="grad_half_swap",
        in_specs=[HBM_SPEC] * n, out_specs=[HBM_SPEC] * n,
        out_shape=[jax.ShapeDtypeStruct(b.shape, b.dtype) for b in both],
        input_output_aliases={i: i for i in range(n)},
        scratch_shapes=[pltpu.SemaphoreType.DMA((n,)), pltpu.SemaphoreType.DMA((n,))],
        compiler_params=pltpu.CompilerParams(has_side_effects=True),
    )(*both)


def _allreduce_small(pack):
    rows, d = pack.shape

    def body(p_ref, o_ref, slots, send, recv):
        x, y, c, _ = _place()
        me = 4 * x + 2 * y + c
        slots[me] = p_ref[...]
        cps = []
        for k in range(1, N_DEV):
            px, py, pc = x ^ (k >> 2), y ^ ((k >> 1) & 1), c ^ (k & 1)
            cp = pltpu.make_async_remote_copy(
                src_ref=p_ref, dst_ref=slots.at[me], send_sem=send.at[k - 1], recv_sem=recv.at[k - 1],
                device_id=(px, py, pc), device_id_type=MESH)
            cp.start()
            cps.append(cp)
        for k in range(1, N_DEV):
            px, py, pc = x ^ (k >> 2), y ^ ((k >> 1) & 1), c ^ (k & 1)
            arrival = pltpu.make_async_remote_copy(
                src_ref=p_ref, dst_ref=slots.at[4 * px + 2 * py + pc], send_sem=send.at[k - 1],
                recv_sem=recv.at[k - 1], device_id=(px, py, pc), device_id_type=MESH)
            arrival.wait_recv()
            arrival.wait_send()
        acc = slots[0]
        for k in range(1, N_DEV):
            acc = acc + slots[k]
        o_ref[...] = acc

    vm = pl.BlockSpec(memory_space=pltpu.VMEM)
    return pl.pallas_call(
        body, name="allreduce_small", in_specs=[vm], out_specs=vm,
        out_shape=jax.ShapeDtypeStruct((rows, d), F32),
        scratch_shapes=[pltpu.VMEM((N_DEV, rows, d), F32), pltpu.SemaphoreType.DMA((N_DEV - 1,)),
                        pltpu.SemaphoreType.DMA((N_DEV - 1,))],
        compiler_params=pltpu.CompilerParams(has_side_effects=True),
    )(pack)


def _adamw(w, g, m, v, name):
    rows, cols = w.shape

    def fn(wv, gv, mv, vv):
        m2 = ADAM_B1 * mv + (1.0 - ADAM_B1) * gv
        v2 = ADAM_B2 * vv + (1.0 - ADAM_B2) * (gv * gv)
        m_hat = m2 / (1.0 - ADAM_B1 ** ADAM_STEP)
        v_hat = v2 / (1.0 - ADAM_B2 ** ADAM_STEP)
        delta = -ADAM_LR * (m_hat / (jnp.sqrt(v_hat) + ADAM_EPS) + ADAM_WD * wv)
        return delta, m2, v2

    ins = [(a, cols, 0) for a in (w, g, m, v)]
    return _rowwise(fn, ins, [(cols, F32)] * 3, rows=rows, tm=_tile(rows, max(8, 262144 // cols)), name=name)


BIG = ["w_in", "w_branch_a", "w_branch_b", "w_mix_out", "w_mem_q", "w_mem_kv", "w_mem_o", "w_ffn_in", "w_ffn_out"]
BIG_AXIS = {"w_in": 1, "w_branch_a": 1, "w_branch_b": 1, "w_mix_out": 0, "w_mem_q": 0, "w_mem_kv": 1,
            "w_mem_o": 0, "w_ffn_in": 1, "w_ffn_out": 0}
NORMS = ["norm_mix", "norm_mem_q", "norm_mem_kv", "norm_ffn", "norm_final"]
ORDER = ["norm_mix", "w_in", "conv_w", "w_branch_a", "w_branch_b", "w_mix_out", "norm_mem_q", "norm_mem_kv",
         "w_mem_q", "w_mem_kv", "w_mem_o", "norm_ffn", "w_ffn_in", "w_ffn_out", "norm_final"]


def _pack_small(vals, conv):
    d = vals[0].shape[-1]
    rows = [v.reshape(1, d) for v in vals]
    conv = jnp.pad(conv, ((0, 0), (0, d - conv.shape[1])))
    pad = jnp.zeros((SMALL_ROWS - len(rows) - CONV_K, d), F32)
    return jnp.concatenate(rows + [conv, pad], axis=0)


def kernel(x, mem, norm_mix, w_in, conv_w, w_branch_a, w_branch_b, w_mix_out, norm_mem_q, norm_mem_kv, w_mem_q, w_mem_kv, w_mem_o, norm_ffn, w_ffn_in, w_ffn_out, norm_final, loss_target, m_norm_mix, m_w_in, m_conv_w, m_w_branch_a, m_w_branch_b, m_w_mix_out, m_norm_mem_q, m_norm_mem_kv, m_w_mem_q, m_w_mem_kv, m_w_mem_o, m_norm_ffn, m_w_ffn_in, m_w_ffn_out, m_norm_final, v_norm_mix, v_w_in, v_conv_w, v_w_branch_a, v_w_branch_b, v_w_mix_out, v_norm_mem_q, v_norm_mem_kv, v_w_mem_q, v_w_mem_kv, v_w_mem_o, v_norm_ffn, v_w_ffn_in, v_w_ffn_out, v_norm_final):
    args = dict(locals())
    wts = {n: args[n] for n in ORDER}
    mom = {n: args["m_" + n] for n in ORDER}
    var = {n: args["v_" + n] for n in ORDER}
    x = x[0]
    mem = mem[0]
    target = loss_target[0]
    s, d = x.shape
    gains = {n: wts[n].reshape(1, d) for n in NORMS}
    chip = 2 * lax.axis_index("x") + lax.axis_index("y")
    core = lax.axis_index("c").astype(jnp.int32).reshape(1)
    place = jnp.stack([chip, lax.axis_index("c")]).astype(jnp.int32)

    conv_shard = jnp.pad(conv_w[0], ((0, CONV_ROWS - CONV_K), (0, 0)))
    placed = {n: _cast_place(wts[n][0], BIG_AXIS[n], place, BF, "place_" + n) for n in BIG}
    conv_placed = _cast_place(conv_shard, 1, place, F32, "place_conv_w")
    w_in_full, conv_full = _gather_weights([placed["w_in"], conv_placed], [BIG_AXIS["w_in"], 1], [True, False])
    W = {"w_in": w_in_full}
    later_w = [n for n in BIG if n != "w_in"]
    later_axes = [BIG_AXIS[n] for n in later_w]
    f_sems, later_bufs, f_token = _fetch_start([placed[n] for n in later_w], later_axes, (w_in_full,), "fetch_start")

    h1 = _rmsnorm(x, gains["norm_mix"], "norm_mix_fwd", after=(f_token,))
    proj = _matmul(h1, W["w_in"], tn=1280, name="in_proj")
    o_a, o_a32 = _sb_fwd(proj)
    y_b = _conv_fwd(proj, conv_full)
    later_bufs = _fetch_wait(f_sems, later_bufs, later_axes, (o_a, y_b), "fetch_wait")
    W.update(zip(later_w, _hand_on(later_bufs, later_axes)))
    br_a = _matmul(o_a, W["w_branch_a"], tn=1024, name="branch_a")
    br_b = _matmul(y_b, W["w_branch_b"], tn=1024, name="branch_b")
    ga_blk, gb_blk = 3, 4

    def merge(ga, gb, a, b):
        return jax.nn.sigmoid(ga.astype(F32)) * a.astype(F32) + jax.nn.sigmoid(gb.astype(F32)) * b.astype(F32)

    merged = _rowwise(merge, [(proj, d, ga_blk), (proj, d, gb_blk), (br_a, d, 0), (br_b, d, 0)], [(d, BF)],
                      rows=s, tm=512, name="merge_fwd")[0]
    x1 = _matmul(merged, W["w_mix_out"], tn=1024, out_dtype=F32, resid=x, name="mix_out")

    hq = _rmsnorm(x1, gains["norm_mem_q"], "norm_mem_q_fwd")
    mn = _rmsnorm(mem, gains["norm_mem_kv"], "norm_mem_kv_fwd")
    q_m = _matmul(hq, W["w_mem_q"], tn=1024, name="mem_q")
    kv = _matmul(mn, W["w_mem_kv"], tn=1024, name="mem_kv")
    o_m = _mem_fwd(q_m, kv)
    x2 = _matmul(o_m, W["w_mem_o"], tn=1024, out_dtype=F32, resid=x1, name="mem_o")

    hf = _rmsnorm(x2, gains["norm_ffn"], "norm_ffn_fwd")
    gu = _matmul(hf, W["w_ffn_in"], tn=1408, name="ffn_in")
    f = gu.shape[1] // 2

    def swiglu(gate, up):
        gate = gate.astype(F32)
        return gate * jax.nn.sigmoid(gate) * up.astype(F32)

    act = _rowwise(swiglu, [(gu, f, 0), (gu, f, 1)], [(f, BF)], rows=s, tm=512, name="swiglu_fwd")[0]
    x3 = _matmul(act, W["w_ffn_out"], tn=1024, out_dtype=F32, resid=x2, name="ffn_out")

    dx3, dx3_b, dg_final, loss_part = _loss_head(x3, gains["norm_final"], target, "loss_head")

    dact = _matmul(dx3_b, W["w_ffn_out"], tb=True, tn=1408, name="d_act")
    gw = {"w_ffn_out": _matmul(act, dx3_b, ta=True, tm=1408, tn=512, name="gw_ffn_out")}

    def swiglu_bwd(gate, up, da):
        gate, up, da = gate.astype(F32), up.astype(F32), da.astype(F32)
        sg = jax.nn.sigmoid(gate)
        dgate = da * up * (sg * (1.0 + gate * (1.0 - sg)))
        return jnp.concatenate([dgate, da * (gate * sg)], axis=-1)

    dgu = _rowwise(swiglu_bwd, [(gu, f, 0), (gu, f, 1), (dact, f, 0)], [(2 * f, BF)], rows=s, tm=256,
                   name="swiglu_bwd")[0]
    dhf = _matmul(dgu, W["w_ffn_in"], tb=True, tm=512, tn=1024, out_dtype=F32, name="d_hf")
    gw["w_ffn_in"] = _matmul(hf, dgu, ta=True, tn=512, name="gw_ffn_in")
    dx2, dx2_b, dg_ffn = _rmsnorm_bwd(x2, gains["norm_ffn"], dhf, dx3, "norm_ffn_bwd", bf_copy=True)

    def reduce_start(names, tag):
        views = []
        for n in names:
            r, cdim = gw[n].shape
            views.append(gw[n].reshape(1, 2, r // 2, cdim) if BIG_AXIS[n] == 1
                         else gw[n].reshape(N_CHIPS, 2, r // (2 * N_CHIPS), cdim))
        got = _pair_exchange(views, "grad_pair_exchange_" + tag)
        sums = [_pair_add(v, g, core, "pair_add_" + n) for n, v, g in zip(names, views, got)]
        group_axes = [BIG_AXIS[n] for n in names]
        sems, sums, lands, token = _chip_exchange_start(sums, group_axes, "grad_chip_start_" + tag)
        return (names, group_axes, sems, sums, lands), token

    def reduce_finish(group, after, tag):
        names, group_axes, sems, sums, lands = group
        sums, slots = _chip_exchange_wait(sems, sums, lands, group_axes, after, "grad_chip_wait_" + tag)
        return [_chip_sum(sm, sl, BIG_AXIS[n], place, "chip_sum_" + n) for n, sm, sl in zip(names, sums, slots)]

    group_ffn, token_ffn = reduce_start(["w_ffn_in", "w_ffn_out"], "ffn")

    do_m = _matmul(dx2_b, W["w_mem_o"], tb=True, tn=1024, after=(token_ffn,), name="d_o_m")
    gw["w_mem_o"] = _matmul(o_m, dx2_b, ta=True, tn=512, name="gw_mem_o")
    dq_m, dk_m, dv_m = _mem_bwd(q_m, kv, do_m)
    dkv = jnp.concatenate([dk_m, dv_m], axis=-1)
    dhq = _matmul(dq_m, W["w_mem_q"], tb=True, tn=1024, out_dtype=F32, name="d_hq")
    gw["w_mem_q"] = _matmul(hq, dq_m, ta=True, tn=512, name="gw_mem_q")
    dmn = _matmul(dkv, W["w_mem_kv"], tb=True, tn=1024, out_dtype=F32, name="d_mn")
    gw["w_mem_kv"] = _matmul(mn, dkv, ta=True, tn=1024, name="gw_mem_kv")
    _, dg_kv = _rmsnorm_bwd(mem, gains["norm_mem_kv"], dmn, None, "norm_mem_kv_bwd")
    dx1, dx1_b, dg_q = _rmsnorm_bwd(x1, gains["norm_mem_q"], dhq, dx2, "norm_mem_q_bwd", bf_copy=True)

    dmerged = _matmul(dx1_b, W["w_mix_out"], tb=True, tn=1024, name="d_merged")
    gw["w_mix_out"] = _matmul(merged, dx1_b, ta=True, tn=512, name="gw_mix_out")

    def merge_bwd(ga, gb, a, b, dm):
        dm = dm.astype(F32)
        sa, sb = jax.nn.sigmoid(ga.astype(F32)), jax.nn.sigmoid(gb.astype(F32))
        a, b = a.astype(F32), b.astype(F32)
        return dm * sa, dm * sb, dm * a * (sa * (1.0 - sa)), dm * b * (sb * (1.0 - sb))

    dbr_a, dbr_b, dga, dgb = _rowwise(
        merge_bwd, [(proj, d, ga_blk), (proj, d, gb_blk), (br_a, d, 0), (br_b, d, 0), (dmerged, d, 0)],
        [(d, BF)] * 4, rows=s, tm=256, name="merge_bwd")
    do_a = _matmul(dbr_a, W["w_branch_a"], tb=True, name="d_o_a")
    gw["w_branch_a"] = _matmul(o_a, dbr_a, ta=True, tn=512, name="gw_branch_a")
    dy_b = _matmul(dbr_b, W["w_branch_b"], tb=True, name="d_y_b")
    gw["w_branch_b"] = _matmul(y_b, dbr_b, ta=True, tn=512, name="gw_branch_b")
    group_mid, token_mid = reduce_start(
        ["w_mem_o", "w_mem_q", "w_mem_kv", "w_mix_out", "w_branch_a", "w_branch_b"], "mid")
    du, dgate_b, dgate_c, dconv = _conv_bwd(proj, conv_full, dy_b)
    dq, dk, dv = _sb_bwd(proj, o_a32, do_a, after=(token_mid,))

    def assemble(*parts):
        return jnp.concatenate([p.astype(BF) for p in parts], axis=-1)

    hw = dq.shape[1]
    dproj = _rowwise(assemble, [(t, hw, 0) for t in (dq, dk, dv, du, dgate_b, dgate_c)] + [(dga, d, 0), (dgb, d, 0)],
                     [(proj.shape[1], BF)], rows=s, tm=256, name="assemble_dproj")[0]
    gw["w_in"] = _matmul(h1, dproj, ta=True, tn=640, name="gw_in")
    group_in, token_in = reduce_start(["w_in"], "in")
    dh1 = _matmul(dproj, W["w_in"], tb=True, tm=512, tn=1024, out_dtype=F32, after=(token_in,), name="d_h1")
    grad_x, dg_mix = _rmsnorm_bwd(x, gains["norm_mix"], dh1, dx1, "norm_mix_bwd")

    halves = {}
    for group, tag in ((group_ffn, "ffn"), (group_mid, "mid"), (group_in, "in")):
        halves.update(zip(group[0], reduce_finish(group, (grad_x,), tag)))
    both = _half_swap([halves[n] for n in BIG])
    grads = {n: b.reshape(wts[n].shape[1:]) for n, b in zip(BIG, both)}

    small_g = [dg_mix, dg_q, dg_kv, dg_ffn, dg_final]
    pack = _pack_small(small_g, dconv[:CONV_K])
    pack = pack.at[ROW_LOSS].set(jnp.broadcast_to(loss_part[0, :1], (d,)))
    red = _allreduce_small(pack)
    loss = red[ROW_LOSS, 0]
    cw = conv_w.shape[2]
    conv_g = lax.dynamic_slice(red, (ROW_CONV, chip * cw), (CONV_K, cw))
    small_grad = _pack_small([red[i] for i in range(len(NORMS))], conv_g)
    small = [_pack_small([t[n] for n in NORMS], t["conv_w"][0]) for t in (wts, mom, var)]
    s_delta, s_m, s_v = _adamw(small[0], small_grad, small[1], small[2], "adamw_small")

    out = {"grad": {}, "delta": {}, "new_m": {}, "new_v": {}}
    for n in BIG:
        shp = wts[n].shape
        dl, m2, v2 = _adamw(wts[n][0], grads[n], mom[n][0], var[n][0], "adamw_" + n)
        out["grad"][n] = grads[n].reshape(shp)
        out["delta"][n], out["new_m"][n], out["new_v"][n] = dl.reshape(shp), m2.reshape(shp), v2.reshape(shp)
    for key, blk in (("grad", small_grad), ("delta", s_delta), ("new_m", s_m), ("new_v", s_v)):
        for i, n in enumerate(NORMS):
            out[key][n] = blk[i].reshape(wts[n].shape)
        out[key]["conv_w"] = blk[ROW_CONV:ROW_CONV + CONV_K, :cw].reshape(conv_w.shape)

    return (loss, grad_x[None], *[out["grad"][n] for n in ORDER], *[out["delta"][n] for n in ORDER],
            *[out["new_m"][n] for n in ORDER], *[out["new_v"][n] for n in ORDER])
```

```python
import functools
import math

import jax
import jax.numpy as jnp
from jax import lax
from jax.experimental import pallas as pl
from jax.experimental.pallas import tpu as pltpu

BF = jnp.bfloat16
F32 = jnp.float32
MESH = pl.DeviceIdType.MESH

SB_HEAD_DIM = 64
LANES = 128
MEM_HEADS = 4
CONV_K = 3
CONV_ROWS = 8
EPS = 1e-6
N_CHIPS = 4
N_DEV = 8
VMEM_LIMIT = 56 * 1024 * 1024

ADAM_LR = 0.001
ADAM_B1 = 0.9
ADAM_B2 = 0.999
ADAM_EPS = 1e-08
ADAM_WD = 0.01
ADAM_STEP = 10

SMALL_ROWS = 16
ROW_CONV = 5
ROW_LOSS = 8


def _params(sem=None, **kw):
    return pltpu.CompilerParams(dimension_semantics=sem, vmem_limit_bytes=VMEM_LIMIT, **kw)


def _tile(dim, pref):
    if dim <= pref:
        return dim
    for step in (LANES, 8):
        t = (pref // step) * step
        while t >= step:
            if dim % t == 0:
                return t
            t -= step
    raise ValueError(f"no tile of {dim} under {pref}")


def _matmul(a, b, *, ta=False, tb=False, tm=1024, tn=512, tk=None, out_dtype=BF, resid=None, after=(), name):
    if ta:
        kdim, m = a.shape
    else:
        m, kdim = a.shape
    n = b.shape[0] if tb else b.shape[1]
    tm, tn = _tile(m, tm), _tile(n, tn)
    tk = _tile(kdim, tk or kdim)
    nk = kdim // tk
    a_spec = pl.BlockSpec((tk, tm), lambda i, j, k: (k, i)) if ta else pl.BlockSpec((tm, tk), lambda i, j, k: (i, k))
    b_spec = pl.BlockSpec((tn, tk), lambda i, j, k: (j, k)) if tb else pl.BlockSpec((tk, tn), lambda i, j, k: (k, j))
    o_spec = pl.BlockSpec((tm, tn), lambda i, j, k: (i, j))
    dims = (((0 if ta else 1,), (1 if tb else 0,)), ((), ()))
    has_res = resid is not None

    def body(*refs):
        a_ref, b_ref = refs[0], refs[1]
        o_ref = refs[2 + has_res + len(after)]
        av, bv = a_ref[...], b_ref[...]
        if av.dtype != BF:
            av = av.astype(BF)
        if bv.dtype != BF:
            bv = bv.astype(BF)
        p = lax.dot_general(av, bv, dims, preferred_element_type=F32)

        def finish(acc):
            if has_res:
                acc = refs[2][...] + acc
            o_ref[...] = acc.astype(o_ref.dtype)

        if nk == 1:
            finish(p)
        else:
            acc_ref = refs[-1]
            k = pl.program_id(2)

            @pl.when(k == 0)
            def _():
                acc_ref[...] = p

            @pl.when(k > 0)
            def _():
                acc_ref[...] += p

            @pl.when(k == nk - 1)
            def _():
                finish(acc_ref[...])

    return pl.pallas_call(
        body, name=name, grid=(m // tm, n // tn, nk),
        in_specs=[a_spec, b_spec] + ([o_spec] if has_res else []) + [HBM_SPEC] * len(after),
        out_specs=o_spec, out_shape=jax.ShapeDtypeStruct((m, n), out_dtype),
        scratch_shapes=[pltpu.VMEM((tm, tn), F32)] if nk > 1 else [],
        compiler_params=_params(("parallel", "parallel", "arbitrary")),
    )(*([a, b] + ([resid] if has_res else []) + list(after)))


def _rowwise(fn, ins, outs, *, rows, tm, name, accs=(), after=()):
    tm = _tile(rows, tm)
    in_specs, args = [], []
    for arr, cols, cb in ins:
        if cols is None:
            in_specs.append(pl.BlockSpec(arr.shape, lambda i, nd=arr.ndim: (0,) * nd))
        else:
            in_specs.append(pl.BlockSpec((tm, cols), lambda i, cb=cb: (i, cb)))
        args.append(arr)
    out_specs = [pl.BlockSpec((tm, cols), lambda i: (i, 0)) for cols, _ in outs]
    out_shape = [jax.ShapeDtypeStruct((rows, cols), dt) for cols, dt in outs]
    for r, c in accs:
        out_specs.append(pl.BlockSpec((r, c), lambda i: (0, 0)))
        out_shape.append(jax.ShapeDtypeStruct((r, c), F32))
    n_in, n_out = len(ins), len(outs)
    in_specs += [HBM_SPEC] * len(after)
    args += list(after)

    def body(*refs):
        res = fn(*[r[...] for r in refs[:n_in]])
        if not isinstance(res, (tuple, list)):
            res = (res,)
        refs = refs[n_in + len(after):]
        for o_ref, val in zip(refs[:n_out], res[:n_out]):
            o_ref[...] = val.astype(o_ref.dtype)
        first = pl.program_id(0) == 0
        for a_ref, val in zip(refs[n_out:], res[n_out:]):
            @pl.when(first)
            def _(a_ref=a_ref, val=val):
                a_ref[...] = val

            @pl.when(jnp.logical_not(first))
            def _(a_ref=a_ref, val=val):
                a_ref[...] += val

    res = pl.pallas_call(
        body, name=name, grid=(rows // tm,), in_specs=in_specs, out_specs=out_specs, out_shape=out_shape,
        compiler_params=_params(("arbitrary",) if accs else ("parallel",)),
    )(*args)
    return res


def _rstd(xf):
    return lax.rsqrt(jnp.mean(xf * xf, axis=-1, keepdims=True) + EPS)


def _rmsnorm(x, g, name, after=()):
    rows, d = x.shape
    return _rowwise(lambda xv, gv: xv * _rstd(xv) * gv, [(x, d, 0), (g, None, None)], [(d, BF)],
                    rows=rows, tm=512, name=name, after=after)[0]


def _rmsnorm_bwd(x, g, dy, resid, name, bf_copy=False):
    rows, d = x.shape

    def fn(xv, gv, dyv, *rest):
        dyv = dyv.astype(F32)
        r = _rstd(xv)
        xn = xv * r
        dxn = dyv * gv
        dx = r * (dxn - xn * jnp.mean(dxn * xn, axis=-1, keepdims=True))
        if rest:
            dx = rest[0] + dx
        return (dx,) * (1 + bf_copy) + (jnp.sum(dyv * xn, axis=0, keepdims=True),)

    ins = [(x, d, 0), (g, None, None), (dy, d, 0)] + ([(resid, d, 0)] if resid is not None else [])
    outs = [(d, F32)] + ([(d, BF)] if bf_copy else [])
    return _rowwise(fn, ins, outs, rows=rows, tm=512, name=name, accs=[(1, d)])


def _matmul_norm_bwd(a, w, x, g, resid, *, tm, bf_copy, name, after=()):
    m, kdim = a.shape
    d = w.shape[0]
    tm = _tile(m, tm)
    row = lambda i: (i, 0)
    whole = lambda i: (0, 0)

    def body(a_ref, w_ref, x_ref, g_ref, r_ref, *rest):
        outs = rest[len(after):]
        dy = lax.dot_general(a_ref[...], w_ref[...], NT, preferred_element_type=F32)
        xv = x_ref[...]
        r = _rstd(xv)
        xn = xv * r
        dxn = dy * g_ref[...]
        dx = r_ref[...] + r * (dxn - xn * jnp.mean(dxn * xn, axis=-1, keepdims=True))
        outs[0][...] = dx
        if bf_copy:
            outs[1][...] = dx.astype(BF)
        dg = jnp.sum(dy * xn, axis=0, keepdims=True)
        first = pl.program_id(0) == 0

        @pl.when(first)
        def _():
            outs[-1][...] = dg

        @pl.when(jnp.logical_not(first))
        def _():
            outs[-1][...] += dg

    tok = pl.BlockSpec((tm, d), row)
    out_specs = [tok] + ([tok] if bf_copy else []) + [pl.BlockSpec((1, d), whole)]
    out_shape = ([jax.ShapeDtypeStruct((m, d), F32)] + ([jax.ShapeDtypeStruct((m, d), BF)] if bf_copy else [])
                 + [jax.ShapeDtypeStruct((1, d), F32)])
    return pl.pallas_call(
        body, name=name, grid=(m // tm,),
        in_specs=[pl.BlockSpec((tm, kdim), row), pl.BlockSpec((d, kdim), whole), tok, pl.BlockSpec((1, d), whole), tok]
        + [HBM_SPEC] * len(after),
        out_specs=out_specs, out_shape=out_shape,
        compiler_params=_params(("arbitrary",)),
    )(a, w, x, g, resid, *after)


def _loss_head(x, g, target, name):
    rows, d = x.shape

    def fn(xv, gv, tv):
        r = _rstd(xv)
        xn = xv * r
        err = xn * gv - tv
        per_tok = jnp.mean(err * err, axis=-1, keepdims=True)
        loss = 0.5 * jnp.sum(per_tok, axis=0, keepdims=True)
        dyv = err * (1.0 / d)
        dxn = dyv * gv
        dx = r * (dxn - xn * jnp.mean(dxn * xn, axis=-1, keepdims=True))
        return dx, dx, jnp.sum(dyv * xn, axis=0, keepdims=True), jnp.broadcast_to(loss, (1, LANES))

    return _rowwise(fn, [(x, d, 0), (g, None, None), (target, d, 0)], [(d, F32), (d, BF)],
                    rows=rows, tm=512, name=name, accs=[(1, d), (1, LANES)])


SB_TK = 128
SB_KT = 4


def _sb_consts(tq):
    tk = SB_TK
    diff = lax.broadcasted_iota(jnp.int32, (tq, tk), 1) - lax.broadcasted_iota(jnp.int32, (tq, tk), 0)
    rj = lax.broadcasted_iota(jnp.int32, (2 * tk, 2 * tk), 0) & (tk - 1)
    cj = lax.broadcasted_iota(jnp.int32, (2 * tk, 2 * tk), 1)
    ones_half = cj >= tk
    later = jnp.where((rj > cj) | ones_half, 1.0, 0.0).astype(BF)
    later_incl = jnp.where((rj >= cj) | ones_half, 1.0, 0.0).astype(BF)
    return diff, later, later_incl


def _split_dot(val, rhs_twice):
    hi = val.astype(BF)
    lo = (val - hi.astype(F32)).astype(BF)
    return jnp.dot(jnp.concatenate([hi, lo], axis=1), rhs_twice, preferred_element_type=F32)


def _log_terms(z):
    sp = jnp.maximum(z, 0.0) + jnp.log(1.0 + jnp.exp(-jnp.abs(z)))
    return z - sp, sp


NT = (((1,), (1,)), ((), ()))
TN = (((0,), (0,)), ((), ()))


def _head_lane_masks(rows):
    lane = lax.broadcasted_iota(jnp.int32, (rows, LANES), 1)
    first = jnp.where(lane < SB_HEAD_DIM, 1.0, 0.0)
    return first.astype(BF), (1.0 - first).astype(BF)


def _both_heads(tile, masks):
    return jnp.concatenate([tile * masks[0], tile * masks[1]], axis=0)


def _sb_fwd(proj):
    s = proj.shape[0]
    tk, tq = SB_TK, SB_KT * SB_TK
    n_pairs = 4
    scale = 1.0 / math.sqrt(SB_HEAD_DIM)

    def body(q_ref, k_ref, v_ref, o_ref, o32_ref):
        i = pl.program_id(1)
        diff, later, _ = _sb_consts(tq)
        qs = (q_ref[...].astype(F32) * scale).astype(BF)
        lane_masks = _head_lane_masks(tk)

        def step(g, state, masked):
            tiles = list(reversed(range(SB_KT)))
            chains = [(t, h) for t in tiles for h in range(2)]
            rows = {t: pl.ds(pl.multiple_of((g * SB_KT + t) * tk, tk), tk) for t in tiles}
            ks = {t: _both_heads(k_ref[rows[t], :], lane_masks) for t in tiles}
            vs = {t: _both_heads(v_ref[rows[t], :], lane_masks) for t in tiles}
            allowed = {t: diff < -t * tk for t in tiles}
            zs = {t: lax.dot_general(qs, ks[t], NT, preferred_element_type=F32) for t in tiles}
            logs = {}
            for t, h in chains:
                log_b, sp = _log_terms(zs[t][:, h * tk:(h + 1) * tk])
                logs[t, h] = (log_b, jnp.where(allowed[t], sp, 0.0) if masked else sp)
            sums = {c: _split_dot(logs[c][1], later) for c in chains}
            carries = list(state[0])
            ws = {}
            for t, h in chains:
                w = jnp.exp(logs[t, h][0] - (sums[t, h][:, :tk] + carries[h]))
                ws[t, h] = (jnp.where(allowed[t], w, 0.0) if masked else w).astype(BF)
                carries[h] = carries[h] + sums[t, h][:, tk:]
            acc = state[1]
            for t in tiles:
                acc = acc + jnp.dot(jnp.concatenate([ws[t, 0], ws[t, 1]], axis=1), vs[t],
                                    preferred_element_type=F32)
            return tuple(carries), acc

        zero = jnp.zeros((tq, LANES), F32)
        state = step(i, ((zero, zero), zero), True)
        state = lax.fori_loop(0, i, lambda gg, st: step(i - 1 - gg, st, False), state)
        o_ref[...] = state[1].astype(o_ref.dtype)
        o32_ref[...] = state[1]

    tok = pl.BlockSpec((tq, LANES), lambda p, i: (i, p))
    return pl.pallas_call(
        body, name="sb_attn_fwd", grid=(n_pairs, s // tq),
        in_specs=[tok,
                  pl.BlockSpec((s, LANES), lambda p, i: (0, n_pairs + p)),
                  pl.BlockSpec((s, LANES), lambda p, i: (0, 2 * n_pairs + p))],
        out_specs=[tok, tok],
        out_shape=[jax.ShapeDtypeStruct((s, n_pairs * LANES), BF), jax.ShapeDtypeStruct((s, n_pairs * LANES), F32)],
        compiler_params=_params(("parallel", "arbitrary")),
    )(proj, proj, proj)


def _sb_bwd(proj, o32, do_a, after=()):
    s = proj.shape[0]
    tk, tq = SB_TK, SB_KT * SB_TK
    n_pairs = 4
    scale = 1.0 / math.sqrt(SB_HEAD_DIM)

    def body(q_ref, k_ref, v_ref, o_ref, do_ref, *rest):
        dq_ref, dk_ref, dv_ref = rest[len(after):]
        i = pl.program_id(1)

        @pl.when(i == 0)
        def _():
            dk_ref[...] = jnp.zeros_like(dk_ref)
            dv_ref[...] = jnp.zeros_like(dv_ref)

        diff, later, later_incl = _sb_consts(tq)
        qs = (q_ref[...].astype(F32) * scale).astype(BF)
        do2 = do_ref[...]
        prod = do2.astype(F32) * o_ref[...]
        lane_masks = _head_lane_masks(tk)
        first_head = lax.broadcasted_iota(jnp.int32, (tq, LANES), 1) < SB_HEAD_DIM
        totals = [jnp.broadcast_to(jnp.sum(jnp.where(keep, prod, 0.0), axis=-1, keepdims=True), (tq, tk))
                  for keep in (first_head, jnp.logical_not(first_head))]
        first_head_k = first_head[:tk]

        def step(g_idx, state, masked):
            tiles = list(reversed(range(SB_KT)))
            chains = [(t, h) for t in tiles for h in range(2)]
            rows = {t: pl.ds(pl.multiple_of((g_idx * SB_KT + t) * tk, tk), tk) for t in tiles}
            ks = {t: _both_heads(k_ref[rows[t], :], lane_masks) for t in tiles}
            vs = {t: _both_heads(v_ref[rows[t], :], lane_masks) for t in tiles}
            allowed = {t: diff < -t * tk for t in tiles}
            zs = {t: lax.dot_general(qs, ks[t], NT, preferred_element_type=F32) for t in tiles}
            dws = {t: lax.dot_general(do2, vs[t], NT, preferred_element_type=F32) for t in tiles}
            logs = {}
            for t, h in chains:
                log_b, sp = _log_terms(zs[t][:, h * tk:(h + 1) * tk])
                logs[t, h] = (log_b, jnp.where(allowed[t], sp, 0.0) if masked else sp)
            sums = {c: _split_dot(logs[c][1], later) for c in chains}
            c_log, c_g = list(state[0]), list(state[1])
            ws, gs = {}, {}
            for t, h in chains:
                w = jnp.exp(logs[t, h][0] - (sums[t, h][:, :tk] + c_log[h]))
                ws[t, h] = (jnp.where(allowed[t], w, 0.0) if masked else w).astype(BF)
                c_log[h] = c_log[h] + sums[t, h][:, tk:]
                gs[t, h] = ws[t, h].astype(F32) * dws[t][:, h * tk:(h + 1) * tk]
            gsums = {c: _split_dot(gs[c], later_incl) for c in chains}
            dzs = {}
            for t, h in chains:
                beta = jnp.exp(logs[t, h][0])
                earlier = totals[h] - (gsums[t, h][:, :tk] + c_g[h])
                dz = gs[t, h] * (1.0 - beta) - earlier * beta
                dzs[t, h] = (jnp.where(allowed[t], dz, 0.0) if masked else dz).astype(BF)
                c_g[h] = c_g[h] + gsums[t, h][:, tk:]
            dq = state[2]
            for t in tiles:
                dz_both = jnp.concatenate([dzs[t, 0], dzs[t, 1]], axis=1)
                w_both = jnp.concatenate([ws[t, 0], ws[t, 1]], axis=1)
                dq = dq + jnp.dot(dz_both, ks[t], preferred_element_type=F32)
                dk2 = lax.dot_general(dz_both, qs, TN, preferred_element_type=F32)
                dv2 = lax.dot_general(w_both, do2, TN, preferred_element_type=F32)
                dk_ref[rows[t], :] += jnp.where(first_head_k, dk2[:tk], dk2[tk:])
                dv_ref[rows[t], :] += jnp.where(first_head_k, dv2[:tk], dv2[tk:])
            return tuple(c_log), tuple(c_g), dq

        zero = jnp.zeros((tq, LANES), F32)
        state = step(i, ((zero, zero), (zero, zero), zero), True)
        state = lax.fori_loop(0, i, lambda gg, st: step(i - 1 - gg, st, False), state)
        dq_ref[...] = (state[2] * scale).astype(dq_ref.dtype)

    width = n_pairs * LANES
    return pl.pallas_call(
        body, name="sb_attn_bwd", grid=(n_pairs, s // tq),
        in_specs=[pl.BlockSpec((tq, LANES), lambda p, i: (i, p)),
                  pl.BlockSpec((s, LANES), lambda p, i: (0, n_pairs + p)),
                  pl.BlockSpec((s, LANES), lambda p, i: (0, 2 * n_pairs + p)),
                  pl.BlockSpec((tq, LANES), lambda p, i: (i, p)),
                  pl.BlockSpec((tq, LANES), lambda p, i: (i, p))] + [HBM_SPEC] * len(after),
        out_specs=[pl.BlockSpec((tq, LANES), lambda p, i: (i, p)),
                   pl.BlockSpec((s, LANES), lambda p, i: (0, p)),
                   pl.BlockSpec((s, LANES), lambda p, i: (0, p))],
        out_shape=[jax.ShapeDtypeStruct((s, width), BF), jax.ShapeDtypeStruct((s, width), F32),
                   jax.ShapeDtypeStruct((s, width), F32)],
        compiler_params=_params(("parallel", "arbitrary")),
    )(proj, proj, proj, o32, do_a, *after)


CONV_COL0 = 12


def _shift_rows(v, k):
    n = v.shape[0]
    row = lax.broadcasted_iota(jnp.int32, v.shape, 0)
    rolled = pltpu.roll(v, k % n, axis=0)
    keep = row >= k if k > 0 else row < n + k
    return jnp.where(keep, rolled, 0.0)


def _conv_specs(s):
    return [pl.BlockSpec((s, LANES), lambda cb: (0, CONV_COL0 + cb)),
            pl.BlockSpec((s, LANES), lambda cb: (0, CONV_COL0 + 4 + cb)),
            pl.BlockSpec((s, LANES), lambda cb: (0, CONV_COL0 + 8 + cb)),
            pl.BlockSpec((CONV_ROWS, LANES), lambda cb: (0, cb))]


def _conv_fwd(proj, conv_w):
    s = proj.shape[0]

    def body(u_ref, gb_ref, gc_ref, w_ref, y_ref):
        cu = gc_ref[...].astype(F32) * u_ref[...].astype(F32)
        w = w_ref[...]
        y = w[0:1] * _shift_rows(cu, 2) + w[1:2] * _shift_rows(cu, 1) + w[2:3] * cu
        y_ref[...] = (gb_ref[...].astype(F32) * y).astype(y_ref.dtype)

    return pl.pallas_call(
        body, name="conv_fwd", grid=(4,), in_specs=_conv_specs(s),
        out_specs=pl.BlockSpec((s, LANES), lambda cb: (0, cb)),
        out_shape=jax.ShapeDtypeStruct((s, 4 * LANES), BF),
        compiler_params=_params(("parallel",)),
    )(proj, proj, proj, conv_w)


def _conv_bwd(proj, conv_w, dy):
    s = proj.shape[0]

    def body(u_ref, gb_ref, gc_ref, w_ref, dy_ref, du_ref, dgb_ref, dgc_ref, dw_ref):
        u, gc = u_ref[...].astype(F32), gc_ref[...].astype(F32)
        dyv = dy_ref[...].astype(F32)
        w = w_ref[...]
        cu = gc * u
        cu1, cu2 = _shift_rows(cu, 1), _shift_rows(cu, 2)
        conv = w[0:1] * cu2 + w[1:2] * cu1 + w[2:3] * cu
        dgb_ref[...] = (dyv * conv).astype(dgb_ref.dtype)
        dc = dyv * gb_ref[...].astype(F32)
        dcu = w[2:3] * dc + w[1:2] * _shift_rows(dc, -1) + w[0:1] * _shift_rows(dc, -2)
        dgc_ref[...] = (dcu * u).astype(dgc_ref.dtype)
        du_ref[...] = (dcu * gc).astype(du_ref.dtype)
        tap_row = lax.broadcasted_iota(jnp.int32, (CONV_ROWS, LANES), 0)
        dw = jnp.zeros((CONV_ROWS, LANES), F32)
        for t, shifted in enumerate((cu2, cu1, cu)):
            dw = jnp.where(tap_row == t, jnp.sum(dc * shifted, axis=0, keepdims=True), dw)
        dw_ref[...] = dw

    col = pl.BlockSpec((s, LANES), lambda cb: (0, cb))
    act = jax.ShapeDtypeStruct((s, 4 * LANES), BF)
    return pl.pallas_call(
        body, name="conv_bwd", grid=(4,), in_specs=_conv_specs(s) + [col],
        out_specs=[col, col, col, pl.BlockSpec((CONV_ROWS, LANES), lambda cb: (0, cb))],
        out_shape=[act, act, act, jax.ShapeDtypeStruct((CONV_ROWS, 4 * LANES), F32)],
        compiler_params=_params(("parallel",)),
    )(proj, proj, proj, conv_w, dy)


def _mem_probs(q, k, scale):
    sc = lax.dot_general(q, k, NT, preferred_element_type=F32) * scale
    p = jnp.exp(sc - jnp.max(sc, axis=-1, keepdims=True))
    return p / jnp.sum(p, axis=-1, keepdims=True)


def _mem_fwd(q_m, kv, tq=2048):
    s, d = q_m.shape
    mlen = kv.shape[0]
    hd = d // MEM_HEADS
    tq = _tile(s, tq)
    scale = 1.0 / math.sqrt(hd)

    def body(q_ref, k_ref, v_ref, o_ref):
        p = _mem_probs(q_ref[...], k_ref[...], scale)
        o_ref[...] = jnp.dot(p.astype(BF), v_ref[...], preferred_element_type=F32).astype(o_ref.dtype)

    return pl.pallas_call(
        body, name="mem_attn_fwd", grid=(MEM_HEADS, s // tq),
        in_specs=[pl.BlockSpec((tq, hd), lambda h, i: (i, h)),
                  pl.BlockSpec((mlen, hd), lambda h, i: (0, h)),
                  pl.BlockSpec((mlen, hd), lambda h, i: (0, MEM_HEADS + h))],
        out_specs=pl.BlockSpec((tq, hd), lambda h, i: (i, h)),
        out_shape=jax.ShapeDtypeStruct((s, d), BF),
        compiler_params=_params(("parallel", "parallel")),
    )(q_m, kv, kv)


def _mem_bwd(q_m, kv, do_m, tq=2048):
    s, d = q_m.shape
    mlen = kv.shape[0]
    hd = d // MEM_HEADS
    tq = _tile(s, tq)
    scale = 1.0 / math.sqrt(hd)

    def body(q_ref, k_ref, v_ref, do_ref, dq_ref, dk_ref, dv_ref):
        q, k, v, do = q_ref[...], k_ref[...], v_ref[...], do_ref[...]
        p = _mem_probs(q, k, scale)
        dp = lax.dot_general(do, v, NT, preferred_element_type=F32)
        ds = p * (dp - jnp.sum(dp * p, axis=-1, keepdims=True)) * scale
        dsb = ds.astype(BF)
        dq_ref[...] = jnp.dot(dsb, k, preferred_element_type=F32).astype(dq_ref.dtype)
        dk = lax.dot_general(dsb, q, TN, preferred_element_type=F32)
        dv = lax.dot_general(p.astype(BF), do, TN, preferred_element_type=F32)
        first = pl.program_id(1) == 0

        @pl.when(first)
        def _():
            dk_ref[...] = dk
            dv_ref[...] = dv

        @pl.when(jnp.logical_not(first))
        def _():
            dk_ref[...] += dk
            dv_ref[...] += dv

    tok = pl.BlockSpec((tq, hd), lambda h, i: (i, h))
    memb = pl.BlockSpec((mlen, hd), lambda h, i: (0, h))
    return pl.pallas_call(
        body, name="mem_attn_bwd", grid=(MEM_HEADS, s // tq),
        in_specs=[tok, memb, pl.BlockSpec((mlen, hd), lambda h, i: (0, MEM_HEADS + h)), tok],
        out_specs=[tok, memb, memb],
        out_shape=[jax.ShapeDtypeStruct((s, d), BF), jax.ShapeDtypeStruct((mlen, d), F32),
                   jax.ShapeDtypeStruct((mlen, d), F32)],
        compiler_params=_params(("parallel", "arbitrary")),
    )(q_m, kv, kv, do_m)


def _place():
    x, y, c = lax.axis_index("x"), lax.axis_index("y"), lax.axis_index("c")
    other_chips = [(1 - x, y), (x, 1 - y), (1 - x, 1 - y)]
    return x, y, c, other_chips


def _chip_no(cx, cy):
    return 2 * cx + cy


HBM_SPEC = pl.BlockSpec(memory_space=pl.ANY)


def _cast_place(shard, axis, place, dtype, name, after=()):
    r, c = shard.shape
    tr = _tile(r, max(16, 1048576 // c))
    nblk = r // tr
    if axis == 1:
        full, out_map = (r, N_CHIPS * c), lambda i, pref: (i, pref[0])
    else:
        full, out_map = (N_CHIPS * r, c), lambda i, pref: (pref[0] * nblk + i, 0)

    def body(pref, s_ref, *rest):
        o_ref = rest[-1]
        o_ref[...] = s_ref[...].astype(o_ref.dtype)

    return pl.pallas_call(
        body, name=name,
        grid_spec=pltpu.PrefetchScalarGridSpec(
            num_scalar_prefetch=1, grid=(nblk,),
            in_specs=[pl.BlockSpec((tr, c), lambda i, pref: (i, 0))] + [HBM_SPEC] * len(after),
            out_specs=pl.BlockSpec((tr, c), out_map)),
        out_shape=jax.ShapeDtypeStruct(full, dtype),
        compiler_params=_params(("parallel",)),
    )(place, shard, *after)


def _region(ref, axis, chip_no, half):
    width = ref.shape[axis] // N_CHIPS
    start = pl.multiple_of(chip_no * width, width)
    if axis == 1:
        if half is None:
            return ref.at[:, pl.ds(start, width)]
        hr = ref.shape[0] // 2
        return ref.at[pl.ds(pl.multiple_of(half * hr, hr), hr), pl.ds(start, width)]
    if half is None:
        return ref.at[pl.ds(start, width), :]
    hr = width // 2
    return ref.at[pl.ds(pl.multiple_of(start + half * hr, hr), hr), :]


def _gather_weights(fulls, axes, split):
    n = len(fulls)

    def body(*refs):
        outs = refs[n:2 * n]
        send, recv, fsend, frecv = refs[2 * n:]
        x, y, c, others = _place()
        me = _chip_no(x, y)
        sibling = (x, y, 1 - c)

        def copy(w, chip_no, half, sems, p, to):
            reg = _region(outs[w], axes[w], chip_no, half)
            return pltpu.make_async_remote_copy(
                src_ref=reg, dst_ref=reg, send_sem=sems[0].at[w, p], recv_sem=sems[1].at[w, p],
                device_id=to, device_id_type=MESH)

        for w in range(n):
            for p, chip in enumerate(others):
                copy(w, me, c if split[w] else None, (send, recv), p, (chip[0], chip[1], c)).start()
        for w in range(n):
            for p, chip in enumerate(others):
                half = c if split[w] else None
                copy(w, _chip_no(*chip), half, (send, recv), p, (chip[0], chip[1], c)).wait_recv()
                if split[w]:
                    copy(w, _chip_no(*chip), c, (fsend, frecv), p, sibling).start()
        for w in range(n):
            for p, chip in enumerate(others):
                copy(w, me, c if split[w] else None, (send, recv), p, (chip[0], chip[1], c)).wait_send()
                if split[w]:
                    handed = copy(w, _chip_no(*chip), 1 - c, (fsend, frecv), p, sibling)
                    handed.wait_recv()
                    handed.wait_send()

    return pl.pallas_call(
        body, name="gather_weights",
        in_specs=[HBM_SPEC] * n, out_specs=[HBM_SPEC] * n,
        out_shape=[jax.ShapeDtypeStruct(f.shape, f.dtype) for f in fulls],
        input_output_aliases={i: i for i in range(n)},
        scratch_shapes=[pltpu.SemaphoreType.DMA((n, 3))] * 4,
        compiler_params=pltpu.CompilerParams(has_side_effects=True),
    )(*fulls)


SEM_SPEC = pl.BlockSpec(memory_space=pltpu.SEMAPHORE)
IN_HBM = pl.BlockSpec(memory_space=pltpu.HBM)
FLOWS = pltpu.CompilerParams(has_side_effects=pltpu.SideEffectType.DATAFLOW_SIDE_EFFECTING)
TOKEN = jax.ShapeDtypeStruct((8, LANES), F32)


def _in_hbm(arrays):
    return [pltpu.with_memory_space_constraint(a, pltpu.HBM) for a in arrays]


def _hbm_like(arrays):
    return [pltpu.HBM(a.shape, a.dtype) for a in arrays]


def _fetch_copy(refs, axes, whole, send, recv, w, p, chip, c, arriving):
    owner = _chip_no(*chip) if arriving else _chip_no(lax.axis_index("x"), lax.axis_index("y"))
    reg = _region(refs[w], axes[w], owner, None if whole[w] else c)
    return pltpu.make_async_remote_copy(
        src_ref=reg, dst_ref=reg, send_sem=send[p], recv_sem=recv[p],
        device_id=(chip[0], chip[1], c), device_id_type=MESH)


N_PEERS = N_CHIPS - 1
PEER_SEMS = [pltpu.SemaphoreType.DMA(())] * (2 * N_PEERS)


def _fetch_start(fulls, axes, whole, after, name):
    n = len(fulls)

    def body(*refs):
        ins = refs[:n]
        sems = refs[n + len(after):]
        send, recv = sems[:N_PEERS], sems[N_PEERS:2 * N_PEERS]
        token = refs[-1]
        _, _, c, others = _place()
        for w in range(n):
            for p, chip in enumerate(others):
                _fetch_copy(ins, axes, whole, send, recv, w, p, chip, c, False).start()
        token[...] = jnp.zeros_like(token)

    res = pl.pallas_call(
        body, name=name,
        in_specs=[IN_HBM] * n + [HBM_SPEC] * len(after),
        out_specs=[SEM_SPEC] * len(PEER_SEMS) + [IN_HBM] * n + [pl.BlockSpec(memory_space=pltpu.VMEM)],
        out_shape=PEER_SEMS + _hbm_like(fulls) + [TOKEN],
        input_output_aliases={i: len(PEER_SEMS) + i for i in range(n)},
        compiler_params=FLOWS,
    )(*_in_hbm(fulls), *after)
    k = len(PEER_SEMS)
    return list(res[:k]), list(res[k:k + n]), res[-1]


def _fetch_wait(sems, fulls, axes, whole, after, name):
    n = len(fulls)

    def body(*refs):
        ins = refs[:n]
        send, recv = refs[n:n + N_PEERS], refs[n + N_PEERS:n + 2 * N_PEERS]
        _, _, c, others = _place()
        for w in range(n):
            for p, chip in enumerate(others):
                _fetch_copy(ins, axes, whole, send, recv, w, p, chip, c, False).wait_send()
                _fetch_copy(ins, axes, whole, send, recv, w, p, chip, c, True).wait_recv()

    res = pl.pallas_call(
        body, name=name,
        in_specs=[IN_HBM] * n + [SEM_SPEC] * len(sems) + [HBM_SPEC] * len(after),
        out_specs=[IN_HBM] * n,
        out_shape=_hbm_like(fulls),
        input_output_aliases={i: i for i in range(n)},
        compiler_params=FLOWS,
    )(*fulls, *sems, *after)
    return list(res)


def _hand_on(fulls, axes, name):
    n = len(fulls)

    def body(*refs):
        outs = refs[n:2 * n]
        send, recv = refs[2 * n:]
        x, y, c, others = _place()

        def copy(w, p, chip, half):
            reg = _region(outs[w], axes[w], _chip_no(*chip), half)
            return pltpu.make_async_remote_copy(
                src_ref=reg, dst_ref=reg, send_sem=send.at[w, p], recv_sem=recv.at[w, p],
                device_id=(x, y, 1 - c), device_id_type=MESH)

        for w in range(n):
            for p, chip in enumerate(others):
                copy(w, p, chip, c).start()
        for w in range(n):
            for p, chip in enumerate(others):
                copy(w, p, chip, 1 - c).wait()

    return pl.pallas_call(
        body, name=name,
        in_specs=[HBM_SPEC] * n, out_specs=[HBM_SPEC] * n,
        out_shape=[jax.ShapeDtypeStruct(f.shape, f.dtype) for f in fulls],
        input_output_aliases={i: i for i in range(n)},
        scratch_shapes=[pltpu.SemaphoreType.DMA((n, 3)), pltpu.SemaphoreType.DMA((n, 3))],
        compiler_params=pltpu.CompilerParams(has_side_effects=True),
    )(*fulls)


def _pair_exchange(grads, name):
    n = len(grads)

    def body(*refs):
        ins, outs = refs[:n], refs[n:2 * n]
        send, recv = refs[2 * n:]
        x, y, c, _ = _place()
        cps = []
        for w in range(n):
            cp = pltpu.make_async_remote_copy(
                src_ref=ins[w].at[:, 1 - c], dst_ref=outs[w], send_sem=send.at[w], recv_sem=recv.at[w],
                device_id=(x, y, 1 - c), device_id_type=MESH)
            cp.start()
            cps.append(cp)
        for cp in cps:
            cp.wait()

    return pl.pallas_call(
        body, name=name,
        in_specs=[HBM_SPEC] * n, out_specs=[HBM_SPEC] * n,
        out_shape=[jax.ShapeDtypeStruct((g.shape[0],) + g.shape[2:], g.dtype) for g in grads],
        scratch_shapes=[pltpu.SemaphoreType.DMA((n,)), pltpu.SemaphoreType.DMA((n,))],
        compiler_params=pltpu.CompilerParams(has_side_effects=True),
    )(*grads)


def _pair_add(g4, got, core, name):
    nj, _, hr, cdim = g4.shape
    tr = _tile(hr, max(8, 524288 // cdim))

    def body(core_ref, a_ref, b_ref, o_ref):
        o_ref[...] = (a_ref[...].astype(F32) + b_ref[...].astype(F32)).astype(o_ref.dtype)

    return pl.pallas_call(
        body, name=name,
        grid_spec=pltpu.PrefetchScalarGridSpec(
            num_scalar_prefetch=1, grid=(nj, hr // tr),
            in_specs=[pl.BlockSpec((1, None, tr, cdim), lambda j, i, core_ref: (j, core_ref[0], i, 0)),
                      pl.BlockSpec((1, tr, cdim), lambda j, i, core_ref: (j, i, 0))],
            out_specs=pl.BlockSpec((1, tr, cdim), lambda j, i, core_ref: (j, i, 0))),
        out_shape=jax.ShapeDtypeStruct((nj, hr, cdim), BF),
        compiler_params=_params(("parallel", "parallel")),
    )(core, g4, got)


def _piece(ref, axis, j, hc):
    if axis == 0:
        return ref.at[j]
    return ref.at[0, :, pl.ds(pl.multiple_of(j * hc, hc), hc)]


def _slot_shapes(sums, axes):
    return [(N_CHIPS - 1, sm.shape[1], sm.shape[2] // (1 if ax == 0 else N_CHIPS)) for sm, ax in zip(sums, axes)]


def _slot_copy(sums, lands, axes, send, recv, w, p, chip, c):
    return pltpu.make_async_remote_copy(
        src_ref=_piece(sums[w], axes[w], _chip_no(*chip), lands[w].shape[2]), dst_ref=lands[w].at[p],
        send_sem=send[p], recv_sem=recv[p],
        device_id=(chip[0], chip[1], c), device_id_type=MESH)


def _chip_exchange_start(sums, axes, name):
    n = len(sums)
    shapes = _slot_shapes(sums, axes)
    lands = [lax.empty(sh, sm.dtype) for sh, sm in zip(shapes, sums)]

    def body(*refs):
        ins, land_refs = refs[:n], refs[n:2 * n]
        send, recv = refs[2 * n:2 * n + N_PEERS], refs[2 * n + N_PEERS:2 * n + 2 * N_PEERS]
        token = refs[-1]
        _, _, c, others = _place()
        for w in range(n):
            for p, chip in enumerate(others):
                _slot_copy(ins, land_refs, axes, send, recv, w, p, chip, c).start()
        token[...] = jnp.zeros_like(token)

    k = len(PEER_SEMS)
    res = pl.pallas_call(
        body, name=name,
        in_specs=[IN_HBM] * (2 * n),
        out_specs=[SEM_SPEC] * k + [IN_HBM] * (2 * n) + [pl.BlockSpec(memory_space=pltpu.VMEM)],
        out_shape=PEER_SEMS + _hbm_like(list(sums) + lands) + [TOKEN],
        input_output_aliases={i: k + i for i in range(2 * n)},
        compiler_params=FLOWS,
    )(*_in_hbm(list(sums) + lands))
    return list(res[:k]), list(res[k:k + n]), list(res[k + n:k + 2 * n]), res[-1]


def _chip_exchange_wait(sems, sums, lands, axes, after, name):
    n = len(sums)

    def body(*refs):
        ins, land_refs = refs[:n], refs[n:2 * n]
        send, recv = refs[2 * n:2 * n + N_PEERS], refs[2 * n + N_PEERS:2 * n + 2 * N_PEERS]
        _, _, c, others = _place()
        for w in range(n):
            for p, chip in enumerate(others):
                cp = _slot_copy(ins, land_refs, axes, send, recv, w, p, chip, c)
                cp.wait_send()
                cp.wait_recv()

    res = pl.pallas_call(
        body, name=name,
        in_specs=[IN_HBM] * (2 * n) + [SEM_SPEC] * len(sems) + [HBM_SPEC] * len(after),
        out_specs=[IN_HBM] * (2 * n), out_shape=_hbm_like(list(sums) + list(lands)),
        input_output_aliases={i: i for i in range(2 * n)},
        compiler_params=FLOWS,
    )(*sums, *lands, *sems, *after)
    return list(res[:n]), list(res[n:])


def _chip_sum(psum, slots, axis, place, name):
    _, hr, hc = slots.shape
    tr = _tile(hr, 256)
    own_map = (lambda i, pref: (0, i, pref[0])) if axis == 1 else (lambda i, pref: (pref[0], i, 0))

    def body(pref, own_ref, s_ref, o_ref):
        o_ref[...] = ((own_ref[...].astype(F32) + s_ref[0].astype(F32)) + s_ref[1].astype(F32)) + s_ref[2].astype(F32)

    return pl.pallas_call(
        body, name=name,
        grid_spec=pltpu.PrefetchScalarGridSpec(
            num_scalar_prefetch=1, grid=(hr // tr,),
            in_specs=[pl.BlockSpec((None, tr, hc), own_map),
                      pl.BlockSpec((N_CHIPS - 1, tr, hc), lambda i, pref: (0, i, 0))],
            out_specs=pl.BlockSpec((None, tr, hc), lambda i, pref: (pref[1], i, 0))),
        out_shape=jax.ShapeDtypeStruct((2, hr, hc), F32),
        compiler_params=_params(("parallel",)),
    )(place, psum, slots)


def _half_swap(both):
    n = len(both)

    def body(*refs):
        outs = refs[n:2 * n]
        send, recv = refs[2 * n:]
        x, y, c, _ = _place()

        def copy(w, half):
            return pltpu.make_async_remote_copy(
                src_ref=outs[w].at[half], dst_ref=outs[w].at[half], send_sem=send.at[w], recv_sem=recv.at[w],
                device_id=(x, y, 1 - c), device_id_type=MESH)

        for w in range(n):
            copy(w, c).start()
        for w in range(n):
            copy(w, 1 - c).wait()

    return pl.pallas_call(
        body, name="grad_half_swap",
        in_specs=[HBM_SPEC] * n, out_specs=[HBM_SPEC] * n,
        out_shape=[jax.ShapeDtypeStruct(b.shape, b.dtype) for b in both],
        input_output_aliases={i: i for i in range(n)},
        scratch_shapes=[pltpu.SemaphoreType.DMA((n,)), pltpu.SemaphoreType.DMA((n,))],
        compiler_params=pltpu.CompilerParams(has_side_effects=True),
    )(*both)


def _allreduce_small(pack):
    rows, d = pack.shape

    def body(p_ref, o_ref, slots, send, recv):
        x, y, c, _ = _place()
        me = 4 * x + 2 * y + c
        slots[me] = p_ref[...]
        cps = []
        for k in range(1, N_DEV):
            px, py, pc = x ^ (k >> 2), y ^ ((k >> 1) & 1), c ^ (k & 1)
            cp = pltpu.make_async_remote_copy(
                src_ref=p_ref, dst_ref=slots.at[me], send_sem=send.at[k - 1], recv_sem=recv.at[k - 1],
                device_id=(px, py, pc), device_id_type=MESH)
            cp.start()
            cps.append(cp)
        for k in range(1, N_DEV):
            px, py, pc = x ^ (k >> 2), y ^ ((k >> 1) & 1), c ^ (k & 1)
            arrival = pltpu.make_async_remote_copy(
                src_ref=p_ref, dst_ref=slots.at[4 * px + 2 * py + pc], send_sem=send.at[k - 1],
                recv_sem=recv.at[k - 1], device_id=(px, py, pc), device_id_type=MESH)
            arrival.wait_recv()
            arrival.wait_send()
        acc = slots[0]
        for k in range(1, N_DEV):
            acc = acc + slots[k]
        o_ref[...] = acc

    vm = pl.BlockSpec(memory_space=pltpu.VMEM)
    return pl.pallas_call(
        body, name="allreduce_small", in_specs=[vm], out_specs=vm,
        out_shape=jax.ShapeDtypeStruct((rows, d), F32),
        scratch_shapes=[pltpu.VMEM((N_DEV, rows, d), F32), pltpu.SemaphoreType.DMA((N_DEV - 1,)),
                        pltpu.SemaphoreType.DMA((N_DEV - 1,))],
        compiler_params=pltpu.CompilerParams(has_side_effects=True),
    )(pack)


def _adamw(w, g, m, v, name):
    rows, cols = w.shape

    def fn(wv, gv, mv, vv):
        m2 = ADAM_B1 * mv + (1.0 - ADAM_B1) * gv
        v2 = ADAM_B2 * vv + (1.0 - ADAM_B2) * (gv * gv)
        m_hat = m2 / (1.0 - ADAM_B1 ** ADAM_STEP)
        v_hat = v2 / (1.0 - ADAM_B2 ** ADAM_STEP)
        delta = -ADAM_LR * (m_hat / (jnp.sqrt(v_hat) + ADAM_EPS) + ADAM_WD * wv)
        return delta, m2, v2

    ins = [(a, cols, 0) for a in (w, g, m, v)]
    return _rowwise(fn, ins, [(cols, F32)] * 3, rows=rows, tm=_tile(rows, max(8, 262144 // cols)), name=name)


BIG = ["w_in", "w_branch_a", "w_branch_b", "w_mix_out", "w_mem_q", "w_mem_kv", "w_mem_o", "w_ffn_in", "w_ffn_out"]
BIG_AXIS = {"w_in": 1, "w_branch_a": 1, "w_branch_b": 1, "w_mix_out": 0, "w_mem_q": 0, "w_mem_kv": 1,
            "w_mem_o": 0, "w_ffn_in": 1, "w_ffn_out": 0}
NORMS = ["norm_mix", "norm_mem_q", "norm_mem_kv", "norm_ffn", "norm_final"]
ORDER = ["norm_mix", "w_in", "conv_w", "w_branch_a", "w_branch_b", "w_mix_out", "norm_mem_q", "norm_mem_kv",
         "w_mem_q", "w_mem_kv", "w_mem_o", "norm_ffn", "w_ffn_in", "w_ffn_out", "norm_final"]


def _pack_small(vals, conv):
    d = vals[0].shape[-1]
    rows = [v.reshape(1, d) for v in vals]
    conv = jnp.pad(conv, ((0, 0), (0, d - conv.shape[1])))
    pad = jnp.zeros((SMALL_ROWS - len(rows) - CONV_K, d), F32)
    return jnp.concatenate(rows + [conv, pad], axis=0)


def kernel(x, mem, norm_mix, w_in, conv_w, w_branch_a, w_branch_b, w_mix_out, norm_mem_q, norm_mem_kv, w_mem_q, w_mem_kv, w_mem_o, norm_ffn, w_ffn_in, w_ffn_out, norm_final, loss_target, m_norm_mix, m_w_in, m_conv_w, m_w_branch_a, m_w_branch_b, m_w_mix_out, m_norm_mem_q, m_norm_mem_kv, m_w_mem_q, m_w_mem_kv, m_w_mem_o, m_norm_ffn, m_w_ffn_in, m_w_ffn_out, m_norm_final, v_norm_mix, v_w_in, v_conv_w, v_w_branch_a, v_w_branch_b, v_w_mix_out, v_norm_mem_q, v_norm_mem_kv, v_w_mem_q, v_w_mem_kv, v_w_mem_o, v_norm_ffn, v_w_ffn_in, v_w_ffn_out, v_norm_final):
    args = dict(locals())
    wts = {n: args[n] for n in ORDER}
    mom = {n: args["m_" + n] for n in ORDER}
    var = {n: args["v_" + n] for n in ORDER}
    x = x[0]
    mem = mem[0]
    target = loss_target[0]
    s, d = x.shape
    gains = {n: wts[n].reshape(1, d) for n in NORMS}
    chip = 2 * lax.axis_index("x") + lax.axis_index("y")
    core = lax.axis_index("c").astype(jnp.int32).reshape(1)
    place = jnp.stack([chip, lax.axis_index("c")]).astype(jnp.int32)

    conv_shard = jnp.pad(conv_w[0], ((0, CONV_ROWS - CONV_K), (0, 0)))
    first_axes, first_whole = [BIG_AXIS["w_in"], 1], [False, True]
    first = [_cast_place(wts["w_in"][0], BIG_AXIS["w_in"], place, BF, "place_w_in"),
             _cast_place(conv_shard, 1, place, F32, "place_conv_w")]
    a_sems, first, a_token = _fetch_start(first, first_axes, first_whole, (), "fetch_start_first")
    later_w = [n for n in BIG if n != "w_in"]
    later_axes = [BIG_AXIS[n] for n in later_w]
    placed = [_cast_place(wts[n][0], BIG_AXIS[n], place, BF, "place_" + n, after=(a_token,)) for n in later_w]
    h1 = _rmsnorm(x, gains["norm_mix"], "norm_mix_fwd", after=(a_token,))
    first = _fetch_wait(a_sems, first, first_axes, first_whole, [h1] + placed, "fetch_wait_first")
    conv_full = first[1]
    W = {"w_in": _hand_on(first[:1], first_axes[:1], "gather_hand_on_first")[0]}
    f_sems, later_bufs, f_token = _fetch_start(placed, later_axes, [False] * len(placed), (W["w_in"],), "fetch_start")

    proj = _matmul(h1, W["w_in"], tn=1280, after=(f_token,), name="in_proj")
    o_a, o_a32 = _sb_fwd(proj)
    y_b = _conv_fwd(proj, conv_full)
    later_bufs = _fetch_wait(f_sems, later_bufs, later_axes, [False] * len(placed), (o_a, y_b), "fetch_wait")
    W.update(zip(later_w, _hand_on(later_bufs, later_axes, "gather_hand_on")))
    br_a = _matmul(o_a, W["w_branch_a"], tn=1024, name="branch_a")
    br_b = _matmul(y_b, W["w_branch_b"], tn=1024, name="branch_b")
    ga_blk, gb_blk = 3, 4

    def merge(ga, gb, a, b):
        return jax.nn.sigmoid(ga.astype(F32)) * a.astype(F32) + jax.nn.sigmoid(gb.astype(F32)) * b.astype(F32)

    merged = _rowwise(merge, [(proj, d, ga_blk), (proj, d, gb_blk), (br_a, d, 0), (br_b, d, 0)], [(d, BF)],
                      rows=s, tm=512, name="merge_fwd")[0]
    x1 = _matmul(merged, W["w_mix_out"], tn=1024, out_dtype=F32, resid=x, name="mix_out")

    hq = _rmsnorm(x1, gains["norm_mem_q"], "norm_mem_q_fwd")
    mn = _rmsnorm(mem, gains["norm_mem_kv"], "norm_mem_kv_fwd")
    q_m = _matmul(hq, W["w_mem_q"], tn=1024, name="mem_q")
    kv = _matmul(mn, W["w_mem_kv"], tn=1024, name="mem_kv")
    o_m = _mem_fwd(q_m, kv)
    x2 = _matmul(o_m, W["w_mem_o"], tn=1024, out_dtype=F32, resid=x1, name="mem_o")

    hf = _rmsnorm(x2, gains["norm_ffn"], "norm_ffn_fwd")
    gu = _matmul(hf, W["w_ffn_in"], tn=1408, name="ffn_in")
    f = gu.shape[1] // 2

    def swiglu(gate, up):
        gate = gate.astype(F32)
        return gate * jax.nn.sigmoid(gate) * up.astype(F32)

    act = _rowwise(swiglu, [(gu, f, 0), (gu, f, 1)], [(f, BF)], rows=s, tm=512, name="swiglu_fwd")[0]
    x3 = _matmul(act, W["w_ffn_out"], tn=1024, out_dtype=F32, resid=x2, name="ffn_out")

    dx3, dx3_b, dg_final, loss_part = _loss_head(x3, gains["norm_final"], target, "loss_head")

    dact = _matmul(dx3_b, W["w_ffn_out"], tb=True, tn=1408, name="d_act")
    gw = {"w_ffn_out": _matmul(act, dx3_b, ta=True, tm=1408, tn=512, name="gw_ffn_out")}

    def swiglu_bwd(gate, up, da):
        gate, up, da = gate.astype(F32), up.astype(F32), da.astype(F32)
        sg = jax.nn.sigmoid(gate)
        dgate = da * up * (sg * (1.0 + gate * (1.0 - sg)))
        return jnp.concatenate([dgate, da * (gate * sg)], axis=-1)

    dgu = _rowwise(swiglu_bwd, [(gu, f, 0), (gu, f, 1), (dact, f, 0)], [(2 * f, BF)], rows=s, tm=256,
                   name="swiglu_bwd")[0]
    dx2, dx2_b, dg_ffn = _matmul_norm_bwd(dgu, W["w_ffn_in"], x2, gains["norm_ffn"], dx3, tm=256, bf_copy=True,
                                          name="d_hf_norm_bwd")
    gw["w_ffn_in"] = _matmul(hf, dgu, ta=True, tn=512, name="gw_ffn_in")

    def reduce_start(names, tag):
        views = []
        for n in names:
            r, cdim = gw[n].shape
            views.append(gw[n].reshape(1, 2, r // 2, cdim) if BIG_AXIS[n] == 1
                         else gw[n].reshape(N_CHIPS, 2, r // (2 * N_CHIPS), cdim))
        got = _pair_exchange(views, "grad_pair_exchange_" + tag)
        sums = [_pair_add(v, g, core, "pair_add_" + n) for n, v, g in zip(names, views, got)]
        group_axes = [BIG_AXIS[n] for n in names]
        sems, sums, lands, token = _chip_exchange_start(sums, group_axes, "grad_chip_start_" + tag)
        return (names, group_axes, sems, sums, lands), token

    def reduce_finish(group, after, tag):
        names, group_axes, sems, sums, lands = group
        sums, slots = _chip_exchange_wait(sems, sums, lands, group_axes, after, "grad_chip_wait_" + tag)
        return [_chip_sum(sm, sl, BIG_AXIS[n], place, "chip_sum_" + n) for n, sm, sl in zip(names, sums, slots)]

    group_ffn, token_ffn = reduce_start(["w_ffn_in", "w_ffn_out"], "ffn")

    do_m = _matmul(dx2_b, W["w_mem_o"], tb=True, tn=1024, after=(token_ffn,), name="d_o_m")
    gw["w_mem_o"] = _matmul(o_m, dx2_b, ta=True, tn=512, name="gw_mem_o")
    dq_m, dk_m, dv_m = _mem_bwd(q_m, kv, do_m)
    dkv = jnp.concatenate([dk_m, dv_m], axis=-1)
    dx1, dx1_b, dg_q = _matmul_norm_bwd(dq_m, W["w_mem_q"], x1, gains["norm_mem_q"], dx2, tm=512, bf_copy=True,
                                        name="d_hq_norm_bwd")
    gw["w_mem_q"] = _matmul(hq, dq_m, ta=True, tn=512, name="gw_mem_q")
    dmn = _matmul(dkv, W["w_mem_kv"], tb=True, tn=1024, out_dtype=F32, name="d_mn")
    gw["w_mem_kv"] = _matmul(mn, dkv, ta=True, tn=1024, name="gw_mem_kv")
    _, dg_kv = _rmsnorm_bwd(mem, gains["norm_mem_kv"], dmn, None, "norm_mem_kv_bwd")

    dmerged = _matmul(dx1_b, W["w_mix_out"], tb=True, tn=1024, name="d_merged")
    gw["w_mix_out"] = _matmul(merged, dx1_b, ta=True, tn=512, name="gw_mix_out")

    def merge_bwd(ga, gb, a, b, dm):
        dm = dm.astype(F32)
        sa, sb = jax.nn.sigmoid(ga.astype(F32)), jax.nn.sigmoid(gb.astype(F32))
        a, b = a.astype(F32), b.astype(F32)
        return dm * sa, dm * sb, dm * a * (sa * (1.0 - sa)), dm * b * (sb * (1.0 - sb))

    dbr_a, dbr_b, dga, dgb = _rowwise(
        merge_bwd, [(proj, d, ga_blk), (proj, d, gb_blk), (br_a, d, 0), (br_b, d, 0), (dmerged, d, 0)],
        [(d, BF)] * 4, rows=s, tm=256, name="merge_bwd")
    do_a = _matmul(dbr_a, W["w_branch_a"], tb=True, name="d_o_a")
    gw["w_branch_a"] = _matmul(o_a, dbr_a, ta=True, tn=512, name="gw_branch_a")
    dy_b = _matmul(dbr_b, W["w_branch_b"], tb=True, name="d_y_b")
    gw["w_branch_b"] = _matmul(y_b, dbr_b, ta=True, tn=512, name="gw_branch_b")
    group_mid, token_mid = reduce_start(
        ["w_mem_o", "w_mem_q", "w_mem_kv", "w_mix_out", "w_branch_a", "w_branch_b"], "mid")
    du, dgate_b, dgate_c, dconv = _conv_bwd(proj, conv_full, dy_b)
    dq, dk, dv = _sb_bwd(proj, o_a32, do_a, after=(token_mid,))

    def assemble(*parts):
        return jnp.concatenate([p.astype(BF) for p in parts], axis=-1)

    hw = dq.shape[1]
    dproj = _rowwise(assemble, [(t, hw, 0) for t in (dq, dk, dv, du, dgate_b, dgate_c)] + [(dga, d, 0), (dgb, d, 0)],
                     [(proj.shape[1], BF)], rows=s, tm=256, name="assemble_dproj")[0]
    gw["w_in"] = _matmul(h1, dproj, ta=True, tn=640, name="gw_in")
    group_in, token_in = reduce_start(["w_in"], "in")
    grad_x, dg_mix = _matmul_norm_bwd(dproj, W["w_in"], x, gains["norm_mix"], dx1, tm=256, bf_copy=False,
                                      after=(token_in,), name="d_h1_norm_bwd")

    halves = {}
    for group, tag in ((group_ffn, "ffn"), (group_mid, "mid"), (group_in, "in")):
        halves.update(zip(group[0], reduce_finish(group, (grad_x,), tag)))
    both = _half_swap([halves[n] for n in BIG])
    grads = {n: b.reshape(wts[n].shape[1:]) for n, b in zip(BIG, both)}

    small_g = [dg_mix, dg_q, dg_kv, dg_ffn, dg_final]
    pack = _pack_small(small_g, dconv[:CONV_K])
    pack = pack.at[ROW_LOSS].set(jnp.broadcast_to(loss_part[0, :1], (d,)))
    red = _allreduce_small(pack)
    loss = red[ROW_LOSS, 0]
    cw = conv_w.shape[2]
    conv_g = lax.dynamic_slice(red, (ROW_CONV, chip * cw), (CONV_K, cw))
    small_grad = _pack_small([red[i] for i in range(len(NORMS))], conv_g)
    small = [_pack_small([t[n] for n in NORMS], t["conv_w"][0]) for t in (wts, mom, var)]
    s_delta, s_m, s_v = _adamw(small[0], small_grad, small[1], small[2], "adamw_small")

    out = {"grad": {}, "delta": {}, "new_m": {}, "new_v": {}}
    for n in BIG:
        shp = wts[n].shape
        dl, m2, v2 = _adamw(wts[n][0], grads[n], mom[n][0], var[n][0], "adamw_" + n)
        out["grad"][n] = grads[n].reshape(shp)
        out["delta"][n], out["new_m"][n], out["new_v"][n] = dl.reshape(shp), m2.reshape(shp), v2.reshape(shp)
    for key, blk in (("grad", small_grad), ("delta", s_delta), ("new_m", s_m), ("new_v", s_v)):
        for i, n in enumerate(NORMS):
            out[key][n] = blk[i].reshape(wts[n].shape)
        out[key]["conv_w"] = blk[ROW_CONV:ROW_CONV + CONV_K, :cw].reshape(conv_w.shape)

    return (loss, grad_x[None], *[out["grad"][n] for n in ORDER], *[out["delta"][n] for n in ORDER],
            *[out["new_m"][n] for n in ORDER], *[out["new_v"][n] for n in ORDER])
```

```python
import functools
import math

import jax
import jax.numpy as jnp
from jax import lax
from jax.experimental import pallas as pl
from jax.experimental.pallas import tpu as pltpu

BF = jnp.bfloat16
F32 = jnp.float32
MESH = pl.DeviceIdType.MESH

SB_HEAD_DIM = 64
LANES = 128
MEM_HEADS = 4
CONV_K = 3
CONV_ROWS = 8
EPS = 1e-6
N_CHIPS = 4
N_DEV = 8
VMEM_LIMIT = 56 * 1024 * 1024

ADAM_LR = 0.001
ADAM_B1 = 0.9
ADAM_B2 = 0.999
ADAM_EPS = 1e-08
ADAM_WD = 0.01
ADAM_STEP = 10

SMALL_ROWS = 16
ROW_CONV = 5
ROW_LOSS = 8


def _params(sem=None, **kw):
    return pltpu.CompilerParams(dimension_semantics=sem, vmem_limit_bytes=VMEM_LIMIT, **kw)


def _tile(dim, pref):
    if dim <= pref:
        return dim
    for step in (LANES, 8):
        t = (pref // step) * step
        while t >= step:
            if dim % t == 0:
                return t
            t -= step
    raise ValueError(f"no tile of {dim} under {pref}")


def _matmul(a, b, *, ta=False, tb=False, tm=1024, tn=512, tk=None, out_dtype=BF, resid=None, after=(), name):
    if ta:
        kdim, m = a.shape
    else:
        m, kdim = a.shape
    n = b.shape[0] if tb else b.shape[1]
    tm, tn = _tile(m, tm), _tile(n, tn)
    tk = _tile(kdim, tk or kdim)
    nk = kdim // tk
    a_spec = pl.BlockSpec((tk, tm), lambda i, j, k: (k, i)) if ta else pl.BlockSpec((tm, tk), lambda i, j, k: (i, k))
    b_spec = pl.BlockSpec((tn, tk), lambda i, j, k: (j, k)) if tb else pl.BlockSpec((tk, tn), lambda i, j, k: (k, j))
    o_spec = pl.BlockSpec((tm, tn), lambda i, j, k: (i, j))
    dims = (((0 if ta else 1,), (1 if tb else 0,)), ((), ()))
    has_res = resid is not None

    def body(*refs):
        a_ref, b_ref = refs[0], refs[1]
        o_ref = refs[2 + has_res + len(after)]
        av, bv = a_ref[...], b_ref[...]
        if av.dtype != BF:
            av = av.astype(BF)
        if bv.dtype != BF:
            bv = bv.astype(BF)
        p = lax.dot_general(av, bv, dims, preferred_element_type=F32)

        def finish(acc):
            if has_res:
                acc = refs[2][...] + acc
            o_ref[...] = acc.astype(o_ref.dtype)

        if nk == 1:
            finish(p)
        else:
            acc_ref = refs[-1]
            k = pl.program_id(2)

            @pl.when(k == 0)
            def _():
                acc_ref[...] = p

            @pl.when(k > 0)
            def _():
                acc_ref[...] += p

            @pl.when(k == nk - 1)
            def _():
                finish(acc_ref[...])

    return pl.pallas_call(
        body, name=name, grid=(m // tm, n // tn, nk),
        in_specs=[a_spec, b_spec] + ([o_spec] if has_res else []) + [HBM_SPEC] * len(after),
        out_specs=o_spec, out_shape=jax.ShapeDtypeStruct((m, n), out_dtype),
        scratch_shapes=[pltpu.VMEM((tm, tn), F32)] if nk > 1 else [],
        compiler_params=_params(("parallel", "parallel", "arbitrary")),
    )(*([a, b] + ([resid] if has_res else []) + list(after)))


def _rowwise(fn, ins, outs, *, rows, tm, name, accs=(), after=()):
    tm = _tile(rows, tm)
    in_specs, args = [], []
    for arr, cols, cb in ins:
        if cols is None:
            in_specs.append(pl.BlockSpec(arr.shape, lambda i, nd=arr.ndim: (0,) * nd))
        else:
            in_specs.append(pl.BlockSpec((tm, cols), lambda i, cb=cb: (i, cb)))
        args.append(arr)
    out_specs = [pl.BlockSpec((tm, cols), lambda i: (i, 0)) for cols, _ in outs]
    out_shape = [jax.ShapeDtypeStruct((rows, cols), dt) for cols, dt in outs]
    for r, c in accs:
        out_specs.append(pl.BlockSpec((r, c), lambda i: (0, 0)))
        out_shape.append(jax.ShapeDtypeStruct((r, c), F32))
    n_in, n_out = len(ins), len(outs)
    in_specs += [HBM_SPEC] * len(after)
    args += list(after)

    def body(*refs):
        res = fn(*[r[...] for r in refs[:n_in]])
        if not isinstance(res, (tuple, list)):
            res = (res,)
        refs = refs[n_in + len(after):]
        for o_ref, val in zip(refs[:n_out], res[:n_out]):
            o_ref[...] = val.astype(o_ref.dtype)
        first = pl.program_id(0) == 0
        for a_ref, val in zip(refs[n_out:], res[n_out:]):
            @pl.when(first)
            def _(a_ref=a_ref, val=val):
                a_ref[...] = val

            @pl.when(jnp.logical_not(first))
            def _(a_ref=a_ref, val=val):
                a_ref[...] += val

    res = pl.pallas_call(
        body, name=name, grid=(rows // tm,), in_specs=in_specs, out_specs=out_specs, out_shape=out_shape,
        compiler_params=_params(("arbitrary",) if accs else ("parallel",)),
    )(*args)
    return res


def _rstd(xf):
    return lax.rsqrt(jnp.mean(xf * xf, axis=-1, keepdims=True) + EPS)


def _rmsnorm(x, g, name, after=()):
    rows, d = x.shape
    return _rowwise(lambda xv, gv: xv * _rstd(xv) * gv, [(x, d, 0), (g, None, None)], [(d, BF)],
                    rows=rows, tm=512, name=name, after=after)[0]


def _rmsnorm_bwd(x, g, dy, resid, name, bf_copy=False):
    rows, d = x.shape

    def fn(xv, gv, dyv, *rest):
        dyv = dyv.astype(F32)
        r = _rstd(xv)
        xn = xv * r
        dxn = dyv * gv
        dx = r * (dxn - xn * jnp.mean(dxn * xn, axis=-1, keepdims=True))
        if rest:
            dx = rest[0] + dx
        return (dx,) * (1 + bf_copy) + (jnp.sum(dyv * xn, axis=0, keepdims=True),)

    ins = [(x, d, 0), (g, None, None), (dy, d, 0)] + ([(resid, d, 0)] if resid is not None else [])
    outs = [(d, F32)] + ([(d, BF)] if bf_copy else [])
    return _rowwise(fn, ins, outs, rows=rows, tm=512, name=name, accs=[(1, d)])


def _matmul_norm_bwd(a, w, x, g, resid, *, tm, bf_copy, name, after=()):
    m, kdim = a.shape
    d = w.shape[0]
    tm = _tile(m, tm)
    row = lambda i: (i, 0)
    whole = lambda i: (0, 0)

    def body(a_ref, w_ref, x_ref, g_ref, r_ref, *rest):
        outs = rest[len(after):]
        dy = lax.dot_general(a_ref[...], w_ref[...], NT, preferred_element_type=F32)
        xv = x_ref[...]
        r = _rstd(xv)
        xn = xv * r
        dxn = dy * g_ref[...]
        dx = r_ref[...] + r * (dxn - xn * jnp.mean(dxn * xn, axis=-1, keepdims=True))
        outs[0][...] = dx
        if bf_copy:
            outs[1][...] = dx.astype(BF)
        dg = jnp.sum(dy * xn, axis=0, keepdims=True)
        first = pl.program_id(0) == 0

        @pl.when(first)
        def _():
            outs[-1][...] = dg

        @pl.when(jnp.logical_not(first))
        def _():
            outs[-1][...] += dg

    tok = pl.BlockSpec((tm, d), row)
    out_specs = [tok] + ([tok] if bf_copy else []) + [pl.BlockSpec((1, d), whole)]
    out_shape = ([jax.ShapeDtypeStruct((m, d), F32)] + ([jax.ShapeDtypeStruct((m, d), BF)] if bf_copy else [])
                 + [jax.ShapeDtypeStruct((1, d), F32)])
    return pl.pallas_call(
        body, name=name, grid=(m // tm,),
        in_specs=[pl.BlockSpec((tm, kdim), row), pl.BlockSpec((d, kdim), whole), tok, pl.BlockSpec((1, d), whole), tok]
        + [HBM_SPEC] * len(after),
        out_specs=out_specs, out_shape=out_shape,
        compiler_params=_params(("arbitrary",)),
    )(a, w, x, g, resid, *after)


def _ffn_in_swiglu(h, w, *, tm=1024, tn=1408):
    m, kdim = h.shape
    f = w.shape[1] // 2
    tm, tn = _tile(m, tm), _tile(f, tn)
    nj = f // tn

    def body(h_ref, wg_ref, wu_ref, gate_ref, up_ref, act_ref):
        hv = h_ref[...]
        gate = jnp.dot(hv, wg_ref[...], preferred_element_type=F32)
        up = jnp.dot(hv, wu_ref[...], preferred_element_type=F32)
        gate_ref[...] = gate.astype(gate_ref.dtype)
        up_ref[...] = up.astype(up_ref.dtype)
        act_ref[...] = (gate * jax.nn.sigmoid(gate) * up).astype(act_ref.dtype)

    tile = pl.BlockSpec((tm, tn), lambda i, j: (i, j))
    return pl.pallas_call(
        body, name="ffn_in_swiglu", grid=(m // tm, nj),
        in_specs=[pl.BlockSpec((tm, kdim), lambda i, j: (i, 0)),
                  pl.BlockSpec((kdim, tn), lambda i, j: (0, j)),
                  pl.BlockSpec((kdim, tn), lambda i, j: (0, nj + j))],
        out_specs=[tile, tile, tile], out_shape=[jax.ShapeDtypeStruct((m, f), BF)] * 3,
        compiler_params=_params(("parallel", "parallel")),
    )(h, w, w)


def _d_act_swiglu(dx, w, gate, up, *, tm=256):
    m, d = dx.shape
    f = w.shape[0]
    tm = _tile(m, tm)
    row = lambda i: (i, 0)

    def body(dx_ref, w_ref, gate_ref, up_ref, o_ref):
        da = lax.dot_general(dx_ref[...], w_ref[...], NT, preferred_element_type=F32)
        gv, uv = gate_ref[...].astype(F32), up_ref[...].astype(F32)
        sg = jax.nn.sigmoid(gv)
        dgate = da * uv * (sg * (1.0 + gv * (1.0 - sg)))
        o_ref[...] = jnp.concatenate([dgate, da * (gv * sg)], axis=-1).astype(o_ref.dtype)

    return pl.pallas_call(
        body, name="d_act_swiglu", grid=(m // tm,),
        in_specs=[pl.BlockSpec((tm, d), row), pl.BlockSpec((f, d), lambda i: (0, 0)),
                  pl.BlockSpec((tm, f), row), pl.BlockSpec((tm, f), row)],
        out_specs=pl.BlockSpec((tm, 2 * f), row), out_shape=jax.ShapeDtypeStruct((m, 2 * f), BF),
        compiler_params=_params(("parallel",)),
    )(dx, w, gate, up)


GATE_A_BLK, GATE_B_BLK = 3, 4


def _branches_merge(o_a, y_b, w_a, w_b, proj, *, tm=1024):
    m, kdim = o_a.shape
    d = w_a.shape[1]
    tm = _tile(m, tm)
    row = lambda i: (i, 0)

    def body(a_ref, b_ref, wa_ref, wb_ref, ga_ref, gb_ref, bra_ref, brb_ref, merged_ref):
        bra = jnp.dot(a_ref[...], wa_ref[...], preferred_element_type=F32)
        brb = jnp.dot(b_ref[...], wb_ref[...], preferred_element_type=F32)
        bra_ref[...] = bra.astype(bra_ref.dtype)
        brb_ref[...] = brb.astype(brb_ref.dtype)
        merged = jax.nn.sigmoid(ga_ref[...].astype(F32)) * bra + jax.nn.sigmoid(gb_ref[...].astype(F32)) * brb
        merged_ref[...] = merged.astype(merged_ref.dtype)

    tok = pl.BlockSpec((tm, d), row)
    return pl.pallas_call(
        body, name="branches_merge", grid=(m // tm,),
        in_specs=[pl.BlockSpec((tm, kdim), row), pl.BlockSpec((tm, kdim), row),
                  pl.BlockSpec((kdim, d), lambda i: (0, 0)), pl.BlockSpec((kdim, d), lambda i: (0, 0)),
                  pl.BlockSpec((tm, d), lambda i: (i, GATE_A_BLK)), pl.BlockSpec((tm, d), lambda i: (i, GATE_B_BLK))],
        out_specs=[tok, tok, tok], out_shape=[jax.ShapeDtypeStruct((m, d), BF)] * 3,
        compiler_params=_params(("parallel",)),
    )(o_a, y_b, w_a, w_b, proj, proj)


def _d_merged_gates(dx, w_mix, proj, br_a, br_b, *, tm=512):
    m, d = dx.shape
    tm = _tile(m, tm)
    row = lambda i: (i, 0)

    def body(dx_ref, w_ref, ga_ref, gb_ref, bra_ref, brb_ref, dbra_ref, dbrb_ref, dga_ref, dgb_ref):
        dm = lax.dot_general(dx_ref[...], w_ref[...], NT, preferred_element_type=F32)
        sa, sb = jax.nn.sigmoid(ga_ref[...].astype(F32)), jax.nn.sigmoid(gb_ref[...].astype(F32))
        dbra_ref[...] = (dm * sa).astype(dbra_ref.dtype)
        dbrb_ref[...] = (dm * sb).astype(dbrb_ref.dtype)
        dga_ref[...] = (dm * bra_ref[...].astype(F32) * (sa * (1.0 - sa))).astype(dga_ref.dtype)
        dgb_ref[...] = (dm * brb_ref[...].astype(F32) * (sb * (1.0 - sb))).astype(dgb_ref.dtype)

    tok = pl.BlockSpec((tm, d), row)
    return pl.pallas_call(
        body, name="d_merged_gates", grid=(m // tm,),
        in_specs=[tok, pl.BlockSpec((d, d), lambda i: (0, 0)),
                  pl.BlockSpec((tm, d), lambda i: (i, GATE_A_BLK)), pl.BlockSpec((tm, d), lambda i: (i, GATE_B_BLK)),
                  tok, tok],
        out_specs=[tok] * 4, out_shape=[jax.ShapeDtypeStruct((m, d), BF)] * 4,
        compiler_params=_params(("parallel",)),
    )(dx, w_mix, proj, proj, br_a, br_b)


def _loss_head(x, g, target, name):
    rows, d = x.shape

    def fn(xv, gv, tv):
        r = _rstd(xv)
        xn = xv * r
        err = xn * gv - tv
        per_tok = jnp.mean(err * err, axis=-1, keepdims=True)
        loss = 0.5 * jnp.sum(per_tok, axis=0, keepdims=True)
        dyv = err * (1.0 / d)
        dxn = dyv * gv
        dx = r * (dxn - xn * jnp.mean(dxn * xn, axis=-1, keepdims=True))
        return dx, dx, jnp.sum(dyv * xn, axis=0, keepdims=True), jnp.broadcast_to(loss, (1, LANES))

    return _rowwise(fn, [(x, d, 0), (g, None, None), (target, d, 0)], [(d, F32), (d, BF)],
                    rows=rows, tm=512, name=name, accs=[(1, d), (1, LANES)])


SB_TK = 128
SB_KT = 4


def _sb_consts(tq):
    tk = SB_TK
    diff = lax.broadcasted_iota(jnp.int32, (tq, tk), 1) - lax.broadcasted_iota(jnp.int32, (tq, tk), 0)
    rj = lax.broadcasted_iota(jnp.int32, (2 * tk, 2 * tk), 0) & (tk - 1)
    cj = lax.broadcasted_iota(jnp.int32, (2 * tk, 2 * tk), 1)
    ones_half = cj >= tk
    later = jnp.where((rj > cj) | ones_half, 1.0, 0.0).astype(BF)
    later_incl = jnp.where((rj >= cj) | ones_half, 1.0, 0.0).astype(BF)
    return diff, later, later_incl


def _split_dot(val, rhs_twice):
    hi = val.astype(BF)
    lo = (val - hi.astype(F32)).astype(BF)
    return jnp.dot(jnp.concatenate([hi, lo], axis=1), rhs_twice, preferred_element_type=F32)


def _log_terms(z):
    sp = jnp.maximum(z, 0.0) + jnp.log(1.0 + jnp.exp(-jnp.abs(z)))
    return z - sp, sp


NT = (((1,), (1,)), ((), ()))
TN = (((0,), (0,)), ((), ()))


def _head_lane_masks(rows):
    lane = lax.broadcasted_iota(jnp.int32, (rows, LANES), 1)
    first = jnp.where(lane < SB_HEAD_DIM, 1.0, 0.0)
    return first.astype(BF), (1.0 - first).astype(BF)


def _tail(a, r0):
    return a if r0 == 0 else a[r0:]


def _add_tail(a, r0, delta):
    return a + delta if r0 == 0 else jnp.concatenate([a[:r0], a[r0:] + delta], axis=0)


def _both_heads(tile, masks):
    return jnp.concatenate([tile * masks[0], tile * masks[1]], axis=0)


def _sb_fwd(proj):
    s = proj.shape[0]
    tk, tq = SB_TK, SB_KT * SB_TK
    n_pairs = 4
    scale = 1.0 / math.sqrt(SB_HEAD_DIM)

    def body(q_ref, k_ref, v_ref, o_ref, o32_ref):
        i = pl.program_id(1)
        diff, later, _ = _sb_consts(tq)
        qs = (q_ref[...].astype(F32) * scale).astype(BF)
        lane_masks = _head_lane_masks(tk)

        def step(g, state, masked):
            tiles = list(reversed(range(SB_KT)))
            chains = [(t, h) for t in tiles for h in range(2)]
            rows = {t: pl.ds(pl.multiple_of((g * SB_KT + t) * tk, tk), tk) for t in tiles}
            ks = {t: _both_heads(k_ref[rows[t], :], lane_masks) for t in tiles}
            vs = {t: _both_heads(v_ref[rows[t], :], lane_masks) for t in tiles}
            r0 = {t: t * tk if masked else 0 for t in tiles}
            allowed = {t: _tail(diff, r0[t]) < -t * tk for t in tiles}
            zs = {t: lax.dot_general(_tail(qs, r0[t]), ks[t], NT, preferred_element_type=F32) for t in tiles}
            logs = {}
            for t, h in chains:
                log_b, sp = _log_terms(zs[t][:, h * tk:(h + 1) * tk])
                logs[t, h] = (log_b, jnp.where(allowed[t], sp, 0.0) if masked else sp)
            sums = {c: _split_dot(logs[c][1], later) for c in chains}
            carries = list(state[0])
            ws = {}
            for t, h in chains:
                w = jnp.exp(logs[t, h][0] - (sums[t, h][:, :tk] + _tail(carries[h], r0[t])))
                ws[t, h] = (jnp.where(allowed[t], w, 0.0) if masked else w).astype(BF)
                carries[h] = _add_tail(carries[h], r0[t], sums[t, h][:, tk:])
            acc = state[1]
            for t in tiles:
                acc = _add_tail(acc, r0[t], jnp.dot(jnp.concatenate([ws[t, 0], ws[t, 1]], axis=1), vs[t],
                                                    preferred_element_type=F32))
            return tuple(carries), acc

        zero = jnp.zeros((tq, LANES), F32)
        state = step(i, ((zero, zero), zero), True)
        state = lax.fori_loop(0, i, lambda gg, st: step(i - 1 - gg, st, False), state)
        o_ref[...] = state[1].astype(o_ref.dtype)
        o32_ref[...] = state[1]

    tok = pl.BlockSpec((tq, LANES), lambda p, i: (i, p))
    return pl.pallas_call(
        body, name="sb_attn_fwd", grid=(n_pairs, s // tq),
        in_specs=[tok,
                  pl.BlockSpec((s, LANES), lambda p, i: (0, n_pairs + p)),
                  pl.BlockSpec((s, LANES), lambda p, i: (0, 2 * n_pairs + p))],
        out_specs=[tok, tok],
        out_shape=[jax.ShapeDtypeStruct((s, n_pairs * LANES), BF), jax.ShapeDtypeStruct((s, n_pairs * LANES), F32)],
        compiler_params=_params(("parallel", "arbitrary")),
    )(proj, proj, proj)


def _sb_bwd(proj, o32, do_a, after=()):
    s = proj.shape[0]
    tk, tq = SB_TK, SB_KT * SB_TK
    n_pairs = 4
    scale = 1.0 / math.sqrt(SB_HEAD_DIM)

    def body(q_ref, k_ref, v_ref, o_ref, do_ref, *rest):
        dq_ref, dk_ref, dv_ref = rest[len(after):]
        i = pl.program_id(1)

        @pl.when(i == 0)
        def _():
            dk_ref[...] = jnp.zeros_like(dk_ref)
            dv_ref[...] = jnp.zeros_like(dv_ref)

        diff, later, later_incl = _sb_consts(tq)
        qs = (q_ref[...].astype(F32) * scale).astype(BF)
        do2 = do_ref[...]
        prod = do2.astype(F32) * o_ref[...]
        lane_masks = _head_lane_masks(tk)
        first_head = lax.broadcasted_iota(jnp.int32, (tq, LANES), 1) < SB_HEAD_DIM
        totals = [jnp.broadcast_to(jnp.sum(jnp.where(keep, prod, 0.0), axis=-1, keepdims=True), (tq, tk))
                  for keep in (first_head, jnp.logical_not(first_head))]
        first_head_k = first_head[:tk]

        def step(g_idx, state, masked):
            tiles = list(reversed(range(SB_KT)))
            chains = [(t, h) for t in tiles for h in range(2)]
            rows = {t: pl.ds(pl.multiple_of((g_idx * SB_KT + t) * tk, tk), tk) for t in tiles}
            ks = {t: _both_heads(k_ref[rows[t], :], lane_masks) for t in tiles}
            vs = {t: _both_heads(v_ref[rows[t], :], lane_masks) for t in tiles}
            r0 = {t: t * tk if masked else 0 for t in tiles}
            allowed = {t: _tail(diff, r0[t]) < -t * tk for t in tiles}
            zs = {t: lax.dot_general(_tail(qs, r0[t]), ks[t], NT, preferred_element_type=F32) for t in tiles}
            dws = {t: lax.dot_general(_tail(do2, r0[t]), vs[t], NT, preferred_element_type=F32) for t in tiles}
            logs = {}
            for t, h in chains:
                log_b, sp = _log_terms(zs[t][:, h * tk:(h + 1) * tk])
                logs[t, h] = (log_b, jnp.where(allowed[t], sp, 0.0) if masked else sp)
            sums = {c: _split_dot(logs[c][1], later) for c in chains}
            c_log, c_g = list(state[0]), list(state[1])
            ws, gs = {}, {}
            for t, h in chains:
                w = jnp.exp(logs[t, h][0] - (sums[t, h][:, :tk] + _tail(c_log[h], r0[t])))
                ws[t, h] = (jnp.where(allowed[t], w, 0.0) if masked else w).astype(BF)
                c_log[h] = _add_tail(c_log[h], r0[t], sums[t, h][:, tk:])
                gs[t, h] = ws[t, h].astype(F32) * dws[t][:, h * tk:(h + 1) * tk]
            gsums = {c: _split_dot(gs[c], later_incl) for c in chains}
            dzs = {}
            for t, h in chains:
                beta = jnp.exp(logs[t, h][0])
                earlier = _tail(totals[h], r0[t]) - (gsums[t, h][:, :tk] + _tail(c_g[h], r0[t]))
                dz = gs[t, h] * (1.0 - beta) - earlier * beta
                dzs[t, h] = (jnp.where(allowed[t], dz, 0.0) if masked else dz).astype(BF)
                c_g[h] = _add_tail(c_g[h], r0[t], gsums[t, h][:, tk:])
            dq = state[2]
            for t in tiles:
                dz_both = jnp.concatenate([dzs[t, 0], dzs[t, 1]], axis=1)
                w_both = jnp.concatenate([ws[t, 0], ws[t, 1]], axis=1)
                dq = _add_tail(dq, r0[t], jnp.dot(dz_both, ks[t], preferred_element_type=F32))
                dk2 = lax.dot_general(dz_both, _tail(qs, r0[t]), TN, preferred_element_type=F32)
                dv2 = lax.dot_general(w_both, _tail(do2, r0[t]), TN, preferred_element_type=F32)
                dk_ref[rows[t], :] += jnp.where(first_head_k, dk2[:tk], dk2[tk:])
                dv_ref[rows[t], :] += jnp.where(first_head_k, dv2[:tk], dv2[tk:])
            return tuple(c_log), tuple(c_g), dq

        zero = jnp.zeros((tq, LANES), F32)
        state = step(i, ((zero, zero), (zero, zero), zero), True)
        state = lax.fori_loop(0, i, lambda gg, st: step(i - 1 - gg, st, False), state)
        dq_ref[...] = (state[2] * scale).astype(dq_ref.dtype)

    width = n_pairs * LANES
    return pl.pallas_call(
        body, name="sb_attn_bwd", grid=(n_pairs, s // tq),
        in_specs=[pl.BlockSpec((tq, LANES), lambda p, i: (i, p)),
                  pl.BlockSpec((s, LANES), lambda p, i: (0, n_pairs + p)),
                  pl.BlockSpec((s, LANES), lambda p, i: (0, 2 * n_pairs + p)),
                  pl.BlockSpec((tq, LANES), lambda p, i: (i, p)),
                  pl.BlockSpec((tq, LANES), lambda p, i: (i, p))] + [HBM_SPEC] * len(after),
        out_specs=[pl.BlockSpec((tq, LANES), lambda p, i: (i, p)),
                   pl.BlockSpec((s, LANES), lambda p, i: (0, p)),
                   pl.BlockSpec((s, LANES), lambda p, i: (0, p))],
        out_shape=[jax.ShapeDtypeStruct((s, width), BF), jax.ShapeDtypeStruct((s, width), F32),
                   jax.ShapeDtypeStruct((s, width), F32)],
        compiler_params=_params(("parallel", "arbitrary")),
    )(proj, proj, proj, o32, do_a, *after)


CONV_COL0 = 12


def _shift_rows(v, k):
    n = v.shape[0]
    row = lax.broadcasted_iota(jnp.int32, v.shape, 0)
    rolled = pltpu.roll(v, k % n, axis=0)
    keep = row >= k if k > 0 else row < n + k
    return jnp.where(keep, rolled, 0.0)


def _conv_specs(s):
    return [pl.BlockSpec((s, LANES), lambda cb: (0, CONV_COL0 + cb)),
            pl.BlockSpec((s, LANES), lambda cb: (0, CONV_COL0 + 4 + cb)),
            pl.BlockSpec((s, LANES), lambda cb: (0, CONV_COL0 + 8 + cb)),
            pl.BlockSpec((CONV_ROWS, LANES), lambda cb: (0, cb))]


def _conv_fwd(proj, conv_w):
    s = proj.shape[0]

    def body(u_ref, gb_ref, gc_ref, w_ref, y_ref):
        cu = gc_ref[...].astype(F32) * u_ref[...].astype(F32)
        w = w_ref[...]
        y = w[0:1] * _shift_rows(cu, 2) + w[1:2] * _shift_rows(cu, 1) + w[2:3] * cu
        y_ref[...] = (gb_ref[...].astype(F32) * y).astype(y_ref.dtype)

    return pl.pallas_call(
        body, name="conv_fwd", grid=(4,), in_specs=_conv_specs(s),
        out_specs=pl.BlockSpec((s, LANES), lambda cb: (0, cb)),
        out_shape=jax.ShapeDtypeStruct((s, 4 * LANES), BF),
        compiler_params=_params(("parallel",)),
    )(proj, proj, proj, conv_w)


def _conv_bwd(proj, conv_w, dy):
    s = proj.shape[0]

    def body(u_ref, gb_ref, gc_ref, w_ref, dy_ref, du_ref, dgb_ref, dgc_ref, dw_ref):
        u, gc = u_ref[...].astype(F32), gc_ref[...].astype(F32)
        dyv = dy_ref[...].astype(F32)
        w = w_ref[...]
        cu = gc * u
        cu1, cu2 = _shift_rows(cu, 1), _shift_rows(cu, 2)
        conv = w[0:1] * cu2 + w[1:2] * cu1 + w[2:3] * cu
        dgb_ref[...] = (dyv * conv).astype(dgb_ref.dtype)
        dc = dyv * gb_ref[...].astype(F32)
        dcu = w[2:3] * dc + w[1:2] * _shift_rows(dc, -1) + w[0:1] * _shift_rows(dc, -2)
        dgc_ref[...] = (dcu * u).astype(dgc_ref.dtype)
        du_ref[...] = (dcu * gc).astype(du_ref.dtype)
        tap_row = lax.broadcasted_iota(jnp.int32, (CONV_ROWS, LANES), 0)
        dw = jnp.zeros((CONV_ROWS, LANES), F32)
        for t, shifted in enumerate((cu2, cu1, cu)):
            dw = jnp.where(tap_row == t, jnp.sum(dc * shifted, axis=0, keepdims=True), dw)
        dw_ref[...] = dw

    col = pl.BlockSpec((s, LANES), lambda cb: (0, cb))
    act = jax.ShapeDtypeStruct((s, 4 * LANES), BF)
    return pl.pallas_call(
        body, name="conv_bwd", grid=(4,), in_specs=_conv_specs(s) + [col],
        out_specs=[col, col, col, pl.BlockSpec((CONV_ROWS, LANES), lambda cb: (0, cb))],
        out_shape=[act, act, act, jax.ShapeDtypeStruct((CONV_ROWS, 4 * LANES), F32)],
        compiler_params=_params(("parallel",)),
    )(proj, proj, proj, conv_w, dy)


def _mem_probs(q, k, scale):
    sc = lax.dot_general(q, k, NT, preferred_element_type=F32) * scale
    p = jnp.exp(sc - jnp.max(sc, axis=-1, keepdims=True))
    return p / jnp.sum(p, axis=-1, keepdims=True)


def _mem_fwd(q_m, kv, tq=2048):
    s, d = q_m.shape
    mlen = kv.shape[0]
    hd = d // MEM_HEADS
    tq = _tile(s, tq)
    scale = 1.0 / math.sqrt(hd)

    def body(q_ref, k_ref, v_ref, o_ref):
        p = _mem_probs(q_ref[...], k_ref[...], scale)
        o_ref[...] = jnp.dot(p.astype(BF), v_ref[...], preferred_element_type=F32).astype(o_ref.dtype)

    return pl.pallas_call(
        body, name="mem_attn_fwd", grid=(MEM_HEADS, s // tq),
        in_specs=[pl.BlockSpec((tq, hd), lambda h, i: (i, h)),
                  pl.BlockSpec((mlen, hd), lambda h, i: (0, h)),
                  pl.BlockSpec((mlen, hd), lambda h, i: (0, MEM_HEADS + h))],
        out_specs=pl.BlockSpec((tq, hd), lambda h, i: (i, h)),
        out_shape=jax.ShapeDtypeStruct((s, d), BF),
        compiler_params=_params(("parallel", "parallel")),
    )(q_m, kv, kv)


def _mem_bwd(q_m, kv, do_m, tq=2048):
    s, d = q_m.shape
    mlen = kv.shape[0]
    hd = d // MEM_HEADS
    tq = _tile(s, tq)
    scale = 1.0 / math.sqrt(hd)

    def body(q_ref, k_ref, v_ref, do_ref, dq_ref, dk_ref, dv_ref):
        q, k, v, do = q_ref[...], k_ref[...], v_ref[...], do_ref[...]
        p = _mem_probs(q, k, scale)
        dp = lax.dot_general(do, v, NT, preferred_element_type=F32)
        ds = p * (dp - jnp.sum(dp * p, axis=-1, keepdims=True)) * scale
        dsb = ds.astype(BF)
        dq_ref[...] = jnp.dot(dsb, k, preferred_element_type=F32).astype(dq_ref.dtype)
        dk = lax.dot_general(dsb, q, TN, preferred_element_type=F32)
        dv = lax.dot_general(p.astype(BF), do, TN, preferred_element_type=F32)
        first = pl.program_id(1) == 0

        @pl.when(first)
        def _():
            dk_ref[...] = dk
            dv_ref[...] = dv

        @pl.when(jnp.logical_not(first))
        def _():
            dk_ref[...] += dk
            dv_ref[...] += dv

    tok = pl.BlockSpec((tq, hd), lambda h, i: (i, h))
    memb = pl.BlockSpec((mlen, hd), lambda h, i: (0, h))
    return pl.pallas_call(
        body, name="mem_attn_bwd", grid=(MEM_HEADS, s // tq),
        in_specs=[tok, memb, pl.BlockSpec((mlen, hd), lambda h, i: (0, MEM_HEADS + h)), tok],
        out_specs=[tok, memb, memb],
        out_shape=[jax.ShapeDtypeStruct((s, d), BF), jax.ShapeDtypeStruct((mlen, d), F32),
                   jax.ShapeDtypeStruct((mlen, d), F32)],
        compiler_params=_params(("parallel", "arbitrary")),
    )(q_m, kv, kv, do_m)


def _place():
    x, y, c = lax.axis_index("x"), lax.axis_index("y"), lax.axis_index("c")
    other_chips = [(1 - x, y), (x, 1 - y), (1 - x, 1 - y)]
    return x, y, c, other_chips


def _chip_no(cx, cy):
    return 2 * cx + cy


HBM_SPEC = pl.BlockSpec(memory_space=pl.ANY)


def _cast_place(shard, axis, place, dtype, name, after=()):
    r, c = shard.shape
    tr = _tile(r, max(16, 1048576 // c))
    nblk = r // tr
    if axis == 1:
        full, out_map = (r, N_CHIPS * c), lambda i, pref: (i, pref[0])
    else:
        full, out_map = (N_CHIPS * r, c), lambda i, pref: (pref[0] * nblk + i, 0)

    def body(pref, s_ref, *rest):
        o_ref = rest[-1]
        o_ref[...] = s_ref[...].astype(o_ref.dtype)

    return pl.pallas_call(
        body, name=name,
        grid_spec=pltpu.PrefetchScalarGridSpec(
            num_scalar_prefetch=1, grid=(nblk,),
            in_specs=[pl.BlockSpec((tr, c), lambda i, pref: (i, 0))] + [HBM_SPEC] * len(after),
            out_specs=pl.BlockSpec((tr, c), out_map)),
        out_shape=jax.ShapeDtypeStruct(full, dtype),
        compiler_params=_params(("parallel",)),
    )(place, shard, *after)


def _region(ref, axis, chip_no, half):
    width = ref.shape[axis] // N_CHIPS
    start = pl.multiple_of(chip_no * width, width)
    if axis == 1:
        if half is None:
            return ref.at[:, pl.ds(start, width)]
        hr = ref.shape[0] // 2
        return ref.at[pl.ds(pl.multiple_of(half * hr, hr), hr), pl.ds(start, width)]
    if half is None:
        return ref.at[pl.ds(start, width), :]
    hr = width // 2
    return ref.at[pl.ds(pl.multiple_of(start + half * hr, hr), hr), :]


def _gather_weights(fulls, axes, split):
    n = len(fulls)

    def body(*refs):
        outs = refs[n:2 * n]
        send, recv, fsend, frecv = refs[2 * n:]
        x, y, c, others = _place()
        me = _chip_no(x, y)
        sibling = (x, y, 1 - c)

        def copy(w, chip_no, half, sems, p, to):
            reg = _region(outs[w], axes[w], chip_no, half)
            return pltpu.make_async_remote_copy(
                src_ref=reg, dst_ref=reg, send_sem=sems[0].at[w, p], recv_sem=sems[1].at[w, p],
                device_id=to, device_id_type=MESH)

        for w in range(n):
            for p, chip in enumerate(others):
                copy(w, me, c if split[w] else None, (send, recv), p, (chip[0], chip[1], c)).start()
        for w in range(n):
            for p, chip in enumerate(others):
                half = c if split[w] else None
                copy(w, _chip_no(*chip), half, (send, recv), p, (chip[0], chip[1], c)).wait_recv()
                if split[w]:
                    copy(w, _chip_no(*chip), c, (fsend, frecv), p, sibling).start()
        for w in range(n):
            for p, chip in enumerate(others):
                copy(w, me, c if split[w] else None, (send, recv), p, (chip[0], chip[1], c)).wait_send()
                if split[w]:
                    handed = copy(w, _chip_no(*chip), 1 - c, (fsend, frecv), p, sibling)
                    handed.wait_recv()
                    handed.wait_send()

    return pl.pallas_call(
        body, name="gather_weights",
        in_specs=[HBM_SPEC] * n, out_specs=[HBM_SPEC] * n,
        out_shape=[jax.ShapeDtypeStruct(f.shape, f.dtype) for f in fulls],
        input_output_aliases={i: i for i in range(n)},
        scratch_shapes=[pltpu.SemaphoreType.DMA((n, 3))] * 4,
        compiler_params=pltpu.CompilerParams(has_side_effects=True),
    )(*fulls)


SEM_SPEC = pl.BlockSpec(memory_space=pltpu.SEMAPHORE)
IN_HBM = pl.BlockSpec(memory_space=pltpu.HBM)
FLOWS = pltpu.CompilerParams(has_side_effects=pltpu.SideEffectType.DATAFLOW_SIDE_EFFECTING)
TOKEN = jax.ShapeDtypeStruct((8, LANES), F32)


def _in_hbm(arrays):
    return [pltpu.with_memory_space_constraint(a, pltpu.HBM) for a in arrays]


def _hbm_like(arrays):
    return [pltpu.HBM(a.shape, a.dtype) for a in arrays]


def _fetch_copy(refs, axes, whole, send, recv, w, p, chip, c, arriving):
    owner = _chip_no(*chip) if arriving else _chip_no(lax.axis_index("x"), lax.axis_index("y"))
    reg = _region(refs[w], axes[w], owner, None if whole[w] else c)
    return pltpu.make_async_remote_copy(
        src_ref=reg, dst_ref=reg, send_sem=send[p], recv_sem=recv[p],
        device_id=(chip[0], chip[1], c), device_id_type=MESH)


N_PEERS = N_CHIPS - 1
PEER_SEMS = [pltpu.SemaphoreType.DMA(())] * (2 * N_PEERS)


def _fetch_start(fulls, axes, whole, after, name):
    n = len(fulls)

    def body(*refs):
        ins = refs[:n]
        sems = refs[n + len(after):]
        send, recv = sems[:N_PEERS], sems[N_PEERS:2 * N_PEERS]
        token = refs[-1]
        _, _, c, others = _place()
        for w in range(n):
            for p, chip in enumerate(others):
                _fetch_copy(ins, axes, whole, send, recv, w, p, chip, c, False).start()
        token[...] = jnp.zeros_like(token)

    res = pl.pallas_call(
        body, name=name,
        in_specs=[IN_HBM] * n + [HBM_SPEC] * len(after),
        out_specs=[SEM_SPEC] * len(PEER_SEMS) + [IN_HBM] * n + [pl.BlockSpec(memory_space=pltpu.VMEM)],
        out_shape=PEER_SEMS + _hbm_like(fulls) + [TOKEN],
        input_output_aliases={i: len(PEER_SEMS) + i for i in range(n)},
        compiler_params=FLOWS,
    )(*_in_hbm(fulls), *after)
    k = len(PEER_SEMS)
    return list(res[:k]), list(res[k:k + n]), res[-1]


def _fetch_wait(sems, fulls, axes, whole, after, name):
    n = len(fulls)

    def body(*refs):
        ins = refs[:n]
        send, recv = refs[n:n + N_PEERS], refs[n + N_PEERS:n + 2 * N_PEERS]
        _, _, c, others = _place()
        for w in range(n):
            for p, chip in enumerate(others):
                _fetch_copy(ins, axes, whole, send, recv, w, p, chip, c, False).wait_send()
                _fetch_copy(ins, axes, whole, send, recv, w, p, chip, c, True).wait_recv()

    res = pl.pallas_call(
        body, name=name,
        in_specs=[IN_HBM] * n + [SEM_SPEC] * len(sems) + [HBM_SPEC] * len(after),
        out_specs=[IN_HBM] * n,
        out_shape=_hbm_like(fulls),
        input_output_aliases={i: i for i in range(n)},
        compiler_params=FLOWS,
    )(*fulls, *sems, *after)
    return list(res)


def _hand_on(fulls, axes, name):
    n = len(fulls)

    def body(*refs):
        outs = refs[n:2 * n]
        send, recv = refs[2 * n:]
        x, y, c, others = _place()

        def copy(w, p, chip, half):
            reg = _region(outs[w], axes[w], _chip_no(*chip), half)
            return pltpu.make_async_remote_copy(
                src_ref=reg, dst_ref=reg, send_sem=send.at[w, p], recv_sem=recv.at[w, p],
                device_id=(x, y, 1 - c), device_id_type=MESH)

        for w in range(n):
            for p, chip in enumerate(others):
                copy(w, p, chip, c).start()
        for w in range(n):
            for p, chip in enumerate(others):
                copy(w, p, chip, 1 - c).wait()

    return pl.pallas_call(
        body, name=name,
        in_specs=[HBM_SPEC] * n, out_specs=[HBM_SPEC] * n,
        out_shape=[jax.ShapeDtypeStruct(f.shape, f.dtype) for f in fulls],
        input_output_aliases={i: i for i in range(n)},
        scratch_shapes=[pltpu.SemaphoreType.DMA((n, 3)), pltpu.SemaphoreType.DMA((n, 3))],
        compiler_params=pltpu.CompilerParams(has_side_effects=True),
    )(*fulls)


def _pair_exchange(grads, name):
    n = len(grads)

    def body(*refs):
        ins, outs = refs[:n], refs[n:2 * n]
        send, recv = refs[2 * n:]
        x, y, c, _ = _place()
        cps = []
        for w in range(n):
            cp = pltpu.make_async_remote_copy(
                src_ref=ins[w].at[:, 1 - c], dst_ref=outs[w], send_sem=send.at[w], recv_sem=recv.at[w],
                device_id=(x, y, 1 - c), device_id_type=MESH)
            cp.start()
            cps.append(cp)
        for cp in cps:
            cp.wait()

    return pl.pallas_call(
        body, name=name,
        in_specs=[HBM_SPEC] * n, out_specs=[HBM_SPEC] * n,
        out_shape=[jax.ShapeDtypeStruct((g.shape[0],) + g.shape[2:], g.dtype) for g in grads],
        scratch_shapes=[pltpu.SemaphoreType.DMA((n,)), pltpu.SemaphoreType.DMA((n,))],
        compiler_params=pltpu.CompilerParams(has_side_effects=True),
    )(*grads)


def _pair_add(g4, got, core, name):
    nj, _, hr, cdim = g4.shape
    tr = _tile(hr, max(8, 524288 // cdim))

    def body(core_ref, a_ref, b_ref, o_ref):
        o_ref[...] = (a_ref[...].astype(F32) + b_ref[...].astype(F32)).astype(o_ref.dtype)

    return pl.pallas_call(
        body, name=name,
        grid_spec=pltpu.PrefetchScalarGridSpec(
            num_scalar_prefetch=1, grid=(nj, hr // tr),
            in_specs=[pl.BlockSpec((1, None, tr, cdim), lambda j, i, core_ref: (j, core_ref[0], i, 0)),
                      pl.BlockSpec((1, tr, cdim), lambda j, i, core_ref: (j, i, 0))],
            out_specs=pl.BlockSpec((1, tr, cdim), lambda j, i, core_ref: (j, i, 0))),
        out_shape=jax.ShapeDtypeStruct((nj, hr, cdim), BF),
        compiler_params=_params(("parallel", "parallel")),
    )(core, g4, got)


def _piece(ref, axis, j, hc):
    if axis == 0:
        return ref.at[j]
    return ref.at[0, :, pl.ds(pl.multiple_of(j * hc, hc), hc)]


def _slot_shapes(sums, axes):
    return [(N_CHIPS - 1, sm.shape[1], sm.shape[2] // (1 if ax == 0 else N_CHIPS)) for sm, ax in zip(sums, axes)]


def _slot_copy(sums, lands, axes, send, recv, w, p, chip, c):
    return pltpu.make_async_remote_copy(
        src_ref=_piece(sums[w], axes[w], _chip_no(*chip), lands[w].shape[2]), dst_ref=lands[w].at[p],
        send_sem=send[p], recv_sem=recv[p],
        device_id=(chip[0], chip[1], c), device_id_type=MESH)


def _chip_exchange_start(sums, axes, name):
    n = len(sums)
    shapes = _slot_shapes(sums, axes)
    lands = [lax.empty(sh, sm.dtype) for sh, sm in zip(shapes, sums)]

    def body(*refs):
        ins, land_refs = refs[:n], refs[n:2 * n]
        send, recv = refs[2 * n:2 * n + N_PEERS], refs[2 * n + N_PEERS:2 * n + 2 * N_PEERS]
        token = refs[-1]
        _, _, c, others = _place()
        for w in range(n):
            for p, chip in enumerate(others):
                _slot_copy(ins, land_refs, axes, send, recv, w, p, chip, c).start()
        token[...] = jnp.zeros_like(token)

    k = len(PEER_SEMS)
    res = pl.pallas_call(
        body, name=name,
        in_specs=[IN_HBM] * (2 * n),
        out_specs=[SEM_SPEC] * k + [IN_HBM] * (2 * n) + [pl.BlockSpec(memory_space=pltpu.VMEM)],
        out_shape=PEER_SEMS + _hbm_like(list(sums) + lands) + [TOKEN],
        input_output_aliases={i: k + i for i in range(2 * n)},
        compiler_params=FLOWS,
    )(*_in_hbm(list(sums) + lands))
    return list(res[:k]), list(res[k:k + n]), list(res[k + n:k + 2 * n]), res[-1]


def _chip_exchange_wait(sems, sums, lands, axes, after, name):
    n = len(sums)

    def body(*refs):
        ins, land_refs = refs[:n], refs[n:2 * n]
        send, recv = refs[2 * n:2 * n + N_PEERS], refs[2 * n + N_PEERS:2 * n + 2 * N_PEERS]
        _, _, c, others = _place()
        for w in range(n):
            for p, chip in enumerate(others):
                cp = _slot_copy(ins, land_refs, axes, send, recv, w, p, chip, c)
                cp.wait_send()
                cp.wait_recv()

    res = pl.pallas_call(
        body, name=name,
        in_specs=[IN_HBM] * (2 * n) + [SEM_SPEC] * len(sems) + [HBM_SPEC] * len(after),
        out_specs=[IN_HBM] * (2 * n), out_shape=_hbm_like(list(sums) + list(lands)),
        input_output_aliases={i: i for i in range(2 * n)},
        compiler_params=FLOWS,
    )(*sums, *lands, *sems, *after)
    return list(res[:n]), list(res[n:])


def _chip_sum(psum, slots, axis, place, name):
    _, hr, hc = slots.shape
    tr = _tile(hr, 256)
    own_map = (lambda i, pref: (0, i, pref[0])) if axis == 1 else (lambda i, pref: (pref[0], i, 0))

    def body(pref, own_ref, s_ref, o_ref):
        o_ref[...] = ((own_ref[...].astype(F32) + s_ref[0].astype(F32)) + s_ref[1].astype(F32)) + s_ref[2].astype(F32)

    return pl.pallas_call(
        body, name=name,
        grid_spec=pltpu.PrefetchScalarGridSpec(
            num_scalar_prefetch=1, grid=(hr // tr,),
            in_specs=[pl.BlockSpec((None, tr, hc), own_map),
                      pl.BlockSpec((N_CHIPS - 1, tr, hc), lambda i, pref: (0, i, 0))],
            out_specs=pl.BlockSpec((None, tr, hc), lambda i, pref: (pref[1], i, 0))),
        out_shape=jax.ShapeDtypeStruct((2, hr, hc), F32),
        compiler_params=_params(("parallel",)),
    )(place, psum, slots)


def _half_swap(both):
    n = len(both)

    def body(*refs):
        outs = refs[n:2 * n]
        send, recv = refs[2 * n:]
        x, y, c, _ = _place()

        def copy(w, half):
            return pltpu.make_async_remote_copy(
                src_ref=outs[w].at[half], dst_ref=outs[w].at[half], send_sem=send.at[w], recv_sem=recv.at[w],
                device_id=(x, y, 1 - c), device_id_type=MESH)

        for w in range(n):
            copy(w, c).start()
        for w in range(n):
            copy(w, 1 - c).wait()

    return pl.pallas_call(
        body, name="grad_half_swap",
        in_specs=[HBM_SPEC] * n, out_specs=[HBM_SPEC] * n,
        out_shape=[jax.ShapeDtypeStruct(b.shape, b.dtype) for b in both],
        input_output_aliases={i: i for i in range(n)},
        scratch_shapes=[pltpu.SemaphoreType.DMA((n,)), pltpu.SemaphoreType.DMA((n,))],
        compiler_params=pltpu.CompilerParams(has_side_effects=True),
    )(*both)


def _allreduce_small(pack):
    rows, d = pack.shape

    def body(p_ref, o_ref, slots, send, recv):
        x, y, c, _ = _place()
        me = 4 * x + 2 * y + c
        slots[me] = p_ref[...]
        cps = []
        for k in range(1, N_DEV):
            px, py, pc = x ^ (k >> 2), y ^ ((k >> 1) & 1), c ^ (k & 1)
            cp = pltpu.make_async_remote_copy(
                src_ref=p_ref, dst_ref=slots.at[me], send_sem=send.at[k - 1], recv_sem=recv.at[k - 1],
                device_id=(px, py, pc), device_id_type=MESH)
            cp.start()
            cps.append(cp)
        for k in range(1, N_DEV):
            px, py, pc = x ^ (k >> 2), y ^ ((k >> 1) & 1), c ^ (k & 1)
            arrival = pltpu.make_async_remote_copy(
                src_ref=p_ref, dst_ref=slots.at[4 * px + 2 * py + pc], send_sem=send.at[k - 1],
                recv_sem=recv.at[k - 1], device_id=(px, py, pc), device_id_type=MESH)
            arrival.wait_recv()
            arrival.wait_send()
        acc = slots[0]
        for k in range(1, N_DEV):
            acc = acc + slots[k]
        o_ref[...] = acc

    vm = pl.BlockSpec(memory_space=pltpu.VMEM)
    return pl.pallas_call(
        body, name="allreduce_small", in_specs=[vm], out_specs=vm,
        out_shape=jax.ShapeDtypeStruct((rows, d), F32),
        scratch_shapes=[pltpu.VMEM((N_DEV, rows, d), F32), pltpu.SemaphoreType.DMA((N_DEV - 1,)),
                        pltpu.SemaphoreType.DMA((N_DEV - 1,))],
        compiler_params=pltpu.CompilerParams(has_side_effects=True),
    )(pack)


def _adamw(w, g, m, v, name):
    rows, cols = w.shape

    def fn(wv, gv, mv, vv):
        m2 = ADAM_B1 * mv + (1.0 - ADAM_B1) * gv
        v2 = ADAM_B2 * vv + (1.0 - ADAM_B2) * (gv * gv)
        m_hat = m2 / (1.0 - ADAM_B1 ** ADAM_STEP)
        v_hat = v2 / (1.0 - ADAM_B2 ** ADAM_STEP)
        delta = -ADAM_LR * (m_hat / (jnp.sqrt(v_hat) + ADAM_EPS) + ADAM_WD * wv)
        return delta, m2, v2

    ins = [(a, cols, 0) for a in (w, g, m, v)]
    return _rowwise(fn, ins, [(cols, F32)] * 3, rows=rows, tm=_tile(rows, max(8, 262144 // cols)), name=name)


BIG = ["w_in", "w_branch_a", "w_branch_b", "w_mix_out", "w_mem_q", "w_mem_kv", "w_mem_o", "w_ffn_in", "w_ffn_out"]
BIG_AXIS = {"w_in": 1, "w_branch_a": 1, "w_branch_b": 1, "w_mix_out": 0, "w_mem_q": 0, "w_mem_kv": 1,
            "w_mem_o": 0, "w_ffn_in": 1, "w_ffn_out": 0}
NORMS = ["norm_mix", "norm_mem_q", "norm_mem_kv", "norm_ffn", "norm_final"]
ORDER = ["norm_mix", "w_in", "conv_w", "w_branch_a", "w_branch_b", "w_mix_out", "norm_mem_q", "norm_mem_kv",
         "w_mem_q", "w_mem_kv", "w_mem_o", "norm_ffn", "w_ffn_in", "w_ffn_out", "norm_final"]


def _pack_small(vals, conv):
    d = vals[0].shape[-1]
    rows = [v.reshape(1, d) for v in vals]
    conv = jnp.pad(conv, ((0, 0), (0, d - conv.shape[1])))
    pad = jnp.zeros((SMALL_ROWS - len(rows) - CONV_K, d), F32)
    return jnp.concatenate(rows + [conv, pad], axis=0)


def kernel(x, mem, norm_mix, w_in, conv_w, w_branch_a, w_branch_b, w_mix_out, norm_mem_q, norm_mem_kv, w_mem_q, w_mem_kv, w_mem_o, norm_ffn, w_ffn_in, w_ffn_out, norm_final, loss_target, m_norm_mix, m_w_in, m_conv_w, m_w_branch_a, m_w_branch_b, m_w_mix_out, m_norm_mem_q, m_norm_mem_kv, m_w_mem_q, m_w_mem_kv, m_w_mem_o, m_norm_ffn, m_w_ffn_in, m_w_ffn_out, m_norm_final, v_norm_mix, v_w_in, v_conv_w, v_w_branch_a, v_w_branch_b, v_w_mix_out, v_norm_mem_q, v_norm_mem_kv, v_w_mem_q, v_w_mem_kv, v_w_mem_o, v_norm_ffn, v_w_ffn_in, v_w_ffn_out, v_norm_final):
    args = dict(locals())
    wts = {n: args[n] for n in ORDER}
    mom = {n: args["m_" + n] for n in ORDER}
    var = {n: args["v_" + n] for n in ORDER}
    x = x[0]
    mem = mem[0]
    target = loss_target[0]
    s, d = x.shape
    gains = {n: wts[n].reshape(1, d) for n in NORMS}
    chip = 2 * lax.axis_index("x") + lax.axis_index("y")
    core = lax.axis_index("c").astype(jnp.int32).reshape(1)
    place = jnp.stack([chip, lax.axis_index("c")]).astype(jnp.int32)

    conv_shard = jnp.pad(conv_w[0], ((0, CONV_ROWS - CONV_K), (0, 0)))
    first_axes, first_whole = [BIG_AXIS["w_in"], 1], [False, True]
    first = [_cast_place(wts["w_in"][0], BIG_AXIS["w_in"], place, BF, "place_w_in"),
             _cast_place(conv_shard, 1, place, F32, "place_conv_w")]
    a_sems, first, a_token = _fetch_start(first, first_axes, first_whole, (), "fetch_start_first")
    later_w = [n for n in BIG if n != "w_in"]
    later_axes = [BIG_AXIS[n] for n in later_w]
    placed = [_cast_place(wts[n][0], BIG_AXIS[n], place, BF, "place_" + n, after=(a_token,)) for n in later_w]
    h1 = _rmsnorm(x, gains["norm_mix"], "norm_mix_fwd", after=(a_token,))
    first = _fetch_wait(a_sems, first, first_axes, first_whole, [h1] + placed, "fetch_wait_first")
    conv_full = first[1]
    W = {"w_in": _hand_on(first[:1], first_axes[:1], "gather_hand_on_first")[0]}
    f_sems, later_bufs, f_token = _fetch_start(placed, later_axes, [False] * len(placed), (W["w_in"],), "fetch_start")

    proj = _matmul(h1, W["w_in"], tn=1280, after=(f_token,), name="in_proj")
    o_a, o_a32 = _sb_fwd(proj)
    y_b = _conv_fwd(proj, conv_full)
    later_bufs = _fetch_wait(f_sems, later_bufs, later_axes, [False] * len(placed), (o_a, y_b), "fetch_wait")
    W.update(zip(later_w, _hand_on(later_bufs, later_axes, "gather_hand_on")))
    br_a, br_b, merged = _branches_merge(o_a, y_b, W["w_branch_a"], W["w_branch_b"], proj)
    x1 = _matmul(merged, W["w_mix_out"], tn=1024, out_dtype=F32, resid=x, name="mix_out")

    hq = _rmsnorm(x1, gains["norm_mem_q"], "norm_mem_q_fwd")
    mn = _rmsnorm(mem, gains["norm_mem_kv"], "norm_mem_kv_fwd")
    q_m = _matmul(hq, W["w_mem_q"], tn=1024, name="mem_q")
    kv = _matmul(mn, W["w_mem_kv"], tn=1024, name="mem_kv")
    o_m = _mem_fwd(q_m, kv)
    x2 = _matmul(o_m, W["w_mem_o"], tn=1024, out_dtype=F32, resid=x1, name="mem_o")

    hf = _rmsnorm(x2, gains["norm_ffn"], "norm_ffn_fwd")
    gate, up, act = _ffn_in_swiglu(hf, W["w_ffn_in"])
    x3 = _matmul(act, W["w_ffn_out"], tn=1024, out_dtype=F32, resid=x2, name="ffn_out")

    dx3, dx3_b, dg_final, loss_part = _loss_head(x3, gains["norm_final"], target, "loss_head")

    dgu = _d_act_swiglu(dx3_b, W["w_ffn_out"], gate, up)
    gw = {"w_ffn_out": _matmul(act, dx3_b, ta=True, tm=1408, tn=512, name="gw_ffn_out")}
    dx2, dx2_b, dg_ffn = _matmul_norm_bwd(dgu, W["w_ffn_in"], x2, gains["norm_ffn"], dx3, tm=256, bf_copy=True,
                                          name="d_hf_norm_bwd")
    gw["w_ffn_in"] = _matmul(hf, dgu, ta=True, tn=512, name="gw_ffn_in")

    def reduce_start(names, tag):
        views = []
        for n in names:
            r, cdim = gw[n].shape
            views.append(gw[n].reshape(1, 2, r // 2, cdim) if BIG_AXIS[n] == 1
                         else gw[n].reshape(N_CHIPS, 2, r // (2 * N_CHIPS), cdim))
        got = _pair_exchange(views, "grad_pair_exchange_" + tag)
        sums = [_pair_add(v, g, core, "pair_add_" + n) for n, v, g in zip(names, views, got)]
        group_axes = [BIG_AXIS[n] for n in names]
        sems, sums, lands, token = _chip_exchange_start(sums, group_axes, "grad_chip_start_" + tag)
        return (names, group_axes, sems, sums, lands), token

    def reduce_finish(group, after, tag):
        names, group_axes, sems, sums, lands = group
        sums, slots = _chip_exchange_wait(sems, sums, lands, group_axes, after, "grad_chip_wait_" + tag)
        return [_chip_sum(sm, sl, BIG_AXIS[n], place, "chip_sum_" + n) for n, sm, sl in zip(names, sums, slots)]

    group_ffn, token_ffn = reduce_start(["w_ffn_in", "w_ffn_out"], "ffn")

    do_m = _matmul(dx2_b, W["w_mem_o"], tb=True, tn=1024, after=(token_ffn,), name="d_o_m")
    gw["w_mem_o"] = _matmul(o_m, dx2_b, ta=True, tn=512, name="gw_mem_o")
    dq_m, dk_m, dv_m = _mem_bwd(q_m, kv, do_m)
    dkv = jnp.concatenate([dk_m, dv_m], axis=-1)
    dx1, dx1_b, dg_q = _matmul_norm_bwd(dq_m, W["w_mem_q"], x1, gains["norm_mem_q"], dx2, tm=512, bf_copy=True,
                                        name="d_hq_norm_bwd")
    gw["w_mem_q"] = _matmul(hq, dq_m, ta=True, tn=512, name="gw_mem_q")
    dmn = _matmul(dkv, W["w_mem_kv"], tb=True, tn=1024, out_dtype=F32, name="d_mn")
    gw["w_mem_kv"] = _matmul(mn, dkv, ta=True, tn=1024, name="gw_mem_kv")
    _, dg_kv = _rmsnorm_bwd(mem, gains["norm_mem_kv"], dmn, None, "norm_mem_kv_bwd")

    dbr_a, dbr_b, dga, dgb = _d_merged_gates(dx1_b, W["w_mix_out"], proj, br_a, br_b)
    gw["w_mix_out"] = _matmul(merged, dx1_b, ta=True, tn=512, name="gw_mix_out")
    do_a = _matmul(dbr_a, W["w_branch_a"], tb=True, name="d_o_a")
    gw["w_branch_a"] = _matmul(o_a, dbr_a, ta=True, tn=512, name="gw_branch_a")
    dy_b = _matmul(dbr_b, W["w_branch_b"], tb=True, name="d_y_b")
    gw["w_branch_b"] = _matmul(y_b, dbr_b, ta=True, tn=512, name="gw_branch_b")
    group_mid, token_mid = reduce_start(
        ["w_mem_o", "w_mem_q", "w_mem_kv", "w_mix_out", "w_branch_a", "w_branch_b"], "mid")
    du, dgate_b, dgate_c, dconv = _conv_bwd(proj, conv_full, dy_b)
    dq, dk, dv = _sb_bwd(proj, o_a32, do_a, after=(token_mid,))

    def assemble(*parts):
        return jnp.concatenate([p.astype(BF) for p in parts], axis=-1)

    hw = dq.shape[1]
    dproj = _rowwise(assemble, [(t, hw, 0) for t in (dq, dk, dv, du, dgate_b, dgate_c)] + [(dga, d, 0), (dgb, d, 0)],
                     [(proj.shape[1], BF)], rows=s, tm=256, name="assemble_dproj")[0]
    gw["w_in"] = _matmul(h1, dproj, ta=True, tn=640, name="gw_in")
    group_in, token_in = reduce_start(["w_in"], "in")
    grad_x, dg_mix = _matmul_norm_bwd(dproj, W["w_in"], x, gains["norm_mix"], dx1, tm=256, bf_copy=False,
                                      after=(token_in,), name="d_h1_norm_bwd")

    halves = {}
    for group, tag in ((group_ffn, "ffn"), (group_mid, "mid"), (group_in, "in")):
        halves.update(zip(group[0], reduce_finish(group, (grad_x,), tag)))
    both = _half_swap([halves[n] for n in BIG])
    grads = {n: b.reshape(wts[n].shape[1:]) for n, b in zip(BIG, both)}

    small_g = [dg_mix, dg_q, dg_kv, dg_ffn, dg_final]
    pack = _pack_small(small_g, dconv[:CONV_K])
    pack = pack.at[ROW_LOSS].set(jnp.broadcast_to(loss_part[0, :1], (d,)))
    red = _allreduce_small(pack)
    loss = red[ROW_LOSS, 0]
    cw = conv_w.shape[2]
    conv_g = lax.dynamic_slice(red, (ROW_CONV, chip * cw), (CONV_K, cw))
    small_grad = _pack_small([red[i] for i in range(len(NORMS))], conv_g)
    small = [_pack_small([t[n] for n in NORMS], t["conv_w"][0]) for t in (wts, mom, var)]
    s_delta, s_m, s_v = _adamw(small[0], small_grad, small[1], small[2], "adamw_small")

    out = {"grad": {}, "delta": {}, "new_m": {}, "new_v": {}}
    for n in BIG:
        shp = wts[n].shape
        dl, m2, v2 = _adamw(wts[n][0], grads[n], mom[n][0], var[n][0], "adamw_" + n)
        out["grad"][n] = grads[n].reshape(shp)
        out["delta"][n], out["new_m"][n], out["new_v"][n] = dl.reshape(shp), m2.reshape(shp), v2.reshape(shp)
    for key, blk in (("grad", small_grad), ("delta", s_delta), ("new_m", s_m), ("new_v", s_v)):
        for i, n in enumerate(NORMS):
            out[key][n] = blk[i].reshape(wts[n].shape)
        out[key]["conv_w"] = blk[ROW_CONV:ROW_CONV + CONV_K, :cw].reshape(conv_w.shape)

    return (loss, grad_x[None], *[out["grad"][n] for n in ORDER], *[out["delta"][n] for n in ORDER],
            *[out["new_m"][n] for n in ORDER], *[out["new_v"][n] for n in ORDER])
```

```python
import functools
import math

import jax
import jax.numpy as jnp
from jax import lax
from jax.experimental import pallas as pl
from jax.experimental.pallas import tpu as pltpu

BF = jnp.bfloat16
F32 = jnp.float32
MESH = pl.DeviceIdType.MESH

SB_HEAD_DIM = 64
LANES = 128
MEM_HEADS = 4
CONV_K = 3
CONV_ROWS = 8
EPS = 1e-6
N_CHIPS = 4
N_DEV = 8
VMEM_LIMIT = 56 * 1024 * 1024

ADAM_LR = 0.001
ADAM_B1 = 0.9
ADAM_B2 = 0.999
ADAM_EPS = 1e-08
ADAM_WD = 0.01
ADAM_STEP = 10

SMALL_ROWS = 16
ROW_CONV = 5
ROW_LOSS = 8


def _params(sem=None, **kw):
    return pltpu.CompilerParams(dimension_semantics=sem, vmem_limit_bytes=VMEM_LIMIT, **kw)


def _tile(dim, pref):
    if dim <= pref:
        return dim
    for step in (LANES, 8):
        t = (pref // step) * step
        while t >= step:
            if dim % t == 0:
                return t
            t -= step
    raise ValueError(f"no tile of {dim} under {pref}")


def _matmul(a, b, *, ta=False, tb=False, tm=1024, tn=512, tk=None, out_dtype=BF, resid=None, after=(), name):
    if ta:
        kdim, m = a.shape
    else:
        m, kdim = a.shape
    n = b.shape[0] if tb else b.shape[1]
    tm, tn = _tile(m, tm), _tile(n, tn)
    tk = _tile(kdim, tk or kdim)
    nk = kdim // tk
    a_spec = pl.BlockSpec((tk, tm), lambda i, j, k: (k, i)) if ta else pl.BlockSpec((tm, tk), lambda i, j, k: (i, k))
    b_spec = pl.BlockSpec((tn, tk), lambda i, j, k: (j, k)) if tb else pl.BlockSpec((tk, tn), lambda i, j, k: (k, j))
    o_spec = pl.BlockSpec((tm, tn), lambda i, j, k: (i, j))
    dims = (((0 if ta else 1,), (1 if tb else 0,)), ((), ()))
    has_res = resid is not None

    def body(*refs):
        a_ref, b_ref = refs[0], refs[1]
        o_ref = refs[2 + has_res + len(after)]
        av, bv = a_ref[...], b_ref[...]
        if av.dtype != BF:
            av = av.astype(BF)
        if bv.dtype != BF:
            bv = bv.astype(BF)
        p = lax.dot_general(av, bv, dims, preferred_element_type=F32)

        def finish(acc):
            if has_res:
                acc = refs[2][...] + acc
            o_ref[...] = acc.astype(o_ref.dtype)

        if nk == 1:
            finish(p)
        else:
            acc_ref = refs[-1]
            k = pl.program_id(2)

            @pl.when(k == 0)
            def _():
                acc_ref[...] = p

            @pl.when(k > 0)
            def _():
                acc_ref[...] += p

            @pl.when(k == nk - 1)
            def _():
                finish(acc_ref[...])

    return pl.pallas_call(
        body, name=name, grid=(m // tm, n // tn, nk),
        in_specs=[a_spec, b_spec] + ([o_spec] if has_res else []) + [HBM_SPEC] * len(after),
        out_specs=o_spec, out_shape=jax.ShapeDtypeStruct((m, n), out_dtype),
        scratch_shapes=[pltpu.VMEM((tm, tn), F32)] if nk > 1 else [],
        compiler_params=_params(("parallel", "parallel", "arbitrary")),
    )(*([a, b] + ([resid] if has_res else []) + list(after)))


def _rowwise(fn, ins, outs, *, rows, tm, name, accs=(), after=()):
    tm = _tile(rows, tm)
    in_specs, args = [], []
    for arr, cols, cb in ins:
        if cols is None:
            in_specs.append(pl.BlockSpec(arr.shape, lambda i, nd=arr.ndim: (0,) * nd))
        else:
            in_specs.append(pl.BlockSpec((tm, cols), lambda i, cb=cb: (i, cb)))
        args.append(arr)
    out_specs = [pl.BlockSpec((tm, cols), lambda i: (i, 0)) for cols, _ in outs]
    out_shape = [jax.ShapeDtypeStruct((rows, cols), dt) for cols, dt in outs]
    for r, c in accs:
        out_specs.append(pl.BlockSpec((r, c), lambda i: (0, 0)))
        out_shape.append(jax.ShapeDtypeStruct((r, c), F32))
    n_in, n_out = len(ins), len(outs)
    in_specs += [HBM_SPEC] * len(after)
    args += list(after)

    def body(*refs):
        res = fn(*[r[...] for r in refs[:n_in]])
        if not isinstance(res, (tuple, list)):
            res = (res,)
        refs = refs[n_in + len(after):]
        for o_ref, val in zip(refs[:n_out], res[:n_out]):
            o_ref[...] = val.astype(o_ref.dtype)
        first = pl.program_id(0) == 0
        for a_ref, val in zip(refs[n_out:], res[n_out:]):
            @pl.when(first)
            def _(a_ref=a_ref, val=val):
                a_ref[...] = val

            @pl.when(jnp.logical_not(first))
            def _(a_ref=a_ref, val=val):
                a_ref[...] += val

    res = pl.pallas_call(
        body, name=name, grid=(rows // tm,), in_specs=in_specs, out_specs=out_specs, out_shape=out_shape,
        compiler_params=_params(("arbitrary",) if accs else ("parallel",)),
    )(*args)
    return res


def _rstd(xf):
    return lax.rsqrt(jnp.mean(xf * xf, axis=-1, keepdims=True) + EPS)


def _rmsnorm(x, g, name, after=()):
    rows, d = x.shape
    return _rowwise(lambda xv, gv: xv * _rstd(xv) * gv, [(x, d, 0), (g, None, None)], [(d, BF)],
                    rows=rows, tm=512, name=name, after=after)[0]


def _rmsnorm_bwd(x, g, dy, resid, name, bf_copy=False):
    rows, d = x.shape

    def fn(xv, gv, dyv, *rest):
        dyv = dyv.astype(F32)
        r = _rstd(xv)
        xn = xv * r
        dxn = dyv * gv
        dx = r * (dxn - xn * jnp.mean(dxn * xn, axis=-1, keepdims=True))
        if rest:
            dx = rest[0] + dx
        return (dx,) * (1 + bf_copy) + (jnp.sum(dyv * xn, axis=0, keepdims=True),)

    ins = [(x, d, 0), (g, None, None), (dy, d, 0)] + ([(resid, d, 0)] if resid is not None else [])
    outs = [(d, F32)] + ([(d, BF)] if bf_copy else [])
    return _rowwise(fn, ins, outs, rows=rows, tm=512, name=name, accs=[(1, d)])


def _matmul_norm_bwd(a, w, x, g, resid, *, tm, bf_copy, name, after=()):
    m, kdim = a.shape
    d = w.shape[0]
    tm = _tile(m, tm)
    row = lambda i: (i, 0)
    whole = lambda i: (0, 0)

    def body(a_ref, w_ref, x_ref, g_ref, r_ref, *rest):
        outs = rest[len(after):]
        dy = lax.dot_general(a_ref[...], w_ref[...], NT, preferred_element_type=F32)
        xv = x_ref[...]
        r = _rstd(xv)
        xn = xv * r
        dxn = dy * g_ref[...]
        dx = r_ref[...] + r * (dxn - xn * jnp.mean(dxn * xn, axis=-1, keepdims=True))
        outs[0][...] = dx
        if bf_copy:
            outs[1][...] = dx.astype(BF)
        dg = jnp.sum(dy * xn, axis=0, keepdims=True)
        first = pl.program_id(0) == 0

        @pl.when(first)
        def _():
            outs[-1][...] = dg

        @pl.when(jnp.logical_not(first))
        def _():
            outs[-1][...] += dg

    tok = pl.BlockSpec((tm, d), row)
    out_specs = [tok] + ([tok] if bf_copy else []) + [pl.BlockSpec((1, d), whole)]
    out_shape = ([jax.ShapeDtypeStruct((m, d), F32)] + ([jax.ShapeDtypeStruct((m, d), BF)] if bf_copy else [])
                 + [jax.ShapeDtypeStruct((1, d), F32)])
    return pl.pallas_call(
        body, name=name, grid=(m // tm,),
        in_specs=[pl.BlockSpec((tm, kdim), row), pl.BlockSpec((d, kdim), whole), tok, pl.BlockSpec((1, d), whole), tok]
        + [HBM_SPEC] * len(after),
        out_specs=out_specs, out_shape=out_shape,
        compiler_params=_params(("arbitrary",)),
    )(a, w, x, g, resid, *after)


def _norm_ffn_in_swiglu(x, g, w, *, tm=1024, tn=1408):
    m, kdim = x.shape
    f = w.shape[1] // 2
    tm, tn = _tile(m, tm), _tile(f, tn)
    nj = f // tn

    def body(x_ref, g_ref, wg_ref, wu_ref, h_ref, gate_ref, up_ref, act_ref):
        @pl.when(pl.program_id(1) == 0)
        def _():
            xv = x_ref[...]
            h_ref[...] = (xv * _rstd(xv) * g_ref[...]).astype(h_ref.dtype)

        hv = h_ref[...]
        gate = jnp.dot(hv, wg_ref[...], preferred_element_type=F32)
        up = jnp.dot(hv, wu_ref[...], preferred_element_type=F32)
        gate_ref[...] = gate.astype(gate_ref.dtype)
        up_ref[...] = up.astype(up_ref.dtype)
        act_ref[...] = (gate * jax.nn.sigmoid(gate) * up).astype(act_ref.dtype)

    tile = pl.BlockSpec((tm, tn), lambda i, j: (i, j))
    rows = pl.BlockSpec((tm, kdim), lambda i, j: (i, 0))
    return pl.pallas_call(
        body, name="norm_ffn_in_swiglu", grid=(m // tm, nj),
        in_specs=[rows, pl.BlockSpec((1, kdim), lambda i, j: (0, 0)),
                  pl.BlockSpec((kdim, tn), lambda i, j: (0, j)),
                  pl.BlockSpec((kdim, tn), lambda i, j: (0, nj + j))],
        out_specs=[rows, tile, tile, tile],
        out_shape=[jax.ShapeDtypeStruct((m, kdim), BF)] + [jax.ShapeDtypeStruct((m, f), BF)] * 3,
        compiler_params=_params(("parallel", "arbitrary")),
    )(x, g, w, w)


def _norm_matmul(x, g, w, *, tm=1024, tn, name, after=()):
    m, kdim = x.shape
    n = w.shape[1]
    tm, tn = _tile(m, tm), _tile(n, tn)

    def body(x_ref, g_ref, w_ref, *rest):
        h_ref, o_ref = rest[len(after):]

        @pl.when(pl.program_id(1) == 0)
        def _():
            xv = x_ref[...]
            h_ref[...] = (xv * _rstd(xv) * g_ref[...]).astype(h_ref.dtype)

        o_ref[...] = jnp.dot(h_ref[...], w_ref[...], preferred_element_type=F32).astype(o_ref.dtype)

    rows = pl.BlockSpec((tm, kdim), lambda i, j: (i, 0))
    return pl.pallas_call(
        body, name=name, grid=(m // tm, n // tn),
        in_specs=[rows, pl.BlockSpec((1, kdim), lambda i, j: (0, 0)), pl.BlockSpec((kdim, tn), lambda i, j: (0, j))]
        + [HBM_SPEC] * len(after),
        out_specs=[rows, pl.BlockSpec((tm, tn), lambda i, j: (i, j))],
        out_shape=[jax.ShapeDtypeStruct((m, kdim), BF), jax.ShapeDtypeStruct((m, n), BF)],
        compiler_params=_params(("parallel", "arbitrary")),
    )(x, g, w, *after)


def _ffn_out_loss(act, w, resid, g, target, *, tm=512):
    m, f = act.shape
    d = w.shape[1]
    tm = _tile(m, tm)
    row = lambda i: (i, 0)
    whole = lambda i: (0, 0)

    def body(a_ref, w_ref, r_ref, g_ref, t_ref, dx_ref, dxb_ref, dg_ref, loss_ref):
        xv = r_ref[...] + jnp.dot(a_ref[...], w_ref[...], preferred_element_type=F32)
        gv = g_ref[...]
        r = _rstd(xv)
        xn = xv * r
        err = xn * gv - t_ref[...]
        loss = 0.5 * jnp.sum(jnp.mean(err * err, axis=-1, keepdims=True), axis=0, keepdims=True)
        dyv = err * (1.0 / d)
        dxn = dyv * gv
        dx = r * (dxn - xn * jnp.mean(dxn * xn, axis=-1, keepdims=True))
        dx_ref[...] = dx
        dxb_ref[...] = dx.astype(dxb_ref.dtype)
        dg = jnp.sum(dyv * xn, axis=0, keepdims=True)
        loss_b = jnp.broadcast_to(loss, (1, LANES))
        first = pl.program_id(0) == 0

        @pl.when(first)
        def _():
            dg_ref[...] = dg
            loss_ref[...] = loss_b

        @pl.when(jnp.logical_not(first))
        def _():
            dg_ref[...] += dg
            loss_ref[...] += loss_b

    tok = pl.BlockSpec((tm, d), row)
    return pl.pallas_call(
        body, name="ffn_out_loss", grid=(m // tm,),
        in_specs=[pl.BlockSpec((tm, f), row), pl.BlockSpec((f, d), whole), tok, pl.BlockSpec((1, d), whole), tok],
        out_specs=[tok, tok, pl.BlockSpec((1, d), whole), pl.BlockSpec((1, LANES), whole)],
        out_shape=[jax.ShapeDtypeStruct((m, d), F32), jax.ShapeDtypeStruct((m, d), BF),
                   jax.ShapeDtypeStruct((1, d), F32), jax.ShapeDtypeStruct((1, LANES), F32)],
        compiler_params=_params(("arbitrary",)),
    )(act, w, resid, g, target)


def _d_act_swiglu(dx, w, gate, up, *, tm=256):
    m, d = dx.shape
    f = w.shape[0]
    tm = _tile(m, tm)
    row = lambda i: (i, 0)

    def body(dx_ref, w_ref, gate_ref, up_ref, o_ref):
        da = lax.dot_general(dx_ref[...], w_ref[...], NT, preferred_element_type=F32)
        gv, uv = gate_ref[...].astype(F32), up_ref[...].astype(F32)
        sg = jax.nn.sigmoid(gv)
        dgate = da * uv * (sg * (1.0 + gv * (1.0 - sg)))
        o_ref[...] = jnp.concatenate([dgate, da * (gv * sg)], axis=-1).astype(o_ref.dtype)

    return pl.pallas_call(
        body, name="d_act_swiglu", grid=(m // tm,),
        in_specs=[pl.BlockSpec((tm, d), row), pl.BlockSpec((f, d), lambda i: (0, 0)),
                  pl.BlockSpec((tm, f), row), pl.BlockSpec((tm, f), row)],
        out_specs=pl.BlockSpec((tm, 2 * f), row), out_shape=jax.ShapeDtypeStruct((m, 2 * f), BF),
        compiler_params=_params(("parallel",)),
    )(dx, w, gate, up)


GATE_A_BLK, GATE_B_BLK = 3, 4


def _branches_merge(o_a, y_b, w_a, w_b, proj, *, tm=1024):
    m, kdim = o_a.shape
    d = w_a.shape[1]
    tm = _tile(m, tm)
    row = lambda i: (i, 0)

    def body(a_ref, b_ref, wa_ref, wb_ref, ga_ref, gb_ref, bra_ref, brb_ref, merged_ref):
        bra = jnp.dot(a_ref[...], wa_ref[...], preferred_element_type=F32)
        brb = jnp.dot(b_ref[...], wb_ref[...], preferred_element_type=F32)
        bra_ref[...] = bra.astype(bra_ref.dtype)
        brb_ref[...] = brb.astype(brb_ref.dtype)
        merged = jax.nn.sigmoid(ga_ref[...].astype(F32)) * bra + jax.nn.sigmoid(gb_ref[...].astype(F32)) * brb
        merged_ref[...] = merged.astype(merged_ref.dtype)

    tok = pl.BlockSpec((tm, d), row)
    return pl.pallas_call(
        body, name="branches_merge", grid=(m // tm,),
        in_specs=[pl.BlockSpec((tm, kdim), row), pl.BlockSpec((tm, kdim), row),
                  pl.BlockSpec((kdim, d), lambda i: (0, 0)), pl.BlockSpec((kdim, d), lambda i: (0, 0)),
                  pl.BlockSpec((tm, d), lambda i: (i, GATE_A_BLK)), pl.BlockSpec((tm, d), lambda i: (i, GATE_B_BLK))],
        out_specs=[tok, tok, tok], out_shape=[jax.ShapeDtypeStruct((m, d), BF)] * 3,
        compiler_params=_params(("parallel",)),
    )(o_a, y_b, w_a, w_b, proj, proj)


def _d_merged_gates(dx, w_mix, proj, br_a, br_b, *, tm=512):
    m, d = dx.shape
    tm = _tile(m, tm)
    row = lambda i: (i, 0)

    def body(dx_ref, w_ref, ga_ref, gb_ref, bra_ref, brb_ref, dbra_ref, dbrb_ref, dga_ref, dgb_ref):
        dm = lax.dot_general(dx_ref[...], w_ref[...], NT, preferred_element_type=F32)
        sa, sb = jax.nn.sigmoid(ga_ref[...].astype(F32)), jax.nn.sigmoid(gb_ref[...].astype(F32))
        dbra_ref[...] = (dm * sa).astype(dbra_ref.dtype)
        dbrb_ref[...] = (dm * sb).astype(dbrb_ref.dtype)
        dga_ref[...] = (dm * bra_ref[...].astype(F32) * (sa * (1.0 - sa))).astype(dga_ref.dtype)
        dgb_ref[...] = (dm * brb_ref[...].astype(F32) * (sb * (1.0 - sb))).astype(dgb_ref.dtype)

    tok = pl.BlockSpec((tm, d), row)
    return pl.pallas_call(
        body, name="d_merged_gates", grid=(m // tm,),
        in_specs=[tok, pl.BlockSpec((d, d), lambda i: (0, 0)),
                  pl.BlockSpec((tm, d), lambda i: (i, GATE_A_BLK)), pl.BlockSpec((tm, d), lambda i: (i, GATE_B_BLK)),
                  tok, tok],
        out_specs=[tok] * 4, out_shape=[jax.ShapeDtypeStruct((m, d), BF)] * 4,
        compiler_params=_params(("parallel",)),
    )(dx, w_mix, proj, proj, br_a, br_b)


def _loss_head(x, g, target, name):
    rows, d = x.shape

    def fn(xv, gv, tv):
        r = _rstd(xv)
        xn = xv * r
        err = xn * gv - tv
        per_tok = jnp.mean(err * err, axis=-1, keepdims=True)
        loss = 0.5 * jnp.sum(per_tok, axis=0, keepdims=True)
        dyv = err * (1.0 / d)
        dxn = dyv * gv
        dx = r * (dxn - xn * jnp.mean(dxn * xn, axis=-1, keepdims=True))
        return dx, dx, jnp.sum(dyv * xn, axis=0, keepdims=True), jnp.broadcast_to(loss, (1, LANES))

    return _rowwise(fn, [(x, d, 0), (g, None, None), (target, d, 0)], [(d, F32), (d, BF)],
                    rows=rows, tm=512, name=name, accs=[(1, d), (1, LANES)])


SB_TK = 128
SB_KT = 4


def _sb_consts(tq):
    tk = SB_TK
    diff = lax.broadcasted_iota(jnp.int32, (tq, tk), 1) - lax.broadcasted_iota(jnp.int32, (tq, tk), 0)
    rj = lax.broadcasted_iota(jnp.int32, (2 * tk, 2 * tk), 0) & (tk - 1)
    cj = lax.broadcasted_iota(jnp.int32, (2 * tk, 2 * tk), 1)
    ones_half = cj >= tk
    later = jnp.where((rj > cj) | ones_half, 1.0, 0.0).astype(BF)
    later_incl = jnp.where((rj >= cj) | ones_half, 1.0, 0.0).astype(BF)
    return diff, later, later_incl


def _split_dot(val, rhs_twice):
    hi = val.astype(BF)
    lo = (val - hi.astype(F32)).astype(BF)
    return jnp.dot(jnp.concatenate([hi, lo], axis=1), rhs_twice, preferred_element_type=F32)


def _log_terms(z):
    sp = jnp.maximum(z, 0.0) + jnp.log(1.0 + jnp.exp(-jnp.abs(z)))
    return z - sp, sp


NT = (((1,), (1,)), ((), ()))
TN = (((0,), (0,)), ((), ()))


def _head_lane_masks(rows):
    lane = lax.broadcasted_iota(jnp.int32, (rows, LANES), 1)
    first = jnp.where(lane < SB_HEAD_DIM, 1.0, 0.0)
    return first.astype(BF), (1.0 - first).astype(BF)


def _tail(a, r0):
    return a if r0 == 0 else a[r0:]


def _add_tail(a, r0, delta):
    return a + delta if r0 == 0 else jnp.concatenate([a[:r0], a[r0:] + delta], axis=0)


def _both_heads(tile, masks):
    return jnp.concatenate([tile * masks[0], tile * masks[1]], axis=0)


def _sb_fwd(proj):
    s = proj.shape[0]
    tk, tq = SB_TK, SB_KT * SB_TK
    n_pairs = 4
    scale = 1.0 / math.sqrt(SB_HEAD_DIM)

    def body(q_ref, k_ref, v_ref, o_ref, o32_ref):
        i = pl.program_id(1)
        diff, later, _ = _sb_consts(tq)
        qs = (q_ref[...].astype(F32) * scale).astype(BF)
        lane_masks = _head_lane_masks(tk)

        def step(g, state, masked):
            tiles = list(reversed(range(SB_KT)))
            chains = [(t, h) for t in tiles for h in range(2)]
            rows = {t: pl.ds(pl.multiple_of((g * SB_KT + t) * tk, tk), tk) for t in tiles}
            ks = {t: _both_heads(k_ref[rows[t], :], lane_masks) for t in tiles}
            vs = {t: _both_heads(v_ref[rows[t], :], lane_masks) for t in tiles}
            r0 = {t: t * tk if masked else 0 for t in tiles}
            allowed = {t: _tail(diff, r0[t]) < -t * tk for t in tiles}
            zs = {t: lax.dot_general(_tail(qs, r0[t]), ks[t], NT, preferred_element_type=F32) for t in tiles}
            logs = {}
            for t, h in chains:
                log_b, sp = _log_terms(zs[t][:, h * tk:(h + 1) * tk])
                logs[t, h] = (log_b, jnp.where(allowed[t], sp, 0.0) if masked else sp)
            sums = {c: _split_dot(logs[c][1], later) for c in chains}
            carries = list(state[0])
            ws = {}
            for t, h in chains:
                w = jnp.exp(logs[t, h][0] - (sums[t, h][:, :tk] + _tail(carries[h], r0[t])))
                ws[t, h] = (jnp.where(allowed[t], w, 0.0) if masked else w).astype(BF)
                carries[h] = _add_tail(carries[h], r0[t], sums[t, h][:, tk:])
            acc = state[1]
            for t in tiles:
                acc = _add_tail(acc, r0[t], jnp.dot(jnp.concatenate([ws[t, 0], ws[t, 1]], axis=1), vs[t],
                                                    preferred_element_type=F32))
            return tuple(carries), acc

        zero = jnp.zeros((tq, LANES), F32)
        state = step(i, ((zero, zero), zero), True)
        state = lax.fori_loop(0, i, lambda gg, st: step(i - 1 - gg, st, False), state)
        o_ref[...] = state[1].astype(o_ref.dtype)
        o32_ref[...] = state[1]

    tok = pl.BlockSpec((tq, LANES), lambda p, i: (i, p))
    return pl.pallas_call(
        body, name="sb_attn_fwd", grid=(n_pairs, s // tq),
        in_specs=[tok,
                  pl.BlockSpec((s, LANES), lambda p, i: (0, n_pairs + p)),
                  pl.BlockSpec((s, LANES), lambda p, i: (0, 2 * n_pairs + p))],
        out_specs=[tok, tok],
        out_shape=[jax.ShapeDtypeStruct((s, n_pairs * LANES), BF), jax.ShapeDtypeStruct((s, n_pairs * LANES), F32)],
        compiler_params=_params(("parallel", "arbitrary")),
    )(proj, proj, proj)


def _sb_bwd(proj, o32, do_a, after=()):
    s = proj.shape[0]
    tk, tq = SB_TK, SB_KT * SB_TK
    n_pairs = 4
    scale = 1.0 / math.sqrt(SB_HEAD_DIM)

    def body(q_ref, k_ref, v_ref, o_ref, do_ref, *rest):
        dq_ref, dk_ref, dv_ref = rest[len(after):]
        i = pl.program_id(1)

        @pl.when(i == 0)
        def _():
            dk_ref[...] = jnp.zeros_like(dk_ref)
            dv_ref[...] = jnp.zeros_like(dv_ref)

        diff, later, later_incl = _sb_consts(tq)
        qs = (q_ref[...].astype(F32) * scale).astype(BF)
        do2 = do_ref[...]
        prod = do2.astype(F32) * o_ref[...]
        lane_masks = _head_lane_masks(tk)
        first_head = lax.broadcasted_iota(jnp.int32, (tq, LANES), 1) < SB_HEAD_DIM
        totals = [jnp.broadcast_to(jnp.sum(jnp.where(keep, prod, 0.0), axis=-1, keepdims=True), (tq, tk))
                  for keep in (first_head, jnp.logical_not(first_head))]
        first_head_k = first_head[:tk]

        def step(g_idx, state, masked):
            tiles = list(reversed(range(SB_KT)))
            chains = [(t, h) for t in tiles for h in range(2)]
            rows = {t: pl.ds(pl.multiple_of((g_idx * SB_KT + t) * tk, tk), tk) for t in tiles}
            ks = {t: _both_heads(k_ref[rows[t], :], lane_masks) for t in tiles}
            vs = {t: _both_heads(v_ref[rows[t], :], lane_masks) for t in tiles}
            r0 = {t: t * tk if masked else 0 for t in tiles}
            allowed = {t: _tail(diff, r0[t]) < -t * tk for t in tiles}
            zs = {t: lax.dot_general(_tail(qs, r0[t]), ks[t], NT, preferred_element_type=F32) for t in tiles}
            dws = {t: lax.dot_general(_tail(do2, r0[t]), vs[t], NT, preferred_element_type=F32) for t in tiles}
            logs = {}
            for t, h in chains:
                log_b, sp = _log_terms(zs[t][:, h * tk:(h + 1) * tk])
                logs[t, h] = (log_b, jnp.where(allowed[t], sp, 0.0) if masked else sp)
            sums = {c: _split_dot(logs[c][1], later) for c in chains}
            c_log, c_g = list(state[0]), list(state[1])
            ws, gs = {}, {}
            for t, h in chains:
                w = jnp.exp(logs[t, h][0] - (sums[t, h][:, :tk] + _tail(c_log[h], r0[t])))
                ws[t, h] = (jnp.where(allowed[t], w, 0.0) if masked else w).astype(BF)
                c_log[h] = _add_tail(c_log[h], r0[t], sums[t, h][:, tk:])
                gs[t, h] = ws[t, h].astype(F32) * dws[t][:, h * tk:(h + 1) * tk]
            gsums = {c: _split_dot(gs[c], later_incl) for c in chains}
            dzs = {}
            for t, h in chains:
                beta = jnp.exp(logs[t, h][0])
                earlier = _tail(totals[h], r0[t]) - (gsums[t, h][:, :tk] + _tail(c_g[h], r0[t]))
                dz = gs[t, h] * (1.0 - beta) - earlier * beta
                dzs[t, h] = (jnp.where(allowed[t], dz, 0.0) if masked else dz).astype(BF)
                c_g[h] = _add_tail(c_g[h], r0[t], gsums[t, h][:, tk:])
            dq = state[2]
            for t in tiles:
                dz_both = jnp.concatenate([dzs[t, 0], dzs[t, 1]], axis=1)
                w_both = jnp.concatenate([ws[t, 0], ws[t, 1]], axis=1)
                dq = _add_tail(dq, r0[t], jnp.dot(dz_both, ks[t], preferred_element_type=F32))
                dk2 = lax.dot_general(dz_both, _tail(qs, r0[t]), TN, preferred_element_type=F32)
                dv2 = lax.dot_general(w_both, _tail(do2, r0[t]), TN, preferred_element_type=F32)
                dk_ref[rows[t], :] += jnp.where(first_head_k, dk2[:tk], dk2[tk:])
                dv_ref[rows[t], :] += jnp.where(first_head_k, dv2[:tk], dv2[tk:])
            return tuple(c_log), tuple(c_g), dq

        zero = jnp.zeros((tq, LANES), F32)
        state = step(i, ((zero, zero), (zero, zero), zero), True)
        state = lax.fori_loop(0, i, lambda gg, st: step(i - 1 - gg, st, False), state)
        dq_ref[...] = (state[2] * scale).astype(dq_ref.dtype)

    width = n_pairs * LANES
    return pl.pallas_call(
        body, name="sb_attn_bwd", grid=(n_pairs, s // tq),
        in_specs=[pl.BlockSpec((tq, LANES), lambda p, i: (i, p)),
                  pl.BlockSpec((s, LANES), lambda p, i: (0, n_pairs + p)),
                  pl.BlockSpec((s, LANES), lambda p, i: (0, 2 * n_pairs + p)),
                  pl.BlockSpec((tq, LANES), lambda p, i: (i, p)),
                  pl.BlockSpec((tq, LANES), lambda p, i: (i, p))] + [HBM_SPEC] * len(after),
        out_specs=[pl.BlockSpec((tq, LANES), lambda p, i: (i, p)),
                   pl.BlockSpec((s, LANES), lambda p, i: (0, p)),
                   pl.BlockSpec((s, LANES), lambda p, i: (0, p))],
        out_shape=[jax.ShapeDtypeStruct((s, width), BF), jax.ShapeDtypeStruct((s, width), F32),
                   jax.ShapeDtypeStruct((s, width), F32)],
        compiler_params=_params(("parallel", "arbitrary")),
    )(proj, proj, proj, o32, do_a, *after)


CONV_COL0 = 12


def _shift_rows(v, k):
    n = v.shape[0]
    row = lax.broadcasted_iota(jnp.int32, v.shape, 0)
    rolled = pltpu.roll(v, k % n, axis=0)
    keep = row >= k if k > 0 else row < n + k
    return jnp.where(keep, rolled, 0.0)


def _conv_specs(s):
    return [pl.BlockSpec((s, LANES), lambda cb: (0, CONV_COL0 + cb)),
            pl.BlockSpec((s, LANES), lambda cb: (0, CONV_COL0 + 4 + cb)),
            pl.BlockSpec((s, LANES), lambda cb: (0, CONV_COL0 + 8 + cb)),
            pl.BlockSpec((CONV_ROWS, LANES), lambda cb: (0, cb))]


def _conv_fwd(proj, conv_w):
    s = proj.shape[0]

    def body(u_ref, gb_ref, gc_ref, w_ref, y_ref):
        cu = gc_ref[...].astype(F32) * u_ref[...].astype(F32)
        w = w_ref[...]
        y = w[0:1] * _shift_rows(cu, 2) + w[1:2] * _shift_rows(cu, 1) + w[2:3] * cu
        y_ref[...] = (gb_ref[...].astype(F32) * y).astype(y_ref.dtype)

    return pl.pallas_call(
        body, name="conv_fwd", grid=(4,), in_specs=_conv_specs(s),
        out_specs=pl.BlockSpec((s, LANES), lambda cb: (0, cb)),
        out_shape=jax.ShapeDtypeStruct((s, 4 * LANES), BF),
        compiler_params=_params(("parallel",)),
    )(proj, proj, proj, conv_w)


def _conv_bwd(proj, conv_w, dy):
    s = proj.shape[0]

    def body(u_ref, gb_ref, gc_ref, w_ref, dy_ref, du_ref, dgb_ref, dgc_ref, dw_ref):
        u, gc = u_ref[...].astype(F32), gc_ref[...].astype(F32)
        dyv = dy_ref[...].astype(F32)
        w = w_ref[...]
        cu = gc * u
        cu1, cu2 = _shift_rows(cu, 1), _shift_rows(cu, 2)
        conv = w[0:1] * cu2 + w[1:2] * cu1 + w[2:3] * cu
        dgb_ref[...] = (dyv * conv).astype(dgb_ref.dtype)
        dc = dyv * gb_ref[...].astype(F32)
        dcu = w[2:3] * dc + w[1:2] * _shift_rows(dc, -1) + w[0:1] * _shift_rows(dc, -2)
        dgc_ref[...] = (dcu * u).astype(dgc_ref.dtype)
        du_ref[...] = (dcu * gc).astype(du_ref.dtype)
        tap_row = lax.broadcasted_iota(jnp.int32, (CONV_ROWS, LANES), 0)
        dw = jnp.zeros((CONV_ROWS, LANES), F32)
        for t, shifted in enumerate((cu2, cu1, cu)):
            dw = jnp.where(tap_row == t, jnp.sum(dc * shifted, axis=0, keepdims=True), dw)
        dw_ref[...] = dw

    col = pl.BlockSpec((s, LANES), lambda cb: (0, cb))
    act = jax.ShapeDtypeStruct((s, 4 * LANES), BF)
    return pl.pallas_call(
        body, name="conv_bwd", grid=(4,), in_specs=_conv_specs(s) + [col],
        out_specs=[col, col, col, pl.BlockSpec((CONV_ROWS, LANES), lambda cb: (0, cb))],
        out_shape=[act, act, act, jax.ShapeDtypeStruct((CONV_ROWS, 4 * LANES), F32)],
        compiler_params=_params(("parallel",)),
    )(proj, proj, proj, conv_w, dy)


def _mem_probs(q, k, scale):
    sc = lax.dot_general(q, k, NT, preferred_element_type=F32) * scale
    p = jnp.exp(sc - jnp.max(sc, axis=-1, keepdims=True))
    return p / jnp.sum(p, axis=-1, keepdims=True)


def _mem_fwd(q_m, kv, tq=2048):
    s, d = q_m.shape
    mlen = kv.shape[0]
    hd = d // MEM_HEADS
    tq = _tile(s, tq)
    scale = 1.0 / math.sqrt(hd)

    def body(q_ref, k_ref, v_ref, o_ref):
        p = _mem_probs(q_ref[...], k_ref[...], scale)
        o_ref[...] = jnp.dot(p.astype(BF), v_ref[...], preferred_element_type=F32).astype(o_ref.dtype)

    return pl.pallas_call(
        body, name="mem_attn_fwd", grid=(MEM_HEADS, s // tq),
        in_specs=[pl.BlockSpec((tq, hd), lambda h, i: (i, h)),
                  pl.BlockSpec((mlen, hd), lambda h, i: (0, h)),
                  pl.BlockSpec((mlen, hd), lambda h, i: (0, MEM_HEADS + h))],
        out_specs=pl.BlockSpec((tq, hd), lambda h, i: (i, h)),
        out_shape=jax.ShapeDtypeStruct((s, d), BF),
        compiler_params=_params(("parallel", "parallel")),
    )(q_m, kv, kv)


def _mem_bwd(q_m, kv, do_m, tq=2048):
    s, d = q_m.shape
    mlen = kv.shape[0]
    hd = d // MEM_HEADS
    tq = _tile(s, tq)
    scale = 1.0 / math.sqrt(hd)

    def body(q_ref, k_ref, v_ref, do_ref, dq_ref, dk_ref, dv_ref):
        q, k, v, do = q_ref[...], k_ref[...], v_ref[...], do_ref[...]
        p = _mem_probs(q, k, scale)
        dp = lax.dot_general(do, v, NT, preferred_element_type=F32)
        ds = p * (dp - jnp.sum(dp * p, axis=-1, keepdims=True)) * scale
        dsb = ds.astype(BF)
        dq_ref[...] = jnp.dot(dsb, k, preferred_element_type=F32).astype(dq_ref.dtype)
        dk = lax.dot_general(dsb, q, TN, preferred_element_type=F32)
        dv = lax.dot_general(p.astype(BF), do, TN, preferred_element_type=F32)
        first = pl.program_id(1) == 0

        @pl.when(first)
        def _():
            dk_ref[...] = dk
            dv_ref[...] = dv

        @pl.when(jnp.logical_not(first))
        def _():
            dk_ref[...] += dk
            dv_ref[...] += dv

    tok = pl.BlockSpec((tq, hd), lambda h, i: (i, h))
    memb = pl.BlockSpec((mlen, hd), lambda h, i: (0, h))
    return pl.pallas_call(
        body, name="mem_attn_bwd", grid=(MEM_HEADS, s // tq),
        in_specs=[tok, memb, pl.BlockSpec((mlen, hd), lambda h, i: (0, MEM_HEADS + h)), tok],
        out_specs=[tok, memb, memb],
        out_shape=[jax.ShapeDtypeStruct((s, d), BF), jax.ShapeDtypeStruct((mlen, d), F32),
                   jax.ShapeDtypeStruct((mlen, d), F32)],
        compiler_params=_params(("parallel", "arbitrary")),
    )(q_m, kv, kv, do_m)


def _place():
    x, y, c = lax.axis_index("x"), lax.axis_index("y"), lax.axis_index("c")
    other_chips = [(1 - x, y), (x, 1 - y), (1 - x, 1 - y)]
    return x, y, c, other_chips


def _chip_no(cx, cy):
    return 2 * cx + cy


HBM_SPEC = pl.BlockSpec(memory_space=pl.ANY)


def _cast_place(shard, axis, place, dtype, name, after=()):
    r, c = shard.shape
    tr = _tile(r, max(16, 1048576 // c))
    nblk = r // tr
    if axis == 1:
        full, out_map = (r, N_CHIPS * c), lambda i, pref: (i, pref[0])
    else:
        full, out_map = (N_CHIPS * r, c), lambda i, pref: (pref[0] * nblk + i, 0)

    def body(pref, s_ref, *rest):
        o_ref = rest[-1]
        o_ref[...] = s_ref[...].astype(o_ref.dtype)

    return pl.pallas_call(
        body, name=name,
        grid_spec=pltpu.PrefetchScalarGridSpec(
            num_scalar_prefetch=1, grid=(nblk,),
            in_specs=[pl.BlockSpec((tr, c), lambda i, pref: (i, 0))] + [HBM_SPEC] * len(after),
            out_specs=pl.BlockSpec((tr, c), out_map)),
        out_shape=jax.ShapeDtypeStruct(full, dtype),
        compiler_params=_params(("parallel",)),
    )(place, shard, *after)


def _region(ref, axis, chip_no, half):
    width = ref.shape[axis] // N_CHIPS
    start = pl.multiple_of(chip_no * width, width)
    if axis == 1:
        if half is None:
            return ref.at[:, pl.ds(start, width)]
        hr = ref.shape[0] // 2
        return ref.at[pl.ds(pl.multiple_of(half * hr, hr), hr), pl.ds(start, width)]
    if half is None:
        return ref.at[pl.ds(start, width), :]
    hr = width // 2
    return ref.at[pl.ds(pl.multiple_of(start + half * hr, hr), hr), :]


def _gather_weights(fulls, axes, split):
    n = len(fulls)

    def body(*refs):
        outs = refs[n:2 * n]
        send, recv, fsend, frecv = refs[2 * n:]
        x, y, c, others = _place()
        me = _chip_no(x, y)
        sibling = (x, y, 1 - c)

        def copy(w, chip_no, half, sems, p, to):
            reg = _region(outs[w], axes[w], chip_no, half)
            return pltpu.make_async_remote_copy(
                src_ref=reg, dst_ref=reg, send_sem=sems[0].at[w, p], recv_sem=sems[1].at[w, p],
                device_id=to, device_id_type=MESH)

        for w in range(n):
            for p, chip in enumerate(others):
                copy(w, me, c if split[w] else None, (send, recv), p, (chip[0], chip[1], c)).start()
        for w in range(n):
            for p, chip in enumerate(others):
                half = c if split[w] else None
                copy(w, _chip_no(*chip), half, (send, recv), p, (chip[0], chip[1], c)).wait_recv()
                if split[w]:
                    copy(w, _chip_no(*chip), c, (fsend, frecv), p, sibling).start()
        for w in range(n):
            for p, chip in enumerate(others):
                copy(w, me, c if split[w] else None, (send, recv), p, (chip[0], chip[1], c)).wait_send()
                if split[w]:
                    handed = copy(w, _chip_no(*chip), 1 - c, (fsend, frecv), p, sibling)
                    handed.wait_recv()
                    handed.wait_send()

    return pl.pallas_call(
        body, name="gather_weights",
        in_specs=[HBM_SPEC] * n, out_specs=[HBM_SPEC] * n,
        out_shape=[jax.ShapeDtypeStruct(f.shape, f.dtype) for f in fulls],
        input_output_aliases={i: i for i in range(n)},
        scratch_shapes=[pltpu.SemaphoreType.DMA((n, 3))] * 4,
        compiler_params=pltpu.CompilerParams(has_side_effects=True),
    )(*fulls)


SEM_SPEC = pl.BlockSpec(memory_space=pltpu.SEMAPHORE)
IN_HBM = pl.BlockSpec(memory_space=pltpu.HBM)
FLOWS = pltpu.CompilerParams(has_side_effects=pltpu.SideEffectType.DATAFLOW_SIDE_EFFECTING)
TOKEN = jax.ShapeDtypeStruct((8, LANES), F32)


def _in_hbm(arrays):
    return [pltpu.with_memory_space_constraint(a, pltpu.HBM) for a in arrays]


def _hbm_like(arrays):
    return [pltpu.HBM(a.shape, a.dtype) for a in arrays]


def _fetch_copy(refs, axes, whole, send, recv, w, p, chip, c, arriving):
    owner = _chip_no(*chip) if arriving else _chip_no(lax.axis_index("x"), lax.axis_index("y"))
    reg = _region(refs[w], axes[w], owner, None if whole[w] else c)
    return pltpu.make_async_remote_copy(
        src_ref=reg, dst_ref=reg, send_sem=send[p], recv_sem=recv[p],
        device_id=(chip[0], chip[1], c), device_id_type=MESH)


N_PEERS = N_CHIPS - 1
PEER_SEMS = [pltpu.SemaphoreType.DMA(())] * (2 * N_PEERS)


def _fetch_start(fulls, axes, whole, after, name):
    n = len(fulls)

    def body(*refs):
        ins = refs[:n]
        sems = refs[n + len(after):]
        send, recv = sems[:N_PEERS], sems[N_PEERS:2 * N_PEERS]
        token = refs[-1]
        _, _, c, others = _place()
        for w in range(n):
            for p, chip in enumerate(others):
                _fetch_copy(ins, axes, whole, send, recv, w, p, chip, c, False).start()
        token[...] = jnp.zeros_like(token)

    res = pl.pallas_call(
        body, name=name,
        in_specs=[IN_HBM] * n + [HBM_SPEC] * len(after),
        out_specs=[SEM_SPEC] * len(PEER_SEMS) + [IN_HBM] * n + [pl.BlockSpec(memory_space=pltpu.VMEM)],
        out_shape=PEER_SEMS + _hbm_like(fulls) + [TOKEN],
        input_output_aliases={i: len(PEER_SEMS) + i for i in range(n)},
        compiler_params=FLOWS,
    )(*_in_hbm(fulls), *after)
    k = len(PEER_SEMS)
    return list(res[:k]), list(res[k:k + n]), res[-1]


def _fetch_wait(sems, fulls, axes, whole, after, name):
    n = len(fulls)

    def body(*refs):
        ins = refs[:n]
        send, recv = refs[n:n + N_PEERS], refs[n + N_PEERS:n + 2 * N_PEERS]
        _, _, c, others = _place()
        for w in range(n):
            for p, chip in enumerate(others):
                _fetch_copy(ins, axes, whole, send, recv, w, p, chip, c, False).wait_send()
                _fetch_copy(ins, axes, whole, send, recv, w, p, chip, c, True).wait_recv()

    res = pl.pallas_call(
        body, name=name,
        in_specs=[IN_HBM] * n + [SEM_SPEC] * len(sems) + [HBM_SPEC] * len(after),
        out_specs=[IN_HBM] * n,
        out_shape=_hbm_like(fulls),
        input_output_aliases={i: i for i in range(n)},
        compiler_params=FLOWS,
    )(*fulls, *sems, *after)
    return list(res)


def _hand_on(fulls, axes, name):
    n = len(fulls)

    def body(*refs):
        outs = refs[n:2 * n]
        send, recv = refs[2 * n:]
        x, y, c, others = _place()

        def copy(w, p, chip, half):
            reg = _region(outs[w], axes[w], _chip_no(*chip), half)
            return pltpu.make_async_remote_copy(
                src_ref=reg, dst_ref=reg, send_sem=send.at[w, p], recv_sem=recv.at[w, p],
                device_id=(x, y, 1 - c), device_id_type=MESH)

        for w in range(n):
            for p, chip in enumerate(others):
                copy(w, p, chip, c).start()
        for w in range(n):
            for p, chip in enumerate(others):
                copy(w, p, chip, 1 - c).wait()

    return pl.pallas_call(
        body, name=name,
        in_specs=[HBM_SPEC] * n, out_specs=[HBM_SPEC] * n,
        out_shape=[jax.ShapeDtypeStruct(f.shape, f.dtype) for f in fulls],
        input_output_aliases={i: i for i in range(n)},
        scratch_shapes=[pltpu.SemaphoreType.DMA((n, 3)), pltpu.SemaphoreType.DMA((n, 3))],
        compiler_params=pltpu.CompilerParams(has_side_effects=True),
    )(*fulls)


def _pair_exchange(grads, name):
    n = len(grads)

    def body(*refs):
        ins, outs = refs[:n], refs[n:2 * n]
        send, recv = refs[2 * n:]
        x, y, c, _ = _place()
        cps = []
        for w in range(n):
            cp = pltpu.make_async_remote_copy(
                src_ref=ins[w].at[:, 1 - c], dst_ref=outs[w], send_sem=send.at[w], recv_sem=recv.at[w],
                device_id=(x, y, 1 - c), device_id_type=MESH)
            cp.start()
            cps.append(cp)
        for cp in cps:
            cp.wait()

    return pl.pallas_call(
        body, name=name,
        in_specs=[HBM_SPEC] * n, out_specs=[HBM_SPEC] * n,
        out_shape=[jax.ShapeDtypeStruct((g.shape[0],) + g.shape[2:], g.dtype) for g in grads],
        scratch_shapes=[pltpu.SemaphoreType.DMA((n,)), pltpu.SemaphoreType.DMA((n,))],
        compiler_params=pltpu.CompilerParams(has_side_effects=True),
    )(*grads)


PAIR_SEMS = [pltpu.SemaphoreType.DMA(())] * 2


def _pair_copy(grads, lands, send, recv, w):
    x, y, c, _ = _place()
    return pltpu.make_async_remote_copy(
        src_ref=grads[w].at[:, 1 - c], dst_ref=lands[w], send_sem=send, recv_sem=recv,
        device_id=(x, y, 1 - c), device_id_type=MESH)


def _pair_exchange_start(grads, name):
    n = len(grads)
    lands = [lax.empty((g.shape[0],) + g.shape[2:], g.dtype) for g in grads]

    def body(*refs):
        ins, land_refs = refs[:n], refs[n:2 * n]
        send, recv = refs[2 * n], refs[2 * n + 1]
        token = refs[-1]
        for w in range(n):
            _pair_copy(ins, land_refs, send, recv, w).start()
        token[...] = jnp.zeros_like(token)

    k = len(PAIR_SEMS)
    res = pl.pallas_call(
        body, name=name,
        in_specs=[IN_HBM] * (2 * n),
        out_specs=[SEM_SPEC] * k + [IN_HBM] * (2 * n) + [pl.BlockSpec(memory_space=pltpu.VMEM)],
        out_shape=PAIR_SEMS + _hbm_like(list(grads) + lands) + [TOKEN],
        input_output_aliases={i: k + i for i in range(2 * n)},
        compiler_params=FLOWS,
    )(*_in_hbm(list(grads) + lands))
    return list(res[:k]), list(res[k:k + n]), list(res[k + n:k + 2 * n]), res[-1]


def _pair_exchange_wait(sems, grads, lands, after, name):
    n = len(grads)

    def body(*refs):
        ins, land_refs = refs[:n], refs[n:2 * n]
        send, recv = refs[2 * n], refs[2 * n + 1]
        for w in range(n):
            cp = _pair_copy(ins, land_refs, send, recv, w)
            cp.wait_send()
            cp.wait_recv()

    res = pl.pallas_call(
        body, name=name,
        in_specs=[IN_HBM] * (2 * n) + [SEM_SPEC] * len(sems) + [HBM_SPEC] * len(after),
        out_specs=[IN_HBM] * (2 * n), out_shape=_hbm_like(list(grads) + list(lands)),
        input_output_aliases={i: i for i in range(2 * n)},
        compiler_params=FLOWS,
    )(*grads, *lands, *sems, *after)
    return list(res[:n]), list(res[n:])


def _pair_add(g4, got, core, name):
    nj, _, hr, cdim = g4.shape
    tr = _tile(hr, max(8, 524288 // cdim))

    def body(core_ref, a_ref, b_ref, o_ref):
        o_ref[...] = (a_ref[...].astype(F32) + b_ref[...].astype(F32)).astype(o_ref.dtype)

    return pl.pallas_call(
        body, name=name,
        grid_spec=pltpu.PrefetchScalarGridSpec(
            num_scalar_prefetch=1, grid=(nj, hr // tr),
            in_specs=[pl.BlockSpec((1, None, tr, cdim), lambda j, i, core_ref: (j, core_ref[0], i, 0)),
                      pl.BlockSpec((1, tr, cdim), lambda j, i, core_ref: (j, i, 0))],
            out_specs=pl.BlockSpec((1, tr, cdim), lambda j, i, core_ref: (j, i, 0))),
        out_shape=jax.ShapeDtypeStruct((nj, hr, cdim), BF),
        compiler_params=_params(("parallel", "parallel")),
    )(core, g4, got)


def _piece(ref, axis, j, hc):
    if axis == 0:
        return ref.at[j]
    return ref.at[0, :, pl.ds(pl.multiple_of(j * hc, hc), hc)]


def _slot_shapes(sums, axes):
    return [(N_CHIPS - 1, sm.shape[1], sm.shape[2] // (1 if ax == 0 else N_CHIPS)) for sm, ax in zip(sums, axes)]


def _slot_copy(sums, lands, axes, send, recv, w, p, chip, c):
    return pltpu.make_async_remote_copy(
        src_ref=_piece(sums[w], axes[w], _chip_no(*chip), lands[w].shape[2]), dst_ref=lands[w].at[p],
        send_sem=send[p], recv_sem=recv[p],
        device_id=(chip[0], chip[1], c), device_id_type=MESH)


def _chip_exchange_start(sums, axes, name):
    n = len(sums)
    shapes = _slot_shapes(sums, axes)
    lands = [lax.empty(sh, sm.dtype) for sh, sm in zip(shapes, sums)]

    def body(*refs):
        ins, land_refs = refs[:n], refs[n:2 * n]
        send, recv = refs[2 * n:2 * n + N_PEERS], refs[2 * n + N_PEERS:2 * n + 2 * N_PEERS]
        token = refs[-1]
        _, _, c, others = _place()
        for w in range(n):
            for p, chip in enumerate(others):
                _slot_copy(ins, land_refs, axes, send, recv, w, p, chip, c).start()
        token[...] = jnp.zeros_like(token)

    k = len(PEER_SEMS)
    res = pl.pallas_call(
        body, name=name,
        in_specs=[IN_HBM] * (2 * n),
        out_specs=[SEM_SPEC] * k + [IN_HBM] * (2 * n) + [pl.BlockSpec(memory_space=pltpu.VMEM)],
        out_shape=PEER_SEMS + _hbm_like(list(sums) + lands) + [TOKEN],
        input_output_aliases={i: k + i for i in range(2 * n)},
        compiler_params=FLOWS,
    )(*_in_hbm(list(sums) + lands))
    return list(res[:k]), list(res[k:k + n]), list(res[k + n:k + 2 * n]), res[-1]


def _chip_exchange_wait(sems, sums, lands, axes, after, name):
    n = len(sums)

    def body(*refs):
        ins, land_refs = refs[:n], refs[n:2 * n]
        send, recv = refs[2 * n:2 * n + N_PEERS], refs[2 * n + N_PEERS:2 * n + 2 * N_PEERS]
        _, _, c, others = _place()
        for w in range(n):
            for p, chip in enumerate(others):
                cp = _slot_copy(ins, land_refs, axes, send, recv, w, p, chip, c)
                cp.wait_send()
                cp.wait_recv()

    res = pl.pallas_call(
        body, name=name,
        in_specs=[IN_HBM] * (2 * n) + [SEM_SPEC] * len(sems) + [HBM_SPEC] * len(after),
        out_specs=[IN_HBM] * (2 * n), out_shape=_hbm_like(list(sums) + list(lands)),
        input_output_aliases={i: i for i in range(2 * n)},
        compiler_params=FLOWS,
    )(*sums, *lands, *sems, *after)
    return list(res[:n]), list(res[n:])


def _chip_sum(psum, slots, axis, place, name):
    _, hr, hc = slots.shape
    tr = _tile(hr, 256)
    own_map = (lambda i, pref: (0, i, pref[0])) if axis == 1 else (lambda i, pref: (pref[0], i, 0))

    def body(pref, own_ref, s_ref, o_ref):
        o_ref[...] = ((own_ref[...].astype(F32) + s_ref[0].astype(F32)) + s_ref[1].astype(F32)) + s_ref[2].astype(F32)

    return pl.pallas_call(
        body, name=name,
        grid_spec=pltpu.PrefetchScalarGridSpec(
            num_scalar_prefetch=1, grid=(hr // tr,),
            in_specs=[pl.BlockSpec((None, tr, hc), own_map),
                      pl.BlockSpec((N_CHIPS - 1, tr, hc), lambda i, pref: (0, i, 0))],
            out_specs=pl.BlockSpec((None, tr, hc), lambda i, pref: (pref[1], i, 0))),
        out_shape=jax.ShapeDtypeStruct((2, hr, hc), F32),
        compiler_params=_params(("parallel",)),
    )(place, psum, slots)


def _half_swap(both):
    n = len(both)

    def body(*refs):
        outs = refs[n:2 * n]
        send, recv = refs[2 * n:]
        x, y, c, _ = _place()

        def copy(w, half):
            return pltpu.make_async_remote_copy(
                src_ref=outs[w].at[half], dst_ref=outs[w].at[half], send_sem=send.at[w], recv_sem=recv.at[w],
                device_id=(x, y, 1 - c), device_id_type=MESH)

        for w in range(n):
            copy(w, c).start()
        for w in range(n):
            copy(w, 1 - c).wait()

    return pl.pallas_call(
        body, name="grad_half_swap",
        in_specs=[HBM_SPEC] * n, out_specs=[HBM_SPEC] * n,
        out_shape=[jax.ShapeDtypeStruct(b.shape, b.dtype) for b in both],
        input_output_aliases={i: i for i in range(n)},
        scratch_shapes=[pltpu.SemaphoreType.DMA((n,)), pltpu.SemaphoreType.DMA((n,))],
        compiler_params=pltpu.CompilerParams(has_side_effects=True),
    )(*both)


def _allreduce_small(pack):
    rows, d = pack.shape

    def body(p_ref, o_ref, slots, send, recv):
        x, y, c, _ = _place()
        me = 4 * x + 2 * y + c
        slots[me] = p_ref[...]
        cps = []
        for k in range(1, N_DEV):
            px, py, pc = x ^ (k >> 2), y ^ ((k >> 1) & 1), c ^ (k & 1)
            cp = pltpu.make_async_remote_copy(
                src_ref=p_ref, dst_ref=slots.at[me], send_sem=send.at[k - 1], recv_sem=recv.at[k - 1],
                device_id=(px, py, pc), device_id_type=MESH)
            cp.start()
            cps.append(cp)
        for k in range(1, N_DEV):
            px, py, pc = x ^ (k >> 2), y ^ ((k >> 1) & 1), c ^ (k & 1)
            arrival = pltpu.make_async_remote_copy(
                src_ref=p_ref, dst_ref=slots.at[4 * px + 2 * py + pc], send_sem=send.at[k - 1],
                recv_sem=recv.at[k - 1], device_id=(px, py, pc), device_id_type=MESH)
            arrival.wait_recv()
            arrival.wait_send()
        acc = slots[0]
        for k in range(1, N_DEV):
            acc = acc + slots[k]
        o_ref[...] = acc

    vm = pl.BlockSpec(memory_space=pltpu.VMEM)
    return pl.pallas_call(
        body, name="allreduce_small", in_specs=[vm], out_specs=vm,
        out_shape=jax.ShapeDtypeStruct((rows, d), F32),
        scratch_shapes=[pltpu.VMEM((N_DEV, rows, d), F32), pltpu.SemaphoreType.DMA((N_DEV - 1,)),
                        pltpu.SemaphoreType.DMA((N_DEV - 1,))],
        compiler_params=pltpu.CompilerParams(has_side_effects=True),
    )(pack)


def _adamw(w, g, m, v, name):
    rows, cols = w.shape

    def fn(wv, gv, mv, vv):
        m2 = ADAM_B1 * mv + (1.0 - ADAM_B1) * gv
        v2 = ADAM_B2 * vv + (1.0 - ADAM_B2) * (gv * gv)
        m_hat = m2 / (1.0 - ADAM_B1 ** ADAM_STEP)
        v_hat = v2 / (1.0 - ADAM_B2 ** ADAM_STEP)
        delta = -ADAM_LR * (m_hat / (jnp.sqrt(v_hat) + ADAM_EPS) + ADAM_WD * wv)
        return delta, m2, v2

    ins = [(a, cols, 0) for a in (w, g, m, v)]
    return _rowwise(fn, ins, [(cols, F32)] * 3, rows=rows, tm=_tile(rows, max(8, 262144 // cols)), name=name)


BIG = ["w_in", "w_branch_a", "w_branch_b", "w_mix_out", "w_mem_q", "w_mem_kv", "w_mem_o", "w_ffn_in", "w_ffn_out"]
BIG_AXIS = {"w_in": 1, "w_branch_a": 1, "w_branch_b": 1, "w_mix_out": 0, "w_mem_q": 0, "w_mem_kv": 1,
            "w_mem_o": 0, "w_ffn_in": 1, "w_ffn_out": 0}
NORMS = ["norm_mix", "norm_mem_q", "norm_mem_kv", "norm_ffn", "norm_final"]
ORDER = ["norm_mix", "w_in", "conv_w", "w_branch_a", "w_branch_b", "w_mix_out", "norm_mem_q", "norm_mem_kv",
         "w_mem_q", "w_mem_kv", "w_mem_o", "norm_ffn", "w_ffn_in", "w_ffn_out", "norm_final"]


def _pack_small(vals, conv):
    d = vals[0].shape[-1]
    rows = [v.reshape(1, d) for v in vals]
    conv = jnp.pad(conv, ((0, 0), (0, d - conv.shape[1])))
    pad = jnp.zeros((SMALL_ROWS - len(rows) - CONV_K, d), F32)
    return jnp.concatenate(rows + [conv, pad], axis=0)


def kernel(x, mem, norm_mix, w_in, conv_w, w_branch_a, w_branch_b, w_mix_out, norm_mem_q, norm_mem_kv, w_mem_q, w_mem_kv, w_mem_o, norm_ffn, w_ffn_in, w_ffn_out, norm_final, loss_target, m_norm_mix, m_w_in, m_conv_w, m_w_branch_a, m_w_branch_b, m_w_mix_out, m_norm_mem_q, m_norm_mem_kv, m_w_mem_q, m_w_mem_kv, m_w_mem_o, m_norm_ffn, m_w_ffn_in, m_w_ffn_out, m_norm_final, v_norm_mix, v_w_in, v_conv_w, v_w_branch_a, v_w_branch_b, v_w_mix_out, v_norm_mem_q, v_norm_mem_kv, v_w_mem_q, v_w_mem_kv, v_w_mem_o, v_norm_ffn, v_w_ffn_in, v_w_ffn_out, v_norm_final):
    args = dict(locals())
    wts = {n: args[n] for n in ORDER}
    mom = {n: args["m_" + n] for n in ORDER}
    var = {n: args["v_" + n] for n in ORDER}
    x = x[0]
    mem = mem[0]
    target = loss_target[0]
    s, d = x.shape
    gains = {n: wts[n].reshape(1, d) for n in NORMS}
    chip = 2 * lax.axis_index("x") + lax.axis_index("y")
    core = lax.axis_index("c").astype(jnp.int32).reshape(1)
    place = jnp.stack([chip, lax.axis_index("c")]).astype(jnp.int32)

    conv_shard = jnp.pad(conv_w[0], ((0, CONV_ROWS - CONV_K), (0, 0)))
    first_axes, first_whole = [BIG_AXIS["w_in"], 1], [False, True]
    first = [_cast_place(wts["w_in"][0], BIG_AXIS["w_in"], place, BF, "place_w_in"),
             _cast_place(conv_shard, 1, place, F32, "place_conv_w")]
    a_sems, first, a_token = _fetch_start(first, first_axes, first_whole, (), "fetch_start_first")
    later_w = [n for n in BIG if n != "w_in"]
    later_axes = [BIG_AXIS[n] for n in later_w]
    placed = [_cast_place(wts[n][0], BIG_AXIS[n], place, BF, "place_" + n, after=(a_token,)) for n in later_w]
    h1 = _rmsnorm(x, gains["norm_mix"], "norm_mix_fwd", after=(a_token,))
    first = _fetch_wait(a_sems, first, first_axes, first_whole, [h1] + placed, "fetch_wait_first")
    conv_full = first[1]
    W = {"w_in": _hand_on(first[:1], first_axes[:1], "gather_hand_on_first")[0]}
    f_sems, later_bufs, f_token = _fetch_start(placed, later_axes, [False] * len(placed), (W["w_in"],), "fetch_start")

    proj = _matmul(h1, W["w_in"], tn=1280, after=(f_token,), name="in_proj")
    o_a, o_a32 = _sb_fwd(proj)
    y_b = _conv_fwd(proj, conv_full)
    later_bufs = _fetch_wait(f_sems, later_bufs, later_axes, [False] * len(placed), (o_a, y_b), "fetch_wait")
    W.update(zip(later_w, _hand_on(later_bufs, later_axes, "gather_hand_on")))
    br_a, br_b, merged = _branches_merge(o_a, y_b, W["w_branch_a"], W["w_branch_b"], proj)
    x1 = _matmul(merged, W["w_mix_out"], tn=1024, out_dtype=F32, resid=x, name="mix_out")

    hq, q_m = _norm_matmul(x1, gains["norm_mem_q"], W["w_mem_q"], tn=1024, name="norm_mem_q")
    mn, kv = _norm_matmul(mem, gains["norm_mem_kv"], W["w_mem_kv"], tn=1024, name="norm_mem_kv")
    o_m = _mem_fwd(q_m, kv)
    x2 = _matmul(o_m, W["w_mem_o"], tn=1024, out_dtype=F32, resid=x1, name="mem_o")

    hf, gate, up, act = _norm_ffn_in_swiglu(x2, gains["norm_ffn"], W["w_ffn_in"])

    dx3, dx3_b, dg_final, loss_part = _ffn_out_loss(act, W["w_ffn_out"], x2, gains["norm_final"], target)

    dgu = _d_act_swiglu(dx3_b, W["w_ffn_out"], gate, up)
    gw = {"w_ffn_out": _matmul(act, dx3_b, ta=True, tm=1408, tn=512, name="gw_ffn_out")}
    gw["w_ffn_in"] = _matmul(hf, dgu, ta=True, tn=512, name="gw_ffn_in")

    def halves_of(names):
        views = []
        for n in names:
            r, cdim = gw[n].shape
            views.append(gw[n].reshape(1, 2, r // 2, cdim) if BIG_AXIS[n] == 1
                         else gw[n].reshape(N_CHIPS, 2, r // (2 * N_CHIPS), cdim))
        return views

    def pair_start(names, tag):
        sems, views, lands, token = _pair_exchange_start(halves_of(names), "grad_pair_start_" + tag)
        return (names, sems, views, lands), token

    def chip_start(names, views, got, tag):
        sums = [_pair_add(v, g, core, "pair_add_" + n) for n, v, g in zip(names, views, got)]
        group_axes = [BIG_AXIS[n] for n in names]
        sems, sums, lands, token = _chip_exchange_start(sums, group_axes, "grad_chip_start_" + tag)
        return (names, group_axes, sems, sums, lands), token

    def pair_finish(pair, after, tag):
        names, sems, views, lands = pair
        views, got = _pair_exchange_wait(sems, views, lands, after, "grad_pair_wait_" + tag)
        return chip_start(names, views, got, tag)

    def reduce_finish(group, after, tag):
        names, group_axes, sems, sums, lands = group
        sums, slots = _chip_exchange_wait(sems, sums, lands, group_axes, after, "grad_chip_wait_" + tag)
        return [_chip_sum(sm, sl, BIG_AXIS[n], place, "chip_sum_" + n) for n, sm, sl in zip(names, sums, slots)]

    pair_ffn, token_pair_ffn = pair_start(["w_ffn_in", "w_ffn_out"], "ffn")
    dx2, dx2_b, dg_ffn = _matmul_norm_bwd(dgu, W["w_ffn_in"], x2, gains["norm_ffn"], dx3, tm=256, bf_copy=True,
                                          after=(token_pair_ffn,), name="d_hf_norm_bwd")
    group_ffn, token_ffn = pair_finish(pair_ffn, (dx2,), "ffn")

    do_m = _matmul(dx2_b, W["w_mem_o"], tb=True, tn=1024, after=(token_ffn,), name="d_o_m")
    gw["w_mem_o"] = _matmul(o_m, dx2_b, ta=True, tn=512, name="gw_mem_o")
    dq_m, dk_m, dv_m = _mem_bwd(q_m, kv, do_m)
    dkv = jnp.concatenate([dk_m, dv_m], axis=-1)
    dx1, dx1_b, dg_q = _matmul_norm_bwd(dq_m, W["w_mem_q"], x1, gains["norm_mem_q"], dx2, tm=512, bf_copy=True,
                                        name="d_hq_norm_bwd")
    gw["w_mem_q"] = _matmul(hq, dq_m, ta=True, tn=512, name="gw_mem_q")
    dmn = _matmul(dkv, W["w_mem_kv"], tb=True, tn=1024, out_dtype=F32, name="d_mn")
    gw["w_mem_kv"] = _matmul(mn, dkv, ta=True, tn=1024, name="gw_mem_kv")
    _, dg_kv = _rmsnorm_bwd(mem, gains["norm_mem_kv"], dmn, None, "norm_mem_kv_bwd")

    dbr_a, dbr_b, dga, dgb = _d_merged_gates(dx1_b, W["w_mix_out"], proj, br_a, br_b)
    gw["w_mix_out"] = _matmul(merged, dx1_b, ta=True, tn=512, name="gw_mix_out")
    do_a = _matmul(dbr_a, W["w_branch_a"], tb=True, name="d_o_a")
    gw["w_branch_a"] = _matmul(o_a, dbr_a, ta=True, tn=512, name="gw_branch_a")
    dy_b = _matmul(dbr_b, W["w_branch_b"], tb=True, name="d_y_b")
    gw["w_branch_b"] = _matmul(y_b, dbr_b, ta=True, tn=512, name="gw_branch_b")
    pair_mid, token_pair_mid = pair_start(
        ["w_mem_o", "w_mem_q", "w_mem_kv", "w_mix_out", "w_branch_a", "w_branch_b"], "mid")
    du, dgate_b, dgate_c, dconv = _conv_bwd(proj, conv_full, dy_b)
    dq, dk, dv = _sb_bwd(proj, o_a32, do_a, after=(token_pair_mid,))
    group_mid, token_mid = pair_finish(pair_mid, (dq,), "mid")

    def assemble(*parts):
        return jnp.concatenate([p.astype(BF) for p in parts], axis=-1)

    hw = dq.shape[1]
    dproj = _rowwise(assemble, [(t, hw, 0) for t in (dq, dk, dv, du, dgate_b, dgate_c)] + [(dga, d, 0), (dgb, d, 0)],
                     [(proj.shape[1], BF)], rows=s, tm=256, name="assemble_dproj", after=(token_mid,))[0]
    gw["w_in"] = _matmul(h1, dproj, ta=True, tn=640, name="gw_in")
    views_in = halves_of(["w_in"])
    group_in, token_in = chip_start(["w_in"], views_in, _pair_exchange(views_in, "grad_pair_exchange_in"), "in")
    grad_x, dg_mix = _matmul_norm_bwd(dproj, W["w_in"], x, gains["norm_mix"], dx1, tm=256, bf_copy=False,
                                      after=(token_in,), name="d_h1_norm_bwd")

    halves = {}
    for group, tag in ((group_ffn, "ffn"), (group_mid, "mid"), (group_in, "in")):
        halves.update(zip(group[0], reduce_finish(group, (grad_x,), tag)))
    both = _half_swap([halves[n] for n in BIG])
    grads = {n: b.reshape(wts[n].shape[1:]) for n, b in zip(BIG, both)}

    small_g = [dg_mix, dg_q, dg_kv, dg_ffn, dg_final]
    pack = _pack_small(small_g, dconv[:CONV_K])
    pack = pack.at[ROW_LOSS].set(jnp.broadcast_to(loss_part[0, :1], (d,)))
    red = _allreduce_small(pack)
    loss = red[ROW_LOSS, 0]
    cw = conv_w.shape[2]
    conv_g = lax.dynamic_slice(red, (ROW_CONV, chip * cw), (CONV_K, cw))
    small_grad = _pack_small([red[i] for i in range(len(NORMS))], conv_g)
    small = [_pack_small([t[n] for n in NORMS], t["conv_w"][0]) for t in (wts, mom, var)]
    s_delta, s_m, s_v = _adamw(small[0], small_grad, small[1], small[2], "adamw_small")

    out = {"grad": {}, "delta": {}, "new_m": {}, "new_v": {}}
    for n in BIG:
        shp = wts[n].shape
        dl, m2, v2 = _adamw(wts[n][0], grads[n], mom[n][0], var[n][0], "adamw_" + n)
        out["grad"][n] = grads[n].reshape(shp)
        out["delta"][n], out["new_m"][n], out["new_v"][n] = dl.reshape(shp), m2.reshape(shp), v2.reshape(shp)
    for key, blk in (("grad", small_grad), ("delta", s_delta), ("new_m", s_m), ("new_v", s_v)):
        for i, n in enumerate(NORMS):
            out[key][n] = blk[i].reshape(wts[n].shape)
        out[key]["conv_w"] = blk[ROW_CONV:ROW_CONV + CONV_K, :cw].reshape(conv_w.shape)

    return (loss, grad_x[None], *[out["grad"][n] for n in ORDER], *[out["delta"][n] for n in ORDER],
            *[out["new_m"][n] for n in ORDER], *[out["new_v"][n] for n in ORDER])
```

```python
import functools
import math

import jax
import jax.numpy as jnp
from jax import lax
from jax.experimental import pallas as pl
from jax.experimental.pallas import tpu as pltpu

BF = jnp.bfloat16
F32 = jnp.float32
MESH = pl.DeviceIdType.MESH

SB_HEAD_DIM = 64
LANES = 128
MEM_HEADS = 4
CONV_K = 3
CONV_ROWS = 8
EPS = 1e-6
N_CHIPS = 4
N_DEV = 8
VMEM_LIMIT = 56 * 1024 * 1024

ADAM_LR = 0.001
ADAM_B1 = 0.9
ADAM_B2 = 0.999
ADAM_EPS = 1e-08
ADAM_WD = 0.01
ADAM_STEP = 10

SMALL_ROWS = 16
ROW_CONV = 5
ROW_LOSS = 8


def _params(sem=None, **kw):
    return pltpu.CompilerParams(dimension_semantics=sem, vmem_limit_bytes=VMEM_LIMIT, **kw)


def _tile(dim, pref):
    if dim <= pref:
        return dim
    for step in (LANES, 8):
        t = (pref // step) * step
        while t >= step:
            if dim % t == 0:
                return t
            t -= step
    raise ValueError(f"no tile of {dim} under {pref}")


def _matmul(a, b, *, ta=False, tb=False, tm=1024, tn=512, tk=None, out_dtype=BF, resid=None, after=(), name):
    if ta:
        kdim, m = a.shape
    else:
        m, kdim = a.shape
    n = b.shape[0] if tb else b.shape[1]
    tm, tn = _tile(m, tm), _tile(n, tn)
    tk = _tile(kdim, tk or kdim)
    nk = kdim // tk
    a_spec = pl.BlockSpec((tk, tm), lambda i, j, k: (k, i)) if ta else pl.BlockSpec((tm, tk), lambda i, j, k: (i, k))
    b_spec = pl.BlockSpec((tn, tk), lambda i, j, k: (j, k)) if tb else pl.BlockSpec((tk, tn), lambda i, j, k: (k, j))
    o_spec = pl.BlockSpec((tm, tn), lambda i, j, k: (i, j))
    dims = (((0 if ta else 1,), (1 if tb else 0,)), ((), ()))
    has_res = resid is not None

    def body(*refs):
        a_ref, b_ref = refs[0], refs[1]
        o_ref = refs[2 + has_res + len(after)]
        av, bv = a_ref[...], b_ref[...]
        if av.dtype != BF:
            av = av.astype(BF)
        if bv.dtype != BF:
            bv = bv.astype(BF)
        p = lax.dot_general(av, bv, dims, preferred_element_type=F32)

        def finish(acc):
            if has_res:
                acc = refs[2][...] + acc
            o_ref[...] = acc.astype(o_ref.dtype)

        if nk == 1:
            finish(p)
        else:
            acc_ref = refs[-1]
            k = pl.program_id(2)

            @pl.when(k == 0)
            def _():
                acc_ref[...] = p

            @pl.when(k > 0)
            def _():
                acc_ref[...] += p

            @pl.when(k == nk - 1)
            def _():
                finish(acc_ref[...])

    return pl.pallas_call(
        body, name=name, grid=(m // tm, n // tn, nk),
        in_specs=[a_spec, b_spec] + ([o_spec] if has_res else []) + [HBM_SPEC] * len(after),
        out_specs=o_spec, out_shape=jax.ShapeDtypeStruct((m, n), out_dtype),
        scratch_shapes=[pltpu.VMEM((tm, tn), F32)] if nk > 1 else [],
        compiler_params=_params(("parallel", "parallel", "arbitrary")),
    )(*([a, b] + ([resid] if has_res else []) + list(after)))


def _rowwise(fn, ins, outs, *, rows, tm, name, accs=(), after=()):
    tm = _tile(rows, tm)
    in_specs, args = [], []
    for arr, cols, cb in ins:
        if cols is None:
            in_specs.append(pl.BlockSpec(arr.shape, lambda i, nd=arr.ndim: (0,) * nd))
        else:
            in_specs.append(pl.BlockSpec((tm, cols), lambda i, cb=cb: (i, cb)))
        args.append(arr)
    out_specs = [pl.BlockSpec((tm, cols), lambda i: (i, 0)) for cols, _ in outs]
    out_shape = [jax.ShapeDtypeStruct((rows, cols), dt) for cols, dt in outs]
    for r, c in accs:
        out_specs.append(pl.BlockSpec((r, c), lambda i: (0, 0)))
        out_shape.append(jax.ShapeDtypeStruct((r, c), F32))
    n_in, n_out = len(ins), len(outs)
    in_specs += [HBM_SPEC] * len(after)
    args += list(after)

    def body(*refs):
        res = fn(*[r[...] for r in refs[:n_in]])
        if not isinstance(res, (tuple, list)):
            res = (res,)
        refs = refs[n_in + len(after):]
        for o_ref, val in zip(refs[:n_out], res[:n_out]):
            o_ref[...] = val.astype(o_ref.dtype)
        first = pl.program_id(0) == 0
        for a_ref, val in zip(refs[n_out:], res[n_out:]):
            @pl.when(first)
            def _(a_ref=a_ref, val=val):
                a_ref[...] = val

            @pl.when(jnp.logical_not(first))
            def _(a_ref=a_ref, val=val):
                a_ref[...] += val

    res = pl.pallas_call(
        body, name=name, grid=(rows // tm,), in_specs=in_specs, out_specs=out_specs, out_shape=out_shape,
        compiler_params=_params(("arbitrary",) if accs else ("parallel",)),
    )(*args)
    return res


def _rstd(xf):
    return lax.rsqrt(jnp.mean(xf * xf, axis=-1, keepdims=True) + EPS)


def _rmsnorm(x, g, name, after=()):
    rows, d = x.shape
    return _rowwise(lambda xv, gv: xv * _rstd(xv) * gv, [(x, d, 0), (g, None, None)], [(d, BF)],
                    rows=rows, tm=512, name=name, after=after)[0]


def _rmsnorm_bwd(x, g, dy, resid, name, bf_copy=False):
    rows, d = x.shape

    def fn(xv, gv, dyv, *rest):
        dyv = dyv.astype(F32)
        r = _rstd(xv)
        xn = xv * r
        dxn = dyv * gv
        dx = r * (dxn - xn * jnp.mean(dxn * xn, axis=-1, keepdims=True))
        if rest:
            dx = rest[0] + dx
        return (dx,) * (1 + bf_copy) + (jnp.sum(dyv * xn, axis=0, keepdims=True),)

    ins = [(x, d, 0), (g, None, None), (dy, d, 0)] + ([(resid, d, 0)] if resid is not None else [])
    outs = [(d, F32)] + ([(d, BF)] if bf_copy else [])
    return _rowwise(fn, ins, outs, rows=rows, tm=512, name=name, accs=[(1, d)])


def _matmul_norm_bwd(a, w, x, g, resid, *, tm, bf_copy, name, rider=None):
    m, kdim = a.shape
    d = w.shape[0]
    tm = _tile(m, tm)
    row = lambda i: (i, 0)
    whole = lambda i: (0, 0)
    n_out = 2 + bf_copy

    def body(*refs):
        (a_ref, w_ref, x_ref, g_ref, r_ref), outs = _host_refs(refs, rider, 5, n_out)
        dy = lax.dot_general(a_ref[...], w_ref[...], NT, preferred_element_type=F32)
        xv = x_ref[...]
        r = _rstd(xv)
        xn = xv * r
        dxn = dy * g_ref[...]
        dx = r_ref[...] + r * (dxn - xn * jnp.mean(dxn * xn, axis=-1, keepdims=True))
        outs[0][...] = dx
        if bf_copy:
            outs[1][...] = dx.astype(BF)
        dg = jnp.sum(dy * xn, axis=0, keepdims=True)
        first = pl.program_id(0) == 0

        @pl.when(first)
        def _():
            outs[-1][...] = dg

        @pl.when(jnp.logical_not(first))
        def _():
            outs[-1][...] += dg

        if rider:
            rider.run(refs, 5, n_out, first, pl.program_id(0) == m // tm - 1)

    tok = pl.BlockSpec((tm, d), row)
    out_specs = [tok] + ([tok] if bf_copy else []) + [pl.BlockSpec((1, d), whole)]
    out_shape = ([jax.ShapeDtypeStruct((m, d), F32)] + ([jax.ShapeDtypeStruct((m, d), BF)] if bf_copy else [])
                 + [jax.ShapeDtypeStruct((1, d), F32)])
    return pl.pallas_call(
        body, name=name, grid=(m // tm,),
        compiler_params=_params(("arbitrary",), has_side_effects=rider is not None),
        **_with_rider(
            rider, 5, n_out,
            [pl.BlockSpec((tm, kdim), row), pl.BlockSpec((d, kdim), whole), tok, pl.BlockSpec((1, d), whole), tok],
            out_specs, out_shape),
    )(a, w, x, g, resid, *(rider.arrays if rider else []))


def _norm_ffn_in_swiglu(x, g, w, *, tm=1024, tn=1408):
    m, kdim = x.shape
    f = w.shape[1] // 2
    tm, tn = _tile(m, tm), _tile(f, tn)
    nj = f // tn

    def body(x_ref, g_ref, wg_ref, wu_ref, h_ref, gate_ref, up_ref, act_ref):
        @pl.when(pl.program_id(1) == 0)
        def _():
            xv = x_ref[...]
            h_ref[...] = (xv * _rstd(xv) * g_ref[...]).astype(h_ref.dtype)

        hv = h_ref[...]
        gate = jnp.dot(hv, wg_ref[...], preferred_element_type=F32)
        up = jnp.dot(hv, wu_ref[...], preferred_element_type=F32)
        gate_ref[...] = gate.astype(gate_ref.dtype)
        up_ref[...] = up.astype(up_ref.dtype)
        act_ref[...] = (gate * jax.nn.sigmoid(gate) * up).astype(act_ref.dtype)

    tile = pl.BlockSpec((tm, tn), lambda i, j: (i, j))
    rows = pl.BlockSpec((tm, kdim), lambda i, j: (i, 0))
    return pl.pallas_call(
        body, name="norm_ffn_in_swiglu", grid=(m // tm, nj),
        in_specs=[rows, pl.BlockSpec((1, kdim), lambda i, j: (0, 0)),
                  pl.BlockSpec((kdim, tn), lambda i, j: (0, j)),
                  pl.BlockSpec((kdim, tn), lambda i, j: (0, nj + j))],
        out_specs=[rows, tile, tile, tile],
        out_shape=[jax.ShapeDtypeStruct((m, kdim), BF)] + [jax.ShapeDtypeStruct((m, f), BF)] * 3,
        compiler_params=_params(("parallel", "arbitrary")),
    )(x, g, w, w)


def _norm_matmul(x, g, w, *, tm=1024, tn, name, after=()):
    m, kdim = x.shape
    n = w.shape[1]
    tm, tn = _tile(m, tm), _tile(n, tn)

    def body(x_ref, g_ref, w_ref, *rest):
        h_ref, o_ref = rest[len(after):]

        @pl.when(pl.program_id(1) == 0)
        def _():
            xv = x_ref[...]
            h_ref[...] = (xv * _rstd(xv) * g_ref[...]).astype(h_ref.dtype)

        o_ref[...] = jnp.dot(h_ref[...], w_ref[...], preferred_element_type=F32).astype(o_ref.dtype)

    rows = pl.BlockSpec((tm, kdim), lambda i, j: (i, 0))
    return pl.pallas_call(
        body, name=name, grid=(m // tm, n // tn),
        in_specs=[rows, pl.BlockSpec((1, kdim), lambda i, j: (0, 0)), pl.BlockSpec((kdim, tn), lambda i, j: (0, j))]
        + [HBM_SPEC] * len(after),
        out_specs=[rows, pl.BlockSpec((tm, tn), lambda i, j: (i, j))],
        out_shape=[jax.ShapeDtypeStruct((m, kdim), BF), jax.ShapeDtypeStruct((m, n), BF)],
        compiler_params=_params(("parallel", "arbitrary")),
    )(x, g, w, *after)


def _ffn_out_loss(act, w, resid, g, target, *, tm=512):
    m, f = act.shape
    d = w.shape[1]
    tm = _tile(m, tm)
    row = lambda i: (i, 0)
    whole = lambda i: (0, 0)

    def body(a_ref, w_ref, r_ref, g_ref, t_ref, dx_ref, dxb_ref, dg_ref, loss_ref):
        xv = r_ref[...] + jnp.dot(a_ref[...], w_ref[...], preferred_element_type=F32)
        gv = g_ref[...]
        r = _rstd(xv)
        xn = xv * r
        err = xn * gv - t_ref[...]
        loss = 0.5 * jnp.sum(jnp.mean(err * err, axis=-1, keepdims=True), axis=0, keepdims=True)
        dyv = err * (1.0 / d)
        dxn = dyv * gv
        dx = r * (dxn - xn * jnp.mean(dxn * xn, axis=-1, keepdims=True))
        dx_ref[...] = dx
        dxb_ref[...] = dx.astype(dxb_ref.dtype)
        dg = jnp.sum(dyv * xn, axis=0, keepdims=True)
        loss_b = jnp.broadcast_to(loss, (1, LANES))
        first = pl.program_id(0) == 0

        @pl.when(first)
        def _():
            dg_ref[...] = dg
            loss_ref[...] = loss_b

        @pl.when(jnp.logical_not(first))
        def _():
            dg_ref[...] += dg
            loss_ref[...] += loss_b

    tok = pl.BlockSpec((tm, d), row)
    return pl.pallas_call(
        body, name="ffn_out_loss", grid=(m // tm,),
        in_specs=[pl.BlockSpec((tm, f), row), pl.BlockSpec((f, d), whole), tok, pl.BlockSpec((1, d), whole), tok],
        out_specs=[tok, tok, pl.BlockSpec((1, d), whole), pl.BlockSpec((1, LANES), whole)],
        out_shape=[jax.ShapeDtypeStruct((m, d), F32), jax.ShapeDtypeStruct((m, d), BF),
                   jax.ShapeDtypeStruct((1, d), F32), jax.ShapeDtypeStruct((1, LANES), F32)],
        compiler_params=_params(("arbitrary",)),
    )(act, w, resid, g, target)


def _d_act_swiglu(dx, w, gate, up, *, tm=256):
    m, d = dx.shape
    f = w.shape[0]
    tm = _tile(m, tm)
    row = lambda i: (i, 0)

    def body(dx_ref, w_ref, gate_ref, up_ref, o_ref):
        da = lax.dot_general(dx_ref[...], w_ref[...], NT, preferred_element_type=F32)
        gv, uv = gate_ref[...].astype(F32), up_ref[...].astype(F32)
        sg = jax.nn.sigmoid(gv)
        dgate = da * uv * (sg * (1.0 + gv * (1.0 - sg)))
        o_ref[...] = jnp.concatenate([dgate, da * (gv * sg)], axis=-1).astype(o_ref.dtype)

    return pl.pallas_call(
        body, name="d_act_swiglu", grid=(m // tm,),
        in_specs=[pl.BlockSpec((tm, d), row), pl.BlockSpec((f, d), lambda i: (0, 0)),
                  pl.BlockSpec((tm, f), row), pl.BlockSpec((tm, f), row)],
        out_specs=pl.BlockSpec((tm, 2 * f), row), out_shape=jax.ShapeDtypeStruct((m, 2 * f), BF),
        compiler_params=_params(("parallel",)),
    )(dx, w, gate, up)


GATE_A_BLK, GATE_B_BLK = 3, 4


def _branches_merge(o_a, y_b, w_a, w_b, proj, *, tm=1024):
    m, kdim = o_a.shape
    d = w_a.shape[1]
    tm = _tile(m, tm)
    row = lambda i: (i, 0)

    def body(a_ref, b_ref, wa_ref, wb_ref, ga_ref, gb_ref, bra_ref, brb_ref, merged_ref):
        bra = jnp.dot(a_ref[...], wa_ref[...], preferred_element_type=F32)
        brb = jnp.dot(b_ref[...], wb_ref[...], preferred_element_type=F32)
        bra_ref[...] = bra.astype(bra_ref.dtype)
        brb_ref[...] = brb.astype(brb_ref.dtype)
        merged = jax.nn.sigmoid(ga_ref[...].astype(F32)) * bra + jax.nn.sigmoid(gb_ref[...].astype(F32)) * brb
        merged_ref[...] = merged.astype(merged_ref.dtype)

    tok = pl.BlockSpec((tm, d), row)
    return pl.pallas_call(
        body, name="branches_merge", grid=(m // tm,),
        in_specs=[pl.BlockSpec((tm, kdim), row), pl.BlockSpec((tm, kdim), row),
                  pl.BlockSpec((kdim, d), lambda i: (0, 0)), pl.BlockSpec((kdim, d), lambda i: (0, 0)),
                  pl.BlockSpec((tm, d), lambda i: (i, GATE_A_BLK)), pl.BlockSpec((tm, d), lambda i: (i, GATE_B_BLK))],
        out_specs=[tok, tok, tok], out_shape=[jax.ShapeDtypeStruct((m, d), BF)] * 3,
        compiler_params=_params(("parallel",)),
    )(o_a, y_b, w_a, w_b, proj, proj)


def _d_merged_gates(dx, w_mix, proj, br_a, br_b, *, tm=512):
    m, d = dx.shape
    tm = _tile(m, tm)
    row = lambda i: (i, 0)

    def body(dx_ref, w_ref, ga_ref, gb_ref, bra_ref, brb_ref, dbra_ref, dbrb_ref, dga_ref, dgb_ref):
        dm = lax.dot_general(dx_ref[...], w_ref[...], NT, preferred_element_type=F32)
        sa, sb = jax.nn.sigmoid(ga_ref[...].astype(F32)), jax.nn.sigmoid(gb_ref[...].astype(F32))
        dbra_ref[...] = (dm * sa).astype(dbra_ref.dtype)
        dbrb_ref[...] = (dm * sb).astype(dbrb_ref.dtype)
        dga_ref[...] = (dm * bra_ref[...].astype(F32) * (sa * (1.0 - sa))).astype(dga_ref.dtype)
        dgb_ref[...] = (dm * brb_ref[...].astype(F32) * (sb * (1.0 - sb))).astype(dgb_ref.dtype)

    tok = pl.BlockSpec((tm, d), row)
    return pl.pallas_call(
        body, name="d_merged_gates", grid=(m // tm,),
        in_specs=[tok, pl.BlockSpec((d, d), lambda i: (0, 0)),
                  pl.BlockSpec((tm, d), lambda i: (i, GATE_A_BLK)), pl.BlockSpec((tm, d), lambda i: (i, GATE_B_BLK)),
                  tok, tok],
        out_specs=[tok] * 4, out_shape=[jax.ShapeDtypeStruct((m, d), BF)] * 4,
        compiler_params=_params(("parallel",)),
    )(dx, w_mix, proj, proj, br_a, br_b)


def _loss_head(x, g, target, name):
    rows, d = x.shape

    def fn(xv, gv, tv):
        r = _rstd(xv)
        xn = xv * r
        err = xn * gv - tv
        per_tok = jnp.mean(err * err, axis=-1, keepdims=True)
        loss = 0.5 * jnp.sum(per_tok, axis=0, keepdims=True)
        dyv = err * (1.0 / d)
        dxn = dyv * gv
        dx = r * (dxn - xn * jnp.mean(dxn * xn, axis=-1, keepdims=True))
        return dx, dx, jnp.sum(dyv * xn, axis=0, keepdims=True), jnp.broadcast_to(loss, (1, LANES))

    return _rowwise(fn, [(x, d, 0), (g, None, None), (target, d, 0)], [(d, F32), (d, BF)],
                    rows=rows, tm=512, name=name, accs=[(1, d), (1, LANES)])


SB_TK = 128
SB_KT = 4


def _sb_consts(tq):
    tk = SB_TK
    diff = lax.broadcasted_iota(jnp.int32, (tq, tk), 1) - lax.broadcasted_iota(jnp.int32, (tq, tk), 0)
    rj = lax.broadcasted_iota(jnp.int32, (2 * tk, 2 * tk), 0) & (tk - 1)
    cj = lax.broadcasted_iota(jnp.int32, (2 * tk, 2 * tk), 1)
    ones_half = cj >= tk
    later = jnp.where((rj > cj) | ones_half, 1.0, 0.0).astype(BF)
    later_incl = jnp.where((rj >= cj) | ones_half, 1.0, 0.0).astype(BF)
    return diff, later, later_incl


def _split_dot(val, rhs_twice):
    hi = val.astype(BF)
    lo = (val - hi.astype(F32)).astype(BF)
    return jnp.dot(jnp.concatenate([hi, lo], axis=1), rhs_twice, preferred_element_type=F32)


def _log_terms(z):
    sp = jnp.maximum(z, 0.0) + jnp.log(1.0 + jnp.exp(-jnp.abs(z)))
    return z - sp, sp


NT = (((1,), (1,)), ((), ()))
TN = (((0,), (0,)), ((), ()))


def _head_lane_masks(rows):
    lane = lax.broadcasted_iota(jnp.int32, (rows, LANES), 1)
    first = jnp.where(lane < SB_HEAD_DIM, 1.0, 0.0)
    return first.astype(BF), (1.0 - first).astype(BF)


DEAD_LOG = 104.0


def _walk_back(i, step, state, carries_of):
    def alive(st):
        c0, c1 = carries_of(st)
        return jnp.min(jnp.minimum(c0, c1)) < DEAD_LOG

    def cond(loop):
        done, live, _ = loop
        return jnp.logical_and(done < i, live)

    def body(loop):
        done, _, st = loop
        st = step(i - 1 - done, st)
        return done + 1, alive(st), st

    return lax.while_loop(cond, body, (jnp.int32(0), alive(state), state))[2]


def _tail(a, r0):
    return a if r0 == 0 else a[r0:]


def _add_tail(a, r0, delta):
    return a + delta if r0 == 0 else jnp.concatenate([a[:r0], a[r0:] + delta], axis=0)


def _both_heads(tile, masks):
    return jnp.concatenate([tile * masks[0], tile * masks[1]], axis=0)


def _with_rider(rider, n_in, n_out, in_specs, out_specs, out_shape, scratch=()):
    kw = dict(in_specs=list(in_specs), out_specs=list(out_specs), out_shape=list(out_shape),
              scratch_shapes=list(scratch), input_output_aliases={})
    if rider:
        extra = rider.call_args(n_in, n_out)
        kw["in_specs"] += extra["in_specs"]
        kw["out_specs"] += extra["out_specs"]
        kw["out_shape"] += extra["out_shape"]
        kw["scratch_shapes"] += extra["scratch"]
        kw["input_output_aliases"] = extra["aliases"]
    return kw


def _sb_fwd(proj, rider=None):
    s = proj.shape[0]
    tk, tq = SB_TK, SB_KT * SB_TK
    n_pairs = 4
    scale = 1.0 / math.sqrt(SB_HEAD_DIM)

    def body(*refs):
        (q_ref, k_ref, v_ref), (o_ref, o32_ref) = _host_refs(refs, rider, 3, 2)
        i = pl.program_id(1)
        diff, later, _ = _sb_consts(tq)
        qs = (q_ref[...].astype(F32) * scale).astype(BF)
        lane_masks = _head_lane_masks(tk)

        def step(g, state, masked):
            tiles = list(reversed(range(SB_KT)))
            chains = [(t, h) for t in tiles for h in range(2)]
            rows = {t: pl.ds(pl.multiple_of((g * SB_KT + t) * tk, tk), tk) for t in tiles}
            ks = {t: _both_heads(k_ref[rows[t], :], lane_masks) for t in tiles}
            vs = {t: _both_heads(v_ref[rows[t], :], lane_masks) for t in tiles}
            r0 = {t: t * tk if masked else 0 for t in tiles}
            allowed = {t: _tail(diff, r0[t]) < -t * tk for t in tiles}
            zs = {t: lax.dot_general(_tail(qs, r0[t]), ks[t], NT, preferred_element_type=F32) for t in tiles}
            logs = {}
            for t, h in chains:
                log_b, sp = _log_terms(zs[t][:, h * tk:(h + 1) * tk])
                logs[t, h] = (log_b, jnp.where(allowed[t], sp, 0.0) if masked else sp)
            sums = {c: _split_dot(logs[c][1], later) for c in chains}
            carries = list(state[0])
            ws = {}
            for t, h in chains:
                w = jnp.exp(logs[t, h][0] - (sums[t, h][:, :tk] + _tail(carries[h], r0[t])))
                ws[t, h] = (jnp.where(allowed[t], w, 0.0) if masked else w).astype(BF)
                carries[h] = _add_tail(carries[h], r0[t], sums[t, h][:, tk:])
            acc = state[1]
            for t in tiles:
                acc = _add_tail(acc, r0[t], jnp.dot(jnp.concatenate([ws[t, 0], ws[t, 1]], axis=1), vs[t],
                                                    preferred_element_type=F32))
            return tuple(carries), acc

        zero = jnp.zeros((tq, LANES), F32)
        state = step(i, ((zero, zero), zero), True)
        state = _walk_back(i, lambda g, st: step(g, st, False), state, lambda st: st[0])
        o_ref[...] = state[1].astype(o_ref.dtype)
        o32_ref[...] = state[1]
        if rider:
            rider.run(refs, 3, 2, (pl.program_id(0) == 0) & (i == 0),
                      (pl.program_id(0) == n_pairs - 1) & (i == s // tq - 1))

    tok = pl.BlockSpec((tq, LANES), lambda p, i: (i, p))
    return pl.pallas_call(
        body, name="sb_attn_fwd", grid=(n_pairs, s // tq),
        compiler_params=_params(("arbitrary", "arbitrary"), has_side_effects=rider is not None),
        **_with_rider(
            rider, 3, 2,
            [tok, pl.BlockSpec((s, LANES), lambda p, i: (0, n_pairs + p)),
             pl.BlockSpec((s, LANES), lambda p, i: (0, 2 * n_pairs + p))],
            [tok, tok],
            [jax.ShapeDtypeStruct((s, n_pairs * LANES), BF), jax.ShapeDtypeStruct((s, n_pairs * LANES), F32)]),
    )(proj, proj, proj, *(rider.arrays if rider else []))


def _sb_bwd(proj, o32, do_a, rider=None):
    s = proj.shape[0]
    tk, tq = SB_TK, SB_KT * SB_TK
    n_pairs = 4
    scale = 1.0 / math.sqrt(SB_HEAD_DIM)

    def body(*refs):
        (q_ref, k_ref, v_ref, o_ref, do_ref), (dq_ref, dk_ref, dv_ref) = _host_refs(refs, rider, 5, 3)
        i = pl.program_id(1)

        @pl.when(i == 0)
        def _():
            dk_ref[...] = jnp.zeros_like(dk_ref)
            dv_ref[...] = jnp.zeros_like(dv_ref)

        diff, later, later_incl = _sb_consts(tq)
        qs = (q_ref[...].astype(F32) * scale).astype(BF)
        do2 = do_ref[...]
        prod = do2.astype(F32) * o_ref[...]
        lane_masks = _head_lane_masks(tk)
        first_head = lax.broadcasted_iota(jnp.int32, (tq, LANES), 1) < SB_HEAD_DIM
        totals = [jnp.broadcast_to(jnp.sum(jnp.where(keep, prod, 0.0), axis=-1, keepdims=True), (tq, tk))
                  for keep in (first_head, jnp.logical_not(first_head))]
        first_head_k = first_head[:tk]

        def step(g_idx, state, masked):
            tiles = list(reversed(range(SB_KT)))
            chains = [(t, h) for t in tiles for h in range(2)]
            rows = {t: pl.ds(pl.multiple_of((g_idx * SB_KT + t) * tk, tk), tk) for t in tiles}
            ks = {t: _both_heads(k_ref[rows[t], :], lane_masks) for t in tiles}
            vs = {t: _both_heads(v_ref[rows[t], :], lane_masks) for t in tiles}
            r0 = {t: t * tk if masked else 0 for t in tiles}
            allowed = {t: _tail(diff, r0[t]) < -t * tk for t in tiles}
            zs = {t: lax.dot_general(_tail(qs, r0[t]), ks[t], NT, preferred_element_type=F32) for t in tiles}
            dws = {t: lax.dot_general(_tail(do2, r0[t]), vs[t], NT, preferred_element_type=F32) for t in tiles}
            logs = {}
            for t, h in chains:
                log_b, sp = _log_terms(zs[t][:, h * tk:(h + 1) * tk])
                logs[t, h] = (log_b, jnp.where(allowed[t], sp, 0.0) if masked else sp)
            sums = {c: _split_dot(logs[c][1], later) for c in chains}
            c_log, c_g = list(state[0]), list(state[1])
            ws, gs = {}, {}
            for t, h in chains:
                w = jnp.exp(logs[t, h][0] - (sums[t, h][:, :tk] + _tail(c_log[h], r0[t])))
                ws[t, h] = (jnp.where(allowed[t], w, 0.0) if masked else w).astype(BF)
                c_log[h] = _add_tail(c_log[h], r0[t], sums[t, h][:, tk:])
                gs[t, h] = ws[t, h].astype(F32) * dws[t][:, h * tk:(h + 1) * tk]
            gsums = {c: _split_dot(gs[c], later_incl) for c in chains}
            dzs = {}
            for t, h in chains:
                beta = jnp.exp(logs[t, h][0])
                earlier = _tail(totals[h], r0[t]) - (gsums[t, h][:, :tk] + _tail(c_g[h], r0[t]))
                dz = gs[t, h] * (1.0 - beta) - earlier * beta
                dzs[t, h] = (jnp.where(allowed[t], dz, 0.0) if masked else dz).astype(BF)
                c_g[h] = _add_tail(c_g[h], r0[t], gsums[t, h][:, tk:])
            dq = state[2]
            for t in tiles:
                dz_both = jnp.concatenate([dzs[t, 0], dzs[t, 1]], axis=1)
                w_both = jnp.concatenate([ws[t, 0], ws[t, 1]], axis=1)
                dq = _add_tail(dq, r0[t], jnp.dot(dz_both, ks[t], preferred_element_type=F32))
                dk2 = lax.dot_general(dz_both, _tail(qs, r0[t]), TN, preferred_element_type=F32)
                dv2 = lax.dot_general(w_both, _tail(do2, r0[t]), TN, preferred_element_type=F32)
                dk_ref[rows[t], :] += jnp.where(first_head_k, dk2[:tk], dk2[tk:])
                dv_ref[rows[t], :] += jnp.where(first_head_k, dv2[:tk], dv2[tk:])
            return tuple(c_log), tuple(c_g), dq

        zero = jnp.zeros((tq, LANES), F32)
        state = step(i, ((zero, zero), (zero, zero), zero), True)
        state = _walk_back(i, lambda g, st: step(g, st, False), state, lambda st: st[0])
        dq_ref[...] = (state[2] * scale).astype(dq_ref.dtype)
        if rider:
            rider.run(refs, 5, 3, (pl.program_id(0) == 0) & (i == 0),
                      (pl.program_id(0) == n_pairs - 1) & (i == s // tq - 1))

    width = n_pairs * LANES
    tok = pl.BlockSpec((tq, LANES), lambda p, i: (i, p))
    return pl.pallas_call(
        body, name="sb_attn_bwd", grid=(n_pairs, s // tq),
        compiler_params=_params(("arbitrary", "arbitrary"), has_side_effects=rider is not None),
        **_with_rider(
            rider, 5, 3,
            [tok, pl.BlockSpec((s, LANES), lambda p, i: (0, n_pairs + p)),
             pl.BlockSpec((s, LANES), lambda p, i: (0, 2 * n_pairs + p)), tok, tok],
            [tok, pl.BlockSpec((s, LANES), lambda p, i: (0, p)), pl.BlockSpec((s, LANES), lambda p, i: (0, p))],
            [jax.ShapeDtypeStruct((s, width), BF), jax.ShapeDtypeStruct((s, width), F32),
             jax.ShapeDtypeStruct((s, width), F32)]),
    )(proj, proj, proj, o32, do_a, *(rider.arrays if rider else []))


CONV_COL0 = 12


def _shift_rows(v, k):
    n = v.shape[0]
    row = lax.broadcasted_iota(jnp.int32, v.shape, 0)
    rolled = pltpu.roll(v, k % n, axis=0)
    keep = row >= k if k > 0 else row < n + k
    return jnp.where(keep, rolled, 0.0)


def _conv_specs(s):
    return [pl.BlockSpec((s, LANES), lambda cb: (0, CONV_COL0 + cb)),
            pl.BlockSpec((s, LANES), lambda cb: (0, CONV_COL0 + 4 + cb)),
            pl.BlockSpec((s, LANES), lambda cb: (0, CONV_COL0 + 8 + cb)),
            pl.BlockSpec((CONV_ROWS, LANES), lambda cb: (0, cb))]


def _conv_fwd(proj, conv_w):
    s = proj.shape[0]

    def body(u_ref, gb_ref, gc_ref, w_ref, y_ref):
        cu = gc_ref[...].astype(F32) * u_ref[...].astype(F32)
        w = w_ref[...]
        y = w[0:1] * _shift_rows(cu, 2) + w[1:2] * _shift_rows(cu, 1) + w[2:3] * cu
        y_ref[...] = (gb_ref[...].astype(F32) * y).astype(y_ref.dtype)

    return pl.pallas_call(
        body, name="conv_fwd", grid=(4,), in_specs=_conv_specs(s),
        out_specs=pl.BlockSpec((s, LANES), lambda cb: (0, cb)),
        out_shape=jax.ShapeDtypeStruct((s, 4 * LANES), BF),
        compiler_params=_params(("parallel",)),
    )(proj, proj, proj, conv_w)


def _conv_bwd(proj, conv_w, dy):
    s = proj.shape[0]

    def body(u_ref, gb_ref, gc_ref, w_ref, dy_ref, du_ref, dgb_ref, dgc_ref, dw_ref):
        u, gc = u_ref[...].astype(F32), gc_ref[...].astype(F32)
        dyv = dy_ref[...].astype(F32)
        w = w_ref[...]
        cu = gc * u
        cu1, cu2 = _shift_rows(cu, 1), _shift_rows(cu, 2)
        conv = w[0:1] * cu2 + w[1:2] * cu1 + w[2:3] * cu
        dgb_ref[...] = (dyv * conv).astype(dgb_ref.dtype)
        dc = dyv * gb_ref[...].astype(F32)
        dcu = w[2:3] * dc + w[1:2] * _shift_rows(dc, -1) + w[0:1] * _shift_rows(dc, -2)
        dgc_ref[...] = (dcu * u).astype(dgc_ref.dtype)
        du_ref[...] = (dcu * gc).astype(du_ref.dtype)
        tap_row = lax.broadcasted_iota(jnp.int32, (CONV_ROWS, LANES), 0)
        dw = jnp.zeros((CONV_ROWS, LANES), F32)
        for t, shifted in enumerate((cu2, cu1, cu)):
            dw = jnp.where(tap_row == t, jnp.sum(dc * shifted, axis=0, keepdims=True), dw)
        dw_ref[...] = dw

    col = pl.BlockSpec((s, LANES), lambda cb: (0, cb))
    act = jax.ShapeDtypeStruct((s, 4 * LANES), BF)
    return pl.pallas_call(
        body, name="conv_bwd", grid=(4,), in_specs=_conv_specs(s) + [col],
        out_specs=[col, col, col, pl.BlockSpec((CONV_ROWS, LANES), lambda cb: (0, cb))],
        out_shape=[act, act, act, jax.ShapeDtypeStruct((CONV_ROWS, 4 * LANES), F32)],
        compiler_params=_params(("parallel",)),
    )(proj, proj, proj, conv_w, dy)


def _mem_probs(q, k, scale):
    sc = lax.dot_general(q, k, NT, preferred_element_type=F32) * scale
    p = jnp.exp(sc - jnp.max(sc, axis=-1, keepdims=True))
    return p / jnp.sum(p, axis=-1, keepdims=True)


def _mem_fwd(q_m, kv, tq=2048):
    s, d = q_m.shape
    mlen = kv.shape[0]
    hd = d // MEM_HEADS
    tq = _tile(s, tq)
    scale = 1.0 / math.sqrt(hd)

    def body(q_ref, k_ref, v_ref, o_ref):
        p = _mem_probs(q_ref[...], k_ref[...], scale)
        o_ref[...] = jnp.dot(p.astype(BF), v_ref[...], preferred_element_type=F32).astype(o_ref.dtype)

    return pl.pallas_call(
        body, name="mem_attn_fwd", grid=(MEM_HEADS, s // tq),
        in_specs=[pl.BlockSpec((tq, hd), lambda h, i: (i, h)),
                  pl.BlockSpec((mlen, hd), lambda h, i: (0, h)),
                  pl.BlockSpec((mlen, hd), lambda h, i: (0, MEM_HEADS + h))],
        out_specs=pl.BlockSpec((tq, hd), lambda h, i: (i, h)),
        out_shape=jax.ShapeDtypeStruct((s, d), BF),
        compiler_params=_params(("parallel", "parallel")),
    )(q_m, kv, kv)


def _mem_bwd(q_m, kv, do_m, tq=2048):
    s, d = q_m.shape
    mlen = kv.shape[0]
    hd = d // MEM_HEADS
    tq = _tile(s, tq)
    scale = 1.0 / math.sqrt(hd)

    def body(q_ref, k_ref, v_ref, do_ref, dq_ref, dk_ref, dv_ref):
        q, k, v, do = q_ref[...], k_ref[...], v_ref[...], do_ref[...]
        p = _mem_probs(q, k, scale)
        dp = lax.dot_general(do, v, NT, preferred_element_type=F32)
        ds = p * (dp - jnp.sum(dp * p, axis=-1, keepdims=True)) * scale
        dsb = ds.astype(BF)
        dq_ref[...] = jnp.dot(dsb, k, preferred_element_type=F32).astype(dq_ref.dtype)
        dk = lax.dot_general(dsb, q, TN, preferred_element_type=F32)
        dv = lax.dot_general(p.astype(BF), do, TN, preferred_element_type=F32)
        first = pl.program_id(1) == 0

        @pl.when(first)
        def _():
            dk_ref[...] = dk
            dv_ref[...] = dv

        @pl.when(jnp.logical_not(first))
        def _():
            dk_ref[...] += dk
            dv_ref[...] += dv

    tok = pl.BlockSpec((tq, hd), lambda h, i: (i, h))
    memb = pl.BlockSpec((mlen, hd), lambda h, i: (0, h))
    return pl.pallas_call(
        body, name="mem_attn_bwd", grid=(MEM_HEADS, s // tq),
        in_specs=[tok, memb, pl.BlockSpec((mlen, hd), lambda h, i: (0, MEM_HEADS + h)), tok],
        out_specs=[tok, memb, memb],
        out_shape=[jax.ShapeDtypeStruct((s, d), BF), jax.ShapeDtypeStruct((mlen, d), F32),
                   jax.ShapeDtypeStruct((mlen, d), F32)],
        compiler_params=_params(("parallel", "arbitrary")),
    )(q_m, kv, kv, do_m)


def _place():
    x, y, c = lax.axis_index("x"), lax.axis_index("y"), lax.axis_index("c")
    other_chips = [(1 - x, y), (x, 1 - y), (1 - x, 1 - y)]
    return x, y, c, other_chips


def _chip_no(cx, cy):
    return 2 * cx + cy


HBM_SPEC = pl.BlockSpec(memory_space=pl.ANY)


def _cast_place(shard, axis, place, dtype, name, after=()):
    r, c = shard.shape
    tr = _tile(r, max(16, 1048576 // c))
    nblk = r // tr
    if axis == 1:
        full, out_map = (r, N_CHIPS * c), lambda i, pref: (i, pref[0])
    else:
        full, out_map = (N_CHIPS * r, c), lambda i, pref: (pref[0] * nblk + i, 0)

    def body(pref, s_ref, *rest):
        o_ref = rest[-1]
        o_ref[...] = s_ref[...].astype(o_ref.dtype)

    return pl.pallas_call(
        body, name=name,
        grid_spec=pltpu.PrefetchScalarGridSpec(
            num_scalar_prefetch=1, grid=(nblk,),
            in_specs=[pl.BlockSpec((tr, c), lambda i, pref: (i, 0))] + [HBM_SPEC] * len(after),
            out_specs=pl.BlockSpec((tr, c), out_map)),
        out_shape=jax.ShapeDtypeStruct(full, dtype),
        compiler_params=_params(("parallel",)),
    )(place, shard, *after)


def _region(ref, axis, chip_no, half):
    width = ref.shape[axis] // N_CHIPS
    start = pl.multiple_of(chip_no * width, width)
    if axis == 1:
        if half is None:
            return ref.at[:, pl.ds(start, width)]
        hr = ref.shape[0] // 2
        return ref.at[pl.ds(pl.multiple_of(half * hr, hr), hr), pl.ds(start, width)]
    if half is None:
        return ref.at[pl.ds(start, width), :]
    hr = width // 2
    return ref.at[pl.ds(pl.multiple_of(start + half * hr, hr), hr), :]


def _gather_weights(fulls, axes, split):
    n = len(fulls)

    def body(*refs):
        outs = refs[n:2 * n]
        send, recv, fsend, frecv = refs[2 * n:]
        x, y, c, others = _place()
        me = _chip_no(x, y)
        sibling = (x, y, 1 - c)

        def copy(w, chip_no, half, sems, p, to):
            reg = _region(outs[w], axes[w], chip_no, half)
            return pltpu.make_async_remote_copy(
                src_ref=reg, dst_ref=reg, send_sem=sems[0].at[w, p], recv_sem=sems[1].at[w, p],
                device_id=to, device_id_type=MESH)

        for w in range(n):
            for p, chip in enumerate(others):
                copy(w, me, c if split[w] else None, (send, recv), p, (chip[0], chip[1], c)).start()
        for w in range(n):
            for p, chip in enumerate(others):
                half = c if split[w] else None
                copy(w, _chip_no(*chip), half, (send, recv), p, (chip[0], chip[1], c)).wait_recv()
                if split[w]:
                    copy(w, _chip_no(*chip), c, (fsend, frecv), p, sibling).start()
        for w in range(n):
            for p, chip in enumerate(others):
                copy(w, me, c if split[w] else None, (send, recv), p, (chip[0], chip[1], c)).wait_send()
                if split[w]:
                    handed = copy(w, _chip_no(*chip), 1 - c, (fsend, frecv), p, sibling)
                    handed.wait_recv()
                    handed.wait_send()

    return pl.pallas_call(
        body, name="gather_weights",
        in_specs=[HBM_SPEC] * n, out_specs=[HBM_SPEC] * n,
        out_shape=[jax.ShapeDtypeStruct(f.shape, f.dtype) for f in fulls],
        input_output_aliases={i: i for i in range(n)},
        scratch_shapes=[pltpu.SemaphoreType.DMA((n, 3))] * 4,
        compiler_params=pltpu.CompilerParams(has_side_effects=True),
    )(*fulls)


def _fetch_copy(refs, axes, whole, send, recv, w, p, chip, c, arriving):
    owner = _chip_no(*chip) if arriving else _chip_no(lax.axis_index("x"), lax.axis_index("y"))
    reg = _region(refs[w], axes[w], owner, None if whole[w] else c)
    return pltpu.make_async_remote_copy(
        src_ref=reg, dst_ref=reg, send_sem=send[p], recv_sem=recv[p],
        device_id=(chip[0], chip[1], c), device_id_type=MESH)


N_PEERS = N_CHIPS - 1


class _Rider:
    def __init__(self, arrays, outs, aliases, start, wait):
        self.arrays, self.outs, self.aliases, self.start, self.wait = list(arrays), list(outs), aliases, start, wait
        self.scratch = [pltpu.SemaphoreType.DMA((len(self.arrays), N_PEERS))] * 2

    def run(self, refs, n_in, n_out, first, last):
        ra, ro = len(self.arrays), len(self.outs)
        ins = refs[n_in:n_in + ra]
        outs = refs[n_in + ra + n_out:n_in + ra + n_out + ro]
        send, recv = refs[-2], refs[-1]

        @pl.when(first)
        def _():
            self.start(ins, outs, send, recv)

        @pl.when(last)
        def _():
            self.wait(ins, outs, send, recv)

    def call_args(self, n_in, n_out):
        ra = len(self.arrays)
        return dict(in_specs=[HBM_SPEC] * ra, out_specs=[HBM_SPEC] * len(self.outs), out_shape=self.outs,
                    aliases={n_in + k: n_out + o for k, o in self.aliases.items()}, scratch=self.scratch)


def _host_refs(refs, rider, n_in, n_out):
    ra = len(rider.arrays) if rider else 0
    return refs[:n_in], refs[n_in + ra:n_in + ra + n_out]


def _riding_fetch(fulls, axes):
    n = len(fulls)
    whole = [False] * n

    def sems(ref, w):
        return [ref.at[w, q] for q in range(N_PEERS)]

    def start(ins, outs, send, recv):
        _, _, c, others = _place()
        for w in range(n):
            for p, chip in enumerate(others):
                _fetch_copy(outs, axes, whole, sems(send, w), sems(recv, w), w, p, chip, c, False).start()

    def wait(ins, outs, send, recv):
        _, _, c, others = _place()
        for w in range(n):
            for p, chip in enumerate(others):
                _fetch_copy(outs, axes, whole, sems(send, w), sems(recv, w), w, p, chip, c, False).wait_send()
                _fetch_copy(outs, axes, whole, sems(send, w), sems(recv, w), w, p, chip, c, True).wait_recv()

    return _Rider(fulls, [jax.ShapeDtypeStruct(f.shape, f.dtype) for f in fulls], {k: k for k in range(n)}, start, wait)


def _hand_on(fulls, axes, name):
    n = len(fulls)

    def body(*refs):
        outs = refs[n:2 * n]
        send, recv = refs[2 * n:]
        x, y, c, others = _place()

        def copy(w, p, chip, half):
            reg = _region(outs[w], axes[w], _chip_no(*chip), half)
            return pltpu.make_async_remote_copy(
                src_ref=reg, dst_ref=reg, send_sem=send.at[w, p], recv_sem=recv.at[w, p],
                device_id=(x, y, 1 - c), device_id_type=MESH)

        for w in range(n):
            for p, chip in enumerate(others):
                copy(w, p, chip, c).start()
        for w in range(n):
            for p, chip in enumerate(others):
                copy(w, p, chip, 1 - c).wait()

    return pl.pallas_call(
        body, name=name,
        in_specs=[HBM_SPEC] * n, out_specs=[HBM_SPEC] * n,
        out_shape=[jax.ShapeDtypeStruct(f.shape, f.dtype) for f in fulls],
        input_output_aliases={i: i for i in range(n)},
        scratch_shapes=[pltpu.SemaphoreType.DMA((n, 3)), pltpu.SemaphoreType.DMA((n, 3))],
        compiler_params=pltpu.CompilerParams(has_side_effects=True),
    )(*fulls)


def _pair_exchange(grads, name):
    n = len(grads)

    def body(*refs):
        ins, outs = refs[:n], refs[n:2 * n]
        send, recv = refs[2 * n:]
        x, y, c, _ = _place()
        cps = []
        for w in range(n):
            cp = pltpu.make_async_remote_copy(
                src_ref=ins[w].at[:, 1 - c], dst_ref=outs[w], send_sem=send.at[w], recv_sem=recv.at[w],
                device_id=(x, y, 1 - c), device_id_type=MESH)
            cp.start()
            cps.append(cp)
        for cp in cps:
            cp.wait()

    return pl.pallas_call(
        body, name=name,
        in_specs=[HBM_SPEC] * n, out_specs=[HBM_SPEC] * n,
        out_shape=[jax.ShapeDtypeStruct((g.shape[0],) + g.shape[2:], g.dtype) for g in grads],
        scratch_shapes=[pltpu.SemaphoreType.DMA((n,)), pltpu.SemaphoreType.DMA((n,))],
        compiler_params=pltpu.CompilerParams(has_side_effects=True),
    )(*grads)


def _pair_add(g4, got, core, name):
    nj, _, hr, cdim = g4.shape
    tr = _tile(hr, max(8, 524288 // cdim))

    def body(core_ref, a_ref, b_ref, o_ref):
        o_ref[...] = (a_ref[...].astype(F32) + b_ref[...].astype(F32)).astype(o_ref.dtype)

    return pl.pallas_call(
        body, name=name,
        grid_spec=pltpu.PrefetchScalarGridSpec(
            num_scalar_prefetch=1, grid=(nj, hr // tr),
            in_specs=[pl.BlockSpec((1, None, tr, cdim), lambda j, i, core_ref: (j, core_ref[0], i, 0)),
                      pl.BlockSpec((1, tr, cdim), lambda j, i, core_ref: (j, i, 0))],
            out_specs=pl.BlockSpec((1, tr, cdim), lambda j, i, core_ref: (j, i, 0))),
        out_shape=jax.ShapeDtypeStruct((nj, hr, cdim), BF),
        compiler_params=_params(("parallel", "parallel")),
    )(core, g4, got)


def _piece(ref, axis, j, hc):
    if axis == 0:
        return ref.at[j]
    return ref.at[0, :, pl.ds(pl.multiple_of(j * hc, hc), hc)]


def _slot_shapes(sums, axes):
    return [(N_CHIPS - 1, sm.shape[1], sm.shape[2] // (1 if ax == 0 else N_CHIPS)) for sm, ax in zip(sums, axes)]


def _slot_copy(sums, lands, axes, send, recv, w, p, chip, c):
    return pltpu.make_async_remote_copy(
        src_ref=_piece(sums[w], axes[w], _chip_no(*chip), lands[w].shape[2]), dst_ref=lands[w].at[p],
        send_sem=send[p], recv_sem=recv[p],
        device_id=(chip[0], chip[1], c), device_id_type=MESH)


def _riding_slots(sums, axes):
    n = len(sums)
    shapes = _slot_shapes(sums, axes)

    def copies(ins, outs, send, recv):
        _, _, c, others = _place()
        return [_slot_copy(ins, outs, axes, [send.at[w, q] for q in range(N_PEERS)],
                           [recv.at[w, q] for q in range(N_PEERS)], w, p, chip, c)
                for w in range(n) for p, chip in enumerate(others)]

    def start(ins, outs, send, recv):
        for cp in copies(ins, outs, send, recv):
            cp.start()

    def wait(ins, outs, send, recv):
        for cp in copies(ins, outs, send, recv):
            cp.wait()

    return _Rider(sums, [jax.ShapeDtypeStruct(sh, sm.dtype) for sh, sm in zip(shapes, sums)], {}, start, wait)


def _chip_sum(psum, slots, axis, place, name):
    _, hr, hc = slots.shape
    tr = _tile(hr, 256)
    own_map = (lambda i, pref: (0, i, pref[0])) if axis == 1 else (lambda i, pref: (pref[0], i, 0))

    def body(pref, own_ref, s_ref, o_ref):
        o_ref[...] = ((own_ref[...].astype(F32) + s_ref[0].astype(F32)) + s_ref[1].astype(F32)) + s_ref[2].astype(F32)

    return pl.pallas_call(
        body, name=name,
        grid_spec=pltpu.PrefetchScalarGridSpec(
            num_scalar_prefetch=1, grid=(hr // tr,),
            in_specs=[pl.BlockSpec((None, tr, hc), own_map),
                      pl.BlockSpec((N_CHIPS - 1, tr, hc), lambda i, pref: (0, i, 0))],
            out_specs=pl.BlockSpec((None, tr, hc), lambda i, pref: (pref[1], i, 0))),
        out_shape=jax.ShapeDtypeStruct((2, hr, hc), F32),
        compiler_params=_params(("parallel",)),
    )(place, psum, slots)


def _half_swap(both):
    n = len(both)

    def body(*refs):
        outs = refs[n:2 * n]
        send, recv = refs[2 * n:]
        x, y, c, _ = _place()

        def copy(w, half):
            return pltpu.make_async_remote_copy(
                src_ref=outs[w].at[half], dst_ref=outs[w].at[half], send_sem=send.at[w], recv_sem=recv.at[w],
                device_id=(x, y, 1 - c), device_id_type=MESH)

        for w in range(n):
            copy(w, c).start()
        for w in range(n):
            copy(w, 1 - c).wait()

    return pl.pallas_call(
        body, name="grad_half_swap",
        in_specs=[HBM_SPEC] * n, out_specs=[HBM_SPEC] * n,
        out_shape=[jax.ShapeDtypeStruct(b.shape, b.dtype) for b in both],
        input_output_aliases={i: i for i in range(n)},
        scratch_shapes=[pltpu.SemaphoreType.DMA((n,)), pltpu.SemaphoreType.DMA((n,))],
        compiler_params=pltpu.CompilerParams(has_side_effects=True),
    )(*both)


def _allreduce_small(pack):
    rows, d = pack.shape

    def body(p_ref, o_ref, slots, send, recv):
        x, y, c, _ = _place()
        me = 4 * x + 2 * y + c
        slots[me] = p_ref[...]
        cps = []
        for k in range(1, N_DEV):
            px, py, pc = x ^ (k >> 2), y ^ ((k >> 1) & 1), c ^ (k & 1)
            cp = pltpu.make_async_remote_copy(
                src_ref=p_ref, dst_ref=slots.at[me], send_sem=send.at[k - 1], recv_sem=recv.at[k - 1],
                device_id=(px, py, pc), device_id_type=MESH)
            cp.start()
            cps.append(cp)
        for k in range(1, N_DEV):
            px, py, pc = x ^ (k >> 2), y ^ ((k >> 1) & 1), c ^ (k & 1)
            arrival = pltpu.make_async_remote_copy(
                src_ref=p_ref, dst_ref=slots.at[4 * px + 2 * py + pc], send_sem=send.at[k - 1],
                recv_sem=recv.at[k - 1], device_id=(px, py, pc), device_id_type=MESH)
            arrival.wait_recv()
            arrival.wait_send()
        acc = slots[0]
        for k in range(1, N_DEV):
            acc = acc + slots[k]
        o_ref[...] = acc

    vm = pl.BlockSpec(memory_space=pltpu.VMEM)
    return pl.pallas_call(
        body, name="allreduce_small", in_specs=[vm], out_specs=vm,
        out_shape=jax.ShapeDtypeStruct((rows, d), F32),
        scratch_shapes=[pltpu.VMEM((N_DEV, rows, d), F32), pltpu.SemaphoreType.DMA((N_DEV - 1,)),
                        pltpu.SemaphoreType.DMA((N_DEV - 1,))],
        compiler_params=pltpu.CompilerParams(has_side_effects=True),
    )(pack)


def _adamw(w, g, m, v, name):
    rows, cols = w.shape

    def fn(wv, gv, mv, vv):
        m2 = ADAM_B1 * mv + (1.0 - ADAM_B1) * gv
        v2 = ADAM_B2 * vv + (1.0 - ADAM_B2) * (gv * gv)
        m_hat = m2 / (1.0 - ADAM_B1 ** ADAM_STEP)
        v_hat = v2 / (1.0 - ADAM_B2 ** ADAM_STEP)
        delta = -ADAM_LR * (m_hat / (jnp.sqrt(v_hat) + ADAM_EPS) + ADAM_WD * wv)
        return delta, m2, v2

    ins = [(a, cols, 0) for a in (w, g, m, v)]
    return _rowwise(fn, ins, [(cols, F32)] * 3, rows=rows, tm=_tile(rows, max(8, 262144 // cols)), name=name)


BIG = ["w_in", "w_branch_a", "w_branch_b", "w_mix_out", "w_mem_q", "w_mem_kv", "w_mem_o", "w_ffn_in", "w_ffn_out"]
BIG_AXIS = {"w_in": 1, "w_branch_a": 1, "w_branch_b": 1, "w_mix_out": 0, "w_mem_q": 0, "w_mem_kv": 1,
            "w_mem_o": 0, "w_ffn_in": 1, "w_ffn_out": 0}
NORMS = ["norm_mix", "norm_mem_q", "norm_mem_kv", "norm_ffn", "norm_final"]
ORDER = ["norm_mix", "w_in", "conv_w", "w_branch_a", "w_branch_b", "w_mix_out", "norm_mem_q", "norm_mem_kv",
         "w_mem_q", "w_mem_kv", "w_mem_o", "norm_ffn", "w_ffn_in", "w_ffn_out", "norm_final"]


def _pack_small(vals, conv):
    d = vals[0].shape[-1]
    rows = [v.reshape(1, d) for v in vals]
    conv = jnp.pad(conv, ((0, 0), (0, d - conv.shape[1])))
    pad = jnp.zeros((SMALL_ROWS - len(rows) - CONV_K, d), F32)
    return jnp.concatenate(rows + [conv, pad], axis=0)


def kernel(x, mem, norm_mix, w_in, conv_w, w_branch_a, w_branch_b, w_mix_out, norm_mem_q, norm_mem_kv, w_mem_q, w_mem_kv, w_mem_o, norm_ffn, w_ffn_in, w_ffn_out, norm_final, loss_target, m_norm_mix, m_w_in, m_conv_w, m_w_branch_a, m_w_branch_b, m_w_mix_out, m_norm_mem_q, m_norm_mem_kv, m_w_mem_q, m_w_mem_kv, m_w_mem_o, m_norm_ffn, m_w_ffn_in, m_w_ffn_out, m_norm_final, v_norm_mix, v_w_in, v_conv_w, v_w_branch_a, v_w_branch_b, v_w_mix_out, v_norm_mem_q, v_norm_mem_kv, v_w_mem_q, v_w_mem_kv, v_w_mem_o, v_norm_ffn, v_w_ffn_in, v_w_ffn_out, v_norm_final):
    args = dict(locals())
    wts = {n: args[n] for n in ORDER}
    mom = {n: args["m_" + n] for n in ORDER}
    var = {n: args["v_" + n] for n in ORDER}
    x = x[0]
    mem = mem[0]
    target = loss_target[0]
    s, d = x.shape
    gains = {n: wts[n].reshape(1, d) for n in NORMS}
    chip = 2 * lax.axis_index("x") + lax.axis_index("y")
    core = lax.axis_index("c").astype(jnp.int32).reshape(1)
    place = jnp.stack([chip, lax.axis_index("c")]).astype(jnp.int32)

    conv_shard = jnp.pad(conv_w[0], ((0, CONV_ROWS - CONV_K), (0, 0)))
    first = [_cast_place(wts["w_in"][0], BIG_AXIS["w_in"], place, BF, "place_w_in"),
             _cast_place(conv_shard, 1, place, F32, "place_conv_w")]
    w_in_full, conv_full = _gather_weights(first, [BIG_AXIS["w_in"], 1], [True, False])
    W = {"w_in": w_in_full}
    later_w = [n for n in BIG if n != "w_in"]
    later_axes = [BIG_AXIS[n] for n in later_w]
    placed = [_cast_place(wts[n][0], BIG_AXIS[n], place, BF, "place_" + n) for n in later_w]

    h1 = _rmsnorm(x, gains["norm_mix"], "norm_mix_fwd")
    proj = _matmul(h1, W["w_in"], tn=1280, name="in_proj")
    o_a, o_a32, *later_bufs = _sb_fwd(proj, _riding_fetch(placed, later_axes))
    y_b = _conv_fwd(proj, conv_full)
    W.update(zip(later_w, _hand_on(later_bufs, later_axes, "gather_hand_on")))
    br_a, br_b, merged = _branches_merge(o_a, y_b, W["w_branch_a"], W["w_branch_b"], proj)
    x1 = _matmul(merged, W["w_mix_out"], tn=1024, out_dtype=F32, resid=x, name="mix_out")

    hq, q_m = _norm_matmul(x1, gains["norm_mem_q"], W["w_mem_q"], tn=1024, name="norm_mem_q")
    mn, kv = _norm_matmul(mem, gains["norm_mem_kv"], W["w_mem_kv"], tn=1024, name="norm_mem_kv")
    o_m = _mem_fwd(q_m, kv)
    x2 = _matmul(o_m, W["w_mem_o"], tn=1024, out_dtype=F32, resid=x1, name="mem_o")

    hf, gate, up, act = _norm_ffn_in_swiglu(x2, gains["norm_ffn"], W["w_ffn_in"])

    dx3, dx3_b, dg_final, loss_part = _ffn_out_loss(act, W["w_ffn_out"], x2, gains["norm_final"], target)

    dgu = _d_act_swiglu(dx3_b, W["w_ffn_out"], gate, up)
    gw = {"w_ffn_out": _matmul(act, dx3_b, ta=True, tm=1408, tn=512, name="gw_ffn_out")}
    gw["w_ffn_in"] = _matmul(hf, dgu, ta=True, tn=512, name="gw_ffn_in")

    def pair_sums(names, tag):
        views = []
        for n in names:
            r, cdim = gw[n].shape
            views.append(gw[n].reshape(1, 2, r // 2, cdim) if BIG_AXIS[n] == 1
                         else gw[n].reshape(N_CHIPS, 2, r // (2 * N_CHIPS), cdim))
        got = _pair_exchange(views, "grad_pair_exchange_" + tag)
        return [_pair_add(v, g, core, "pair_add_" + n) for n, v, g in zip(names, views, got)]

    def chip_sums(names, sums, slots):
        return [_chip_sum(sm, sl, BIG_AXIS[n], place, "chip_sum_" + n) for n, sm, sl in zip(names, sums, slots)]

    dx2, dx2_b, dg_ffn = _matmul_norm_bwd(dgu, W["w_ffn_in"], x2, gains["norm_ffn"], dx3, tm=256, bf_copy=True,
                                          name="d_hf_norm_bwd")

    do_m = _matmul(dx2_b, W["w_mem_o"], tb=True, tn=1024, name="d_o_m")
    gw["w_mem_o"] = _matmul(o_m, dx2_b, ta=True, tn=512, name="gw_mem_o")
    dq_m, dk_m, dv_m = _mem_bwd(q_m, kv, do_m)
    dkv = jnp.concatenate([dk_m, dv_m], axis=-1)
    dx1, dx1_b, dg_q = _matmul_norm_bwd(dq_m, W["w_mem_q"], x1, gains["norm_mem_q"], dx2, tm=512, bf_copy=True,
                                        name="d_hq_norm_bwd")
    gw["w_mem_q"] = _matmul(hq, dq_m, ta=True, tn=512, name="gw_mem_q")
    dmn = _matmul(dkv, W["w_mem_kv"], tb=True, tn=1024, out_dtype=F32, name="d_mn")
    gw["w_mem_kv"] = _matmul(mn, dkv, ta=True, tn=1024, name="gw_mem_kv")
    _, dg_kv = _rmsnorm_bwd(mem, gains["norm_mem_kv"], dmn, None, "norm_mem_kv_bwd")

    dbr_a, dbr_b, dga, dgb = _d_merged_gates(dx1_b, W["w_mix_out"], proj, br_a, br_b)
    gw["w_mix_out"] = _matmul(merged, dx1_b, ta=True, tn=512, name="gw_mix_out")
    do_a = _matmul(dbr_a, W["w_branch_a"], tb=True, name="d_o_a")
    gw["w_branch_a"] = _matmul(o_a, dbr_a, ta=True, tn=512, name="gw_branch_a")
    dy_b = _matmul(dbr_b, W["w_branch_b"], tb=True, name="d_y_b")
    gw["w_branch_b"] = _matmul(y_b, dbr_b, ta=True, tn=512, name="gw_branch_b")
    du, dgate_b, dgate_c, dconv = _conv_bwd(proj, conv_full, dy_b)
    early = [n for n in BIG if n != "w_in"]
    early_sums = pair_sums(early, "early")
    dq, dk, dv, *early_slots = _sb_bwd(proj, o_a32, do_a, _riding_slots(early_sums, [BIG_AXIS[n] for n in early]))

    def assemble(*parts):
        return jnp.concatenate([p.astype(BF) for p in parts], axis=-1)

    hw = dq.shape[1]
    dproj = _rowwise(assemble, [(t, hw, 0) for t in (dq, dk, dv, du, dgate_b, dgate_c)] + [(dga, d, 0), (dgb, d, 0)],
                     [(proj.shape[1], BF)], rows=s, tm=256, name="assemble_dproj")[0]
    gw["w_in"] = _matmul(h1, dproj, ta=True, tn=640, name="gw_in")
    in_sums = pair_sums(["w_in"], "in")
    grad_x, dg_mix, *in_slots = _matmul_norm_bwd(dproj, W["w_in"], x, gains["norm_mix"], dx1, tm=256, bf_copy=False,
                                                 name="d_h1_norm_bwd",
                                                 rider=_riding_slots(in_sums, [BIG_AXIS["w_in"]]))

    halves = dict(zip(early, chip_sums(early, early_sums, early_slots)))
    halves.update(zip(["w_in"], chip_sums(["w_in"], in_sums, in_slots)))
    both = _half_swap([halves[n] for n in BIG])
    grads = {n: b.reshape(wts[n].shape[1:]) for n, b in zip(BIG, both)}

    small_g = [dg_mix, dg_q, dg_kv, dg_ffn, dg_final]
    pack = _pack_small(small_g, dconv[:CONV_K])
    pack = pack.at[ROW_LOSS].set(jnp.broadcast_to(loss_part[0, :1], (d,)))
    red = _allreduce_small(pack)
    loss = red[ROW_LOSS, 0]
    cw = conv_w.shape[2]
    conv_g = lax.dynamic_slice(red, (ROW_CONV, chip * cw), (CONV_K, cw))
    small_grad = _pack_small([red[i] for i in range(len(NORMS))], conv_g)
    small = [_pack_small([t[n] for n in NORMS], t["conv_w"][0]) for t in (wts, mom, var)]
    s_delta, s_m, s_v = _adamw(small[0], small_grad, small[1], small[2], "adamw_small")

    out = {"grad": {}, "delta": {}, "new_m": {}, "new_v": {}}
    for n in BIG:
        shp = wts[n].shape
        dl, m2, v2 = _adamw(wts[n][0], grads[n], mom[n][0], var[n][0], "adamw_" + n)
        out["grad"][n] = grads[n].reshape(shp)
        out["delta"][n], out["new_m"][n], out["new_v"][n] = dl.reshape(shp), m2.reshape(shp), v2.reshape(shp)
    for key, blk in (("grad", small_grad), ("delta", s_delta), ("new_m", s_m), ("new_v", s_v)):
        for i, n in enumerate(NORMS):
            out[key][n] = blk[i].reshape(wts[n].shape)
        out[key]["conv_w"] = blk[ROW_CONV:ROW_CONV + CONV_K, :cw].reshape(conv_w.shape)

    return (loss, grad_x[None], *[out["grad"][n] for n in ORDER], *[out["delta"][n] for n in ORDER],
            *[out["new_m"][n] for n in ORDER], *[out["new_v"][n] for n in ORDER])
```

```python
import functools
import math

import jax
import jax.numpy as jnp
from jax import lax
from jax.experimental import pallas as pl
from jax.experimental.pallas import tpu as pltpu

BF = jnp.bfloat16
F32 = jnp.float32
MESH = pl.DeviceIdType.MESH

SB_HEAD_DIM = 64
LANES = 128
MEM_HEADS = 4
CONV_K = 3
CONV_ROWS = 8
EPS = 1e-6
N_CHIPS = 4
N_DEV = 8
VMEM_LIMIT = 56 * 1024 * 1024

ADAM_LR = 0.001
ADAM_B1 = 0.9
ADAM_B2 = 0.999
ADAM_EPS = 1e-08
ADAM_WD = 0.01
ADAM_STEP = 10

SMALL_ROWS = 16
ROW_CONV = 5
ROW_LOSS = 8


def _params(sem=None, **kw):
    return pltpu.CompilerParams(dimension_semantics=sem, vmem_limit_bytes=VMEM_LIMIT, **kw)


def _tile(dim, pref):
    if dim <= pref:
        return dim
    for step in (LANES, 8):
        t = (pref // step) * step
        while t >= step:
            if dim % t == 0:
                return t
            t -= step
    raise ValueError(f"no tile of {dim} under {pref}")


def _matmul(a, b, *, ta=False, tb=False, tm=1024, tn=512, out_dtype=BF, resid=None, rider=None, name):
    if ta:
        kdim, m = a.shape
    else:
        m, kdim = a.shape
    n = b.shape[0] if tb else b.shape[1]
    tm, tn = _tile(m, tm), _tile(n, tn)
    a_spec = pl.BlockSpec((kdim, tm), lambda i, j: (0, i)) if ta else pl.BlockSpec((tm, kdim), lambda i, j: (i, 0))
    b_spec = pl.BlockSpec((tn, kdim), lambda i, j: (j, 0)) if tb else pl.BlockSpec((kdim, tn), lambda i, j: (0, j))
    o_spec = pl.BlockSpec((tm, tn), lambda i, j: (i, j))
    dims = (((0 if ta else 1,), (1 if tb else 0,)), ((), ()))
    has_res = resid is not None
    n_in = 2 + has_res

    def body(*refs):
        ins, (o_ref,) = _host_refs(refs, rider, n_in, 1)
        av, bv = ins[0][...], ins[1][...]
        if av.dtype != BF:
            av = av.astype(BF)
        if bv.dtype != BF:
            bv = bv.astype(BF)
        acc = lax.dot_general(av, bv, dims, preferred_element_type=F32)
        if has_res:
            acc = ins[2][...] + acc
        o_ref[...] = acc.astype(o_ref.dtype)
        if rider:
            rider.run(refs, n_in, 1, (pl.program_id(0) == 0) & (pl.program_id(1) == 0),
                      (pl.program_id(0) == m // tm - 1) & (pl.program_id(1) == n // tn - 1))

    res = pl.pallas_call(
        body, name=name, grid=(m // tm, n // tn),
        compiler_params=_params(("arbitrary", "arbitrary") if rider else ("parallel", "parallel"),
                                has_side_effects=rider is not None),
        **_with_rider(rider, n_in, 1, [a_spec, b_spec] + ([o_spec] if has_res else []), [o_spec],
                      [jax.ShapeDtypeStruct((m, n), out_dtype)]),
    )(*([a, b] + ([resid] if has_res else []) + (rider.arrays if rider else [])))
    return res if rider else res[0]


def _rowwise(fn, ins, outs, *, rows, tm, name, accs=(), after=()):
    tm = _tile(rows, tm)
    in_specs, args = [], []
    for arr, cols, cb in ins:
        if cols is None:
            in_specs.append(pl.BlockSpec(arr.shape, lambda i, nd=arr.ndim: (0,) * nd))
        else:
            in_specs.append(pl.BlockSpec((tm, cols), lambda i, cb=cb: (i, cb)))
        args.append(arr)
    out_specs = [pl.BlockSpec((tm, cols), lambda i: (i, 0)) for cols, _ in outs]
    out_shape = [jax.ShapeDtypeStruct((rows, cols), dt) for cols, dt in outs]
    for r, c in accs:
        out_specs.append(pl.BlockSpec((r, c), lambda i: (0, 0)))
        out_shape.append(jax.ShapeDtypeStruct((r, c), F32))
    n_in, n_out = len(ins), len(outs)
    in_specs += [HBM_SPEC] * len(after)
    args += list(after)

    def body(*refs):
        res = fn(*[r[...] for r in refs[:n_in]])
        if not isinstance(res, (tuple, list)):
            res = (res,)
        refs = refs[n_in + len(after):]
        for o_ref, val in zip(refs[:n_out], res[:n_out]):
            o_ref[...] = val.astype(o_ref.dtype)
        first = pl.program_id(0) == 0
        for a_ref, val in zip(refs[n_out:], res[n_out:]):
            @pl.when(first)
            def _(a_ref=a_ref, val=val):
                a_ref[...] = val

            @pl.when(jnp.logical_not(first))
            def _(a_ref=a_ref, val=val):
                a_ref[...] += val

    res = pl.pallas_call(
        body, name=name, grid=(rows // tm,), in_specs=in_specs, out_specs=out_specs, out_shape=out_shape,
        compiler_params=_params(("arbitrary",) if accs else ("parallel",)),
    )(*args)
    return res


def _rstd(xf):
    return lax.rsqrt(jnp.mean(xf * xf, axis=-1, keepdims=True) + EPS)


def _rmsnorm(x, g, name, after=()):
    rows, d = x.shape
    return _rowwise(lambda xv, gv: xv * _rstd(xv) * gv, [(x, d, 0), (g, None, None)], [(d, BF)],
                    rows=rows, tm=512, name=name, after=after)[0]


def _rmsnorm_bwd(x, g, dy, resid, name, bf_copy=False):
    rows, d = x.shape

    def fn(xv, gv, dyv, *rest):
        dyv = dyv.astype(F32)
        r = _rstd(xv)
        xn = xv * r
        dxn = dyv * gv
        dx = r * (dxn - xn * jnp.mean(dxn * xn, axis=-1, keepdims=True))
        if rest:
            dx = rest[0] + dx
        return (dx,) * (1 + bf_copy) + (jnp.sum(dyv * xn, axis=0, keepdims=True),)

    ins = [(x, d, 0), (g, None, None), (dy, d, 0)] + ([(resid, d, 0)] if resid is not None else [])
    outs = [(d, F32)] + ([(d, BF)] if bf_copy else [])
    return _rowwise(fn, ins, outs, rows=rows, tm=512, name=name, accs=[(1, d)])


def _matmul_norm_bwd(a, w, x, g, resid, *, tm, bf_copy, name, rider=None):
    m, kdim = a.shape
    d = w.shape[0]
    tm = _tile(m, tm)
    row = lambda i: (i, 0)
    whole = lambda i: (0, 0)
    n_out = 2 + bf_copy

    def body(*refs):
        (a_ref, w_ref, x_ref, g_ref, r_ref), outs = _host_refs(refs, rider, 5, n_out)
        dy = lax.dot_general(a_ref[...], w_ref[...], NT, preferred_element_type=F32)
        xv = x_ref[...]
        r = _rstd(xv)
        xn = xv * r
        dxn = dy * g_ref[...]
        dx = r_ref[...] + r * (dxn - xn * jnp.mean(dxn * xn, axis=-1, keepdims=True))
        outs[0][...] = dx
        if bf_copy:
            outs[1][...] = dx.astype(BF)
        dg = jnp.sum(dy * xn, axis=0, keepdims=True)
        first = pl.program_id(0) == 0

        @pl.when(first)
        def _():
            outs[-1][...] = dg

        @pl.when(jnp.logical_not(first))
        def _():
            outs[-1][...] += dg

        if rider:
            rider.run(refs, 5, n_out, first, pl.program_id(0) == m // tm - 1)

    tok = pl.BlockSpec((tm, d), row)
    out_specs = [tok] + ([tok] if bf_copy else []) + [pl.BlockSpec((1, d), whole)]
    out_shape = ([jax.ShapeDtypeStruct((m, d), F32)] + ([jax.ShapeDtypeStruct((m, d), BF)] if bf_copy else [])
                 + [jax.ShapeDtypeStruct((1, d), F32)])
    return pl.pallas_call(
        body, name=name, grid=(m // tm,),
        compiler_params=_params(("arbitrary",), has_side_effects=rider is not None),
        **_with_rider(
            rider, 5, n_out,
            [pl.BlockSpec((tm, kdim), row), pl.BlockSpec((d, kdim), whole), tok, pl.BlockSpec((1, d), whole), tok],
            out_specs, out_shape),
    )(a, w, x, g, resid, *(rider.arrays if rider else []))


def _norm_ffn_in_swiglu(x, g, w, *, tm=1024, tn=1408):
    m, kdim = x.shape
    f = w.shape[1] // 2
    tm, tn = _tile(m, tm), _tile(f, tn)
    nj = f // tn

    def body(x_ref, g_ref, wg_ref, wu_ref, h_ref, gate_ref, up_ref, act_ref):
        @pl.when(pl.program_id(1) == 0)
        def _():
            xv = x_ref[...]
            h_ref[...] = (xv * _rstd(xv) * g_ref[...]).astype(h_ref.dtype)

        hv = h_ref[...]
        gate = jnp.dot(hv, wg_ref[...], preferred_element_type=F32)
        up = jnp.dot(hv, wu_ref[...], preferred_element_type=F32)
        gate_ref[...] = gate.astype(gate_ref.dtype)
        up_ref[...] = up.astype(up_ref.dtype)
        act_ref[...] = (gate * jax.nn.sigmoid(gate) * up).astype(act_ref.dtype)

    tile = pl.BlockSpec((tm, tn), lambda i, j: (i, j))
    rows = pl.BlockSpec((tm, kdim), lambda i, j: (i, 0))
    return pl.pallas_call(
        body, name="norm_ffn_in_swiglu", grid=(m // tm, nj),
        in_specs=[rows, pl.BlockSpec((1, kdim), lambda i, j: (0, 0)),
                  pl.BlockSpec((kdim, tn), lambda i, j: (0, j)),
                  pl.BlockSpec((kdim, tn), lambda i, j: (0, nj + j))],
        out_specs=[rows, tile, tile, tile],
        out_shape=[jax.ShapeDtypeStruct((m, kdim), BF)] + [jax.ShapeDtypeStruct((m, f), BF)] * 3,
        compiler_params=_params(("parallel", "arbitrary")),
    )(x, g, w, w)


def _norm_matmul(x, g, w, *, tm=1024, tn, name, after=()):
    m, kdim = x.shape
    n = w.shape[1]
    tm, tn = _tile(m, tm), _tile(n, tn)

    def body(x_ref, g_ref, w_ref, *rest):
        h_ref, o_ref = rest[len(after):]

        @pl.when(pl.program_id(1) == 0)
        def _():
            xv = x_ref[...]
            h_ref[...] = (xv * _rstd(xv) * g_ref[...]).astype(h_ref.dtype)

        o_ref[...] = jnp.dot(h_ref[...], w_ref[...], preferred_element_type=F32).astype(o_ref.dtype)

    rows = pl.BlockSpec((tm, kdim), lambda i, j: (i, 0))
    return pl.pallas_call(
        body, name=name, grid=(m // tm, n // tn),
        in_specs=[rows, pl.BlockSpec((1, kdim), lambda i, j: (0, 0)), pl.BlockSpec((kdim, tn), lambda i, j: (0, j))]
        + [HBM_SPEC] * len(after),
        out_specs=[rows, pl.BlockSpec((tm, tn), lambda i, j: (i, j))],
        out_shape=[jax.ShapeDtypeStruct((m, kdim), BF), jax.ShapeDtypeStruct((m, n), BF)],
        compiler_params=_params(("parallel", "arbitrary")),
    )(x, g, w, *after)


def _ffn_out_loss(act, w, resid, g, target, *, tm=512):
    m, f = act.shape
    d = w.shape[1]
    tm = _tile(m, tm)
    row = lambda i: (i, 0)
    whole = lambda i: (0, 0)

    def body(a_ref, w_ref, r_ref, g_ref, t_ref, dx_ref, dxb_ref, dg_ref, loss_ref):
        xv = r_ref[...] + jnp.dot(a_ref[...], w_ref[...], preferred_element_type=F32)
        gv = g_ref[...]
        r = _rstd(xv)
        xn = xv * r
        err = xn * gv - t_ref[...]
        loss = 0.5 * jnp.sum(jnp.mean(err * err, axis=-1, keepdims=True), axis=0, keepdims=True)
        dyv = err * (1.0 / d)
        dxn = dyv * gv
        dx = r * (dxn - xn * jnp.mean(dxn * xn, axis=-1, keepdims=True))
        dx_ref[...] = dx
        dxb_ref[...] = dx.astype(dxb_ref.dtype)
        dg = jnp.sum(dyv * xn, axis=0, keepdims=True)
        loss_b = jnp.broadcast_to(loss, (1, LANES))
        first = pl.program_id(0) == 0

        @pl.when(first)
        def _():
            dg_ref[...] = dg
            loss_ref[...] = loss_b

        @pl.when(jnp.logical_not(first))
        def _():
            dg_ref[...] += dg
            loss_ref[...] += loss_b

    tok = pl.BlockSpec((tm, d), row)
    return pl.pallas_call(
        body, name="ffn_out_loss", grid=(m // tm,),
        in_specs=[pl.BlockSpec((tm, f), row), pl.BlockSpec((f, d), whole), tok, pl.BlockSpec((1, d), whole), tok],
        out_specs=[tok, tok, pl.BlockSpec((1, d), whole), pl.BlockSpec((1, LANES), whole)],
        out_shape=[jax.ShapeDtypeStruct((m, d), F32), jax.ShapeDtypeStruct((m, d), BF),
                   jax.ShapeDtypeStruct((1, d), F32), jax.ShapeDtypeStruct((1, LANES), F32)],
        compiler_params=_params(("arbitrary",)),
    )(act, w, resid, g, target)


def _d_act_swiglu(dx, w, gate, up, *, tm=256):
    m, d = dx.shape
    f = w.shape[0]
    tm = _tile(m, tm)
    row = lambda i: (i, 0)

    def body(dx_ref, w_ref, gate_ref, up_ref, o_ref):
        da = lax.dot_general(dx_ref[...], w_ref[...], NT, preferred_element_type=F32)
        gv, uv = gate_ref[...].astype(F32), up_ref[...].astype(F32)
        sg = jax.nn.sigmoid(gv)
        dgate = da * uv * (sg * (1.0 + gv * (1.0 - sg)))
        o_ref[...] = jnp.concatenate([dgate, da * (gv * sg)], axis=-1).astype(o_ref.dtype)

    return pl.pallas_call(
        body, name="d_act_swiglu", grid=(m // tm,),
        in_specs=[pl.BlockSpec((tm, d), row), pl.BlockSpec((f, d), lambda i: (0, 0)),
                  pl.BlockSpec((tm, f), row), pl.BlockSpec((tm, f), row)],
        out_specs=pl.BlockSpec((tm, 2 * f), row), out_shape=jax.ShapeDtypeStruct((m, 2 * f), BF),
        compiler_params=_params(("parallel",)),
    )(dx, w, gate, up)


GATE_A_BLK, GATE_B_BLK = 3, 4


def _branches_merge(o_a, y_b, w_a, w_b, proj, *, tm=1024):
    m, kdim = o_a.shape
    d = w_a.shape[1]
    tm = _tile(m, tm)
    row = lambda i: (i, 0)

    def body(a_ref, b_ref, wa_ref, wb_ref, ga_ref, gb_ref, bra_ref, brb_ref, merged_ref):
        bra = jnp.dot(a_ref[...], wa_ref[...], preferred_element_type=F32)
        brb = jnp.dot(b_ref[...], wb_ref[...], preferred_element_type=F32)
        bra_ref[...] = bra.astype(bra_ref.dtype)
        brb_ref[...] = brb.astype(brb_ref.dtype)
        merged = jax.nn.sigmoid(ga_ref[...].astype(F32)) * bra + jax.nn.sigmoid(gb_ref[...].astype(F32)) * brb
        merged_ref[...] = merged.astype(merged_ref.dtype)

    tok = pl.BlockSpec((tm, d), row)
    return pl.pallas_call(
        body, name="branches_merge", grid=(m // tm,),
        in_specs=[pl.BlockSpec((tm, kdim), row), pl.BlockSpec((tm, kdim), row),
                  pl.BlockSpec((kdim, d), lambda i: (0, 0)), pl.BlockSpec((kdim, d), lambda i: (0, 0)),
                  pl.BlockSpec((tm, d), lambda i: (i, GATE_A_BLK)), pl.BlockSpec((tm, d), lambda i: (i, GATE_B_BLK))],
        out_specs=[tok, tok, tok], out_shape=[jax.ShapeDtypeStruct((m, d), BF)] * 3,
        compiler_params=_params(("parallel",)),
    )(o_a, y_b, w_a, w_b, proj, proj)


def _d_merged_gates(dx, w_mix, proj, br_a, br_b, *, tm=512):
    m, d = dx.shape
    tm = _tile(m, tm)
    row = lambda i: (i, 0)

    def body(dx_ref, w_ref, ga_ref, gb_ref, bra_ref, brb_ref, dbra_ref, dbrb_ref, dga_ref, dgb_ref):
        dm = lax.dot_general(dx_ref[...], w_ref[...], NT, preferred_element_type=F32)
        sa, sb = jax.nn.sigmoid(ga_ref[...].astype(F32)), jax.nn.sigmoid(gb_ref[...].astype(F32))
        dbra_ref[...] = (dm * sa).astype(dbra_ref.dtype)
        dbrb_ref[...] = (dm * sb).astype(dbrb_ref.dtype)
        dga_ref[...] = (dm * bra_ref[...].astype(F32) * (sa * (1.0 - sa))).astype(dga_ref.dtype)
        dgb_ref[...] = (dm * brb_ref[...].astype(F32) * (sb * (1.0 - sb))).astype(dgb_ref.dtype)

    tok = pl.BlockSpec((tm, d), row)
    return pl.pallas_call(
        body, name="d_merged_gates", grid=(m // tm,),
        in_specs=[tok, pl.BlockSpec((d, d), lambda i: (0, 0)),
                  pl.BlockSpec((tm, d), lambda i: (i, GATE_A_BLK)), pl.BlockSpec((tm, d), lambda i: (i, GATE_B_BLK)),
                  tok, tok],
        out_specs=[tok] * 4, out_shape=[jax.ShapeDtypeStruct((m, d), BF)] * 4,
        compiler_params=_params(("parallel",)),
    )(dx, w_mix, proj, proj, br_a, br_b)


def _loss_head(x, g, target, name):
    rows, d = x.shape

    def fn(xv, gv, tv):
        r = _rstd(xv)
        xn = xv * r
        err = xn * gv - tv
        per_tok = jnp.mean(err * err, axis=-1, keepdims=True)
        loss = 0.5 * jnp.sum(per_tok, axis=0, keepdims=True)
        dyv = err * (1.0 / d)
        dxn = dyv * gv
        dx = r * (dxn - xn * jnp.mean(dxn * xn, axis=-1, keepdims=True))
        return dx, dx, jnp.sum(dyv * xn, axis=0, keepdims=True), jnp.broadcast_to(loss, (1, LANES))

    return _rowwise(fn, [(x, d, 0), (g, None, None), (target, d, 0)], [(d, F32), (d, BF)],
                    rows=rows, tm=512, name=name, accs=[(1, d), (1, LANES)])


SB_TK = 128
SB_KT = 2


def _sb_consts(tq):
    tk = SB_TK
    diff = lax.broadcasted_iota(jnp.int32, (tq, tk), 1) - lax.broadcasted_iota(jnp.int32, (tq, tk), 0)
    rj = lax.broadcasted_iota(jnp.int32, (2 * tk, 2 * tk), 0) & (tk - 1)
    cj = lax.broadcasted_iota(jnp.int32, (2 * tk, 2 * tk), 1)
    ones_half = cj >= tk
    later = jnp.where((rj > cj) | ones_half, 1.0, 0.0).astype(BF)
    later_incl = jnp.where((rj >= cj) | ones_half, 1.0, 0.0).astype(BF)
    return diff, later, later_incl


def _split_dot(val, rhs_twice):
    hi = val.astype(BF)
    lo = (val - hi.astype(F32)).astype(BF)
    return jnp.dot(jnp.concatenate([hi, lo], axis=1), rhs_twice, preferred_element_type=F32)


def _log_terms(z):
    sp = jnp.maximum(z, 0.0) + jnp.log(1.0 + jnp.exp(-jnp.abs(z)))
    return z - sp, sp


NT = (((1,), (1,)), ((), ()))
TN = (((0,), (0,)), ((), ()))


def _head_lane_masks(rows):
    lane = lax.broadcasted_iota(jnp.int32, (rows, LANES), 1)
    first = jnp.where(lane < SB_HEAD_DIM, 1.0, 0.0)
    return first.astype(BF), (1.0 - first).astype(BF)


DEAD_LOG = 104.0


def _walk_back(i, step, state, carries_of):
    def alive(st):
        c0, c1 = carries_of(st)
        return jnp.min(jnp.minimum(c0, c1)) < DEAD_LOG

    def cond(loop):
        done, live, _ = loop
        return jnp.logical_and(done < i, live)

    def body(loop):
        done, _, st = loop
        st = step(i - 1 - done, st)
        return done + 1, alive(st), st

    return lax.while_loop(cond, body, (jnp.int32(0), alive(state), state))[2]


def _tail(a, r0):
    return a if r0 == 0 else a[r0:]


def _add_tail(a, r0, delta):
    return a + delta if r0 == 0 else jnp.concatenate([a[:r0], a[r0:] + delta], axis=0)


def _both_heads(tile, masks):
    return jnp.concatenate([tile * masks[0], tile * masks[1]], axis=0)


def _with_rider(rider, n_in, n_out, in_specs, out_specs, out_shape, scratch=()):
    kw = dict(in_specs=list(in_specs), out_specs=list(out_specs), out_shape=list(out_shape),
              scratch_shapes=list(scratch), input_output_aliases={})
    if rider:
        extra = rider.call_args(n_in, n_out)
        kw["in_specs"] += extra["in_specs"]
        kw["out_specs"] += extra["out_specs"]
        kw["out_shape"] += extra["out_shape"]
        kw["scratch_shapes"] += extra["scratch"]
        kw["input_output_aliases"] = extra["aliases"]
    return kw


def _sb_fwd(proj, rider=None):
    s = proj.shape[0]
    tk, tq = SB_TK, SB_KT * SB_TK
    n_pairs = 4
    scale = 1.0 / math.sqrt(SB_HEAD_DIM)

    def body(*refs):
        (q_ref, k_ref, v_ref), (o_ref, o32_ref) = _host_refs(refs, rider, 3, 2)
        i = pl.program_id(1)
        diff, later, _ = _sb_consts(tq)
        qs = (q_ref[...].astype(F32) * scale).astype(BF)
        lane_masks = _head_lane_masks(tk)

        def step(g, state, masked):
            tiles = list(reversed(range(SB_KT)))
            chains = [(t, h) for t in tiles for h in range(2)]
            rows = {t: pl.ds(pl.multiple_of((g * SB_KT + t) * tk, tk), tk) for t in tiles}
            ks = {t: _both_heads(k_ref[rows[t], :], lane_masks) for t in tiles}
            vs = {t: _both_heads(v_ref[rows[t], :], lane_masks) for t in tiles}
            r0 = {t: t * tk if masked else 0 for t in tiles}
            allowed = {t: _tail(diff, r0[t]) < -t * tk for t in tiles}
            zs = {t: lax.dot_general(_tail(qs, r0[t]), ks[t], NT, preferred_element_type=F32) for t in tiles}
            logs = {}
            for t, h in chains:
                log_b, sp = _log_terms(zs[t][:, h * tk:(h + 1) * tk])
                logs[t, h] = (log_b, jnp.where(allowed[t], sp, 0.0) if masked else sp)
            sums = {c: _split_dot(logs[c][1], later) for c in chains}
            carries = list(state[0])
            ws = {}
            for t, h in chains:
                w = jnp.exp(logs[t, h][0] - (sums[t, h][:, :tk] + _tail(carries[h], r0[t])))
                ws[t, h] = (jnp.where(allowed[t], w, 0.0) if masked else w).astype(BF)
                carries[h] = _add_tail(carries[h], r0[t], sums[t, h][:, tk:])
            acc = state[1]
            for t in tiles:
                acc = _add_tail(acc, r0[t], jnp.dot(jnp.concatenate([ws[t, 0], ws[t, 1]], axis=1), vs[t],
                                                    preferred_element_type=F32))
            return tuple(carries), acc

        zero = jnp.zeros((tq, LANES), F32)
        state = step(i, ((zero, zero), zero), True)
        state = _walk_back(i, lambda g, st: step(g, st, False), state, lambda st: st[0])
        o_ref[...] = state[1].astype(o_ref.dtype)
        o32_ref[...] = state[1]
        if rider:
            rider.run(refs, 3, 2, (pl.program_id(0) == 0) & (i == 0),
                      (pl.program_id(0) == n_pairs - 1) & (i == s // tq - 1))

    tok = pl.BlockSpec((tq, LANES), lambda p, i: (i, p))
    return pl.pallas_call(
        body, name="sb_attn_fwd", grid=(n_pairs, s // tq),
        compiler_params=_params(("arbitrary", "arbitrary"), has_side_effects=rider is not None),
        **_with_rider(
            rider, 3, 2,
            [tok, pl.BlockSpec((s, LANES), lambda p, i: (0, n_pairs + p)),
             pl.BlockSpec((s, LANES), lambda p, i: (0, 2 * n_pairs + p))],
            [tok, tok],
            [jax.ShapeDtypeStruct((s, n_pairs * LANES), BF), jax.ShapeDtypeStruct((s, n_pairs * LANES), F32)]),
    )(proj, proj, proj, *(rider.arrays if rider else []))


def _sb_bwd(proj, o32, do_a, rider=None):
    s = proj.shape[0]
    tk, tq = SB_TK, SB_KT * SB_TK
    n_pairs = 4
    scale = 1.0 / math.sqrt(SB_HEAD_DIM)

    def body(*refs):
        (q_ref, k_ref, v_ref, o_ref, do_ref), (dq_ref, dk_ref, dv_ref) = _host_refs(refs, rider, 5, 3)
        i = pl.program_id(1)

        @pl.when(i == 0)
        def _():
            dk_ref[...] = jnp.zeros_like(dk_ref)
            dv_ref[...] = jnp.zeros_like(dv_ref)

        diff, later, later_incl = _sb_consts(tq)
        qs = (q_ref[...].astype(F32) * scale).astype(BF)
        do2 = do_ref[...]
        prod = do2.astype(F32) * o_ref[...]
        lane_masks = _head_lane_masks(tk)
        first_head = lax.broadcasted_iota(jnp.int32, (tq, LANES), 1) < SB_HEAD_DIM
        totals = [jnp.broadcast_to(jnp.sum(jnp.where(keep, prod, 0.0), axis=-1, keepdims=True), (tq, tk))
                  for keep in (first_head, jnp.logical_not(first_head))]
        first_head_k = first_head[:tk]

        def step(g_idx, state, masked):
            tiles = list(reversed(range(SB_KT)))
            chains = [(t, h) for t in tiles for h in range(2)]
            rows = {t: pl.ds(pl.multiple_of((g_idx * SB_KT + t) * tk, tk), tk) for t in tiles}
            ks = {t: _both_heads(k_ref[rows[t], :], lane_masks) for t in tiles}
            vs = {t: _both_heads(v_ref[rows[t], :], lane_masks) for t in tiles}
            r0 = {t: t * tk if masked else 0 for t in tiles}
            allowed = {t: _tail(diff, r0[t]) < -t * tk for t in tiles}
            zs = {t: lax.dot_general(_tail(qs, r0[t]), ks[t], NT, preferred_element_type=F32) for t in tiles}
            dws = {t: lax.dot_general(_tail(do2, r0[t]), vs[t], NT, preferred_element_type=F32) for t in tiles}
            logs = {}
            for t, h in chains:
                log_b, sp = _log_terms(zs[t][:, h * tk:(h + 1) * tk])
                logs[t, h] = (log_b, jnp.where(allowed[t], sp, 0.0) if masked else sp)
            sums = {c: _split_dot(logs[c][1], later) for c in chains}
            c_log, c_g = list(state[0]), list(state[1])
            ws, gs = {}, {}
            for t, h in chains:
                w = jnp.exp(logs[t, h][0] - (sums[t, h][:, :tk] + _tail(c_log[h], r0[t])))
                ws[t, h] = (jnp.where(allowed[t], w, 0.0) if masked else w).astype(BF)
                c_log[h] = _add_tail(c_log[h], r0[t], sums[t, h][:, tk:])
                gs[t, h] = ws[t, h].astype(F32) * dws[t][:, h * tk:(h + 1) * tk]
            gsums = {c: _split_dot(gs[c], later_incl) for c in chains}
            dzs = {}
            for t, h in chains:
                beta = jnp.exp(logs[t, h][0])
                earlier = _tail(totals[h], r0[t]) - (gsums[t, h][:, :tk] + _tail(c_g[h], r0[t]))
                dz = gs[t, h] * (1.0 - beta) - earlier * beta
                dzs[t, h] = (jnp.where(allowed[t], dz, 0.0) if masked else dz).astype(BF)
                c_g[h] = _add_tail(c_g[h], r0[t], gsums[t, h][:, tk:])
            dq = state[2]
            for t in tiles:
                dz_both = jnp.concatenate([dzs[t, 0], dzs[t, 1]], axis=1)
                w_both = jnp.concatenate([ws[t, 0], ws[t, 1]], axis=1)
                dq = _add_tail(dq, r0[t], jnp.dot(dz_both, ks[t], preferred_element_type=F32))
                dk2 = lax.dot_general(dz_both, _tail(qs, r0[t]), TN, preferred_element_type=F32)
                dv2 = lax.dot_general(w_both, _tail(do2, r0[t]), TN, preferred_element_type=F32)
                dk_ref[rows[t], :] += jnp.where(first_head_k, dk2[:tk], dk2[tk:])
                dv_ref[rows[t], :] += jnp.where(first_head_k, dv2[:tk], dv2[tk:])
            return tuple(c_log), tuple(c_g), dq

        zero = jnp.zeros((tq, LANES), F32)
        state = step(i, ((zero, zero), (zero, zero), zero), True)
        state = _walk_back(i, lambda g, st: step(g, st, False), state, lambda st: st[0])
        dq_ref[...] = (state[2] * scale).astype(dq_ref.dtype)
        if rider:
            rider.run(refs, 5, 3, (pl.program_id(0) == 0) & (i == 0),
                      (pl.program_id(0) == n_pairs - 1) & (i == s // tq - 1))

    width = n_pairs * LANES
    tok = pl.BlockSpec((tq, LANES), lambda p, i: (i, p))
    return pl.pallas_call(
        body, name="sb_attn_bwd", grid=(n_pairs, s // tq),
        compiler_params=_params(("arbitrary", "arbitrary"), has_side_effects=rider is not None),
        **_with_rider(
            rider, 5, 3,
            [tok, pl.BlockSpec((s, LANES), lambda p, i: (0, n_pairs + p)),
             pl.BlockSpec((s, LANES), lambda p, i: (0, 2 * n_pairs + p)), tok, tok],
            [tok, pl.BlockSpec((s, LANES), lambda p, i: (0, p)), pl.BlockSpec((s, LANES), lambda p, i: (0, p))],
            [jax.ShapeDtypeStruct((s, width), BF), jax.ShapeDtypeStruct((s, width), F32),
             jax.ShapeDtypeStruct((s, width), F32)]),
    )(proj, proj, proj, o32, do_a, *(rider.arrays if rider else []))


CONV_COL0 = 12


def _shift_rows(v, k):
    n = v.shape[0]
    row = lax.broadcasted_iota(jnp.int32, v.shape, 0)
    rolled = pltpu.roll(v, k % n, axis=0)
    keep = row >= k if k > 0 else row < n + k
    return jnp.where(keep, rolled, 0.0)


def _conv_specs(s):
    return [pl.BlockSpec((s, LANES), lambda cb: (0, CONV_COL0 + cb)),
            pl.BlockSpec((s, LANES), lambda cb: (0, CONV_COL0 + 4 + cb)),
            pl.BlockSpec((s, LANES), lambda cb: (0, CONV_COL0 + 8 + cb)),
            pl.BlockSpec((CONV_ROWS, LANES), lambda cb: (0, cb))]


def _conv_fwd(proj, conv_w):
    s = proj.shape[0]

    def body(u_ref, gb_ref, gc_ref, w_ref, y_ref):
        cu = gc_ref[...].astype(F32) * u_ref[...].astype(F32)
        w = w_ref[...]
        y = w[0:1] * _shift_rows(cu, 2) + w[1:2] * _shift_rows(cu, 1) + w[2:3] * cu
        y_ref[...] = (gb_ref[...].astype(F32) * y).astype(y_ref.dtype)

    return pl.pallas_call(
        body, name="conv_fwd", grid=(4,), in_specs=_conv_specs(s),
        out_specs=pl.BlockSpec((s, LANES), lambda cb: (0, cb)),
        out_shape=jax.ShapeDtypeStruct((s, 4 * LANES), BF),
        compiler_params=_params(("parallel",)),
    )(proj, proj, proj, conv_w)


def _conv_bwd(proj, conv_w, dy):
    s = proj.shape[0]

    def body(u_ref, gb_ref, gc_ref, w_ref, dy_ref, du_ref, dgb_ref, dgc_ref, dw_ref):
        u, gc = u_ref[...].astype(F32), gc_ref[...].astype(F32)
        dyv = dy_ref[...].astype(F32)
        w = w_ref[...]
        cu = gc * u
        cu1, cu2 = _shift_rows(cu, 1), _shift_rows(cu, 2)
        conv = w[0:1] * cu2 + w[1:2] * cu1 + w[2:3] * cu
        dgb_ref[...] = (dyv * conv).astype(dgb_ref.dtype)
        dc = dyv * gb_ref[...].astype(F32)
        dcu = w[2:3] * dc + w[1:2] * _shift_rows(dc, -1) + w[0:1] * _shift_rows(dc, -2)
        dgc_ref[...] = (dcu * u).astype(dgc_ref.dtype)
        du_ref[...] = (dcu * gc).astype(du_ref.dtype)
        tap_row = lax.broadcasted_iota(jnp.int32, (CONV_ROWS, LANES), 0)
        dw = jnp.zeros((CONV_ROWS, LANES), F32)
        for t, shifted in enumerate((cu2, cu1, cu)):
            dw = jnp.where(tap_row == t, jnp.sum(dc * shifted, axis=0, keepdims=True), dw)
        dw_ref[...] = dw

    col = pl.BlockSpec((s, LANES), lambda cb: (0, cb))
    act = jax.ShapeDtypeStruct((s, 4 * LANES), BF)
    return pl.pallas_call(
        body, name="conv_bwd", grid=(4,), in_specs=_conv_specs(s) + [col],
        out_specs=[col, col, col, pl.BlockSpec((CONV_ROWS, LANES), lambda cb: (0, cb))],
        out_shape=[act, act, act, jax.ShapeDtypeStruct((CONV_ROWS, 4 * LANES), F32)],
        compiler_params=_params(("parallel",)),
    )(proj, proj, proj, conv_w, dy)


def _mem_probs(q, k, scale):
    sc = lax.dot_general(q, k, NT, preferred_element_type=F32) * scale
    p = jnp.exp(sc - jnp.max(sc, axis=-1, keepdims=True))
    return p / jnp.sum(p, axis=-1, keepdims=True)


def _mem_fwd(q_m, kv, tq=2048):
    s, d = q_m.shape
    mlen = kv.shape[0]
    hd = d // MEM_HEADS
    tq = _tile(s, tq)
    scale = 1.0 / math.sqrt(hd)

    def body(q_ref, k_ref, v_ref, o_ref):
        p = _mem_probs(q_ref[...], k_ref[...], scale)
        o_ref[...] = jnp.dot(p.astype(BF), v_ref[...], preferred_element_type=F32).astype(o_ref.dtype)

    return pl.pallas_call(
        body, name="mem_attn_fwd", grid=(MEM_HEADS, s // tq),
        in_specs=[pl.BlockSpec((tq, hd), lambda h, i: (i, h)),
                  pl.BlockSpec((mlen, hd), lambda h, i: (0, h)),
                  pl.BlockSpec((mlen, hd), lambda h, i: (0, MEM_HEADS + h))],
        out_specs=pl.BlockSpec((tq, hd), lambda h, i: (i, h)),
        out_shape=jax.ShapeDtypeStruct((s, d), BF),
        compiler_params=_params(("parallel", "parallel")),
    )(q_m, kv, kv)


def _mem_bwd(q_m, kv, do_m, tq=2048):
    s, d = q_m.shape
    mlen = kv.shape[0]
    hd = d // MEM_HEADS
    tq = _tile(s, tq)
    scale = 1.0 / math.sqrt(hd)

    def body(q_ref, k_ref, v_ref, do_ref, dq_ref, dk_ref, dv_ref):
        q, k, v, do = q_ref[...], k_ref[...], v_ref[...], do_ref[...]
        p = _mem_probs(q, k, scale)
        dp = lax.dot_general(do, v, NT, preferred_element_type=F32)
        ds = p * (dp - jnp.sum(dp * p, axis=-1, keepdims=True)) * scale
        dsb = ds.astype(BF)
        dq_ref[...] = jnp.dot(dsb, k, preferred_element_type=F32).astype(dq_ref.dtype)
        dk = lax.dot_general(dsb, q, TN, preferred_element_type=F32)
        dv = lax.dot_general(p.astype(BF), do, TN, preferred_element_type=F32)
        first = pl.program_id(1) == 0

        @pl.when(first)
        def _():
            dk_ref[...] = dk
            dv_ref[...] = dv

        @pl.when(jnp.logical_not(first))
        def _():
            dk_ref[...] += dk
            dv_ref[...] += dv

    tok = pl.BlockSpec((tq, hd), lambda h, i: (i, h))
    memb = pl.BlockSpec((mlen, hd), lambda h, i: (0, h))
    return pl.pallas_call(
        body, name="mem_attn_bwd", grid=(MEM_HEADS, s // tq),
        in_specs=[tok, memb, pl.BlockSpec((mlen, hd), lambda h, i: (0, MEM_HEADS + h)), tok],
        out_specs=[tok, memb, memb],
        out_shape=[jax.ShapeDtypeStruct((s, d), BF), jax.ShapeDtypeStruct((mlen, d), F32),
                   jax.ShapeDtypeStruct((mlen, d), F32)],
        compiler_params=_params(("parallel", "arbitrary")),
    )(q_m, kv, kv, do_m)


def _place():
    x, y, c = lax.axis_index("x"), lax.axis_index("y"), lax.axis_index("c")
    other_chips = [(1 - x, y), (x, 1 - y), (1 - x, 1 - y)]
    return x, y, c, other_chips


def _chip_no(cx, cy):
    return 2 * cx + cy


HBM_SPEC = pl.BlockSpec(memory_space=pl.ANY)


def _cast_place(shard, axis, place, dtype, name, after=()):
    r, c = shard.shape
    tr = _tile(r, max(16, 1048576 // c))
    nblk = r // tr
    if axis == 1:
        full, out_map = (r, N_CHIPS * c), lambda i, pref: (i, pref[0])
    else:
        full, out_map = (N_CHIPS * r, c), lambda i, pref: (pref[0] * nblk + i, 0)

    def body(pref, s_ref, *rest):
        o_ref = rest[-1]
        o_ref[...] = s_ref[...].astype(o_ref.dtype)

    return pl.pallas_call(
        body, name=name,
        grid_spec=pltpu.PrefetchScalarGridSpec(
            num_scalar_prefetch=1, grid=(nblk,),
            in_specs=[pl.BlockSpec((tr, c), lambda i, pref: (i, 0))] + [HBM_SPEC] * len(after),
            out_specs=pl.BlockSpec((tr, c), out_map)),
        out_shape=jax.ShapeDtypeStruct(full, dtype),
        compiler_params=_params(("parallel",)),
    )(place, shard, *after)


def _region(ref, axis, chip_no, half):
    width = ref.shape[axis] // N_CHIPS
    start = pl.multiple_of(chip_no * width, width)
    if axis == 1:
        if half is None:
            return ref.at[:, pl.ds(start, width)]
        hr = ref.shape[0] // 2
        return ref.at[pl.ds(pl.multiple_of(half * hr, hr), hr), pl.ds(start, width)]
    if half is None:
        return ref.at[pl.ds(start, width), :]
    hr = width // 2
    return ref.at[pl.ds(pl.multiple_of(start + half * hr, hr), hr), :]


def _gather_weights(fulls, axes, split):
    n = len(fulls)

    def body(*refs):
        outs = refs[n:2 * n]
        send, recv, fsend, frecv = refs[2 * n:]
        x, y, c, others = _place()
        me = _chip_no(x, y)
        sibling = (x, y, 1 - c)

        def copy(w, chip_no, half, sems, p, to):
            reg = _region(outs[w], axes[w], chip_no, half)
            return pltpu.make_async_remote_copy(
                src_ref=reg, dst_ref=reg, send_sem=sems[0].at[w, p], recv_sem=sems[1].at[w, p],
                device_id=to, device_id_type=MESH)

        for w in range(n):
            for p, chip in enumerate(others):
                copy(w, me, c if split[w] else None, (send, recv), p, (chip[0], chip[1], c)).start()
        for w in range(n):
            for p, chip in enumerate(others):
                half = c if split[w] else None
                copy(w, _chip_no(*chip), half, (send, recv), p, (chip[0], chip[1], c)).wait_recv()
                if split[w]:
                    copy(w, _chip_no(*chip), c, (fsend, frecv), p, sibling).start()
        for w in range(n):
            for p, chip in enumerate(others):
                copy(w, me, c if split[w] else None, (send, recv), p, (chip[0], chip[1], c)).wait_send()
                if split[w]:
                    handed = copy(w, _chip_no(*chip), 1 - c, (fsend, frecv), p, sibling)
                    handed.wait_recv()
                    handed.wait_send()

    return pl.pallas_call(
        body, name="gather_weights",
        in_specs=[HBM_SPEC] * n, out_specs=[HBM_SPEC] * n,
        out_shape=[jax.ShapeDtypeStruct(f.shape, f.dtype) for f in fulls],
        input_output_aliases={i: i for i in range(n)},
        scratch_shapes=[pltpu.SemaphoreType.DMA((n, 3))] * 4,
        compiler_params=pltpu.CompilerParams(has_side_effects=True),
    )(*fulls)


def _fetch_copy(refs, axes, whole, send, recv, w, p, chip, c, arriving):
    owner = _chip_no(*chip) if arriving else _chip_no(lax.axis_index("x"), lax.axis_index("y"))
    reg = _region(refs[w], axes[w], owner, None if whole[w] else c)
    return pltpu.make_async_remote_copy(
        src_ref=reg, dst_ref=reg, send_sem=send[p], recv_sem=recv[p],
        device_id=(chip[0], chip[1], c), device_id_type=MESH)


N_PEERS = N_CHIPS - 1


class _Rider:
    def __init__(self, arrays, outs, aliases, start, wait):
        self.arrays, self.outs, self.aliases, self.start, self.wait = list(arrays), list(outs), aliases, start, wait
        self.scratch = [pltpu.SemaphoreType.DMA((len(self.arrays), N_PEERS))] * 2

    def run(self, refs, n_in, n_out, first, last):
        ra, ro = len(self.arrays), len(self.outs)
        ins = refs[n_in:n_in + ra]
        outs = refs[n_in + ra + n_out:n_in + ra + n_out + ro]
        send, recv = refs[-2], refs[-1]

        @pl.when(first)
        def _():
            self.start(ins, outs, send, recv)

        @pl.when(last)
        def _():
            self.wait(ins, outs, send, recv)

    def call_args(self, n_in, n_out):
        ra = len(self.arrays)
        return dict(in_specs=[HBM_SPEC] * ra, out_specs=[HBM_SPEC] * len(self.outs), out_shape=self.outs,
                    aliases={n_in + k: n_out + o for k, o in self.aliases.items()}, scratch=self.scratch)


def _host_refs(refs, rider, n_in, n_out):
    ra = len(rider.arrays) if rider else 0
    return refs[:n_in], refs[n_in + ra:n_in + ra + n_out]


def _riding_fetch(fulls, axes):
    n = len(fulls)
    whole = [False] * n

    def sems(ref, w):
        return [ref.at[w, q] for q in range(N_PEERS)]

    def start(ins, outs, send, recv):
        _, _, c, others = _place()
        for w in range(n):
            for p, chip in enumerate(others):
                _fetch_copy(outs, axes, whole, sems(send, w), sems(recv, w), w, p, chip, c, False).start()

    def wait(ins, outs, send, recv):
        _, _, c, others = _place()
        for w in range(n):
            for p, chip in enumerate(others):
                _fetch_copy(outs, axes, whole, sems(send, w), sems(recv, w), w, p, chip, c, False).wait_send()
                _fetch_copy(outs, axes, whole, sems(send, w), sems(recv, w), w, p, chip, c, True).wait_recv()

    return _Rider(fulls, [jax.ShapeDtypeStruct(f.shape, f.dtype) for f in fulls], {k: k for k in range(n)}, start, wait)


def _hand_on(fulls, axes, name):
    n = len(fulls)

    def body(*refs):
        outs = refs[n:2 * n]
        send, recv = refs[2 * n:]
        x, y, c, others = _place()

        def copy(w, p, chip, half):
            reg = _region(outs[w], axes[w], _chip_no(*chip), half)
            return pltpu.make_async_remote_copy(
                src_ref=reg, dst_ref=reg, send_sem=send.at[w, p], recv_sem=recv.at[w, p],
                device_id=(x, y, 1 - c), device_id_type=MESH)

        for w in range(n):
            for p, chip in enumerate(others):
                copy(w, p, chip, c).start()
        for w in range(n):
            for p, chip in enumerate(others):
                copy(w, p, chip, 1 - c).wait()

    return pl.pallas_call(
        body, name=name,
        in_specs=[HBM_SPEC] * n, out_specs=[HBM_SPEC] * n,
        out_shape=[jax.ShapeDtypeStruct(f.shape, f.dtype) for f in fulls],
        input_output_aliases={i: i for i in range(n)},
        scratch_shapes=[pltpu.SemaphoreType.DMA((n, 3)), pltpu.SemaphoreType.DMA((n, 3))],
        compiler_params=pltpu.CompilerParams(has_side_effects=True),
    )(*fulls)


def _pair_exchange(grads, name):
    n = len(grads)

    def body(*refs):
        ins, outs = refs[:n], refs[n:2 * n]
        send, recv = refs[2 * n:]
        x, y, c, _ = _place()
        cps = []
        for w in range(n):
            cp = pltpu.make_async_remote_copy(
                src_ref=ins[w].at[:, 1 - c], dst_ref=outs[w], send_sem=send.at[w], recv_sem=recv.at[w],
                device_id=(x, y, 1 - c), device_id_type=MESH)
            cp.start()
            cps.append(cp)
        for cp in cps:
            cp.wait()

    return pl.pallas_call(
        body, name=name,
        in_specs=[HBM_SPEC] * n, out_specs=[HBM_SPEC] * n,
        out_shape=[jax.ShapeDtypeStruct((g.shape[0],) + g.shape[2:], g.dtype) for g in grads],
        scratch_shapes=[pltpu.SemaphoreType.DMA((n,)), pltpu.SemaphoreType.DMA((n,))],
        compiler_params=pltpu.CompilerParams(has_side_effects=True),
    )(*grads)


def _pair_add(g4, got, core, name):
    nj, _, hr, cdim = g4.shape
    tr = _tile(hr, max(8, 524288 // cdim))

    def body(core_ref, a_ref, b_ref, o_ref):
        o_ref[...] = (a_ref[...].astype(F32) + b_ref[...].astype(F32)).astype(o_ref.dtype)

    return pl.pallas_call(
        body, name=name,
        grid_spec=pltpu.PrefetchScalarGridSpec(
            num_scalar_prefetch=1, grid=(nj, hr // tr),
            in_specs=[pl.BlockSpec((1, None, tr, cdim), lambda j, i, core_ref: (j, core_ref[0], i, 0)),
                      pl.BlockSpec((1, tr, cdim), lambda j, i, core_ref: (j, i, 0))],
            out_specs=pl.BlockSpec((1, tr, cdim), lambda j, i, core_ref: (j, i, 0))),
        out_shape=jax.ShapeDtypeStruct((nj, hr, cdim), BF),
        compiler_params=_params(("parallel", "parallel")),
    )(core, g4, got)


def _piece(ref, axis, j, hc):
    if axis == 0:
        return ref.at[j]
    return ref.at[0, :, pl.ds(pl.multiple_of(j * hc, hc), hc)]


def _slot_shapes(sums, axes):
    return [(N_CHIPS - 1, sm.shape[1], sm.shape[2] // (1 if ax == 0 else N_CHIPS)) for sm, ax in zip(sums, axes)]


def _slot_copy(sums, lands, axes, send, recv, w, p, chip, c):
    return pltpu.make_async_remote_copy(
        src_ref=_piece(sums[w], axes[w], _chip_no(*chip), lands[w].shape[2]), dst_ref=lands[w].at[p],
        send_sem=send[p], recv_sem=recv[p],
        device_id=(chip[0], chip[1], c), device_id_type=MESH)


def _riding_slots(sums, axes):
    n = len(sums)
    shapes = _slot_shapes(sums, axes)

    def copies(ins, outs, send, recv):
        _, _, c, others = _place()
        return [_slot_copy(ins, outs, axes, [send.at[w, q] for q in range(N_PEERS)],
                           [recv.at[w, q] for q in range(N_PEERS)], w, p, chip, c)
                for w in range(n) for p, chip in enumerate(others)]

    def start(ins, outs, send, recv):
        for cp in copies(ins, outs, send, recv):
            cp.start()

    def wait(ins, outs, send, recv):
        for cp in copies(ins, outs, send, recv):
            cp.wait()

    return _Rider(sums, [jax.ShapeDtypeStruct(sh, sm.dtype) for sh, sm in zip(shapes, sums)], {}, start, wait)


def _chip_sum(psum, slots, axis, place, name):
    _, hr, hc = slots.shape
    tr = _tile(hr, 256)
    own_map = (lambda i, pref: (0, i, pref[0])) if axis == 1 else (lambda i, pref: (pref[0], i, 0))

    def body(pref, own_ref, s_ref, o_ref):
        o_ref[...] = ((own_ref[...].astype(F32) + s_ref[0].astype(F32)) + s_ref[1].astype(F32)) + s_ref[2].astype(F32)

    return pl.pallas_call(
        body, name=name,
        grid_spec=pltpu.PrefetchScalarGridSpec(
            num_scalar_prefetch=1, grid=(hr // tr,),
            in_specs=[pl.BlockSpec((None, tr, hc), own_map),
                      pl.BlockSpec((N_CHIPS - 1, tr, hc), lambda i, pref: (0, i, 0))],
            out_specs=pl.BlockSpec((None, tr, hc), lambda i, pref: (pref[1], i, 0))),
        out_shape=jax.ShapeDtypeStruct((2, hr, hc), F32),
        compiler_params=_params(("parallel",)),
    )(place, psum, slots)


def _half_swap(both):
    n = len(both)

    def body(*refs):
        outs = refs[n:2 * n]
        send, recv = refs[2 * n:]
        x, y, c, _ = _place()

        def copy(w, half):
            return pltpu.make_async_remote_copy(
                src_ref=outs[w].at[half], dst_ref=outs[w].at[half], send_sem=send.at[w], recv_sem=recv.at[w],
                device_id=(x, y, 1 - c), device_id_type=MESH)

        for w in range(n):
            copy(w, c).start()
        for w in range(n):
            copy(w, 1 - c).wait()

    return pl.pallas_call(
        body, name="grad_half_swap",
        in_specs=[HBM_SPEC] * n, out_specs=[HBM_SPEC] * n,
        out_shape=[jax.ShapeDtypeStruct(b.shape, b.dtype) for b in both],
        input_output_aliases={i: i for i in range(n)},
        scratch_shapes=[pltpu.SemaphoreType.DMA((n,)), pltpu.SemaphoreType.DMA((n,))],
        compiler_params=pltpu.CompilerParams(has_side_effects=True),
    )(*both)


def _allreduce_small(pack):
    rows, d = pack.shape

    def body(p_ref, o_ref, slots, send, recv):
        x, y, c, _ = _place()
        me = 4 * x + 2 * y + c
        slots[me] = p_ref[...]
        cps = []
        for k in range(1, N_DEV):
            px, py, pc = x ^ (k >> 2), y ^ ((k >> 1) & 1), c ^ (k & 1)
            cp = pltpu.make_async_remote_copy(
                src_ref=p_ref, dst_ref=slots.at[me], send_sem=send.at[k - 1], recv_sem=recv.at[k - 1],
                device_id=(px, py, pc), device_id_type=MESH)
            cp.start()
            cps.append(cp)
        for k in range(1, N_DEV):
            px, py, pc = x ^ (k >> 2), y ^ ((k >> 1) & 1), c ^ (k & 1)
            arrival = pltpu.make_async_remote_copy(
                src_ref=p_ref, dst_ref=slots.at[4 * px + 2 * py + pc], send_sem=send.at[k - 1],
                recv_sem=recv.at[k - 1], device_id=(px, py, pc), device_id_type=MESH)
            arrival.wait_recv()
            arrival.wait_send()
        acc = slots[0]
        for k in range(1, N_DEV):
            acc = acc + slots[k]
        o_ref[...] = acc

    vm = pl.BlockSpec(memory_space=pltpu.VMEM)
    return pl.pallas_call(
        body, name="allreduce_small", in_specs=[vm], out_specs=vm,
        out_shape=jax.ShapeDtypeStruct((rows, d), F32),
        scratch_shapes=[pltpu.VMEM((N_DEV, rows, d), F32), pltpu.SemaphoreType.DMA((N_DEV - 1,)),
                        pltpu.SemaphoreType.DMA((N_DEV - 1,))],
        compiler_params=pltpu.CompilerParams(has_side_effects=True),
    )(pack)


def _adamw(w, g, m, v, name):
    rows, cols = w.shape

    def fn(wv, gv, mv, vv):
        m2 = ADAM_B1 * mv + (1.0 - ADAM_B1) * gv
        v2 = ADAM_B2 * vv + (1.0 - ADAM_B2) * (gv * gv)
        m_hat = m2 / (1.0 - ADAM_B1 ** ADAM_STEP)
        v_hat = v2 / (1.0 - ADAM_B2 ** ADAM_STEP)
        delta = -ADAM_LR * (m_hat / (jnp.sqrt(v_hat) + ADAM_EPS) + ADAM_WD * wv)
        return delta, m2, v2

    ins = [(a, cols, 0) for a in (w, g, m, v)]
    return _rowwise(fn, ins, [(cols, F32)] * 3, rows=rows, tm=_tile(rows, max(8, 262144 // cols)), name=name)


BIG = ["w_in", "w_branch_a", "w_branch_b", "w_mix_out", "w_mem_q", "w_mem_kv", "w_mem_o", "w_ffn_in", "w_ffn_out"]
BIG_AXIS = {"w_in": 1, "w_branch_a": 1, "w_branch_b": 1, "w_mix_out": 0, "w_mem_q": 0, "w_mem_kv": 1,
            "w_mem_o": 0, "w_ffn_in": 1, "w_ffn_out": 0}
NORMS = ["norm_mix", "norm_mem_q", "norm_mem_kv", "norm_ffn", "norm_final"]
ORDER = ["norm_mix", "w_in", "conv_w", "w_branch_a", "w_branch_b", "w_mix_out", "norm_mem_q", "norm_mem_kv",
         "w_mem_q", "w_mem_kv", "w_mem_o", "norm_ffn", "w_ffn_in", "w_ffn_out", "norm_final"]


def _pack_small(vals, conv):
    d = vals[0].shape[-1]
    rows = [v.reshape(1, d) for v in vals]
    conv = jnp.pad(conv, ((0, 0), (0, d - conv.shape[1])))
    pad = jnp.zeros((SMALL_ROWS - len(rows) - CONV_K, d), F32)
    return jnp.concatenate(rows + [conv, pad], axis=0)


def kernel(x, mem, norm_mix, w_in, conv_w, w_branch_a, w_branch_b, w_mix_out, norm_mem_q, norm_mem_kv, w_mem_q, w_mem_kv, w_mem_o, norm_ffn, w_ffn_in, w_ffn_out, norm_final, loss_target, m_norm_mix, m_w_in, m_conv_w, m_w_branch_a, m_w_branch_b, m_w_mix_out, m_norm_mem_q, m_norm_mem_kv, m_w_mem_q, m_w_mem_kv, m_w_mem_o, m_norm_ffn, m_w_ffn_in, m_w_ffn_out, m_norm_final, v_norm_mix, v_w_in, v_conv_w, v_w_branch_a, v_w_branch_b, v_w_mix_out, v_norm_mem_q, v_norm_mem_kv, v_w_mem_q, v_w_mem_kv, v_w_mem_o, v_norm_ffn, v_w_ffn_in, v_w_ffn_out, v_norm_final):
    args = dict(locals())
    wts = {n: args[n] for n in ORDER}
    mom = {n: args["m_" + n] for n in ORDER}
    var = {n: args["v_" + n] for n in ORDER}
    x = x[0]
    mem = mem[0]
    target = loss_target[0]
    s, d = x.shape
    gains = {n: wts[n].reshape(1, d) for n in NORMS}
    chip = 2 * lax.axis_index("x") + lax.axis_index("y")
    core = lax.axis_index("c").astype(jnp.int32).reshape(1)
    place = jnp.stack([chip, lax.axis_index("c")]).astype(jnp.int32)

    conv_shard = jnp.pad(conv_w[0], ((0, CONV_ROWS - CONV_K), (0, 0)))
    first = [_cast_place(wts["w_in"][0], BIG_AXIS["w_in"], place, BF, "place_w_in"),
             _cast_place(conv_shard, 1, place, F32, "place_conv_w")]
    w_in_full, conv_full = _gather_weights(first, [BIG_AXIS["w_in"], 1], [True, False])
    W = {"w_in": w_in_full}
    ffn_w = ["w_ffn_in", "w_ffn_out"]
    mid_w = [n for n in BIG if n not in ffn_w + ["w_in"]]
    placed = {n: _cast_place(wts[n][0], BIG_AXIS[n], place, BF, "place_" + n) for n in mid_w + ffn_w}

    def fetch(names):
        return _riding_fetch([placed[n] for n in names], [BIG_AXIS[n] for n in names])

    h1 = _rmsnorm(x, gains["norm_mix"], "norm_mix_fwd")
    proj, *mid_bufs = _matmul(h1, W["w_in"], tn=1280, name="in_proj", rider=fetch(mid_w))
    o_a, o_a32, *ffn_bufs = _sb_fwd(proj, fetch(ffn_w))
    y_b = _conv_fwd(proj, conv_full)
    later_w = mid_w + ffn_w
    W.update(zip(later_w, _hand_on(mid_bufs + ffn_bufs, [BIG_AXIS[n] for n in later_w], "gather_hand_on")))
    br_a, br_b, merged = _branches_merge(o_a, y_b, W["w_branch_a"], W["w_branch_b"], proj)
    x1 = _matmul(merged, W["w_mix_out"], tn=1024, out_dtype=F32, resid=x, name="mix_out")

    hq, q_m = _norm_matmul(x1, gains["norm_mem_q"], W["w_mem_q"], tn=1024, name="norm_mem_q")
    mn, kv = _norm_matmul(mem, gains["norm_mem_kv"], W["w_mem_kv"], tn=1024, name="norm_mem_kv")
    o_m = _mem_fwd(q_m, kv)
    x2 = _matmul(o_m, W["w_mem_o"], tn=1024, out_dtype=F32, resid=x1, name="mem_o")

    hf, gate, up, act = _norm_ffn_in_swiglu(x2, gains["norm_ffn"], W["w_ffn_in"])

    dx3, dx3_b, dg_final, loss_part = _ffn_out_loss(act, W["w_ffn_out"], x2, gains["norm_final"], target)

    dgu = _d_act_swiglu(dx3_b, W["w_ffn_out"], gate, up)
    gw = {"w_ffn_out": _matmul(act, dx3_b, ta=True, tm=1408, tn=512, name="gw_ffn_out")}
    gw["w_ffn_in"] = _matmul(hf, dgu, ta=True, tn=512, name="gw_ffn_in")

    def pair_sums(names, tag):
        views = []
        for n in names:
            r, cdim = gw[n].shape
            views.append(gw[n].reshape(1, 2, r // 2, cdim) if BIG_AXIS[n] == 1
                         else gw[n].reshape(N_CHIPS, 2, r // (2 * N_CHIPS), cdim))
        got = _pair_exchange(views, "grad_pair_exchange_" + tag)
        return [_pair_add(v, g, core, "pair_add_" + n) for n, v, g in zip(names, views, got)]

    def chip_sums(names, sums, slots):
        return [_chip_sum(sm, sl, BIG_AXIS[n], place, "chip_sum_" + n) for n, sm, sl in zip(names, sums, slots)]

    dx2, dx2_b, dg_ffn = _matmul_norm_bwd(dgu, W["w_ffn_in"], x2, gains["norm_ffn"], dx3, tm=256, bf_copy=True,
                                          name="d_hf_norm_bwd")

    do_m = _matmul(dx2_b, W["w_mem_o"], tb=True, tn=1024, name="d_o_m")
    gw["w_mem_o"] = _matmul(o_m, dx2_b, ta=True, tn=512, name="gw_mem_o")
    dq_m, dk_m, dv_m = _mem_bwd(q_m, kv, do_m)
    dkv = jnp.concatenate([dk_m, dv_m], axis=-1)
    dx1, dx1_b, dg_q = _matmul_norm_bwd(dq_m, W["w_mem_q"], x1, gains["norm_mem_q"], dx2, tm=512, bf_copy=True,
                                        name="d_hq_norm_bwd")
    gw["w_mem_q"] = _matmul(hq, dq_m, ta=True, tn=512, name="gw_mem_q")
    dmn = _matmul(dkv, W["w_mem_kv"], tb=True, tn=1024, out_dtype=F32, name="d_mn")
    gw["w_mem_kv"] = _matmul(mn, dkv, ta=True, tn=1024, name="gw_mem_kv")
    _, dg_kv = _rmsnorm_bwd(mem, gains["norm_mem_kv"], dmn, None, "norm_mem_kv_bwd")

    dbr_a, dbr_b, dga, dgb = _d_merged_gates(dx1_b, W["w_mix_out"], proj, br_a, br_b)
    gw["w_mix_out"] = _matmul(merged, dx1_b, ta=True, tn=512, name="gw_mix_out")
    do_a = _matmul(dbr_a, W["w_branch_a"], tb=True, name="d_o_a")
    gw["w_branch_a"] = _matmul(o_a, dbr_a, ta=True, tn=512, name="gw_branch_a")
    dy_b = _matmul(dbr_b, W["w_branch_b"], tb=True, name="d_y_b")
    gw["w_branch_b"] = _matmul(y_b, dbr_b, ta=True, tn=512, name="gw_branch_b")
    du, dgate_b, dgate_c, dconv = _conv_bwd(proj, conv_full, dy_b)
    early = [n for n in BIG if n != "w_in"]
    early_sums = pair_sums(early, "early")
    dq, dk, dv, *early_slots = _sb_bwd(proj, o_a32, do_a, _riding_slots(early_sums, [BIG_AXIS[n] for n in early]))

    def assemble(*parts):
        return jnp.concatenate([p.astype(BF) for p in parts], axis=-1)

    hw = dq.shape[1]
    dproj = _rowwise(assemble, [(t, hw, 0) for t in (dq, dk, dv, du, dgate_b, dgate_c)] + [(dga, d, 0), (dgb, d, 0)],
                     [(proj.shape[1], BF)], rows=s, tm=256, name="assemble_dproj")[0]
    gw["w_in"] = _matmul(h1, dproj, ta=True, tn=640, name="gw_in")
    in_sums = pair_sums(["w_in"], "in")
    grad_x, dg_mix, *in_slots = _matmul_norm_bwd(dproj, W["w_in"], x, gains["norm_mix"], dx1, tm=256, bf_copy=False,
                                                 name="d_h1_norm_bwd",
                                                 rider=_riding_slots(in_sums, [BIG_AXIS["w_in"]]))

    halves = dict(zip(early, chip_sums(early, early_sums, early_slots)))
    halves.update(zip(["w_in"], chip_sums(["w_in"], in_sums, in_slots)))
    both = _half_swap([halves[n] for n in BIG])
    grads = {n: b.reshape(wts[n].shape[1:]) for n, b in zip(BIG, both)}

    small_g = [dg_mix, dg_q, dg_kv, dg_ffn, dg_final]
    pack = _pack_small(small_g, dconv[:CONV_K])
    pack = pack.at[ROW_LOSS].set(jnp.broadcast_to(loss_part[0, :1], (d,)))
    red = _allreduce_small(pack)
    loss = red[ROW_LOSS, 0]
    cw = conv_w.shape[2]
    conv_g = lax.dynamic_slice(red, (ROW_CONV, chip * cw), (CONV_K, cw))
    small_grad = _pack_small([red[i] for i in range(len(NORMS))], conv_g)
    small = [_pack_small([t[n] for n in NORMS], t["conv_w"][0]) for t in (wts, mom, var)]
    s_delta, s_m, s_v = _adamw(small[0], small_grad, small[1], small[2], "adamw_small")

    out = {"grad": {}, "delta": {}, "new_m": {}, "new_v": {}}
    for n in BIG:
        shp = wts[n].shape
        dl, m2, v2 = _adamw(wts[n][0], grads[n], mom[n][0], var[n][0], "adamw_" + n)
        out["grad"][n] = grads[n].reshape(shp)
        out["delta"][n], out["new_m"][n], out["new_v"][n] = dl.reshape(shp), m2.reshape(shp), v2.reshape(shp)
    for key, blk in (("grad", small_grad), ("delta", s_delta), ("new_m", s_m), ("new_v", s_v)):
        for i, n in enumerate(NORMS):
            out[key][n] = blk[i].reshape(wts[n].shape)
        out[key]["conv_w"] = blk[ROW_CONV:ROW_CONV + CONV_K, :cw].reshape(conv_w.shape)

    return (loss, grad_x[None], *[out["grad"][n] for n in ORDER], *[out["delta"][n] for n in ORDER],
            *[out["new_m"][n] for n in ORDER], *[out["new_v"][n] for n in ORDER])
```

```python
import functools
import math

import jax
import jax.numpy as jnp
from jax import lax
from jax.experimental import pallas as pl
from jax.experimental.pallas import tpu as pltpu

BF = jnp.bfloat16
F32 = jnp.float32
MESH = pl.DeviceIdType.MESH

SB_HEAD_DIM = 64
LANES = 128
MEM_HEADS = 4
CONV_K = 3
CONV_ROWS = 8
EPS = 1e-6
N_CHIPS = 4
N_DEV = 8
VMEM_LIMIT = 56 * 1024 * 1024

ADAM_LR = 0.001
ADAM_B1 = 0.9
ADAM_B2 = 0.999
ADAM_EPS = 1e-08
ADAM_WD = 0.01
ADAM_STEP = 10

SMALL_ROWS = 16
ROW_CONV = 5
ROW_LOSS = 8


def _params(sem=None, **kw):
    return pltpu.CompilerParams(dimension_semantics=sem, vmem_limit_bytes=VMEM_LIMIT, **kw)


def _tile(dim, pref):
    if dim <= pref:
        return dim
    for step in (LANES, 8):
        t = (pref // step) * step
        while t >= step:
            if dim % t == 0:
                return t
            t -= step
    raise ValueError(f"no tile of {dim} under {pref}")


def _matmul(a, b, *, ta=False, tb=False, tm=1024, tn=512, out_dtype=BF, resid=None, rider=None, name):
    if ta:
        kdim, m = a.shape
    else:
        m, kdim = a.shape
    n = b.shape[0] if tb else b.shape[1]
    tm, tn = _tile(m, tm), _tile(n, tn)
    a_spec = pl.BlockSpec((kdim, tm), lambda i, j: (0, i)) if ta else pl.BlockSpec((tm, kdim), lambda i, j: (i, 0))
    b_spec = pl.BlockSpec((tn, kdim), lambda i, j: (j, 0)) if tb else pl.BlockSpec((kdim, tn), lambda i, j: (0, j))
    o_spec = pl.BlockSpec((tm, tn), lambda i, j: (i, j))
    dims = (((0 if ta else 1,), (1 if tb else 0,)), ((), ()))
    has_res = resid is not None
    n_in = 2 + has_res

    def body(*refs):
        ins, (o_ref,) = _host_refs(refs, rider, n_in, 1)
        av, bv = ins[0][...], ins[1][...]
        if av.dtype != BF:
            av = av.astype(BF)
        if bv.dtype != BF:
            bv = bv.astype(BF)
        acc = lax.dot_general(av, bv, dims, preferred_element_type=F32)
        if has_res:
            acc = ins[2][...] + acc
        o_ref[...] = acc.astype(o_ref.dtype)
        if rider:
            rider.run(refs, n_in, 1, (pl.program_id(0) == 0) & (pl.program_id(1) == 0),
                      (pl.program_id(0) == m // tm - 1) & (pl.program_id(1) == n // tn - 1))

    res = pl.pallas_call(
        body, name=name, grid=(m // tm, n // tn),
        compiler_params=_params(("arbitrary", "arbitrary") if rider else ("parallel", "parallel"),
                                has_side_effects=rider is not None),
        **_with_rider(rider, n_in, 1, [a_spec, b_spec] + ([o_spec] if has_res else []), [o_spec],
                      [jax.ShapeDtypeStruct((m, n), out_dtype)]),
    )(*([a, b] + ([resid] if has_res else []) + (rider.arrays if rider else [])))
    return res if rider else res[0]


def _rowwise(fn, ins, outs, *, rows, tm, name, accs=(), after=()):
    tm = _tile(rows, tm)
    in_specs, args = [], []
    for arr, cols, cb in ins:
        if cols is None:
            in_specs.append(pl.BlockSpec(arr.shape, lambda i, nd=arr.ndim: (0,) * nd))
        else:
            in_specs.append(pl.BlockSpec((tm, cols), lambda i, cb=cb: (i, cb)))
        args.append(arr)
    out_specs = [pl.BlockSpec((tm, cols), lambda i: (i, 0)) for cols, _ in outs]
    out_shape = [jax.ShapeDtypeStruct((rows, cols), dt) for cols, dt in outs]
    for r, c in accs:
        out_specs.append(pl.BlockSpec((r, c), lambda i: (0, 0)))
        out_shape.append(jax.ShapeDtypeStruct((r, c), F32))
    n_in, n_out = len(ins), len(outs)
    in_specs += [HBM_SPEC] * len(after)
    args += list(after)

    def body(*refs):
        res = fn(*[r[...] for r in refs[:n_in]])
        if not isinstance(res, (tuple, list)):
            res = (res,)
        refs = refs[n_in + len(after):]
        for o_ref, val in zip(refs[:n_out], res[:n_out]):
            o_ref[...] = val.astype(o_ref.dtype)
        first = pl.program_id(0) == 0
        for a_ref, val in zip(refs[n_out:], res[n_out:]):
            @pl.when(first)
            def _(a_ref=a_ref, val=val):
                a_ref[...] = val

            @pl.when(jnp.logical_not(first))
            def _(a_ref=a_ref, val=val):
                a_ref[...] += val

    res = pl.pallas_call(
        body, name=name, grid=(rows // tm,), in_specs=in_specs, out_specs=out_specs, out_shape=out_shape,
        compiler_params=_params(("arbitrary",) if accs else ("parallel",)),
    )(*args)
    return res


def _rstd(xf):
    return lax.rsqrt(jnp.mean(xf * xf, axis=-1, keepdims=True) + EPS)


def _rmsnorm(x, g, name, after=()):
    rows, d = x.shape
    return _rowwise(lambda xv, gv: xv * _rstd(xv) * gv, [(x, d, 0), (g, None, None)], [(d, BF)],
                    rows=rows, tm=512, name=name, after=after)[0]


def _rmsnorm_bwd(x, g, dy, resid, name, bf_copy=False):
    rows, d = x.shape

    def fn(xv, gv, dyv, *rest):
        dyv = dyv.astype(F32)
        r = _rstd(xv)
        xn = xv * r
        dxn = dyv * gv
        dx = r * (dxn - xn * jnp.mean(dxn * xn, axis=-1, keepdims=True))
        if rest:
            dx = rest[0] + dx
        return (dx,) * (1 + bf_copy) + (jnp.sum(dyv * xn, axis=0, keepdims=True),)

    ins = [(x, d, 0), (g, None, None), (dy, d, 0)] + ([(resid, d, 0)] if resid is not None else [])
    outs = [(d, F32)] + ([(d, BF)] if bf_copy else [])
    return _rowwise(fn, ins, outs, rows=rows, tm=512, name=name, accs=[(1, d)])


def _matmul_norm_bwd(a, w, x, g, resid, *, tm, bf_copy, name, rider=None):
    m, kdim = a.shape
    d = w.shape[0]
    tm = _tile(m, tm)
    row = lambda i: (i, 0)
    whole = lambda i: (0, 0)
    n_out = 2 + bf_copy

    def body(*refs):
        (a_ref, w_ref, x_ref, g_ref, r_ref), outs = _host_refs(refs, rider, 5, n_out)
        dy = lax.dot_general(a_ref[...], w_ref[...], NT, preferred_element_type=F32)
        xv = x_ref[...]
        r = _rstd(xv)
        xn = xv * r
        dxn = dy * g_ref[...]
        dx = r_ref[...] + r * (dxn - xn * jnp.mean(dxn * xn, axis=-1, keepdims=True))
        outs[0][...] = dx
        if bf_copy:
            outs[1][...] = dx.astype(BF)
        dg = jnp.sum(dy * xn, axis=0, keepdims=True)
        first = pl.program_id(0) == 0

        @pl.when(first)
        def _():
            outs[-1][...] = dg

        @pl.when(jnp.logical_not(first))
        def _():
            outs[-1][...] += dg

        if rider:
            rider.run(refs, 5, n_out, first, pl.program_id(0) == m // tm - 1)

    tok = pl.BlockSpec((tm, d), row)
    out_specs = [tok] + ([tok] if bf_copy else []) + [pl.BlockSpec((1, d), whole)]
    out_shape = ([jax.ShapeDtypeStruct((m, d), F32)] + ([jax.ShapeDtypeStruct((m, d), BF)] if bf_copy else [])
                 + [jax.ShapeDtypeStruct((1, d), F32)])
    return pl.pallas_call(
        body, name=name, grid=(m // tm,),
        compiler_params=_params(("arbitrary",), has_side_effects=rider is not None),
        **_with_rider(
            rider, 5, n_out,
            [pl.BlockSpec((tm, kdim), row), pl.BlockSpec((d, kdim), whole), tok, pl.BlockSpec((1, d), whole), tok],
            out_specs, out_shape),
    )(a, w, x, g, resid, *(rider.arrays if rider else []))


def _norm_ffn_in_swiglu(x, g, w, *, tm=1024, tn=1408, rider=None):
    m, kdim = x.shape
    f = w.shape[1] // 2
    tm, tn = _tile(m, tm), _tile(f, tn)
    nj = f // tn

    def body(*refs):
        (x_ref, g_ref, wg_ref, wu_ref), (h_ref, gate_ref, up_ref, act_ref) = _host_refs(refs, rider, 4, 4)
        if rider:
            rider.run(refs, 4, 4, (pl.program_id(0) == 0) & (pl.program_id(1) == 0),
                      (pl.program_id(0) == m // tm - 1) & (pl.program_id(1) == nj - 1))

        @pl.when(pl.program_id(1) == 0)
        def _():
            xv = x_ref[...]
            h_ref[...] = (xv * _rstd(xv) * g_ref[...]).astype(h_ref.dtype)

        hv = h_ref[...]
        gate = jnp.dot(hv, wg_ref[...], preferred_element_type=F32)
        up = jnp.dot(hv, wu_ref[...], preferred_element_type=F32)
        gate_ref[...] = gate.astype(gate_ref.dtype)
        up_ref[...] = up.astype(up_ref.dtype)
        act_ref[...] = (gate * jax.nn.sigmoid(gate) * up).astype(act_ref.dtype)

    tile = pl.BlockSpec((tm, tn), lambda i, j: (i, j))
    rows = pl.BlockSpec((tm, kdim), lambda i, j: (i, 0))
    return pl.pallas_call(
        body, name="norm_ffn_in_swiglu", grid=(m // tm, nj),
        compiler_params=_params(("arbitrary", "arbitrary") if rider else ("parallel", "arbitrary"),
                                has_side_effects=rider is not None),
        **_with_rider(
            rider, 4, 4,
            [rows, pl.BlockSpec((1, kdim), lambda i, j: (0, 0)), pl.BlockSpec((kdim, tn), lambda i, j: (0, j)),
             pl.BlockSpec((kdim, tn), lambda i, j: (0, nj + j))],
            [rows, tile, tile, tile],
            [jax.ShapeDtypeStruct((m, kdim), BF)] + [jax.ShapeDtypeStruct((m, f), BF)] * 3),
    )(x, g, w, w, *(rider.arrays if rider else []))


def _norm_matmul(x, g, w, *, tm=1024, tn, name, after=()):
    m, kdim = x.shape
    n = w.shape[1]
    tm, tn = _tile(m, tm), _tile(n, tn)

    def body(x_ref, g_ref, w_ref, *rest):
        h_ref, o_ref = rest[len(after):]

        @pl.when(pl.program_id(1) == 0)
        def _():
            xv = x_ref[...]
            h_ref[...] = (xv * _rstd(xv) * g_ref[...]).astype(h_ref.dtype)

        o_ref[...] = jnp.dot(h_ref[...], w_ref[...], preferred_element_type=F32).astype(o_ref.dtype)

    rows = pl.BlockSpec((tm, kdim), lambda i, j: (i, 0))
    return pl.pallas_call(
        body, name=name, grid=(m // tm, n // tn),
        in_specs=[rows, pl.BlockSpec((1, kdim), lambda i, j: (0, 0)), pl.BlockSpec((kdim, tn), lambda i, j: (0, j))]
        + [HBM_SPEC] * len(after),
        out_specs=[rows, pl.BlockSpec((tm, tn), lambda i, j: (i, j))],
        out_shape=[jax.ShapeDtypeStruct((m, kdim), BF), jax.ShapeDtypeStruct((m, n), BF)],
        compiler_params=_params(("parallel", "arbitrary")),
    )(x, g, w, *after)


def _ffn_out_loss(act, w, resid, g, target, *, tm=512):
    m, f = act.shape
    d = w.shape[1]
    tm = _tile(m, tm)
    row = lambda i: (i, 0)
    whole = lambda i: (0, 0)

    def body(a_ref, w_ref, r_ref, g_ref, t_ref, dx_ref, dxb_ref, dg_ref, loss_ref):
        xv = r_ref[...] + jnp.dot(a_ref[...], w_ref[...], preferred_element_type=F32)
        gv = g_ref[...]
        r = _rstd(xv)
        xn = xv * r
        err = xn * gv - t_ref[...]
        loss = 0.5 * jnp.sum(jnp.mean(err * err, axis=-1, keepdims=True), axis=0, keepdims=True)
        dyv = err * (1.0 / d)
        dxn = dyv * gv
        dx = r * (dxn - xn * jnp.mean(dxn * xn, axis=-1, keepdims=True))
        dx_ref[...] = dx
        dxb_ref[...] = dx.astype(dxb_ref.dtype)
        dg = jnp.sum(dyv * xn, axis=0, keepdims=True)
        loss_b = jnp.broadcast_to(loss, (1, LANES))
        first = pl.program_id(0) == 0

        @pl.when(first)
        def _():
            dg_ref[...] = dg
            loss_ref[...] = loss_b

        @pl.when(jnp.logical_not(first))
        def _():
            dg_ref[...] += dg
            loss_ref[...] += loss_b

    tok = pl.BlockSpec((tm, d), row)
    return pl.pallas_call(
        body, name="ffn_out_loss", grid=(m // tm,),
        in_specs=[pl.BlockSpec((tm, f), row), pl.BlockSpec((f, d), whole), tok, pl.BlockSpec((1, d), whole), tok],
        out_specs=[tok, tok, pl.BlockSpec((1, d), whole), pl.BlockSpec((1, LANES), whole)],
        out_shape=[jax.ShapeDtypeStruct((m, d), F32), jax.ShapeDtypeStruct((m, d), BF),
                   jax.ShapeDtypeStruct((1, d), F32), jax.ShapeDtypeStruct((1, LANES), F32)],
        compiler_params=_params(("arbitrary",)),
    )(act, w, resid, g, target)


def _d_act_swiglu(dx, w, gate, up, *, tm=256):
    m, d = dx.shape
    f = w.shape[0]
    tm = _tile(m, tm)
    row = lambda i: (i, 0)

    def body(dx_ref, w_ref, gate_ref, up_ref, o_ref):
        da = lax.dot_general(dx_ref[...], w_ref[...], NT, preferred_element_type=F32)
        gv, uv = gate_ref[...].astype(F32), up_ref[...].astype(F32)
        sg = jax.nn.sigmoid(gv)
        dgate = da * uv * (sg * (1.0 + gv * (1.0 - sg)))
        o_ref[...] = jnp.concatenate([dgate, da * (gv * sg)], axis=-1).astype(o_ref.dtype)

    return pl.pallas_call(
        body, name="d_act_swiglu", grid=(m // tm,),
        in_specs=[pl.BlockSpec((tm, d), row), pl.BlockSpec((f, d), lambda i: (0, 0)),
                  pl.BlockSpec((tm, f), row), pl.BlockSpec((tm, f), row)],
        out_specs=pl.BlockSpec((tm, 2 * f), row), out_shape=jax.ShapeDtypeStruct((m, 2 * f), BF),
        compiler_params=_params(("parallel",)),
    )(dx, w, gate, up)


GATE_A_BLK, GATE_B_BLK = 3, 4


def _branches_merge(o_a, y_b, w_a, w_b, proj, *, tm=1024):
    m, kdim = o_a.shape
    d = w_a.shape[1]
    tm = _tile(m, tm)
    row = lambda i: (i, 0)

    def body(a_ref, b_ref, wa_ref, wb_ref, ga_ref, gb_ref, bra_ref, brb_ref, merged_ref):
        bra = jnp.dot(a_ref[...], wa_ref[...], preferred_element_type=F32)
        brb = jnp.dot(b_ref[...], wb_ref[...], preferred_element_type=F32)
        bra_ref[...] = bra.astype(bra_ref.dtype)
        brb_ref[...] = brb.astype(brb_ref.dtype)
        merged = jax.nn.sigmoid(ga_ref[...].astype(F32)) * bra + jax.nn.sigmoid(gb_ref[...].astype(F32)) * brb
        merged_ref[...] = merged.astype(merged_ref.dtype)

    tok = pl.BlockSpec((tm, d), row)
    return pl.pallas_call(
        body, name="branches_merge", grid=(m // tm,),
        in_specs=[pl.BlockSpec((tm, kdim), row), pl.BlockSpec((tm, kdim), row),
                  pl.BlockSpec((kdim, d), lambda i: (0, 0)), pl.BlockSpec((kdim, d), lambda i: (0, 0)),
                  pl.BlockSpec((tm, d), lambda i: (i, GATE_A_BLK)), pl.BlockSpec((tm, d), lambda i: (i, GATE_B_BLK))],
        out_specs=[tok, tok, tok], out_shape=[jax.ShapeDtypeStruct((m, d), BF)] * 3,
        compiler_params=_params(("parallel",)),
    )(o_a, y_b, w_a, w_b, proj, proj)


def _d_merged_gates(dx, w_mix, proj, br_a, br_b, *, tm=512):
    m, d = dx.shape
    tm = _tile(m, tm)
    row = lambda i: (i, 0)

    def body(dx_ref, w_ref, ga_ref, gb_ref, bra_ref, brb_ref, dbra_ref, dbrb_ref, dga_ref, dgb_ref):
        dm = lax.dot_general(dx_ref[...], w_ref[...], NT, preferred_element_type=F32)
        sa, sb = jax.nn.sigmoid(ga_ref[...].astype(F32)), jax.nn.sigmoid(gb_ref[...].astype(F32))
        dbra_ref[...] = (dm * sa).astype(dbra_ref.dtype)
        dbrb_ref[...] = (dm * sb).astype(dbrb_ref.dtype)
        dga_ref[...] = (dm * bra_ref[...].astype(F32) * (sa * (1.0 - sa))).astype(dga_ref.dtype)
        dgb_ref[...] = (dm * brb_ref[...].astype(F32) * (sb * (1.0 - sb))).astype(dgb_ref.dtype)

    tok = pl.BlockSpec((tm, d), row)
    return pl.pallas_call(
        body, name="d_merged_gates", grid=(m // tm,),
        in_specs=[tok, pl.BlockSpec((d, d), lambda i: (0, 0)),
                  pl.BlockSpec((tm, d), lambda i: (i, GATE_A_BLK)), pl.BlockSpec((tm, d), lambda i: (i, GATE_B_BLK)),
                  tok, tok],
        out_specs=[tok] * 4, out_shape=[jax.ShapeDtypeStruct((m, d), BF)] * 4,
        compiler_params=_params(("parallel",)),
    )(dx, w_mix, proj, proj, br_a, br_b)


def _loss_head(x, g, target, name):
    rows, d = x.shape

    def fn(xv, gv, tv):
        r = _rstd(xv)
        xn = xv * r
        err = xn * gv - tv
        per_tok = jnp.mean(err * err, axis=-1, keepdims=True)
        loss = 0.5 * jnp.sum(per_tok, axis=0, keepdims=True)
        dyv = err * (1.0 / d)
        dxn = dyv * gv
        dx = r * (dxn - xn * jnp.mean(dxn * xn, axis=-1, keepdims=True))
        return dx, dx, jnp.sum(dyv * xn, axis=0, keepdims=True), jnp.broadcast_to(loss, (1, LANES))

    return _rowwise(fn, [(x, d, 0), (g, None, None), (target, d, 0)], [(d, F32), (d, BF)],
                    rows=rows, tm=512, name=name, accs=[(1, d), (1, LANES)])


SB_TK = 128
SB_KT = 2


def _sb_consts(tq):
    tk = SB_TK
    diff = lax.broadcasted_iota(jnp.int32, (tq, tk), 1) - lax.broadcasted_iota(jnp.int32, (tq, tk), 0)
    rj = lax.broadcasted_iota(jnp.int32, (2 * tk, 2 * tk), 0) & (tk - 1)
    cj = lax.broadcasted_iota(jnp.int32, (2 * tk, 2 * tk), 1)
    ones_half = cj >= tk
    later = jnp.where((rj > cj) | ones_half, 1.0, 0.0).astype(BF)
    later_incl = jnp.where((rj >= cj) | ones_half, 1.0, 0.0).astype(BF)
    return diff, later, later_incl


def _split_dot(val, rhs_twice):
    hi = val.astype(BF)
    lo = (val - hi.astype(F32)).astype(BF)
    return jnp.dot(jnp.concatenate([hi, lo], axis=1), rhs_twice, preferred_element_type=F32)


def _log_terms(z):
    sp = jnp.maximum(z, 0.0) + jnp.log(1.0 + jnp.exp(-jnp.abs(z)))
    return z - sp, sp


NT = (((1,), (1,)), ((), ()))
TN = (((0,), (0,)), ((), ()))


def _head_lane_masks(rows):
    lane = lax.broadcasted_iota(jnp.int32, (rows, LANES), 1)
    first = jnp.where(lane < SB_HEAD_DIM, 1.0, 0.0)
    return first.astype(BF), (1.0 - first).astype(BF)


DEAD_LOG = 104.0


def _walk_back(i, step, state, carries_of):
    def alive(st):
        c0, c1 = carries_of(st)
        return jnp.min(jnp.minimum(c0, c1)) < DEAD_LOG

    def cond(loop):
        done, live, _ = loop
        return jnp.logical_and(done < i, live)

    def body(loop):
        done, _, st = loop
        st = step(i - 1 - done, st)
        return done + 1, alive(st), st

    return lax.while_loop(cond, body, (jnp.int32(0), alive(state), state))[2]


def _tail(a, r0):
    return a if r0 == 0 else a[r0:]


def _add_tail(a, r0, delta):
    return a + delta if r0 == 0 else jnp.concatenate([a[:r0], a[r0:] + delta], axis=0)


def _both_heads(tile, masks):
    return jnp.concatenate([tile * masks[0], tile * masks[1]], axis=0)


def _with_rider(rider, n_in, n_out, in_specs, out_specs, out_shape, scratch=()):
    kw = dict(in_specs=list(in_specs), out_specs=list(out_specs), out_shape=list(out_shape),
              scratch_shapes=list(scratch), input_output_aliases={})
    if rider:
        extra = rider.call_args(n_in, n_out)
        kw["in_specs"] += extra["in_specs"]
        kw["out_specs"] += extra["out_specs"]
        kw["out_shape"] += extra["out_shape"]
        kw["scratch_shapes"] += extra["scratch"]
        kw["input_output_aliases"] = extra["aliases"]
    return kw


def _sb_fwd(proj, rider=None):
    s = proj.shape[0]
    tk, tq = SB_TK, SB_KT * SB_TK
    n_pairs = 4
    scale = 1.0 / math.sqrt(SB_HEAD_DIM)

    def body(*refs):
        (q_ref, k_ref, v_ref), (o_ref, o32_ref) = _host_refs(refs, rider, 3, 2)
        i = pl.program_id(1)
        diff, later, _ = _sb_consts(tq)
        qs = (q_ref[...].astype(F32) * scale).astype(BF)
        lane_masks = _head_lane_masks(tk)

        def step(g, state, masked):
            tiles = list(reversed(range(SB_KT)))
            chains = [(t, h) for t in tiles for h in range(2)]
            rows = {t: pl.ds(pl.multiple_of((g * SB_KT + t) * tk, tk), tk) for t in tiles}
            ks = {t: _both_heads(k_ref[rows[t], :], lane_masks) for t in tiles}
            vs = {t: _both_heads(v_ref[rows[t], :], lane_masks) for t in tiles}
            r0 = {t: t * tk if masked else 0 for t in tiles}
            allowed = {t: _tail(diff, r0[t]) < -t * tk for t in tiles}
            zs = {t: lax.dot_general(_tail(qs, r0[t]), ks[t], NT, preferred_element_type=F32) for t in tiles}
            logs = {}
            for t, h in chains:
                log_b, sp = _log_terms(zs[t][:, h * tk:(h + 1) * tk])
                logs[t, h] = (log_b, jnp.where(allowed[t], sp, 0.0) if masked else sp)
            sums = {c: _split_dot(logs[c][1], later) for c in chains}
            carries = list(state[0])
            ws = {}
            for t, h in chains:
                w = jnp.exp(logs[t, h][0] - (sums[t, h][:, :tk] + _tail(carries[h], r0[t])))
                ws[t, h] = (jnp.where(allowed[t], w, 0.0) if masked else w).astype(BF)
                carries[h] = _add_tail(carries[h], r0[t], sums[t, h][:, tk:])
            acc = state[1]
            for t in tiles:
                acc = _add_tail(acc, r0[t], jnp.dot(jnp.concatenate([ws[t, 0], ws[t, 1]], axis=1), vs[t],
                                                    preferred_element_type=F32))
            return tuple(carries), acc

        zero = jnp.zeros((tq, LANES), F32)
        state = step(i, ((zero, zero), zero), True)
        state = _walk_back(i, lambda g, st: step(g, st, False), state, lambda st: st[0])
        o_ref[...] = state[1].astype(o_ref.dtype)
        o32_ref[...] = state[1]
        if rider:
            rider.run(refs, 3, 2, (pl.program_id(0) == 0) & (i == 0),
                      (pl.program_id(0) == n_pairs - 1) & (i == s // tq - 1))

    tok = pl.BlockSpec((tq, LANES), lambda p, i: (i, p))
    return pl.pallas_call(
        body, name="sb_attn_fwd", grid=(n_pairs, s // tq),
        compiler_params=_params(("arbitrary", "arbitrary"), has_side_effects=rider is not None),
        **_with_rider(
            rider, 3, 2,
            [tok, pl.BlockSpec((s, LANES), lambda p, i: (0, n_pairs + p)),
             pl.BlockSpec((s, LANES), lambda p, i: (0, 2 * n_pairs + p))],
            [tok, tok],
            [jax.ShapeDtypeStruct((s, n_pairs * LANES), BF), jax.ShapeDtypeStruct((s, n_pairs * LANES), F32)]),
    )(proj, proj, proj, *(rider.arrays if rider else []))


def _sb_bwd(proj, o32, do_a, rider=None):
    s = proj.shape[0]
    tk, tq = SB_TK, SB_KT * SB_TK
    n_pairs = 4
    scale = 1.0 / math.sqrt(SB_HEAD_DIM)

    def body(*refs):
        (q_ref, k_ref, v_ref, o_ref, do_ref), (dq_ref, dk_ref, dv_ref) = _host_refs(refs, rider, 5, 3)
        i = pl.program_id(1)

        @pl.when(i == 0)
        def _():
            dk_ref[...] = jnp.zeros_like(dk_ref)
            dv_ref[...] = jnp.zeros_like(dv_ref)

        diff, later, later_incl = _sb_consts(tq)
        qs = (q_ref[...].astype(F32) * scale).astype(BF)
        do2 = do_ref[...]
        prod = do2.astype(F32) * o_ref[...]
        lane_masks = _head_lane_masks(tk)
        first_head = lax.broadcasted_iota(jnp.int32, (tq, LANES), 1) < SB_HEAD_DIM
        totals = [jnp.broadcast_to(jnp.sum(jnp.where(keep, prod, 0.0), axis=-1, keepdims=True), (tq, tk))
                  for keep in (first_head, jnp.logical_not(first_head))]
        first_head_k = first_head[:tk]

        def step(g_idx, state, masked):
            tiles = list(reversed(range(SB_KT)))
            chains = [(t, h) for t in tiles for h in range(2)]
            rows = {t: pl.ds(pl.multiple_of((g_idx * SB_KT + t) * tk, tk), tk) for t in tiles}
            ks = {t: _both_heads(k_ref[rows[t], :], lane_masks) for t in tiles}
            vs = {t: _both_heads(v_ref[rows[t], :], lane_masks) for t in tiles}
            r0 = {t: t * tk if masked else 0 for t in tiles}
            allowed = {t: _tail(diff, r0[t]) < -t * tk for t in tiles}
            zs = {t: lax.dot_general(_tail(qs, r0[t]), ks[t], NT, preferred_element_type=F32) for t in tiles}
            dws = {t: lax.dot_general(_tail(do2, r0[t]), vs[t], NT, preferred_element_type=F32) for t in tiles}
            logs = {}
            for t, h in chains:
                log_b, sp = _log_terms(zs[t][:, h * tk:(h + 1) * tk])
                logs[t, h] = (log_b, jnp.where(allowed[t], sp, 0.0) if masked else sp)
            sums = {c: _split_dot(logs[c][1], later) for c in chains}
            c_log, c_g = list(state[0]), list(state[1])
            ws, gs = {}, {}
            for t, h in chains:
                w = jnp.exp(logs[t, h][0] - (sums[t, h][:, :tk] + _tail(c_log[h], r0[t])))
                ws[t, h] = (jnp.where(allowed[t], w, 0.0) if masked else w).astype(BF)
                c_log[h] = _add_tail(c_log[h], r0[t], sums[t, h][:, tk:])
                gs[t, h] = ws[t, h].astype(F32) * dws[t][:, h * tk:(h + 1) * tk]
            gsums = {c: _split_dot(gs[c], later_incl) for c in chains}
            dzs = {}
            for t, h in chains:
                beta = jnp.exp(logs[t, h][0])
                earlier = _tail(totals[h], r0[t]) - (gsums[t, h][:, :tk] + _tail(c_g[h], r0[t]))
                dz = gs[t, h] * (1.0 - beta) - earlier * beta
                dzs[t, h] = (jnp.where(allowed[t], dz, 0.0) if masked else dz).astype(BF)
                c_g[h] = _add_tail(c_g[h], r0[t], gsums[t, h][:, tk:])
            dq = state[2]
            for t in tiles:
                dz_both = jnp.concatenate([dzs[t, 0], dzs[t, 1]], axis=1)
                w_both = jnp.concatenate([ws[t, 0], ws[t, 1]], axis=1)
                dq = _add_tail(dq, r0[t], jnp.dot(dz_both, ks[t], preferred_element_type=F32))
                dk2 = lax.dot_general(dz_both, _tail(qs, r0[t]), TN, preferred_element_type=F32)
                dv2 = lax.dot_general(w_both, _tail(do2, r0[t]), TN, preferred_element_type=F32)
                dk_ref[rows[t], :] += jnp.where(first_head_k, dk2[:tk], dk2[tk:])
                dv_ref[rows[t], :] += jnp.where(first_head_k, dv2[:tk], dv2[tk:])
            return tuple(c_log), tuple(c_g), dq

        zero = jnp.zeros((tq, LANES), F32)
        state = step(i, ((zero, zero), (zero, zero), zero), True)
        state = _walk_back(i, lambda g, st: step(g, st, False), state, lambda st: st[0])
        dq_ref[...] = (state[2] * scale).astype(dq_ref.dtype)
        if rider:
            rider.run(refs, 5, 3, (pl.program_id(0) == 0) & (i == 0),
                      (pl.program_id(0) == n_pairs - 1) & (i == s // tq - 1))

    width = n_pairs * LANES
    tok = pl.BlockSpec((tq, LANES), lambda p, i: (i, p))
    return pl.pallas_call(
        body, name="sb_attn_bwd", grid=(n_pairs, s // tq),
        compiler_params=_params(("arbitrary", "arbitrary"), has_side_effects=rider is not None),
        **_with_rider(
            rider, 5, 3,
            [tok, pl.BlockSpec((s, LANES), lambda p, i: (0, n_pairs + p)),
             pl.BlockSpec((s, LANES), lambda p, i: (0, 2 * n_pairs + p)), tok, tok],
            [tok, pl.BlockSpec((s, LANES), lambda p, i: (0, p)), pl.BlockSpec((s, LANES), lambda p, i: (0, p))],
            [jax.ShapeDtypeStruct((s, width), BF), jax.ShapeDtypeStruct((s, width), F32),
             jax.ShapeDtypeStruct((s, width), F32)]),
    )(proj, proj, proj, o32, do_a, *(rider.arrays if rider else []))


CONV_COL0 = 12


def _shift_rows(v, k):
    n = v.shape[0]
    row = lax.broadcasted_iota(jnp.int32, v.shape, 0)
    rolled = pltpu.roll(v, k % n, axis=0)
    keep = row >= k if k > 0 else row < n + k
    return jnp.where(keep, rolled, 0.0)


def _conv_specs(s):
    return [pl.BlockSpec((s, LANES), lambda cb: (0, CONV_COL0 + cb)),
            pl.BlockSpec((s, LANES), lambda cb: (0, CONV_COL0 + 4 + cb)),
            pl.BlockSpec((s, LANES), lambda cb: (0, CONV_COL0 + 8 + cb)),
            pl.BlockSpec((CONV_ROWS, LANES), lambda cb: (0, cb))]


def _conv_fwd(proj, conv_w):
    s = proj.shape[0]

    def body(u_ref, gb_ref, gc_ref, w_ref, y_ref):
        cu = gc_ref[...].astype(F32) * u_ref[...].astype(F32)
        w = w_ref[...]
        y = w[0:1] * _shift_rows(cu, 2) + w[1:2] * _shift_rows(cu, 1) + w[2:3] * cu
        y_ref[...] = (gb_ref[...].astype(F32) * y).astype(y_ref.dtype)

    return pl.pallas_call(
        body, name="conv_fwd", grid=(4,), in_specs=_conv_specs(s),
        out_specs=pl.BlockSpec((s, LANES), lambda cb: (0, cb)),
        out_shape=jax.ShapeDtypeStruct((s, 4 * LANES), BF),
        compiler_params=_params(("parallel",)),
    )(proj, proj, proj, conv_w)


def _conv_bwd(proj, conv_w, dy):
    s = proj.shape[0]

    def body(u_ref, gb_ref, gc_ref, w_ref, dy_ref, du_ref, dgb_ref, dgc_ref, dw_ref):
        u, gc = u_ref[...].astype(F32), gc_ref[...].astype(F32)
        dyv = dy_ref[...].astype(F32)
        w = w_ref[...]
        cu = gc * u
        cu1, cu2 = _shift_rows(cu, 1), _shift_rows(cu, 2)
        conv = w[0:1] * cu2 + w[1:2] * cu1 + w[2:3] * cu
        dgb_ref[...] = (dyv * conv).astype(dgb_ref.dtype)
        dc = dyv * gb_ref[...].astype(F32)
        dcu = w[2:3] * dc + w[1:2] * _shift_rows(dc, -1) + w[0:1] * _shift_rows(dc, -2)
        dgc_ref[...] = (dcu * u).astype(dgc_ref.dtype)
        du_ref[...] = (dcu * gc).astype(du_ref.dtype)
        tap_row = lax.broadcasted_iota(jnp.int32, (CONV_ROWS, LANES), 0)
        dw = jnp.zeros((CONV_ROWS, LANES), F32)
        for t, shifted in enumerate((cu2, cu1, cu)):
            dw = jnp.where(tap_row == t, jnp.sum(dc * shifted, axis=0, keepdims=True), dw)
        dw_ref[...] = dw

    col = pl.BlockSpec((s, LANES), lambda cb: (0, cb))
    act = jax.ShapeDtypeStruct((s, 4 * LANES), BF)
    return pl.pallas_call(
        body, name="conv_bwd", grid=(4,), in_specs=_conv_specs(s) + [col],
        out_specs=[col, col, col, pl.BlockSpec((CONV_ROWS, LANES), lambda cb: (0, cb))],
        out_shape=[act, act, act, jax.ShapeDtypeStruct((CONV_ROWS, 4 * LANES), F32)],
        compiler_params=_params(("parallel",)),
    )(proj, proj, proj, conv_w, dy)


def _mem_probs(q, k, scale):
    sc = lax.dot_general(q, k, NT, preferred_element_type=F32) * scale
    p = jnp.exp(sc - jnp.max(sc, axis=-1, keepdims=True))
    return p / jnp.sum(p, axis=-1, keepdims=True)


def _mem_fwd(q_m, kv, tq=2048):
    s, d = q_m.shape
    mlen = kv.shape[0]
    hd = d // MEM_HEADS
    tq = _tile(s, tq)
    scale = 1.0 / math.sqrt(hd)

    def body(q_ref, k_ref, v_ref, o_ref):
        p = _mem_probs(q_ref[...], k_ref[...], scale)
        o_ref[...] = jnp.dot(p.astype(BF), v_ref[...], preferred_element_type=F32).astype(o_ref.dtype)

    return pl.pallas_call(
        body, name="mem_attn_fwd", grid=(MEM_HEADS, s // tq),
        in_specs=[pl.BlockSpec((tq, hd), lambda h, i: (i, h)),
                  pl.BlockSpec((mlen, hd), lambda h, i: (0, h)),
                  pl.BlockSpec((mlen, hd), lambda h, i: (0, MEM_HEADS + h))],
        out_specs=pl.BlockSpec((tq, hd), lambda h, i: (i, h)),
        out_shape=jax.ShapeDtypeStruct((s, d), BF),
        compiler_params=_params(("parallel", "parallel")),
    )(q_m, kv, kv)


def _mem_bwd(q_m, kv, do_m, tq=2048):
    s, d = q_m.shape
    mlen = kv.shape[0]
    hd = d // MEM_HEADS
    tq = _tile(s, tq)
    scale = 1.0 / math.sqrt(hd)

    def body(q_ref, k_ref, v_ref, do_ref, dq_ref, dk_ref, dv_ref):
        q, k, v, do = q_ref[...], k_ref[...], v_ref[...], do_ref[...]
        p = _mem_probs(q, k, scale)
        dp = lax.dot_general(do, v, NT, preferred_element_type=F32)
        ds = p * (dp - jnp.sum(dp * p, axis=-1, keepdims=True)) * scale
        dsb = ds.astype(BF)
        dq_ref[...] = jnp.dot(dsb, k, preferred_element_type=F32).astype(dq_ref.dtype)
        dk = lax.dot_general(dsb, q, TN, preferred_element_type=F32)
        dv = lax.dot_general(p.astype(BF), do, TN, preferred_element_type=F32)
        first = pl.program_id(1) == 0

        @pl.when(first)
        def _():
            dk_ref[...] = dk
            dv_ref[...] = dv

        @pl.when(jnp.logical_not(first))
        def _():
            dk_ref[...] += dk
            dv_ref[...] += dv

    tok = pl.BlockSpec((tq, hd), lambda h, i: (i, h))
    memb = pl.BlockSpec((mlen, hd), lambda h, i: (0, h))
    return pl.pallas_call(
        body, name="mem_attn_bwd", grid=(MEM_HEADS, s // tq),
        in_specs=[tok, memb, pl.BlockSpec((mlen, hd), lambda h, i: (0, MEM_HEADS + h)), tok],
        out_specs=[tok, memb, memb],
        out_shape=[jax.ShapeDtypeStruct((s, d), BF), jax.ShapeDtypeStruct((mlen, d), F32),
                   jax.ShapeDtypeStruct((mlen, d), F32)],
        compiler_params=_params(("parallel", "arbitrary")),
    )(q_m, kv, kv, do_m)


def _place():
    x, y, c = lax.axis_index("x"), lax.axis_index("y"), lax.axis_index("c")
    other_chips = [(1 - x, y), (x, 1 - y), (1 - x, 1 - y)]
    return x, y, c, other_chips


def _chip_no(cx, cy):
    return 2 * cx + cy


HBM_SPEC = pl.BlockSpec(memory_space=pl.ANY)


def _cast_place(shard, axis, place, dtype, name, after=()):
    r, c = shard.shape
    tr = _tile(r, max(16, 1048576 // c))
    nblk = r // tr
    if axis == 1:
        full, out_map = (r, N_CHIPS * c), lambda i, pref: (i, pref[0])
    else:
        full, out_map = (N_CHIPS * r, c), lambda i, pref: (pref[0] * nblk + i, 0)

    def body(pref, s_ref, *rest):
        o_ref = rest[-1]
        o_ref[...] = s_ref[...].astype(o_ref.dtype)

    return pl.pallas_call(
        body, name=name,
        grid_spec=pltpu.PrefetchScalarGridSpec(
            num_scalar_prefetch=1, grid=(nblk,),
            in_specs=[pl.BlockSpec((tr, c), lambda i, pref: (i, 0))] + [HBM_SPEC] * len(after),
            out_specs=pl.BlockSpec((tr, c), out_map)),
        out_shape=jax.ShapeDtypeStruct(full, dtype),
        compiler_params=_params(("parallel",)),
    )(place, shard, *after)


def _region(ref, axis, chip_no, half):
    width = ref.shape[axis] // N_CHIPS
    start = pl.multiple_of(chip_no * width, width)
    if axis == 1:
        if half is None:
            return ref.at[:, pl.ds(start, width)]
        hr = ref.shape[0] // 2
        return ref.at[pl.ds(pl.multiple_of(half * hr, hr), hr), pl.ds(start, width)]
    if half is None:
        return ref.at[pl.ds(start, width), :]
    hr = width // 2
    return ref.at[pl.ds(pl.multiple_of(start + half * hr, hr), hr), :]


def _gather_weights(fulls, axes, split):
    n = len(fulls)

    def body(*refs):
        outs = refs[n:2 * n]
        send, recv, fsend, frecv = refs[2 * n:]
        x, y, c, others = _place()
        me = _chip_no(x, y)
        sibling = (x, y, 1 - c)

        def copy(w, chip_no, half, sems, p, to):
            reg = _region(outs[w], axes[w], chip_no, half)
            return pltpu.make_async_remote_copy(
                src_ref=reg, dst_ref=reg, send_sem=sems[0].at[w, p], recv_sem=sems[1].at[w, p],
                device_id=to, device_id_type=MESH)

        for w in range(n):
            for p, chip in enumerate(others):
                copy(w, me, c if split[w] else None, (send, recv), p, (chip[0], chip[1], c)).start()
        for w in range(n):
            for p, chip in enumerate(others):
                half = c if split[w] else None
                copy(w, _chip_no(*chip), half, (send, recv), p, (chip[0], chip[1], c)).wait_recv()
                if split[w]:
                    copy(w, _chip_no(*chip), c, (fsend, frecv), p, sibling).start()
        for w in range(n):
            for p, chip in enumerate(others):
                copy(w, me, c if split[w] else None, (send, recv), p, (chip[0], chip[1], c)).wait_send()
                if split[w]:
                    handed = copy(w, _chip_no(*chip), 1 - c, (fsend, frecv), p, sibling)
                    handed.wait_recv()
                    handed.wait_send()

    return pl.pallas_call(
        body, name="gather_weights",
        in_specs=[HBM_SPEC] * n, out_specs=[HBM_SPEC] * n,
        out_shape=[jax.ShapeDtypeStruct(f.shape, f.dtype) for f in fulls],
        input_output_aliases={i: i for i in range(n)},
        scratch_shapes=[pltpu.SemaphoreType.DMA((n, 3))] * 4,
        compiler_params=pltpu.CompilerParams(has_side_effects=True),
    )(*fulls)


def _fetch_copy(refs, axes, whole, send, recv, w, p, chip, c, arriving):
    owner = _chip_no(*chip) if arriving else _chip_no(lax.axis_index("x"), lax.axis_index("y"))
    reg = _region(refs[w], axes[w], owner, None if whole[w] else c)
    return pltpu.make_async_remote_copy(
        src_ref=reg, dst_ref=reg, send_sem=send[p], recv_sem=recv[p],
        device_id=(chip[0], chip[1], c), device_id_type=MESH)


N_PEERS = N_CHIPS - 1


class _Rider:
    def __init__(self, arrays, outs, aliases, start, wait):
        self.arrays, self.outs, self.aliases, self.start, self.wait = list(arrays), list(outs), aliases, start, wait
        self.scratch = [pltpu.SemaphoreType.DMA((len(self.arrays), N_PEERS))] * 2

    def run(self, refs, n_in, n_out, first, last):
        ra, ro = len(self.arrays), len(self.outs)
        ins = refs[n_in:n_in + ra]
        outs = refs[n_in + ra + n_out:n_in + ra + n_out + ro]
        send, recv = refs[-2], refs[-1]

        @pl.when(first)
        def _():
            self.start(ins, outs, send, recv)

        @pl.when(last)
        def _():
            self.wait(ins, outs, send, recv)

    def call_args(self, n_in, n_out):
        ra = len(self.arrays)
        return dict(in_specs=[HBM_SPEC] * ra, out_specs=[HBM_SPEC] * len(self.outs), out_shape=self.outs,
                    aliases={n_in + k: n_out + o for k, o in self.aliases.items()}, scratch=self.scratch)


def _host_refs(refs, rider, n_in, n_out):
    ra = len(rider.arrays) if rider else 0
    return refs[:n_in], refs[n_in + ra:n_in + ra + n_out]


def _riding_fetch(fulls, axes):
    n = len(fulls)
    whole = [False] * n

    def sems(ref, w):
        return [ref.at[w, q] for q in range(N_PEERS)]

    def start(ins, outs, send, recv):
        _, _, c, others = _place()
        for w in range(n):
            for p, chip in enumerate(others):
                _fetch_copy(outs, axes, whole, sems(send, w), sems(recv, w), w, p, chip, c, False).start()

    def wait(ins, outs, send, recv):
        _, _, c, others = _place()
        for w in range(n):
            for p, chip in enumerate(others):
                _fetch_copy(outs, axes, whole, sems(send, w), sems(recv, w), w, p, chip, c, False).wait_send()
                _fetch_copy(outs, axes, whole, sems(send, w), sems(recv, w), w, p, chip, c, True).wait_recv()

    return _Rider(fulls, [jax.ShapeDtypeStruct(f.shape, f.dtype) for f in fulls], {k: k for k in range(n)}, start, wait)


def _hand_on(fulls, axes, name):
    n = len(fulls)

    def body(*refs):
        outs = refs[n:2 * n]
        send, recv = refs[2 * n:]
        x, y, c, others = _place()

        def copy(w, p, chip, half):
            reg = _region(outs[w], axes[w], _chip_no(*chip), half)
            return pltpu.make_async_remote_copy(
                src_ref=reg, dst_ref=reg, send_sem=send.at[w, p], recv_sem=recv.at[w, p],
                device_id=(x, y, 1 - c), device_id_type=MESH)

        for w in range(n):
            for p, chip in enumerate(others):
                copy(w, p, chip, c).start()
        for w in range(n):
            for p, chip in enumerate(others):
                copy(w, p, chip, 1 - c).wait()

    return pl.pallas_call(
        body, name=name,
        in_specs=[HBM_SPEC] * n, out_specs=[HBM_SPEC] * n,
        out_shape=[jax.ShapeDtypeStruct(f.shape, f.dtype) for f in fulls],
        input_output_aliases={i: i for i in range(n)},
        scratch_shapes=[pltpu.SemaphoreType.DMA((n, 3)), pltpu.SemaphoreType.DMA((n, 3))],
        compiler_params=pltpu.CompilerParams(has_side_effects=True),
    )(*fulls)


def _pair_exchange(grads, name):
    n = len(grads)

    def body(*refs):
        ins, outs = refs[:n], refs[n:2 * n]
        send, recv = refs[2 * n:]
        x, y, c, _ = _place()
        cps = []
        for w in range(n):
            cp = pltpu.make_async_remote_copy(
                src_ref=ins[w].at[:, 1 - c], dst_ref=outs[w], send_sem=send.at[w], recv_sem=recv.at[w],
                device_id=(x, y, 1 - c), device_id_type=MESH)
            cp.start()
            cps.append(cp)
        for cp in cps:
            cp.wait()

    return pl.pallas_call(
        body, name=name,
        in_specs=[HBM_SPEC] * n, out_specs=[HBM_SPEC] * n,
        out_shape=[jax.ShapeDtypeStruct((g.shape[0],) + g.shape[2:], g.dtype) for g in grads],
        scratch_shapes=[pltpu.SemaphoreType.DMA((n,)), pltpu.SemaphoreType.DMA((n,))],
        compiler_params=pltpu.CompilerParams(has_side_effects=True),
    )(*grads)


def _pair_add(g4, got, core, name):
    nj, _, hr, cdim = g4.shape
    tr = _tile(hr, max(8, 524288 // cdim))

    def body(core_ref, a_ref, b_ref, o_ref):
        o_ref[...] = (a_ref[...].astype(F32) + b_ref[...].astype(F32)).astype(o_ref.dtype)

    return pl.pallas_call(
        body, name=name,
        grid_spec=pltpu.PrefetchScalarGridSpec(
            num_scalar_prefetch=1, grid=(nj, hr // tr),
            in_specs=[pl.BlockSpec((1, None, tr, cdim), lambda j, i, core_ref: (j, core_ref[0], i, 0)),
                      pl.BlockSpec((1, tr, cdim), lambda j, i, core_ref: (j, i, 0))],
            out_specs=pl.BlockSpec((1, tr, cdim), lambda j, i, core_ref: (j, i, 0))),
        out_shape=jax.ShapeDtypeStruct((nj, hr, cdim), BF),
        compiler_params=_params(("parallel", "parallel")),
    )(core, g4, got)


def _piece(ref, axis, j, hc):
    if axis == 0:
        return ref.at[j]
    return ref.at[0, :, pl.ds(pl.multiple_of(j * hc, hc), hc)]


def _slot_shapes(sums, axes):
    return [(N_CHIPS - 1, sm.shape[1], sm.shape[2] // (1 if ax == 0 else N_CHIPS)) for sm, ax in zip(sums, axes)]


def _slot_copy(sums, lands, axes, send, recv, w, p, chip, c):
    return pltpu.make_async_remote_copy(
        src_ref=_piece(sums[w], axes[w], _chip_no(*chip), lands[w].shape[2]), dst_ref=lands[w].at[p],
        send_sem=send[p], recv_sem=recv[p],
        device_id=(chip[0], chip[1], c), device_id_type=MESH)


def _riding_slots(sums, axes):
    n = len(sums)
    shapes = _slot_shapes(sums, axes)

    def copies(ins, outs, send, recv):
        _, _, c, others = _place()
        return [_slot_copy(ins, outs, axes, [send.at[w, q] for q in range(N_PEERS)],
                           [recv.at[w, q] for q in range(N_PEERS)], w, p, chip, c)
                for w in range(n) for p, chip in enumerate(others)]

    def start(ins, outs, send, recv):
        for cp in copies(ins, outs, send, recv):
            cp.start()

    def wait(ins, outs, send, recv):
        for cp in copies(ins, outs, send, recv):
            cp.wait()

    return _Rider(sums, [jax.ShapeDtypeStruct(sh, sm.dtype) for sh, sm in zip(shapes, sums)], {}, start, wait)


def _chip_sum(psum, slots, axis, place, name):
    _, hr, hc = slots.shape
    tr = _tile(hr, 256)
    own_map = (lambda i, pref: (0, i, pref[0])) if axis == 1 else (lambda i, pref: (pref[0], i, 0))

    def body(pref, own_ref, s_ref, o_ref):
        o_ref[...] = ((own_ref[...].astype(F32) + s_ref[0].astype(F32)) + s_ref[1].astype(F32)) + s_ref[2].astype(F32)

    return pl.pallas_call(
        body, name=name,
        grid_spec=pltpu.PrefetchScalarGridSpec(
            num_scalar_prefetch=1, grid=(hr // tr,),
            in_specs=[pl.BlockSpec((None, tr, hc), own_map),
                      pl.BlockSpec((N_CHIPS - 1, tr, hc), lambda i, pref: (0, i, 0))],
            out_specs=pl.BlockSpec((None, tr, hc), lambda i, pref: (pref[1], i, 0))),
        out_shape=jax.ShapeDtypeStruct((2, hr, hc), F32),
        compiler_params=_params(("parallel",)),
    )(place, psum, slots)


def _half_swap(both):
    n = len(both)

    def body(*refs):
        outs = refs[n:2 * n]
        send, recv = refs[2 * n:]
        x, y, c, _ = _place()

        def copy(w, half):
            return pltpu.make_async_remote_copy(
                src_ref=outs[w].at[half], dst_ref=outs[w].at[half], send_sem=send.at[w], recv_sem=recv.at[w],
                device_id=(x, y, 1 - c), device_id_type=MESH)

        for w in range(n):
            copy(w, c).start()
        for w in range(n):
            copy(w, 1 - c).wait()

    return pl.pallas_call(
        body, name="grad_half_swap",
        in_specs=[HBM_SPEC] * n, out_specs=[HBM_SPEC] * n,
        out_shape=[jax.ShapeDtypeStruct(b.shape, b.dtype) for b in both],
        input_output_aliases={i: i for i in range(n)},
        scratch_shapes=[pltpu.SemaphoreType.DMA((n,)), pltpu.SemaphoreType.DMA((n,))],
        compiler_params=pltpu.CompilerParams(has_side_effects=True),
    )(*both)


def _allreduce_small(pack):
    rows, d = pack.shape

    def body(p_ref, o_ref, slots, send, recv):
        x, y, c, _ = _place()
        me = 4 * x + 2 * y + c
        slots[me] = p_ref[...]
        cps = []
        for k in range(1, N_DEV):
            px, py, pc = x ^ (k >> 2), y ^ ((k >> 1) & 1), c ^ (k & 1)
            cp = pltpu.make_async_remote_copy(
                src_ref=p_ref, dst_ref=slots.at[me], send_sem=send.at[k - 1], recv_sem=recv.at[k - 1],
                device_id=(px, py, pc), device_id_type=MESH)
            cp.start()
            cps.append(cp)
        for k in range(1, N_DEV):
            px, py, pc = x ^ (k >> 2), y ^ ((k >> 1) & 1), c ^ (k & 1)
            arrival = pltpu.make_async_remote_copy(
                src_ref=p_ref, dst_ref=slots.at[4 * px + 2 * py + pc], send_sem=send.at[k - 1],
                recv_sem=recv.at[k - 1], device_id=(px, py, pc), device_id_type=MESH)
            arrival.wait_recv()
            arrival.wait_send()
        acc = slots[0]
        for k in range(1, N_DEV):
            acc = acc + slots[k]
        o_ref[...] = acc

    vm = pl.BlockSpec(memory_space=pltpu.VMEM)
    return pl.pallas_call(
        body, name="allreduce_small", in_specs=[vm], out_specs=vm,
        out_shape=jax.ShapeDtypeStruct((rows, d), F32),
        scratch_shapes=[pltpu.VMEM((N_DEV, rows, d), F32), pltpu.SemaphoreType.DMA((N_DEV - 1,)),
                        pltpu.SemaphoreType.DMA((N_DEV - 1,))],
        compiler_params=pltpu.CompilerParams(has_side_effects=True),
    )(pack)


def _adamw(w, g, m, v, name):
    rows, cols = w.shape

    def fn(wv, gv, mv, vv):
        m2 = ADAM_B1 * mv + (1.0 - ADAM_B1) * gv
        v2 = ADAM_B2 * vv + (1.0 - ADAM_B2) * (gv * gv)
        m_hat = m2 / (1.0 - ADAM_B1 ** ADAM_STEP)
        v_hat = v2 / (1.0 - ADAM_B2 ** ADAM_STEP)
        delta = -ADAM_LR * (m_hat / (jnp.sqrt(v_hat) + ADAM_EPS) + ADAM_WD * wv)
        return delta, m2, v2

    ins = [(a, cols, 0) for a in (w, g, m, v)]
    return _rowwise(fn, ins, [(cols, F32)] * 3, rows=rows, tm=_tile(rows, max(8, 262144 // cols)), name=name)


BIG = ["w_in", "w_branch_a", "w_branch_b", "w_mix_out", "w_mem_q", "w_mem_kv", "w_mem_o", "w_ffn_in", "w_ffn_out"]
BIG_AXIS = {"w_in": 1, "w_branch_a": 1, "w_branch_b": 1, "w_mix_out": 0, "w_mem_q": 0, "w_mem_kv": 1,
            "w_mem_o": 0, "w_ffn_in": 1, "w_ffn_out": 0}
NORMS = ["norm_mix", "norm_mem_q", "norm_mem_kv", "norm_ffn", "norm_final"]
ORDER = ["norm_mix", "w_in", "conv_w", "w_branch_a", "w_branch_b", "w_mix_out", "norm_mem_q", "norm_mem_kv",
         "w_mem_q", "w_mem_kv", "w_mem_o", "norm_ffn", "w_ffn_in", "w_ffn_out", "norm_final"]


def _pack_small(vals, conv):
    d = vals[0].shape[-1]
    rows = [v.reshape(1, d) for v in vals]
    conv = jnp.pad(conv, ((0, 0), (0, d - conv.shape[1])))
    pad = jnp.zeros((SMALL_ROWS - len(rows) - CONV_K, d), F32)
    return jnp.concatenate(rows + [conv, pad], axis=0)


def kernel(x, mem, norm_mix, w_in, conv_w, w_branch_a, w_branch_b, w_mix_out, norm_mem_q, norm_mem_kv, w_mem_q, w_mem_kv, w_mem_o, norm_ffn, w_ffn_in, w_ffn_out, norm_final, loss_target, m_norm_mix, m_w_in, m_conv_w, m_w_branch_a, m_w_branch_b, m_w_mix_out, m_norm_mem_q, m_norm_mem_kv, m_w_mem_q, m_w_mem_kv, m_w_mem_o, m_norm_ffn, m_w_ffn_in, m_w_ffn_out, m_norm_final, v_norm_mix, v_w_in, v_conv_w, v_w_branch_a, v_w_branch_b, v_w_mix_out, v_norm_mem_q, v_norm_mem_kv, v_w_mem_q, v_w_mem_kv, v_w_mem_o, v_norm_ffn, v_w_ffn_in, v_w_ffn_out, v_norm_final):
    args = dict(locals())
    wts = {n: args[n] for n in ORDER}
    mom = {n: args["m_" + n] for n in ORDER}
    var = {n: args["v_" + n] for n in ORDER}
    x = x[0]
    mem = mem[0]
    target = loss_target[0]
    s, d = x.shape
    gains = {n: wts[n].reshape(1, d) for n in NORMS}
    chip = 2 * lax.axis_index("x") + lax.axis_index("y")
    core = lax.axis_index("c").astype(jnp.int32).reshape(1)
    place = jnp.stack([chip, lax.axis_index("c")]).astype(jnp.int32)

    conv_shard = jnp.pad(conv_w[0], ((0, CONV_ROWS - CONV_K), (0, 0)))
    first = [_cast_place(wts["w_in"][0], BIG_AXIS["w_in"], place, BF, "place_w_in"),
             _cast_place(conv_shard, 1, place, F32, "place_conv_w")]
    w_in_full, conv_full = _gather_weights(first, [BIG_AXIS["w_in"], 1], [True, False])
    W = {"w_in": w_in_full}
    ride_in_proj = ["w_branch_a", "w_branch_b", "w_mix_out", "w_mem_q", "w_mem_o"]
    ride_attention = ["w_ffn_in", "w_mem_kv"]
    ride_ffn_in = ["w_ffn_out"]
    placed = {n: _cast_place(wts[n][0], BIG_AXIS[n], place, BF, "place_" + n)
              for n in ride_in_proj + ride_attention + ride_ffn_in}

    def fetch(names):
        return _riding_fetch([placed[n] for n in names], [BIG_AXIS[n] for n in names])

    def hand_on(names, bufs, tag):
        W.update(zip(names, _hand_on(bufs, [BIG_AXIS[n] for n in names], "gather_hand_on" + tag)))

    h1 = _rmsnorm(x, gains["norm_mix"], "norm_mix_fwd")
    proj, *bufs_a = _matmul(h1, W["w_in"], tn=1280, name="in_proj", rider=fetch(ride_in_proj))
    o_a, o_a32, *bufs_b = _sb_fwd(proj, fetch(ride_attention))
    y_b = _conv_fwd(proj, conv_full)
    hand_on(ride_in_proj + ride_attention, bufs_a + bufs_b, "")
    br_a, br_b, merged = _branches_merge(o_a, y_b, W["w_branch_a"], W["w_branch_b"], proj)
    x1 = _matmul(merged, W["w_mix_out"], tn=1024, out_dtype=F32, resid=x, name="mix_out")

    hq, q_m = _norm_matmul(x1, gains["norm_mem_q"], W["w_mem_q"], tn=1024, name="norm_mem_q")
    mn, kv = _norm_matmul(mem, gains["norm_mem_kv"], W["w_mem_kv"], tn=1024, name="norm_mem_kv")
    o_m = _mem_fwd(q_m, kv)
    x2 = _matmul(o_m, W["w_mem_o"], tn=1024, out_dtype=F32, resid=x1, name="mem_o")

    hf, gate, up, act, *bufs_c = _norm_ffn_in_swiglu(x2, gains["norm_ffn"], W["w_ffn_in"], rider=fetch(ride_ffn_in))
    hand_on(ride_ffn_in, bufs_c, "_late")

    dx3, dx3_b, dg_final, loss_part = _ffn_out_loss(act, W["w_ffn_out"], x2, gains["norm_final"], target)

    dgu = _d_act_swiglu(dx3_b, W["w_ffn_out"], gate, up)
    gw = {"w_ffn_out": _matmul(act, dx3_b, ta=True, tm=1408, tn=512, name="gw_ffn_out")}
    gw["w_ffn_in"] = _matmul(hf, dgu, ta=True, tn=512, name="gw_ffn_in")

    def pair_sums(names, tag):
        views = []
        for n in names:
            r, cdim = gw[n].shape
            views.append(gw[n].reshape(1, 2, r // 2, cdim) if BIG_AXIS[n] == 1
                         else gw[n].reshape(N_CHIPS, 2, r // (2 * N_CHIPS), cdim))
        got = _pair_exchange(views, "grad_pair_exchange_" + tag)
        return [_pair_add(v, g, core, "pair_add_" + n) for n, v, g in zip(names, views, got)]

    def chip_sums(names, sums, slots):
        return [_chip_sum(sm, sl, BIG_AXIS[n], place, "chip_sum_" + n) for n, sm, sl in zip(names, sums, slots)]

    first_sums = pair_sums(["w_ffn_out"], "first")
    dx2, dx2_b, dg_ffn, *first_slots = _matmul_norm_bwd(
        dgu, W["w_ffn_in"], x2, gains["norm_ffn"], dx3, tm=256, bf_copy=True, name="d_hf_norm_bwd",
        rider=_riding_slots(first_sums, [BIG_AXIS["w_ffn_out"]]))

    do_m = _matmul(dx2_b, W["w_mem_o"], tb=True, tn=1024, name="d_o_m")
    gw["w_mem_o"] = _matmul(o_m, dx2_b, ta=True, tn=512, name="gw_mem_o")
    dq_m, dk_m, dv_m = _mem_bwd(q_m, kv, do_m)
    dkv = jnp.concatenate([dk_m, dv_m], axis=-1)
    dx1, dx1_b, dg_q = _matmul_norm_bwd(dq_m, W["w_mem_q"], x1, gains["norm_mem_q"], dx2, tm=512, bf_copy=True,
                                        name="d_hq_norm_bwd")
    gw["w_mem_q"] = _matmul(hq, dq_m, ta=True, tn=512, name="gw_mem_q")
    dmn = _matmul(dkv, W["w_mem_kv"], tb=True, tn=1024, out_dtype=F32, name="d_mn")
    gw["w_mem_kv"] = _matmul(mn, dkv, ta=True, tn=1024, name="gw_mem_kv")
    _, dg_kv = _rmsnorm_bwd(mem, gains["norm_mem_kv"], dmn, None, "norm_mem_kv_bwd")

    dbr_a, dbr_b, dga, dgb = _d_merged_gates(dx1_b, W["w_mix_out"], proj, br_a, br_b)
    gw["w_mix_out"] = _matmul(merged, dx1_b, ta=True, tn=512, name="gw_mix_out")
    do_a = _matmul(dbr_a, W["w_branch_a"], tb=True, name="d_o_a")
    gw["w_branch_a"] = _matmul(o_a, dbr_a, ta=True, tn=512, name="gw_branch_a")
    dy_b = _matmul(dbr_b, W["w_branch_b"], tb=True, name="d_y_b")
    gw["w_branch_b"] = _matmul(y_b, dbr_b, ta=True, tn=512, name="gw_branch_b")
    du, dgate_b, dgate_c, dconv = _conv_bwd(proj, conv_full, dy_b)
    early = [n for n in BIG if n not in ("w_in", "w_ffn_out")]
    early_sums = pair_sums(early, "early")
    dq, dk, dv, *early_slots = _sb_bwd(proj, o_a32, do_a, _riding_slots(early_sums, [BIG_AXIS[n] for n in early]))

    def assemble(*parts):
        return jnp.concatenate([p.astype(BF) for p in parts], axis=-1)

    hw = dq.shape[1]
    dproj = _rowwise(assemble, [(t, hw, 0) for t in (dq, dk, dv, du, dgate_b, dgate_c)] + [(dga, d, 0), (dgb, d, 0)],
                     [(proj.shape[1], BF)], rows=s, tm=256, name="assemble_dproj")[0]
    gw["w_in"] = _matmul(h1, dproj, ta=True, tn=640, name="gw_in")
    in_sums = pair_sums(["w_in"], "in")
    grad_x, dg_mix, *in_slots = _matmul_norm_bwd(dproj, W["w_in"], x, gains["norm_mix"], dx1, tm=256, bf_copy=False,
                                                 name="d_h1_norm_bwd",
                                                 rider=_riding_slots(in_sums, [BIG_AXIS["w_in"]]))

    halves = dict(zip(early, chip_sums(early, early_sums, early_slots)))
    halves.update(zip(["w_ffn_out"], chip_sums(["w_ffn_out"], first_sums, first_slots)))
    halves.update(zip(["w_in"], chip_sums(["w_in"], in_sums, in_slots)))
    both = _half_swap([halves[n] for n in BIG])
    grads = {n: b.reshape(wts[n].shape[1:]) for n, b in zip(BIG, both)}

    small_g = [dg_mix, dg_q, dg_kv, dg_ffn, dg_final]
    pack = _pack_small(small_g, dconv[:CONV_K])
    pack = pack.at[ROW_LOSS].set(jnp.broadcast_to(loss_part[0, :1], (d,)))
    red = _allreduce_small(pack)
    loss = red[ROW_LOSS, 0]
    cw = conv_w.shape[2]
    conv_g = lax.dynamic_slice(red, (ROW_CONV, chip * cw), (CONV_K, cw))
    small_grad = _pack_small([red[i] for i in range(len(NORMS))], conv_g)
    small = [_pack_small([t[n] for n in NORMS], t["conv_w"][0]) for t in (wts, mom, var)]
    s_delta, s_m, s_v = _adamw(small[0], small_grad, small[1], small[2], "adamw_small")

    out = {"grad": {}, "delta": {}, "new_m": {}, "new_v": {}}
    for n in BIG:
        shp = wts[n].shape
        dl, m2, v2 = _adamw(wts[n][0], grads[n], mom[n][0], var[n][0], "adamw_" + n)
        out["grad"][n] = grads[n].reshape(shp)
        out["delta"][n], out["new_m"][n], out["new_v"][n] = dl.reshape(shp), m2.reshape(shp), v2.reshape(shp)
    for key, blk in (("grad", small_grad), ("delta", s_delta), ("new_m", s_m), ("new_v", s_v)):
        for i, n in enumerate(NORMS):
            out[key][n] = blk[i].reshape(wts[n].shape)
        out[key]["conv_w"] = blk[ROW_CONV:ROW_CONV + CONV_K, :cw].reshape(conv_w.shape)

    return (loss, grad_x[None], *[out["grad"][n] for n in ORDER], *[out["delta"][n] for n in ORDER],
            *[out["new_m"][n] for n in ORDER], *[out["new_v"][n] for n in ORDER])
```

```python
import math

import jax
import jax.numpy as jnp
from jax import lax
from jax.experimental import pallas as pl
from jax.experimental.pallas import tpu as pltpu

BF = jnp.bfloat16
F32 = jnp.float32
MESH = pl.DeviceIdType.MESH

SB_HEAD_DIM = 64
LANES = 128
MEM_HEADS = 4
CONV_K = 3
CONV_ROWS = 8
EPS = 1e-6
N_CHIPS = 4
N_DEV = 8
VMEM_LIMIT = 56 * 1024 * 1024

ADAM_LR = 0.001
ADAM_B1 = 0.9
ADAM_B2 = 0.999
ADAM_EPS = 1e-08
ADAM_WD = 0.01
ADAM_STEP = 10

SMALL_ROWS = 16
ROW_CONV = 5
ROW_LOSS = 8


def _params(sem=None, **kw):
    return pltpu.CompilerParams(dimension_semantics=sem, vmem_limit_bytes=VMEM_LIMIT, **kw)


def _tile(dim, pref):
    if dim <= pref:
        return dim
    for step in (LANES, 8):
        t = (pref // step) * step
        while t >= step:
            if dim % t == 0:
                return t
            t -= step
    raise ValueError(f"no tile of {dim} under {pref}")


def _matmul(a, b, *, ta=False, tb=False, tm=1024, tn=512, out_dtype=BF, resid=None, rider=None, name):
    if ta:
        kdim, m = a.shape
    else:
        m, kdim = a.shape
    n = b.shape[0] if tb else b.shape[1]
    tm, tn = _tile(m, tm), _tile(n, tn)
    a_spec = pl.BlockSpec((kdim, tm), lambda i, j: (0, i)) if ta else pl.BlockSpec((tm, kdim), lambda i, j: (i, 0))
    b_spec = pl.BlockSpec((tn, kdim), lambda i, j: (j, 0)) if tb else pl.BlockSpec((kdim, tn), lambda i, j: (0, j))
    o_spec = pl.BlockSpec((tm, tn), lambda i, j: (i, j))
    dims = (((0 if ta else 1,), (1 if tb else 0,)), ((), ()))
    has_res = resid is not None
    n_in = 2 + has_res

    def body(*refs):
        ins, (o_ref,) = _host_refs(refs, rider, n_in, 1)
        av, bv = ins[0][...], ins[1][...]
        if av.dtype != BF:
            av = av.astype(BF)
        if bv.dtype != BF:
            bv = bv.astype(BF)
        acc = lax.dot_general(av, bv, dims, preferred_element_type=F32)
        if has_res:
            acc = ins[2][...] + acc
        o_ref[...] = acc.astype(o_ref.dtype)
        if rider:
            rider.run(refs, n_in, 1, (pl.program_id(0) == 0) & (pl.program_id(1) == 0),
                      (pl.program_id(0) == m // tm - 1) & (pl.program_id(1) == n // tn - 1))

    res = pl.pallas_call(
        body, name=name, grid=(m // tm, n // tn),
        compiler_params=_params(("arbitrary", "arbitrary") if rider else ("parallel", "parallel"),
                                has_side_effects=rider is not None),
        **_with_rider(rider, n_in, 1, [a_spec, b_spec] + ([o_spec] if has_res else []), [o_spec],
                      [jax.ShapeDtypeStruct((m, n), out_dtype)]),
    )(*([a, b] + ([resid] if has_res else []) + (rider.arrays if rider else [])))
    return res if rider else res[0]


def _rowwise(fn, ins, outs, *, rows, tm, name, accs=()):
    tm = _tile(rows, tm)
    in_specs, args = [], []
    for arr, cols, cb in ins:
        if cols is None:
            in_specs.append(pl.BlockSpec(arr.shape, lambda i, nd=arr.ndim: (0,) * nd))
        else:
            in_specs.append(pl.BlockSpec((tm, cols), lambda i, cb=cb: (i, cb)))
        args.append(arr)
    out_specs = [pl.BlockSpec((tm, cols), lambda i: (i, 0)) for cols, _ in outs]
    out_shape = [jax.ShapeDtypeStruct((rows, cols), dt) for cols, dt in outs]
    for r, c in accs:
        out_specs.append(pl.BlockSpec((r, c), lambda i: (0, 0)))
        out_shape.append(jax.ShapeDtypeStruct((r, c), F32))
    n_in, n_out = len(ins), len(outs)

    def body(*refs):
        res = fn(*[r[...] for r in refs[:n_in]])
        if not isinstance(res, (tuple, list)):
            res = (res,)
        refs = refs[n_in:]
        for o_ref, val in zip(refs[:n_out], res[:n_out]):
            o_ref[...] = val.astype(o_ref.dtype)
        first = pl.program_id(0) == 0
        for a_ref, val in zip(refs[n_out:], res[n_out:]):
            @pl.when(first)
            def _(a_ref=a_ref, val=val):
                a_ref[...] = val

            @pl.when(jnp.logical_not(first))
            def _(a_ref=a_ref, val=val):
                a_ref[...] += val

    res = pl.pallas_call(
        body, name=name, grid=(rows // tm,), in_specs=in_specs, out_specs=out_specs, out_shape=out_shape,
        compiler_params=_params(("arbitrary",) if accs else ("parallel",)),
    )(*args)
    return res


def _rstd(xf):
    return lax.rsqrt(jnp.mean(xf * xf, axis=-1, keepdims=True) + EPS)


def _rmsnorm(x, g, name):
    rows, d = x.shape
    return _rowwise(lambda xv, gv: xv * _rstd(xv) * gv, [(x, d, 0), (g, None, None)], [(d, BF)],
                    rows=rows, tm=512, name=name)[0]


def _rmsnorm_bwd(x, g, dy, resid, name, bf_copy=False):
    rows, d = x.shape

    def fn(xv, gv, dyv, *rest):
        dyv = dyv.astype(F32)
        r = _rstd(xv)
        xn = xv * r
        dxn = dyv * gv
        dx = r * (dxn - xn * jnp.mean(dxn * xn, axis=-1, keepdims=True))
        if rest:
            dx = rest[0] + dx
        return (dx,) * (1 + bf_copy) + (jnp.sum(dyv * xn, axis=0, keepdims=True),)

    ins = [(x, d, 0), (g, None, None), (dy, d, 0)] + ([(resid, d, 0)] if resid is not None else [])
    outs = [(d, F32)] + ([(d, BF)] if bf_copy else [])
    return _rowwise(fn, ins, outs, rows=rows, tm=512, name=name, accs=[(1, d)])


def _matmul_norm_bwd(a, w, x, g, resid, *, tm, bf_copy, name, rider=None):
    m, kdim = a.shape
    d = w.shape[0]
    tm = _tile(m, tm)
    row = lambda i: (i, 0)
    whole = lambda i: (0, 0)
    n_out = 2 + bf_copy

    def body(*refs):
        (a_ref, w_ref, x_ref, g_ref, r_ref), outs = _host_refs(refs, rider, 5, n_out)
        dy = lax.dot_general(a_ref[...], w_ref[...], NT, preferred_element_type=F32)
        xv = x_ref[...]
        r = _rstd(xv)
        xn = xv * r
        dxn = dy * g_ref[...]
        dx = r_ref[...] + r * (dxn - xn * jnp.mean(dxn * xn, axis=-1, keepdims=True))
        outs[0][...] = dx
        if bf_copy:
            outs[1][...] = dx.astype(BF)
        dg = jnp.sum(dy * xn, axis=0, keepdims=True)
        first = pl.program_id(0) == 0

        @pl.when(first)
        def _():
            outs[-1][...] = dg

        @pl.when(jnp.logical_not(first))
        def _():
            outs[-1][...] += dg

        if rider:
            rider.run(refs, 5, n_out, first, pl.program_id(0) == m // tm - 1)

    tok = pl.BlockSpec((tm, d), row)
    out_specs = [tok] + ([tok] if bf_copy else []) + [pl.BlockSpec((1, d), whole)]
    out_shape = ([jax.ShapeDtypeStruct((m, d), F32)] + ([jax.ShapeDtypeStruct((m, d), BF)] if bf_copy else [])
                 + [jax.ShapeDtypeStruct((1, d), F32)])
    return pl.pallas_call(
        body, name=name, grid=(m // tm,),
        compiler_params=_params(("arbitrary",), has_side_effects=rider is not None),
        **_with_rider(
            rider, 5, n_out,
            [pl.BlockSpec((tm, kdim), row), pl.BlockSpec((d, kdim), whole), tok, pl.BlockSpec((1, d), whole), tok],
            out_specs, out_shape),
    )(a, w, x, g, resid, *(rider.arrays if rider else []))


def _norm_ffn_in_swiglu(x, g, w, *, tm=1024, tn=1408, rider=None):
    m, kdim = x.shape
    f = w.shape[1] // 2
    tm, tn = _tile(m, tm), _tile(f, tn)
    nj = f // tn

    def body(*refs):
        (x_ref, g_ref, wg_ref, wu_ref), (h_ref, gate_ref, up_ref, act_ref) = _host_refs(refs, rider, 4, 4)
        if rider:
            rider.run(refs, 4, 4, (pl.program_id(0) == 0) & (pl.program_id(1) == 0),
                      (pl.program_id(0) == m // tm - 1) & (pl.program_id(1) == nj - 1))

        @pl.when(pl.program_id(1) == 0)
        def _():
            xv = x_ref[...]
            h_ref[...] = (xv * _rstd(xv) * g_ref[...]).astype(h_ref.dtype)

        hv = h_ref[...]
        gate = jnp.dot(hv, wg_ref[...], preferred_element_type=F32)
        up = jnp.dot(hv, wu_ref[...], preferred_element_type=F32)
        gate_ref[...] = gate.astype(gate_ref.dtype)
        up_ref[...] = up.astype(up_ref.dtype)
        act_ref[...] = (gate * jax.nn.sigmoid(gate) * up).astype(act_ref.dtype)

    tile = pl.BlockSpec((tm, tn), lambda i, j: (i, j))
    rows = pl.BlockSpec((tm, kdim), lambda i, j: (i, 0))
    return pl.pallas_call(
        body, name="norm_ffn_in_swiglu", grid=(m // tm, nj),
        compiler_params=_params(("arbitrary", "arbitrary") if rider else ("parallel", "arbitrary"),
                                has_side_effects=rider is not None),
        **_with_rider(
            rider, 4, 4,
            [rows, pl.BlockSpec((1, kdim), lambda i, j: (0, 0)), pl.BlockSpec((kdim, tn), lambda i, j: (0, j)),
             pl.BlockSpec((kdim, tn), lambda i, j: (0, nj + j))],
            [rows, tile, tile, tile],
            [jax.ShapeDtypeStruct((m, kdim), BF)] + [jax.ShapeDtypeStruct((m, f), BF)] * 3),
    )(x, g, w, w, *(rider.arrays if rider else []))


def _norm_matmul(x, g, w, *, tm=1024, tn, name):
    m, kdim = x.shape
    n = w.shape[1]
    tm, tn = _tile(m, tm), _tile(n, tn)

    def body(x_ref, g_ref, w_ref, h_ref, o_ref):
        @pl.when(pl.program_id(1) == 0)
        def _():
            xv = x_ref[...]
            h_ref[...] = (xv * _rstd(xv) * g_ref[...]).astype(h_ref.dtype)

        o_ref[...] = jnp.dot(h_ref[...], w_ref[...], preferred_element_type=F32).astype(o_ref.dtype)

    rows = pl.BlockSpec((tm, kdim), lambda i, j: (i, 0))
    return pl.pallas_call(
        body, name=name, grid=(m // tm, n // tn),
        in_specs=[rows, pl.BlockSpec((1, kdim), lambda i, j: (0, 0)), pl.BlockSpec((kdim, tn), lambda i, j: (0, j))],
        out_specs=[rows, pl.BlockSpec((tm, tn), lambda i, j: (i, j))],
        out_shape=[jax.ShapeDtypeStruct((m, kdim), BF), jax.ShapeDtypeStruct((m, n), BF)],
        compiler_params=_params(("parallel", "arbitrary")),
    )(x, g, w)


def _ffn_out_loss(act, w, resid, g, target, *, tm=512):
    m, f = act.shape
    d = w.shape[1]
    tm = _tile(m, tm)
    row = lambda i: (i, 0)
    whole = lambda i: (0, 0)

    def body(a_ref, w_ref, r_ref, g_ref, t_ref, dx_ref, dxb_ref, dg_ref, loss_ref):
        xv = r_ref[...] + jnp.dot(a_ref[...], w_ref[...], preferred_element_type=F32)
        gv = g_ref[...]
        r = _rstd(xv)
        xn = xv * r
        err = xn * gv - t_ref[...]
        loss = 0.5 * jnp.sum(jnp.mean(err * err, axis=-1, keepdims=True), axis=0, keepdims=True)
        dyv = err * (1.0 / d)
        dxn = dyv * gv
        dx = r * (dxn - xn * jnp.mean(dxn * xn, axis=-1, keepdims=True))
        dx_ref[...] = dx
        dxb_ref[...] = dx.astype(dxb_ref.dtype)
        dg = jnp.sum(dyv * xn, axis=0, keepdims=True)
        loss_b = jnp.broadcast_to(loss, (1, LANES))
        first = pl.program_id(0) == 0

        @pl.when(first)
        def _():
            dg_ref[...] = dg
            loss_ref[...] = loss_b

        @pl.when(jnp.logical_not(first))
        def _():
            dg_ref[...] += dg
            loss_ref[...] += loss_b

    tok = pl.BlockSpec((tm, d), row)
    return pl.pallas_call(
        body, name="ffn_out_loss", grid=(m // tm,),
        in_specs=[pl.BlockSpec((tm, f), row), pl.BlockSpec((f, d), whole), tok, pl.BlockSpec((1, d), whole), tok],
        out_specs=[tok, tok, pl.BlockSpec((1, d), whole), pl.BlockSpec((1, LANES), whole)],
        out_shape=[jax.ShapeDtypeStruct((m, d), F32), jax.ShapeDtypeStruct((m, d), BF),
                   jax.ShapeDtypeStruct((1, d), F32), jax.ShapeDtypeStruct((1, LANES), F32)],
        compiler_params=_params(("arbitrary",)),
    )(act, w, resid, g, target)


def _d_act_swiglu(dx, w, gate, up, *, tm=256, rider=None):
    m, d = dx.shape
    f = w.shape[0]
    tm = _tile(m, tm)
    row = lambda i: (i, 0)

    def body(*refs):
        (dx_ref, w_ref, gate_ref, up_ref), (o_ref,) = _host_refs(refs, rider, 4, 1)
        if rider:
            rider.run(refs, 4, 1, pl.program_id(0) == 0, pl.program_id(0) == m // tm - 1)
        da = lax.dot_general(dx_ref[...], w_ref[...], NT, preferred_element_type=F32)
        gv, uv = gate_ref[...].astype(F32), up_ref[...].astype(F32)
        sg = jax.nn.sigmoid(gv)
        dgate = da * uv * (sg * (1.0 + gv * (1.0 - sg)))
        o_ref[...] = jnp.concatenate([dgate, da * (gv * sg)], axis=-1).astype(o_ref.dtype)

    return pl.pallas_call(
        body, name="d_act_swiglu", grid=(m // tm,),
        compiler_params=_params(("arbitrary",) if rider else ("parallel",), has_side_effects=rider is not None),
        **_with_rider(
            rider, 4, 1,
            [pl.BlockSpec((tm, d), row), pl.BlockSpec((f, d), lambda i: (0, 0)),
             pl.BlockSpec((tm, f), row), pl.BlockSpec((tm, f), row)],
            [pl.BlockSpec((tm, 2 * f), row)], [jax.ShapeDtypeStruct((m, 2 * f), BF)]),
    )(dx, w, gate, up, *(rider.arrays if rider else []))


GATE_A_BLK, GATE_B_BLK = 3, 4


def _branches_merge(o_a, y_b, w_a, w_b, proj, *, tm=1024):
    m, kdim = o_a.shape
    d = w_a.shape[1]
    tm = _tile(m, tm)
    row = lambda i: (i, 0)

    def body(a_ref, b_ref, wa_ref, wb_ref, ga_ref, gb_ref, bra_ref, brb_ref, merged_ref):
        bra = jnp.dot(a_ref[...], wa_ref[...], preferred_element_type=F32)
        brb = jnp.dot(b_ref[...], wb_ref[...], preferred_element_type=F32)
        bra_ref[...] = bra.astype(bra_ref.dtype)
        brb_ref[...] = brb.astype(brb_ref.dtype)
        merged = jax.nn.sigmoid(ga_ref[...].astype(F32)) * bra + jax.nn.sigmoid(gb_ref[...].astype(F32)) * brb
        merged_ref[...] = merged.astype(merged_ref.dtype)

    tok = pl.BlockSpec((tm, d), row)
    return pl.pallas_call(
        body, name="branches_merge", grid=(m // tm,),
        in_specs=[pl.BlockSpec((tm, kdim), row), pl.BlockSpec((tm, kdim), row),
                  pl.BlockSpec((kdim, d), lambda i: (0, 0)), pl.BlockSpec((kdim, d), lambda i: (0, 0)),
                  pl.BlockSpec((tm, d), lambda i: (i, GATE_A_BLK)), pl.BlockSpec((tm, d), lambda i: (i, GATE_B_BLK))],
        out_specs=[tok, tok, tok], out_shape=[jax.ShapeDtypeStruct((m, d), BF)] * 3,
        compiler_params=_params(("parallel",)),
    )(o_a, y_b, w_a, w_b, proj, proj)


def _d_merged_gates(dx, w_mix, proj, br_a, br_b, *, tm=512):
    m, d = dx.shape
    tm = _tile(m, tm)
    row = lambda i: (i, 0)

    def body(dx_ref, w_ref, ga_ref, gb_ref, bra_ref, brb_ref, dbra_ref, dbrb_ref, dga_ref, dgb_ref):
        dm = lax.dot_general(dx_ref[...], w_ref[...], NT, preferred_element_type=F32)
        sa, sb = jax.nn.sigmoid(ga_ref[...].astype(F32)), jax.nn.sigmoid(gb_ref[...].astype(F32))
        dbra_ref[...] = (dm * sa).astype(dbra_ref.dtype)
        dbrb_ref[...] = (dm * sb).astype(dbrb_ref.dtype)
        dga_ref[...] = (dm * bra_ref[...].astype(F32) * (sa * (1.0 - sa))).astype(dga_ref.dtype)
        dgb_ref[...] = (dm * brb_ref[...].astype(F32) * (sb * (1.0 - sb))).astype(dgb_ref.dtype)

    tok = pl.BlockSpec((tm, d), row)
    return pl.pallas_call(
        body, name="d_merged_gates", grid=(m // tm,),
        in_specs=[tok, pl.BlockSpec((d, d), lambda i: (0, 0)),
                  pl.BlockSpec((tm, d), lambda i: (i, GATE_A_BLK)), pl.BlockSpec((tm, d), lambda i: (i, GATE_B_BLK)),
                  tok, tok],
        out_specs=[tok] * 4, out_shape=[jax.ShapeDtypeStruct((m, d), BF)] * 4,
        compiler_params=_params(("parallel",)),
    )(dx, w_mix, proj, proj, br_a, br_b)


SB_TK = 128
SB_KT = 2


def _sb_consts(tq):
    tk = SB_TK
    diff = lax.broadcasted_iota(jnp.int32, (tq, tk), 1) - lax.broadcasted_iota(jnp.int32, (tq, tk), 0)
    rj = lax.broadcasted_iota(jnp.int32, (2 * tk, 2 * tk), 0) & (tk - 1)
    cj = lax.broadcasted_iota(jnp.int32, (2 * tk, 2 * tk), 1)
    ones_half = cj >= tk
    later = jnp.where((rj > cj) | ones_half, 1.0, 0.0).astype(BF)
    later_incl = jnp.where((rj >= cj) | ones_half, 1.0, 0.0).astype(BF)
    return diff, later, later_incl


def _split_dot(val, rhs_twice):
    hi = val.astype(BF)
    lo = (val - hi.astype(F32)).astype(BF)
    return jnp.dot(jnp.concatenate([hi, lo], axis=1), rhs_twice, preferred_element_type=F32)


def _log_terms(z):
    sp = jnp.maximum(z, 0.0) + jnp.log(1.0 + jnp.exp(-jnp.abs(z)))
    return z - sp, sp


NT = (((1,), (1,)), ((), ()))
TN = (((0,), (0,)), ((), ()))


def _head_lane_masks(rows):
    lane = lax.broadcasted_iota(jnp.int32, (rows, LANES), 1)
    first = jnp.where(lane < SB_HEAD_DIM, 1.0, 0.0)
    return first.astype(BF), (1.0 - first).astype(BF)


DEAD_LOG = 104.0


def _walk_back(i, step, state, carries_of):
    def alive(st):
        c0, c1 = carries_of(st)
        return jnp.min(jnp.minimum(c0, c1)) < DEAD_LOG

    def cond(loop):
        done, live, _ = loop
        return jnp.logical_and(done < i, live)

    def body(loop):
        done, _, st = loop
        st = step(i - 1 - done, st)
        return done + 1, alive(st), st

    return lax.while_loop(cond, body, (jnp.int32(0), alive(state), state))[2]


def _tail(a, r0):
    return a if r0 == 0 else a[r0:]


def _add_tail(a, r0, delta):
    return a + delta if r0 == 0 else jnp.concatenate([a[:r0], a[r0:] + delta], axis=0)


def _both_heads(tile, masks):
    return jnp.concatenate([tile * masks[0], tile * masks[1]], axis=0)


def _with_rider(rider, n_in, n_out, in_specs, out_specs, out_shape, scratch=()):
    kw = dict(in_specs=list(in_specs), out_specs=list(out_specs), out_shape=list(out_shape),
              scratch_shapes=list(scratch), input_output_aliases={})
    if rider:
        extra = rider.call_args(n_in, n_out)
        kw["in_specs"] += extra["in_specs"]
        kw["out_specs"] += extra["out_specs"]
        kw["out_shape"] += extra["out_shape"]
        kw["scratch_shapes"] += extra["scratch"]
        kw["input_output_aliases"] = extra["aliases"]
    return kw


def _sb_fwd(proj, rider=None):
    s = proj.shape[0]
    tk, tq = SB_TK, SB_KT * SB_TK
    n_pairs = 4
    scale = 1.0 / math.sqrt(SB_HEAD_DIM)

    def body(*refs):
        (q_ref, k_ref, v_ref), (o_ref, o32_ref) = _host_refs(refs, rider, 3, 2)
        i = pl.program_id(1)
        diff, later, _ = _sb_consts(tq)
        qs = (q_ref[...].astype(F32) * scale).astype(BF)
        lane_masks = _head_lane_masks(tk)

        def step(g, state, masked):
            tiles = list(reversed(range(SB_KT)))
            chains = [(t, h) for t in tiles for h in range(2)]
            rows = {t: pl.ds(pl.multiple_of((g * SB_KT + t) * tk, tk), tk) for t in tiles}
            ks = {t: _both_heads(k_ref[rows[t], :], lane_masks) for t in tiles}
            vs = {t: _both_heads(v_ref[rows[t], :], lane_masks) for t in tiles}
            r0 = {t: t * tk if masked else 0 for t in tiles}
            allowed = {t: _tail(diff, r0[t]) < -t * tk for t in tiles}
            zs = {t: lax.dot_general(_tail(qs, r0[t]), ks[t], NT, preferred_element_type=F32) for t in tiles}
            logs = {}
            for t, h in chains:
                log_b, sp = _log_terms(zs[t][:, h * tk:(h + 1) * tk])
                logs[t, h] = (log_b, jnp.where(allowed[t], sp, 0.0) if masked else sp)
            sums = {c: _split_dot(logs[c][1], later) for c in chains}
            carries = list(state[0])
            ws = {}
            for t, h in chains:
                w = jnp.exp(logs[t, h][0] - (sums[t, h][:, :tk] + _tail(carries[h], r0[t])))
                ws[t, h] = (jnp.where(allowed[t], w, 0.0) if masked else w).astype(BF)
                carries[h] = _add_tail(carries[h], r0[t], sums[t, h][:, tk:])
            acc = state[1]
            for t in tiles:
                acc = _add_tail(acc, r0[t], jnp.dot(jnp.concatenate([ws[t, 0], ws[t, 1]], axis=1), vs[t],
                                                    preferred_element_type=F32))
            return tuple(carries), acc

        zero = jnp.zeros((tq, LANES), F32)
        state = step(i, ((zero, zero), zero), True)
        state = _walk_back(i, lambda g, st: step(g, st, False), state, lambda st: st[0])
        o_ref[...] = state[1].astype(o_ref.dtype)
        o32_ref[...] = state[1]
        if rider:
            rider.run(refs, 3, 2, (pl.program_id(0) == 0) & (i == 0),
                      (pl.program_id(0) == n_pairs - 1) & (i == s // tq - 1))

    tok = pl.BlockSpec((tq, LANES), lambda p, i: (i, p))
    return pl.pallas_call(
        body, name="sb_attn_fwd", grid=(n_pairs, s // tq),
        compiler_params=_params(("arbitrary", "arbitrary"), has_side_effects=rider is not None),
        **_with_rider(
            rider, 3, 2,
            [tok, pl.BlockSpec((s, LANES), lambda p, i: (0, n_pairs + p)),
             pl.BlockSpec((s, LANES), lambda p, i: (0, 2 * n_pairs + p))],
            [tok, tok],
            [jax.ShapeDtypeStruct((s, n_pairs * LANES), BF), jax.ShapeDtypeStruct((s, n_pairs * LANES), F32)]),
    )(proj, proj, proj, *(rider.arrays if rider else []))


def _sb_bwd(proj, o32, do_a, rider=None):
    s = proj.shape[0]
    tk, tq = SB_TK, SB_KT * SB_TK
    n_pairs = 4
    scale = 1.0 / math.sqrt(SB_HEAD_DIM)

    def body(*refs):
        (q_ref, k_ref, v_ref, o_ref, do_ref), (dq_ref, dk_ref, dv_ref) = _host_refs(refs, rider, 5, 3)
        i = pl.program_id(1)

        @pl.when(i == 0)
        def _():
            dk_ref[...] = jnp.zeros_like(dk_ref)
            dv_ref[...] = jnp.zeros_like(dv_ref)

        diff, later, later_incl = _sb_consts(tq)
        qs = (q_ref[...].astype(F32) * scale).astype(BF)
        do2 = do_ref[...]
        prod = do2.astype(F32) * o_ref[...]
        lane_masks = _head_lane_masks(tk)
        first_head = lax.broadcasted_iota(jnp.int32, (tq, LANES), 1) < SB_HEAD_DIM
        totals = [jnp.broadcast_to(jnp.sum(jnp.where(keep, prod, 0.0), axis=-1, keepdims=True), (tq, tk))
                  for keep in (first_head, jnp.logical_not(first_head))]
        first_head_k = first_head[:tk]

        def step(g_idx, state, masked):
            tiles = list(reversed(range(SB_KT)))
            chains = [(t, h) for t in tiles for h in range(2)]
            rows = {t: pl.ds(pl.multiple_of((g_idx * SB_KT + t) * tk, tk), tk) for t in tiles}
            ks = {t: _both_heads(k_ref[rows[t], :], lane_masks) for t in tiles}
            vs = {t: _both_heads(v_ref[rows[t], :], lane_masks) for t in tiles}
            r0 = {t: t * tk if masked else 0 for t in tiles}
            allowed = {t: _tail(diff, r0[t]) < -t * tk for t in tiles}
            zs = {t: lax.dot_general(_tail(qs, r0[t]), ks[t], NT, preferred_element_type=F32) for t in tiles}
            dws = {t: lax.dot_general(_tail(do2, r0[t]), vs[t], NT, preferred_element_type=F32) for t in tiles}
            logs = {}
            for t, h in chains:
                log_b, sp = _log_terms(zs[t][:, h * tk:(h + 1) * tk])
                logs[t, h] = (log_b, jnp.where(allowed[t], sp, 0.0) if masked else sp)
            sums = {c: _split_dot(logs[c][1], later) for c in chains}
            c_log, c_g = list(state[0]), list(state[1])
            ws, gs = {}, {}
            for t, h in chains:
                w = jnp.exp(logs[t, h][0] - (sums[t, h][:, :tk] + _tail(c_log[h], r0[t])))
                ws[t, h] = (jnp.where(allowed[t], w, 0.0) if masked else w).astype(BF)
                c_log[h] = _add_tail(c_log[h], r0[t], sums[t, h][:, tk:])
                gs[t, h] = ws[t, h].astype(F32) * dws[t][:, h * tk:(h + 1) * tk]
            gsums = {c: _split_dot(gs[c], later_incl) for c in chains}
            dzs = {}
            for t, h in chains:
                beta = jnp.exp(logs[t, h][0])
                earlier = _tail(totals[h], r0[t]) - (gsums[t, h][:, :tk] + _tail(c_g[h], r0[t]))
                dz = gs[t, h] * (1.0 - beta) - earlier * beta
                dzs[t, h] = (jnp.where(allowed[t], dz, 0.0) if masked else dz).astype(BF)
                c_g[h] = _add_tail(c_g[h], r0[t], gsums[t, h][:, tk:])
            dq = state[2]
            for t in tiles:
                dz_both = jnp.concatenate([dzs[t, 0], dzs[t, 1]], axis=1)
                w_both = jnp.concatenate([ws[t, 0], ws[t, 1]], axis=1)
                dq = _add_tail(dq, r0[t], jnp.dot(dz_both, ks[t], preferred_element_type=F32))
                dk2 = lax.dot_general(dz_both, _tail(qs, r0[t]), TN, preferred_element_type=F32)
                dv2 = lax.dot_general(w_both, _tail(do2, r0[t]), TN, preferred_element_type=F32)
                dk_ref[rows[t], :] += jnp.where(first_head_k, dk2[:tk], dk2[tk:])
                dv_ref[rows[t], :] += jnp.where(first_head_k, dv2[:tk], dv2[tk:])
            return tuple(c_log), tuple(c_g), dq

        zero = jnp.zeros((tq, LANES), F32)
        state = step(i, ((zero, zero), (zero, zero), zero), True)
        state = _walk_back(i, lambda g, st: step(g, st, False), state, lambda st: st[0])
        dq_ref[...] = (state[2] * scale).astype(dq_ref.dtype)
        if rider:
            rider.run(refs, 5, 3, (pl.program_id(0) == 0) & (i == 0),
                      (pl.program_id(0) == n_pairs - 1) & (i == s // tq - 1))

    width = n_pairs * LANES
    tok = pl.BlockSpec((tq, LANES), lambda p, i: (i, p))
    return pl.pallas_call(
        body, name="sb_attn_bwd", grid=(n_pairs, s // tq),
        compiler_params=_params(("arbitrary", "arbitrary"), has_side_effects=rider is not None),
        **_with_rider(
            rider, 5, 3,
            [tok, pl.BlockSpec((s, LANES), lambda p, i: (0, n_pairs + p)),
             pl.BlockSpec((s, LANES), lambda p, i: (0, 2 * n_pairs + p)), tok, tok],
            [tok, pl.BlockSpec((s, LANES), lambda p, i: (0, p)), pl.BlockSpec((s, LANES), lambda p, i: (0, p))],
            [jax.ShapeDtypeStruct((s, width), BF), jax.ShapeDtypeStruct((s, width), F32),
             jax.ShapeDtypeStruct((s, width), F32)]),
    )(proj, proj, proj, o32, do_a, *(rider.arrays if rider else []))


CONV_COL0 = 12


def _shift_rows(v, k):
    n = v.shape[0]
    row = lax.broadcasted_iota(jnp.int32, v.shape, 0)
    rolled = pltpu.roll(v, k % n, axis=0)
    keep = row >= k if k > 0 else row < n + k
    return jnp.where(keep, rolled, 0.0)


def _conv_specs(s):
    return [pl.BlockSpec((s, LANES), lambda cb: (0, CONV_COL0 + cb)),
            pl.BlockSpec((s, LANES), lambda cb: (0, CONV_COL0 + 4 + cb)),
            pl.BlockSpec((s, LANES), lambda cb: (0, CONV_COL0 + 8 + cb)),
            pl.BlockSpec((CONV_ROWS, LANES), lambda cb: (0, cb))]


def _conv_fwd(proj, conv_w):
    s = proj.shape[0]

    def body(u_ref, gb_ref, gc_ref, w_ref, y_ref):
        cu = gc_ref[...].astype(F32) * u_ref[...].astype(F32)
        w = w_ref[...]
        y = w[0:1] * _shift_rows(cu, 2) + w[1:2] * _shift_rows(cu, 1) + w[2:3] * cu
        y_ref[...] = (gb_ref[...].astype(F32) * y).astype(y_ref.dtype)

    return pl.pallas_call(
        body, name="conv_fwd", grid=(4,), in_specs=_conv_specs(s),
        out_specs=pl.BlockSpec((s, LANES), lambda cb: (0, cb)),
        out_shape=jax.ShapeDtypeStruct((s, 4 * LANES), BF),
        compiler_params=_params(("parallel",)),
    )(proj, proj, proj, conv_w)


def _conv_bwd(proj, conv_w, dy, rider=None):
    s = proj.shape[0]
    n_blocks = 4

    def body(*refs):
        (u_ref, gb_ref, gc_ref, w_ref, dy_ref), (du_ref, dgb_ref, dgc_ref, dw_ref) = _host_refs(refs, rider, 5, 4)
        if rider:
            rider.run(refs, 5, 4, pl.program_id(0) == 0, pl.program_id(0) == n_blocks - 1)
        u, gc = u_ref[...].astype(F32), gc_ref[...].astype(F32)
        dyv = dy_ref[...].astype(F32)
        w = w_ref[...]
        cu = gc * u
        cu1, cu2 = _shift_rows(cu, 1), _shift_rows(cu, 2)
        conv = w[0:1] * cu2 + w[1:2] * cu1 + w[2:3] * cu
        dgb_ref[...] = (dyv * conv).astype(dgb_ref.dtype)
        dc = dyv * gb_ref[...].astype(F32)
        dcu = w[2:3] * dc + w[1:2] * _shift_rows(dc, -1) + w[0:1] * _shift_rows(dc, -2)
        dgc_ref[...] = (dcu * u).astype(dgc_ref.dtype)
        du_ref[...] = (dcu * gc).astype(du_ref.dtype)
        tap_row = lax.broadcasted_iota(jnp.int32, (CONV_ROWS, LANES), 0)
        dw = jnp.zeros((CONV_ROWS, LANES), F32)
        for t, shifted in enumerate((cu2, cu1, cu)):
            dw = jnp.where(tap_row == t, jnp.sum(dc * shifted, axis=0, keepdims=True), dw)
        dw_ref[...] = dw

    col = pl.BlockSpec((s, LANES), lambda cb: (0, cb))
    act = jax.ShapeDtypeStruct((s, 4 * LANES), BF)
    return pl.pallas_call(
        body, name="conv_bwd", grid=(n_blocks,),
        compiler_params=_params(("arbitrary",) if rider else ("parallel",), has_side_effects=rider is not None),
        **_with_rider(
            rider, 5, 4, _conv_specs(s) + [col],
            [col, col, col, pl.BlockSpec((CONV_ROWS, LANES), lambda cb: (0, cb))],
            [act, act, act, jax.ShapeDtypeStruct((CONV_ROWS, n_blocks * LANES), F32)]),
    )(proj, proj, proj, conv_w, dy, *(rider.arrays if rider else []))


def _mem_probs(q, k, scale):
    sc = lax.dot_general(q, k, NT, preferred_element_type=F32) * scale
    p = jnp.exp(sc - jnp.max(sc, axis=-1, keepdims=True))
    return p / jnp.sum(p, axis=-1, keepdims=True)


def _mem_fwd(q_m, kv, tq=2048):
    s, d = q_m.shape
    mlen = kv.shape[0]
    hd = d // MEM_HEADS
    tq = _tile(s, tq)
    scale = 1.0 / math.sqrt(hd)

    def body(q_ref, k_ref, v_ref, o_ref):
        p = _mem_probs(q_ref[...], k_ref[...], scale)
        o_ref[...] = jnp.dot(p.astype(BF), v_ref[...], preferred_element_type=F32).astype(o_ref.dtype)

    return pl.pallas_call(
        body, name="mem_attn_fwd", grid=(MEM_HEADS, s // tq),
        in_specs=[pl.BlockSpec((tq, hd), lambda h, i: (i, h)),
                  pl.BlockSpec((mlen, hd), lambda h, i: (0, h)),
                  pl.BlockSpec((mlen, hd), lambda h, i: (0, MEM_HEADS + h))],
        out_specs=pl.BlockSpec((tq, hd), lambda h, i: (i, h)),
        out_shape=jax.ShapeDtypeStruct((s, d), BF),
        compiler_params=_params(("parallel", "parallel")),
    )(q_m, kv, kv)


def _mem_bwd(q_m, kv, do_m, tq=2048):
    s, d = q_m.shape
    mlen = kv.shape[0]
    hd = d // MEM_HEADS
    tq = _tile(s, tq)
    scale = 1.0 / math.sqrt(hd)

    def body(q_ref, k_ref, v_ref, do_ref, dq_ref, dk_ref, dv_ref):
        q, k, v, do = q_ref[...], k_ref[...], v_ref[...], do_ref[...]
        p = _mem_probs(q, k, scale)
        dp = lax.dot_general(do, v, NT, preferred_element_type=F32)
        ds = p * (dp - jnp.sum(dp * p, axis=-1, keepdims=True)) * scale
        dsb = ds.astype(BF)
        dq_ref[...] = jnp.dot(dsb, k, preferred_element_type=F32).astype(dq_ref.dtype)
        dk = lax.dot_general(dsb, q, TN, preferred_element_type=F32)
        dv = lax.dot_general(p.astype(BF), do, TN, preferred_element_type=F32)
        first = pl.program_id(1) == 0

        @pl.when(first)
        def _():
            dk_ref[...] = dk
            dv_ref[...] = dv

        @pl.when(jnp.logical_not(first))
        def _():
            dk_ref[...] += dk
            dv_ref[...] += dv

    tok = pl.BlockSpec((tq, hd), lambda h, i: (i, h))
    memb = pl.BlockSpec((mlen, hd), lambda h, i: (0, h))
    return pl.pallas_call(
        body, name="mem_attn_bwd", grid=(MEM_HEADS, s // tq),
        in_specs=[tok, memb, pl.BlockSpec((mlen, hd), lambda h, i: (0, MEM_HEADS + h)), tok],
        out_specs=[tok, memb, memb],
        out_shape=[jax.ShapeDtypeStruct((s, d), BF), jax.ShapeDtypeStruct((mlen, d), F32),
                   jax.ShapeDtypeStruct((mlen, d), F32)],
        compiler_params=_params(("parallel", "arbitrary")),
    )(q_m, kv, kv, do_m)


def _place():
    x, y, c = lax.axis_index("x"), lax.axis_index("y"), lax.axis_index("c")
    other_chips = [(1 - x, y), (x, 1 - y), (1 - x, 1 - y)]
    return x, y, c, other_chips


def _chip_no(cx, cy):
    return 2 * cx + cy


HBM_SPEC = pl.BlockSpec(memory_space=pl.ANY)


def _cast_place(shard, axis, place, dtype, name):
    r, c = shard.shape
    tr = _tile(r, max(16, 1048576 // c))
    nblk = r // tr
    if axis == 1:
        full, out_map = (r, N_CHIPS * c), lambda i, pref: (i, pref[0])
    else:
        full, out_map = (N_CHIPS * r, c), lambda i, pref: (pref[0] * nblk + i, 0)

    def body(pref, s_ref, o_ref):
        o_ref[...] = s_ref[...].astype(o_ref.dtype)

    return pl.pallas_call(
        body, name=name,
        grid_spec=pltpu.PrefetchScalarGridSpec(
            num_scalar_prefetch=1, grid=(nblk,),
            in_specs=[pl.BlockSpec((tr, c), lambda i, pref: (i, 0))],
            out_specs=pl.BlockSpec((tr, c), out_map)),
        out_shape=jax.ShapeDtypeStruct(full, dtype),
        compiler_params=_params(("parallel",)),
    )(place, shard)


def _region(ref, axis, chip_no, half):
    width = ref.shape[axis] // N_CHIPS
    start = pl.multiple_of(chip_no * width, width)
    if axis == 1:
        if half is None:
            return ref.at[:, pl.ds(start, width)]
        hr = ref.shape[0] // 2
        return ref.at[pl.ds(pl.multiple_of(half * hr, hr), hr), pl.ds(start, width)]
    if half is None:
        return ref.at[pl.ds(start, width), :]
    hr = width // 2
    return ref.at[pl.ds(pl.multiple_of(start + half * hr, hr), hr), :]


def _gather_weights(fulls, axes, split):
    n = len(fulls)

    def body(*refs):
        outs = refs[n:2 * n]
        send, recv, fsend, frecv = refs[2 * n:]
        x, y, c, others = _place()
        me = _chip_no(x, y)
        sibling = (x, y, 1 - c)

        def copy(w, chip_no, half, sems, p, to):
            reg = _region(outs[w], axes[w], chip_no, half)
            return pltpu.make_async_remote_copy(
                src_ref=reg, dst_ref=reg, send_sem=sems[0].at[w, p], recv_sem=sems[1].at[w, p],
                device_id=to, device_id_type=MESH)

        for w in range(n):
            for p, chip in enumerate(others):
                copy(w, me, c if split[w] else None, (send, recv), p, (chip[0], chip[1], c)).start()
        for w in range(n):
            for p, chip in enumerate(others):
                half = c if split[w] else None
                copy(w, _chip_no(*chip), half, (send, recv), p, (chip[0], chip[1], c)).wait_recv()
                if split[w]:
                    copy(w, _chip_no(*chip), c, (fsend, frecv), p, sibling).start()
        for w in range(n):
            for p, chip in enumerate(others):
                copy(w, me, c if split[w] else None, (send, recv), p, (chip[0], chip[1], c)).wait_send()
                if split[w]:
                    handed = copy(w, _chip_no(*chip), 1 - c, (fsend, frecv), p, sibling)
                    handed.wait_recv()
                    handed.wait_send()

    return pl.pallas_call(
        body, name="gather_weights",
        in_specs=[HBM_SPEC] * n, out_specs=[HBM_SPEC] * n,
        out_shape=[jax.ShapeDtypeStruct(f.shape, f.dtype) for f in fulls],
        input_output_aliases={i: i for i in range(n)},
        scratch_shapes=[pltpu.SemaphoreType.DMA((n, 3))] * 4,
        compiler_params=pltpu.CompilerParams(has_side_effects=True),
    )(*fulls)


def _fetch_copy(refs, axes, whole, send, recv, w, p, chip, c, arriving):
    owner = _chip_no(*chip) if arriving else _chip_no(lax.axis_index("x"), lax.axis_index("y"))
    reg = _region(refs[w], axes[w], owner, None if whole[w] else c)
    return pltpu.make_async_remote_copy(
        src_ref=reg, dst_ref=reg, send_sem=send[p], recv_sem=recv[p],
        device_id=(chip[0], chip[1], c), device_id_type=MESH)


N_PEERS = N_CHIPS - 1


class _Rider:
    def __init__(self, arrays, outs, aliases, start, wait):
        self.arrays, self.outs, self.aliases, self.start, self.wait = list(arrays), list(outs), aliases, start, wait
        self.scratch = [pltpu.SemaphoreType.DMA((len(self.arrays), N_PEERS))] * 2

    def run(self, refs, n_in, n_out, first, last):
        ra, ro = len(self.arrays), len(self.outs)
        ins = refs[n_in:n_in + ra]
        outs = refs[n_in + ra + n_out:n_in + ra + n_out + ro]
        send, recv = refs[-2], refs[-1]

        @pl.when(first)
        def _():
            self.start(ins, outs, send, recv)

        @pl.when(last)
        def _():
            self.wait(ins, outs, send, recv)

    def call_args(self, n_in, n_out):
        ra = len(self.arrays)
        return dict(in_specs=[HBM_SPEC] * ra, out_specs=[HBM_SPEC] * len(self.outs), out_shape=self.outs,
                    aliases={n_in + k: n_out + o for k, o in self.aliases.items()}, scratch=self.scratch)


def _host_refs(refs, rider, n_in, n_out):
    ra = len(rider.arrays) if rider else 0
    return refs[:n_in], refs[n_in + ra:n_in + ra + n_out]


def _riding_fetch(fulls, axes):
    n = len(fulls)
    whole = [False] * n

    def sems(ref, w):
        return [ref.at[w, q] for q in range(N_PEERS)]

    def start(ins, outs, send, recv):
        _, _, c, others = _place()
        for w in range(n):
            for p, chip in enumerate(others):
                _fetch_copy(outs, axes, whole, sems(send, w), sems(recv, w), w, p, chip, c, False).start()

    def wait(ins, outs, send, recv):
        _, _, c, others = _place()
        for w in range(n):
            for p, chip in enumerate(others):
                _fetch_copy(outs, axes, whole, sems(send, w), sems(recv, w), w, p, chip, c, False).wait_send()
                _fetch_copy(outs, axes, whole, sems(send, w), sems(recv, w), w, p, chip, c, True).wait_recv()

    return _Rider(fulls, [jax.ShapeDtypeStruct(f.shape, f.dtype) for f in fulls], {k: k for k in range(n)}, start, wait)


def _hand_on(fulls, axes, name):
    n = len(fulls)

    def body(*refs):
        outs = refs[n:2 * n]
        send, recv = refs[2 * n:]
        x, y, c, others = _place()

        def copy(w, p, chip, half):
            reg = _region(outs[w], axes[w], _chip_no(*chip), half)
            return pltpu.make_async_remote_copy(
                src_ref=reg, dst_ref=reg, send_sem=send.at[w, p], recv_sem=recv.at[w, p],
                device_id=(x, y, 1 - c), device_id_type=MESH)

        for w in range(n):
            for p, chip in enumerate(others):
                copy(w, p, chip, c).start()
        for w in range(n):
            for p, chip in enumerate(others):
                copy(w, p, chip, 1 - c).wait()

    return pl.pallas_call(
        body, name=name,
        in_specs=[HBM_SPEC] * n, out_specs=[HBM_SPEC] * n,
        out_shape=[jax.ShapeDtypeStruct(f.shape, f.dtype) for f in fulls],
        input_output_aliases={i: i for i in range(n)},
        scratch_shapes=[pltpu.SemaphoreType.DMA((n, 3)), pltpu.SemaphoreType.DMA((n, 3))],
        compiler_params=pltpu.CompilerParams(has_side_effects=True),
    )(*fulls)


def _pair_exchange(grads, name):
    n = len(grads)

    def body(*refs):
        ins, outs = refs[:n], refs[n:2 * n]
        send, recv = refs[2 * n:]
        x, y, c, _ = _place()
        cps = []
        for w in range(n):
            cp = pltpu.make_async_remote_copy(
                src_ref=ins[w].at[:, 1 - c], dst_ref=outs[w], send_sem=send.at[w], recv_sem=recv.at[w],
                device_id=(x, y, 1 - c), device_id_type=MESH)
            cp.start()
            cps.append(cp)
        for cp in cps:
            cp.wait()

    return pl.pallas_call(
        body, name=name,
        in_specs=[HBM_SPEC] * n, out_specs=[HBM_SPEC] * n,
        out_shape=[jax.ShapeDtypeStruct((g.shape[0],) + g.shape[2:], g.dtype) for g in grads],
        scratch_shapes=[pltpu.SemaphoreType.DMA((n,)), pltpu.SemaphoreType.DMA((n,))],
        compiler_params=pltpu.CompilerParams(has_side_effects=True),
    )(*grads)


def _pair_add(g4, got, core, name):
    nj, _, hr, cdim = g4.shape
    tr = _tile(hr, max(8, 524288 // cdim))

    def body(core_ref, a_ref, b_ref, o_ref):
        o_ref[...] = (a_ref[...].astype(F32) + b_ref[...].astype(F32)).astype(o_ref.dtype)

    return pl.pallas_call(
        body, name=name,
        grid_spec=pltpu.PrefetchScalarGridSpec(
            num_scalar_prefetch=1, grid=(nj, hr // tr),
            in_specs=[pl.BlockSpec((1, None, tr, cdim), lambda j, i, core_ref: (j, core_ref[0], i, 0)),
                      pl.BlockSpec((1, tr, cdim), lambda j, i, core_ref: (j, i, 0))],
            out_specs=pl.BlockSpec((1, tr, cdim), lambda j, i, core_ref: (j, i, 0))),
        out_shape=jax.ShapeDtypeStruct((nj, hr, cdim), BF),
        compiler_params=_params(("parallel", "parallel")),
    )(core, g4, got)


def _piece(ref, axis, j, hc):
    if axis == 0:
        return ref.at[j]
    return ref.at[0, :, pl.ds(pl.multiple_of(j * hc, hc), hc)]


def _slot_shapes(sums, axes):
    return [(N_CHIPS - 1, sm.shape[1], sm.shape[2] // (1 if ax == 0 else N_CHIPS)) for sm, ax in zip(sums, axes)]


def _slot_copy(sums, lands, axes, send, recv, w, p, chip, c):
    return pltpu.make_async_remote_copy(
        src_ref=_piece(sums[w], axes[w], _chip_no(*chip), lands[w].shape[2]), dst_ref=lands[w].at[p],
        send_sem=send[p], recv_sem=recv[p],
        device_id=(chip[0], chip[1], c), device_id_type=MESH)


def _riding_pairs(views):
    n = len(views)

    def copies(ins, outs, send, recv):
        x, y, c, _ = _place()
        return [pltpu.make_async_remote_copy(
            src_ref=ins[w].at[:, 1 - c], dst_ref=outs[w], send_sem=send.at[w, 0], recv_sem=recv.at[w, 0],
            device_id=(x, y, 1 - c), device_id_type=MESH) for w in range(n)]

    def start(ins, outs, send, recv):
        for cp in copies(ins, outs, send, recv):
            cp.start()

    def wait(ins, outs, send, recv):
        for cp in copies(ins, outs, send, recv):
            cp.wait()

    outs = [jax.ShapeDtypeStruct((g.shape[0],) + g.shape[2:], g.dtype) for g in views]
    return _Rider(views, outs, {}, start, wait)


def _riding_slots(sums, axes):
    n = len(sums)
    shapes = _slot_shapes(sums, axes)

    def copies(ins, outs, send, recv):
        _, _, c, others = _place()
        return [_slot_copy(ins, outs, axes, [send.at[w, q] for q in range(N_PEERS)],
                           [recv.at[w, q] for q in range(N_PEERS)], w, p, chip, c)
                for w in range(n) for p, chip in enumerate(others)]

    def start(ins, outs, send, recv):
        for cp in copies(ins, outs, send, recv):
            cp.start()

    def wait(ins, outs, send, recv):
        for cp in copies(ins, outs, send, recv):
            cp.wait()

    return _Rider(sums, [jax.ShapeDtypeStruct(sh, sm.dtype) for sh, sm in zip(shapes, sums)], {}, start, wait)


def _chip_sum(psum, slots, axis, place, name):
    _, hr, hc = slots.shape
    tr = _tile(hr, 256)
    own_map = (lambda i, pref: (0, i, pref[0])) if axis == 1 else (lambda i, pref: (pref[0], i, 0))

    def body(pref, own_ref, s_ref, o_ref):
        o_ref[...] = ((own_ref[...].astype(F32) + s_ref[0].astype(F32)) + s_ref[1].astype(F32)) + s_ref[2].astype(F32)

    return pl.pallas_call(
        body, name=name,
        grid_spec=pltpu.PrefetchScalarGridSpec(
            num_scalar_prefetch=1, grid=(hr // tr,),
            in_specs=[pl.BlockSpec((None, tr, hc), own_map),
                      pl.BlockSpec((N_CHIPS - 1, tr, hc), lambda i, pref: (0, i, 0))],
            out_specs=pl.BlockSpec((None, tr, hc), lambda i, pref: (pref[1], i, 0))),
        out_shape=jax.ShapeDtypeStruct((2, hr, hc), F32),
        compiler_params=_params(("parallel",)),
    )(place, psum, slots)


def _half_swap(both):
    n = len(both)

    def body(*refs):
        outs = refs[n:2 * n]
        send, recv = refs[2 * n:]
        x, y, c, _ = _place()

        def copy(w, half):
            return pltpu.make_async_remote_copy(
                src_ref=outs[w].at[half], dst_ref=outs[w].at[half], send_sem=send.at[w], recv_sem=recv.at[w],
                device_id=(x, y, 1 - c), device_id_type=MESH)

        for w in range(n):
            copy(w, c).start()
        for w in range(n):
            copy(w, 1 - c).wait()

    return pl.pallas_call(
        body, name="grad_half_swap",
        in_specs=[HBM_SPEC] * n, out_specs=[HBM_SPEC] * n,
        out_shape=[jax.ShapeDtypeStruct(b.shape, b.dtype) for b in both],
        input_output_aliases={i: i for i in range(n)},
        scratch_shapes=[pltpu.SemaphoreType.DMA((n,)), pltpu.SemaphoreType.DMA((n,))],
        compiler_params=pltpu.CompilerParams(has_side_effects=True),
    )(*both)


def _allreduce_small(pack):
    rows, d = pack.shape

    def body(p_ref, o_ref, slots, send, recv):
        x, y, c, _ = _place()
        me = 4 * x + 2 * y + c
        slots[me] = p_ref[...]
        cps = []
        for k in range(1, N_DEV):
            px, py, pc = x ^ (k >> 2), y ^ ((k >> 1) & 1), c ^ (k & 1)
            cp = pltpu.make_async_remote_copy(
                src_ref=p_ref, dst_ref=slots.at[me], send_sem=send.at[k - 1], recv_sem=recv.at[k - 1],
                device_id=(px, py, pc), device_id_type=MESH)
            cp.start()
            cps.append(cp)
        for k in range(1, N_DEV):
            px, py, pc = x ^ (k >> 2), y ^ ((k >> 1) & 1), c ^ (k & 1)
            arrival = pltpu.make_async_remote_copy(
                src_ref=p_ref, dst_ref=slots.at[4 * px + 2 * py + pc], send_sem=send.at[k - 1],
                recv_sem=recv.at[k - 1], device_id=(px, py, pc), device_id_type=MESH)
            arrival.wait_recv()
            arrival.wait_send()
        acc = slots[0]
        for k in range(1, N_DEV):
            acc = acc + slots[k]
        o_ref[...] = acc

    vm = pl.BlockSpec(memory_space=pltpu.VMEM)
    return pl.pallas_call(
        body, name="allreduce_small", in_specs=[vm], out_specs=vm,
        out_shape=jax.ShapeDtypeStruct((rows, d), F32),
        scratch_shapes=[pltpu.VMEM((N_DEV, rows, d), F32), pltpu.SemaphoreType.DMA((N_DEV - 1,)),
                        pltpu.SemaphoreType.DMA((N_DEV - 1,))],
        compiler_params=pltpu.CompilerParams(has_side_effects=True),
    )(pack)


def _adamw(w, g, m, v, name):
    rows, cols = w.shape

    def fn(wv, gv, mv, vv):
        m2 = ADAM_B1 * mv + (1.0 - ADAM_B1) * gv
        v2 = ADAM_B2 * vv + (1.0 - ADAM_B2) * (gv * gv)
        m_hat = m2 / (1.0 - ADAM_B1 ** ADAM_STEP)
        v_hat = v2 / (1.0 - ADAM_B2 ** ADAM_STEP)
        delta = -ADAM_LR * (m_hat / (jnp.sqrt(v_hat) + ADAM_EPS) + ADAM_WD * wv)
        return delta, m2, v2

    ins = [(a, cols, 0) for a in (w, g, m, v)]
    return _rowwise(fn, ins, [(cols, F32)] * 3, rows=rows, tm=_tile(rows, max(8, 262144 // cols)), name=name)


BIG = ["w_in", "w_branch_a", "w_branch_b", "w_mix_out", "w_mem_q", "w_mem_kv", "w_mem_o", "w_ffn_in", "w_ffn_out"]
BIG_AXIS = {"w_in": 1, "w_branch_a": 1, "w_branch_b": 1, "w_mix_out": 0, "w_mem_q": 0, "w_mem_kv": 1,
            "w_mem_o": 0, "w_ffn_in": 1, "w_ffn_out": 0}
NORMS = ["norm_mix", "norm_mem_q", "norm_mem_kv", "norm_ffn", "norm_final"]
ORDER = ["norm_mix", "w_in", "conv_w", "w_branch_a", "w_branch_b", "w_mix_out", "norm_mem_q", "norm_mem_kv",
         "w_mem_q", "w_mem_kv", "w_mem_o", "norm_ffn", "w_ffn_in", "w_ffn_out", "norm_final"]


def _pack_small(vals, conv):
    d = vals[0].shape[-1]
    rows = [v.reshape(1, d) for v in vals]
    conv = jnp.pad(conv, ((0, 0), (0, d - conv.shape[1])))
    pad = jnp.zeros((SMALL_ROWS - len(rows) - CONV_K, d), F32)
    return jnp.concatenate(rows + [conv, pad], axis=0)


def kernel(x, mem, norm_mix, w_in, conv_w, w_branch_a, w_branch_b, w_mix_out, norm_mem_q, norm_mem_kv, w_mem_q, w_mem_kv, w_mem_o, norm_ffn, w_ffn_in, w_ffn_out, norm_final, loss_target, m_norm_mix, m_w_in, m_conv_w, m_w_branch_a, m_w_branch_b, m_w_mix_out, m_norm_mem_q, m_norm_mem_kv, m_w_mem_q, m_w_mem_kv, m_w_mem_o, m_norm_ffn, m_w_ffn_in, m_w_ffn_out, m_norm_final, v_norm_mix, v_w_in, v_conv_w, v_w_branch_a, v_w_branch_b, v_w_mix_out, v_norm_mem_q, v_norm_mem_kv, v_w_mem_q, v_w_mem_kv, v_w_mem_o, v_norm_ffn, v_w_ffn_in, v_w_ffn_out, v_norm_final):
    args = dict(locals())
    wts = {n: args[n] for n in ORDER}
    mom = {n: args["m_" + n] for n in ORDER}
    var = {n: args["v_" + n] for n in ORDER}
    x = x[0]
    mem = mem[0]
    target = loss_target[0]
    s, d = x.shape
    gains = {n: wts[n].reshape(1, d) for n in NORMS}
    chip = 2 * lax.axis_index("x") + lax.axis_index("y")
    core = lax.axis_index("c").astype(jnp.int32).reshape(1)
    place = jnp.stack([chip, lax.axis_index("c")]).astype(jnp.int32)

    conv_shard = jnp.pad(conv_w[0], ((0, CONV_ROWS - CONV_K), (0, 0)))
    first = [_cast_place(wts["w_in"][0], BIG_AXIS["w_in"], place, BF, "place_w_in"),
             _cast_place(conv_shard, 1, place, F32, "place_conv_w")]
    w_in_full, conv_full = _gather_weights(first, [BIG_AXIS["w_in"], 1], [True, False])
    W = {"w_in": w_in_full}
    ride_in_proj = ["w_branch_a", "w_branch_b", "w_mix_out", "w_mem_q", "w_mem_o"]
    ride_attention = ["w_ffn_in", "w_mem_kv"]
    ride_ffn_in = ["w_ffn_out"]
    placed = {n: _cast_place(wts[n][0], BIG_AXIS[n], place, BF, "place_" + n)
              for n in ride_in_proj + ride_attention + ride_ffn_in}

    def fetch(names):
        return _riding_fetch([placed[n] for n in names], [BIG_AXIS[n] for n in names])

    def hand_on(names, bufs, tag):
        W.update(zip(names, _hand_on(bufs, [BIG_AXIS[n] for n in names], "gather_hand_on" + tag)))

    h1 = _rmsnorm(x, gains["norm_mix"], "norm_mix_fwd")
    proj, *bufs_a = _matmul(h1, W["w_in"], tn=1280, name="in_proj", rider=fetch(ride_in_proj))
    o_a, o_a32, *bufs_b = _sb_fwd(proj, fetch(ride_attention))
    y_b = _conv_fwd(proj, conv_full)
    hand_on(ride_in_proj + ride_attention, bufs_a + bufs_b, "")
    br_a, br_b, merged = _branches_merge(o_a, y_b, W["w_branch_a"], W["w_branch_b"], proj)
    x1 = _matmul(merged, W["w_mix_out"], tn=1024, out_dtype=F32, resid=x, name="mix_out")

    hq, q_m = _norm_matmul(x1, gains["norm_mem_q"], W["w_mem_q"], tn=1024, name="norm_mem_q")
    mn, kv = _norm_matmul(mem, gains["norm_mem_kv"], W["w_mem_kv"], tn=1024, name="norm_mem_kv")
    o_m = _mem_fwd(q_m, kv)
    x2 = _matmul(o_m, W["w_mem_o"], tn=1024, out_dtype=F32, resid=x1, name="mem_o")

    hf, gate, up, act, *bufs_c = _norm_ffn_in_swiglu(x2, gains["norm_ffn"], W["w_ffn_in"], rider=fetch(ride_ffn_in))
    hand_on(ride_ffn_in, bufs_c, "_late")

    dx3, dx3_b, dg_final, loss_part = _ffn_out_loss(act, W["w_ffn_out"], x2, gains["norm_final"], target)

    def halves_of(names):
        views = []
        for n in names:
            r, cdim = gw[n].shape
            views.append(gw[n].reshape(1, 2, r // 2, cdim) if BIG_AXIS[n] == 1
                         else gw[n].reshape(N_CHIPS, 2, r // (2 * N_CHIPS), cdim))
        return views

    def pair_adds(names, views, got):
        return [_pair_add(v, g, core, "pair_add_" + n) for n, v, g in zip(names, views, got)]

    def chip_sums(names, sums, slots):
        return [_chip_sum(sm, sl, BIG_AXIS[n], place, "chip_sum_" + n) for n, sm, sl in zip(names, sums, slots)]

    gw = {"w_ffn_out": _matmul(act, dx3_b, ta=True, tm=1408, tn=512, name="gw_ffn_out")}
    first_views = halves_of(["w_ffn_out"])
    dgu, *first_got = _d_act_swiglu(dx3_b, W["w_ffn_out"], gate, up, rider=_riding_pairs(first_views))
    gw["w_ffn_in"] = _matmul(hf, dgu, ta=True, tn=512, name="gw_ffn_in")
    first_sums = pair_adds(["w_ffn_out"], first_views, first_got)
    dx2, dx2_b, dg_ffn, *first_slots = _matmul_norm_bwd(
        dgu, W["w_ffn_in"], x2, gains["norm_ffn"], dx3, tm=256, bf_copy=True, name="d_hf_norm_bwd",
        rider=_riding_slots(first_sums, [BIG_AXIS["w_ffn_out"]]))

    do_m = _matmul(dx2_b, W["w_mem_o"], tb=True, tn=1024, name="d_o_m")
    gw["w_mem_o"] = _matmul(o_m, dx2_b, ta=True, tn=512, name="gw_mem_o")
    dq_m, dk_m, dv_m = _mem_bwd(q_m, kv, do_m)
    dkv = jnp.concatenate([dk_m, dv_m], axis=-1)
    dx1, dx1_b, dg_q = _matmul_norm_bwd(dq_m, W["w_mem_q"], x1, gains["norm_mem_q"], dx2, tm=512, bf_copy=True,
                                        name="d_hq_norm_bwd")
    gw["w_mem_q"] = _matmul(hq, dq_m, ta=True, tn=512, name="gw_mem_q")
    dmn = _matmul(dkv, W["w_mem_kv"], tb=True, tn=1024, out_dtype=F32, name="d_mn")
    gw["w_mem_kv"] = _matmul(mn, dkv, ta=True, tn=1024, name="gw_mem_kv")
    _, dg_kv = _rmsnorm_bwd(mem, gains["norm_mem_kv"], dmn, None, "norm_mem_kv_bwd")

    dbr_a, dbr_b, dga, dgb = _d_merged_gates(dx1_b, W["w_mix_out"], proj, br_a, br_b)
    gw["w_mix_out"] = _matmul(merged, dx1_b, ta=True, tn=512, name="gw_mix_out")
    do_a = _matmul(dbr_a, W["w_branch_a"], tb=True, name="d_o_a")
    gw["w_branch_a"] = _matmul(o_a, dbr_a, ta=True, tn=512, name="gw_branch_a")
    dy_b = _matmul(dbr_b, W["w_branch_b"], tb=True, name="d_y_b")
    gw["w_branch_b"] = _matmul(y_b, dbr_b, ta=True, tn=512, name="gw_branch_b")
    early = [n for n in BIG if n not in ("w_in", "w_ffn_out")]
    early_views = halves_of(early)
    du, dgate_b, dgate_c, dconv, *early_got = _conv_bwd(proj, conv_full, dy_b, rider=_riding_pairs(early_views))
    early_sums = pair_adds(early, early_views, early_got)
    dq, dk, dv, *early_slots = _sb_bwd(proj, o_a32, do_a, _riding_slots(early_sums, [BIG_AXIS[n] for n in early]))

    def assemble(*parts):
        return jnp.concatenate([p.astype(BF) for p in parts], axis=-1)

    hw = dq.shape[1]
    dproj = _rowwise(assemble, [(t, hw, 0) for t in (dq, dk, dv, du, dgate_b, dgate_c)] + [(dga, d, 0), (dgb, d, 0)],
                     [(proj.shape[1], BF)], rows=s, tm=256, name="assemble_dproj")[0]
    gw["w_in"] = _matmul(h1, dproj, ta=True, tn=640, name="gw_in")
    in_views = halves_of(["w_in"])
    in_sums = pair_adds(["w_in"], in_views, _pair_exchange(in_views, "grad_pair_exchange_in"))
    grad_x, dg_mix, *in_slots = _matmul_norm_bwd(dproj, W["w_in"], x, gains["norm_mix"], dx1, tm=256, bf_copy=False,
                                                 name="d_h1_norm_bwd",
                                                 rider=_riding_slots(in_sums, [BIG_AXIS["w_in"]]))

    halves = dict(zip(early, chip_sums(early, early_sums, early_slots)))
    halves.update(zip(["w_ffn_out"], chip_sums(["w_ffn_out"], first_sums, first_slots)))
    halves.update(zip(["w_in"], chip_sums(["w_in"], in_sums, in_slots)))
    both = _half_swap([halves[n] for n in BIG])
    grads = {n: b.reshape(wts[n].shape[1:]) for n, b in zip(BIG, both)}

    small_g = [dg_mix, dg_q, dg_kv, dg_ffn, dg_final]
    pack = _pack_small(small_g, dconv[:CONV_K])
    pack = pack.at[ROW_LOSS].set(jnp.broadcast_to(loss_part[0, :1], (d,)))
    red = _allreduce_small(pack)
    loss = red[ROW_LOSS, 0]
    cw = conv_w.shape[2]
    conv_g = lax.dynamic_slice(red, (ROW_CONV, chip * cw), (CONV_K, cw))
    small_grad = _pack_small([red[i] for i in range(len(NORMS))], conv_g)
    small = [_pack_small([t[n] for n in NORMS], t["conv_w"][0]) for t in (wts, mom, var)]
    s_delta, s_m, s_v = _adamw(small[0], small_grad, small[1], small[2], "adamw_small")

    out = {"grad": {}, "delta": {}, "new_m": {}, "new_v": {}}
    for n in BIG:
        shp = wts[n].shape
        dl, m2, v2 = _adamw(wts[n][0], grads[n], mom[n][0], var[n][0], "adamw_" + n)
        out["grad"][n] = grads[n].reshape(shp)
        out["delta"][n], out["new_m"][n], out["new_v"][n] = dl.reshape(shp), m2.reshape(shp), v2.reshape(shp)
    for key, blk in (("grad", small_grad), ("delta", s_delta), ("new_m", s_m), ("new_v", s_v)):
        for i, n in enumerate(NORMS):
            out[key][n] = blk[i].reshape(wts[n].shape)
        out[key]["conv_w"] = blk[ROW_CONV:ROW_CONV + CONV_K, :cw].reshape(conv_w.shape)

    return (loss, grad_x[None], *[out["grad"][n] for n in ORDER], *[out["delta"][n] for n in ORDER],
            *[out["new_m"][n] for n in ORDER], *[out["new_v"][n] for n in ORDER])
```

```python
import math

import jax
import jax.numpy as jnp
from jax import lax
from jax.experimental import pallas as pl
from jax.experimental.pallas import tpu as pltpu

BF = jnp.bfloat16
F32 = jnp.float32
MESH = pl.DeviceIdType.MESH

SB_HEAD_DIM = 64
LANES = 128
MEM_HEADS = 4
CONV_K = 3
CONV_ROWS = 8
EPS = 1e-6
N_CHIPS = 4
N_DEV = 8
VMEM_LIMIT = 56 * 1024 * 1024
STREAM_BLOCK_ELEMS = 1 << 20

ADAM_LR = 0.001
ADAM_B1 = 0.9
ADAM_B2 = 0.999
ADAM_EPS = 1e-08
ADAM_WD = 0.01
ADAM_STEP = 10

SMALL_ROWS = 16
ROW_CONV = 5
ROW_LOSS = 8


def _params(sem=None, **kw):
    return pltpu.CompilerParams(dimension_semantics=sem, vmem_limit_bytes=VMEM_LIMIT, **kw)


def _tile(dim, pref):
    if dim <= pref:
        return dim
    for step in (LANES, 8):
        t = (pref // step) * step
        while t >= step:
            if dim % t == 0:
                return t
            t -= step
    raise ValueError(f"no tile of {dim} under {pref}")


def _matmul(a, b, *, ta=False, tb=False, tm=1024, tn=512, out_dtype=BF, resid=None, rider=None, name):
    if ta:
        kdim, m = a.shape
    else:
        m, kdim = a.shape
    n = b.shape[0] if tb else b.shape[1]
    tm, tn = _tile(m, tm), _tile(n, tn)
    a_spec = pl.BlockSpec((kdim, tm), lambda i, j: (0, i)) if ta else pl.BlockSpec((tm, kdim), lambda i, j: (i, 0))
    b_spec = pl.BlockSpec((tn, kdim), lambda i, j: (j, 0)) if tb else pl.BlockSpec((kdim, tn), lambda i, j: (0, j))
    o_spec = pl.BlockSpec((tm, tn), lambda i, j: (i, j))
    dims = (((0 if ta else 1,), (1 if tb else 0,)), ((), ()))
    has_res = resid is not None
    n_in = 2 + has_res

    def body(*refs):
        ins, (o_ref,) = _host_refs(refs, rider, n_in, 1)
        av, bv = ins[0][...], ins[1][...]
        if av.dtype != BF:
            av = av.astype(BF)
        if bv.dtype != BF:
            bv = bv.astype(BF)
        acc = lax.dot_general(av, bv, dims, preferred_element_type=F32)
        if has_res:
            acc = ins[2][...] + acc
        o_ref[...] = acc.astype(o_ref.dtype)
        if rider:
            rider.run(refs, n_in, 1, (pl.program_id(0) == 0) & (pl.program_id(1) == 0),
                      (pl.program_id(0) == m // tm - 1) & (pl.program_id(1) == n // tn - 1))

    res = pl.pallas_call(
        body, name=name, grid=(m // tm, n // tn),
        compiler_params=_params(("arbitrary", "arbitrary") if rider else ("parallel", "parallel"),
                                has_side_effects=rider is not None),
        **_with_rider(rider, n_in, 1, [a_spec, b_spec] + ([o_spec] if has_res else []), [o_spec],
                      [jax.ShapeDtypeStruct((m, n), out_dtype)]),
    )(*([a, b] + ([resid] if has_res else []) + (rider.arrays if rider else [])))
    return res if rider else res[0]


def _rowwise(fn, ins, outs, *, rows, tm, name, accs=()):
    tm = _tile(rows, tm)
    in_specs, args = [], []
    for arr, cols, cb in ins:
        if cols is None:
            in_specs.append(pl.BlockSpec(arr.shape, lambda i, nd=arr.ndim: (0,) * nd))
        else:
            in_specs.append(pl.BlockSpec((tm, cols), lambda i, cb=cb: (i, cb)))
        args.append(arr)
    out_specs = [pl.BlockSpec((tm, cols), lambda i: (i, 0)) for cols, _ in outs]
    out_shape = [jax.ShapeDtypeStruct((rows, cols), dt) for cols, dt in outs]
    for r, c in accs:
        out_specs.append(pl.BlockSpec((r, c), lambda i: (0, 0)))
        out_shape.append(jax.ShapeDtypeStruct((r, c), F32))
    n_in, n_out = len(ins), len(outs)

    def body(*refs):
        res = fn(*[r[...] for r in refs[:n_in]])
        if not isinstance(res, (tuple, list)):
            res = (res,)
        refs = refs[n_in:]
        for o_ref, val in zip(refs[:n_out], res[:n_out]):
            o_ref[...] = val.astype(o_ref.dtype)
        first = pl.program_id(0) == 0
        for a_ref, val in zip(refs[n_out:], res[n_out:]):
            @pl.when(first)
            def _(a_ref=a_ref, val=val):
                a_ref[...] = val

            @pl.when(jnp.logical_not(first))
            def _(a_ref=a_ref, val=val):
                a_ref[...] += val

    res = pl.pallas_call(
        body, name=name, grid=(rows // tm,), in_specs=in_specs, out_specs=out_specs, out_shape=out_shape,
        compiler_params=_params(("arbitrary",) if accs else ("parallel",)),
    )(*args)
    return res


def _rstd(xf):
    return lax.rsqrt(jnp.mean(xf * xf, axis=-1, keepdims=True) + EPS)


def _rmsnorm(x, g, name):
    rows, d = x.shape
    return _rowwise(lambda xv, gv: xv * _rstd(xv) * gv, [(x, d, 0), (g, None, None)], [(d, BF)],
                    rows=rows, tm=512, name=name)[0]


def _norm_gain_grad(x, dy, name):
    rows, d = x.shape

    def fn(xv, dyv):
        return (jnp.sum(dyv.astype(F32) * (xv * _rstd(xv)), axis=0, keepdims=True),)

    return _rowwise(fn, [(x, d, 0), (dy, d, 0)], [], rows=rows, tm=512, name=name, accs=[(1, d)])[0]


def _matmul_norm_bwd(a, w, x, g, resid, *, tm, bf_copy, name, rider=None):
    m, kdim = a.shape
    d = w.shape[0]
    tm = _tile(m, tm)
    row = lambda i: (i, 0)
    whole = lambda i: (0, 0)
    n_out = 2 + bf_copy

    def body(*refs):
        (a_ref, w_ref, x_ref, g_ref, r_ref), outs = _host_refs(refs, rider, 5, n_out)
        dy = lax.dot_general(a_ref[...], w_ref[...], NT, preferred_element_type=F32)
        xv = x_ref[...]
        r = _rstd(xv)
        xn = xv * r
        dxn = dy * g_ref[...]
        dx = r_ref[...] + r * (dxn - xn * jnp.mean(dxn * xn, axis=-1, keepdims=True))
        outs[0][...] = dx
        if bf_copy:
            outs[1][...] = dx.astype(BF)
        dg = jnp.sum(dy * xn, axis=0, keepdims=True)
        first = pl.program_id(0) == 0

        @pl.when(first)
        def _():
            outs[-1][...] = dg

        @pl.when(jnp.logical_not(first))
        def _():
            outs[-1][...] += dg

        if rider:
            rider.run(refs, 5, n_out, first, pl.program_id(0) == m // tm - 1)

    tok = pl.BlockSpec((tm, d), row)
    out_specs = [tok] + ([tok] if bf_copy else []) + [pl.BlockSpec((1, d), whole)]
    out_shape = ([jax.ShapeDtypeStruct((m, d), F32)] + ([jax.ShapeDtypeStruct((m, d), BF)] if bf_copy else [])
                 + [jax.ShapeDtypeStruct((1, d), F32)])
    return pl.pallas_call(
        body, name=name, grid=(m // tm,),
        compiler_params=_params(("arbitrary",), has_side_effects=rider is not None),
        **_with_rider(
            rider, 5, n_out,
            [pl.BlockSpec((tm, kdim), row), pl.BlockSpec((d, kdim), whole), tok, pl.BlockSpec((1, d), whole), tok],
            out_specs, out_shape),
    )(a, w, x, g, resid, *(rider.arrays if rider else []))


def _norm_ffn_in_swiglu(x, g, w, *, tm=1024, tn=1408, rider=None):
    m, kdim = x.shape
    f = w.shape[1] // 2
    tm, tn = _tile(m, tm), _tile(f, tn)
    nj = f // tn

    def body(*refs):
        (x_ref, g_ref, wg_ref, wu_ref), (h_ref, gate_ref, up_ref, act_ref) = _host_refs(refs, rider, 4, 4)
        if rider:
            rider.run(refs, 4, 4, (pl.program_id(0) == 0) & (pl.program_id(1) == 0),
                      (pl.program_id(0) == m // tm - 1) & (pl.program_id(1) == nj - 1))

        @pl.when(pl.program_id(1) == 0)
        def _():
            xv = x_ref[...]
            h_ref[...] = (xv * _rstd(xv) * g_ref[...]).astype(h_ref.dtype)

        hv = h_ref[...]
        gate = jnp.dot(hv, wg_ref[...], preferred_element_type=F32)
        up = jnp.dot(hv, wu_ref[...], preferred_element_type=F32)
        gate_ref[...] = gate.astype(gate_ref.dtype)
        up_ref[...] = up.astype(up_ref.dtype)
        act_ref[...] = (gate * jax.nn.sigmoid(gate) * up).astype(act_ref.dtype)

    tile = pl.BlockSpec((tm, tn), lambda i, j: (i, j))
    rows = pl.BlockSpec((tm, kdim), lambda i, j: (i, 0))
    return pl.pallas_call(
        body, name="norm_ffn_in_swiglu", grid=(m // tm, nj),
        compiler_params=_params(("arbitrary", "arbitrary") if rider else ("parallel", "arbitrary"),
                                has_side_effects=rider is not None),
        **_with_rider(
            rider, 4, 4,
            [rows, pl.BlockSpec((1, kdim), lambda i, j: (0, 0)), pl.BlockSpec((kdim, tn), lambda i, j: (0, j)),
             pl.BlockSpec((kdim, tn), lambda i, j: (0, nj + j))],
            [rows, tile, tile, tile],
            [jax.ShapeDtypeStruct((m, kdim), BF)] + [jax.ShapeDtypeStruct((m, f), BF)] * 3),
    )(x, g, w, w, *(rider.arrays if rider else []))


def _norm_matmul(x, g, w, *, tm=1024, tn, name):
    m, kdim = x.shape
    n = w.shape[1]
    tm, tn = _tile(m, tm), _tile(n, tn)

    def body(x_ref, g_ref, w_ref, h_ref, o_ref):
        @pl.when(pl.program_id(1) == 0)
        def _():
            xv = x_ref[...]
            h_ref[...] = (xv * _rstd(xv) * g_ref[...]).astype(h_ref.dtype)

        o_ref[...] = jnp.dot(h_ref[...], w_ref[...], preferred_element_type=F32).astype(o_ref.dtype)

    rows = pl.BlockSpec((tm, kdim), lambda i, j: (i, 0))
    return pl.pallas_call(
        body, name=name, grid=(m // tm, n // tn),
        in_specs=[rows, pl.BlockSpec((1, kdim), lambda i, j: (0, 0)), pl.BlockSpec((kdim, tn), lambda i, j: (0, j))],
        out_specs=[rows, pl.BlockSpec((tm, tn), lambda i, j: (i, j))],
        out_shape=[jax.ShapeDtypeStruct((m, kdim), BF), jax.ShapeDtypeStruct((m, n), BF)],
        compiler_params=_params(("parallel", "arbitrary")),
    )(x, g, w)


def _ffn_out_loss(act, w, resid, g, target, *, tm=512):
    m, f = act.shape
    d = w.shape[1]
    tm = _tile(m, tm)
    row = lambda i: (i, 0)
    whole = lambda i: (0, 0)

    def body(a_ref, w_ref, r_ref, g_ref, t_ref, dx_ref, dxb_ref, dg_ref, loss_ref):
        xv = r_ref[...] + jnp.dot(a_ref[...], w_ref[...], preferred_element_type=F32)
        gv = g_ref[...]
        r = _rstd(xv)
        xn = xv * r
        err = xn * gv - t_ref[...]
        loss = 0.5 * jnp.sum(jnp.mean(err * err, axis=-1, keepdims=True), axis=0, keepdims=True)
        dyv = err * (1.0 / d)
        dxn = dyv * gv
        dx = r * (dxn - xn * jnp.mean(dxn * xn, axis=-1, keepdims=True))
        dx_ref[...] = dx
        dxb_ref[...] = dx.astype(dxb_ref.dtype)
        dg = jnp.sum(dyv * xn, axis=0, keepdims=True)
        loss_b = jnp.broadcast_to(loss, (1, LANES))
        first = pl.program_id(0) == 0

        @pl.when(first)
        def _():
            dg_ref[...] = dg
            loss_ref[...] = loss_b

        @pl.when(jnp.logical_not(first))
        def _():
            dg_ref[...] += dg
            loss_ref[...] += loss_b

    tok = pl.BlockSpec((tm, d), row)
    return pl.pallas_call(
        body, name="ffn_out_loss", grid=(m // tm,),
        in_specs=[pl.BlockSpec((tm, f), row), pl.BlockSpec((f, d), whole), tok, pl.BlockSpec((1, d), whole), tok],
        out_specs=[tok, tok, pl.BlockSpec((1, d), whole), pl.BlockSpec((1, LANES), whole)],
        out_shape=[jax.ShapeDtypeStruct((m, d), F32), jax.ShapeDtypeStruct((m, d), BF),
                   jax.ShapeDtypeStruct((1, d), F32), jax.ShapeDtypeStruct((1, LANES), F32)],
        compiler_params=_params(("arbitrary",)),
    )(act, w, resid, g, target)


def _d_act_swiglu(dx, w, gate, up, *, tm=256, rider=None):
    m, d = dx.shape
    f = w.shape[0]
    tm = _tile(m, tm)
    row = lambda i: (i, 0)

    def body(*refs):
        (dx_ref, w_ref, gate_ref, up_ref), (o_ref,) = _host_refs(refs, rider, 4, 1)
        if rider:
            rider.run(refs, 4, 1, pl.program_id(0) == 0, pl.program_id(0) == m // tm - 1)
        da = lax.dot_general(dx_ref[...], w_ref[...], NT, preferred_element_type=F32)
        gv, uv = gate_ref[...].astype(F32), up_ref[...].astype(F32)
        sg = jax.nn.sigmoid(gv)
        dgate = da * uv * (sg * (1.0 + gv * (1.0 - sg)))
        o_ref[...] = jnp.concatenate([dgate, da * (gv * sg)], axis=-1).astype(o_ref.dtype)

    return pl.pallas_call(
        body, name="d_act_swiglu", grid=(m // tm,),
        compiler_params=_params(("arbitrary",) if rider else ("parallel",), has_side_effects=rider is not None),
        **_with_rider(
            rider, 4, 1,
            [pl.BlockSpec((tm, d), row), pl.BlockSpec((f, d), lambda i: (0, 0)),
             pl.BlockSpec((tm, f), row), pl.BlockSpec((tm, f), row)],
            [pl.BlockSpec((tm, 2 * f), row)], [jax.ShapeDtypeStruct((m, 2 * f), BF)]),
    )(dx, w, gate, up, *(rider.arrays if rider else []))


GATE_A_BLK, GATE_B_BLK = 3, 4


def _branches_merge(o_a, y_b, w_a, w_b, proj, *, tm=1024):
    m, kdim = o_a.shape
    d = w_a.shape[1]
    tm = _tile(m, tm)
    row = lambda i: (i, 0)

    def body(a_ref, b_ref, wa_ref, wb_ref, ga_ref, gb_ref, bra_ref, brb_ref, merged_ref):
        bra = jnp.dot(a_ref[...], wa_ref[...], preferred_element_type=F32)
        brb = jnp.dot(b_ref[...], wb_ref[...], preferred_element_type=F32)
        bra_ref[...] = bra.astype(bra_ref.dtype)
        brb_ref[...] = brb.astype(brb_ref.dtype)
        merged = jax.nn.sigmoid(ga_ref[...].astype(F32)) * bra + jax.nn.sigmoid(gb_ref[...].astype(F32)) * brb
        merged_ref[...] = merged.astype(merged_ref.dtype)

    tok = pl.BlockSpec((tm, d), row)
    return pl.pallas_call(
        body, name="branches_merge", grid=(m // tm,),
        in_specs=[pl.BlockSpec((tm, kdim), row), pl.BlockSpec((tm, kdim), row),
                  pl.BlockSpec((kdim, d), lambda i: (0, 0)), pl.BlockSpec((kdim, d), lambda i: (0, 0)),
                  pl.BlockSpec((tm, d), lambda i: (i, GATE_A_BLK)), pl.BlockSpec((tm, d), lambda i: (i, GATE_B_BLK))],
        out_specs=[tok, tok, tok], out_shape=[jax.ShapeDtypeStruct((m, d), BF)] * 3,
        compiler_params=_params(("parallel",)),
    )(o_a, y_b, w_a, w_b, proj, proj)


def _d_merged_gates(dx, w_mix, proj, br_a, br_b, *, tm=512):
    m, d = dx.shape
    tm = _tile(m, tm)
    row = lambda i: (i, 0)

    def body(dx_ref, w_ref, ga_ref, gb_ref, bra_ref, brb_ref, dbra_ref, dbrb_ref, dga_ref, dgb_ref):
        dm = lax.dot_general(dx_ref[...], w_ref[...], NT, preferred_element_type=F32)
        sa, sb = jax.nn.sigmoid(ga_ref[...].astype(F32)), jax.nn.sigmoid(gb_ref[...].astype(F32))
        dbra_ref[...] = (dm * sa).astype(dbra_ref.dtype)
        dbrb_ref[...] = (dm * sb).astype(dbrb_ref.dtype)
        dga_ref[...] = (dm * bra_ref[...].astype(F32) * (sa * (1.0 - sa))).astype(dga_ref.dtype)
        dgb_ref[...] = (dm * brb_ref[...].astype(F32) * (sb * (1.0 - sb))).astype(dgb_ref.dtype)

    tok = pl.BlockSpec((tm, d), row)
    return pl.pallas_call(
        body, name="d_merged_gates", grid=(m // tm,),
        in_specs=[tok, pl.BlockSpec((d, d), lambda i: (0, 0)),
                  pl.BlockSpec((tm, d), lambda i: (i, GATE_A_BLK)), pl.BlockSpec((tm, d), lambda i: (i, GATE_B_BLK)),
                  tok, tok],
        out_specs=[tok] * 4, out_shape=[jax.ShapeDtypeStruct((m, d), BF)] * 4,
        compiler_params=_params(("parallel",)),
    )(dx, w_mix, proj, proj, br_a, br_b)


SB_TK = 128
SB_KT = 2


def _sb_consts(tq):
    tk = SB_TK
    diff = lax.broadcasted_iota(jnp.int32, (tq, tk), 1) - lax.broadcasted_iota(jnp.int32, (tq, tk), 0)
    rj = lax.broadcasted_iota(jnp.int32, (2 * tk, 2 * tk), 0) & (tk - 1)
    cj = lax.broadcasted_iota(jnp.int32, (2 * tk, 2 * tk), 1)
    ones_half = cj >= tk
    later = jnp.where((rj > cj) | ones_half, 1.0, 0.0).astype(BF)
    later_incl = jnp.where((rj >= cj) | ones_half, 1.0, 0.0).astype(BF)
    return diff, later, later_incl


def _split_dot(val, rhs_twice):
    hi = val.astype(BF)
    lo = (val - hi.astype(F32)).astype(BF)
    return jnp.dot(jnp.concatenate([hi, lo], axis=1), rhs_twice, preferred_element_type=F32)


def _log_terms(z):
    sp = jnp.maximum(z, 0.0) + jnp.log(1.0 + jnp.exp(-jnp.abs(z)))
    return z - sp, sp


NT = (((1,), (1,)), ((), ()))
TN = (((0,), (0,)), ((), ()))


def _head_lane_masks(rows):
    lane = lax.broadcasted_iota(jnp.int32, (rows, LANES), 1)
    first = jnp.where(lane < SB_HEAD_DIM, 1.0, 0.0)
    return first.astype(BF), (1.0 - first).astype(BF)


DEAD_LOG = 104.0


def _walk_back(i, step, state, carries_of):
    def alive(st):
        c0, c1 = carries_of(st)
        return jnp.min(jnp.minimum(c0, c1)) < DEAD_LOG

    def cond(loop):
        done, live, _ = loop
        return jnp.logical_and(done < i, live)

    def body(loop):
        done, _, st = loop
        st = step(i - 1 - done, st)
        return done + 1, alive(st), st

    return lax.while_loop(cond, body, (jnp.int32(0), alive(state), state))[2]


def _tail(a, r0):
    return a if r0 == 0 else a[r0:]


def _add_tail(a, r0, delta):
    return a + delta if r0 == 0 else jnp.concatenate([a[:r0], a[r0:] + delta], axis=0)


def _both_heads(tile, masks):
    return jnp.concatenate([tile * masks[0], tile * masks[1]], axis=0)


def _with_rider(rider, n_in, n_out, in_specs, out_specs, out_shape, scratch=()):
    kw = dict(in_specs=list(in_specs), out_specs=list(out_specs), out_shape=list(out_shape),
              scratch_shapes=list(scratch), input_output_aliases={})
    if rider:
        extra = rider.call_args(n_in, n_out)
        kw["in_specs"] += extra["in_specs"]
        kw["out_specs"] += extra["out_specs"]
        kw["out_shape"] += extra["out_shape"]
        kw["scratch_shapes"] += extra["scratch"]
        kw["input_output_aliases"] = extra["aliases"]
    return kw


def _sb_fwd(proj, rider=None):
    s = proj.shape[0]
    tk, tq = SB_TK, SB_KT * SB_TK
    n_pairs = 4
    scale = 1.0 / math.sqrt(SB_HEAD_DIM)

    def body(*refs):
        (q_ref, k_ref, v_ref), (o_ref, o32_ref) = _host_refs(refs, rider, 3, 2)
        i = pl.program_id(1)
        diff, later, _ = _sb_consts(tq)
        qs = (q_ref[...].astype(F32) * scale).astype(BF)
        lane_masks = _head_lane_masks(tk)

        def step(g, state, masked):
            tiles = list(reversed(range(SB_KT)))
            chains = [(t, h) for t in tiles for h in range(2)]
            rows = {t: pl.ds(pl.multiple_of((g * SB_KT + t) * tk, tk), tk) for t in tiles}
            ks = {t: _both_heads(k_ref[rows[t], :], lane_masks) for t in tiles}
            vs = {t: _both_heads(v_ref[rows[t], :], lane_masks) for t in tiles}
            r0 = {t: t * tk if masked else 0 for t in tiles}
            allowed = {t: _tail(diff, r0[t]) < -t * tk for t in tiles}
            zs = {t: lax.dot_general(_tail(qs, r0[t]), ks[t], NT, preferred_element_type=F32) for t in tiles}
            logs = {}
            for t, h in chains:
                log_b, sp = _log_terms(zs[t][:, h * tk:(h + 1) * tk])
                logs[t, h] = (log_b, jnp.where(allowed[t], sp, 0.0) if masked else sp)
            sums = {c: _split_dot(logs[c][1], later) for c in chains}
            carries = list(state[0])
            ws = {}
            for t, h in chains:
                w = jnp.exp(logs[t, h][0] - (sums[t, h][:, :tk] + _tail(carries[h], r0[t])))
                ws[t, h] = (jnp.where(allowed[t], w, 0.0) if masked else w).astype(BF)
                carries[h] = _add_tail(carries[h], r0[t], sums[t, h][:, tk:])
            acc = state[1]
            for t in tiles:
                acc = _add_tail(acc, r0[t], jnp.dot(jnp.concatenate([ws[t, 0], ws[t, 1]], axis=1), vs[t],
                                                    preferred_element_type=F32))
            return tuple(carries), acc

        zero = jnp.zeros((tq, LANES), F32)
        state = step(i, ((zero, zero), zero), True)
        state = _walk_back(i, lambda g, st: step(g, st, False), state, lambda st: st[0])
        o_ref[...] = state[1].astype(o_ref.dtype)
        o32_ref[...] = state[1]
        if rider:
            rider.run(refs, 3, 2, (pl.program_id(0) == 0) & (i == 0),
                      (pl.program_id(0) == n_pairs - 1) & (i == s // tq - 1))

    tok = pl.BlockSpec((tq, LANES), lambda p, i: (i, p))
    return pl.pallas_call(
        body, name="sb_attn_fwd", grid=(n_pairs, s // tq),
        compiler_params=_params(("arbitrary", "arbitrary"), has_side_effects=rider is not None),
        **_with_rider(
            rider, 3, 2,
            [tok, pl.BlockSpec((s, LANES), lambda p, i: (0, n_pairs + p)),
             pl.BlockSpec((s, LANES), lambda p, i: (0, 2 * n_pairs + p))],
            [tok, tok],
            [jax.ShapeDtypeStruct((s, n_pairs * LANES), BF), jax.ShapeDtypeStruct((s, n_pairs * LANES), F32)]),
    )(proj, proj, proj, *(rider.arrays if rider else []))


def _sb_bwd(proj, o32, do_a, rider=None):
    s = proj.shape[0]
    tk, tq = SB_TK, SB_KT * SB_TK
    n_pairs = 4
    scale = 1.0 / math.sqrt(SB_HEAD_DIM)

    def body(*refs):
        (q_ref, k_ref, v_ref, o_ref, do_ref), (dq_ref, dk_ref, dv_ref) = _host_refs(refs, rider, 5, 3)
        i = pl.program_id(1)

        @pl.when(i == 0)
        def _():
            dk_ref[...] = jnp.zeros_like(dk_ref)
            dv_ref[...] = jnp.zeros_like(dv_ref)

        diff, later, later_incl = _sb_consts(tq)
        qs = (q_ref[...].astype(F32) * scale).astype(BF)
        do2 = do_ref[...]
        prod = do2.astype(F32) * o_ref[...]
        lane_masks = _head_lane_masks(tk)
        first_head = lax.broadcasted_iota(jnp.int32, (tq, LANES), 1) < SB_HEAD_DIM
        totals = [jnp.broadcast_to(jnp.sum(jnp.where(keep, prod, 0.0), axis=-1, keepdims=True), (tq, tk))
                  for keep in (first_head, jnp.logical_not(first_head))]
        first_head_k = first_head[:tk]

        def step(g_idx, state, masked):
            tiles = list(reversed(range(SB_KT)))
            chains = [(t, h) for t in tiles for h in range(2)]
            rows = {t: pl.ds(pl.multiple_of((g_idx * SB_KT + t) * tk, tk), tk) for t in tiles}
            ks = {t: _both_heads(k_ref[rows[t], :], lane_masks) for t in tiles}
            vs = {t: _both_heads(v_ref[rows[t], :], lane_masks) for t in tiles}
            r0 = {t: t * tk if masked else 0 for t in tiles}
            allowed = {t: _tail(diff, r0[t]) < -t * tk for t in tiles}
            zs = {t: lax.dot_general(_tail(qs, r0[t]), ks[t], NT, preferred_element_type=F32) for t in tiles}
            dws = {t: lax.dot_general(_tail(do2, r0[t]), vs[t], NT, preferred_element_type=F32) for t in tiles}
            logs = {}
            for t, h in chains:
                log_b, sp = _log_terms(zs[t][:, h * tk:(h + 1) * tk])
                logs[t, h] = (log_b, jnp.where(allowed[t], sp, 0.0) if masked else sp)
            sums = {c: _split_dot(logs[c][1], later) for c in chains}
            c_log, c_g = list(state[0]), list(state[1])
            ws, gs = {}, {}
            for t, h in chains:
                w = jnp.exp(logs[t, h][0] - (sums[t, h][:, :tk] + _tail(c_log[h], r0[t])))
                ws[t, h] = (jnp.where(allowed[t], w, 0.0) if masked else w).astype(BF)
                c_log[h] = _add_tail(c_log[h], r0[t], sums[t, h][:, tk:])
                gs[t, h] = ws[t, h].astype(F32) * dws[t][:, h * tk:(h + 1) * tk]
            gsums = {c: _split_dot(gs[c], later_incl) for c in chains}
            dzs = {}
            for t, h in chains:
                beta = jnp.exp(logs[t, h][0])
                earlier = _tail(totals[h], r0[t]) - (gsums[t, h][:, :tk] + _tail(c_g[h], r0[t]))
                dz = gs[t, h] * (1.0 - beta) - earlier * beta
                dzs[t, h] = (jnp.where(allowed[t], dz, 0.0) if masked else dz).astype(BF)
                c_g[h] = _add_tail(c_g[h], r0[t], gsums[t, h][:, tk:])
            dq = state[2]
            for t in tiles:
                dz_both = jnp.concatenate([dzs[t, 0], dzs[t, 1]], axis=1)
                w_both = jnp.concatenate([ws[t, 0], ws[t, 1]], axis=1)
                dq = _add_tail(dq, r0[t], jnp.dot(dz_both, ks[t], preferred_element_type=F32))
                dk2 = lax.dot_general(dz_both, _tail(qs, r0[t]), TN, preferred_element_type=F32)
                dv2 = lax.dot_general(w_both, _tail(do2, r0[t]), TN, preferred_element_type=F32)
                dk_ref[rows[t], :] += jnp.where(first_head_k, dk2[:tk], dk2[tk:])
                dv_ref[rows[t], :] += jnp.where(first_head_k, dv2[:tk], dv2[tk:])
            return tuple(c_log), tuple(c_g), dq

        zero = jnp.zeros((tq, LANES), F32)
        state = step(i, ((zero, zero), (zero, zero), zero), True)
        state = _walk_back(i, lambda g, st: step(g, st, False), state, lambda st: st[0])
        dq_ref[...] = (state[2] * scale).astype(dq_ref.dtype)
        if rider:
            rider.run(refs, 5, 3, (pl.program_id(0) == 0) & (i == 0),
                      (pl.program_id(0) == n_pairs - 1) & (i == s // tq - 1))

    width = n_pairs * LANES
    tok = pl.BlockSpec((tq, LANES), lambda p, i: (i, p))
    return pl.pallas_call(
        body, name="sb_attn_bwd", grid=(n_pairs, s // tq),
        compiler_params=_params(("arbitrary", "arbitrary"), has_side_effects=rider is not None),
        **_with_rider(
            rider, 5, 3,
            [tok, pl.BlockSpec((s, LANES), lambda p, i: (0, n_pairs + p)),
             pl.BlockSpec((s, LANES), lambda p, i: (0, 2 * n_pairs + p)), tok, tok],
            [tok, pl.BlockSpec((s, LANES), lambda p, i: (0, p)), pl.BlockSpec((s, LANES), lambda p, i: (0, p))],
            [jax.ShapeDtypeStruct((s, width), BF), jax.ShapeDtypeStruct((s, width), F32),
             jax.ShapeDtypeStruct((s, width), F32)]),
    )(proj, proj, proj, o32, do_a, *(rider.arrays if rider else []))


CONV_COL0 = 12


def _shift_rows(v, k):
    n = v.shape[0]
    row = lax.broadcasted_iota(jnp.int32, v.shape, 0)
    rolled = pltpu.roll(v, k % n, axis=0)
    keep = row >= k if k > 0 else row < n + k
    return jnp.where(keep, rolled, 0.0)


def _conv_specs(s):
    return [pl.BlockSpec((s, LANES), lambda cb: (0, CONV_COL0 + cb)),
            pl.BlockSpec((s, LANES), lambda cb: (0, CONV_COL0 + 4 + cb)),
            pl.BlockSpec((s, LANES), lambda cb: (0, CONV_COL0 + 8 + cb)),
            pl.BlockSpec((CONV_ROWS, LANES), lambda cb: (0, cb))]


def _conv_fwd(proj, conv_w):
    s = proj.shape[0]

    def body(u_ref, gb_ref, gc_ref, w_ref, y_ref):
        cu = gc_ref[...].astype(F32) * u_ref[...].astype(F32)
        w = w_ref[...]
        y = w[0:1] * _shift_rows(cu, 2) + w[1:2] * _shift_rows(cu, 1) + w[2:3] * cu
        y_ref[...] = (gb_ref[...].astype(F32) * y).astype(y_ref.dtype)

    return pl.pallas_call(
        body, name="conv_fwd", grid=(4,), in_specs=_conv_specs(s),
        out_specs=pl.BlockSpec((s, LANES), lambda cb: (0, cb)),
        out_shape=jax.ShapeDtypeStruct((s, 4 * LANES), BF),
        compiler_params=_params(("parallel",)),
    )(proj, proj, proj, conv_w)


def _conv_bwd(proj, conv_w, dy, rider=None):
    s = proj.shape[0]
    n_blocks = 4

    def body(*refs):
        (u_ref, gb_ref, gc_ref, w_ref, dy_ref), (du_ref, dgb_ref, dgc_ref, dw_ref) = _host_refs(refs, rider, 5, 4)
        if rider:
            rider.run(refs, 5, 4, pl.program_id(0) == 0, pl.program_id(0) == n_blocks - 1)
        u, gc = u_ref[...].astype(F32), gc_ref[...].astype(F32)
        dyv = dy_ref[...].astype(F32)
        w = w_ref[...]
        cu = gc * u
        cu1, cu2 = _shift_rows(cu, 1), _shift_rows(cu, 2)
        conv = w[0:1] * cu2 + w[1:2] * cu1 + w[2:3] * cu
        dgb_ref[...] = (dyv * conv).astype(dgb_ref.dtype)
        dc = dyv * gb_ref[...].astype(F32)
        dcu = w[2:3] * dc + w[1:2] * _shift_rows(dc, -1) + w[0:1] * _shift_rows(dc, -2)
        dgc_ref[...] = (dcu * u).astype(dgc_ref.dtype)
        du_ref[...] = (dcu * gc).astype(du_ref.dtype)
        tap_row = lax.broadcasted_iota(jnp.int32, (CONV_ROWS, LANES), 0)
        dw = jnp.zeros((CONV_ROWS, LANES), F32)
        for t, shifted in enumerate((cu2, cu1, cu)):
            dw = jnp.where(tap_row == t, jnp.sum(dc * shifted, axis=0, keepdims=True), dw)
        dw_ref[...] = dw

    col = pl.BlockSpec((s, LANES), lambda cb: (0, cb))
    act = jax.ShapeDtypeStruct((s, 4 * LANES), BF)
    return pl.pallas_call(
        body, name="conv_bwd", grid=(n_blocks,),
        compiler_params=_params(("arbitrary",) if rider else ("parallel",), has_side_effects=rider is not None),
        **_with_rider(
            rider, 5, 4, _conv_specs(s) + [col],
            [col, col, col, pl.BlockSpec((CONV_ROWS, LANES), lambda cb: (0, cb))],
            [act, act, act, jax.ShapeDtypeStruct((CONV_ROWS, n_blocks * LANES), F32)]),
    )(proj, proj, proj, conv_w, dy, *(rider.arrays if rider else []))


def _mem_probs(q, k, scale):
    sc = lax.dot_general(q, k, NT, preferred_element_type=F32) * scale
    p = jnp.exp(sc - jnp.max(sc, axis=-1, keepdims=True))
    return p / jnp.sum(p, axis=-1, keepdims=True)


def _mem_fwd(q_m, kv, tq=2048):
    s, d = q_m.shape
    mlen = kv.shape[0]
    hd = d // MEM_HEADS
    tq = _tile(s, tq)
    scale = 1.0 / math.sqrt(hd)

    def body(q_ref, k_ref, v_ref, o_ref):
        p = _mem_probs(q_ref[...], k_ref[...], scale)
        o_ref[...] = jnp.dot(p.astype(BF), v_ref[...], preferred_element_type=F32).astype(o_ref.dtype)

    return pl.pallas_call(
        body, name="mem_attn_fwd", grid=(MEM_HEADS, s // tq),
        in_specs=[pl.BlockSpec((tq, hd), lambda h, i: (i, h)),
                  pl.BlockSpec((mlen, hd), lambda h, i: (0, h)),
                  pl.BlockSpec((mlen, hd), lambda h, i: (0, MEM_HEADS + h))],
        out_specs=pl.BlockSpec((tq, hd), lambda h, i: (i, h)),
        out_shape=jax.ShapeDtypeStruct((s, d), BF),
        compiler_params=_params(("parallel", "parallel")),
    )(q_m, kv, kv)


def _mem_bwd(q_m, kv, do_m, tq=2048):
    s, d = q_m.shape
    mlen = kv.shape[0]
    hd = d // MEM_HEADS
    tq = _tile(s, tq)
    scale = 1.0 / math.sqrt(hd)

    def body(q_ref, k_ref, v_ref, do_ref, dq_ref, dk_ref, dv_ref):
        q, k, v, do = q_ref[...], k_ref[...], v_ref[...], do_ref[...]
        p = _mem_probs(q, k, scale)
        dp = lax.dot_general(do, v, NT, preferred_element_type=F32)
        ds = p * (dp - jnp.sum(dp * p, axis=-1, keepdims=True)) * scale
        dsb = ds.astype(BF)
        dq_ref[...] = jnp.dot(dsb, k, preferred_element_type=F32).astype(dq_ref.dtype)
        dk = lax.dot_general(dsb, q, TN, preferred_element_type=F32)
        dv = lax.dot_general(p.astype(BF), do, TN, preferred_element_type=F32)
        first = pl.program_id(1) == 0

        @pl.when(first)
        def _():
            dk_ref[...] = dk
            dv_ref[...] = dv

        @pl.when(jnp.logical_not(first))
        def _():
            dk_ref[...] += dk
            dv_ref[...] += dv

    tok = pl.BlockSpec((tq, hd), lambda h, i: (i, h))
    memb = pl.BlockSpec((mlen, hd), lambda h, i: (0, h))
    return pl.pallas_call(
        body, name="mem_attn_bwd", grid=(MEM_HEADS, s // tq),
        in_specs=[tok, memb, pl.BlockSpec((mlen, hd), lambda h, i: (0, MEM_HEADS + h)), tok],
        out_specs=[tok, memb, memb],
        out_shape=[jax.ShapeDtypeStruct((s, d), BF), jax.ShapeDtypeStruct((mlen, d), F32),
                   jax.ShapeDtypeStruct((mlen, d), F32)],
        compiler_params=_params(("parallel", "arbitrary")),
    )(q_m, kv, kv, do_m)


def _place():
    x, y, c = lax.axis_index("x"), lax.axis_index("y"), lax.axis_index("c")
    other_chips = [(1 - x, y), (x, 1 - y), (1 - x, 1 - y)]
    return x, y, c, other_chips


def _chip_no(cx, cy):
    return 2 * cx + cy


HBM_SPEC = pl.BlockSpec(memory_space=pl.ANY)


def _cast_place(shard, axis, place, dtype, name):
    r, c = shard.shape
    tr = _tile(r, max(16, STREAM_BLOCK_ELEMS // c))
    nblk = r // tr
    if axis == 1:
        full, out_map = (r, N_CHIPS * c), lambda i, pref: (i, pref[0])
    else:
        full, out_map = (N_CHIPS * r, c), lambda i, pref: (pref[0] * nblk + i, 0)

    def body(pref, s_ref, o_ref):
        o_ref[...] = s_ref[...].astype(o_ref.dtype)

    return pl.pallas_call(
        body, name=name,
        grid_spec=pltpu.PrefetchScalarGridSpec(
            num_scalar_prefetch=1, grid=(nblk,),
            in_specs=[pl.BlockSpec((tr, c), lambda i, pref: (i, 0))],
            out_specs=pl.BlockSpec((tr, c), out_map)),
        out_shape=jax.ShapeDtypeStruct(full, dtype),
        compiler_params=_params(("parallel",)),
    )(place, shard)


def _region(ref, axis, chip_no, half):
    width = ref.shape[axis] // N_CHIPS
    start = pl.multiple_of(chip_no * width, width)
    if axis == 1:
        if half is None:
            return ref.at[:, pl.ds(start, width)]
        hr = ref.shape[0] // 2
        return ref.at[pl.ds(pl.multiple_of(half * hr, hr), hr), pl.ds(start, width)]
    if half is None:
        return ref.at[pl.ds(start, width), :]
    hr = width // 2
    return ref.at[pl.ds(pl.multiple_of(start + half * hr, hr), hr), :]


def _gather_weights(fulls, axes, split):
    n = len(fulls)

    def body(*refs):
        outs = refs[n:2 * n]
        send, recv, fsend, frecv = refs[2 * n:]
        x, y, c, others = _place()
        me = _chip_no(x, y)
        sibling = (x, y, 1 - c)

        def copy(w, chip_no, half, sems, p, to):
            reg = _region(outs[w], axes[w], chip_no, half)
            return pltpu.make_async_remote_copy(
                src_ref=reg, dst_ref=reg, send_sem=sems[0].at[w, p], recv_sem=sems[1].at[w, p],
                device_id=to, device_id_type=MESH)

        for w in range(n):
            for p, chip in enumerate(others):
                copy(w, me, c if split[w] else None, (send, recv), p, (chip[0], chip[1], c)).start()
        for w in range(n):
            for p, chip in enumerate(others):
                half = c if split[w] else None
                copy(w, _chip_no(*chip), half, (send, recv), p, (chip[0], chip[1], c)).wait_recv()
                if split[w]:
                    copy(w, _chip_no(*chip), c, (fsend, frecv), p, sibling).start()
        for w in range(n):
            for p, chip in enumerate(others):
                copy(w, me, c if split[w] else None, (send, recv), p, (chip[0], chip[1], c)).wait_send()
                if split[w]:
                    handed = copy(w, _chip_no(*chip), 1 - c, (fsend, frecv), p, sibling)
                    handed.wait_recv()
                    handed.wait_send()

    return pl.pallas_call(
        body, name="gather_weights",
        in_specs=[HBM_SPEC] * n, out_specs=[HBM_SPEC] * n,
        out_shape=[jax.ShapeDtypeStruct(f.shape, f.dtype) for f in fulls],
        input_output_aliases={i: i for i in range(n)},
        scratch_shapes=[pltpu.SemaphoreType.DMA((n, 3))] * 4,
        compiler_params=pltpu.CompilerParams(has_side_effects=True),
    )(*fulls)


def _fetch_copy(refs, axes, whole, send, recv, w, p, chip, c, arriving):
    owner = _chip_no(*chip) if arriving else _chip_no(lax.axis_index("x"), lax.axis_index("y"))
    reg = _region(refs[w], axes[w], owner, None if whole[w] else c)
    return pltpu.make_async_remote_copy(
        src_ref=reg, dst_ref=reg, send_sem=send[p], recv_sem=recv[p],
        device_id=(chip[0], chip[1], c), device_id_type=MESH)


N_PEERS = N_CHIPS - 1


class _Rider:
    def __init__(self, arrays, outs, aliases, start, wait):
        self.arrays, self.outs, self.aliases, self.start, self.wait = list(arrays), list(outs), aliases, start, wait
        self.scratch = [pltpu.SemaphoreType.DMA((len(self.arrays), N_PEERS))] * 2

    def run(self, refs, n_in, n_out, first, last):
        ra, ro = len(self.arrays), len(self.outs)
        ins = refs[n_in:n_in + ra]
        outs = refs[n_in + ra + n_out:n_in + ra + n_out + ro]
        send, recv = refs[-2], refs[-1]

        @pl.when(first)
        def _():
            self.start(ins, outs, send, recv)

        @pl.when(last)
        def _():
            self.wait(ins, outs, send, recv)

    def call_args(self, n_in, n_out):
        ra = len(self.arrays)
        return dict(in_specs=[HBM_SPEC] * ra, out_specs=[HBM_SPEC] * len(self.outs), out_shape=self.outs,
                    aliases={n_in + k: n_out + o for k, o in self.aliases.items()}, scratch=self.scratch)


def _host_refs(refs, rider, n_in, n_out):
    ra = len(rider.arrays) if rider else 0
    return refs[:n_in], refs[n_in + ra:n_in + ra + n_out]


def _riding_fetch(fulls, axes):
    n = len(fulls)
    whole = [False] * n

    def sems(ref, w):
        return [ref.at[w, q] for q in range(N_PEERS)]

    def start(ins, outs, send, recv):
        _, _, c, others = _place()
        for w in range(n):
            for p, chip in enumerate(others):
                _fetch_copy(outs, axes, whole, sems(send, w), sems(recv, w), w, p, chip, c, False).start()

    def wait(ins, outs, send, recv):
        _, _, c, others = _place()
        for w in range(n):
            for p, chip in enumerate(others):
                _fetch_copy(outs, axes, whole, sems(send, w), sems(recv, w), w, p, chip, c, False).wait_send()
                _fetch_copy(outs, axes, whole, sems(send, w), sems(recv, w), w, p, chip, c, True).wait_recv()

    return _Rider(fulls, [jax.ShapeDtypeStruct(f.shape, f.dtype) for f in fulls], {k: k for k in range(n)}, start, wait)


def _hand_on(fulls, axes, name):
    n = len(fulls)

    def body(*refs):
        outs = refs[n:2 * n]
        send, recv = refs[2 * n:]
        x, y, c, others = _place()

        def copy(w, p, chip, half):
            reg = _region(outs[w], axes[w], _chip_no(*chip), half)
            return pltpu.make_async_remote_copy(
                src_ref=reg, dst_ref=reg, send_sem=send.at[w, p], recv_sem=recv.at[w, p],
                device_id=(x, y, 1 - c), device_id_type=MESH)

        for w in range(n):
            for p, chip in enumerate(others):
                copy(w, p, chip, c).start()
        for w in range(n):
            for p, chip in enumerate(others):
                copy(w, p, chip, 1 - c).wait()

    return pl.pallas_call(
        body, name=name,
        in_specs=[HBM_SPEC] * n, out_specs=[HBM_SPEC] * n,
        out_shape=[jax.ShapeDtypeStruct(f.shape, f.dtype) for f in fulls],
        input_output_aliases={i: i for i in range(n)},
        scratch_shapes=[pltpu.SemaphoreType.DMA((n, 3)), pltpu.SemaphoreType.DMA((n, 3))],
        compiler_params=pltpu.CompilerParams(has_side_effects=True),
    )(*fulls)


def _pair_exchange(grads, name):
    n = len(grads)

    def body(*refs):
        ins, outs = refs[:n], refs[n:2 * n]
        send, recv = refs[2 * n:]
        x, y, c, _ = _place()
        cps = []
        for w in range(n):
            cp = pltpu.make_async_remote_copy(
                src_ref=ins[w].at[:, 1 - c], dst_ref=outs[w], send_sem=send.at[w], recv_sem=recv.at[w],
                device_id=(x, y, 1 - c), device_id_type=MESH)
            cp.start()
            cps.append(cp)
        for cp in cps:
            cp.wait()

    return pl.pallas_call(
        body, name=name,
        in_specs=[HBM_SPEC] * n, out_specs=[HBM_SPEC] * n,
        out_shape=[jax.ShapeDtypeStruct((g.shape[0],) + g.shape[2:], g.dtype) for g in grads],
        scratch_shapes=[pltpu.SemaphoreType.DMA((n,)), pltpu.SemaphoreType.DMA((n,))],
        compiler_params=pltpu.CompilerParams(has_side_effects=True),
    )(*grads)


def _pair_add(g4, got, core, name):
    nj, _, hr, cdim = g4.shape
    tr = _tile(hr, max(8, STREAM_BLOCK_ELEMS // 2 // cdim))

    def body(core_ref, a_ref, b_ref, o_ref):
        o_ref[...] = (a_ref[...].astype(F32) + b_ref[...].astype(F32)).astype(o_ref.dtype)

    return pl.pallas_call(
        body, name=name,
        grid_spec=pltpu.PrefetchScalarGridSpec(
            num_scalar_prefetch=1, grid=(nj, hr // tr),
            in_specs=[pl.BlockSpec((1, None, tr, cdim), lambda j, i, core_ref: (j, core_ref[0], i, 0)),
                      pl.BlockSpec((1, tr, cdim), lambda j, i, core_ref: (j, i, 0))],
            out_specs=pl.BlockSpec((1, tr, cdim), lambda j, i, core_ref: (j, i, 0))),
        out_shape=jax.ShapeDtypeStruct((nj, hr, cdim), BF),
        compiler_params=_params(("parallel", "parallel")),
    )(core, g4, got)


def _piece(ref, axis, j, hc):
    if axis == 0:
        return ref.at[j]
    return ref.at[0, :, pl.ds(pl.multiple_of(j * hc, hc), hc)]


def _slot_shapes(sums, axes):
    return [(N_CHIPS - 1, sm.shape[1], sm.shape[2] // (1 if ax == 0 else N_CHIPS)) for sm, ax in zip(sums, axes)]


def _slot_copy(sums, lands, axes, send, recv, w, p, chip, c):
    return pltpu.make_async_remote_copy(
        src_ref=_piece(sums[w], axes[w], _chip_no(*chip), lands[w].shape[2]), dst_ref=lands[w].at[p],
        send_sem=send[p], recv_sem=recv[p],
        device_id=(chip[0], chip[1], c), device_id_type=MESH)


def _riding_pairs(views):
    n = len(views)

    def copies(ins, outs, send, recv):
        x, y, c, _ = _place()
        return [pltpu.make_async_remote_copy(
            src_ref=ins[w].at[:, 1 - c], dst_ref=outs[w], send_sem=send.at[w, 0], recv_sem=recv.at[w, 0],
            device_id=(x, y, 1 - c), device_id_type=MESH) for w in range(n)]

    def start(ins, outs, send, recv):
        for cp in copies(ins, outs, send, recv):
            cp.start()

    def wait(ins, outs, send, recv):
        for cp in copies(ins, outs, send, recv):
            cp.wait()

    outs = [jax.ShapeDtypeStruct((g.shape[0],) + g.shape[2:], g.dtype) for g in views]
    return _Rider(views, outs, {}, start, wait)


def _riding_slots(sums, axes):
    n = len(sums)
    shapes = _slot_shapes(sums, axes)

    def copies(ins, outs, send, recv):
        _, _, c, others = _place()
        return [_slot_copy(ins, outs, axes, [send.at[w, q] for q in range(N_PEERS)],
                           [recv.at[w, q] for q in range(N_PEERS)], w, p, chip, c)
                for w in range(n) for p, chip in enumerate(others)]

    def start(ins, outs, send, recv):
        for cp in copies(ins, outs, send, recv):
            cp.start()

    def wait(ins, outs, send, recv):
        for cp in copies(ins, outs, send, recv):
            cp.wait()

    return _Rider(sums, [jax.ShapeDtypeStruct(sh, sm.dtype) for sh, sm in zip(shapes, sums)], {}, start, wait)


def _chip_sum(psum, slots, axis, place, name):
    _, hr, hc = slots.shape
    tr = _tile(hr, 256)
    own_map = (lambda i, pref: (0, i, pref[0])) if axis == 1 else (lambda i, pref: (pref[0], i, 0))

    def body(pref, own_ref, s_ref, o_ref):
        o_ref[...] = ((own_ref[...].astype(F32) + s_ref[0].astype(F32)) + s_ref[1].astype(F32)) + s_ref[2].astype(F32)

    return pl.pallas_call(
        body, name=name,
        grid_spec=pltpu.PrefetchScalarGridSpec(
            num_scalar_prefetch=1, grid=(hr // tr,),
            in_specs=[pl.BlockSpec((None, tr, hc), own_map),
                      pl.BlockSpec((N_CHIPS - 1, tr, hc), lambda i, pref: (0, i, 0))],
            out_specs=pl.BlockSpec((None, tr, hc), lambda i, pref: (pref[1], i, 0))),
        out_shape=jax.ShapeDtypeStruct((2, hr, hc), F32),
        compiler_params=_params(("parallel",)),
    )(place, psum, slots)


def _half_swap(both):
    n = len(both)

    def body(*refs):
        outs = refs[n:2 * n]
        send, recv = refs[2 * n:]
        x, y, c, _ = _place()

        def copy(w, half):
            return pltpu.make_async_remote_copy(
                src_ref=outs[w].at[half], dst_ref=outs[w].at[half], send_sem=send.at[w], recv_sem=recv.at[w],
                device_id=(x, y, 1 - c), device_id_type=MESH)

        for w in range(n):
            copy(w, c).start()
        for w in range(n):
            copy(w, 1 - c).wait()

    return pl.pallas_call(
        body, name="grad_half_swap",
        in_specs=[HBM_SPEC] * n, out_specs=[HBM_SPEC] * n,
        out_shape=[jax.ShapeDtypeStruct(b.shape, b.dtype) for b in both],
        input_output_aliases={i: i for i in range(n)},
        scratch_shapes=[pltpu.SemaphoreType.DMA((n,)), pltpu.SemaphoreType.DMA((n,))],
        compiler_params=pltpu.CompilerParams(has_side_effects=True),
    )(*both)


def _allreduce_small(pack):
    rows, d = pack.shape

    def body(p_ref, o_ref, slots, send, recv):
        x, y, c, _ = _place()
        me = 4 * x + 2 * y + c
        slots[me] = p_ref[...]
        cps = []
        for k in range(1, N_DEV):
            px, py, pc = x ^ (k >> 2), y ^ ((k >> 1) & 1), c ^ (k & 1)
            cp = pltpu.make_async_remote_copy(
                src_ref=p_ref, dst_ref=slots.at[me], send_sem=send.at[k - 1], recv_sem=recv.at[k - 1],
                device_id=(px, py, pc), device_id_type=MESH)
            cp.start()
            cps.append(cp)
        for k in range(1, N_DEV):
            px, py, pc = x ^ (k >> 2), y ^ ((k >> 1) & 1), c ^ (k & 1)
            arrival = pltpu.make_async_remote_copy(
                src_ref=p_ref, dst_ref=slots.at[4 * px + 2 * py + pc], send_sem=send.at[k - 1],
                recv_sem=recv.at[k - 1], device_id=(px, py, pc), device_id_type=MESH)
            arrival.wait_recv()
            arrival.wait_send()
        acc = slots[0]
        for k in range(1, N_DEV):
            acc = acc + slots[k]
        o_ref[...] = acc

    vm = pl.BlockSpec(memory_space=pltpu.VMEM)
    return pl.pallas_call(
        body, name="allreduce_small", in_specs=[vm], out_specs=vm,
        out_shape=jax.ShapeDtypeStruct((rows, d), F32),
        scratch_shapes=[pltpu.VMEM((N_DEV, rows, d), F32), pltpu.SemaphoreType.DMA((N_DEV - 1,)),
                        pltpu.SemaphoreType.DMA((N_DEV - 1,))],
        compiler_params=pltpu.CompilerParams(has_side_effects=True),
    )(pack)


def _adamw(w, g, m, v, name):
    rows, cols = w.shape

    def fn(wv, gv, mv, vv):
        m2 = ADAM_B1 * mv + (1.0 - ADAM_B1) * gv
        v2 = ADAM_B2 * vv + (1.0 - ADAM_B2) * (gv * gv)
        m_hat = m2 / (1.0 - ADAM_B1 ** ADAM_STEP)
        v_hat = v2 / (1.0 - ADAM_B2 ** ADAM_STEP)
        delta = -ADAM_LR * (m_hat / (jnp.sqrt(v_hat) + ADAM_EPS) + ADAM_WD * wv)
        return delta, m2, v2

    ins = [(a, cols, 0) for a in (w, g, m, v)]
    return _rowwise(fn, ins, [(cols, F32)] * 3, rows=rows, tm=_tile(rows, max(8, STREAM_BLOCK_ELEMS // 4 // cols)),
                    name=name)


BIG = ["w_in", "w_branch_a", "w_branch_b", "w_mix_out", "w_mem_q", "w_mem_kv", "w_mem_o", "w_ffn_in", "w_ffn_out"]
BIG_AXIS = {"w_in": 1, "w_branch_a": 1, "w_branch_b": 1, "w_mix_out": 0, "w_mem_q": 0, "w_mem_kv": 1,
            "w_mem_o": 0, "w_ffn_in": 1, "w_ffn_out": 0}
NORMS = ["norm_mix", "norm_mem_q", "norm_mem_kv", "norm_ffn", "norm_final"]
ORDER = ["norm_mix", "w_in", "conv_w", "w_branch_a", "w_branch_b", "w_mix_out", "norm_mem_q", "norm_mem_kv",
         "w_mem_q", "w_mem_kv", "w_mem_o", "norm_ffn", "w_ffn_in", "w_ffn_out", "norm_final"]


def _pack_small(vals, conv):
    d = vals[0].shape[-1]
    rows = [v.reshape(1, d) for v in vals]
    conv = jnp.pad(conv, ((0, 0), (0, d - conv.shape[1])))
    pad = jnp.zeros((SMALL_ROWS - len(rows) - CONV_K, d), F32)
    return jnp.concatenate(rows + [conv, pad], axis=0)


def kernel(x, mem, norm_mix, w_in, conv_w, w_branch_a, w_branch_b, w_mix_out, norm_mem_q, norm_mem_kv, w_mem_q, w_mem_kv, w_mem_o, norm_ffn, w_ffn_in, w_ffn_out, norm_final, loss_target, m_norm_mix, m_w_in, m_conv_w, m_w_branch_a, m_w_branch_b, m_w_mix_out, m_norm_mem_q, m_norm_mem_kv, m_w_mem_q, m_w_mem_kv, m_w_mem_o, m_norm_ffn, m_w_ffn_in, m_w_ffn_out, m_norm_final, v_norm_mix, v_w_in, v_conv_w, v_w_branch_a, v_w_branch_b, v_w_mix_out, v_norm_mem_q, v_norm_mem_kv, v_w_mem_q, v_w_mem_kv, v_w_mem_o, v_norm_ffn, v_w_ffn_in, v_w_ffn_out, v_norm_final):
    args = dict(locals())
    wts = {n: args[n] for n in ORDER}
    mom = {n: args["m_" + n] for n in ORDER}
    var = {n: args["v_" + n] for n in ORDER}
    x = x[0]
    mem = mem[0]
    target = loss_target[0]
    s, d = x.shape
    gains = {n: wts[n].reshape(1, d) for n in NORMS}
    chip = 2 * lax.axis_index("x") + lax.axis_index("y")
    core = lax.axis_index("c").astype(jnp.int32).reshape(1)
    place = jnp.stack([chip, lax.axis_index("c")]).astype(jnp.int32)

    conv_shard = jnp.pad(conv_w[0], ((0, CONV_ROWS - CONV_K), (0, 0)))
    first = [_cast_place(wts["w_in"][0], BIG_AXIS["w_in"], place, BF, "place_w_in"),
             _cast_place(conv_shard, 1, place, F32, "place_conv_w")]
    w_in_full, conv_full = _gather_weights(first, [BIG_AXIS["w_in"], 1], [True, False])
    W = {"w_in": w_in_full}
    ride_in_proj = ["w_branch_a", "w_branch_b", "w_mix_out", "w_mem_q", "w_mem_o"]
    ride_attention = ["w_ffn_in", "w_mem_kv"]
    ride_ffn_in = ["w_ffn_out"]
    placed = {n: _cast_place(wts[n][0], BIG_AXIS[n], place, BF, "place_" + n)
              for n in ride_in_proj + ride_attention + ride_ffn_in}

    def fetch(names):
        return _riding_fetch([placed[n] for n in names], [BIG_AXIS[n] for n in names])

    def hand_on(names, bufs, tag):
        W.update(zip(names, _hand_on(bufs, [BIG_AXIS[n] for n in names], "gather_hand_on" + tag)))

    h1 = _rmsnorm(x, gains["norm_mix"], "norm_mix_fwd")
    proj, *bufs_a = _matmul(h1, W["w_in"], tn=1280, name="in_proj", rider=fetch(ride_in_proj))
    o_a, o_a32, *bufs_b = _sb_fwd(proj, fetch(ride_attention))
    y_b = _conv_fwd(proj, conv_full)
    hand_on(ride_in_proj + ride_attention, bufs_a + bufs_b, "")
    br_a, br_b, merged = _branches_merge(o_a, y_b, W["w_branch_a"], W["w_branch_b"], proj)
    x1 = _matmul(merged, W["w_mix_out"], tn=1024, out_dtype=F32, resid=x, name="mix_out")

    hq, q_m = _norm_matmul(x1, gains["norm_mem_q"], W["w_mem_q"], tn=1024, name="norm_mem_q")
    mn, kv = _norm_matmul(mem, gains["norm_mem_kv"], W["w_mem_kv"], tn=1024, name="norm_mem_kv")
    o_m = _mem_fwd(q_m, kv)
    x2 = _matmul(o_m, W["w_mem_o"], tn=1024, out_dtype=F32, resid=x1, name="mem_o")

    hf, gate, up, act, *bufs_c = _norm_ffn_in_swiglu(x2, gains["norm_ffn"], W["w_ffn_in"], rider=fetch(ride_ffn_in))
    hand_on(ride_ffn_in, bufs_c, "_late")

    dx3, dx3_b, dg_final, loss_part = _ffn_out_loss(act, W["w_ffn_out"], x2, gains["norm_final"], target)

    def halves_of(names):
        views = []
        for n in names:
            r, cdim = gw[n].shape
            views.append(gw[n].reshape(1, 2, r // 2, cdim) if BIG_AXIS[n] == 1
                         else gw[n].reshape(N_CHIPS, 2, r // (2 * N_CHIPS), cdim))
        return views

    def pair_adds(names, views, got):
        return [_pair_add(v, g, core, "pair_add_" + n) for n, v, g in zip(names, views, got)]

    def chip_sums(names, sums, slots):
        return [_chip_sum(sm, sl, BIG_AXIS[n], place, "chip_sum_" + n) for n, sm, sl in zip(names, sums, slots)]

    gw = {"w_ffn_out": _matmul(act, dx3_b, ta=True, tm=1408, tn=512, name="gw_ffn_out")}
    first_views = halves_of(["w_ffn_out"])
    dgu, *first_got = _d_act_swiglu(dx3_b, W["w_ffn_out"], gate, up, rider=_riding_pairs(first_views))
    gw["w_ffn_in"] = _matmul(hf, dgu, ta=True, tn=512, name="gw_ffn_in")
    first_sums = pair_adds(["w_ffn_out"], first_views, first_got)
    dx2, dx2_b, dg_ffn, *first_slots = _matmul_norm_bwd(
        dgu, W["w_ffn_in"], x2, gains["norm_ffn"], dx3, tm=256, bf_copy=True, name="d_hf_norm_bwd",
        rider=_riding_slots(first_sums, [BIG_AXIS["w_ffn_out"]]))

    do_m = _matmul(dx2_b, W["w_mem_o"], tb=True, tn=1024, name="d_o_m")
    gw["w_mem_o"] = _matmul(o_m, dx2_b, ta=True, tn=512, name="gw_mem_o")
    dq_m, dk_m, dv_m = _mem_bwd(q_m, kv, do_m)
    dkv = jnp.concatenate([dk_m, dv_m], axis=-1)
    dx1, dx1_b, dg_q = _matmul_norm_bwd(dq_m, W["w_mem_q"], x1, gains["norm_mem_q"], dx2, tm=512, bf_copy=True,
                                        name="d_hq_norm_bwd")
    gw["w_mem_q"] = _matmul(hq, dq_m, ta=True, tn=512, name="gw_mem_q")
    dmn = _matmul(dkv, W["w_mem_kv"], tb=True, tn=1024, out_dtype=F32, name="d_mn")
    gw["w_mem_kv"] = _matmul(mn, dkv, ta=True, tn=1024, name="gw_mem_kv")
    dg_kv = _norm_gain_grad(mem, dmn, "norm_mem_kv_bwd")

    dbr_a, dbr_b, dga, dgb = _d_merged_gates(dx1_b, W["w_mix_out"], proj, br_a, br_b)
    gw["w_mix_out"] = _matmul(merged, dx1_b, ta=True, tn=512, name="gw_mix_out")
    do_a = _matmul(dbr_a, W["w_branch_a"], tb=True, name="d_o_a")
    gw["w_branch_a"] = _matmul(o_a, dbr_a, ta=True, tn=512, name="gw_branch_a")
    dy_b = _matmul(dbr_b, W["w_branch_b"], tb=True, name="d_y_b")
    gw["w_branch_b"] = _matmul(y_b, dbr_b, ta=True, tn=512, name="gw_branch_b")
    early = [n for n in BIG if n not in ("w_in", "w_ffn_out")]
    early_views = halves_of(early)
    du, dgate_b, dgate_c, dconv, *early_got = _conv_bwd(proj, conv_full, dy_b, rider=_riding_pairs(early_views))
    early_sums = pair_adds(early, early_views, early_got)
    dq, dk, dv, *early_slots = _sb_bwd(proj, o_a32, do_a, _riding_slots(early_sums, [BIG_AXIS[n] for n in early]))

    def assemble(*parts):
        return jnp.concatenate([p.astype(BF) for p in parts], axis=-1)

    hw = dq.shape[1]
    dproj = _rowwise(assemble, [(t, hw, 0) for t in (dq, dk, dv, du, dgate_b, dgate_c)] + [(dga, d, 0), (dgb, d, 0)],
                     [(proj.shape[1], BF)], rows=s, tm=256, name="assemble_dproj")[0]
    gw["w_in"] = _matmul(h1, dproj, ta=True, tn=640, name="gw_in")
    in_views = halves_of(["w_in"])
    in_sums = pair_adds(["w_in"], in_views, _pair_exchange(in_views, "grad_pair_exchange_in"))
    grad_x, dg_mix, *in_slots = _matmul_norm_bwd(dproj, W["w_in"], x, gains["norm_mix"], dx1, tm=256, bf_copy=False,
                                                 name="d_h1_norm_bwd",
                                                 rider=_riding_slots(in_sums, [BIG_AXIS["w_in"]]))

    halves = dict(zip(early, chip_sums(early, early_sums, early_slots)))
    halves.update(zip(["w_ffn_out"], chip_sums(["w_ffn_out"], first_sums, first_slots)))
    halves.update(zip(["w_in"], chip_sums(["w_in"], in_sums, in_slots)))
    both = _half_swap([halves[n] for n in BIG])
    grads = {n: b.reshape(wts[n].shape[1:]) for n, b in zip(BIG, both)}

    small_g = [dg_mix, dg_q, dg_kv, dg_ffn, dg_final]
    pack = _pack_small(small_g, dconv[:CONV_K])
    pack = pack.at[ROW_LOSS].set(jnp.broadcast_to(loss_part[0, :1], (d,)))
    red = _allreduce_small(pack)
    loss = red[ROW_LOSS, 0]
    cw = conv_w.shape[2]
    conv_g = lax.dynamic_slice(red, (ROW_CONV, chip * cw), (CONV_K, cw))
    small_grad = _pack_small([red[i] for i in range(len(NORMS))], conv_g)
    small = [_pack_small([t[n] for n in NORMS], t["conv_w"][0]) for t in (wts, mom, var)]
    s_delta, s_m, s_v = _adamw(small[0], small_grad, small[1], small[2], "adamw_small")

    out = {"grad": {}, "delta": {}, "new_m": {}, "new_v": {}}
    for n in BIG:
        shp = wts[n].shape
        dl, m2, v2 = _adamw(wts[n][0], grads[n], mom[n][0], var[n][0], "adamw_" + n)
        out["grad"][n] = grads[n].reshape(shp)
        out["delta"][n], out["new_m"][n], out["new_v"][n] = dl.reshape(shp), m2.reshape(shp), v2.reshape(shp)
    for key, blk in (("grad", small_grad), ("delta", s_delta), ("new_m", s_m), ("new_v", s_v)):
        for i, n in enumerate(NORMS):
            out[key][n] = blk[i].reshape(wts[n].shape)
        out[key]["conv_w"] = blk[ROW_CONV:ROW_CONV + CONV_K, :cw].reshape(conv_w.shape)

    return (loss, grad_x[None], *[out["grad"][n] for n in ORDER], *[out["delta"][n] for n in ORDER],
            *[out["new_m"][n] for n in ORDER], *[out["new_v"][n] for n in ORDER])
```

```python
import math

import jax
import jax.numpy as jnp
from jax import lax
from jax.experimental import pallas as pl
from jax.experimental.pallas import tpu as pltpu

BF = jnp.bfloat16
F32 = jnp.float32
MESH = pl.DeviceIdType.MESH

SB_HEAD_DIM = 64
LANES = 128
MEM_HEADS = 4
CONV_K = 3
CONV_ROWS = 8
EPS = 1e-6
N_CHIPS = 4
N_DEV = 8
VMEM_LIMIT = 56 * 1024 * 1024
STREAM_BLOCK_ELEMS = 1 << 20

ADAM_LR = 0.001
ADAM_B1 = 0.9
ADAM_B2 = 0.999
ADAM_EPS = 1e-08
ADAM_WD = 0.01
ADAM_STEP = 10

SMALL_ROWS = 16
ROW_CONV = 5
ROW_LOSS = 8


def _params(sem=None, **kw):
    return pltpu.CompilerParams(dimension_semantics=sem, vmem_limit_bytes=VMEM_LIMIT, **kw)


def _tile(dim, pref):
    if dim <= pref:
        return dim
    for step in (LANES, 8):
        t = (pref // step) * step
        while t >= step:
            if dim % t == 0:
                return t
            t -= step
    raise ValueError(f"no tile of {dim} under {pref}")


def _matmul(a, b, *, ta=False, tb=False, tm=1024, tn=512, out_dtype=BF, resid=None, rider=None, name):
    if ta:
        kdim, m = a.shape
    else:
        m, kdim = a.shape
    n = b.shape[0] if tb else b.shape[1]
    tm, tn = _tile(m, tm), _tile(n, tn)
    a_spec = pl.BlockSpec((kdim, tm), lambda i, j: (0, i)) if ta else pl.BlockSpec((tm, kdim), lambda i, j: (i, 0))
    b_spec = pl.BlockSpec((tn, kdim), lambda i, j: (j, 0)) if tb else pl.BlockSpec((kdim, tn), lambda i, j: (0, j))
    o_spec = pl.BlockSpec((tm, tn), lambda i, j: (i, j))
    dims = (((0 if ta else 1,), (1 if tb else 0,)), ((), ()))
    has_res = resid is not None
    n_in = 2 + has_res

    def body(*refs):
        ins, (o_ref,) = _host_refs(refs, rider, n_in, 1)
        av, bv = ins[0][...], ins[1][...]
        if av.dtype != BF:
            av = av.astype(BF)
        if bv.dtype != BF:
            bv = bv.astype(BF)
        acc = lax.dot_general(av, bv, dims, preferred_element_type=F32)
        if has_res:
            acc = ins[2][...] + acc
        o_ref[...] = acc.astype(o_ref.dtype)
        if rider:
            rider.run(refs, n_in, 1, (pl.program_id(0) == 0) & (pl.program_id(1) == 0),
                      (pl.program_id(0) == m // tm - 1) & (pl.program_id(1) == n // tn - 1))

    res = pl.pallas_call(
        body, name=name, grid=(m // tm, n // tn),
        compiler_params=_params(("arbitrary", "arbitrary") if rider else ("parallel", "parallel"),
                                has_side_effects=rider is not None),
        **_with_rider(rider, n_in, 1, [a_spec, b_spec] + ([o_spec] if has_res else []), [o_spec],
                      [jax.ShapeDtypeStruct((m, n), out_dtype)]),
    )(*([a, b] + ([resid] if has_res else []) + (rider.arrays if rider else [])))
    return res if rider else res[0]


def _rowwise(fn, ins, outs, *, rows, tm, name, accs=()):
    tm = _tile(rows, tm)
    in_specs, args = [], []
    for arr, cols, cb in ins:
        if cols is None:
            in_specs.append(pl.BlockSpec(arr.shape, lambda i, nd=arr.ndim: (0,) * nd))
        else:
            in_specs.append(pl.BlockSpec((tm, cols), lambda i, cb=cb: (i, cb)))
        args.append(arr)
    out_specs = [pl.BlockSpec((tm, cols), lambda i: (i, 0)) for cols, _ in outs]
    out_shape = [jax.ShapeDtypeStruct((rows, cols), dt) for cols, dt in outs]
    for r, c in accs:
        out_specs.append(pl.BlockSpec((r, c), lambda i: (0, 0)))
        out_shape.append(jax.ShapeDtypeStruct((r, c), F32))
    n_in, n_out = len(ins), len(outs)

    def body(*refs):
        res = fn(*[r[...] for r in refs[:n_in]])
        if not isinstance(res, (tuple, list)):
            res = (res,)
        refs = refs[n_in:]
        for o_ref, val in zip(refs[:n_out], res[:n_out]):
            o_ref[...] = val.astype(o_ref.dtype)
        first = pl.program_id(0) == 0
        for a_ref, val in zip(refs[n_out:], res[n_out:]):
            @pl.when(first)
            def _(a_ref=a_ref, val=val):
                a_ref[...] = val

            @pl.when(jnp.logical_not(first))
            def _(a_ref=a_ref, val=val):
                a_ref[...] += val

    res = pl.pallas_call(
        body, name=name, grid=(rows // tm,), in_specs=in_specs, out_specs=out_specs, out_shape=out_shape,
        compiler_params=_params(("arbitrary",) if accs else ("parallel",)),
    )(*args)
    return res


def _rstd(xf):
    return lax.rsqrt(jnp.mean(xf * xf, axis=-1, keepdims=True) + EPS)


def _rmsnorm(x, g, name):
    rows, d = x.shape
    return _rowwise(lambda xv, gv: xv * _rstd(xv) * gv, [(x, d, 0), (g, None, None)], [(d, BF)],
                    rows=rows, tm=512, name=name)[0]


def _norm_gain_grad(x, dy, name):
    rows, d = x.shape

    def fn(xv, dyv):
        return (jnp.sum(dyv.astype(F32) * (xv * _rstd(xv)), axis=0, keepdims=True),)

    return _rowwise(fn, [(x, d, 0), (dy, d, 0)], [], rows=rows, tm=512, name=name, accs=[(1, d)])[0]


def _matmul_norm_bwd(a, w, x, g, resid, *, tm, bf_copy, name, rider=None):
    m, kdim = a.shape
    d = w.shape[0]
    tm = _tile(m, tm)
    row = lambda i: (i, 0)
    whole = lambda i: (0, 0)
    n_out = 2 + bf_copy

    def body(*refs):
        (a_ref, w_ref, x_ref, g_ref, r_ref), outs = _host_refs(refs, rider, 5, n_out)
        dy = lax.dot_general(a_ref[...], w_ref[...], NT, preferred_element_type=F32)
        xv = x_ref[...]
        r = _rstd(xv)
        xn = xv * r
        dxn = dy * g_ref[...]
        dx = r_ref[...] + r * (dxn - xn * jnp.mean(dxn * xn, axis=-1, keepdims=True))
        outs[0][...] = dx
        if bf_copy:
            outs[1][...] = dx.astype(BF)
        dg = jnp.sum(dy * xn, axis=0, keepdims=True)
        first = pl.program_id(0) == 0

        @pl.when(first)
        def _():
            outs[-1][...] = dg

        @pl.when(jnp.logical_not(first))
        def _():
            outs[-1][...] += dg

        if rider:
            rider.run(refs, 5, n_out, first, pl.program_id(0) == m // tm - 1)

    tok = pl.BlockSpec((tm, d), row)
    out_specs = [tok] + ([tok] if bf_copy else []) + [pl.BlockSpec((1, d), whole)]
    out_shape = ([jax.ShapeDtypeStruct((m, d), F32)] + ([jax.ShapeDtypeStruct((m, d), BF)] if bf_copy else [])
                 + [jax.ShapeDtypeStruct((1, d), F32)])
    return pl.pallas_call(
        body, name=name, grid=(m // tm,),
        compiler_params=_params(("arbitrary",), has_side_effects=rider is not None),
        **_with_rider(
            rider, 5, n_out,
            [pl.BlockSpec((tm, kdim), row), pl.BlockSpec((d, kdim), whole), tok, pl.BlockSpec((1, d), whole), tok],
            out_specs, out_shape),
    )(a, w, x, g, resid, *(rider.arrays if rider else []))


def _norm_ffn_in_swiglu(x, g, w, *, tm=1024, tn=1408, rider=None):
    m, kdim = x.shape
    f = w.shape[1] // 2
    tm, tn = _tile(m, tm), _tile(f, tn)
    nj = f // tn

    def body(*refs):
        (x_ref, g_ref, wg_ref, wu_ref), (h_ref, gate_ref, up_ref, act_ref) = _host_refs(refs, rider, 4, 4)
        if rider:
            rider.run(refs, 4, 4, (pl.program_id(0) == 0) & (pl.program_id(1) == 0),
                      (pl.program_id(0) == m // tm - 1) & (pl.program_id(1) == nj - 1))

        @pl.when(pl.program_id(1) == 0)
        def _():
            xv = x_ref[...]
            h_ref[...] = (xv * _rstd(xv) * g_ref[...]).astype(h_ref.dtype)

        hv = h_ref[...]
        gate = jnp.dot(hv, wg_ref[...], preferred_element_type=F32)
        up = jnp.dot(hv, wu_ref[...], preferred_element_type=F32)
        gate_ref[...] = gate.astype(gate_ref.dtype)
        up_ref[...] = up.astype(up_ref.dtype)
        act_ref[...] = (gate * jax.nn.sigmoid(gate) * up).astype(act_ref.dtype)

    tile = pl.BlockSpec((tm, tn), lambda i, j: (i, j))
    rows = pl.BlockSpec((tm, kdim), lambda i, j: (i, 0))
    return pl.pallas_call(
        body, name="norm_ffn_in_swiglu", grid=(m // tm, nj),
        compiler_params=_params(("arbitrary", "arbitrary") if rider else ("parallel", "arbitrary"),
                                has_side_effects=rider is not None),
        **_with_rider(
            rider, 4, 4,
            [rows, pl.BlockSpec((1, kdim), lambda i, j: (0, 0)), pl.BlockSpec((kdim, tn), lambda i, j: (0, j)),
             pl.BlockSpec((kdim, tn), lambda i, j: (0, nj + j))],
            [rows, tile, tile, tile],
            [jax.ShapeDtypeStruct((m, kdim), BF)] + [jax.ShapeDtypeStruct((m, f), BF)] * 3),
    )(x, g, w, w, *(rider.arrays if rider else []))


def _norm_matmul(x, g, w, *, tm=1024, tn, name):
    m, kdim = x.shape
    n = w.shape[1]
    tm, tn = _tile(m, tm), _tile(n, tn)

    def body(x_ref, g_ref, w_ref, h_ref, o_ref):
        @pl.when(pl.program_id(1) == 0)
        def _():
            xv = x_ref[...]
            h_ref[...] = (xv * _rstd(xv) * g_ref[...]).astype(h_ref.dtype)

        o_ref[...] = jnp.dot(h_ref[...], w_ref[...], preferred_element_type=F32).astype(o_ref.dtype)

    rows = pl.BlockSpec((tm, kdim), lambda i, j: (i, 0))
    return pl.pallas_call(
        body, name=name, grid=(m // tm, n // tn),
        in_specs=[rows, pl.BlockSpec((1, kdim), lambda i, j: (0, 0)), pl.BlockSpec((kdim, tn), lambda i, j: (0, j))],
        out_specs=[rows, pl.BlockSpec((tm, tn), lambda i, j: (i, j))],
        out_shape=[jax.ShapeDtypeStruct((m, kdim), BF), jax.ShapeDtypeStruct((m, n), BF)],
        compiler_params=_params(("parallel", "arbitrary")),
    )(x, g, w)


def _ffn_out_loss(act, w, resid, g, target, *, tm=512):
    m, f = act.shape
    d = w.shape[1]
    tm = _tile(m, tm)
    row = lambda i: (i, 0)
    whole = lambda i: (0, 0)

    def body(a_ref, w_ref, r_ref, g_ref, t_ref, dx_ref, dxb_ref, dg_ref, loss_ref):
        xv = r_ref[...] + jnp.dot(a_ref[...], w_ref[...], preferred_element_type=F32)
        gv = g_ref[...]
        r = _rstd(xv)
        xn = xv * r
        err = xn * gv - t_ref[...]
        loss = 0.5 * jnp.sum(jnp.mean(err * err, axis=-1, keepdims=True), axis=0, keepdims=True)
        dyv = err * (1.0 / d)
        dxn = dyv * gv
        dx = r * (dxn - xn * jnp.mean(dxn * xn, axis=-1, keepdims=True))
        dx_ref[...] = dx
        dxb_ref[...] = dx.astype(dxb_ref.dtype)
        dg = jnp.sum(dyv * xn, axis=0, keepdims=True)
        loss_b = jnp.broadcast_to(loss, (1, LANES))
        first = pl.program_id(0) == 0

        @pl.when(first)
        def _():
            dg_ref[...] = dg
            loss_ref[...] = loss_b

        @pl.when(jnp.logical_not(first))
        def _():
            dg_ref[...] += dg
            loss_ref[...] += loss_b

    tok = pl.BlockSpec((tm, d), row)
    return pl.pallas_call(
        body, name="ffn_out_loss", grid=(m // tm,),
        in_specs=[pl.BlockSpec((tm, f), row), pl.BlockSpec((f, d), whole), tok, pl.BlockSpec((1, d), whole), tok],
        out_specs=[tok, tok, pl.BlockSpec((1, d), whole), pl.BlockSpec((1, LANES), whole)],
        out_shape=[jax.ShapeDtypeStruct((m, d), F32), jax.ShapeDtypeStruct((m, d), BF),
                   jax.ShapeDtypeStruct((1, d), F32), jax.ShapeDtypeStruct((1, LANES), F32)],
        compiler_params=_params(("arbitrary",)),
    )(act, w, resid, g, target)


def _d_act_swiglu(dx, w, gate, up, *, tm=256, rider=None):
    m, d = dx.shape
    f = w.shape[0]
    tm = _tile(m, tm)
    row = lambda i: (i, 0)

    def body(*refs):
        (dx_ref, w_ref, gate_ref, up_ref), (o_ref,) = _host_refs(refs, rider, 4, 1)
        if rider:
            rider.run(refs, 4, 1, pl.program_id(0) == 0, pl.program_id(0) == m // tm - 1)
        da = lax.dot_general(dx_ref[...], w_ref[...], NT, preferred_element_type=F32)
        gv, uv = gate_ref[...].astype(F32), up_ref[...].astype(F32)
        sg = jax.nn.sigmoid(gv)
        dgate = da * uv * (sg * (1.0 + gv * (1.0 - sg)))
        o_ref[...] = jnp.concatenate([dgate, da * (gv * sg)], axis=-1).astype(o_ref.dtype)

    return pl.pallas_call(
        body, name="d_act_swiglu", grid=(m // tm,),
        compiler_params=_params(("arbitrary",) if rider else ("parallel",), has_side_effects=rider is not None),
        **_with_rider(
            rider, 4, 1,
            [pl.BlockSpec((tm, d), row), pl.BlockSpec((f, d), lambda i: (0, 0)),
             pl.BlockSpec((tm, f), row), pl.BlockSpec((tm, f), row)],
            [pl.BlockSpec((tm, 2 * f), row)], [jax.ShapeDtypeStruct((m, 2 * f), BF)]),
    )(dx, w, gate, up, *(rider.arrays if rider else []))


GATE_A_BLK, GATE_B_BLK = 3, 4


def _branches_merge(o_a, y_b, w_a, w_b, proj, *, tm=1024):
    m, kdim = o_a.shape
    d = w_a.shape[1]
    tm = _tile(m, tm)
    row = lambda i: (i, 0)

    def body(a_ref, b_ref, wa_ref, wb_ref, ga_ref, gb_ref, bra_ref, brb_ref, merged_ref):
        bra = jnp.dot(a_ref[...], wa_ref[...], preferred_element_type=F32)
        brb = jnp.dot(b_ref[...], wb_ref[...], preferred_element_type=F32)
        bra_ref[...] = bra.astype(bra_ref.dtype)
        brb_ref[...] = brb.astype(brb_ref.dtype)
        merged = jax.nn.sigmoid(ga_ref[...].astype(F32)) * bra + jax.nn.sigmoid(gb_ref[...].astype(F32)) * brb
        merged_ref[...] = merged.astype(merged_ref.dtype)

    tok = pl.BlockSpec((tm, d), row)
    return pl.pallas_call(
        body, name="branches_merge", grid=(m // tm,),
        in_specs=[pl.BlockSpec((tm, kdim), row), pl.BlockSpec((tm, kdim), row),
                  pl.BlockSpec((kdim, d), lambda i: (0, 0)), pl.BlockSpec((kdim, d), lambda i: (0, 0)),
                  pl.BlockSpec((tm, d), lambda i: (i, GATE_A_BLK)), pl.BlockSpec((tm, d), lambda i: (i, GATE_B_BLK))],
        out_specs=[tok, tok, tok], out_shape=[jax.ShapeDtypeStruct((m, d), BF)] * 3,
        compiler_params=_params(("parallel",)),
    )(o_a, y_b, w_a, w_b, proj, proj)


def _d_merged_gates(dx, w_mix, proj, br_a, br_b, *, tm=512, rider=None):
    m, d = dx.shape
    tm = _tile(m, tm)
    row = lambda i: (i, 0)

    def body(*refs):
        ins, (dbra_ref, dbrb_ref, dga_ref, dgb_ref) = _host_refs(refs, rider, 6, 4)
        dx_ref, w_ref, ga_ref, gb_ref, bra_ref, brb_ref = ins
        if rider:
            rider.run(refs, 6, 4, pl.program_id(0) == 0, pl.program_id(0) == m // tm - 1)
        dm = lax.dot_general(dx_ref[...], w_ref[...], NT, preferred_element_type=F32)
        sa, sb = jax.nn.sigmoid(ga_ref[...].astype(F32)), jax.nn.sigmoid(gb_ref[...].astype(F32))
        dbra_ref[...] = (dm * sa).astype(dbra_ref.dtype)
        dbrb_ref[...] = (dm * sb).astype(dbrb_ref.dtype)
        dga_ref[...] = (dm * bra_ref[...].astype(F32) * (sa * (1.0 - sa))).astype(dga_ref.dtype)
        dgb_ref[...] = (dm * brb_ref[...].astype(F32) * (sb * (1.0 - sb))).astype(dgb_ref.dtype)

    tok = pl.BlockSpec((tm, d), row)
    return pl.pallas_call(
        body, name="d_merged_gates", grid=(m // tm,),
        compiler_params=_params(("arbitrary",) if rider else ("parallel",), has_side_effects=rider is not None),
        **_with_rider(
            rider, 6, 4,
            [tok, pl.BlockSpec((d, d), lambda i: (0, 0)),
             pl.BlockSpec((tm, d), lambda i: (i, GATE_A_BLK)), pl.BlockSpec((tm, d), lambda i: (i, GATE_B_BLK)),
             tok, tok],
            [tok] * 4, [jax.ShapeDtypeStruct((m, d), BF)] * 4),
    )(dx, w_mix, proj, proj, br_a, br_b, *(rider.arrays if rider else []))


SB_TK = 128
SB_KT = 2


def _sb_consts(tq):
    tk = SB_TK
    diff = lax.broadcasted_iota(jnp.int32, (tq, tk), 1) - lax.broadcasted_iota(jnp.int32, (tq, tk), 0)
    rj = lax.broadcasted_iota(jnp.int32, (2 * tk, 2 * tk), 0) & (tk - 1)
    cj = lax.broadcasted_iota(jnp.int32, (2 * tk, 2 * tk), 1)
    ones_half = cj >= tk
    later = jnp.where((rj > cj) | ones_half, 1.0, 0.0).astype(BF)
    later_incl = jnp.where((rj >= cj) | ones_half, 1.0, 0.0).astype(BF)
    return diff, later, later_incl


def _split_dot(val, rhs_twice):
    hi = val.astype(BF)
    lo = (val - hi.astype(F32)).astype(BF)
    return jnp.dot(jnp.concatenate([hi, lo], axis=1), rhs_twice, preferred_element_type=F32)


def _log_terms(z):
    sp = jnp.maximum(z, 0.0) + jnp.log(1.0 + jnp.exp(-jnp.abs(z)))
    return z - sp, sp


NT = (((1,), (1,)), ((), ()))
TN = (((0,), (0,)), ((), ()))


def _head_lane_masks(rows):
    lane = lax.broadcasted_iota(jnp.int32, (rows, LANES), 1)
    first = jnp.where(lane < SB_HEAD_DIM, 1.0, 0.0)
    return first.astype(BF), (1.0 - first).astype(BF)


DEAD_LOG = 104.0


def _walk_back(i, step, state, carries_of):
    def alive(st):
        c0, c1 = carries_of(st)
        return jnp.min(jnp.minimum(c0, c1)) < DEAD_LOG

    def cond(loop):
        done, live, _ = loop
        return jnp.logical_and(done < i, live)

    def body(loop):
        done, _, st = loop
        st = step(i - 1 - done, st)
        return done + 1, alive(st), st

    return lax.while_loop(cond, body, (jnp.int32(0), alive(state), state))[2]


def _tail(a, r0):
    return a if r0 == 0 else a[r0:]


def _add_tail(a, r0, delta):
    return a + delta if r0 == 0 else jnp.concatenate([a[:r0], a[r0:] + delta], axis=0)


def _both_heads(tile, masks):
    return jnp.concatenate([tile * masks[0], tile * masks[1]], axis=0)


def _with_rider(rider, n_in, n_out, in_specs, out_specs, out_shape, scratch=()):
    kw = dict(in_specs=list(in_specs), out_specs=list(out_specs), out_shape=list(out_shape),
              scratch_shapes=list(scratch), input_output_aliases={})
    if rider:
        extra = rider.call_args(n_in, n_out)
        kw["in_specs"] += extra["in_specs"]
        kw["out_specs"] += extra["out_specs"]
        kw["out_shape"] += extra["out_shape"]
        kw["scratch_shapes"] += extra["scratch"]
        kw["input_output_aliases"] = extra["aliases"]
    return kw


def _sb_fwd(proj, rider=None):
    s = proj.shape[0]
    tk, tq = SB_TK, SB_KT * SB_TK
    n_pairs = 4
    scale = 1.0 / math.sqrt(SB_HEAD_DIM)

    def body(*refs):
        (q_ref, k_ref, v_ref), (o_ref, o32_ref) = _host_refs(refs, rider, 3, 2)
        i = pl.program_id(1)
        diff, later, _ = _sb_consts(tq)
        qs = (q_ref[...].astype(F32) * scale).astype(BF)
        lane_masks = _head_lane_masks(tk)

        def step(g, state, masked):
            tiles = list(reversed(range(SB_KT)))
            chains = [(t, h) for t in tiles for h in range(2)]
            rows = {t: pl.ds(pl.multiple_of((g * SB_KT + t) * tk, tk), tk) for t in tiles}
            ks = {t: _both_heads(k_ref[rows[t], :], lane_masks) for t in tiles}
            vs = {t: _both_heads(v_ref[rows[t], :], lane_masks) for t in tiles}
            r0 = {t: t * tk if masked else 0 for t in tiles}
            allowed = {t: _tail(diff, r0[t]) < -t * tk for t in tiles}
            zs = {t: lax.dot_general(_tail(qs, r0[t]), ks[t], NT, preferred_element_type=F32) for t in tiles}
            logs = {}
            for t, h in chains:
                log_b, sp = _log_terms(zs[t][:, h * tk:(h + 1) * tk])
                logs[t, h] = (log_b, jnp.where(allowed[t], sp, 0.0) if masked else sp)
            sums = {c: _split_dot(logs[c][1], later) for c in chains}
            carries = list(state[0])
            ws = {}
            for t, h in chains:
                w = jnp.exp(logs[t, h][0] - (sums[t, h][:, :tk] + _tail(carries[h], r0[t])))
                ws[t, h] = (jnp.where(allowed[t], w, 0.0) if masked else w).astype(BF)
                carries[h] = _add_tail(carries[h], r0[t], sums[t, h][:, tk:])
            acc = state[1]
            for t in tiles:
                acc = _add_tail(acc, r0[t], jnp.dot(jnp.concatenate([ws[t, 0], ws[t, 1]], axis=1), vs[t],
                                                    preferred_element_type=F32))
            return tuple(carries), acc

        zero = jnp.zeros((tq, LANES), F32)
        state = step(i, ((zero, zero), zero), True)
        state = _walk_back(i, lambda g, st: step(g, st, False), state, lambda st: st[0])
        o_ref[...] = state[1].astype(o_ref.dtype)
        o32_ref[...] = state[1]
        if rider:
            rider.run(refs, 3, 2, (pl.program_id(0) == 0) & (i == 0),
                      (pl.program_id(0) == n_pairs - 1) & (i == s // tq - 1))

    tok = pl.BlockSpec((tq, LANES), lambda p, i: (i, p))
    return pl.pallas_call(
        body, name="sb_attn_fwd", grid=(n_pairs, s // tq),
        compiler_params=_params(("arbitrary", "arbitrary"), has_side_effects=rider is not None),
        **_with_rider(
            rider, 3, 2,
            [tok, pl.BlockSpec((s, LANES), lambda p, i: (0, n_pairs + p)),
             pl.BlockSpec((s, LANES), lambda p, i: (0, 2 * n_pairs + p))],
            [tok, tok],
            [jax.ShapeDtypeStruct((s, n_pairs * LANES), BF), jax.ShapeDtypeStruct((s, n_pairs * LANES), F32)]),
    )(proj, proj, proj, *(rider.arrays if rider else []))


def _sb_bwd(proj, o32, do_a, rider=None):
    s = proj.shape[0]
    tk, tq = SB_TK, SB_KT * SB_TK
    n_pairs = 4
    scale = 1.0 / math.sqrt(SB_HEAD_DIM)

    def body(*refs):
        (q_ref, k_ref, v_ref, o_ref, do_ref), (dq_ref, dk_ref, dv_ref) = _host_refs(refs, rider, 5, 3)
        i = pl.program_id(1)

        @pl.when(i == 0)
        def _():
            dk_ref[...] = jnp.zeros_like(dk_ref)
            dv_ref[...] = jnp.zeros_like(dv_ref)

        diff, later, later_incl = _sb_consts(tq)
        qs = (q_ref[...].astype(F32) * scale).astype(BF)
        do2 = do_ref[...]
        prod = do2.astype(F32) * o_ref[...]
        lane_masks = _head_lane_masks(tk)
        first_head = lax.broadcasted_iota(jnp.int32, (tq, LANES), 1) < SB_HEAD_DIM
        totals = [jnp.broadcast_to(jnp.sum(jnp.where(keep, prod, 0.0), axis=-1, keepdims=True), (tq, tk))
                  for keep in (first_head, jnp.logical_not(first_head))]
        first_head_k = first_head[:tk]

        def step(g_idx, state, masked):
            tiles = list(reversed(range(SB_KT)))
            chains = [(t, h) for t in tiles for h in range(2)]
            rows = {t: pl.ds(pl.multiple_of((g_idx * SB_KT + t) * tk, tk), tk) for t in tiles}
            ks = {t: _both_heads(k_ref[rows[t], :], lane_masks) for t in tiles}
            vs = {t: _both_heads(v_ref[rows[t], :], lane_masks) for t in tiles}
            r0 = {t: t * tk if masked else 0 for t in tiles}
            allowed = {t: _tail(diff, r0[t]) < -t * tk for t in tiles}
            zs = {t: lax.dot_general(_tail(qs, r0[t]), ks[t], NT, preferred_element_type=F32) for t in tiles}
            dws = {t: lax.dot_general(_tail(do2, r0[t]), vs[t], NT, preferred_element_type=F32) for t in tiles}
            logs = {}
            for t, h in chains:
                log_b, sp = _log_terms(zs[t][:, h * tk:(h + 1) * tk])
                logs[t, h] = (log_b, jnp.where(allowed[t], sp, 0.0) if masked else sp)
            sums = {c: _split_dot(logs[c][1], later) for c in chains}
            c_log, c_g = list(state[0]), list(state[1])
            ws, gs = {}, {}
            for t, h in chains:
                w = jnp.exp(logs[t, h][0] - (sums[t, h][:, :tk] + _tail(c_log[h], r0[t])))
                ws[t, h] = (jnp.where(allowed[t], w, 0.0) if masked else w).astype(BF)
                c_log[h] = _add_tail(c_log[h], r0[t], sums[t, h][:, tk:])
                gs[t, h] = ws[t, h].astype(F32) * dws[t][:, h * tk:(h + 1) * tk]
            gsums = {c: _split_dot(gs[c], later_incl) for c in chains}
            dzs = {}
            for t, h in chains:
                beta = jnp.exp(logs[t, h][0])
                earlier = _tail(totals[h], r0[t]) - (gsums[t, h][:, :tk] + _tail(c_g[h], r0[t]))
                dz = gs[t, h] * (1.0 - beta) - earlier * beta
                dzs[t, h] = (jnp.where(allowed[t], dz, 0.0) if masked else dz).astype(BF)
                c_g[h] = _add_tail(c_g[h], r0[t], gsums[t, h][:, tk:])
            dq = state[2]
            for t in tiles:
                dz_both = jnp.concatenate([dzs[t, 0], dzs[t, 1]], axis=1)
                w_both = jnp.concatenate([ws[t, 0], ws[t, 1]], axis=1)
                dq = _add_tail(dq, r0[t], jnp.dot(dz_both, ks[t], preferred_element_type=F32))
                dk2 = lax.dot_general(dz_both, _tail(qs, r0[t]), TN, preferred_element_type=F32)
                dv2 = lax.dot_general(w_both, _tail(do2, r0[t]), TN, preferred_element_type=F32)
                dk_ref[rows[t], :] += jnp.where(first_head_k, dk2[:tk], dk2[tk:])
                dv_ref[rows[t], :] += jnp.where(first_head_k, dv2[:tk], dv2[tk:])
            return tuple(c_log), tuple(c_g), dq

        zero = jnp.zeros((tq, LANES), F32)
        state = step(i, ((zero, zero), (zero, zero), zero), True)
        state = _walk_back(i, lambda g, st: step(g, st, False), state, lambda st: st[0])
        dq_ref[...] = (state[2] * scale).astype(dq_ref.dtype)
        if rider:
            rider.run(refs, 5, 3, (pl.program_id(0) == 0) & (i == 0),
                      (pl.program_id(0) == n_pairs - 1) & (i == s // tq - 1))

    width = n_pairs * LANES
    tok = pl.BlockSpec((tq, LANES), lambda p, i: (i, p))
    return pl.pallas_call(
        body, name="sb_attn_bwd", grid=(n_pairs, s // tq),
        compiler_params=_params(("arbitrary", "arbitrary"), has_side_effects=rider is not None),
        **_with_rider(
            rider, 5, 3,
            [tok, pl.BlockSpec((s, LANES), lambda p, i: (0, n_pairs + p)),
             pl.BlockSpec((s, LANES), lambda p, i: (0, 2 * n_pairs + p)), tok, tok],
            [tok, pl.BlockSpec((s, LANES), lambda p, i: (0, p)), pl.BlockSpec((s, LANES), lambda p, i: (0, p))],
            [jax.ShapeDtypeStruct((s, width), BF), jax.ShapeDtypeStruct((s, width), F32),
             jax.ShapeDtypeStruct((s, width), F32)]),
    )(proj, proj, proj, o32, do_a, *(rider.arrays if rider else []))


CONV_COL0 = 12


def _shift_rows(v, k):
    n = v.shape[0]
    row = lax.broadcasted_iota(jnp.int32, v.shape, 0)
    rolled = pltpu.roll(v, k % n, axis=0)
    keep = row >= k if k > 0 else row < n + k
    return jnp.where(keep, rolled, 0.0)


def _conv_specs(s):
    return [pl.BlockSpec((s, LANES), lambda cb: (0, CONV_COL0 + cb)),
            pl.BlockSpec((s, LANES), lambda cb: (0, CONV_COL0 + 4 + cb)),
            pl.BlockSpec((s, LANES), lambda cb: (0, CONV_COL0 + 8 + cb)),
            pl.BlockSpec((CONV_ROWS, LANES), lambda cb: (0, cb))]


def _conv_fwd(proj, conv_w):
    s = proj.shape[0]

    def body(u_ref, gb_ref, gc_ref, w_ref, y_ref):
        cu = gc_ref[...].astype(F32) * u_ref[...].astype(F32)
        w = w_ref[...]
        y = w[0:1] * _shift_rows(cu, 2) + w[1:2] * _shift_rows(cu, 1) + w[2:3] * cu
        y_ref[...] = (gb_ref[...].astype(F32) * y).astype(y_ref.dtype)

    return pl.pallas_call(
        body, name="conv_fwd", grid=(4,), in_specs=_conv_specs(s),
        out_specs=pl.BlockSpec((s, LANES), lambda cb: (0, cb)),
        out_shape=jax.ShapeDtypeStruct((s, 4 * LANES), BF),
        compiler_params=_params(("parallel",)),
    )(proj, proj, proj, conv_w)


def _conv_bwd(proj, conv_w, dy, rider=None):
    s = proj.shape[0]
    n_blocks = 4

    def body(*refs):
        (u_ref, gb_ref, gc_ref, w_ref, dy_ref), (du_ref, dgb_ref, dgc_ref, dw_ref) = _host_refs(refs, rider, 5, 4)
        u, gc = u_ref[...].astype(F32), gc_ref[...].astype(F32)
        dyv = dy_ref[...].astype(F32)
        w = w_ref[...]
        cu = gc * u
        cu1, cu2 = _shift_rows(cu, 1), _shift_rows(cu, 2)
        conv = w[0:1] * cu2 + w[1:2] * cu1 + w[2:3] * cu
        dgb_ref[...] = (dyv * conv).astype(dgb_ref.dtype)
        dc = dyv * gb_ref[...].astype(F32)
        dcu = w[2:3] * dc + w[1:2] * _shift_rows(dc, -1) + w[0:1] * _shift_rows(dc, -2)
        dgc_ref[...] = (dcu * u).astype(dgc_ref.dtype)
        du_ref[...] = (dcu * gc).astype(du_ref.dtype)
        tap_row = lax.broadcasted_iota(jnp.int32, (CONV_ROWS, LANES), 0)
        dw = jnp.zeros((CONV_ROWS, LANES), F32)
        for t, shifted in enumerate((cu2, cu1, cu)):
            dw = jnp.where(tap_row == t, jnp.sum(dc * shifted, axis=0, keepdims=True), dw)
        dw_ref[...] = dw
        if rider:
            rider.run(refs, 5, 4, pl.program_id(0) == 0, pl.program_id(0) == n_blocks - 1)

    col = pl.BlockSpec((s, LANES), lambda cb: (0, cb))
    act = jax.ShapeDtypeStruct((s, 4 * LANES), BF)
    return pl.pallas_call(
        body, name="conv_bwd", grid=(n_blocks,),
        compiler_params=_params(("arbitrary",) if rider else ("parallel",), has_side_effects=rider is not None),
        **_with_rider(
            rider, 5, 4, _conv_specs(s) + [col],
            [col, col, col, pl.BlockSpec((CONV_ROWS, LANES), lambda cb: (0, cb))],
            [act, act, act, jax.ShapeDtypeStruct((CONV_ROWS, n_blocks * LANES), F32)]),
    )(proj, proj, proj, conv_w, dy, *(rider.arrays if rider else []))


def _mem_probs(q, k, scale):
    sc = lax.dot_general(q, k, NT, preferred_element_type=F32) * scale
    p = jnp.exp(sc - jnp.max(sc, axis=-1, keepdims=True))
    return p / jnp.sum(p, axis=-1, keepdims=True)


def _mem_fwd(q_m, kv, tq=2048):
    s, d = q_m.shape
    mlen = kv.shape[0]
    hd = d // MEM_HEADS
    tq = _tile(s, tq)
    scale = 1.0 / math.sqrt(hd)

    def body(q_ref, k_ref, v_ref, o_ref):
        p = _mem_probs(q_ref[...], k_ref[...], scale)
        o_ref[...] = jnp.dot(p.astype(BF), v_ref[...], preferred_element_type=F32).astype(o_ref.dtype)

    return pl.pallas_call(
        body, name="mem_attn_fwd", grid=(MEM_HEADS, s // tq),
        in_specs=[pl.BlockSpec((tq, hd), lambda h, i: (i, h)),
                  pl.BlockSpec((mlen, hd), lambda h, i: (0, h)),
                  pl.BlockSpec((mlen, hd), lambda h, i: (0, MEM_HEADS + h))],
        out_specs=pl.BlockSpec((tq, hd), lambda h, i: (i, h)),
        out_shape=jax.ShapeDtypeStruct((s, d), BF),
        compiler_params=_params(("parallel", "parallel")),
    )(q_m, kv, kv)


def _mem_bwd(q_m, kv, do_m, tq=2048):
    s, d = q_m.shape
    mlen = kv.shape[0]
    hd = d // MEM_HEADS
    tq = _tile(s, tq)
    scale = 1.0 / math.sqrt(hd)

    def body(q_ref, k_ref, v_ref, do_ref, dq_ref, dk_ref, dv_ref):
        q, k, v, do = q_ref[...], k_ref[...], v_ref[...], do_ref[...]
        p = _mem_probs(q, k, scale)
        dp = lax.dot_general(do, v, NT, preferred_element_type=F32)
        ds = p * (dp - jnp.sum(dp * p, axis=-1, keepdims=True)) * scale
        dsb = ds.astype(BF)
        dq_ref[...] = jnp.dot(dsb, k, preferred_element_type=F32).astype(dq_ref.dtype)
        dk = lax.dot_general(dsb, q, TN, preferred_element_type=F32)
        dv = lax.dot_general(p.astype(BF), do, TN, preferred_element_type=F32)
        first = pl.program_id(1) == 0

        @pl.when(first)
        def _():
            dk_ref[...] = dk
            dv_ref[...] = dv

        @pl.when(jnp.logical_not(first))
        def _():
            dk_ref[...] += dk
            dv_ref[...] += dv

    tok = pl.BlockSpec((tq, hd), lambda h, i: (i, h))
    memb = pl.BlockSpec((mlen, hd), lambda h, i: (0, h))
    return pl.pallas_call(
        body, name="mem_attn_bwd", grid=(MEM_HEADS, s // tq),
        in_specs=[tok, memb, pl.BlockSpec((mlen, hd), lambda h, i: (0, MEM_HEADS + h)), tok],
        out_specs=[tok, memb, memb],
        out_shape=[jax.ShapeDtypeStruct((s, d), BF), jax.ShapeDtypeStruct((mlen, d), F32),
                   jax.ShapeDtypeStruct((mlen, d), F32)],
        compiler_params=_params(("parallel", "arbitrary")),
    )(q_m, kv, kv, do_m)


def _place():
    x, y, c = lax.axis_index("x"), lax.axis_index("y"), lax.axis_index("c")
    other_chips = [(1 - x, y), (x, 1 - y), (1 - x, 1 - y)]
    return x, y, c, other_chips


def _chip_no(cx, cy):
    return 2 * cx + cy


HBM_SPEC = pl.BlockSpec(memory_space=pl.ANY)


def _cast_place(shard, axis, place, dtype, name):
    r, c = shard.shape
    tr = _tile(r, max(16, STREAM_BLOCK_ELEMS // c))
    nblk = r // tr
    if axis == 1:
        full, out_map = (r, N_CHIPS * c), lambda i, pref: (i, pref[0])
    else:
        full, out_map = (N_CHIPS * r, c), lambda i, pref: (pref[0] * nblk + i, 0)

    def body(pref, s_ref, o_ref):
        o_ref[...] = s_ref[...].astype(o_ref.dtype)

    return pl.pallas_call(
        body, name=name,
        grid_spec=pltpu.PrefetchScalarGridSpec(
            num_scalar_prefetch=1, grid=(nblk,),
            in_specs=[pl.BlockSpec((tr, c), lambda i, pref: (i, 0))],
            out_specs=pl.BlockSpec((tr, c), out_map)),
        out_shape=jax.ShapeDtypeStruct(full, dtype),
        compiler_params=_params(("parallel",)),
    )(place, shard)


def _region(ref, axis, chip_no, half):
    width = ref.shape[axis] // N_CHIPS
    start = pl.multiple_of(chip_no * width, width)
    if axis == 1:
        if half is None:
            return ref.at[:, pl.ds(start, width)]
        hr = ref.shape[0] // 2
        return ref.at[pl.ds(pl.multiple_of(half * hr, hr), hr), pl.ds(start, width)]
    if half is None:
        return ref.at[pl.ds(start, width), :]
    hr = width // 2
    return ref.at[pl.ds(pl.multiple_of(start + half * hr, hr), hr), :]


def _gather_weights(fulls, axes, split):
    n = len(fulls)

    def body(*refs):
        outs = refs[n:2 * n]
        send, recv, fsend, frecv = refs[2 * n:]
        x, y, c, others = _place()
        me = _chip_no(x, y)
        sibling = (x, y, 1 - c)

        def copy(w, chip_no, half, sems, p, to):
            reg = _region(outs[w], axes[w], chip_no, half)
            return pltpu.make_async_remote_copy(
                src_ref=reg, dst_ref=reg, send_sem=sems[0].at[w, p], recv_sem=sems[1].at[w, p],
                device_id=to, device_id_type=MESH)

        for w in range(n):
            for p, chip in enumerate(others):
                copy(w, me, c if split[w] else None, (send, recv), p, (chip[0], chip[1], c)).start()
        for w in range(n):
            for p, chip in enumerate(others):
                half = c if split[w] else None
                copy(w, _chip_no(*chip), half, (send, recv), p, (chip[0], chip[1], c)).wait_recv()
                if split[w]:
                    copy(w, _chip_no(*chip), c, (fsend, frecv), p, sibling).start()
        for w in range(n):
            for p, chip in enumerate(others):
                copy(w, me, c if split[w] else None, (send, recv), p, (chip[0], chip[1], c)).wait_send()
                if split[w]:
                    handed = copy(w, _chip_no(*chip), 1 - c, (fsend, frecv), p, sibling)
                    handed.wait_recv()
                    handed.wait_send()

    return pl.pallas_call(
        body, name="gather_weights",
        in_specs=[HBM_SPEC] * n, out_specs=[HBM_SPEC] * n,
        out_shape=[jax.ShapeDtypeStruct(f.shape, f.dtype) for f in fulls],
        input_output_aliases={i: i for i in range(n)},
        scratch_shapes=[pltpu.SemaphoreType.DMA((n, 3))] * 4,
        compiler_params=pltpu.CompilerParams(has_side_effects=True),
    )(*fulls)


def _fetch_copy(refs, axes, whole, send, recv, w, p, chip, c, arriving):
    owner = _chip_no(*chip) if arriving else _chip_no(lax.axis_index("x"), lax.axis_index("y"))
    reg = _region(refs[w], axes[w], owner, None if whole[w] else c)
    return pltpu.make_async_remote_copy(
        src_ref=reg, dst_ref=reg, send_sem=send[p], recv_sem=recv[p],
        device_id=(chip[0], chip[1], c), device_id_type=MESH)


N_PEERS = N_CHIPS - 1


class _Rider:
    def __init__(self, arrays, outs, aliases, start, wait):
        self.arrays, self.outs, self.aliases, self.start, self.wait = list(arrays), list(outs), aliases, start, wait
        self.scratch = [pltpu.SemaphoreType.DMA((len(self.arrays), N_PEERS))] * 2

    def run(self, refs, n_in, n_out, first, last):
        ra, ro = len(self.arrays), len(self.outs)
        ins = refs[n_in:n_in + ra]
        outs = refs[n_in + ra + n_out:n_in + ra + n_out + ro]
        send, recv = refs[-2], refs[-1]

        @pl.when(first)
        def _():
            self.start(ins, outs, send, recv)

        @pl.when(last)
        def _():
            self.wait(ins, outs, send, recv)

    def call_args(self, n_in, n_out):
        ra = len(self.arrays)
        return dict(in_specs=[HBM_SPEC] * ra, out_specs=[HBM_SPEC] * len(self.outs), out_shape=self.outs,
                    aliases={n_in + k: n_out + o for k, o in self.aliases.items()}, scratch=self.scratch)


def _host_refs(refs, rider, n_in, n_out):
    ra = len(rider.arrays) if rider else 0
    return refs[:n_in], refs[n_in + ra:n_in + ra + n_out]


def _riding_fetch(fulls, axes):
    n = len(fulls)
    whole = [False] * n

    def sems(ref, w):
        return [ref.at[w, q] for q in range(N_PEERS)]

    def start(ins, outs, send, recv):
        _, _, c, others = _place()
        for w in range(n):
            for p, chip in enumerate(others):
                _fetch_copy(outs, axes, whole, sems(send, w), sems(recv, w), w, p, chip, c, False).start()

    def wait(ins, outs, send, recv):
        _, _, c, others = _place()
        for w in range(n):
            for p, chip in enumerate(others):
                _fetch_copy(outs, axes, whole, sems(send, w), sems(recv, w), w, p, chip, c, False).wait_send()
                _fetch_copy(outs, axes, whole, sems(send, w), sems(recv, w), w, p, chip, c, True).wait_recv()

    return _Rider(fulls, [jax.ShapeDtypeStruct(f.shape, f.dtype) for f in fulls], {k: k for k in range(n)}, start, wait)


def _hand_on(fulls, axes, name):
    n = len(fulls)

    def body(*refs):
        outs = refs[n:2 * n]
        send, recv = refs[2 * n:]
        x, y, c, others = _place()

        def copy(w, p, chip, half):
            reg = _region(outs[w], axes[w], _chip_no(*chip), half)
            return pltpu.make_async_remote_copy(
                src_ref=reg, dst_ref=reg, send_sem=send.at[w, p], recv_sem=recv.at[w, p],
                device_id=(x, y, 1 - c), device_id_type=MESH)

        for w in range(n):
            for p, chip in enumerate(others):
                copy(w, p, chip, c).start()
        for w in range(n):
            for p, chip in enumerate(others):
                copy(w, p, chip, 1 - c).wait()

    return pl.pallas_call(
        body, name=name,
        in_specs=[HBM_SPEC] * n, out_specs=[HBM_SPEC] * n,
        out_shape=[jax.ShapeDtypeStruct(f.shape, f.dtype) for f in fulls],
        input_output_aliases={i: i for i in range(n)},
        scratch_shapes=[pltpu.SemaphoreType.DMA((n, 3)), pltpu.SemaphoreType.DMA((n, 3))],
        compiler_params=pltpu.CompilerParams(has_side_effects=True),
    )(*fulls)


def _pair_exchange(grads, name):
    n = len(grads)

    def body(*refs):
        ins, outs = refs[:n], refs[n:2 * n]
        send, recv = refs[2 * n:]
        x, y, c, _ = _place()
        cps = []
        for w in range(n):
            cp = pltpu.make_async_remote_copy(
                src_ref=ins[w].at[:, 1 - c], dst_ref=outs[w], send_sem=send.at[w], recv_sem=recv.at[w],
                device_id=(x, y, 1 - c), device_id_type=MESH)
            cp.start()
            cps.append(cp)
        for cp in cps:
            cp.wait()

    return pl.pallas_call(
        body, name=name,
        in_specs=[HBM_SPEC] * n, out_specs=[HBM_SPEC] * n,
        out_shape=[jax.ShapeDtypeStruct((g.shape[0],) + g.shape[2:], g.dtype) for g in grads],
        scratch_shapes=[pltpu.SemaphoreType.DMA((n,)), pltpu.SemaphoreType.DMA((n,))],
        compiler_params=pltpu.CompilerParams(has_side_effects=True),
    )(*grads)


def _pair_add(g4, got, core, name):
    nj, _, hr, cdim = g4.shape
    tr = _tile(hr, max(8, STREAM_BLOCK_ELEMS // 2 // cdim))

    def body(core_ref, a_ref, b_ref, o_ref):
        o_ref[...] = (a_ref[...].astype(F32) + b_ref[...].astype(F32)).astype(o_ref.dtype)

    return pl.pallas_call(
        body, name=name,
        grid_spec=pltpu.PrefetchScalarGridSpec(
            num_scalar_prefetch=1, grid=(nj, hr // tr),
            in_specs=[pl.BlockSpec((1, None, tr, cdim), lambda j, i, core_ref: (j, core_ref[0], i, 0)),
                      pl.BlockSpec((1, tr, cdim), lambda j, i, core_ref: (j, i, 0))],
            out_specs=pl.BlockSpec((1, tr, cdim), lambda j, i, core_ref: (j, i, 0))),
        out_shape=jax.ShapeDtypeStruct((nj, hr, cdim), BF),
        compiler_params=_params(("parallel", "parallel")),
    )(core, g4, got)


def _piece(ref, axis, j, hc):
    if axis == 0:
        return ref.at[j]
    return ref.at[0, :, pl.ds(pl.multiple_of(j * hc, hc), hc)]


def _slot_shapes(sums, axes):
    return [(N_CHIPS - 1, sm.shape[1], sm.shape[2] // (1 if ax == 0 else N_CHIPS)) for sm, ax in zip(sums, axes)]


def _slot_copy(sums, lands, axes, send, recv, w, p, chip, c):
    return pltpu.make_async_remote_copy(
        src_ref=_piece(sums[w], axes[w], _chip_no(*chip), lands[w].shape[2]), dst_ref=lands[w].at[p],
        send_sem=send[p], recv_sem=recv[p],
        device_id=(chip[0], chip[1], c), device_id_type=MESH)


def _riding_pairs(views):
    n = len(views)

    def copies(ins, outs, send, recv):
        x, y, c, _ = _place()
        return [pltpu.make_async_remote_copy(
            src_ref=ins[w].at[:, 1 - c], dst_ref=outs[w], send_sem=send.at[w, 0], recv_sem=recv.at[w, 0],
            device_id=(x, y, 1 - c), device_id_type=MESH) for w in range(n)]

    def start(ins, outs, send, recv):
        for cp in copies(ins, outs, send, recv):
            cp.start()

    def wait(ins, outs, send, recv):
        for cp in copies(ins, outs, send, recv):
            cp.wait()

    outs = [jax.ShapeDtypeStruct((g.shape[0],) + g.shape[2:], g.dtype) for g in views]
    return _Rider(views, outs, {}, start, wait)


def _riding_slots(sums, axes):
    n = len(sums)
    shapes = _slot_shapes(sums, axes)

    def copies(ins, outs, send, recv):
        _, _, c, others = _place()
        return [_slot_copy(ins, outs, axes, [send.at[w, q] for q in range(N_PEERS)],
                           [recv.at[w, q] for q in range(N_PEERS)], w, p, chip, c)
                for w in range(n) for p, chip in enumerate(others)]

    def start(ins, outs, send, recv):
        for cp in copies(ins, outs, send, recv):
            cp.start()

    def wait(ins, outs, send, recv):
        for cp in copies(ins, outs, send, recv):
            cp.wait()

    return _Rider(sums, [jax.ShapeDtypeStruct(sh, sm.dtype) for sh, sm in zip(shapes, sums)], {}, start, wait)


def _chip_sum(psum, slots, axis, place, name):
    _, hr, hc = slots.shape
    tr = _tile(hr, 256)
    own_map = (lambda i, pref: (0, i, pref[0])) if axis == 1 else (lambda i, pref: (pref[0], i, 0))

    def body(pref, own_ref, s_ref, o_ref):
        o_ref[...] = ((own_ref[...].astype(F32) + s_ref[0].astype(F32)) + s_ref[1].astype(F32)) + s_ref[2].astype(F32)

    return pl.pallas_call(
        body, name=name,
        grid_spec=pltpu.PrefetchScalarGridSpec(
            num_scalar_prefetch=1, grid=(hr // tr,),
            in_specs=[pl.BlockSpec((None, tr, hc), own_map),
                      pl.BlockSpec((N_CHIPS - 1, tr, hc), lambda i, pref: (0, i, 0))],
            out_specs=pl.BlockSpec((None, tr, hc), lambda i, pref: (pref[1], i, 0))),
        out_shape=jax.ShapeDtypeStruct((2, hr, hc), F32),
        compiler_params=_params(("parallel",)),
    )(place, psum, slots)


def _half_swap(both):
    n = len(both)

    def body(*refs):
        outs = refs[n:2 * n]
        send, recv = refs[2 * n:]
        x, y, c, _ = _place()

        def copy(w, half):
            return pltpu.make_async_remote_copy(
                src_ref=outs[w].at[half], dst_ref=outs[w].at[half], send_sem=send.at[w], recv_sem=recv.at[w],
                device_id=(x, y, 1 - c), device_id_type=MESH)

        for w in range(n):
            copy(w, c).start()
        for w in range(n):
            copy(w, 1 - c).wait()

    return pl.pallas_call(
        body, name="grad_half_swap",
        in_specs=[HBM_SPEC] * n, out_specs=[HBM_SPEC] * n,
        out_shape=[jax.ShapeDtypeStruct(b.shape, b.dtype) for b in both],
        input_output_aliases={i: i for i in range(n)},
        scratch_shapes=[pltpu.SemaphoreType.DMA((n,)), pltpu.SemaphoreType.DMA((n,))],
        compiler_params=pltpu.CompilerParams(has_side_effects=True),
    )(*both)


def _allreduce_small(pack):
    rows, d = pack.shape

    def body(p_ref, o_ref, slots, send, recv):
        x, y, c, _ = _place()
        me = 4 * x + 2 * y + c
        slots[me] = p_ref[...]
        cps = []
        for k in range(1, N_DEV):
            px, py, pc = x ^ (k >> 2), y ^ ((k >> 1) & 1), c ^ (k & 1)
            cp = pltpu.make_async_remote_copy(
                src_ref=p_ref, dst_ref=slots.at[me], send_sem=send.at[k - 1], recv_sem=recv.at[k - 1],
                device_id=(px, py, pc), device_id_type=MESH)
            cp.start()
            cps.append(cp)
        for k in range(1, N_DEV):
            px, py, pc = x ^ (k >> 2), y ^ ((k >> 1) & 1), c ^ (k & 1)
            arrival = pltpu.make_async_remote_copy(
                src_ref=p_ref, dst_ref=slots.at[4 * px + 2 * py + pc], send_sem=send.at[k - 1],
                recv_sem=recv.at[k - 1], device_id=(px, py, pc), device_id_type=MESH)
            arrival.wait_recv()
            arrival.wait_send()
        acc = slots[0]
        for k in range(1, N_DEV):
            acc = acc + slots[k]
        o_ref[...] = acc

    vm = pl.BlockSpec(memory_space=pltpu.VMEM)
    return pl.pallas_call(
        body, name="allreduce_small", in_specs=[vm], out_specs=vm,
        out_shape=jax.ShapeDtypeStruct((rows, d), F32),
        scratch_shapes=[pltpu.VMEM((N_DEV, rows, d), F32), pltpu.SemaphoreType.DMA((N_DEV - 1,)),
                        pltpu.SemaphoreType.DMA((N_DEV - 1,))],
        compiler_params=pltpu.CompilerParams(has_side_effects=True),
    )(pack)


def _adamw(w, g, m, v, name):
    rows, cols = w.shape

    def fn(wv, gv, mv, vv):
        m2 = ADAM_B1 * mv + (1.0 - ADAM_B1) * gv
        v2 = ADAM_B2 * vv + (1.0 - ADAM_B2) * (gv * gv)
        m_hat = m2 / (1.0 - ADAM_B1 ** ADAM_STEP)
        v_hat = v2 / (1.0 - ADAM_B2 ** ADAM_STEP)
        delta = -ADAM_LR * (m_hat / (jnp.sqrt(v_hat) + ADAM_EPS) + ADAM_WD * wv)
        return delta, m2, v2

    ins = [(a, cols, 0) for a in (w, g, m, v)]
    return _rowwise(fn, ins, [(cols, F32)] * 3, rows=rows, tm=_tile(rows, max(8, STREAM_BLOCK_ELEMS // 4 // cols)),
                    name=name)


BIG = ["w_in", "w_branch_a", "w_branch_b", "w_mix_out", "w_mem_q", "w_mem_kv", "w_mem_o", "w_ffn_in", "w_ffn_out"]
BIG_AXIS = {"w_in": 1, "w_branch_a": 1, "w_branch_b": 1, "w_mix_out": 0, "w_mem_q": 0, "w_mem_kv": 1,
            "w_mem_o": 0, "w_ffn_in": 1, "w_ffn_out": 0}
NORMS = ["norm_mix", "norm_mem_q", "norm_mem_kv", "norm_ffn", "norm_final"]
ORDER = ["norm_mix", "w_in", "conv_w", "w_branch_a", "w_branch_b", "w_mix_out", "norm_mem_q", "norm_mem_kv",
         "w_mem_q", "w_mem_kv", "w_mem_o", "norm_ffn", "w_ffn_in", "w_ffn_out", "norm_final"]


def _pack_small(vals, conv):
    d = vals[0].shape[-1]
    rows = [v.reshape(1, d) for v in vals]
    conv = jnp.pad(conv, ((0, 0), (0, d - conv.shape[1])))
    pad = jnp.zeros((SMALL_ROWS - len(rows) - CONV_K, d), F32)
    return jnp.concatenate(rows + [conv, pad], axis=0)


def kernel(x, mem, norm_mix, w_in, conv_w, w_branch_a, w_branch_b, w_mix_out, norm_mem_q, norm_mem_kv, w_mem_q, w_mem_kv, w_mem_o, norm_ffn, w_ffn_in, w_ffn_out, norm_final, loss_target, m_norm_mix, m_w_in, m_conv_w, m_w_branch_a, m_w_branch_b, m_w_mix_out, m_norm_mem_q, m_norm_mem_kv, m_w_mem_q, m_w_mem_kv, m_w_mem_o, m_norm_ffn, m_w_ffn_in, m_w_ffn_out, m_norm_final, v_norm_mix, v_w_in, v_conv_w, v_w_branch_a, v_w_branch_b, v_w_mix_out, v_norm_mem_q, v_norm_mem_kv, v_w_mem_q, v_w_mem_kv, v_w_mem_o, v_norm_ffn, v_w_ffn_in, v_w_ffn_out, v_norm_final):
    args = dict(locals())
    wts = {n: args[n] for n in ORDER}
    mom = {n: args["m_" + n] for n in ORDER}
    var = {n: args["v_" + n] for n in ORDER}
    x = x[0]
    mem = mem[0]
    target = loss_target[0]
    s, d = x.shape
    gains = {n: wts[n].reshape(1, d) for n in NORMS}
    chip = 2 * lax.axis_index("x") + lax.axis_index("y")
    core = lax.axis_index("c").astype(jnp.int32).reshape(1)
    place = jnp.stack([chip, lax.axis_index("c")]).astype(jnp.int32)

    conv_shard = jnp.pad(conv_w[0], ((0, CONV_ROWS - CONV_K), (0, 0)))
    first = [_cast_place(wts["w_in"][0], BIG_AXIS["w_in"], place, BF, "place_w_in"),
             _cast_place(conv_shard, 1, place, F32, "place_conv_w")]
    w_in_full, conv_full = _gather_weights(first, [BIG_AXIS["w_in"], 1], [True, False])
    W = {"w_in": w_in_full}
    ride_in_proj = ["w_branch_a", "w_branch_b", "w_mix_out", "w_mem_q", "w_mem_o"]
    ride_attention = ["w_ffn_in", "w_mem_kv"]
    ride_ffn_in = ["w_ffn_out"]
    placed = {n: _cast_place(wts[n][0], BIG_AXIS[n], place, BF, "place_" + n)
              for n in ride_in_proj + ride_attention + ride_ffn_in}

    def fetch(names):
        return _riding_fetch([placed[n] for n in names], [BIG_AXIS[n] for n in names])

    def hand_on(names, bufs, tag):
        W.update(zip(names, _hand_on(bufs, [BIG_AXIS[n] for n in names], "gather_hand_on" + tag)))

    h1 = _rmsnorm(x, gains["norm_mix"], "norm_mix_fwd")
    proj, *bufs_a = _matmul(h1, W["w_in"], tn=1280, name="in_proj", rider=fetch(ride_in_proj))
    o_a, o_a32, *bufs_b = _sb_fwd(proj, fetch(ride_attention))
    y_b = _conv_fwd(proj, conv_full)
    hand_on(ride_in_proj + ride_attention, bufs_a + bufs_b, "")
    br_a, br_b, merged = _branches_merge(o_a, y_b, W["w_branch_a"], W["w_branch_b"], proj)
    x1 = _matmul(merged, W["w_mix_out"], tn=1024, out_dtype=F32, resid=x, name="mix_out")

    hq, q_m = _norm_matmul(x1, gains["norm_mem_q"], W["w_mem_q"], tn=1024, name="norm_mem_q")
    mn, kv = _norm_matmul(mem, gains["norm_mem_kv"], W["w_mem_kv"], tn=1024, name="norm_mem_kv")
    o_m = _mem_fwd(q_m, kv)
    x2 = _matmul(o_m, W["w_mem_o"], tn=1024, out_dtype=F32, resid=x1, name="mem_o")

    hf, gate, up, act, *bufs_c = _norm_ffn_in_swiglu(x2, gains["norm_ffn"], W["w_ffn_in"], rider=fetch(ride_ffn_in))
    hand_on(ride_ffn_in, bufs_c, "_late")

    dx3, dx3_b, dg_final, loss_part = _ffn_out_loss(act, W["w_ffn_out"], x2, gains["norm_final"], target)

    def halves_of(names):
        views = []
        for n in names:
            r, cdim = gw[n].shape
            views.append(gw[n].reshape(1, 2, r // 2, cdim) if BIG_AXIS[n] == 1
                         else gw[n].reshape(N_CHIPS, 2, r // (2 * N_CHIPS), cdim))
        return views

    def pair_adds(names, views, got):
        return [_pair_add(v, g, core, "pair_add_" + n) for n, v, g in zip(names, views, got)]

    def chip_sums(names, sums, slots):
        return [_chip_sum(sm, sl, BIG_AXIS[n], place, "chip_sum_" + n) for n, sm, sl in zip(names, sums, slots)]

    gw = {"w_ffn_out": _matmul(act, dx3_b, ta=True, tm=1408, tn=512, name="gw_ffn_out")}
    first_views = halves_of(["w_ffn_out"])
    dgu, *first_got = _d_act_swiglu(dx3_b, W["w_ffn_out"], gate, up, rider=_riding_pairs(first_views))
    gw["w_ffn_in"] = _matmul(hf, dgu, ta=True, tn=512, name="gw_ffn_in")
    first_sums = pair_adds(["w_ffn_out"], first_views, first_got)
    dx2, dx2_b, dg_ffn, *first_slots = _matmul_norm_bwd(
        dgu, W["w_ffn_in"], x2, gains["norm_ffn"], dx3, tm=256, bf_copy=True, name="d_hf_norm_bwd",
        rider=_riding_slots(first_sums, [BIG_AXIS["w_ffn_out"]]))

    do_m = _matmul(dx2_b, W["w_mem_o"], tb=True, tn=1024, name="d_o_m")
    gw["w_mem_o"] = _matmul(o_m, dx2_b, ta=True, tn=512, name="gw_mem_o")
    dq_m, dk_m, dv_m = _mem_bwd(q_m, kv, do_m)
    dkv = jnp.concatenate([dk_m, dv_m], axis=-1)
    dx1, dx1_b, dg_q = _matmul_norm_bwd(dq_m, W["w_mem_q"], x1, gains["norm_mem_q"], dx2, tm=512, bf_copy=True,
                                        name="d_hq_norm_bwd")
    gw["w_mem_q"] = _matmul(hq, dq_m, ta=True, tn=512, name="gw_mem_q")
    dmn = _matmul(dkv, W["w_mem_kv"], tb=True, tn=1024, out_dtype=F32, name="d_mn")
    gw["w_mem_kv"] = _matmul(mn, dkv, ta=True, tn=1024, name="gw_mem_kv")
    dg_kv = _norm_gain_grad(mem, dmn, "norm_mem_kv_bwd")

    gw["w_mix_out"] = _matmul(merged, dx1_b, ta=True, tn=512, name="gw_mix_out")
    mid = ["w_ffn_in", "w_mem_o", "w_mem_q", "w_mem_kv", "w_mix_out"]
    mid_views = halves_of(mid)
    dbr_a, dbr_b, dga, dgb, *mid_got = _d_merged_gates(dx1_b, W["w_mix_out"], proj, br_a, br_b,
                                                       rider=_riding_pairs(mid_views))
    do_a = _matmul(dbr_a, W["w_branch_a"], tb=True, name="d_o_a")
    gw["w_branch_a"] = _matmul(o_a, dbr_a, ta=True, tn=512, name="gw_branch_a")
    dy_b = _matmul(dbr_b, W["w_branch_b"], tb=True, name="d_y_b")
    gw["w_branch_b"] = _matmul(y_b, dbr_b, ta=True, tn=512, name="gw_branch_b")
    branches = ["w_branch_a", "w_branch_b"]
    branch_views = halves_of(branches)
    du, dgate_b, dgate_c, dconv, *branch_got = _conv_bwd(proj, conv_full, dy_b, rider=_riding_pairs(branch_views))
    early = mid + branches
    early_sums = pair_adds(mid, mid_views, mid_got) + pair_adds(branches, branch_views, branch_got)
    dq, dk, dv, *early_slots = _sb_bwd(proj, o_a32, do_a, _riding_slots(early_sums, [BIG_AXIS[n] for n in early]))

    def assemble(*parts):
        return jnp.concatenate([p.astype(BF) for p in parts], axis=-1)

    hw = dq.shape[1]
    dproj = _rowwise(assemble, [(t, hw, 0) for t in (dq, dk, dv, du, dgate_b, dgate_c)] + [(dga, d, 0), (dgb, d, 0)],
                     [(proj.shape[1], BF)], rows=s, tm=256, name="assemble_dproj")[0]
    gw["w_in"] = _matmul(h1, dproj, ta=True, tn=640, name="gw_in")
    in_views = halves_of(["w_in"])
    in_sums = pair_adds(["w_in"], in_views, _pair_exchange(in_views, "grad_pair_exchange_in"))
    grad_x, dg_mix, *in_slots = _matmul_norm_bwd(dproj, W["w_in"], x, gains["norm_mix"], dx1, tm=256, bf_copy=False,
                                                 name="d_h1_norm_bwd",
                                                 rider=_riding_slots(in_sums, [BIG_AXIS["w_in"]]))

    halves = dict(zip(early, chip_sums(early, early_sums, early_slots)))
    halves.update(zip(["w_ffn_out"], chip_sums(["w_ffn_out"], first_sums, first_slots)))
    halves.update(zip(["w_in"], chip_sums(["w_in"], in_sums, in_slots)))
    both = _half_swap([halves[n] for n in BIG])
    grads = {n: b.reshape(wts[n].shape[1:]) for n, b in zip(BIG, both)}

    small_g = [dg_mix, dg_q, dg_kv, dg_ffn, dg_final]
    pack = _pack_small(small_g, dconv[:CONV_K])
    pack = pack.at[ROW_LOSS].set(jnp.broadcast_to(loss_part[0, :1], (d,)))
    red = _allreduce_small(pack)
    loss = red[ROW_LOSS, 0]
    cw = conv_w.shape[2]
    conv_g = lax.dynamic_slice(red, (ROW_CONV, chip * cw), (CONV_K, cw))
    small_grad = _pack_small([red[i] for i in range(len(NORMS))], conv_g)
    small = [_pack_small([t[n] for n in NORMS], t["conv_w"][0]) for t in (wts, mom, var)]
    s_delta, s_m, s_v = _adamw(small[0], small_grad, small[1], small[2], "adamw_small")

    out = {"grad": {}, "delta": {}, "new_m": {}, "new_v": {}}
    for n in BIG:
        shp = wts[n].shape
        dl, m2, v2 = _adamw(wts[n][0], grads[n], mom[n][0], var[n][0], "adamw_" + n)
        out["grad"][n] = grads[n].reshape(shp)
        out["delta"][n], out["new_m"][n], out["new_v"][n] = dl.reshape(shp), m2.reshape(shp), v2.reshape(shp)
    for key, blk in (("grad", small_grad), ("delta", s_delta), ("new_m", s_m), ("new_v", s_v)):
        for i, n in enumerate(NORMS):
            out[key][n] = blk[i].reshape(wts[n].shape)
        out[key]["conv_w"] = blk[ROW_CONV:ROW_CONV + CONV_K, :cw].reshape(conv_w.shape)

    return (loss, grad_x[None], *[out["grad"][n] for n in ORDER], *[out["delta"][n] for n in ORDER],
            *[out["new_m"][n] for n in ORDER], *[out["new_v"][n] for n in ORDER])
```

```python
import math

import jax
import jax.numpy as jnp
from jax import lax
from jax.experimental import pallas as pl
from jax.experimental.pallas import tpu as pltpu

BF = jnp.bfloat16
F32 = jnp.float32
MESH = pl.DeviceIdType.MESH

SB_HEAD_DIM = 64
LANES = 128
MEM_HEADS = 4
CONV_K = 3
CONV_ROWS = 8
EPS = 1e-6
N_CHIPS = 4
N_DEV = 8
VMEM_LIMIT = 56 * 1024 * 1024
STREAM_BLOCK_ELEMS = 1 << 20

ADAM_LR = 0.001
ADAM_B1 = 0.9
ADAM_B2 = 0.999
ADAM_EPS = 1e-08
ADAM_WD = 0.01
ADAM_STEP = 10

SMALL_ROWS = 16
ROW_CONV = 5
ROW_LOSS = 8


def _params(sem=None, **kw):
    return pltpu.CompilerParams(dimension_semantics=sem, vmem_limit_bytes=VMEM_LIMIT, **kw)


def _tile(dim, pref):
    if dim <= pref:
        return dim
    for step in (LANES, 8):
        t = (pref // step) * step
        while t >= step:
            if dim % t == 0:
                return t
            t -= step
    raise ValueError(f"no tile of {dim} under {pref}")


def _matmul(a, b, *, ta=False, tb=False, tm=1024, tn=512, out_dtype=BF, resid=None, rider=None, name):
    if ta:
        kdim, m = a.shape
    else:
        m, kdim = a.shape
    n = b.shape[0] if tb else b.shape[1]
    tm, tn = _tile(m, tm), _tile(n, tn)
    a_spec = pl.BlockSpec((kdim, tm), lambda i, j: (0, i)) if ta else pl.BlockSpec((tm, kdim), lambda i, j: (i, 0))
    b_spec = pl.BlockSpec((tn, kdim), lambda i, j: (j, 0)) if tb else pl.BlockSpec((kdim, tn), lambda i, j: (0, j))
    o_spec = pl.BlockSpec((tm, tn), lambda i, j: (i, j))
    dims = (((0 if ta else 1,), (1 if tb else 0,)), ((), ()))
    has_res = resid is not None
    n_in = 2 + has_res

    def body(*refs):
        ins, (o_ref,) = _host_refs(refs, rider, n_in, 1)
        av, bv = ins[0][...], ins[1][...]
        if av.dtype != BF:
            av = av.astype(BF)
        if bv.dtype != BF:
            bv = bv.astype(BF)
        acc = lax.dot_general(av, bv, dims, preferred_element_type=F32)
        if has_res:
            acc = ins[2][...] + acc
        o_ref[...] = acc.astype(o_ref.dtype)
        if rider:
            rider.run(refs, n_in, 1, (pl.program_id(0) == 0) & (pl.program_id(1) == 0),
                      (pl.program_id(0) == m // tm - 1) & (pl.program_id(1) == n // tn - 1))

    res = pl.pallas_call(
        body, name=name, grid=(m // tm, n // tn),
        compiler_params=_params(("arbitrary", "arbitrary") if rider else ("parallel", "parallel"),
                                has_side_effects=rider is not None),
        **_with_rider(rider, n_in, 1, [a_spec, b_spec] + ([o_spec] if has_res else []), [o_spec],
                      [jax.ShapeDtypeStruct((m, n), out_dtype)]),
    )(*([a, b] + ([resid] if has_res else []) + (rider.arrays if rider else [])))
    return res if rider else res[0]


def _rowwise(fn, ins, outs, *, rows, tm, name, accs=()):
    tm = _tile(rows, tm)
    in_specs, args = [], []
    for arr, cols, cb in ins:
        if cols is None:
            in_specs.append(pl.BlockSpec(arr.shape, lambda i, nd=arr.ndim: (0,) * nd))
        else:
            in_specs.append(pl.BlockSpec((tm, cols), lambda i, cb=cb: (i, cb)))
        args.append(arr)
    out_specs = [pl.BlockSpec((tm, cols), lambda i: (i, 0)) for cols, _ in outs]
    out_shape = [jax.ShapeDtypeStruct((rows, cols), dt) for cols, dt in outs]
    for r, c in accs:
        out_specs.append(pl.BlockSpec((r, c), lambda i: (0, 0)))
        out_shape.append(jax.ShapeDtypeStruct((r, c), F32))
    n_in, n_out = len(ins), len(outs)

    def body(*refs):
        res = fn(*[r[...] for r in refs[:n_in]])
        if not isinstance(res, (tuple, list)):
            res = (res,)
        refs = refs[n_in:]
        for o_ref, val in zip(refs[:n_out], res[:n_out]):
            o_ref[...] = val.astype(o_ref.dtype)
        first = pl.program_id(0) == 0
        for a_ref, val in zip(refs[n_out:], res[n_out:]):
            @pl.when(first)
            def _(a_ref=a_ref, val=val):
                a_ref[...] = val

            @pl.when(jnp.logical_not(first))
            def _(a_ref=a_ref, val=val):
                a_ref[...] += val

    res = pl.pallas_call(
        body, name=name, grid=(rows // tm,), in_specs=in_specs, out_specs=out_specs, out_shape=out_shape,
        compiler_params=_params(("arbitrary",) if accs else ("parallel",)),
    )(*args)
    return res


def _rstd(xf):
    return lax.rsqrt(jnp.mean(xf * xf, axis=-1, keepdims=True) + EPS)


def _rmsnorm(x, g, name):
    rows, d = x.shape
    return _rowwise(lambda xv, gv: xv * _rstd(xv) * gv, [(x, d, 0), (g, None, None)], [(d, BF)],
                    rows=rows, tm=512, name=name)[0]


def _norm_gain_grad(x, dy, name):
    rows, d = x.shape

    def fn(xv, dyv):
        return (jnp.sum(dyv.astype(F32) * (xv * _rstd(xv)), axis=0, keepdims=True),)

    return _rowwise(fn, [(x, d, 0), (dy, d, 0)], [], rows=rows, tm=512, name=name, accs=[(1, d)])[0]


def _matmul_norm_bwd(a, w, x, g, resid, *, tm, bf_copy, name, rider=None):
    m, kdim = a.shape
    d = w.shape[0]
    tm = _tile(m, tm)
    row = lambda i: (i, 0)
    whole = lambda i: (0, 0)
    n_out = 2 + bf_copy

    def body(*refs):
        (a_ref, w_ref, x_ref, g_ref, r_ref), outs = _host_refs(refs, rider, 5, n_out)
        dy = lax.dot_general(a_ref[...], w_ref[...], NT, preferred_element_type=F32)
        xv = x_ref[...]
        r = _rstd(xv)
        xn = xv * r
        dxn = dy * g_ref[...]
        dx = r_ref[...] + r * (dxn - xn * jnp.mean(dxn * xn, axis=-1, keepdims=True))
        outs[0][...] = dx
        if bf_copy:
            outs[1][...] = dx.astype(BF)
        dg = jnp.sum(dy * xn, axis=0, keepdims=True)
        first = pl.program_id(0) == 0

        @pl.when(first)
        def _():
            outs[-1][...] = dg

        @pl.when(jnp.logical_not(first))
        def _():
            outs[-1][...] += dg

        if rider:
            rider.run(refs, 5, n_out, first, pl.program_id(0) == m // tm - 1)

    tok = pl.BlockSpec((tm, d), row)
    out_specs = [tok] + ([tok] if bf_copy else []) + [pl.BlockSpec((1, d), whole)]
    out_shape = ([jax.ShapeDtypeStruct((m, d), F32)] + ([jax.ShapeDtypeStruct((m, d), BF)] if bf_copy else [])
                 + [jax.ShapeDtypeStruct((1, d), F32)])
    return pl.pallas_call(
        body, name=name, grid=(m // tm,),
        compiler_params=_params(("arbitrary",), has_side_effects=rider is not None),
        **_with_rider(
            rider, 5, n_out,
            [pl.BlockSpec((tm, kdim), row), pl.BlockSpec((d, kdim), whole), tok, pl.BlockSpec((1, d), whole), tok],
            out_specs, out_shape),
    )(a, w, x, g, resid, *(rider.arrays if rider else []))


def _norm_ffn_in_swiglu(x, g, w, *, tm=1024, tn=1408, rider=None):
    m, kdim = x.shape
    f = w.shape[1] // 2
    tm, tn = _tile(m, tm), _tile(f, tn)
    nj = f // tn

    def body(*refs):
        (x_ref, g_ref, wg_ref, wu_ref), (h_ref, gate_ref, up_ref, act_ref) = _host_refs(refs, rider, 4, 4)
        if rider:
            rider.run(refs, 4, 4, (pl.program_id(0) == 0) & (pl.program_id(1) == 0),
                      (pl.program_id(0) == m // tm - 1) & (pl.program_id(1) == nj - 1))

        @pl.when(pl.program_id(1) == 0)
        def _():
            xv = x_ref[...]
            h_ref[...] = (xv * _rstd(xv) * g_ref[...]).astype(h_ref.dtype)

        hv = h_ref[...]
        gate = jnp.dot(hv, wg_ref[...], preferred_element_type=F32)
        up = jnp.dot(hv, wu_ref[...], preferred_element_type=F32)
        gate_ref[...] = gate.astype(gate_ref.dtype)
        up_ref[...] = up.astype(up_ref.dtype)
        act_ref[...] = (gate * jax.nn.sigmoid(gate) * up).astype(act_ref.dtype)

    tile = pl.BlockSpec((tm, tn), lambda i, j: (i, j))
    rows = pl.BlockSpec((tm, kdim), lambda i, j: (i, 0))
    return pl.pallas_call(
        body, name="norm_ffn_in_swiglu", grid=(m // tm, nj),
        compiler_params=_params(("arbitrary", "arbitrary") if rider else ("parallel", "arbitrary"),
                                has_side_effects=rider is not None),
        **_with_rider(
            rider, 4, 4,
            [rows, pl.BlockSpec((1, kdim), lambda i, j: (0, 0)), pl.BlockSpec((kdim, tn), lambda i, j: (0, j)),
             pl.BlockSpec((kdim, tn), lambda i, j: (0, nj + j))],
            [rows, tile, tile, tile],
            [jax.ShapeDtypeStruct((m, kdim), BF)] + [jax.ShapeDtypeStruct((m, f), BF)] * 3),
    )(x, g, w, w, *(rider.arrays if rider else []))


def _norm_matmul(x, g, w, *, tm=1024, tn, name):
    m, kdim = x.shape
    n = w.shape[1]
    tm, tn = _tile(m, tm), _tile(n, tn)

    def body(x_ref, g_ref, w_ref, h_ref, o_ref):
        @pl.when(pl.program_id(1) == 0)
        def _():
            xv = x_ref[...]
            h_ref[...] = (xv * _rstd(xv) * g_ref[...]).astype(h_ref.dtype)

        o_ref[...] = jnp.dot(h_ref[...], w_ref[...], preferred_element_type=F32).astype(o_ref.dtype)

    rows = pl.BlockSpec((tm, kdim), lambda i, j: (i, 0))
    return pl.pallas_call(
        body, name=name, grid=(m // tm, n // tn),
        in_specs=[rows, pl.BlockSpec((1, kdim), lambda i, j: (0, 0)), pl.BlockSpec((kdim, tn), lambda i, j: (0, j))],
        out_specs=[rows, pl.BlockSpec((tm, tn), lambda i, j: (i, j))],
        out_shape=[jax.ShapeDtypeStruct((m, kdim), BF), jax.ShapeDtypeStruct((m, n), BF)],
        compiler_params=_params(("parallel", "arbitrary")),
    )(x, g, w)


def _ffn_out_loss(act, w, resid, g, target, *, tm=512):
    m, f = act.shape
    d = w.shape[1]
    tm = _tile(m, tm)
    row = lambda i: (i, 0)
    whole = lambda i: (0, 0)

    def body(a_ref, w_ref, r_ref, g_ref, t_ref, dx_ref, dxb_ref, dg_ref, loss_ref):
        xv = r_ref[...] + jnp.dot(a_ref[...], w_ref[...], preferred_element_type=F32)
        gv = g_ref[...]
        r = _rstd(xv)
        xn = xv * r
        err = xn * gv - t_ref[...]
        loss = 0.5 * jnp.sum(jnp.mean(err * err, axis=-1, keepdims=True), axis=0, keepdims=True)
        dyv = err * (1.0 / d)
        dxn = dyv * gv
        dx = r * (dxn - xn * jnp.mean(dxn * xn, axis=-1, keepdims=True))
        dx_ref[...] = dx
        dxb_ref[...] = dx.astype(dxb_ref.dtype)
        dg = jnp.sum(dyv * xn, axis=0, keepdims=True)
        loss_b = jnp.broadcast_to(loss, (1, LANES))
        first = pl.program_id(0) == 0

        @pl.when(first)
        def _():
            dg_ref[...] = dg
            loss_ref[...] = loss_b

        @pl.when(jnp.logical_not(first))
        def _():
            dg_ref[...] += dg
            loss_ref[...] += loss_b

    tok = pl.BlockSpec((tm, d), row)
    return pl.pallas_call(
        body, name="ffn_out_loss", grid=(m // tm,),
        in_specs=[pl.BlockSpec((tm, f), row), pl.BlockSpec((f, d), whole), tok, pl.BlockSpec((1, d), whole), tok],
        out_specs=[tok, tok, pl.BlockSpec((1, d), whole), pl.BlockSpec((1, LANES), whole)],
        out_shape=[jax.ShapeDtypeStruct((m, d), F32), jax.ShapeDtypeStruct((m, d), BF),
                   jax.ShapeDtypeStruct((1, d), F32), jax.ShapeDtypeStruct((1, LANES), F32)],
        compiler_params=_params(("arbitrary",)),
    )(act, w, resid, g, target)


def _d_act_swiglu(dx, w, gate, up, *, tm=256, rider=None):
    m, d = dx.shape
    f = w.shape[0]
    tm = _tile(m, tm)
    row = lambda i: (i, 0)

    def body(*refs):
        (dx_ref, w_ref, gate_ref, up_ref), (o_ref,) = _host_refs(refs, rider, 4, 1)
        if rider:
            rider.run(refs, 4, 1, pl.program_id(0) == 0, pl.program_id(0) == m // tm - 1)
        da = lax.dot_general(dx_ref[...], w_ref[...], NT, preferred_element_type=F32)
        gv, uv = gate_ref[...].astype(F32), up_ref[...].astype(F32)
        sg = jax.nn.sigmoid(gv)
        dgate = da * uv * (sg * (1.0 + gv * (1.0 - sg)))
        o_ref[...] = jnp.concatenate([dgate, da * (gv * sg)], axis=-1).astype(o_ref.dtype)

    return pl.pallas_call(
        body, name="d_act_swiglu", grid=(m // tm,),
        compiler_params=_params(("arbitrary",) if rider else ("parallel",), has_side_effects=rider is not None),
        **_with_rider(
            rider, 4, 1,
            [pl.BlockSpec((tm, d), row), pl.BlockSpec((f, d), lambda i: (0, 0)),
             pl.BlockSpec((tm, f), row), pl.BlockSpec((tm, f), row)],
            [pl.BlockSpec((tm, 2 * f), row)], [jax.ShapeDtypeStruct((m, 2 * f), BF)]),
    )(dx, w, gate, up, *(rider.arrays if rider else []))


GATE_A_BLK, GATE_B_BLK = 3, 4


def _branches_merge(o_a, y_b, w_a, w_b, proj, *, tm=1024):
    m, kdim = o_a.shape
    d = w_a.shape[1]
    tm = _tile(m, tm)
    row = lambda i: (i, 0)

    def body(a_ref, b_ref, wa_ref, wb_ref, ga_ref, gb_ref, bra_ref, brb_ref, merged_ref):
        bra = jnp.dot(a_ref[...], wa_ref[...], preferred_element_type=F32)
        brb = jnp.dot(b_ref[...], wb_ref[...], preferred_element_type=F32)
        bra_ref[...] = bra.astype(bra_ref.dtype)
        brb_ref[...] = brb.astype(brb_ref.dtype)
        merged = jax.nn.sigmoid(ga_ref[...].astype(F32)) * bra + jax.nn.sigmoid(gb_ref[...].astype(F32)) * brb
        merged_ref[...] = merged.astype(merged_ref.dtype)

    tok = pl.BlockSpec((tm, d), row)
    return pl.pallas_call(
        body, name="branches_merge", grid=(m // tm,),
        in_specs=[pl.BlockSpec((tm, kdim), row), pl.BlockSpec((tm, kdim), row),
                  pl.BlockSpec((kdim, d), lambda i: (0, 0)), pl.BlockSpec((kdim, d), lambda i: (0, 0)),
                  pl.BlockSpec((tm, d), lambda i: (i, GATE_A_BLK)), pl.BlockSpec((tm, d), lambda i: (i, GATE_B_BLK))],
        out_specs=[tok, tok, tok], out_shape=[jax.ShapeDtypeStruct((m, d), BF)] * 3,
        compiler_params=_params(("parallel",)),
    )(o_a, y_b, w_a, w_b, proj, proj)


def _d_merged_gates(dx, w_mix, proj, br_a, br_b, *, tm=512, rider=None):
    m, d = dx.shape
    tm = _tile(m, tm)
    row = lambda i: (i, 0)

    def body(*refs):
        ins, (dbra_ref, dbrb_ref, dga_ref, dgb_ref) = _host_refs(refs, rider, 6, 4)
        dx_ref, w_ref, ga_ref, gb_ref, bra_ref, brb_ref = ins
        if rider:
            rider.run(refs, 6, 4, pl.program_id(0) == 0, pl.program_id(0) == m // tm - 1)
        dm = lax.dot_general(dx_ref[...], w_ref[...], NT, preferred_element_type=F32)
        sa, sb = jax.nn.sigmoid(ga_ref[...].astype(F32)), jax.nn.sigmoid(gb_ref[...].astype(F32))
        dbra_ref[...] = (dm * sa).astype(dbra_ref.dtype)
        dbrb_ref[...] = (dm * sb).astype(dbrb_ref.dtype)
        dga_ref[...] = (dm * bra_ref[...].astype(F32) * (sa * (1.0 - sa))).astype(dga_ref.dtype)
        dgb_ref[...] = (dm * brb_ref[...].astype(F32) * (sb * (1.0 - sb))).astype(dgb_ref.dtype)

    tok = pl.BlockSpec((tm, d), row)
    return pl.pallas_call(
        body, name="d_merged_gates", grid=(m // tm,),
        compiler_params=_params(("arbitrary",) if rider else ("parallel",), has_side_effects=rider is not None),
        **_with_rider(
            rider, 6, 4,
            [tok, pl.BlockSpec((d, d), lambda i: (0, 0)),
             pl.BlockSpec((tm, d), lambda i: (i, GATE_A_BLK)), pl.BlockSpec((tm, d), lambda i: (i, GATE_B_BLK)),
             tok, tok],
            [tok] * 4, [jax.ShapeDtypeStruct((m, d), BF)] * 4),
    )(dx, w_mix, proj, proj, br_a, br_b, *(rider.arrays if rider else []))


SB_TK = 128
SB_KT = 2


def _sb_consts(tq):
    tk = SB_TK
    diff = lax.broadcasted_iota(jnp.int32, (tq, tk), 1) - lax.broadcasted_iota(jnp.int32, (tq, tk), 0)
    rj = lax.broadcasted_iota(jnp.int32, (2 * tk, 2 * tk), 0) & (tk - 1)
    cj = lax.broadcasted_iota(jnp.int32, (2 * tk, 2 * tk), 1)
    ones_half = cj >= tk
    later = jnp.where((rj > cj) | ones_half, 1.0, 0.0).astype(BF)
    later_incl = jnp.where((rj >= cj) | ones_half, 1.0, 0.0).astype(BF)
    return diff, later, later_incl


def _split_dot(val, rhs_twice):
    hi = val.astype(BF)
    lo = (val - hi.astype(F32)).astype(BF)
    return jnp.dot(jnp.concatenate([hi, lo], axis=1), rhs_twice, preferred_element_type=F32)


def _log_terms(z):
    sp = jnp.maximum(z, 0.0) + jnp.log(1.0 + jnp.exp(-jnp.abs(z)))
    return z - sp, sp


NT = (((1,), (1,)), ((), ()))
TN = (((0,), (0,)), ((), ()))


def _head_lane_masks(rows):
    lane = lax.broadcasted_iota(jnp.int32, (rows, LANES), 1)
    first = jnp.where(lane < SB_HEAD_DIM, 1.0, 0.0)
    return first.astype(BF), (1.0 - first).astype(BF)


DEAD_LOG = 104.0


def _walk_back(i, step, state, carries_of):
    def alive(st):
        c0, c1 = carries_of(st)
        return jnp.min(jnp.minimum(c0, c1)) < DEAD_LOG

    def cond(loop):
        done, live, _ = loop
        return jnp.logical_and(done < i, live)

    def body(loop):
        done, _, st = loop
        st = step(i - 1 - done, st)
        return done + 1, alive(st), st

    return lax.while_loop(cond, body, (jnp.int32(0), alive(state), state))[2]


def _tail(a, r0):
    return a if r0 == 0 else a[r0:]


def _add_tail(a, r0, delta):
    return a + delta if r0 == 0 else jnp.concatenate([a[:r0], a[r0:] + delta], axis=0)


def _both_heads(tile, masks):
    return jnp.concatenate([tile * masks[0], tile * masks[1]], axis=0)


def _with_rider(rider, n_in, n_out, in_specs, out_specs, out_shape, scratch=()):
    kw = dict(in_specs=list(in_specs), out_specs=list(out_specs), out_shape=list(out_shape),
              scratch_shapes=list(scratch), input_output_aliases={})
    if rider:
        extra = rider.call_args(n_in, n_out)
        kw["in_specs"] += extra["in_specs"]
        kw["out_specs"] += extra["out_specs"]
        kw["out_shape"] += extra["out_shape"]
        kw["scratch_shapes"] += extra["scratch"]
        kw["input_output_aliases"] = extra["aliases"]
    return kw


def _sb_fwd(proj, rider=None):
    s = proj.shape[0]
    tk, tq = SB_TK, SB_KT * SB_TK
    n_pairs = 4
    scale = 1.0 / math.sqrt(SB_HEAD_DIM)

    def body(*refs):
        (q_ref, k_ref, v_ref), (o_ref, o32_ref) = _host_refs(refs, rider, 3, 2)
        i = pl.program_id(1)
        diff, later, _ = _sb_consts(tq)
        qs = (q_ref[...].astype(F32) * scale).astype(BF)
        lane_masks = _head_lane_masks(tk)

        def step(g, state, masked):
            tiles = list(reversed(range(SB_KT)))
            chains = [(t, h) for t in tiles for h in range(2)]
            rows = {t: pl.ds(pl.multiple_of((g * SB_KT + t) * tk, tk), tk) for t in tiles}
            ks = {t: _both_heads(k_ref[rows[t], :], lane_masks) for t in tiles}
            vs = {t: _both_heads(v_ref[rows[t], :], lane_masks) for t in tiles}
            r0 = {t: t * tk if masked else 0 for t in tiles}
            allowed = {t: _tail(diff, r0[t]) < -t * tk for t in tiles}
            zs = {t: lax.dot_general(_tail(qs, r0[t]), ks[t], NT, preferred_element_type=F32) for t in tiles}
            logs = {}
            for t, h in chains:
                log_b, sp = _log_terms(zs[t][:, h * tk:(h + 1) * tk])
                logs[t, h] = (log_b, jnp.where(allowed[t], sp, 0.0) if masked else sp)
            sums = {c: _split_dot(logs[c][1], later) for c in chains}
            carries = list(state[0])
            ws = {}
            for t, h in chains:
                w = jnp.exp(logs[t, h][0] - (sums[t, h][:, :tk] + _tail(carries[h], r0[t])))
                ws[t, h] = (jnp.where(allowed[t], w, 0.0) if masked else w).astype(BF)
                carries[h] = _add_tail(carries[h], r0[t], sums[t, h][:, tk:])
            acc = state[1]
            for t in tiles:
                acc = _add_tail(acc, r0[t], jnp.dot(jnp.concatenate([ws[t, 0], ws[t, 1]], axis=1), vs[t],
                                                    preferred_element_type=F32))
            return tuple(carries), acc

        zero = jnp.zeros((tq, LANES), F32)
        state = step(i, ((zero, zero), zero), True)
        state = _walk_back(i, lambda g, st: step(g, st, False), state, lambda st: st[0])
        o_ref[...] = state[1].astype(o_ref.dtype)
        o32_ref[...] = state[1]
        if rider:
            rider.run(refs, 3, 2, (pl.program_id(0) == 0) & (i == 0),
                      (pl.program_id(0) == n_pairs - 1) & (i == s // tq - 1))

    tok = pl.BlockSpec((tq, LANES), lambda p, i: (i, p))
    return pl.pallas_call(
        body, name="sb_attn_fwd", grid=(n_pairs, s // tq),
        compiler_params=_params(("arbitrary", "arbitrary"), has_side_effects=rider is not None),
        **_with_rider(
            rider, 3, 2,
            [tok, pl.BlockSpec((s, LANES), lambda p, i: (0, n_pairs + p)),
             pl.BlockSpec((s, LANES), lambda p, i: (0, 2 * n_pairs + p))],
            [tok, tok],
            [jax.ShapeDtypeStruct((s, n_pairs * LANES), BF), jax.ShapeDtypeStruct((s, n_pairs * LANES), F32)]),
    )(proj, proj, proj, *(rider.arrays if rider else []))


def _sb_bwd(proj, o32, do_a, rider=None):
    s = proj.shape[0]
    tk, tq = SB_TK, SB_KT * SB_TK
    n_pairs = 4
    scale = 1.0 / math.sqrt(SB_HEAD_DIM)

    def body(*refs):
        (q_ref, k_ref, v_ref, o_ref, do_ref), (dq_ref, dk_ref, dv_ref) = _host_refs(refs, rider, 5, 3)
        i = pl.program_id(1)

        @pl.when(i == 0)
        def _():
            dk_ref[...] = jnp.zeros_like(dk_ref)
            dv_ref[...] = jnp.zeros_like(dv_ref)

        diff, later, later_incl = _sb_consts(tq)
        qs = (q_ref[...].astype(F32) * scale).astype(BF)
        do2 = do_ref[...]
        prod = do2.astype(F32) * o_ref[...]
        lane_masks = _head_lane_masks(tk)
        first_head = lax.broadcasted_iota(jnp.int32, (tq, LANES), 1) < SB_HEAD_DIM
        totals = [jnp.broadcast_to(jnp.sum(jnp.where(keep, prod, 0.0), axis=-1, keepdims=True), (tq, tk))
                  for keep in (first_head, jnp.logical_not(first_head))]
        first_head_k = first_head[:tk]

        def step(g_idx, state, masked):
            tiles = list(reversed(range(SB_KT)))
            chains = [(t, h) for t in tiles for h in range(2)]
            rows = {t: pl.ds(pl.multiple_of((g_idx * SB_KT + t) * tk, tk), tk) for t in tiles}
            ks = {t: _both_heads(k_ref[rows[t], :], lane_masks) for t in tiles}
            vs = {t: _both_heads(v_ref[rows[t], :], lane_masks) for t in tiles}
            r0 = {t: t * tk if masked else 0 for t in tiles}
            allowed = {t: _tail(diff, r0[t]) < -t * tk for t in tiles}
            zs = {t: lax.dot_general(_tail(qs, r0[t]), ks[t], NT, preferred_element_type=F32) for t in tiles}
            dws = {t: lax.dot_general(_tail(do2, r0[t]), vs[t], NT, preferred_element_type=F32) for t in tiles}
            logs = {}
            for t, h in chains:
                log_b, sp = _log_terms(zs[t][:, h * tk:(h + 1) * tk])
                logs[t, h] = (log_b, jnp.where(allowed[t], sp, 0.0) if masked else sp)
            sums = {c: _split_dot(logs[c][1], later) for c in chains}
            c_log, c_g = list(state[0]), list(state[1])
            ws, gs = {}, {}
            for t, h in chains:
                w = jnp.exp(logs[t, h][0] - (sums[t, h][:, :tk] + _tail(c_log[h], r0[t])))
                ws[t, h] = (jnp.where(allowed[t], w, 0.0) if masked else w).astype(BF)
                c_log[h] = _add_tail(c_log[h], r0[t], sums[t, h][:, tk:])
                gs[t, h] = ws[t, h].astype(F32) * dws[t][:, h * tk:(h + 1) * tk]
            gsums = {c: _split_dot(gs[c], later_incl) for c in chains}
            dzs = {}
            for t, h in chains:
                beta = jnp.exp(logs[t, h][0])
                earlier = _tail(totals[h], r0[t]) - (gsums[t, h][:, :tk] + _tail(c_g[h], r0[t]))
                dz = gs[t, h] * (1.0 - beta) - earlier * beta
                dzs[t, h] = (jnp.where(allowed[t], dz, 0.0) if masked else dz).astype(BF)
                c_g[h] = _add_tail(c_g[h], r0[t], gsums[t, h][:, tk:])
            dq = state[2]
            for t in tiles:
                dz_both = jnp.concatenate([dzs[t, 0], dzs[t, 1]], axis=1)
                w_both = jnp.concatenate([ws[t, 0], ws[t, 1]], axis=1)
                dq = _add_tail(dq, r0[t], jnp.dot(dz_both, ks[t], preferred_element_type=F32))
                dk2 = lax.dot_general(dz_both, _tail(qs, r0[t]), TN, preferred_element_type=F32)
                dv2 = lax.dot_general(w_both, _tail(do2, r0[t]), TN, preferred_element_type=F32)
                dk_ref[rows[t], :] += jnp.where(first_head_k, dk2[:tk], dk2[tk:])
                dv_ref[rows[t], :] += jnp.where(first_head_k, dv2[:tk], dv2[tk:])
            return tuple(c_log), tuple(c_g), dq

        zero = jnp.zeros((tq, LANES), F32)
        state = step(i, ((zero, zero), (zero, zero), zero), True)
        state = _walk_back(i, lambda g, st: step(g, st, False), state, lambda st: st[0])
        dq_ref[...] = (state[2] * scale).astype(dq_ref.dtype)
        if rider:
            rider.run(refs, 5, 3, (pl.program_id(0) == 0) & (i == 0),
                      (pl.program_id(0) == n_pairs - 1) & (i == s // tq - 1))

    width = n_pairs * LANES
    tok = pl.BlockSpec((tq, LANES), lambda p, i: (i, p))
    return pl.pallas_call(
        body, name="sb_attn_bwd", grid=(n_pairs, s // tq),
        compiler_params=_params(("arbitrary", "arbitrary"), has_side_effects=rider is not None),
        **_with_rider(
            rider, 5, 3,
            [tok, pl.BlockSpec((s, LANES), lambda p, i: (0, n_pairs + p)),
             pl.BlockSpec((s, LANES), lambda p, i: (0, 2 * n_pairs + p)), tok, tok],
            [tok, pl.BlockSpec((s, LANES), lambda p, i: (0, p)), pl.BlockSpec((s, LANES), lambda p, i: (0, p))],
            [jax.ShapeDtypeStruct((s, width), BF), jax.ShapeDtypeStruct((s, width), F32),
             jax.ShapeDtypeStruct((s, width), F32)]),
    )(proj, proj, proj, o32, do_a, *(rider.arrays if rider else []))


CONV_COL0 = 12


def _shift_rows(v, k):
    n = v.shape[0]
    row = lax.broadcasted_iota(jnp.int32, v.shape, 0)
    rolled = pltpu.roll(v, k % n, axis=0)
    keep = row >= k if k > 0 else row < n + k
    return jnp.where(keep, rolled, 0.0)


def _conv_specs(s):
    return [pl.BlockSpec((s, LANES), lambda cb: (0, CONV_COL0 + cb)),
            pl.BlockSpec((s, LANES), lambda cb: (0, CONV_COL0 + 4 + cb)),
            pl.BlockSpec((s, LANES), lambda cb: (0, CONV_COL0 + 8 + cb)),
            pl.BlockSpec((CONV_ROWS, LANES), lambda cb: (0, cb))]


def _conv_fwd(proj, conv_w):
    s = proj.shape[0]

    def body(u_ref, gb_ref, gc_ref, w_ref, y_ref):
        cu = gc_ref[...].astype(F32) * u_ref[...].astype(F32)
        w = w_ref[...]
        y = w[0:1] * _shift_rows(cu, 2) + w[1:2] * _shift_rows(cu, 1) + w[2:3] * cu
        y_ref[...] = (gb_ref[...].astype(F32) * y).astype(y_ref.dtype)

    return pl.pallas_call(
        body, name="conv_fwd", grid=(4,), in_specs=_conv_specs(s),
        out_specs=pl.BlockSpec((s, LANES), lambda cb: (0, cb)),
        out_shape=jax.ShapeDtypeStruct((s, 4 * LANES), BF),
        compiler_params=_params(("parallel",)),
    )(proj, proj, proj, conv_w)


def _conv_bwd(proj, conv_w, dy, rider=None):
    s = proj.shape[0]
    n_blocks = 4

    def body(*refs):
        (u_ref, gb_ref, gc_ref, w_ref, dy_ref), (du_ref, dgb_ref, dgc_ref, dw_ref) = _host_refs(refs, rider, 5, 4)
        u, gc = u_ref[...].astype(F32), gc_ref[...].astype(F32)
        dyv = dy_ref[...].astype(F32)
        w = w_ref[...]
        cu = gc * u
        cu1, cu2 = _shift_rows(cu, 1), _shift_rows(cu, 2)
        conv = w[0:1] * cu2 + w[1:2] * cu1 + w[2:3] * cu
        dgb_ref[...] = (dyv * conv).astype(dgb_ref.dtype)
        dc = dyv * gb_ref[...].astype(F32)
        dcu = w[2:3] * dc + w[1:2] * _shift_rows(dc, -1) + w[0:1] * _shift_rows(dc, -2)
        dgc_ref[...] = (dcu * u).astype(dgc_ref.dtype)
        du_ref[...] = (dcu * gc).astype(du_ref.dtype)
        tap_row = lax.broadcasted_iota(jnp.int32, (CONV_ROWS, LANES), 0)
        dw = jnp.zeros((CONV_ROWS, LANES), F32)
        for t, shifted in enumerate((cu2, cu1, cu)):
            dw = jnp.where(tap_row == t, jnp.sum(dc * shifted, axis=0, keepdims=True), dw)
        dw_ref[...] = dw
        if rider:
            rider.run(refs, 5, 4, pl.program_id(0) == 0, pl.program_id(0) == n_blocks - 1)

    col = pl.BlockSpec((s, LANES), lambda cb: (0, cb))
    act = jax.ShapeDtypeStruct((s, 4 * LANES), BF)
    return pl.pallas_call(
        body, name="conv_bwd", grid=(n_blocks,),
        compiler_params=_params(("arbitrary",) if rider else ("parallel",), has_side_effects=rider is not None),
        **_with_rider(
            rider, 5, 4, _conv_specs(s) + [col],
            [col, col, col, pl.BlockSpec((CONV_ROWS, LANES), lambda cb: (0, cb))],
            [act, act, act, jax.ShapeDtypeStruct((CONV_ROWS, n_blocks * LANES), F32)]),
    )(proj, proj, proj, conv_w, dy, *(rider.arrays if rider else []))


def _mem_probs(q, k, scale):
    sc = lax.dot_general(q, k, NT, preferred_element_type=F32) * scale
    p = jnp.exp(sc - jnp.max(sc, axis=-1, keepdims=True))
    return p / jnp.sum(p, axis=-1, keepdims=True)


def _mem_fwd(q_m, kv, tq=2048):
    s, d = q_m.shape
    mlen = kv.shape[0]
    hd = d // MEM_HEADS
    tq = _tile(s, tq)
    scale = 1.0 / math.sqrt(hd)

    def body(q_ref, k_ref, v_ref, o_ref):
        p = _mem_probs(q_ref[...], k_ref[...], scale)
        o_ref[...] = jnp.dot(p.astype(BF), v_ref[...], preferred_element_type=F32).astype(o_ref.dtype)

    return pl.pallas_call(
        body, name="mem_attn_fwd", grid=(MEM_HEADS, s // tq),
        in_specs=[pl.BlockSpec((tq, hd), lambda h, i: (i, h)),
                  pl.BlockSpec((mlen, hd), lambda h, i: (0, h)),
                  pl.BlockSpec((mlen, hd), lambda h, i: (0, MEM_HEADS + h))],
        out_specs=pl.BlockSpec((tq, hd), lambda h, i: (i, h)),
        out_shape=jax.ShapeDtypeStruct((s, d), BF),
        compiler_params=_params(("parallel", "parallel")),
    )(q_m, kv, kv)


def _mem_bwd(q_m, kv, do_m, tq=2048):
    s, d = q_m.shape
    mlen = kv.shape[0]
    hd = d // MEM_HEADS
    tq = _tile(s, tq)
    scale = 1.0 / math.sqrt(hd)

    def body(q_ref, k_ref, v_ref, do_ref, dq_ref, dk_ref, dv_ref):
        q, k, v, do = q_ref[...], k_ref[...], v_ref[...], do_ref[...]
        p = _mem_probs(q, k, scale)
        dp = lax.dot_general(do, v, NT, preferred_element_type=F32)
        ds = p * (dp - jnp.sum(dp * p, axis=-1, keepdims=True)) * scale
        dsb = ds.astype(BF)
        dq_ref[...] = jnp.dot(dsb, k, preferred_element_type=F32).astype(dq_ref.dtype)
        dk = lax.dot_general(dsb, q, TN, preferred_element_type=F32)
        dv = lax.dot_general(p.astype(BF), do, TN, preferred_element_type=F32)
        first = pl.program_id(1) == 0

        @pl.when(first)
        def _():
            dk_ref[...] = dk
            dv_ref[...] = dv

        @pl.when(jnp.logical_not(first))
        def _():
            dk_ref[...] += dk
            dv_ref[...] += dv

    tok = pl.BlockSpec((tq, hd), lambda h, i: (i, h))
    memb = pl.BlockSpec((mlen, hd), lambda h, i: (0, h))
    return pl.pallas_call(
        body, name="mem_attn_bwd", grid=(MEM_HEADS, s // tq),
        in_specs=[tok, memb, pl.BlockSpec((mlen, hd), lambda h, i: (0, MEM_HEADS + h)), tok],
        out_specs=[tok, memb, memb],
        out_shape=[jax.ShapeDtypeStruct((s, d), BF), jax.ShapeDtypeStruct((mlen, d), F32),
                   jax.ShapeDtypeStruct((mlen, d), F32)],
        compiler_params=_params(("parallel", "arbitrary")),
    )(q_m, kv, kv, do_m)


def _place():
    x, y, c = lax.axis_index("x"), lax.axis_index("y"), lax.axis_index("c")
    other_chips = [(1 - x, y), (x, 1 - y), (1 - x, 1 - y)]
    return x, y, c, other_chips


def _chip_no(cx, cy):
    return 2 * cx + cy


HBM_SPEC = pl.BlockSpec(memory_space=pl.ANY)


def _cast_place(shard, axis, place, dtype, name):
    r, c = shard.shape
    tr = _tile(r, max(16, STREAM_BLOCK_ELEMS // c))
    nblk = r // tr
    if axis == 1:
        full, out_map = (r, N_CHIPS * c), lambda i, pref: (i, pref[0])
    else:
        full, out_map = (N_CHIPS * r, c), lambda i, pref: (pref[0] * nblk + i, 0)

    def body(pref, s_ref, o_ref):
        o_ref[...] = s_ref[...].astype(o_ref.dtype)

    return pl.pallas_call(
        body, name=name,
        grid_spec=pltpu.PrefetchScalarGridSpec(
            num_scalar_prefetch=1, grid=(nblk,),
            in_specs=[pl.BlockSpec((tr, c), lambda i, pref: (i, 0))],
            out_specs=pl.BlockSpec((tr, c), out_map)),
        out_shape=jax.ShapeDtypeStruct(full, dtype),
        compiler_params=_params(("parallel",)),
    )(place, shard)


def _region(ref, axis, chip_no, half):
    width = ref.shape[axis] // N_CHIPS
    start = pl.multiple_of(chip_no * width, width)
    if axis == 1:
        if half is None:
            return ref.at[:, pl.ds(start, width)]
        hr = ref.shape[0] // 2
        return ref.at[pl.ds(pl.multiple_of(half * hr, hr), hr), pl.ds(start, width)]
    if half is None:
        return ref.at[pl.ds(start, width), :]
    hr = width // 2
    return ref.at[pl.ds(pl.multiple_of(start + half * hr, hr), hr), :]


def _gather_weights(fulls, axes, split):
    n = len(fulls)

    def body(*refs):
        outs = refs[n:2 * n]
        send, recv, fsend, frecv = refs[2 * n:]
        x, y, c, others = _place()
        me = _chip_no(x, y)
        sibling = (x, y, 1 - c)

        def copy(w, chip_no, half, sems, p, to):
            reg = _region(outs[w], axes[w], chip_no, half)
            return pltpu.make_async_remote_copy(
                src_ref=reg, dst_ref=reg, send_sem=sems[0].at[w, p], recv_sem=sems[1].at[w, p],
                device_id=to, device_id_type=MESH)

        for w in range(n):
            for p, chip in enumerate(others):
                copy(w, me, c if split[w] else None, (send, recv), p, (chip[0], chip[1], c)).start()
        for w in range(n):
            for p, chip in enumerate(others):
                half = c if split[w] else None
                copy(w, _chip_no(*chip), half, (send, recv), p, (chip[0], chip[1], c)).wait_recv()
                if split[w]:
                    copy(w, _chip_no(*chip), c, (fsend, frecv), p, sibling).start()
        for w in range(n):
            for p, chip in enumerate(others):
                copy(w, me, c if split[w] else None, (send, recv), p, (chip[0], chip[1], c)).wait_send()
                if split[w]:
                    handed = copy(w, _chip_no(*chip), 1 - c, (fsend, frecv), p, sibling)
                    handed.wait_recv()
                    handed.wait_send()

    return pl.pallas_call(
        body, name="gather_weights",
        in_specs=[HBM_SPEC] * n, out_specs=[HBM_SPEC] * n,
        out_shape=[jax.ShapeDtypeStruct(f.shape, f.dtype) for f in fulls],
        input_output_aliases={i: i for i in range(n)},
        scratch_shapes=[pltpu.SemaphoreType.DMA((n, 3))] * 4,
        compiler_params=pltpu.CompilerParams(has_side_effects=True),
    )(*fulls)


def _fetch_copy(refs, axes, whole, send, recv, w, p, chip, c, arriving):
    owner = _chip_no(*chip) if arriving else _chip_no(lax.axis_index("x"), lax.axis_index("y"))
    reg = _region(refs[w], axes[w], owner, None if whole[w] else c)
    return pltpu.make_async_remote_copy(
        src_ref=reg, dst_ref=reg, send_sem=send[p], recv_sem=recv[p],
        device_id=(chip[0], chip[1], c), device_id_type=MESH)


N_PEERS = N_CHIPS - 1


class _Rider:
    def __init__(self, arrays, outs, aliases, start, wait):
        self.arrays, self.outs, self.aliases, self.start, self.wait = list(arrays), list(outs), aliases, start, wait
        self.scratch = [pltpu.SemaphoreType.DMA((len(self.arrays), N_PEERS))] * 2

    def run(self, refs, n_in, n_out, first, last):
        ra, ro = len(self.arrays), len(self.outs)
        ins = refs[n_in:n_in + ra]
        outs = refs[n_in + ra + n_out:n_in + ra + n_out + ro]
        send, recv = refs[-2], refs[-1]

        @pl.when(first)
        def _():
            self.start(ins, outs, send, recv)

        @pl.when(last)
        def _():
            self.wait(ins, outs, send, recv)

    def call_args(self, n_in, n_out):
        ra = len(self.arrays)
        return dict(in_specs=[HBM_SPEC] * ra, out_specs=[HBM_SPEC] * len(self.outs), out_shape=self.outs,
                    aliases={n_in + k: n_out + o for k, o in self.aliases.items()}, scratch=self.scratch)


def _host_refs(refs, rider, n_in, n_out):
    ra = len(rider.arrays) if rider else 0
    return refs[:n_in], refs[n_in + ra:n_in + ra + n_out]


def _riding_fetch(fulls, axes):
    n = len(fulls)
    whole = [False] * n

    def sems(ref, w):
        return [ref.at[w, q] for q in range(N_PEERS)]

    def start(ins, outs, send, recv):
        _, _, c, others = _place()
        for w in range(n):
            for p, chip in enumerate(others):
                _fetch_copy(outs, axes, whole, sems(send, w), sems(recv, w), w, p, chip, c, False).start()

    def wait(ins, outs, send, recv):
        _, _, c, others = _place()
        for w in range(n):
            for p, chip in enumerate(others):
                _fetch_copy(outs, axes, whole, sems(send, w), sems(recv, w), w, p, chip, c, False).wait_send()
                _fetch_copy(outs, axes, whole, sems(send, w), sems(recv, w), w, p, chip, c, True).wait_recv()

    return _Rider(fulls, [jax.ShapeDtypeStruct(f.shape, f.dtype) for f in fulls], {k: k for k in range(n)}, start, wait)


def _hand_on(fulls, axes, name):
    n = len(fulls)

    def body(*refs):
        outs = refs[n:2 * n]
        send, recv = refs[2 * n:]
        x, y, c, others = _place()

        def copy(w, p, chip, half):
            reg = _region(outs[w], axes[w], _chip_no(*chip), half)
            return pltpu.make_async_remote_copy(
                src_ref=reg, dst_ref=reg, send_sem=send.at[w, p], recv_sem=recv.at[w, p],
                device_id=(x, y, 1 - c), device_id_type=MESH)

        for w in range(n):
            for p, chip in enumerate(others):
                copy(w, p, chip, c).start()
        for w in range(n):
            for p, chip in enumerate(others):
                copy(w, p, chip, 1 - c).wait()

    return pl.pallas_call(
        body, name=name,
        in_specs=[HBM_SPEC] * n, out_specs=[HBM_SPEC] * n,
        out_shape=[jax.ShapeDtypeStruct(f.shape, f.dtype) for f in fulls],
        input_output_aliases={i: i for i in range(n)},
        scratch_shapes=[pltpu.SemaphoreType.DMA((n, 3)), pltpu.SemaphoreType.DMA((n, 3))],
        compiler_params=pltpu.CompilerParams(has_side_effects=True),
    )(*fulls)


def _pair_exchange(grads, name):
    n = len(grads)

    def body(*refs):
        ins, outs = refs[:n], refs[n:2 * n]
        send, recv = refs[2 * n:]
        x, y, c, _ = _place()
        cps = []
        for w in range(n):
            cp = pltpu.make_async_remote_copy(
                src_ref=ins[w].at[:, 1 - c], dst_ref=outs[w], send_sem=send.at[w], recv_sem=recv.at[w],
                device_id=(x, y, 1 - c), device_id_type=MESH)
            cp.start()
            cps.append(cp)
        for cp in cps:
            cp.wait()

    return pl.pallas_call(
        body, name=name,
        in_specs=[HBM_SPEC] * n, out_specs=[HBM_SPEC] * n,
        out_shape=[jax.ShapeDtypeStruct((g.shape[0],) + g.shape[2:], g.dtype) for g in grads],
        scratch_shapes=[pltpu.SemaphoreType.DMA((n,)), pltpu.SemaphoreType.DMA((n,))],
        compiler_params=pltpu.CompilerParams(has_side_effects=True),
    )(*grads)


def _pair_add(g4, got, core, name):
    nj, _, hr, cdim = g4.shape
    tr = _tile(hr, max(8, STREAM_BLOCK_ELEMS // 2 // cdim))

    def body(core_ref, a_ref, b_ref, o_ref):
        o_ref[...] = (a_ref[...].astype(F32) + b_ref[...].astype(F32)).astype(o_ref.dtype)

    return pl.pallas_call(
        body, name=name,
        grid_spec=pltpu.PrefetchScalarGridSpec(
            num_scalar_prefetch=1, grid=(nj, hr // tr),
            in_specs=[pl.BlockSpec((1, None, tr, cdim), lambda j, i, core_ref: (j, core_ref[0], i, 0)),
                      pl.BlockSpec((1, tr, cdim), lambda j, i, core_ref: (j, i, 0))],
            out_specs=pl.BlockSpec((1, tr, cdim), lambda j, i, core_ref: (j, i, 0))),
        out_shape=jax.ShapeDtypeStruct((nj, hr, cdim), BF),
        compiler_params=_params(("parallel", "parallel")),
    )(core, g4, got)


def _piece(ref, axis, j, hc):
    if axis == 0:
        return ref.at[j]
    return ref.at[0, :, pl.ds(pl.multiple_of(j * hc, hc), hc)]


def _slot_shapes(sums, axes):
    return [(N_CHIPS - 1, sm.shape[1], sm.shape[2] // (1 if ax == 0 else N_CHIPS)) for sm, ax in zip(sums, axes)]


def _slot_copy(sums, lands, axes, send, recv, w, p, chip, c):
    return pltpu.make_async_remote_copy(
        src_ref=_piece(sums[w], axes[w], _chip_no(*chip), lands[w].shape[2]), dst_ref=lands[w].at[p],
        send_sem=send[p], recv_sem=recv[p],
        device_id=(chip[0], chip[1], c), device_id_type=MESH)


def _riding_pairs(views):
    n = len(views)

    def copies(ins, outs, send, recv):
        x, y, c, _ = _place()
        return [pltpu.make_async_remote_copy(
            src_ref=ins[w].at[:, 1 - c], dst_ref=outs[w], send_sem=send.at[w, 0], recv_sem=recv.at[w, 0],
            device_id=(x, y, 1 - c), device_id_type=MESH) for w in range(n)]

    def start(ins, outs, send, recv):
        for cp in copies(ins, outs, send, recv):
            cp.start()

    def wait(ins, outs, send, recv):
        for cp in copies(ins, outs, send, recv):
            cp.wait()

    outs = [jax.ShapeDtypeStruct((g.shape[0],) + g.shape[2:], g.dtype) for g in views]
    return _Rider(views, outs, {}, start, wait)


def _riding_slots(sums, axes):
    n = len(sums)
    shapes = _slot_shapes(sums, axes)

    def copies(ins, outs, send, recv):
        _, _, c, others = _place()
        return [_slot_copy(ins, outs, axes, [send.at[w, q] for q in range(N_PEERS)],
                           [recv.at[w, q] for q in range(N_PEERS)], w, p, chip, c)
                for w in range(n) for p, chip in enumerate(others)]

    def start(ins, outs, send, recv):
        for cp in copies(ins, outs, send, recv):
            cp.start()

    def wait(ins, outs, send, recv):
        for cp in copies(ins, outs, send, recv):
            cp.wait()

    return _Rider(sums, [jax.ShapeDtypeStruct(sh, sm.dtype) for sh, sm in zip(shapes, sums)], {}, start, wait)


def _chip_sum(psum, slots, axis, place, name):
    _, hr, hc = slots.shape
    tr = _tile(hr, 256)
    own_map = (lambda i, pref: (0, i, pref[0])) if axis == 1 else (lambda i, pref: (pref[0], i, 0))

    def body(pref, own_ref, s_ref, o_ref):
        o_ref[...] = ((own_ref[...].astype(F32) + s_ref[0].astype(F32)) + s_ref[1].astype(F32)) + s_ref[2].astype(F32)

    return pl.pallas_call(
        body, name=name,
        grid_spec=pltpu.PrefetchScalarGridSpec(
            num_scalar_prefetch=1, grid=(hr // tr,),
            in_specs=[pl.BlockSpec((None, tr, hc), own_map),
                      pl.BlockSpec((N_CHIPS - 1, tr, hc), lambda i, pref: (0, i, 0))],
            out_specs=pl.BlockSpec((None, tr, hc), lambda i, pref: (pref[1], i, 0))),
        out_shape=jax.ShapeDtypeStruct((2, hr, hc), F32),
        compiler_params=_params(("parallel",)),
    )(place, psum, slots)


def _half_swap(both):
    n = len(both)

    def body(*refs):
        outs = refs[n:2 * n]
        send, recv = refs[2 * n:]
        x, y, c, _ = _place()

        def copy(w, half):
            return pltpu.make_async_remote_copy(
                src_ref=outs[w].at[half], dst_ref=outs[w].at[half], send_sem=send.at[w], recv_sem=recv.at[w],
                device_id=(x, y, 1 - c), device_id_type=MESH)

        for w in range(n):
            copy(w, c).start()
        for w in range(n):
            copy(w, 1 - c).wait()

    return pl.pallas_call(
        body, name="grad_half_swap",
        in_specs=[HBM_SPEC] * n, out_specs=[HBM_SPEC] * n,
        out_shape=[jax.ShapeDtypeStruct(b.shape, b.dtype) for b in both],
        input_output_aliases={i: i for i in range(n)},
        scratch_shapes=[pltpu.SemaphoreType.DMA((n,)), pltpu.SemaphoreType.DMA((n,))],
        compiler_params=pltpu.CompilerParams(has_side_effects=True),
    )(*both)


def _allreduce_small(pack):
    rows, d = pack.shape

    def body(p_ref, o_ref, slots, send, recv):
        x, y, c, _ = _place()
        me = 4 * x + 2 * y + c
        slots[me] = p_ref[...]
        cps = []
        for k in range(1, N_DEV):
            px, py, pc = x ^ (k >> 2), y ^ ((k >> 1) & 1), c ^ (k & 1)
            cp = pltpu.make_async_remote_copy(
                src_ref=p_ref, dst_ref=slots.at[me], send_sem=send.at[k - 1], recv_sem=recv.at[k - 1],
                device_id=(px, py, pc), device_id_type=MESH)
            cp.start()
            cps.append(cp)
        for k in range(1, N_DEV):
            px, py, pc = x ^ (k >> 2), y ^ ((k >> 1) & 1), c ^ (k & 1)
            arrival = pltpu.make_async_remote_copy(
                src_ref=p_ref, dst_ref=slots.at[4 * px + 2 * py + pc], send_sem=send.at[k - 1],
                recv_sem=recv.at[k - 1], device_id=(px, py, pc), device_id_type=MESH)
            arrival.wait_recv()
            arrival.wait_send()
        acc = slots[0]
        for k in range(1, N_DEV):
            acc = acc + slots[k]
        o_ref[...] = acc

    vm = pl.BlockSpec(memory_space=pltpu.VMEM)
    return pl.pallas_call(
        body, name="allreduce_small", in_specs=[vm], out_specs=vm,
        out_shape=jax.ShapeDtypeStruct((rows, d), F32),
        scratch_shapes=[pltpu.VMEM((N_DEV, rows, d), F32), pltpu.SemaphoreType.DMA((N_DEV - 1,)),
                        pltpu.SemaphoreType.DMA((N_DEV - 1,))],
        compiler_params=pltpu.CompilerParams(has_side_effects=True),
    )(pack)


def _adamw(w, g, m, v, name):
    rows, cols = w.shape

    def fn(wv, gv, mv, vv):
        m2 = ADAM_B1 * mv + (1.0 - ADAM_B1) * gv
        v2 = ADAM_B2 * vv + (1.0 - ADAM_B2) * (gv * gv)
        m_hat = m2 / (1.0 - ADAM_B1 ** ADAM_STEP)
        v_hat = v2 / (1.0 - ADAM_B2 ** ADAM_STEP)
        delta = -ADAM_LR * (m_hat / (jnp.sqrt(v_hat) + ADAM_EPS) + ADAM_WD * wv)
        return delta, m2, v2, gv

    ins = [(a, cols, 0) for a in (w, g, m, v)]
    return _rowwise(fn, ins, [(cols, F32)] * 4, rows=rows, tm=_tile(rows, max(8, STREAM_BLOCK_ELEMS // 4 // cols)),
                    name=name)


BIG = ["w_in", "w_branch_a", "w_branch_b", "w_mix_out", "w_mem_q", "w_mem_kv", "w_mem_o", "w_ffn_in", "w_ffn_out"]
BIG_AXIS = {"w_in": 1, "w_branch_a": 1, "w_branch_b": 1, "w_mix_out": 0, "w_mem_q": 0, "w_mem_kv": 1,
            "w_mem_o": 0, "w_ffn_in": 1, "w_ffn_out": 0}
NORMS = ["norm_mix", "norm_mem_q", "norm_mem_kv", "norm_ffn", "norm_final"]
ORDER = ["norm_mix", "w_in", "conv_w", "w_branch_a", "w_branch_b", "w_mix_out", "norm_mem_q", "norm_mem_kv",
         "w_mem_q", "w_mem_kv", "w_mem_o", "norm_ffn", "w_ffn_in", "w_ffn_out", "norm_final"]


def _pack_small(vals, conv):
    d = vals[0].shape[-1]
    rows = [v.reshape(1, d) for v in vals]
    conv = jnp.pad(conv, ((0, 0), (0, d - conv.shape[1])))
    pad = jnp.zeros((SMALL_ROWS - len(rows) - CONV_K, d), F32)
    return jnp.concatenate(rows + [conv, pad], axis=0)


def kernel(x, mem, norm_mix, w_in, conv_w, w_branch_a, w_branch_b, w_mix_out, norm_mem_q, norm_mem_kv, w_mem_q, w_mem_kv, w_mem_o, norm_ffn, w_ffn_in, w_ffn_out, norm_final, loss_target, m_norm_mix, m_w_in, m_conv_w, m_w_branch_a, m_w_branch_b, m_w_mix_out, m_norm_mem_q, m_norm_mem_kv, m_w_mem_q, m_w_mem_kv, m_w_mem_o, m_norm_ffn, m_w_ffn_in, m_w_ffn_out, m_norm_final, v_norm_mix, v_w_in, v_conv_w, v_w_branch_a, v_w_branch_b, v_w_mix_out, v_norm_mem_q, v_norm_mem_kv, v_w_mem_q, v_w_mem_kv, v_w_mem_o, v_norm_ffn, v_w_ffn_in, v_w_ffn_out, v_norm_final):
    args = dict(locals())
    wts = {n: args[n] for n in ORDER}
    mom = {n: args["m_" + n] for n in ORDER}
    var = {n: args["v_" + n] for n in ORDER}
    x = x[0]
    mem = mem[0]
    target = loss_target[0]
    s, d = x.shape
    gains = {n: wts[n].reshape(1, d) for n in NORMS}
    chip = 2 * lax.axis_index("x") + lax.axis_index("y")
    core = lax.axis_index("c").astype(jnp.int32).reshape(1)
    place = jnp.stack([chip, lax.axis_index("c")]).astype(jnp.int32)

    conv_shard = jnp.pad(conv_w[0], ((0, CONV_ROWS - CONV_K), (0, 0)))
    first = [_cast_place(wts["w_in"][0], BIG_AXIS["w_in"], place, BF, "place_w_in"),
             _cast_place(conv_shard, 1, place, F32, "place_conv_w")]
    w_in_full, conv_full = _gather_weights(first, [BIG_AXIS["w_in"], 1], [True, False])
    W = {"w_in": w_in_full}
    ride_in_proj = ["w_branch_a", "w_branch_b", "w_mix_out", "w_mem_q", "w_mem_o"]
    ride_attention = ["w_ffn_in", "w_mem_kv"]
    ride_ffn_in = ["w_ffn_out"]
    placed = {n: _cast_place(wts[n][0], BIG_AXIS[n], place, BF, "place_" + n)
              for n in ride_in_proj + ride_attention + ride_ffn_in}

    def fetch(names):
        return _riding_fetch([placed[n] for n in names], [BIG_AXIS[n] for n in names])

    def hand_on(names, bufs, tag):
        W.update(zip(names, _hand_on(bufs, [BIG_AXIS[n] for n in names], "gather_hand_on" + tag)))

    h1 = _rmsnorm(x, gains["norm_mix"], "norm_mix_fwd")
    proj, *bufs_a = _matmul(h1, W["w_in"], tn=1280, name="in_proj", rider=fetch(ride_in_proj))
    o_a, o_a32, *bufs_b = _sb_fwd(proj, fetch(ride_attention))
    y_b = _conv_fwd(proj, conv_full)
    hand_on(ride_in_proj + ride_attention, bufs_a + bufs_b, "")
    br_a, br_b, merged = _branches_merge(o_a, y_b, W["w_branch_a"], W["w_branch_b"], proj)
    x1 = _matmul(merged, W["w_mix_out"], tn=1024, out_dtype=F32, resid=x, name="mix_out")

    hq, q_m = _norm_matmul(x1, gains["norm_mem_q"], W["w_mem_q"], tn=1024, name="norm_mem_q")
    mn, kv = _norm_matmul(mem, gains["norm_mem_kv"], W["w_mem_kv"], tn=1024, name="norm_mem_kv")
    o_m = _mem_fwd(q_m, kv)
    x2 = _matmul(o_m, W["w_mem_o"], tn=1024, out_dtype=F32, resid=x1, name="mem_o")

    hf, gate, up, act, *bufs_c = _norm_ffn_in_swiglu(x2, gains["norm_ffn"], W["w_ffn_in"], rider=fetch(ride_ffn_in))
    hand_on(ride_ffn_in, bufs_c, "_late")

    dx3, dx3_b, dg_final, loss_part = _ffn_out_loss(act, W["w_ffn_out"], x2, gains["norm_final"], target)

    def halves_of(names):
        views = []
        for n in names:
            r, cdim = gw[n].shape
            views.append(gw[n].reshape(1, 2, r // 2, cdim) if BIG_AXIS[n] == 1
                         else gw[n].reshape(N_CHIPS, 2, r // (2 * N_CHIPS), cdim))
        return views

    def pair_adds(names, views, got):
        return [_pair_add(v, g, core, "pair_add_" + n) for n, v, g in zip(names, views, got)]

    def chip_sums(names, sums, slots):
        return [_chip_sum(sm, sl, BIG_AXIS[n], place, "chip_sum_" + n) for n, sm, sl in zip(names, sums, slots)]

    gw = {"w_ffn_out": _matmul(act, dx3_b, ta=True, tm=1408, tn=512, name="gw_ffn_out")}
    first_views = halves_of(["w_ffn_out"])
    dgu, *first_got = _d_act_swiglu(dx3_b, W["w_ffn_out"], gate, up, rider=_riding_pairs(first_views))
    gw["w_ffn_in"] = _matmul(hf, dgu, ta=True, tn=512, name="gw_ffn_in")
    first_sums = pair_adds(["w_ffn_out"], first_views, first_got)
    dx2, dx2_b, dg_ffn, *first_slots = _matmul_norm_bwd(
        dgu, W["w_ffn_in"], x2, gains["norm_ffn"], dx3, tm=256, bf_copy=True, name="d_hf_norm_bwd",
        rider=_riding_slots(first_sums, [BIG_AXIS["w_ffn_out"]]))

    do_m = _matmul(dx2_b, W["w_mem_o"], tb=True, tn=1024, name="d_o_m")
    gw["w_mem_o"] = _matmul(o_m, dx2_b, ta=True, tn=512, name="gw_mem_o")
    dq_m, dk_m, dv_m = _mem_bwd(q_m, kv, do_m)
    dkv = jnp.concatenate([dk_m, dv_m], axis=-1)
    dx1, dx1_b, dg_q = _matmul_norm_bwd(dq_m, W["w_mem_q"], x1, gains["norm_mem_q"], dx2, tm=512, bf_copy=True,
                                        name="d_hq_norm_bwd")
    gw["w_mem_q"] = _matmul(hq, dq_m, ta=True, tn=512, name="gw_mem_q")
    dmn = _matmul(dkv, W["w_mem_kv"], tb=True, tn=1024, out_dtype=F32, name="d_mn")
    gw["w_mem_kv"] = _matmul(mn, dkv, ta=True, tn=1024, name="gw_mem_kv")
    dg_kv = _norm_gain_grad(mem, dmn, "norm_mem_kv_bwd")

    gw["w_mix_out"] = _matmul(merged, dx1_b, ta=True, tn=512, name="gw_mix_out")
    mid = ["w_ffn_in", "w_mem_o", "w_mem_q", "w_mem_kv", "w_mix_out"]
    mid_views = halves_of(mid)
    dbr_a, dbr_b, dga, dgb, *mid_got = _d_merged_gates(dx1_b, W["w_mix_out"], proj, br_a, br_b,
                                                       rider=_riding_pairs(mid_views))
    do_a = _matmul(dbr_a, W["w_branch_a"], tb=True, name="d_o_a")
    gw["w_branch_a"] = _matmul(o_a, dbr_a, ta=True, tn=512, name="gw_branch_a")
    dy_b = _matmul(dbr_b, W["w_branch_b"], tb=True, name="d_y_b")
    gw["w_branch_b"] = _matmul(y_b, dbr_b, ta=True, tn=512, name="gw_branch_b")
    branches = ["w_branch_a", "w_branch_b"]
    branch_views = halves_of(branches)
    du, dgate_b, dgate_c, dconv, *branch_got = _conv_bwd(proj, conv_full, dy_b, rider=_riding_pairs(branch_views))
    early = mid + branches
    early_sums = pair_adds(mid, mid_views, mid_got) + pair_adds(branches, branch_views, branch_got)
    dq, dk, dv, *early_slots = _sb_bwd(proj, o_a32, do_a, _riding_slots(early_sums, [BIG_AXIS[n] for n in early]))

    def assemble(*parts):
        return jnp.concatenate([p.astype(BF) for p in parts], axis=-1)

    hw = dq.shape[1]
    dproj = _rowwise(assemble, [(t, hw, 0) for t in (dq, dk, dv, du, dgate_b, dgate_c)] + [(dga, d, 0), (dgb, d, 0)],
                     [(proj.shape[1], BF)], rows=s, tm=256, name="assemble_dproj")[0]
    gw["w_in"] = _matmul(h1, dproj, ta=True, tn=640, name="gw_in")
    in_views = halves_of(["w_in"])
    in_sums = pair_adds(["w_in"], in_views, _pair_exchange(in_views, "grad_pair_exchange_in"))
    grad_x, dg_mix, *in_slots = _matmul_norm_bwd(dproj, W["w_in"], x, gains["norm_mix"], dx1, tm=256, bf_copy=False,
                                                 name="d_h1_norm_bwd",
                                                 rider=_riding_slots(in_sums, [BIG_AXIS["w_in"]]))

    halves = dict(zip(early, chip_sums(early, early_sums, early_slots)))
    halves.update(zip(["w_ffn_out"], chip_sums(["w_ffn_out"], first_sums, first_slots)))
    halves.update(zip(["w_in"], chip_sums(["w_in"], in_sums, in_slots)))
    both = _half_swap([halves[n] for n in BIG])
    grads = {n: b.reshape(wts[n].shape[1:]) for n, b in zip(BIG, both)}

    small_g = [dg_mix, dg_q, dg_kv, dg_ffn, dg_final]
    pack = _pack_small(small_g, dconv[:CONV_K])
    pack = pack.at[ROW_LOSS].set(jnp.broadcast_to(loss_part[0, :1], (d,)))
    red = _allreduce_small(pack)
    loss = red[ROW_LOSS, 0]
    cw = conv_w.shape[2]
    conv_g = lax.dynamic_slice(red, (ROW_CONV, chip * cw), (CONV_K, cw))
    small_grad = _pack_small([red[i] for i in range(len(NORMS))], conv_g)
    small = [_pack_small([t[n] for n in NORMS], t["conv_w"][0]) for t in (wts, mom, var)]
    s_delta, s_m, s_v, _ = _adamw(small[0], small_grad, small[1], small[2], "adamw_small")

    out = {"grad": {}, "delta": {}, "new_m": {}, "new_v": {}}
    for n in BIG:
        shp = wts[n].shape
        dl, m2, v2, g_out = _adamw(wts[n][0], grads[n], mom[n][0], var[n][0], "adamw_" + n)
        out["grad"][n] = g_out.reshape(shp)
        out["delta"][n], out["new_m"][n], out["new_v"][n] = dl.reshape(shp), m2.reshape(shp), v2.reshape(shp)
    for key, blk in (("grad", small_grad), ("delta", s_delta), ("new_m", s_m), ("new_v", s_v)):
        for i, n in enumerate(NORMS):
            out[key][n] = blk[i].reshape(wts[n].shape)
        out[key]["conv_w"] = blk[ROW_CONV:ROW_CONV + CONV_K, :cw].reshape(conv_w.shape)

    return (loss, grad_x[None], *[out["grad"][n] for n in ORDER], *[out["delta"][n] for n in ORDER],
            *[out["new_m"][n] for n in ORDER], *[out["new_v"][n] for n in ORDER])
```

```python
import math

import jax
import jax.numpy as jnp
from jax import lax
from jax.experimental import pallas as pl
from jax.experimental.pallas import tpu as pltpu

BF = jnp.bfloat16
F32 = jnp.float32
MESH = pl.DeviceIdType.MESH

SB_HEAD_DIM = 64
LANES = 128
MEM_HEADS = 4
CONV_K = 3
CONV_ROWS = 8
EPS = 1e-6
N_CHIPS = 4
N_DEV = 8
VMEM_LIMIT = 56 * 1024 * 1024
STREAM_BLOCK_ELEMS = 1 << 20

ADAM_LR = 0.001
ADAM_B1 = 0.9
ADAM_B2 = 0.999
ADAM_EPS = 1e-08
ADAM_WD = 0.01
ADAM_STEP = 10

SMALL_ROWS = 16
ROW_CONV = 5
ROW_LOSS = 8


def _params(sem=None, **kw):
    return pltpu.CompilerParams(dimension_semantics=sem, vmem_limit_bytes=VMEM_LIMIT, **kw)


def _tile(dim, pref):
    if dim <= pref:
        return dim
    for step in (LANES, 8):
        t = (pref // step) * step
        while t >= step:
            if dim % t == 0:
                return t
            t -= step
    raise ValueError(f"no tile of {dim} under {pref}")


def _matmul(a, b, *, ta=False, tb=False, tm=1024, tn=512, out_dtype=BF, resid=None, rider=None, name):
    if ta:
        kdim, m = a.shape
    else:
        m, kdim = a.shape
    n = b.shape[0] if tb else b.shape[1]
    tm, tn = _tile(m, tm), _tile(n, tn)
    a_spec = pl.BlockSpec((kdim, tm), lambda i, j: (0, i)) if ta else pl.BlockSpec((tm, kdim), lambda i, j: (i, 0))
    b_spec = pl.BlockSpec((tn, kdim), lambda i, j: (j, 0)) if tb else pl.BlockSpec((kdim, tn), lambda i, j: (0, j))
    o_spec = pl.BlockSpec((tm, tn), lambda i, j: (i, j))
    dims = (((0 if ta else 1,), (1 if tb else 0,)), ((), ()))
    has_res = resid is not None
    n_in = 2 + has_res

    def body(*refs):
        ins, (o_ref,) = _host_refs(refs, rider, n_in, 1)
        av, bv = ins[0][...], ins[1][...]
        if av.dtype != BF:
            av = av.astype(BF)
        if bv.dtype != BF:
            bv = bv.astype(BF)
        acc = lax.dot_general(av, bv, dims, preferred_element_type=F32)
        if has_res:
            acc = ins[2][...] + acc
        o_ref[...] = acc.astype(o_ref.dtype)
        if rider:
            rider.run(refs, n_in, 1, (pl.program_id(0) == 0) & (pl.program_id(1) == 0),
                      (pl.program_id(0) == m // tm - 1) & (pl.program_id(1) == n // tn - 1))

    res = pl.pallas_call(
        body, name=name, grid=(m // tm, n // tn),
        compiler_params=_params(("arbitrary", "arbitrary") if rider else ("parallel", "parallel"),
                                has_side_effects=rider is not None),
        **_with_rider(rider, n_in, 1, [a_spec, b_spec] + ([o_spec] if has_res else []), [o_spec],
                      [jax.ShapeDtypeStruct((m, n), out_dtype)]),
    )(*([a, b] + ([resid] if has_res else []) + (rider.arrays if rider else [])))
    return res if rider else res[0]


def _rowwise(fn, ins, outs, *, rows, tm, name, accs=()):
    tm = _tile(rows, tm)
    in_specs, args = [], []
    for arr, cols, cb in ins:
        if cols is None:
            in_specs.append(pl.BlockSpec(arr.shape, lambda i, nd=arr.ndim: (0,) * nd))
        else:
            in_specs.append(pl.BlockSpec((tm, cols), lambda i, cb=cb: (i, cb)))
        args.append(arr)
    out_specs = [pl.BlockSpec((tm, cols), lambda i: (i, 0)) for cols, _ in outs]
    out_shape = [jax.ShapeDtypeStruct((rows, cols), dt) for cols, dt in outs]
    for r, c in accs:
        out_specs.append(pl.BlockSpec((r, c), lambda i: (0, 0)))
        out_shape.append(jax.ShapeDtypeStruct((r, c), F32))
    n_in, n_out = len(ins), len(outs)

    def body(*refs):
        res = fn(*[r[...] for r in refs[:n_in]])
        if not isinstance(res, (tuple, list)):
            res = (res,)
        refs = refs[n_in:]
        for o_ref, val in zip(refs[:n_out], res[:n_out]):
            o_ref[...] = val.astype(o_ref.dtype)
        first = pl.program_id(0) == 0
        for a_ref, val in zip(refs[n_out:], res[n_out:]):
            @pl.when(first)
            def _(a_ref=a_ref, val=val):
                a_ref[...] = val

            @pl.when(jnp.logical_not(first))
            def _(a_ref=a_ref, val=val):
                a_ref[...] += val

    res = pl.pallas_call(
        body, name=name, grid=(rows // tm,), in_specs=in_specs, out_specs=out_specs, out_shape=out_shape,
        compiler_params=_params(("arbitrary",) if accs else ("parallel",)),
    )(*args)
    return res


def _rstd(xf):
    return lax.rsqrt(jnp.mean(xf * xf, axis=-1, keepdims=True) + EPS)


def _norm_gain_grad(x, dy, name):
    rows, d = x.shape

    def fn(xv, dyv):
        return (jnp.sum(dyv.astype(F32) * (xv * _rstd(xv)), axis=0, keepdims=True),)

    return _rowwise(fn, [(x, d, 0), (dy, d, 0)], [], rows=rows, tm=512, name=name, accs=[(1, d)])[0]


def _matmul_norm_bwd(a, w, x, g, resid, *, tm, bf_copy, name, rider=None):
    m, kdim = a.shape
    d = w.shape[0]
    tm = _tile(m, tm)
    row = lambda i: (i, 0)
    whole = lambda i: (0, 0)
    n_out = 2 + bf_copy

    def body(*refs):
        (a_ref, w_ref, x_ref, g_ref, r_ref), outs = _host_refs(refs, rider, 5, n_out)
        dy = lax.dot_general(a_ref[...], w_ref[...], NT, preferred_element_type=F32)
        xv = x_ref[...]
        r = _rstd(xv)
        xn = xv * r
        dxn = dy * g_ref[...]
        dx = r_ref[...] + r * (dxn - xn * jnp.mean(dxn * xn, axis=-1, keepdims=True))
        outs[0][...] = dx
        if bf_copy:
            outs[1][...] = dx.astype(BF)
        dg = jnp.sum(dy * xn, axis=0, keepdims=True)
        first = pl.program_id(0) == 0

        @pl.when(first)
        def _():
            outs[-1][...] = dg

        @pl.when(jnp.logical_not(first))
        def _():
            outs[-1][...] += dg

        if rider:
            rider.run(refs, 5, n_out, first, pl.program_id(0) == m // tm - 1)

    tok = pl.BlockSpec((tm, d), row)
    out_specs = [tok] + ([tok] if bf_copy else []) + [pl.BlockSpec((1, d), whole)]
    out_shape = ([jax.ShapeDtypeStruct((m, d), F32)] + ([jax.ShapeDtypeStruct((m, d), BF)] if bf_copy else [])
                 + [jax.ShapeDtypeStruct((1, d), F32)])
    return pl.pallas_call(
        body, name=name, grid=(m // tm,),
        compiler_params=_params(("arbitrary",), has_side_effects=rider is not None),
        **_with_rider(
            rider, 5, n_out,
            [pl.BlockSpec((tm, kdim), row), pl.BlockSpec((d, kdim), whole), tok, pl.BlockSpec((1, d), whole), tok],
            out_specs, out_shape),
    )(a, w, x, g, resid, *(rider.arrays if rider else []))


def _norm_ffn_in_swiglu(x, g, w, *, tm=1024, tn=1408, rider=None):
    m, kdim = x.shape
    f = w.shape[1] // 2
    tm, tn = _tile(m, tm), _tile(f, tn)
    nj = f // tn

    def body(*refs):
        (x_ref, g_ref, wg_ref, wu_ref), (h_ref, gate_ref, up_ref, act_ref) = _host_refs(refs, rider, 4, 4)
        if rider:
            rider.run(refs, 4, 4, (pl.program_id(0) == 0) & (pl.program_id(1) == 0),
                      (pl.program_id(0) == m // tm - 1) & (pl.program_id(1) == nj - 1))

        @pl.when(pl.program_id(1) == 0)
        def _():
            xv = x_ref[...]
            h_ref[...] = (xv * _rstd(xv) * g_ref[...]).astype(h_ref.dtype)

        hv = h_ref[...]
        gate = jnp.dot(hv, wg_ref[...], preferred_element_type=F32)
        up = jnp.dot(hv, wu_ref[...], preferred_element_type=F32)
        gate_ref[...] = gate.astype(gate_ref.dtype)
        up_ref[...] = up.astype(up_ref.dtype)
        act_ref[...] = (gate * jax.nn.sigmoid(gate) * up).astype(act_ref.dtype)

    tile = pl.BlockSpec((tm, tn), lambda i, j: (i, j))
    rows = pl.BlockSpec((tm, kdim), lambda i, j: (i, 0))
    return pl.pallas_call(
        body, name="norm_ffn_in_swiglu", grid=(m // tm, nj),
        compiler_params=_params(("arbitrary", "arbitrary") if rider else ("parallel", "arbitrary"),
                                has_side_effects=rider is not None),
        **_with_rider(
            rider, 4, 4,
            [rows, pl.BlockSpec((1, kdim), lambda i, j: (0, 0)), pl.BlockSpec((kdim, tn), lambda i, j: (0, j)),
             pl.BlockSpec((kdim, tn), lambda i, j: (0, nj + j))],
            [rows, tile, tile, tile],
            [jax.ShapeDtypeStruct((m, kdim), BF)] + [jax.ShapeDtypeStruct((m, f), BF)] * 3),
    )(x, g, w, w, *(rider.arrays if rider else []))


def _norm_matmul(x, g, w, *, tm=1024, tn, name):
    m, kdim = x.shape
    n = w.shape[1]
    tm, tn = _tile(m, tm), _tile(n, tn)

    def body(x_ref, g_ref, w_ref, h_ref, o_ref):
        @pl.when(pl.program_id(1) == 0)
        def _():
            xv = x_ref[...]
            h_ref[...] = (xv * _rstd(xv) * g_ref[...]).astype(h_ref.dtype)

        o_ref[...] = jnp.dot(h_ref[...], w_ref[...], preferred_element_type=F32).astype(o_ref.dtype)

    rows = pl.BlockSpec((tm, kdim), lambda i, j: (i, 0))
    return pl.pallas_call(
        body, name=name, grid=(m // tm, n // tn),
        in_specs=[rows, pl.BlockSpec((1, kdim), lambda i, j: (0, 0)), pl.BlockSpec((kdim, tn), lambda i, j: (0, j))],
        out_specs=[rows, pl.BlockSpec((tm, tn), lambda i, j: (i, j))],
        out_shape=[jax.ShapeDtypeStruct((m, kdim), BF), jax.ShapeDtypeStruct((m, n), BF)],
        compiler_params=_params(("parallel", "arbitrary")),
    )(x, g, w)


def _ffn_out_loss(act, w, resid, g, target, *, tm=512):
    m, f = act.shape
    d = w.shape[1]
    tm = _tile(m, tm)
    row = lambda i: (i, 0)
    whole = lambda i: (0, 0)

    def body(a_ref, w_ref, r_ref, g_ref, t_ref, dx_ref, dxb_ref, dg_ref, loss_ref):
        xv = r_ref[...] + jnp.dot(a_ref[...], w_ref[...], preferred_element_type=F32)
        gv = g_ref[...]
        r = _rstd(xv)
        xn = xv * r
        err = xn * gv - t_ref[...]
        loss = 0.5 * jnp.sum(jnp.mean(err * err, axis=-1, keepdims=True), axis=0, keepdims=True)
        dyv = err * (1.0 / d)
        dxn = dyv * gv
        dx = r * (dxn - xn * jnp.mean(dxn * xn, axis=-1, keepdims=True))
        dx_ref[...] = dx
        dxb_ref[...] = dx.astype(dxb_ref.dtype)
        dg = jnp.sum(dyv * xn, axis=0, keepdims=True)
        loss_b = jnp.broadcast_to(loss, (1, LANES))
        first = pl.program_id(0) == 0

        @pl.when(first)
        def _():
            dg_ref[...] = dg
            loss_ref[...] = loss_b

        @pl.when(jnp.logical_not(first))
        def _():
            dg_ref[...] += dg
            loss_ref[...] += loss_b

    tok = pl.BlockSpec((tm, d), row)
    return pl.pallas_call(
        body, name="ffn_out_loss", grid=(m // tm,),
        in_specs=[pl.BlockSpec((tm, f), row), pl.BlockSpec((f, d), whole), tok, pl.BlockSpec((1, d), whole), tok],
        out_specs=[tok, tok, pl.BlockSpec((1, d), whole), pl.BlockSpec((1, LANES), whole)],
        out_shape=[jax.ShapeDtypeStruct((m, d), F32), jax.ShapeDtypeStruct((m, d), BF),
                   jax.ShapeDtypeStruct((1, d), F32), jax.ShapeDtypeStruct((1, LANES), F32)],
        compiler_params=_params(("arbitrary",)),
    )(act, w, resid, g, target)


def _d_act_swiglu(dx, w, gate, up, *, tm=256, rider=None):
    m, d = dx.shape
    f = w.shape[0]
    tm = _tile(m, tm)
    row = lambda i: (i, 0)

    def body(*refs):
        (dx_ref, w_ref, gate_ref, up_ref), (o_ref,) = _host_refs(refs, rider, 4, 1)
        if rider:
            rider.run(refs, 4, 1, pl.program_id(0) == 0, pl.program_id(0) == m // tm - 1)
        da = lax.dot_general(dx_ref[...], w_ref[...], NT, preferred_element_type=F32)
        gv, uv = gate_ref[...].astype(F32), up_ref[...].astype(F32)
        sg = jax.nn.sigmoid(gv)
        dgate = da * uv * (sg * (1.0 + gv * (1.0 - sg)))
        o_ref[...] = jnp.concatenate([dgate, da * (gv * sg)], axis=-1).astype(o_ref.dtype)

    return pl.pallas_call(
        body, name="d_act_swiglu", grid=(m // tm,),
        compiler_params=_params(("arbitrary",) if rider else ("parallel",), has_side_effects=rider is not None),
        **_with_rider(
            rider, 4, 1,
            [pl.BlockSpec((tm, d), row), pl.BlockSpec((f, d), lambda i: (0, 0)),
             pl.BlockSpec((tm, f), row), pl.BlockSpec((tm, f), row)],
            [pl.BlockSpec((tm, 2 * f), row)], [jax.ShapeDtypeStruct((m, 2 * f), BF)]),
    )(dx, w, gate, up, *(rider.arrays if rider else []))


GATE_A_BLK, GATE_B_BLK = 3, 4


def _branches_merge(o_a, y_b, w_a, w_b, proj, *, tm=1024):
    m, kdim = o_a.shape
    d = w_a.shape[1]
    tm = _tile(m, tm)
    row = lambda i: (i, 0)

    def body(a_ref, b_ref, wa_ref, wb_ref, ga_ref, gb_ref, bra_ref, brb_ref, merged_ref):
        bra = jnp.dot(a_ref[...], wa_ref[...], preferred_element_type=F32)
        brb = jnp.dot(b_ref[...], wb_ref[...], preferred_element_type=F32)
        bra_ref[...] = bra.astype(bra_ref.dtype)
        brb_ref[...] = brb.astype(brb_ref.dtype)
        merged = jax.nn.sigmoid(ga_ref[...].astype(F32)) * bra + jax.nn.sigmoid(gb_ref[...].astype(F32)) * brb
        merged_ref[...] = merged.astype(merged_ref.dtype)

    tok = pl.BlockSpec((tm, d), row)
    return pl.pallas_call(
        body, name="branches_merge", grid=(m // tm,),
        in_specs=[pl.BlockSpec((tm, kdim), row), pl.BlockSpec((tm, kdim), row),
                  pl.BlockSpec((kdim, d), lambda i: (0, 0)), pl.BlockSpec((kdim, d), lambda i: (0, 0)),
                  pl.BlockSpec((tm, d), lambda i: (i, GATE_A_BLK)), pl.BlockSpec((tm, d), lambda i: (i, GATE_B_BLK))],
        out_specs=[tok, tok, tok], out_shape=[jax.ShapeDtypeStruct((m, d), BF)] * 3,
        compiler_params=_params(("parallel",)),
    )(o_a, y_b, w_a, w_b, proj, proj)


def _d_merged_gates(dx, w_mix, proj, br_a, br_b, *, tm=512, rider=None):
    m, d = dx.shape
    tm = _tile(m, tm)
    row = lambda i: (i, 0)

    def body(*refs):
        ins, (dbra_ref, dbrb_ref, dga_ref, dgb_ref) = _host_refs(refs, rider, 6, 4)
        dx_ref, w_ref, ga_ref, gb_ref, bra_ref, brb_ref = ins
        if rider:
            rider.run(refs, 6, 4, pl.program_id(0) == 0, pl.program_id(0) == m // tm - 1)
        dm = lax.dot_general(dx_ref[...], w_ref[...], NT, preferred_element_type=F32)
        sa, sb = jax.nn.sigmoid(ga_ref[...].astype(F32)), jax.nn.sigmoid(gb_ref[...].astype(F32))
        dbra_ref[...] = (dm * sa).astype(dbra_ref.dtype)
        dbrb_ref[...] = (dm * sb).astype(dbrb_ref.dtype)
        dga_ref[...] = (dm * bra_ref[...].astype(F32) * (sa * (1.0 - sa))).astype(dga_ref.dtype)
        dgb_ref[...] = (dm * brb_ref[...].astype(F32) * (sb * (1.0 - sb))).astype(dgb_ref.dtype)

    tok = pl.BlockSpec((tm, d), row)
    return pl.pallas_call(
        body, name="d_merged_gates", grid=(m // tm,),
        compiler_params=_params(("arbitrary",) if rider else ("parallel",), has_side_effects=rider is not None),
        **_with_rider(
            rider, 6, 4,
            [tok, pl.BlockSpec((d, d), lambda i: (0, 0)),
             pl.BlockSpec((tm, d), lambda i: (i, GATE_A_BLK)), pl.BlockSpec((tm, d), lambda i: (i, GATE_B_BLK)),
             tok, tok],
            [tok] * 4, [jax.ShapeDtypeStruct((m, d), BF)] * 4),
    )(dx, w_mix, proj, proj, br_a, br_b, *(rider.arrays if rider else []))


SB_TK = 128
SB_KT = 2


def _sb_consts(tq):
    tk = SB_TK
    diff = lax.broadcasted_iota(jnp.int32, (tq, tk), 1) - lax.broadcasted_iota(jnp.int32, (tq, tk), 0)
    rj = lax.broadcasted_iota(jnp.int32, (2 * tk, 2 * tk), 0) & (tk - 1)
    cj = lax.broadcasted_iota(jnp.int32, (2 * tk, 2 * tk), 1)
    ones_half = cj >= tk
    later = jnp.where((rj > cj) | ones_half, 1.0, 0.0).astype(BF)
    later_incl = jnp.where((rj >= cj) | ones_half, 1.0, 0.0).astype(BF)
    return diff, later, later_incl


def _split_dot(val, rhs_twice):
    hi = val.astype(BF)
    lo = (val - hi.astype(F32)).astype(BF)
    return jnp.dot(jnp.concatenate([hi, lo], axis=1), rhs_twice, preferred_element_type=F32)


def _log_terms(z):
    sp = jnp.maximum(z, 0.0) + jnp.log(1.0 + jnp.exp(-jnp.abs(z)))
    return z - sp, sp


NT = (((1,), (1,)), ((), ()))
TN = (((0,), (0,)), ((), ()))


def _head_lane_masks(rows):
    lane = lax.broadcasted_iota(jnp.int32, (rows, LANES), 1)
    first = jnp.where(lane < SB_HEAD_DIM, 1.0, 0.0)
    return first.astype(BF), (1.0 - first).astype(BF)


DEAD_LOG = 104.0


def _walk_back(i, step, state, carries_of):
    def alive(st):
        c0, c1 = carries_of(st)
        return jnp.min(jnp.minimum(c0, c1)) < DEAD_LOG

    def cond(loop):
        done, live, _ = loop
        return jnp.logical_and(done < i, live)

    def body(loop):
        done, _, st = loop
        st = step(i - 1 - done, st)
        return done + 1, alive(st), st

    return lax.while_loop(cond, body, (jnp.int32(0), alive(state), state))[2]


def _tail(a, r0):
    return a if r0 == 0 else a[r0:]


def _add_tail(a, r0, delta):
    return a + delta if r0 == 0 else jnp.concatenate([a[:r0], a[r0:] + delta], axis=0)


def _both_heads(tile, masks):
    return jnp.concatenate([tile * masks[0], tile * masks[1]], axis=0)


def _with_rider(rider, n_in, n_out, in_specs, out_specs, out_shape, scratch=()):
    kw = dict(in_specs=list(in_specs), out_specs=list(out_specs), out_shape=list(out_shape),
              scratch_shapes=list(scratch), input_output_aliases={})
    if rider:
        extra = rider.call_args(n_in, n_out)
        kw["in_specs"] += extra["in_specs"]
        kw["out_specs"] += extra["out_specs"]
        kw["out_shape"] += extra["out_shape"]
        kw["scratch_shapes"] += extra["scratch"]
        kw["input_output_aliases"] = extra["aliases"]
    return kw


def _sb_fwd(proj, rider=None):
    s = proj.shape[0]
    tk, tq = SB_TK, SB_KT * SB_TK
    n_pairs = 4
    scale = 1.0 / math.sqrt(SB_HEAD_DIM)

    def body(*refs):
        (q_ref, k_ref, v_ref), (o_ref, o32_ref) = _host_refs(refs, rider, 3, 2)
        i = pl.program_id(1)
        diff, later, _ = _sb_consts(tq)
        qs = (q_ref[...].astype(F32) * scale).astype(BF)
        lane_masks = _head_lane_masks(tk)

        def step(g, state, masked):
            tiles = list(reversed(range(SB_KT)))
            chains = [(t, h) for t in tiles for h in range(2)]
            rows = {t: pl.ds(pl.multiple_of((g * SB_KT + t) * tk, tk), tk) for t in tiles}
            ks = {t: _both_heads(k_ref[rows[t], :], lane_masks) for t in tiles}
            vs = {t: _both_heads(v_ref[rows[t], :], lane_masks) for t in tiles}
            r0 = {t: t * tk if masked else 0 for t in tiles}
            allowed = {t: _tail(diff, r0[t]) < -t * tk for t in tiles}
            zs = {t: lax.dot_general(_tail(qs, r0[t]), ks[t], NT, preferred_element_type=F32) for t in tiles}
            logs = {}
            for t, h in chains:
                log_b, sp = _log_terms(zs[t][:, h * tk:(h + 1) * tk])
                logs[t, h] = (log_b, jnp.where(allowed[t], sp, 0.0) if masked else sp)
            sums = {c: _split_dot(logs[c][1], later) for c in chains}
            carries = list(state[0])
            ws = {}
            for t, h in chains:
                w = jnp.exp(logs[t, h][0] - (sums[t, h][:, :tk] + _tail(carries[h], r0[t])))
                ws[t, h] = (jnp.where(allowed[t], w, 0.0) if masked else w).astype(BF)
                carries[h] = _add_tail(carries[h], r0[t], sums[t, h][:, tk:])
            acc = state[1]
            for t in tiles:
                acc = _add_tail(acc, r0[t], jnp.dot(jnp.concatenate([ws[t, 0], ws[t, 1]], axis=1), vs[t],
                                                    preferred_element_type=F32))
            return tuple(carries), acc

        zero = jnp.zeros((tq, LANES), F32)
        state = step(i, ((zero, zero), zero), True)
        state = _walk_back(i, lambda g, st: step(g, st, False), state, lambda st: st[0])
        o_ref[...] = state[1].astype(o_ref.dtype)
        o32_ref[...] = state[1]
        if rider:
            rider.run(refs, 3, 2, (pl.program_id(0) == 0) & (i == 0),
                      (pl.program_id(0) == n_pairs - 1) & (i == s // tq - 1))

    tok = pl.BlockSpec((tq, LANES), lambda p, i: (i, p))
    return pl.pallas_call(
        body, name="sb_attn_fwd", grid=(n_pairs, s // tq),
        compiler_params=_params(("arbitrary", "arbitrary"), has_side_effects=rider is not None),
        **_with_rider(
            rider, 3, 2,
            [tok, pl.BlockSpec((s, LANES), lambda p, i: (0, n_pairs + p)),
             pl.BlockSpec((s, LANES), lambda p, i: (0, 2 * n_pairs + p))],
            [tok, tok],
            [jax.ShapeDtypeStruct((s, n_pairs * LANES), BF), jax.ShapeDtypeStruct((s, n_pairs * LANES), F32)]),
    )(proj, proj, proj, *(rider.arrays if rider else []))


def _sb_bwd(proj, o32, do_a, rider=None):
    s = proj.shape[0]
    tk, tq = SB_TK, SB_KT * SB_TK
    n_pairs = 4
    scale = 1.0 / math.sqrt(SB_HEAD_DIM)

    def body(*refs):
        (q_ref, k_ref, v_ref, o_ref, do_ref), (dq_ref, dk_ref, dv_ref) = _host_refs(refs, rider, 5, 3)
        i = pl.program_id(1)

        @pl.when(i == 0)
        def _():
            dk_ref[...] = jnp.zeros_like(dk_ref)
            dv_ref[...] = jnp.zeros_like(dv_ref)

        diff, later, later_incl = _sb_consts(tq)
        qs = (q_ref[...].astype(F32) * scale).astype(BF)
        do2 = do_ref[...]
        prod = do2.astype(F32) * o_ref[...]
        lane_masks = _head_lane_masks(tk)
        first_head = lax.broadcasted_iota(jnp.int32, (tq, LANES), 1) < SB_HEAD_DIM
        totals = [jnp.broadcast_to(jnp.sum(jnp.where(keep, prod, 0.0), axis=-1, keepdims=True), (tq, tk))
                  for keep in (first_head, jnp.logical_not(first_head))]
        first_head_k = first_head[:tk]

        def step(g_idx, state, masked):
            tiles = list(reversed(range(SB_KT)))
            chains = [(t, h) for t in tiles for h in range(2)]
            rows = {t: pl.ds(pl.multiple_of((g_idx * SB_KT + t) * tk, tk), tk) for t in tiles}
            ks = {t: _both_heads(k_ref[rows[t], :], lane_masks) for t in tiles}
            vs = {t: _both_heads(v_ref[rows[t], :], lane_masks) for t in tiles}
            r0 = {t: t * tk if masked else 0 for t in tiles}
            allowed = {t: _tail(diff, r0[t]) < -t * tk for t in tiles}
            zs = {t: lax.dot_general(_tail(qs, r0[t]), ks[t], NT, preferred_element_type=F32) for t in tiles}
            dws = {t: lax.dot_general(_tail(do2, r0[t]), vs[t], NT, preferred_element_type=F32) for t in tiles}
            logs = {}
            for t, h in chains:
                log_b, sp = _log_terms(zs[t][:, h * tk:(h + 1) * tk])
                logs[t, h] = (log_b, jnp.where(allowed[t], sp, 0.0) if masked else sp)
            sums = {c: _split_dot(logs[c][1], later) for c in chains}
            c_log, c_g = list(state[0]), list(state[1])
            ws, gs = {}, {}
            for t, h in chains:
                w = jnp.exp(logs[t, h][0] - (sums[t, h][:, :tk] + _tail(c_log[h], r0[t])))
                ws[t, h] = (jnp.where(allowed[t], w, 0.0) if masked else w).astype(BF)
                c_log[h] = _add_tail(c_log[h], r0[t], sums[t, h][:, tk:])
                gs[t, h] = ws[t, h].astype(F32) * dws[t][:, h * tk:(h + 1) * tk]
            gsums = {c: _split_dot(gs[c], later_incl) for c in chains}
            dzs = {}
            for t, h in chains:
                beta = jnp.exp(logs[t, h][0])
                earlier = _tail(totals[h], r0[t]) - (gsums[t, h][:, :tk] + _tail(c_g[h], r0[t]))
                dz = gs[t, h] * (1.0 - beta) - earlier * beta
                dzs[t, h] = (jnp.where(allowed[t], dz, 0.0) if masked else dz).astype(BF)
                c_g[h] = _add_tail(c_g[h], r0[t], gsums[t, h][:, tk:])
            dq = state[2]
            for t in tiles:
                dz_both = jnp.concatenate([dzs[t, 0], dzs[t, 1]], axis=1)
                w_both = jnp.concatenate([ws[t, 0], ws[t, 1]], axis=1)
                dq = _add_tail(dq, r0[t], jnp.dot(dz_both, ks[t], preferred_element_type=F32))
                dk2 = lax.dot_general(dz_both, _tail(qs, r0[t]), TN, preferred_element_type=F32)
                dv2 = lax.dot_general(w_both, _tail(do2, r0[t]), TN, preferred_element_type=F32)
                dk_ref[rows[t], :] += jnp.where(first_head_k, dk2[:tk], dk2[tk:])
                dv_ref[rows[t], :] += jnp.where(first_head_k, dv2[:tk], dv2[tk:])
            return tuple(c_log), tuple(c_g), dq

        zero = jnp.zeros((tq, LANES), F32)
        state = step(i, ((zero, zero), (zero, zero), zero), True)
        state = _walk_back(i, lambda g, st: step(g, st, False), state, lambda st: st[0])
        dq_ref[...] = (state[2] * scale).astype(dq_ref.dtype)
        if rider:
            rider.run(refs, 5, 3, (pl.program_id(0) == 0) & (i == 0),
                      (pl.program_id(0) == n_pairs - 1) & (i == s // tq - 1))

    width = n_pairs * LANES
    tok = pl.BlockSpec((tq, LANES), lambda p, i: (i, p))
    return pl.pallas_call(
        body, name="sb_attn_bwd", grid=(n_pairs, s // tq),
        compiler_params=_params(("arbitrary", "arbitrary"), has_side_effects=rider is not None),
        **_with_rider(
            rider, 5, 3,
            [tok, pl.BlockSpec((s, LANES), lambda p, i: (0, n_pairs + p)),
             pl.BlockSpec((s, LANES), lambda p, i: (0, 2 * n_pairs + p)), tok, tok],
            [tok, pl.BlockSpec((s, LANES), lambda p, i: (0, p)), pl.BlockSpec((s, LANES), lambda p, i: (0, p))],
            [jax.ShapeDtypeStruct((s, width), BF), jax.ShapeDtypeStruct((s, width), F32),
             jax.ShapeDtypeStruct((s, width), F32)]),
    )(proj, proj, proj, o32, do_a, *(rider.arrays if rider else []))


CONV_COL0 = 12


def _shift_rows(v, k):
    n = v.shape[0]
    row = lax.broadcasted_iota(jnp.int32, v.shape, 0)
    rolled = pltpu.roll(v, k % n, axis=0)
    keep = row >= k if k > 0 else row < n + k
    return jnp.where(keep, rolled, 0.0)


def _conv_specs(s):
    return [pl.BlockSpec((s, LANES), lambda cb: (0, CONV_COL0 + cb)),
            pl.BlockSpec((s, LANES), lambda cb: (0, CONV_COL0 + 4 + cb)),
            pl.BlockSpec((s, LANES), lambda cb: (0, CONV_COL0 + 8 + cb)),
            pl.BlockSpec((CONV_ROWS, LANES), lambda cb: (0, cb))]


def _conv_fwd(proj, conv_w):
    s = proj.shape[0]

    def body(u_ref, gb_ref, gc_ref, w_ref, y_ref):
        cu = gc_ref[...].astype(F32) * u_ref[...].astype(F32)
        w = w_ref[...]
        y = w[0:1] * _shift_rows(cu, 2) + w[1:2] * _shift_rows(cu, 1) + w[2:3] * cu
        y_ref[...] = (gb_ref[...].astype(F32) * y).astype(y_ref.dtype)

    return pl.pallas_call(
        body, name="conv_fwd", grid=(4,), in_specs=_conv_specs(s),
        out_specs=pl.BlockSpec((s, LANES), lambda cb: (0, cb)),
        out_shape=jax.ShapeDtypeStruct((s, 4 * LANES), BF),
        compiler_params=_params(("parallel",)),
    )(proj, proj, proj, conv_w)


def _conv_bwd(proj, conv_w, dy, rider=None):
    s = proj.shape[0]
    n_blocks = 4

    def body(*refs):
        (u_ref, gb_ref, gc_ref, w_ref, dy_ref), (du_ref, dgb_ref, dgc_ref, dw_ref) = _host_refs(refs, rider, 5, 4)
        u, gc = u_ref[...].astype(F32), gc_ref[...].astype(F32)
        dyv = dy_ref[...].astype(F32)
        w = w_ref[...]
        cu = gc * u
        cu1, cu2 = _shift_rows(cu, 1), _shift_rows(cu, 2)
        conv = w[0:1] * cu2 + w[1:2] * cu1 + w[2:3] * cu
        dgb_ref[...] = (dyv * conv).astype(dgb_ref.dtype)
        dc = dyv * gb_ref[...].astype(F32)
        dcu = w[2:3] * dc + w[1:2] * _shift_rows(dc, -1) + w[0:1] * _shift_rows(dc, -2)
        dgc_ref[...] = (dcu * u).astype(dgc_ref.dtype)
        du_ref[...] = (dcu * gc).astype(du_ref.dtype)
        tap_row = lax.broadcasted_iota(jnp.int32, (CONV_ROWS, LANES), 0)
        dw = jnp.zeros((CONV_ROWS, LANES), F32)
        for t, shifted in enumerate((cu2, cu1, cu)):
            dw = jnp.where(tap_row == t, jnp.sum(dc * shifted, axis=0, keepdims=True), dw)
        dw_ref[...] = dw
        if rider:
            rider.run(refs, 5, 4, pl.program_id(0) == 0, pl.program_id(0) == n_blocks - 1)

    col = pl.BlockSpec((s, LANES), lambda cb: (0, cb))
    act = jax.ShapeDtypeStruct((s, 4 * LANES), BF)
    return pl.pallas_call(
        body, name="conv_bwd", grid=(n_blocks,),
        compiler_params=_params(("arbitrary",) if rider else ("parallel",), has_side_effects=rider is not None),
        **_with_rider(
            rider, 5, 4, _conv_specs(s) + [col],
            [col, col, col, pl.BlockSpec((CONV_ROWS, LANES), lambda cb: (0, cb))],
            [act, act, act, jax.ShapeDtypeStruct((CONV_ROWS, n_blocks * LANES), F32)]),
    )(proj, proj, proj, conv_w, dy, *(rider.arrays if rider else []))


def _mem_probs(q, k, scale):
    sc = lax.dot_general(q, k, NT, preferred_element_type=F32) * scale
    p = jnp.exp(sc - jnp.max(sc, axis=-1, keepdims=True))
    return p / jnp.sum(p, axis=-1, keepdims=True)


def _mem_fwd(q_m, kv, tq=2048):
    s, d = q_m.shape
    mlen = kv.shape[0]
    hd = d // MEM_HEADS
    tq = _tile(s, tq)
    scale = 1.0 / math.sqrt(hd)

    def body(q_ref, k_ref, v_ref, o_ref):
        p = _mem_probs(q_ref[...], k_ref[...], scale)
        o_ref[...] = jnp.dot(p.astype(BF), v_ref[...], preferred_element_type=F32).astype(o_ref.dtype)

    return pl.pallas_call(
        body, name="mem_attn_fwd", grid=(MEM_HEADS, s // tq),
        in_specs=[pl.BlockSpec((tq, hd), lambda h, i: (i, h)),
                  pl.BlockSpec((mlen, hd), lambda h, i: (0, h)),
                  pl.BlockSpec((mlen, hd), lambda h, i: (0, MEM_HEADS + h))],
        out_specs=pl.BlockSpec((tq, hd), lambda h, i: (i, h)),
        out_shape=jax.ShapeDtypeStruct((s, d), BF),
        compiler_params=_params(("parallel", "parallel")),
    )(q_m, kv, kv)


def _mem_bwd(q_m, kv, do_m, tq=2048):
    s, d = q_m.shape
    mlen = kv.shape[0]
    hd = d // MEM_HEADS
    tq = _tile(s, tq)
    scale = 1.0 / math.sqrt(hd)

    def body(q_ref, k_ref, v_ref, do_ref, dq_ref, dk_ref, dv_ref):
        q, k, v, do = q_ref[...], k_ref[...], v_ref[...], do_ref[...]
        p = _mem_probs(q, k, scale)
        dp = lax.dot_general(do, v, NT, preferred_element_type=F32)
        ds = p * (dp - jnp.sum(dp * p, axis=-1, keepdims=True)) * scale
        dsb = ds.astype(BF)
        dq_ref[...] = jnp.dot(dsb, k, preferred_element_type=F32).astype(dq_ref.dtype)
        dk = lax.dot_general(dsb, q, TN, preferred_element_type=F32)
        dv = lax.dot_general(p.astype(BF), do, TN, preferred_element_type=F32)
        first = pl.program_id(1) == 0

        @pl.when(first)
        def _():
            dk_ref[...] = dk
            dv_ref[...] = dv

        @pl.when(jnp.logical_not(first))
        def _():
            dk_ref[...] += dk
            dv_ref[...] += dv

    tok = pl.BlockSpec((tq, hd), lambda h, i: (i, h))
    memb = pl.BlockSpec((mlen, hd), lambda h, i: (0, h))
    return pl.pallas_call(
        body, name="mem_attn_bwd", grid=(MEM_HEADS, s // tq),
        in_specs=[tok, memb, pl.BlockSpec((mlen, hd), lambda h, i: (0, MEM_HEADS + h)), tok],
        out_specs=[tok, memb, memb],
        out_shape=[jax.ShapeDtypeStruct((s, d), BF), jax.ShapeDtypeStruct((mlen, d), F32),
                   jax.ShapeDtypeStruct((mlen, d), F32)],
        compiler_params=_params(("parallel", "arbitrary")),
    )(q_m, kv, kv, do_m)


def _place():
    x, y, c = lax.axis_index("x"), lax.axis_index("y"), lax.axis_index("c")
    other_chips = [(1 - x, y), (x, 1 - y), (1 - x, 1 - y)]
    return x, y, c, other_chips


def _chip_no(cx, cy):
    return 2 * cx + cy


HBM_SPEC = pl.BlockSpec(memory_space=pl.ANY)


def _cast_place(shard, axis, place, dtype, name):
    r, c = shard.shape
    tr = _tile(r, max(16, STREAM_BLOCK_ELEMS // c))
    nblk = r // tr
    if axis == 1:
        full, out_map = (r, N_CHIPS * c), lambda i, pref: (i, pref[0])
    else:
        full, out_map = (N_CHIPS * r, c), lambda i, pref: (pref[0] * nblk + i, 0)

    def body(pref, s_ref, o_ref):
        o_ref[...] = s_ref[...].astype(o_ref.dtype)

    return pl.pallas_call(
        body, name=name,
        grid_spec=pltpu.PrefetchScalarGridSpec(
            num_scalar_prefetch=1, grid=(nblk,),
            in_specs=[pl.BlockSpec((tr, c), lambda i, pref: (i, 0))],
            out_specs=pl.BlockSpec((tr, c), out_map)),
        out_shape=jax.ShapeDtypeStruct(full, dtype),
        compiler_params=_params(("parallel",)),
    )(place, shard)


PLACE_STEPS = 8


def _norm_and_place(x, g, shards, axes, place, rider):
    s, d = x.shape
    n = len(shards)
    in_specs = [pl.BlockSpec((s // PLACE_STEPS, d), lambda i, pref: (i, 0)), pl.BlockSpec((1, d), lambda i, pref: (0, 0))]
    out_specs = [pl.BlockSpec((s // PLACE_STEPS, d), lambda i, pref: (i, 0))]
    out_shape = [jax.ShapeDtypeStruct((s, d), BF)]
    for sh, ax in zip(shards, axes):
        r, c = sh.shape
        steps = PLACE_STEPS if (r // PLACE_STEPS) % 16 == 0 else PLACE_STEPS // 2
        rb, rep = r // steps, PLACE_STEPS // steps
        in_specs.append(pl.BlockSpec((rb, c), lambda i, pref, rep=rep: (i // rep, 0)))
        if ax == 1:
            out_specs.append(pl.BlockSpec((rb, c), lambda i, pref, rep=rep: (i // rep, pref[0])))
            out_shape.append(jax.ShapeDtypeStruct((r, N_CHIPS * c), BF))
        else:
            out_specs.append(pl.BlockSpec((rb, c), lambda i, pref, rep=rep, steps=steps: (pref[0] * steps + i // rep, 0)))
            out_shape.append(jax.ShapeDtypeStruct((N_CHIPS * r, c), BF))
    n_in, n_out = 2 + n, 1 + n

    def body(pref, *refs):
        ins, outs = _host_refs(refs, rider, n_in, n_out)
        xv = ins[0][...]
        outs[0][...] = (xv * _rstd(xv) * ins[1][...]).astype(BF)
        for s_ref, o_ref in zip(ins[2:], outs[1:]):
            o_ref[...] = s_ref[...].astype(o_ref.dtype)
        rider.run(refs, n_in, n_out, pl.program_id(0) == 0, pl.program_id(0) == PLACE_STEPS - 1)

    extra = rider.call_args(1 + n_in, n_out)
    res = pl.pallas_call(
        body, name="norm_and_place",
        grid_spec=pltpu.PrefetchScalarGridSpec(
            num_scalar_prefetch=1, grid=(PLACE_STEPS,), in_specs=in_specs + extra["in_specs"],
            out_specs=out_specs + extra["out_specs"], scratch_shapes=extra["scratch"]),
        out_shape=out_shape + extra["out_shape"], input_output_aliases=extra["aliases"],
        compiler_params=_params(("arbitrary",), has_side_effects=True),
    )(place, x, g, *shards, *rider.arrays)
    return res[0], list(res[1:1 + n]), list(res[1 + n:])


def _region(ref, axis, chip_no, half):
    width = ref.shape[axis] // N_CHIPS
    start = pl.multiple_of(chip_no * width, width)
    if axis == 1:
        if half is None:
            return ref.at[:, pl.ds(start, width)]
        hr = ref.shape[0] // 2
        return ref.at[pl.ds(pl.multiple_of(half * hr, hr), hr), pl.ds(start, width)]
    if half is None:
        return ref.at[pl.ds(start, width), :]
    hr = width // 2
    return ref.at[pl.ds(pl.multiple_of(start + half * hr, hr), hr), :]


def _fetch_copy(refs, axes, whole, send, recv, w, p, chip, c, arriving):
    owner = _chip_no(*chip) if arriving else _chip_no(lax.axis_index("x"), lax.axis_index("y"))
    reg = _region(refs[w], axes[w], owner, None if whole[w] else c)
    return pltpu.make_async_remote_copy(
        src_ref=reg, dst_ref=reg, send_sem=send[p], recv_sem=recv[p],
        device_id=(chip[0], chip[1], c), device_id_type=MESH)


N_PEERS = N_CHIPS - 1


class _Rider:
    def __init__(self, arrays, outs, aliases, start, wait):
        self.arrays, self.outs, self.aliases, self.start, self.wait = list(arrays), list(outs), aliases, start, wait
        self.scratch = [pltpu.SemaphoreType.DMA((len(self.arrays), N_PEERS))] * 2

    def run(self, refs, n_in, n_out, first, last):
        ra, ro = len(self.arrays), len(self.outs)
        ins = refs[n_in:n_in + ra]
        outs = refs[n_in + ra + n_out:n_in + ra + n_out + ro]
        send, recv = refs[-2], refs[-1]

        @pl.when(first)
        def _():
            self.start(ins, outs, send, recv)

        @pl.when(last)
        def _():
            self.wait(ins, outs, send, recv)

    def call_args(self, n_in, n_out):
        ra = len(self.arrays)
        return dict(in_specs=[HBM_SPEC] * ra, out_specs=[HBM_SPEC] * len(self.outs), out_shape=self.outs,
                    aliases={n_in + k: n_out + o for k, o in self.aliases.items()}, scratch=self.scratch)


def _host_refs(refs, rider, n_in, n_out):
    ra = len(rider.arrays) if rider else 0
    return refs[:n_in], refs[n_in + ra:n_in + ra + n_out]


def _riding_fetch(fulls, axes, whole=None):
    n = len(fulls)
    whole = whole or [False] * n

    def sems(ref, w):
        return [ref.at[w, q] for q in range(N_PEERS)]

    def start(ins, outs, send, recv):
        _, _, c, others = _place()
        for w in range(n):
            for p, chip in enumerate(others):
                _fetch_copy(outs, axes, whole, sems(send, w), sems(recv, w), w, p, chip, c, False).start()

    def wait(ins, outs, send, recv):
        _, _, c, others = _place()
        for w in range(n):
            for p, chip in enumerate(others):
                _fetch_copy(outs, axes, whole, sems(send, w), sems(recv, w), w, p, chip, c, False).wait_send()
                _fetch_copy(outs, axes, whole, sems(send, w), sems(recv, w), w, p, chip, c, True).wait_recv()

    return _Rider(fulls, [jax.ShapeDtypeStruct(f.shape, f.dtype) for f in fulls], {k: k for k in range(n)}, start, wait)


def _hand_on(fulls, axes, name):
    n = len(fulls)

    def body(*refs):
        outs = refs[n:2 * n]
        send, recv = refs[2 * n:]
        x, y, c, others = _place()

        def copy(w, p, chip, half):
            reg = _region(outs[w], axes[w], _chip_no(*chip), half)
            return pltpu.make_async_remote_copy(
                src_ref=reg, dst_ref=reg, send_sem=send.at[w, p], recv_sem=recv.at[w, p],
                device_id=(x, y, 1 - c), device_id_type=MESH)

        for w in range(n):
            for p, chip in enumerate(others):
                copy(w, p, chip, c).start()
        for w in range(n):
            for p, chip in enumerate(others):
                copy(w, p, chip, 1 - c).wait()

    return pl.pallas_call(
        body, name=name,
        in_specs=[HBM_SPEC] * n, out_specs=[HBM_SPEC] * n,
        out_shape=[jax.ShapeDtypeStruct(f.shape, f.dtype) for f in fulls],
        input_output_aliases={i: i for i in range(n)},
        scratch_shapes=[pltpu.SemaphoreType.DMA((n, 3)), pltpu.SemaphoreType.DMA((n, 3))],
        compiler_params=pltpu.CompilerParams(has_side_effects=True),
    )(*fulls)


def _pair_exchange(grads, name):
    n = len(grads)

    def body(*refs):
        ins, outs = refs[:n], refs[n:2 * n]
        send, recv = refs[2 * n:]
        x, y, c, _ = _place()
        cps = []
        for w in range(n):
            cp = pltpu.make_async_remote_copy(
                src_ref=ins[w].at[:, 1 - c], dst_ref=outs[w], send_sem=send.at[w], recv_sem=recv.at[w],
                device_id=(x, y, 1 - c), device_id_type=MESH)
            cp.start()
            cps.append(cp)
        for cp in cps:
            cp.wait()

    return pl.pallas_call(
        body, name=name,
        in_specs=[HBM_SPEC] * n, out_specs=[HBM_SPEC] * n,
        out_shape=[jax.ShapeDtypeStruct((g.shape[0],) + g.shape[2:], g.dtype) for g in grads],
        scratch_shapes=[pltpu.SemaphoreType.DMA((n,)), pltpu.SemaphoreType.DMA((n,))],
        compiler_params=pltpu.CompilerParams(has_side_effects=True),
    )(*grads)


def _pair_add(g4, got, core, name):
    nj, _, hr, cdim = g4.shape
    tr = _tile(hr, max(8, STREAM_BLOCK_ELEMS // 2 // cdim))

    def body(core_ref, a_ref, b_ref, o_ref):
        o_ref[...] = (a_ref[...].astype(F32) + b_ref[...].astype(F32)).astype(o_ref.dtype)

    return pl.pallas_call(
        body, name=name,
        grid_spec=pltpu.PrefetchScalarGridSpec(
            num_scalar_prefetch=1, grid=(nj, hr // tr),
            in_specs=[pl.BlockSpec((1, None, tr, cdim), lambda j, i, core_ref: (j, core_ref[0], i, 0)),
                      pl.BlockSpec((1, tr, cdim), lambda j, i, core_ref: (j, i, 0))],
            out_specs=pl.BlockSpec((1, tr, cdim), lambda j, i, core_ref: (j, i, 0))),
        out_shape=jax.ShapeDtypeStruct((nj, hr, cdim), BF),
        compiler_params=_params(("parallel", "parallel")),
    )(core, g4, got)


def _piece(ref, axis, j, hc):
    if axis == 0:
        return ref.at[j]
    return ref.at[0, :, pl.ds(pl.multiple_of(j * hc, hc), hc)]


def _slot_shapes(sums, axes):
    return [(N_CHIPS - 1, sm.shape[1], sm.shape[2] // (1 if ax == 0 else N_CHIPS)) for sm, ax in zip(sums, axes)]


def _slot_copy(sums, lands, axes, send, recv, w, p, chip, c):
    return pltpu.make_async_remote_copy(
        src_ref=_piece(sums[w], axes[w], _chip_no(*chip), lands[w].shape[2]), dst_ref=lands[w].at[p],
        send_sem=send[p], recv_sem=recv[p],
        device_id=(chip[0], chip[1], c), device_id_type=MESH)


def _riding_pairs(views):
    n = len(views)

    def copies(ins, outs, send, recv):
        x, y, c, _ = _place()
        return [pltpu.make_async_remote_copy(
            src_ref=ins[w].at[:, 1 - c], dst_ref=outs[w], send_sem=send.at[w, 0], recv_sem=recv.at[w, 0],
            device_id=(x, y, 1 - c), device_id_type=MESH) for w in range(n)]

    def start(ins, outs, send, recv):
        for cp in copies(ins, outs, send, recv):
            cp.start()

    def wait(ins, outs, send, recv):
        for cp in copies(ins, outs, send, recv):
            cp.wait()

    outs = [jax.ShapeDtypeStruct((g.shape[0],) + g.shape[2:], g.dtype) for g in views]
    return _Rider(views, outs, {}, start, wait)


def _riding_slots(sums, axes):
    n = len(sums)
    shapes = _slot_shapes(sums, axes)

    def copies(ins, outs, send, recv):
        _, _, c, others = _place()
        return [_slot_copy(ins, outs, axes, [send.at[w, q] for q in range(N_PEERS)],
                           [recv.at[w, q] for q in range(N_PEERS)], w, p, chip, c)
                for w in range(n) for p, chip in enumerate(others)]

    def start(ins, outs, send, recv):
        for cp in copies(ins, outs, send, recv):
            cp.start()

    def wait(ins, outs, send, recv):
        for cp in copies(ins, outs, send, recv):
            cp.wait()

    return _Rider(sums, [jax.ShapeDtypeStruct(sh, sm.dtype) for sh, sm in zip(shapes, sums)], {}, start, wait)


def _chip_sum(psum, slots, axis, place, name):
    _, hr, hc = slots.shape
    tr = _tile(hr, 256)
    own_map = (lambda i, pref: (0, i, pref[0])) if axis == 1 else (lambda i, pref: (pref[0], i, 0))

    def body(pref, own_ref, s_ref, o_ref):
        o_ref[...] = ((own_ref[...].astype(F32) + s_ref[0].astype(F32)) + s_ref[1].astype(F32)) + s_ref[2].astype(F32)

    return pl.pallas_call(
        body, name=name,
        grid_spec=pltpu.PrefetchScalarGridSpec(
            num_scalar_prefetch=1, grid=(hr // tr,),
            in_specs=[pl.BlockSpec((None, tr, hc), own_map),
                      pl.BlockSpec((N_CHIPS - 1, tr, hc), lambda i, pref: (0, i, 0))],
            out_specs=pl.BlockSpec((None, tr, hc), lambda i, pref: (pref[1], i, 0))),
        out_shape=jax.ShapeDtypeStruct((2, hr, hc), F32),
        compiler_params=_params(("parallel",)),
    )(place, psum, slots)


def _half_swap(both):
    n = len(both)

    def body(*refs):
        outs = refs[n:2 * n]
        send, recv = refs[2 * n:]
        x, y, c, _ = _place()

        def copy(w, half):
            return pltpu.make_async_remote_copy(
                src_ref=outs[w].at[half], dst_ref=outs[w].at[half], send_sem=send.at[w], recv_sem=recv.at[w],
                device_id=(x, y, 1 - c), device_id_type=MESH)

        for w in range(n):
            copy(w, c).start()
        for w in range(n):
            copy(w, 1 - c).wait()

    return pl.pallas_call(
        body, name="grad_half_swap",
        in_specs=[HBM_SPEC] * n, out_specs=[HBM_SPEC] * n,
        out_shape=[jax.ShapeDtypeStruct(b.shape, b.dtype) for b in both],
        input_output_aliases={i: i for i in range(n)},
        scratch_shapes=[pltpu.SemaphoreType.DMA((n,)), pltpu.SemaphoreType.DMA((n,))],
        compiler_params=pltpu.CompilerParams(has_side_effects=True),
    )(*both)


def _allreduce_small(pack):
    rows, d = pack.shape

    def body(p_ref, o_ref, slots, send, recv):
        x, y, c, _ = _place()
        me = 4 * x + 2 * y + c
        slots[me] = p_ref[...]
        cps = []
        for k in range(1, N_DEV):
            px, py, pc = x ^ (k >> 2), y ^ ((k >> 1) & 1), c ^ (k & 1)
            cp = pltpu.make_async_remote_copy(
                src_ref=p_ref, dst_ref=slots.at[me], send_sem=send.at[k - 1], recv_sem=recv.at[k - 1],
                device_id=(px, py, pc), device_id_type=MESH)
            cp.start()
            cps.append(cp)
        for k in range(1, N_DEV):
            px, py, pc = x ^ (k >> 2), y ^ ((k >> 1) & 1), c ^ (k & 1)
            arrival = pltpu.make_async_remote_copy(
                src_ref=p_ref, dst_ref=slots.at[4 * px + 2 * py + pc], send_sem=send.at[k - 1],
                recv_sem=recv.at[k - 1], device_id=(px, py, pc), device_id_type=MESH)
            arrival.wait_recv()
            arrival.wait_send()
        acc = slots[0]
        for k in range(1, N_DEV):
            acc = acc + slots[k]
        o_ref[...] = acc

    vm = pl.BlockSpec(memory_space=pltpu.VMEM)
    return pl.pallas_call(
        body, name="allreduce_small", in_specs=[vm], out_specs=vm,
        out_shape=jax.ShapeDtypeStruct((rows, d), F32),
        scratch_shapes=[pltpu.VMEM((N_DEV, rows, d), F32), pltpu.SemaphoreType.DMA((N_DEV - 1,)),
                        pltpu.SemaphoreType.DMA((N_DEV - 1,))],
        compiler_params=pltpu.CompilerParams(has_side_effects=True),
    )(pack)


def _adamw(w, g, m, v, name):
    rows, cols = w.shape

    def fn(wv, gv, mv, vv):
        m2 = ADAM_B1 * mv + (1.0 - ADAM_B1) * gv
        v2 = ADAM_B2 * vv + (1.0 - ADAM_B2) * (gv * gv)
        m_hat = m2 / (1.0 - ADAM_B1 ** ADAM_STEP)
        v_hat = v2 / (1.0 - ADAM_B2 ** ADAM_STEP)
        delta = -ADAM_LR * (m_hat / (jnp.sqrt(v_hat) + ADAM_EPS) + ADAM_WD * wv)
        return delta, m2, v2, gv

    ins = [(a, cols, 0) for a in (w, g, m, v)]
    return _rowwise(fn, ins, [(cols, F32)] * 4, rows=rows, tm=_tile(rows, max(8, STREAM_BLOCK_ELEMS // 4 // cols)),
                    name=name)


BIG = ["w_in", "w_branch_a", "w_branch_b", "w_mix_out", "w_mem_q", "w_mem_kv", "w_mem_o", "w_ffn_in", "w_ffn_out"]
BIG_AXIS = {"w_in": 1, "w_branch_a": 1, "w_branch_b": 1, "w_mix_out": 0, "w_mem_q": 0, "w_mem_kv": 1,
            "w_mem_o": 0, "w_ffn_in": 1, "w_ffn_out": 0}
NORMS = ["norm_mix", "norm_mem_q", "norm_mem_kv", "norm_ffn", "norm_final"]
ORDER = ["norm_mix", "w_in", "conv_w", "w_branch_a", "w_branch_b", "w_mix_out", "norm_mem_q", "norm_mem_kv",
         "w_mem_q", "w_mem_kv", "w_mem_o", "norm_ffn", "w_ffn_in", "w_ffn_out", "norm_final"]


def _pack_small(vals, conv):
    d = vals[0].shape[-1]
    rows = [v.reshape(1, d) for v in vals]
    conv = jnp.pad(conv, ((0, 0), (0, d - conv.shape[1])))
    pad = jnp.zeros((SMALL_ROWS - len(rows) - CONV_K, d), F32)
    return jnp.concatenate(rows + [conv, pad], axis=0)


def kernel(x, mem, norm_mix, w_in, conv_w, w_branch_a, w_branch_b, w_mix_out, norm_mem_q, norm_mem_kv, w_mem_q, w_mem_kv, w_mem_o, norm_ffn, w_ffn_in, w_ffn_out, norm_final, loss_target, m_norm_mix, m_w_in, m_conv_w, m_w_branch_a, m_w_branch_b, m_w_mix_out, m_norm_mem_q, m_norm_mem_kv, m_w_mem_q, m_w_mem_kv, m_w_mem_o, m_norm_ffn, m_w_ffn_in, m_w_ffn_out, m_norm_final, v_norm_mix, v_w_in, v_conv_w, v_w_branch_a, v_w_branch_b, v_w_mix_out, v_norm_mem_q, v_norm_mem_kv, v_w_mem_q, v_w_mem_kv, v_w_mem_o, v_norm_ffn, v_w_ffn_in, v_w_ffn_out, v_norm_final):
    args = dict(locals())
    wts = {n: args[n] for n in ORDER}
    mom = {n: args["m_" + n] for n in ORDER}
    var = {n: args["v_" + n] for n in ORDER}
    x = x[0]
    mem = mem[0]
    target = loss_target[0]
    s, d = x.shape
    gains = {n: wts[n].reshape(1, d) for n in NORMS}
    chip = 2 * lax.axis_index("x") + lax.axis_index("y")
    core = lax.axis_index("c").astype(jnp.int32).reshape(1)
    place = jnp.stack([chip, lax.axis_index("c")]).astype(jnp.int32)

    conv_shard = jnp.pad(conv_w[0], ((0, CONV_ROWS - CONV_K), (0, 0)))
    first = [_cast_place(wts["w_in"][0], BIG_AXIS["w_in"], place, BF, "place_w_in"),
             _cast_place(conv_shard, 1, place, F32, "place_conv_w")]
    ride_in_proj = ["w_branch_a", "w_branch_b", "w_mix_out", "w_mem_q", "w_mem_o"]
    ride_attention = ["w_ffn_in", "w_mem_kv"]
    ride_ffn_in = ["w_ffn_out"]
    later = ride_in_proj + ride_attention + ride_ffn_in
    h1, placed_list, (w_in_buf, conv_full) = _norm_and_place(
        x, gains["norm_mix"], [wts[n][0] for n in later], [BIG_AXIS[n] for n in later], place,
        _riding_fetch(first, [BIG_AXIS["w_in"], 1], whole=[False, True]))
    placed = dict(zip(later, placed_list))
    W = {"w_in": _hand_on([w_in_buf], [BIG_AXIS["w_in"]], "gather_hand_on_first")[0]}

    def fetch(names):
        return _riding_fetch([placed[n] for n in names], [BIG_AXIS[n] for n in names])

    def hand_on(names, bufs, tag):
        W.update(zip(names, _hand_on(bufs, [BIG_AXIS[n] for n in names], "gather_hand_on" + tag)))

    proj, *bufs_a = _matmul(h1, W["w_in"], tn=1280, name="in_proj", rider=fetch(ride_in_proj))
    o_a, o_a32, *bufs_b = _sb_fwd(proj, fetch(ride_attention))
    y_b = _conv_fwd(proj, conv_full)
    hand_on(ride_in_proj + ride_attention, bufs_a + bufs_b, "")
    br_a, br_b, merged = _branches_merge(o_a, y_b, W["w_branch_a"], W["w_branch_b"], proj)
    x1 = _matmul(merged, W["w_mix_out"], tn=1024, out_dtype=F32, resid=x, name="mix_out")

    hq, q_m = _norm_matmul(x1, gains["norm_mem_q"], W["w_mem_q"], tn=1024, name="norm_mem_q")
    mn, kv = _norm_matmul(mem, gains["norm_mem_kv"], W["w_mem_kv"], tn=1024, name="norm_mem_kv")
    o_m = _mem_fwd(q_m, kv)
    x2 = _matmul(o_m, W["w_mem_o"], tn=1024, out_dtype=F32, resid=x1, name="mem_o")

    hf, gate, up, act, *bufs_c = _norm_ffn_in_swiglu(x2, gains["norm_ffn"], W["w_ffn_in"], rider=fetch(ride_ffn_in))
    hand_on(ride_ffn_in, bufs_c, "_late")

    dx3, dx3_b, dg_final, loss_part = _ffn_out_loss(act, W["w_ffn_out"], x2, gains["norm_final"], target)

    def halves_of(names):
        views = []
        for n in names:
            r, cdim = gw[n].shape
            views.append(gw[n].reshape(1, 2, r // 2, cdim) if BIG_AXIS[n] == 1
                         else gw[n].reshape(N_CHIPS, 2, r // (2 * N_CHIPS), cdim))
        return views

    def pair_adds(names, views, got):
        return [_pair_add(v, g, core, "pair_add_" + n) for n, v, g in zip(names, views, got)]

    def chip_sums(names, sums, slots):
        return [_chip_sum(sm, sl, BIG_AXIS[n], place, "chip_sum_" + n) for n, sm, sl in zip(names, sums, slots)]

    gw = {"w_ffn_out": _matmul(act, dx3_b, ta=True, tm=1408, tn=512, name="gw_ffn_out")}
    first_views = halves_of(["w_ffn_out"])
    dgu, *first_got = _d_act_swiglu(dx3_b, W["w_ffn_out"], gate, up, rider=_riding_pairs(first_views))
    gw["w_ffn_in"] = _matmul(hf, dgu, ta=True, tn=512, name="gw_ffn_in")
    first_sums = pair_adds(["w_ffn_out"], first_views, first_got)
    dx2, dx2_b, dg_ffn, *first_slots = _matmul_norm_bwd(
        dgu, W["w_ffn_in"], x2, gains["norm_ffn"], dx3, tm=256, bf_copy=True, name="d_hf_norm_bwd",
        rider=_riding_slots(first_sums, [BIG_AXIS["w_ffn_out"]]))

    do_m = _matmul(dx2_b, W["w_mem_o"], tb=True, tn=1024, name="d_o_m")
    gw["w_mem_o"] = _matmul(o_m, dx2_b, ta=True, tn=512, name="gw_mem_o")
    dq_m, dk_m, dv_m = _mem_bwd(q_m, kv, do_m)
    dkv = jnp.concatenate([dk_m, dv_m], axis=-1)
    dx1, dx1_b, dg_q = _matmul_norm_bwd(dq_m, W["w_mem_q"], x1, gains["norm_mem_q"], dx2, tm=512, bf_copy=True,
                                        name="d_hq_norm_bwd")
    gw["w_mem_q"] = _matmul(hq, dq_m, ta=True, tn=512, name="gw_mem_q")
    dmn = _matmul(dkv, W["w_mem_kv"], tb=True, tn=1024, out_dtype=F32, name="d_mn")
    gw["w_mem_kv"] = _matmul(mn, dkv, ta=True, tn=1024, name="gw_mem_kv")
    dg_kv = _norm_gain_grad(mem, dmn, "norm_mem_kv_bwd")

    gw["w_mix_out"] = _matmul(merged, dx1_b, ta=True, tn=512, name="gw_mix_out")
    mid = ["w_ffn_in", "w_mem_o", "w_mem_q", "w_mem_kv", "w_mix_out"]
    mid_views = halves_of(mid)
    dbr_a, dbr_b, dga, dgb, *mid_got = _d_merged_gates(dx1_b, W["w_mix_out"], proj, br_a, br_b,
                                                       rider=_riding_pairs(mid_views))
    do_a = _matmul(dbr_a, W["w_branch_a"], tb=True, name="d_o_a")
    gw["w_branch_a"] = _matmul(o_a, dbr_a, ta=True, tn=512, name="gw_branch_a")
    dy_b = _matmul(dbr_b, W["w_branch_b"], tb=True, name="d_y_b")
    gw["w_branch_b"] = _matmul(y_b, dbr_b, ta=True, tn=512, name="gw_branch_b")
    branches = ["w_branch_a", "w_branch_b"]
    branch_views = halves_of(branches)
    du, dgate_b, dgate_c, dconv, *branch_got = _conv_bwd(proj, conv_full, dy_b, rider=_riding_pairs(branch_views))
    early = mid + branches
    early_sums = pair_adds(mid, mid_views, mid_got) + pair_adds(branches, branch_views, branch_got)
    dq, dk, dv, *early_slots = _sb_bwd(proj, o_a32, do_a, _riding_slots(early_sums, [BIG_AXIS[n] for n in early]))

    def assemble(*parts):
        return jnp.concatenate([p.astype(BF) for p in parts], axis=-1)

    hw = dq.shape[1]
    dproj = _rowwise(assemble, [(t, hw, 0) for t in (dq, dk, dv, du, dgate_b, dgate_c)] + [(dga, d, 0), (dgb, d, 0)],
                     [(proj.shape[1], BF)], rows=s, tm=256, name="assemble_dproj")[0]
    gw["w_in"] = _matmul(h1, dproj, ta=True, tn=640, name="gw_in")
    in_views = halves_of(["w_in"])
    in_sums = pair_adds(["w_in"], in_views, _pair_exchange(in_views, "grad_pair_exchange_in"))
    grad_x, dg_mix, *in_slots = _matmul_norm_bwd(dproj, W["w_in"], x, gains["norm_mix"], dx1, tm=256, bf_copy=False,
                                                 name="d_h1_norm_bwd",
                                                 rider=_riding_slots(in_sums, [BIG_AXIS["w_in"]]))

    halves = dict(zip(early, chip_sums(early, early_sums, early_slots)))
    halves.update(zip(["w_ffn_out"], chip_sums(["w_ffn_out"], first_sums, first_slots)))
    halves.update(zip(["w_in"], chip_sums(["w_in"], in_sums, in_slots)))
    both = _half_swap([halves[n] for n in BIG])
    grads = {n: b.reshape(wts[n].shape[1:]) for n, b in zip(BIG, both)}

    small_g = [dg_mix, dg_q, dg_kv, dg_ffn, dg_final]
    pack = _pack_small(small_g, dconv[:CONV_K])
    pack = pack.at[ROW_LOSS].set(jnp.broadcast_to(loss_part[0, :1], (d,)))
    red = _allreduce_small(pack)
    loss = red[ROW_LOSS, 0]
    cw = conv_w.shape[2]
    conv_g = lax.dynamic_slice(red, (ROW_CONV, chip * cw), (CONV_K, cw))
    small_grad = _pack_small([red[i] for i in range(len(NORMS))], conv_g)
    small = [_pack_small([t[n] for n in NORMS], t["conv_w"][0]) for t in (wts, mom, var)]
    s_delta, s_m, s_v, _ = _adamw(small[0], small_grad, small[1], small[2], "adamw_small")

    out = {"grad": {}, "delta": {}, "new_m": {}, "new_v": {}}
    for n in BIG:
        shp = wts[n].shape
        dl, m2, v2, g_out = _adamw(wts[n][0], grads[n], mom[n][0], var[n][0], "adamw_" + n)
        out["grad"][n] = g_out.reshape(shp)
        out["delta"][n], out["new_m"][n], out["new_v"][n] = dl.reshape(shp), m2.reshape(shp), v2.reshape(shp)
    for key, blk in (("grad", small_grad), ("delta", s_delta), ("new_m", s_m), ("new_v", s_v)):
        for i, n in enumerate(NORMS):
            out[key][n] = blk[i].reshape(wts[n].shape)
        out[key]["conv_w"] = blk[ROW_CONV:ROW_CONV + CONV_K, :cw].reshape(conv_w.shape)

    return (loss, grad_x[None], *[out["grad"][n] for n in ORDER], *[out["delta"][n] for n in ORDER],
            *[out["new_m"][n] for n in ORDER], *[out["new_v"][n] for n in ORDER])
```

```python
import math

import jax
import jax.numpy as jnp
from jax import lax
from jax.experimental import pallas as pl
from jax.experimental.pallas import tpu as pltpu

BF = jnp.bfloat16
F32 = jnp.float32
MESH = pl.DeviceIdType.MESH

SB_HEAD_DIM = 64
LANES = 128
MEM_HEADS = 4
CONV_K = 3
CONV_ROWS = 8
EPS = 1e-6
N_CHIPS = 4
N_DEV = 8
VMEM_LIMIT = 56 * 1024 * 1024
STREAM_BLOCK_ELEMS = 1 << 20

ADAM_LR = 0.001
ADAM_B1 = 0.9
ADAM_B2 = 0.999
ADAM_EPS = 1e-08
ADAM_WD = 0.01
ADAM_STEP = 10

SMALL_ROWS = 16
ROW_CONV = 5
ROW_LOSS = 8


def _params(sem=None, **kw):
    return pltpu.CompilerParams(dimension_semantics=sem, vmem_limit_bytes=VMEM_LIMIT, **kw)


def _tile(dim, pref):
    if dim <= pref:
        return dim
    for step in (LANES, 8):
        t = (pref // step) * step
        while t >= step:
            if dim % t == 0:
                return t
            t -= step
    raise ValueError(f"no tile of {dim} under {pref}")


def _matmul(a, b, *, ta=False, tb=False, tm=1024, tn=512, out_dtype=BF, resid=None, rider=None, name):
    if ta:
        kdim, m = a.shape
    else:
        m, kdim = a.shape
    n = b.shape[0] if tb else b.shape[1]
    tm, tn = _tile(m, tm), _tile(n, tn)
    a_spec = pl.BlockSpec((kdim, tm), lambda i, j: (0, i)) if ta else pl.BlockSpec((tm, kdim), lambda i, j: (i, 0))
    b_spec = pl.BlockSpec((tn, kdim), lambda i, j: (j, 0)) if tb else pl.BlockSpec((kdim, tn), lambda i, j: (0, j))
    o_spec = pl.BlockSpec((tm, tn), lambda i, j: (i, j))
    dims = (((0 if ta else 1,), (1 if tb else 0,)), ((), ()))
    has_res = resid is not None
    n_in = 2 + has_res

    def body(*refs):
        ins, (o_ref,) = _host_refs(refs, rider, n_in, 1)
        first = (pl.program_id(0) == 0) & (pl.program_id(1) == 0)
        if rider:
            rider.begin(refs, n_in, 1, first)
        av, bv = ins[0][...], ins[1][...]
        if av.dtype != BF:
            av = av.astype(BF)
        if bv.dtype != BF:
            bv = bv.astype(BF)
        acc = lax.dot_general(av, bv, dims, preferred_element_type=F32)
        if has_res:
            acc = ins[2][...] + acc
        o_ref[...] = acc.astype(o_ref.dtype)
        if rider:
            rider.run(refs, n_in, 1, first,
                      (pl.program_id(0) == m // tm - 1) & (pl.program_id(1) == n // tn - 1), start=False)

    res = pl.pallas_call(
        body, name=name, grid=(m // tm, n // tn),
        compiler_params=_params(("arbitrary", "arbitrary") if rider else ("parallel", "parallel"),
                                has_side_effects=rider is not None),
        **_with_rider(rider, n_in, 1, [a_spec, b_spec] + ([o_spec] if has_res else []), [o_spec],
                      [jax.ShapeDtypeStruct((m, n), out_dtype)]),
    )(*([a, b] + ([resid] if has_res else []) + (rider.arrays if rider else [])))
    return res if rider else res[0]


def _rowwise(fn, ins, outs, *, rows, tm, name, accs=()):
    tm = _tile(rows, tm)
    in_specs, args = [], []
    for arr, cols, cb in ins:
        if cols is None:
            in_specs.append(pl.BlockSpec(arr.shape, lambda i, nd=arr.ndim: (0,) * nd))
        else:
            in_specs.append(pl.BlockSpec((tm, cols), lambda i, cb=cb: (i, cb)))
        args.append(arr)
    out_specs = [pl.BlockSpec((tm, cols), lambda i: (i, 0)) for cols, _ in outs]
    out_shape = [jax.ShapeDtypeStruct((rows, cols), dt) for cols, dt in outs]
    for r, c in accs:
        out_specs.append(pl.BlockSpec((r, c), lambda i: (0, 0)))
        out_shape.append(jax.ShapeDtypeStruct((r, c), F32))
    n_in, n_out = len(ins), len(outs)

    def body(*refs):
        res = fn(*[r[...] for r in refs[:n_in]])
        if not isinstance(res, (tuple, list)):
            res = (res,)
        refs = refs[n_in:]
        for o_ref, val in zip(refs[:n_out], res[:n_out]):
            o_ref[...] = val.astype(o_ref.dtype)
        first = pl.program_id(0) == 0
        for a_ref, val in zip(refs[n_out:], res[n_out:]):
            @pl.when(first)
            def _(a_ref=a_ref, val=val):
                a_ref[...] = val

            @pl.when(jnp.logical_not(first))
            def _(a_ref=a_ref, val=val):
                a_ref[...] += val

    res = pl.pallas_call(
        body, name=name, grid=(rows // tm,), in_specs=in_specs, out_specs=out_specs, out_shape=out_shape,
        compiler_params=_params(("arbitrary",) if accs else ("parallel",)),
    )(*args)
    return res


def _rstd(xf):
    return lax.rsqrt(jnp.mean(xf * xf, axis=-1, keepdims=True) + EPS)


def _norm_gain_grad(x, dy, name):
    rows, d = x.shape

    def fn(xv, dyv):
        return (jnp.sum(dyv.astype(F32) * (xv * _rstd(xv)), axis=0, keepdims=True),)

    return _rowwise(fn, [(x, d, 0), (dy, d, 0)], [], rows=rows, tm=512, name=name, accs=[(1, d)])[0]


def _matmul_norm_bwd(a, w, x, g, resid, *, tm, bf_copy, name, rider=None):
    m, kdim = a.shape
    d = w.shape[0]
    tm = _tile(m, tm)
    row = lambda i: (i, 0)
    whole = lambda i: (0, 0)
    n_out = 2 + bf_copy

    def body(*refs):
        (a_ref, w_ref, x_ref, g_ref, r_ref), outs = _host_refs(refs, rider, 5, n_out)
        if rider:
            rider.begin(refs, 5, n_out, pl.program_id(0) == 0)
        dy = lax.dot_general(a_ref[...], w_ref[...], NT, preferred_element_type=F32)
        xv = x_ref[...]
        r = _rstd(xv)
        xn = xv * r
        dxn = dy * g_ref[...]
        dx = r_ref[...] + r * (dxn - xn * jnp.mean(dxn * xn, axis=-1, keepdims=True))
        outs[0][...] = dx
        if bf_copy:
            outs[1][...] = dx.astype(BF)
        dg = jnp.sum(dy * xn, axis=0, keepdims=True)
        first = pl.program_id(0) == 0

        @pl.when(first)
        def _():
            outs[-1][...] = dg

        @pl.when(jnp.logical_not(first))
        def _():
            outs[-1][...] += dg

        if rider:
            rider.run(refs, 5, n_out, first, pl.program_id(0) == m // tm - 1, start=False)

    tok = pl.BlockSpec((tm, d), row)
    out_specs = [tok] + ([tok] if bf_copy else []) + [pl.BlockSpec((1, d), whole)]
    out_shape = ([jax.ShapeDtypeStruct((m, d), F32)] + ([jax.ShapeDtypeStruct((m, d), BF)] if bf_copy else [])
                 + [jax.ShapeDtypeStruct((1, d), F32)])
    return pl.pallas_call(
        body, name=name, grid=(m // tm,),
        compiler_params=_params(("arbitrary",), has_side_effects=rider is not None),
        **_with_rider(
            rider, 5, n_out,
            [pl.BlockSpec((tm, kdim), row), pl.BlockSpec((d, kdim), whole), tok, pl.BlockSpec((1, d), whole), tok],
            out_specs, out_shape),
    )(a, w, x, g, resid, *(rider.arrays if rider else []))


def _norm_ffn_in_swiglu(x, g, w, *, tm=1024, tn=1408, rider=None):
    m, kdim = x.shape
    f = w.shape[1] // 2
    tm, tn = _tile(m, tm), _tile(f, tn)
    nj = f // tn

    def body(*refs):
        (x_ref, g_ref, wg_ref, wu_ref), (h_ref, gate_ref, up_ref, act_ref) = _host_refs(refs, rider, 4, 4)
        if rider:
            rider.run(refs, 4, 4, (pl.program_id(0) == 0) & (pl.program_id(1) == 0),
                      (pl.program_id(0) == m // tm - 1) & (pl.program_id(1) == nj - 1))

        @pl.when(pl.program_id(1) == 0)
        def _():
            xv = x_ref[...]
            h_ref[...] = (xv * _rstd(xv) * g_ref[...]).astype(h_ref.dtype)

        hv = h_ref[...]
        gate = jnp.dot(hv, wg_ref[...], preferred_element_type=F32)
        up = jnp.dot(hv, wu_ref[...], preferred_element_type=F32)
        gate_ref[...] = gate.astype(gate_ref.dtype)
        up_ref[...] = up.astype(up_ref.dtype)
        act_ref[...] = (gate * jax.nn.sigmoid(gate) * up).astype(act_ref.dtype)

    tile = pl.BlockSpec((tm, tn), lambda i, j: (i, j))
    rows = pl.BlockSpec((tm, kdim), lambda i, j: (i, 0))
    return pl.pallas_call(
        body, name="norm_ffn_in_swiglu", grid=(m // tm, nj),
        compiler_params=_params(("arbitrary", "arbitrary") if rider else ("parallel", "arbitrary"),
                                has_side_effects=rider is not None),
        **_with_rider(
            rider, 4, 4,
            [rows, pl.BlockSpec((1, kdim), lambda i, j: (0, 0)), pl.BlockSpec((kdim, tn), lambda i, j: (0, j)),
             pl.BlockSpec((kdim, tn), lambda i, j: (0, nj + j))],
            [rows, tile, tile, tile],
            [jax.ShapeDtypeStruct((m, kdim), BF)] + [jax.ShapeDtypeStruct((m, f), BF)] * 3),
    )(x, g, w, w, *(rider.arrays if rider else []))


def _norm_matmul(x, g, w, *, tm=1024, tn, name):
    m, kdim = x.shape
    n = w.shape[1]
    tm, tn = _tile(m, tm), _tile(n, tn)

    def body(x_ref, g_ref, w_ref, h_ref, o_ref):
        @pl.when(pl.program_id(1) == 0)
        def _():
            xv = x_ref[...]
            h_ref[...] = (xv * _rstd(xv) * g_ref[...]).astype(h_ref.dtype)

        o_ref[...] = jnp.dot(h_ref[...], w_ref[...], preferred_element_type=F32).astype(o_ref.dtype)

    rows = pl.BlockSpec((tm, kdim), lambda i, j: (i, 0))
    return pl.pallas_call(
        body, name=name, grid=(m // tm, n // tn),
        in_specs=[rows, pl.BlockSpec((1, kdim), lambda i, j: (0, 0)), pl.BlockSpec((kdim, tn), lambda i, j: (0, j))],
        out_specs=[rows, pl.BlockSpec((tm, tn), lambda i, j: (i, j))],
        out_shape=[jax.ShapeDtypeStruct((m, kdim), BF), jax.ShapeDtypeStruct((m, n), BF)],
        compiler_params=_params(("parallel", "arbitrary")),
    )(x, g, w)


def _ffn_out_loss(act, w, resid, g, target, *, tm=512):
    m, f = act.shape
    d = w.shape[1]
    tm = _tile(m, tm)
    row = lambda i: (i, 0)
    whole = lambda i: (0, 0)

    def body(a_ref, w_ref, r_ref, g_ref, t_ref, dx_ref, dxb_ref, dg_ref, loss_ref):
        xv = r_ref[...] + jnp.dot(a_ref[...], w_ref[...], preferred_element_type=F32)
        gv = g_ref[...]
        r = _rstd(xv)
        xn = xv * r
        err = xn * gv - t_ref[...]
        loss = 0.5 * jnp.sum(jnp.mean(err * err, axis=-1, keepdims=True), axis=0, keepdims=True)
        dyv = err * (1.0 / d)
        dxn = dyv * gv
        dx = r * (dxn - xn * jnp.mean(dxn * xn, axis=-1, keepdims=True))
        dx_ref[...] = dx
        dxb_ref[...] = dx.astype(dxb_ref.dtype)
        dg = jnp.sum(dyv * xn, axis=0, keepdims=True)
        loss_b = jnp.broadcast_to(loss, (1, LANES))
        first = pl.program_id(0) == 0

        @pl.when(first)
        def _():
            dg_ref[...] = dg
            loss_ref[...] = loss_b

        @pl.when(jnp.logical_not(first))
        def _():
            dg_ref[...] += dg
            loss_ref[...] += loss_b

    tok = pl.BlockSpec((tm, d), row)
    return pl.pallas_call(
        body, name="ffn_out_loss", grid=(m // tm,),
        in_specs=[pl.BlockSpec((tm, f), row), pl.BlockSpec((f, d), whole), tok, pl.BlockSpec((1, d), whole), tok],
        out_specs=[tok, tok, pl.BlockSpec((1, d), whole), pl.BlockSpec((1, LANES), whole)],
        out_shape=[jax.ShapeDtypeStruct((m, d), F32), jax.ShapeDtypeStruct((m, d), BF),
                   jax.ShapeDtypeStruct((1, d), F32), jax.ShapeDtypeStruct((1, LANES), F32)],
        compiler_params=_params(("arbitrary",)),
    )(act, w, resid, g, target)


def _d_act_swiglu(dx, w, gate, up, *, tm=256, rider=None):
    m, d = dx.shape
    f = w.shape[0]
    tm = _tile(m, tm)
    row = lambda i: (i, 0)

    def body(*refs):
        (dx_ref, w_ref, gate_ref, up_ref), (o_ref,) = _host_refs(refs, rider, 4, 1)
        if rider:
            rider.run(refs, 4, 1, pl.program_id(0) == 0, pl.program_id(0) == m // tm - 1)
        da = lax.dot_general(dx_ref[...], w_ref[...], NT, preferred_element_type=F32)
        gv, uv = gate_ref[...].astype(F32), up_ref[...].astype(F32)
        sg = jax.nn.sigmoid(gv)
        dgate = da * uv * (sg * (1.0 + gv * (1.0 - sg)))
        o_ref[...] = jnp.concatenate([dgate, da * (gv * sg)], axis=-1).astype(o_ref.dtype)

    return pl.pallas_call(
        body, name="d_act_swiglu", grid=(m // tm,),
        compiler_params=_params(("arbitrary",) if rider else ("parallel",), has_side_effects=rider is not None),
        **_with_rider(
            rider, 4, 1,
            [pl.BlockSpec((tm, d), row), pl.BlockSpec((f, d), lambda i: (0, 0)),
             pl.BlockSpec((tm, f), row), pl.BlockSpec((tm, f), row)],
            [pl.BlockSpec((tm, 2 * f), row)], [jax.ShapeDtypeStruct((m, 2 * f), BF)]),
    )(dx, w, gate, up, *(rider.arrays if rider else []))


GATE_A_BLK, GATE_B_BLK = 3, 4


def _branches_merge(o_a, y_b, w_a, w_b, proj, *, tm=1024):
    m, kdim = o_a.shape
    d = w_a.shape[1]
    tm = _tile(m, tm)
    row = lambda i: (i, 0)

    def body(a_ref, b_ref, wa_ref, wb_ref, ga_ref, gb_ref, bra_ref, brb_ref, merged_ref):
        bra = jnp.dot(a_ref[...], wa_ref[...], preferred_element_type=F32)
        brb = jnp.dot(b_ref[...], wb_ref[...], preferred_element_type=F32)
        bra_ref[...] = bra.astype(bra_ref.dtype)
        brb_ref[...] = brb.astype(brb_ref.dtype)
        merged = jax.nn.sigmoid(ga_ref[...].astype(F32)) * bra + jax.nn.sigmoid(gb_ref[...].astype(F32)) * brb
        merged_ref[...] = merged.astype(merged_ref.dtype)

    tok = pl.BlockSpec((tm, d), row)
    return pl.pallas_call(
        body, name="branches_merge", grid=(m // tm,),
        in_specs=[pl.BlockSpec((tm, kdim), row), pl.BlockSpec((tm, kdim), row),
                  pl.BlockSpec((kdim, d), lambda i: (0, 0)), pl.BlockSpec((kdim, d), lambda i: (0, 0)),
                  pl.BlockSpec((tm, d), lambda i: (i, GATE_A_BLK)), pl.BlockSpec((tm, d), lambda i: (i, GATE_B_BLK))],
        out_specs=[tok, tok, tok], out_shape=[jax.ShapeDtypeStruct((m, d), BF)] * 3,
        compiler_params=_params(("parallel",)),
    )(o_a, y_b, w_a, w_b, proj, proj)


def _d_merged_gates(dx, w_mix, proj, br_a, br_b, *, tm=512, rider=None):
    m, d = dx.shape
    tm = _tile(m, tm)
    row = lambda i: (i, 0)

    def body(*refs):
        ins, (dbra_ref, dbrb_ref, dga_ref, dgb_ref) = _host_refs(refs, rider, 6, 4)
        dx_ref, w_ref, ga_ref, gb_ref, bra_ref, brb_ref = ins
        if rider:
            rider.run(refs, 6, 4, pl.program_id(0) == 0, pl.program_id(0) == m // tm - 1)
        dm = lax.dot_general(dx_ref[...], w_ref[...], NT, preferred_element_type=F32)
        sa, sb = jax.nn.sigmoid(ga_ref[...].astype(F32)), jax.nn.sigmoid(gb_ref[...].astype(F32))
        dbra_ref[...] = (dm * sa).astype(dbra_ref.dtype)
        dbrb_ref[...] = (dm * sb).astype(dbrb_ref.dtype)
        dga_ref[...] = (dm * bra_ref[...].astype(F32) * (sa * (1.0 - sa))).astype(dga_ref.dtype)
        dgb_ref[...] = (dm * brb_ref[...].astype(F32) * (sb * (1.0 - sb))).astype(dgb_ref.dtype)

    tok = pl.BlockSpec((tm, d), row)
    return pl.pallas_call(
        body, name="d_merged_gates", grid=(m // tm,),
        compiler_params=_params(("arbitrary",) if rider else ("parallel",), has_side_effects=rider is not None),
        **_with_rider(
            rider, 6, 4,
            [tok, pl.BlockSpec((d, d), lambda i: (0, 0)),
             pl.BlockSpec((tm, d), lambda i: (i, GATE_A_BLK)), pl.BlockSpec((tm, d), lambda i: (i, GATE_B_BLK)),
             tok, tok],
            [tok] * 4, [jax.ShapeDtypeStruct((m, d), BF)] * 4),
    )(dx, w_mix, proj, proj, br_a, br_b, *(rider.arrays if rider else []))


SB_TK = 128
SB_KT = 2


def _sb_consts(tq):
    tk = SB_TK
    diff = lax.broadcasted_iota(jnp.int32, (tq, tk), 1) - lax.broadcasted_iota(jnp.int32, (tq, tk), 0)
    rj = lax.broadcasted_iota(jnp.int32, (2 * tk, 2 * tk), 0) & (tk - 1)
    cj = lax.broadcasted_iota(jnp.int32, (2 * tk, 2 * tk), 1)
    ones_half = cj >= tk
    later = jnp.where((rj > cj) | ones_half, 1.0, 0.0).astype(BF)
    later_incl = jnp.where((rj >= cj) | ones_half, 1.0, 0.0).astype(BF)
    return diff, later, later_incl


def _split_dot(val, rhs_twice):
    hi = val.astype(BF)
    lo = (val - hi.astype(F32)).astype(BF)
    return jnp.dot(jnp.concatenate([hi, lo], axis=1), rhs_twice, preferred_element_type=F32)


def _log_terms(z):
    sp = jnp.maximum(z, 0.0) + jnp.log(1.0 + jnp.exp(-jnp.abs(z)))
    return z - sp, sp


NT = (((1,), (1,)), ((), ()))
TN = (((0,), (0,)), ((), ()))


def _head_lane_masks(rows):
    lane = lax.broadcasted_iota(jnp.int32, (rows, LANES), 1)
    first = jnp.where(lane < SB_HEAD_DIM, 1.0, 0.0)
    return first.astype(BF), (1.0 - first).astype(BF)


DEAD_LOG = 104.0


def _walk_back(i, step, state, carries_of):
    def alive(st):
        c0, c1 = carries_of(st)
        return jnp.min(jnp.minimum(c0, c1)) < DEAD_LOG

    def cond(loop):
        done, live, _ = loop
        return jnp.logical_and(done < i, live)

    def body(loop):
        done, _, st = loop
        st = step(i - 1 - done, st)
        return done + 1, alive(st), st

    return lax.while_loop(cond, body, (jnp.int32(0), alive(state), state))[2]


def _tail(a, r0):
    return a if r0 == 0 else a[r0:]


def _add_tail(a, r0, delta):
    return a + delta if r0 == 0 else jnp.concatenate([a[:r0], a[r0:] + delta], axis=0)


def _both_heads(tile, masks):
    return jnp.concatenate([tile * masks[0], tile * masks[1]], axis=0)


def _with_rider(rider, n_in, n_out, in_specs, out_specs, out_shape, scratch=()):
    kw = dict(in_specs=list(in_specs), out_specs=list(out_specs), out_shape=list(out_shape),
              scratch_shapes=list(scratch), input_output_aliases={})
    if rider:
        extra = rider.call_args(n_in, n_out)
        kw["in_specs"] += extra["in_specs"]
        kw["out_specs"] += extra["out_specs"]
        kw["out_shape"] += extra["out_shape"]
        kw["scratch_shapes"] += extra["scratch"]
        kw["input_output_aliases"] = extra["aliases"]
    return kw


def _sb_fwd(proj, rider=None):
    s = proj.shape[0]
    tk, tq = SB_TK, SB_KT * SB_TK
    n_pairs = 4
    scale = 1.0 / math.sqrt(SB_HEAD_DIM)

    def body(*refs):
        (q_ref, k_ref, v_ref), (o_ref, o32_ref) = _host_refs(refs, rider, 3, 2)
        i = pl.program_id(1)
        if rider:
            rider.begin(refs, 3, 2, (pl.program_id(0) == 0) & (i == 0))
        diff, later, _ = _sb_consts(tq)
        qs = (q_ref[...].astype(F32) * scale).astype(BF)
        lane_masks = _head_lane_masks(tk)

        def step(g, state, masked):
            tiles = list(reversed(range(SB_KT)))
            chains = [(t, h) for t in tiles for h in range(2)]
            rows = {t: pl.ds(pl.multiple_of((g * SB_KT + t) * tk, tk), tk) for t in tiles}
            ks = {t: _both_heads(k_ref[rows[t], :], lane_masks) for t in tiles}
            vs = {t: _both_heads(v_ref[rows[t], :], lane_masks) for t in tiles}
            r0 = {t: t * tk if masked else 0 for t in tiles}
            allowed = {t: _tail(diff, r0[t]) < -t * tk for t in tiles}
            zs = {t: lax.dot_general(_tail(qs, r0[t]), ks[t], NT, preferred_element_type=F32) for t in tiles}
            logs = {}
            for t, h in chains:
                log_b, sp = _log_terms(zs[t][:, h * tk:(h + 1) * tk])
                logs[t, h] = (log_b, jnp.where(allowed[t], sp, 0.0) if masked else sp)
            sums = {c: _split_dot(logs[c][1], later) for c in chains}
            carries = list(state[0])
            ws = {}
            for t, h in chains:
                w = jnp.exp(logs[t, h][0] - (sums[t, h][:, :tk] + _tail(carries[h], r0[t])))
                ws[t, h] = (jnp.where(allowed[t], w, 0.0) if masked else w).astype(BF)
                carries[h] = _add_tail(carries[h], r0[t], sums[t, h][:, tk:])
            acc = state[1]
            for t in tiles:
                acc = _add_tail(acc, r0[t], jnp.dot(jnp.concatenate([ws[t, 0], ws[t, 1]], axis=1), vs[t],
                                                    preferred_element_type=F32))
            return tuple(carries), acc

        zero = jnp.zeros((tq, LANES), F32)
        state = step(i, ((zero, zero), zero), True)
        state = _walk_back(i, lambda g, st: step(g, st, False), state, lambda st: st[0])
        o_ref[...] = state[1].astype(o_ref.dtype)
        o32_ref[...] = state[1]
        if rider:
            rider.run(refs, 3, 2, (pl.program_id(0) == 0) & (i == 0),
                      (pl.program_id(0) == n_pairs - 1) & (i == s // tq - 1), start=False)

    tok = pl.BlockSpec((tq, LANES), lambda p, i: (i, p))
    return pl.pallas_call(
        body, name="sb_attn_fwd", grid=(n_pairs, s // tq),
        compiler_params=_params(("arbitrary", "arbitrary"), has_side_effects=rider is not None),
        **_with_rider(
            rider, 3, 2,
            [tok, pl.BlockSpec((s, LANES), lambda p, i: (0, n_pairs + p)),
             pl.BlockSpec((s, LANES), lambda p, i: (0, 2 * n_pairs + p))],
            [tok, tok],
            [jax.ShapeDtypeStruct((s, n_pairs * LANES), BF), jax.ShapeDtypeStruct((s, n_pairs * LANES), F32)]),
    )(proj, proj, proj, *(rider.arrays if rider else []))


def _sb_bwd(proj, o32, do_a, rider=None):
    s = proj.shape[0]
    tk, tq = SB_TK, SB_KT * SB_TK
    n_pairs = 4
    scale = 1.0 / math.sqrt(SB_HEAD_DIM)

    def body(*refs):
        (q_ref, k_ref, v_ref, o_ref, do_ref), (dq_ref, dk_ref, dv_ref) = _host_refs(refs, rider, 5, 3)
        i = pl.program_id(1)
        if rider:
            rider.begin(refs, 5, 3, (pl.program_id(0) == 0) & (i == 0))

        @pl.when(i == 0)
        def _():
            dk_ref[...] = jnp.zeros_like(dk_ref)
            dv_ref[...] = jnp.zeros_like(dv_ref)

        diff, later, later_incl = _sb_consts(tq)
        qs = (q_ref[...].astype(F32) * scale).astype(BF)
        do2 = do_ref[...]
        prod = do2.astype(F32) * o_ref[...]
        lane_masks = _head_lane_masks(tk)
        first_head = lax.broadcasted_iota(jnp.int32, (tq, LANES), 1) < SB_HEAD_DIM
        totals = [jnp.broadcast_to(jnp.sum(jnp.where(keep, prod, 0.0), axis=-1, keepdims=True), (tq, tk))
                  for keep in (first_head, jnp.logical_not(first_head))]
        first_head_k = first_head[:tk]

        def step(g_idx, state, masked):
            tiles = list(reversed(range(SB_KT)))
            chains = [(t, h) for t in tiles for h in range(2)]
            rows = {t: pl.ds(pl.multiple_of((g_idx * SB_KT + t) * tk, tk), tk) for t in tiles}
            ks = {t: _both_heads(k_ref[rows[t], :], lane_masks) for t in tiles}
            vs = {t: _both_heads(v_ref[rows[t], :], lane_masks) for t in tiles}
            r0 = {t: t * tk if masked else 0 for t in tiles}
            allowed = {t: _tail(diff, r0[t]) < -t * tk for t in tiles}
            zs = {t: lax.dot_general(_tail(qs, r0[t]), ks[t], NT, preferred_element_type=F32) for t in tiles}
            dws = {t: lax.dot_general(_tail(do2, r0[t]), vs[t], NT, preferred_element_type=F32) for t in tiles}
            logs = {}
            for t, h in chains:
                log_b, sp = _log_terms(zs[t][:, h * tk:(h + 1) * tk])
                logs[t, h] = (log_b, jnp.where(allowed[t], sp, 0.0) if masked else sp)
            sums = {c: _split_dot(logs[c][1], later) for c in chains}
            c_log, c_g = list(state[0]), list(state[1])
            ws, gs = {}, {}
            for t, h in chains:
                w = jnp.exp(logs[t, h][0] - (sums[t, h][:, :tk] + _tail(c_log[h], r0[t])))
                ws[t, h] = (jnp.where(allowed[t], w, 0.0) if masked else w).astype(BF)
                c_log[h] = _add_tail(c_log[h], r0[t], sums[t, h][:, tk:])
                gs[t, h] = ws[t, h].astype(F32) * dws[t][:, h * tk:(h + 1) * tk]
            gsums = {c: _split_dot(gs[c], later_incl) for c in chains}
            dzs = {}
            for t, h in chains:
                beta = jnp.exp(logs[t, h][0])
                earlier = _tail(totals[h], r0[t]) - (gsums[t, h][:, :tk] + _tail(c_g[h], r0[t]))
                dz = gs[t, h] * (1.0 - beta) - earlier * beta
                dzs[t, h] = (jnp.where(allowed[t], dz, 0.0) if masked else dz).astype(BF)
                c_g[h] = _add_tail(c_g[h], r0[t], gsums[t, h][:, tk:])
            dq = state[2]
            for t in tiles:
                dz_both = jnp.concatenate([dzs[t, 0], dzs[t, 1]], axis=1)
                w_both = jnp.concatenate([ws[t, 0], ws[t, 1]], axis=1)
                dq = _add_tail(dq, r0[t], jnp.dot(dz_both, ks[t], preferred_element_type=F32))
                dk2 = lax.dot_general(dz_both, _tail(qs, r0[t]), TN, preferred_element_type=F32)
                dv2 = lax.dot_general(w_both, _tail(do2, r0[t]), TN, preferred_element_type=F32)
                dk_ref[rows[t], :] += jnp.where(first_head_k, dk2[:tk], dk2[tk:])
                dv_ref[rows[t], :] += jnp.where(first_head_k, dv2[:tk], dv2[tk:])
            return tuple(c_log), tuple(c_g), dq

        zero = jnp.zeros((tq, LANES), F32)
        state = step(i, ((zero, zero), (zero, zero), zero), True)
        state = _walk_back(i, lambda g, st: step(g, st, False), state, lambda st: st[0])
        dq_ref[...] = (state[2] * scale).astype(dq_ref.dtype)
        if rider:
            rider.run(refs, 5, 3, (pl.program_id(0) == 0) & (i == 0),
                      (pl.program_id(0) == n_pairs - 1) & (i == s // tq - 1), start=False)

    width = n_pairs * LANES
    tok = pl.BlockSpec((tq, LANES), lambda p, i: (i, p))
    return pl.pallas_call(
        body, name="sb_attn_bwd", grid=(n_pairs, s // tq),
        compiler_params=_params(("arbitrary", "arbitrary"), has_side_effects=rider is not None),
        **_with_rider(
            rider, 5, 3,
            [tok, pl.BlockSpec((s, LANES), lambda p, i: (0, n_pairs + p)),
             pl.BlockSpec((s, LANES), lambda p, i: (0, 2 * n_pairs + p)), tok, tok],
            [tok, pl.BlockSpec((s, LANES), lambda p, i: (0, p)), pl.BlockSpec((s, LANES), lambda p, i: (0, p))],
            [jax.ShapeDtypeStruct((s, width), BF), jax.ShapeDtypeStruct((s, width), F32),
             jax.ShapeDtypeStruct((s, width), F32)]),
    )(proj, proj, proj, o32, do_a, *(rider.arrays if rider else []))


CONV_COL0 = 12


def _shift_rows(v, k):
    n = v.shape[0]
    row = lax.broadcasted_iota(jnp.int32, v.shape, 0)
    rolled = pltpu.roll(v, k % n, axis=0)
    keep = row >= k if k > 0 else row < n + k
    return jnp.where(keep, rolled, 0.0)


def _conv_specs(s):
    return [pl.BlockSpec((s, LANES), lambda cb: (0, CONV_COL0 + cb)),
            pl.BlockSpec((s, LANES), lambda cb: (0, CONV_COL0 + 4 + cb)),
            pl.BlockSpec((s, LANES), lambda cb: (0, CONV_COL0 + 8 + cb)),
            pl.BlockSpec((CONV_ROWS, LANES), lambda cb: (0, cb))]


def _conv_fwd(proj, conv_w):
    s = proj.shape[0]

    def body(u_ref, gb_ref, gc_ref, w_ref, y_ref):
        cu = gc_ref[...].astype(F32) * u_ref[...].astype(F32)
        w = w_ref[...]
        y = w[0:1] * _shift_rows(cu, 2) + w[1:2] * _shift_rows(cu, 1) + w[2:3] * cu
        y_ref[...] = (gb_ref[...].astype(F32) * y).astype(y_ref.dtype)

    return pl.pallas_call(
        body, name="conv_fwd", grid=(4,), in_specs=_conv_specs(s),
        out_specs=pl.BlockSpec((s, LANES), lambda cb: (0, cb)),
        out_shape=jax.ShapeDtypeStruct((s, 4 * LANES), BF),
        compiler_params=_params(("parallel",)),
    )(proj, proj, proj, conv_w)


def _conv_bwd(proj, conv_w, dy, rider=None):
    s = proj.shape[0]
    n_blocks = 4

    def body(*refs):
        (u_ref, gb_ref, gc_ref, w_ref, dy_ref), (du_ref, dgb_ref, dgc_ref, dw_ref) = _host_refs(refs, rider, 5, 4)
        u, gc = u_ref[...].astype(F32), gc_ref[...].astype(F32)
        dyv = dy_ref[...].astype(F32)
        w = w_ref[...]
        cu = gc * u
        cu1, cu2 = _shift_rows(cu, 1), _shift_rows(cu, 2)
        conv = w[0:1] * cu2 + w[1:2] * cu1 + w[2:3] * cu
        dgb_ref[...] = (dyv * conv).astype(dgb_ref.dtype)
        dc = dyv * gb_ref[...].astype(F32)
        dcu = w[2:3] * dc + w[1:2] * _shift_rows(dc, -1) + w[0:1] * _shift_rows(dc, -2)
        dgc_ref[...] = (dcu * u).astype(dgc_ref.dtype)
        du_ref[...] = (dcu * gc).astype(du_ref.dtype)
        tap_row = lax.broadcasted_iota(jnp.int32, (CONV_ROWS, LANES), 0)
        dw = jnp.zeros((CONV_ROWS, LANES), F32)
        for t, shifted in enumerate((cu2, cu1, cu)):
            dw = jnp.where(tap_row == t, jnp.sum(dc * shifted, axis=0, keepdims=True), dw)
        dw_ref[...] = dw
        if rider:
            rider.run(refs, 5, 4, pl.program_id(0) == 0, pl.program_id(0) == n_blocks - 1)

    col = pl.BlockSpec((s, LANES), lambda cb: (0, cb))
    act = jax.ShapeDtypeStruct((s, 4 * LANES), BF)
    return pl.pallas_call(
        body, name="conv_bwd", grid=(n_blocks,),
        compiler_params=_params(("arbitrary",) if rider else ("parallel",), has_side_effects=rider is not None),
        **_with_rider(
            rider, 5, 4, _conv_specs(s) + [col],
            [col, col, col, pl.BlockSpec((CONV_ROWS, LANES), lambda cb: (0, cb))],
            [act, act, act, jax.ShapeDtypeStruct((CONV_ROWS, n_blocks * LANES), F32)]),
    )(proj, proj, proj, conv_w, dy, *(rider.arrays if rider else []))


def _mem_probs(q, k, scale):
    sc = lax.dot_general(q, k, NT, preferred_element_type=F32) * scale
    p = jnp.exp(sc - jnp.max(sc, axis=-1, keepdims=True))
    return p / jnp.sum(p, axis=-1, keepdims=True)


def _mem_fwd(q_m, kv, tq=2048):
    s, d = q_m.shape
    mlen = kv.shape[0]
    hd = d // MEM_HEADS
    tq = _tile(s, tq)
    scale = 1.0 / math.sqrt(hd)

    def body(q_ref, k_ref, v_ref, o_ref):
        p = _mem_probs(q_ref[...], k_ref[...], scale)
        o_ref[...] = jnp.dot(p.astype(BF), v_ref[...], preferred_element_type=F32).astype(o_ref.dtype)

    return pl.pallas_call(
        body, name="mem_attn_fwd", grid=(MEM_HEADS, s // tq),
        in_specs=[pl.BlockSpec((tq, hd), lambda h, i: (i, h)),
                  pl.BlockSpec((mlen, hd), lambda h, i: (0, h)),
                  pl.BlockSpec((mlen, hd), lambda h, i: (0, MEM_HEADS + h))],
        out_specs=pl.BlockSpec((tq, hd), lambda h, i: (i, h)),
        out_shape=jax.ShapeDtypeStruct((s, d), BF),
        compiler_params=_params(("parallel", "parallel")),
    )(q_m, kv, kv)


def _mem_bwd(q_m, kv, do_m, tq=2048):
    s, d = q_m.shape
    mlen = kv.shape[0]
    hd = d // MEM_HEADS
    tq = _tile(s, tq)
    scale = 1.0 / math.sqrt(hd)

    def body(q_ref, k_ref, v_ref, do_ref, dq_ref, dk_ref, dv_ref):
        q, k, v, do = q_ref[...], k_ref[...], v_ref[...], do_ref[...]
        p = _mem_probs(q, k, scale)
        dp = lax.dot_general(do, v, NT, preferred_element_type=F32)
        ds = p * (dp - jnp.sum(dp * p, axis=-1, keepdims=True)) * scale
        dsb = ds.astype(BF)
        dq_ref[...] = jnp.dot(dsb, k, preferred_element_type=F32).astype(dq_ref.dtype)
        dk = lax.dot_general(dsb, q, TN, preferred_element_type=F32)
        dv = lax.dot_general(p.astype(BF), do, TN, preferred_element_type=F32)
        first = pl.program_id(1) == 0

        @pl.when(first)
        def _():
            dk_ref[...] = dk
            dv_ref[...] = dv

        @pl.when(jnp.logical_not(first))
        def _():
            dk_ref[...] += dk
            dv_ref[...] += dv

    tok = pl.BlockSpec((tq, hd), lambda h, i: (i, h))
    memb = pl.BlockSpec((mlen, hd), lambda h, i: (0, h))
    return pl.pallas_call(
        body, name="mem_attn_bwd", grid=(MEM_HEADS, s // tq),
        in_specs=[tok, memb, pl.BlockSpec((mlen, hd), lambda h, i: (0, MEM_HEADS + h)), tok],
        out_specs=[tok, memb, memb],
        out_shape=[jax.ShapeDtypeStruct((s, d), BF), jax.ShapeDtypeStruct((mlen, d), F32),
                   jax.ShapeDtypeStruct((mlen, d), F32)],
        compiler_params=_params(("parallel", "arbitrary")),
    )(q_m, kv, kv, do_m)


def _place():
    x, y, c = lax.axis_index("x"), lax.axis_index("y"), lax.axis_index("c")
    other_chips = [(1 - x, y), (x, 1 - y), (1 - x, 1 - y)]
    return x, y, c, other_chips


def _chip_no(cx, cy):
    return 2 * cx + cy


HBM_SPEC = pl.BlockSpec(memory_space=pl.ANY)


def _cast_place(shard, axis, place, dtype, name):
    r, c = shard.shape
    tr = _tile(r, max(16, STREAM_BLOCK_ELEMS // c))
    nblk = r // tr
    if axis == 1:
        full, out_map = (r, N_CHIPS * c), lambda i, pref: (i, pref[0])
    else:
        full, out_map = (N_CHIPS * r, c), lambda i, pref: (pref[0] * nblk + i, 0)

    def body(pref, s_ref, o_ref):
        o_ref[...] = s_ref[...].astype(o_ref.dtype)

    return pl.pallas_call(
        body, name=name,
        grid_spec=pltpu.PrefetchScalarGridSpec(
            num_scalar_prefetch=1, grid=(nblk,),
            in_specs=[pl.BlockSpec((tr, c), lambda i, pref: (i, 0))],
            out_specs=pl.BlockSpec((tr, c), out_map)),
        out_shape=jax.ShapeDtypeStruct(full, dtype),
        compiler_params=_params(("parallel",)),
    )(place, shard)


PLACE_STEPS = 8


def _norm_and_place(x, g, shards, axes, place, rider):
    s, d = x.shape
    n = len(shards)
    in_specs = [pl.BlockSpec((s // PLACE_STEPS, d), lambda i, pref: (i, 0)), pl.BlockSpec((1, d), lambda i, pref: (0, 0))]
    out_specs = [pl.BlockSpec((s // PLACE_STEPS, d), lambda i, pref: (i, 0))]
    out_shape = [jax.ShapeDtypeStruct((s, d), BF)]
    for sh, ax in zip(shards, axes):
        r, c = sh.shape
        steps = PLACE_STEPS if (r // PLACE_STEPS) % 16 == 0 else PLACE_STEPS // 2
        rb, rep = r // steps, PLACE_STEPS // steps
        in_specs.append(pl.BlockSpec((rb, c), lambda i, pref, rep=rep: (i // rep, 0)))
        if ax == 1:
            out_specs.append(pl.BlockSpec((rb, c), lambda i, pref, rep=rep: (i // rep, pref[0])))
            out_shape.append(jax.ShapeDtypeStruct((r, N_CHIPS * c), BF))
        else:
            out_specs.append(pl.BlockSpec((rb, c), lambda i, pref, rep=rep, steps=steps: (pref[0] * steps + i // rep, 0)))
            out_shape.append(jax.ShapeDtypeStruct((N_CHIPS * r, c), BF))
    n_in, n_out = 2 + n, 1 + n

    def body(pref, *refs):
        ins, outs = _host_refs(refs, rider, n_in, n_out)
        rider.begin(refs, n_in, n_out, pl.program_id(0) == 0)
        xv = ins[0][...]
        outs[0][...] = (xv * _rstd(xv) * ins[1][...]).astype(BF)
        for s_ref, o_ref in zip(ins[2:], outs[1:]):
            o_ref[...] = s_ref[...].astype(o_ref.dtype)
        rider.run(refs, n_in, n_out, pl.program_id(0) == 0, pl.program_id(0) == PLACE_STEPS - 1, start=False)

    extra = rider.call_args(1 + n_in, n_out)
    res = pl.pallas_call(
        body, name="norm_and_place",
        grid_spec=pltpu.PrefetchScalarGridSpec(
            num_scalar_prefetch=1, grid=(PLACE_STEPS,), in_specs=in_specs + extra["in_specs"],
            out_specs=out_specs + extra["out_specs"], scratch_shapes=extra["scratch"]),
        out_shape=out_shape + extra["out_shape"], input_output_aliases=extra["aliases"],
        compiler_params=_params(("arbitrary",), has_side_effects=True),
    )(place, x, g, *shards, *rider.arrays)
    return res[0], list(res[1:1 + n]), list(res[1 + n:])


def _region(ref, axis, chip_no, half):
    width = ref.shape[axis] // N_CHIPS
    start = pl.multiple_of(chip_no * width, width)
    if axis == 1:
        if half is None:
            return ref.at[:, pl.ds(start, width)]
        hr = ref.shape[0] // 2
        return ref.at[pl.ds(pl.multiple_of(half * hr, hr), hr), pl.ds(start, width)]
    if half is None:
        return ref.at[pl.ds(start, width), :]
    hr = width // 2
    return ref.at[pl.ds(pl.multiple_of(start + half * hr, hr), hr), :]


def _fetch_copy(refs, axes, whole, send, recv, w, p, chip, c, arriving):
    owner = _chip_no(*chip) if arriving else _chip_no(lax.axis_index("x"), lax.axis_index("y"))
    reg = _region(refs[w], axes[w], owner, None if whole[w] else c)
    return pltpu.make_async_remote_copy(
        src_ref=reg, dst_ref=reg, send_sem=send[p], recv_sem=recv[p],
        device_id=(chip[0], chip[1], c), device_id_type=MESH)


N_PEERS = N_CHIPS - 1


class _Rider:
    def __init__(self, arrays, outs, aliases, start, wait):
        self.arrays, self.outs, self.aliases, self.start, self.wait = list(arrays), list(outs), aliases, start, wait
        self.scratch = [pltpu.SemaphoreType.DMA((len(self.arrays), N_PEERS))] * 2

    def _refs(self, refs, n_in, n_out):
        ra, ro = len(self.arrays), len(self.outs)
        return refs[n_in:n_in + ra], refs[n_in + ra + n_out:n_in + ra + n_out + ro], refs[-2], refs[-1]

    def begin(self, refs, n_in, n_out, first):
        @pl.when(first)
        def _():
            self.start(*self._refs(refs, n_in, n_out))

    def run(self, refs, n_in, n_out, first, last, start=True):
        if start:
            self.begin(refs, n_in, n_out, first)

        @pl.when(last)
        def _():
            self.wait(*self._refs(refs, n_in, n_out))

    def call_args(self, n_in, n_out):
        ra = len(self.arrays)
        return dict(in_specs=[HBM_SPEC] * ra, out_specs=[HBM_SPEC] * len(self.outs), out_shape=self.outs,
                    aliases={n_in + k: n_out + o for k, o in self.aliases.items()}, scratch=self.scratch)


def _host_refs(refs, rider, n_in, n_out):
    ra = len(rider.arrays) if rider else 0
    return refs[:n_in], refs[n_in + ra:n_in + ra + n_out]


def _riding_fetch(fulls, axes, whole=None):
    n = len(fulls)
    whole = whole or [False] * n

    def sems(ref, w):
        return [ref.at[w, q] for q in range(N_PEERS)]

    def start(ins, outs, send, recv):
        _, _, c, others = _place()
        for w in range(n):
            for p, chip in enumerate(others):
                _fetch_copy(outs, axes, whole, sems(send, w), sems(recv, w), w, p, chip, c, False).start()

    def wait(ins, outs, send, recv):
        _, _, c, others = _place()
        for w in range(n):
            for p, chip in enumerate(others):
                _fetch_copy(outs, axes, whole, sems(send, w), sems(recv, w), w, p, chip, c, False).wait_send()
                _fetch_copy(outs, axes, whole, sems(send, w), sems(recv, w), w, p, chip, c, True).wait_recv()

    return _Rider(fulls, [jax.ShapeDtypeStruct(f.shape, f.dtype) for f in fulls], {k: k for k in range(n)}, start, wait)


def _hand_on(fulls, axes, name):
    n = len(fulls)

    def body(*refs):
        outs = refs[n:2 * n]
        send, recv = refs[2 * n:]
        x, y, c, others = _place()

        def copy(w, p, chip, half):
            reg = _region(outs[w], axes[w], _chip_no(*chip), half)
            return pltpu.make_async_remote_copy(
                src_ref=reg, dst_ref=reg, send_sem=send.at[w, p], recv_sem=recv.at[w, p],
                device_id=(x, y, 1 - c), device_id_type=MESH)

        for w in range(n):
            for p, chip in enumerate(others):
                copy(w, p, chip, c).start()
        for w in range(n):
            for p, chip in enumerate(others):
                copy(w, p, chip, 1 - c).wait()

    return pl.pallas_call(
        body, name=name,
        in_specs=[HBM_SPEC] * n, out_specs=[HBM_SPEC] * n,
        out_shape=[jax.ShapeDtypeStruct(f.shape, f.dtype) for f in fulls],
        input_output_aliases={i: i for i in range(n)},
        scratch_shapes=[pltpu.SemaphoreType.DMA((n, 3)), pltpu.SemaphoreType.DMA((n, 3))],
        compiler_params=pltpu.CompilerParams(has_side_effects=True),
    )(*fulls)


def _pair_exchange(grads, name):
    n = len(grads)

    def body(*refs):
        ins, outs = refs[:n], refs[n:2 * n]
        send, recv = refs[2 * n:]
        x, y, c, _ = _place()
        cps = []
        for w in range(n):
            cp = pltpu.make_async_remote_copy(
                src_ref=ins[w].at[:, 1 - c], dst_ref=outs[w], send_sem=send.at[w], recv_sem=recv.at[w],
                device_id=(x, y, 1 - c), device_id_type=MESH)
            cp.start()
            cps.append(cp)
        for cp in cps:
            cp.wait()

    return pl.pallas_call(
        body, name=name,
        in_specs=[HBM_SPEC] * n, out_specs=[HBM_SPEC] * n,
        out_shape=[jax.ShapeDtypeStruct((g.shape[0],) + g.shape[2:], g.dtype) for g in grads],
        scratch_shapes=[pltpu.SemaphoreType.DMA((n,)), pltpu.SemaphoreType.DMA((n,))],
        compiler_params=pltpu.CompilerParams(has_side_effects=True),
    )(*grads)


def _pair_add(g4, got, core, name):
    nj, _, hr, cdim = g4.shape
    tr = _tile(hr, max(8, STREAM_BLOCK_ELEMS // 2 // cdim))

    def body(core_ref, a_ref, b_ref, o_ref):
        o_ref[...] = (a_ref[...].astype(F32) + b_ref[...].astype(F32)).astype(o_ref.dtype)

    return pl.pallas_call(
        body, name=name,
        grid_spec=pltpu.PrefetchScalarGridSpec(
            num_scalar_prefetch=1, grid=(nj, hr // tr),
            in_specs=[pl.BlockSpec((1, None, tr, cdim), lambda j, i, core_ref: (j, core_ref[0], i, 0)),
                      pl.BlockSpec((1, tr, cdim), lambda j, i, core_ref: (j, i, 0))],
            out_specs=pl.BlockSpec((1, tr, cdim), lambda j, i, core_ref: (j, i, 0))),
        out_shape=jax.ShapeDtypeStruct((nj, hr, cdim), BF),
        compiler_params=_params(("parallel", "parallel")),
    )(core, g4, got)


def _piece(ref, axis, j, hc):
    if axis == 0:
        return ref.at[j]
    return ref.at[0, :, pl.ds(pl.multiple_of(j * hc, hc), hc)]


def _slot_shapes(sums, axes):
    return [(N_CHIPS - 1, sm.shape[1], sm.shape[2] // (1 if ax == 0 else N_CHIPS)) for sm, ax in zip(sums, axes)]


def _slot_copy(sums, lands, axes, send, recv, w, p, chip, c):
    return pltpu.make_async_remote_copy(
        src_ref=_piece(sums[w], axes[w], _chip_no(*chip), lands[w].shape[2]), dst_ref=lands[w].at[p],
        send_sem=send[p], recv_sem=recv[p],
        device_id=(chip[0], chip[1], c), device_id_type=MESH)


def _riding_pairs(views):
    n = len(views)

    def copies(ins, outs, send, recv):
        x, y, c, _ = _place()
        return [pltpu.make_async_remote_copy(
            src_ref=ins[w].at[:, 1 - c], dst_ref=outs[w], send_sem=send.at[w, 0], recv_sem=recv.at[w, 0],
            device_id=(x, y, 1 - c), device_id_type=MESH) for w in range(n)]

    def start(ins, outs, send, recv):
        for cp in copies(ins, outs, send, recv):
            cp.start()

    def wait(ins, outs, send, recv):
        for cp in copies(ins, outs, send, recv):
            cp.wait()

    outs = [jax.ShapeDtypeStruct((g.shape[0],) + g.shape[2:], g.dtype) for g in views]
    return _Rider(views, outs, {}, start, wait)


def _riding_slots(sums, axes):
    n = len(sums)
    shapes = _slot_shapes(sums, axes)

    def copies(ins, outs, send, recv):
        _, _, c, others = _place()
        return [_slot_copy(ins, outs, axes, [send.at[w, q] for q in range(N_PEERS)],
                           [recv.at[w, q] for q in range(N_PEERS)], w, p, chip, c)
                for w in range(n) for p, chip in enumerate(others)]

    def start(ins, outs, send, recv):
        for cp in copies(ins, outs, send, recv):
            cp.start()

    def wait(ins, outs, send, recv):
        for cp in copies(ins, outs, send, recv):
            cp.wait()

    return _Rider(sums, [jax.ShapeDtypeStruct(sh, sm.dtype) for sh, sm in zip(shapes, sums)], {}, start, wait)


def _chip_sum(psum, slots, axis, place, name):
    _, hr, hc = slots.shape
    tr = _tile(hr, 256)
    own_map = (lambda i, pref: (0, i, pref[0])) if axis == 1 else (lambda i, pref: (pref[0], i, 0))

    def body(pref, own_ref, s_ref, o_ref):
        o_ref[...] = ((own_ref[...].astype(F32) + s_ref[0].astype(F32)) + s_ref[1].astype(F32)) + s_ref[2].astype(F32)

    return pl.pallas_call(
        body, name=name,
        grid_spec=pltpu.PrefetchScalarGridSpec(
            num_scalar_prefetch=1, grid=(hr // tr,),
            in_specs=[pl.BlockSpec((None, tr, hc), own_map),
                      pl.BlockSpec((N_CHIPS - 1, tr, hc), lambda i, pref: (0, i, 0))],
            out_specs=pl.BlockSpec((None, tr, hc), lambda i, pref: (pref[1], i, 0))),
        out_shape=jax.ShapeDtypeStruct((2, hr, hc), F32),
        compiler_params=_params(("parallel",)),
    )(place, psum, slots)


def _half_swap(both):
    n = len(both)

    def body(*refs):
        outs = refs[n:2 * n]
        send, recv = refs[2 * n:]
        x, y, c, _ = _place()

        def copy(w, half):
            return pltpu.make_async_remote_copy(
                src_ref=outs[w].at[half], dst_ref=outs[w].at[half], send_sem=send.at[w], recv_sem=recv.at[w],
                device_id=(x, y, 1 - c), device_id_type=MESH)

        for w in range(n):
            copy(w, c).start()
        for w in range(n):
            copy(w, 1 - c).wait()

    return pl.pallas_call(
        body, name="grad_half_swap",
        in_specs=[HBM_SPEC] * n, out_specs=[HBM_SPEC] * n,
        out_shape=[jax.ShapeDtypeStruct(b.shape, b.dtype) for b in both],
        input_output_aliases={i: i for i in range(n)},
        scratch_shapes=[pltpu.SemaphoreType.DMA((n,)), pltpu.SemaphoreType.DMA((n,))],
        compiler_params=pltpu.CompilerParams(has_side_effects=True),
    )(*both)


def _allreduce_small(pack):
    rows, d = pack.shape

    def body(p_ref, o_ref, slots, send, recv):
        x, y, c, _ = _place()
        me = 4 * x + 2 * y + c
        slots[me] = p_ref[...]
        cps = []
        for k in range(1, N_DEV):
            px, py, pc = x ^ (k >> 2), y ^ ((k >> 1) & 1), c ^ (k & 1)
            cp = pltpu.make_async_remote_copy(
                src_ref=p_ref, dst_ref=slots.at[me], send_sem=send.at[k - 1], recv_sem=recv.at[k - 1],
                device_id=(px, py, pc), device_id_type=MESH)
            cp.start()
            cps.append(cp)
        for k in range(1, N_DEV):
            px, py, pc = x ^ (k >> 2), y ^ ((k >> 1) & 1), c ^ (k & 1)
            arrival = pltpu.make_async_remote_copy(
                src_ref=p_ref, dst_ref=slots.at[4 * px + 2 * py + pc], send_sem=send.at[k - 1],
                recv_sem=recv.at[k - 1], device_id=(px, py, pc), device_id_type=MESH)
            arrival.wait_recv()
            arrival.wait_send()
        acc = slots[0]
        for k in range(1, N_DEV):
            acc = acc + slots[k]
        o_ref[...] = acc

    vm = pl.BlockSpec(memory_space=pltpu.VMEM)
    return pl.pallas_call(
        body, name="allreduce_small", in_specs=[vm], out_specs=vm,
        out_shape=jax.ShapeDtypeStruct((rows, d), F32),
        scratch_shapes=[pltpu.VMEM((N_DEV, rows, d), F32), pltpu.SemaphoreType.DMA((N_DEV - 1,)),
                        pltpu.SemaphoreType.DMA((N_DEV - 1,))],
        compiler_params=pltpu.CompilerParams(has_side_effects=True),
    )(pack)


def _adamw(w, g, m, v, name):
    rows, cols = w.shape

    def fn(wv, gv, mv, vv):
        m2 = ADAM_B1 * mv + (1.0 - ADAM_B1) * gv
        v2 = ADAM_B2 * vv + (1.0 - ADAM_B2) * (gv * gv)
        m_hat = m2 / (1.0 - ADAM_B1 ** ADAM_STEP)
        v_hat = v2 / (1.0 - ADAM_B2 ** ADAM_STEP)
        delta = -ADAM_LR * (m_hat / (jnp.sqrt(v_hat) + ADAM_EPS) + ADAM_WD * wv)
        return delta, m2, v2, gv

    ins = [(a, cols, 0) for a in (w, g, m, v)]
    return _rowwise(fn, ins, [(cols, F32)] * 4, rows=rows, tm=_tile(rows, max(8, STREAM_BLOCK_ELEMS // 4 // cols)),
                    name=name)


BIG = ["w_in", "w_branch_a", "w_branch_b", "w_mix_out", "w_mem_q", "w_mem_kv", "w_mem_o", "w_ffn_in", "w_ffn_out"]
BIG_AXIS = {"w_in": 1, "w_branch_a": 1, "w_branch_b": 1, "w_mix_out": 0, "w_mem_q": 0, "w_mem_kv": 1,
            "w_mem_o": 0, "w_ffn_in": 1, "w_ffn_out": 0}
NORMS = ["norm_mix", "norm_mem_q", "norm_mem_kv", "norm_ffn", "norm_final"]
ORDER = ["norm_mix", "w_in", "conv_w", "w_branch_a", "w_branch_b", "w_mix_out", "norm_mem_q", "norm_mem_kv",
         "w_mem_q", "w_mem_kv", "w_mem_o", "norm_ffn", "w_ffn_in", "w_ffn_out", "norm_final"]


def _pack_small(vals, conv):
    d = vals[0].shape[-1]
    rows = [v.reshape(1, d) for v in vals]
    conv = jnp.pad(conv, ((0, 0), (0, d - conv.shape[1])))
    pad = jnp.zeros((SMALL_ROWS - len(rows) - CONV_K, d), F32)
    return jnp.concatenate(rows + [conv, pad], axis=0)


def kernel(x, mem, norm_mix, w_in, conv_w, w_branch_a, w_branch_b, w_mix_out, norm_mem_q, norm_mem_kv, w_mem_q, w_mem_kv, w_mem_o, norm_ffn, w_ffn_in, w_ffn_out, norm_final, loss_target, m_norm_mix, m_w_in, m_conv_w, m_w_branch_a, m_w_branch_b, m_w_mix_out, m_norm_mem_q, m_norm_mem_kv, m_w_mem_q, m_w_mem_kv, m_w_mem_o, m_norm_ffn, m_w_ffn_in, m_w_ffn_out, m_norm_final, v_norm_mix, v_w_in, v_conv_w, v_w_branch_a, v_w_branch_b, v_w_mix_out, v_norm_mem_q, v_norm_mem_kv, v_w_mem_q, v_w_mem_kv, v_w_mem_o, v_norm_ffn, v_w_ffn_in, v_w_ffn_out, v_norm_final):
    args = dict(locals())
    wts = {n: args[n] for n in ORDER}
    mom = {n: args["m_" + n] for n in ORDER}
    var = {n: args["v_" + n] for n in ORDER}
    x = x[0]
    mem = mem[0]
    target = loss_target[0]
    s, d = x.shape
    gains = {n: wts[n].reshape(1, d) for n in NORMS}
    chip = 2 * lax.axis_index("x") + lax.axis_index("y")
    core = lax.axis_index("c").astype(jnp.int32).reshape(1)
    place = jnp.stack([chip, lax.axis_index("c")]).astype(jnp.int32)

    conv_shard = jnp.pad(conv_w[0], ((0, CONV_ROWS - CONV_K), (0, 0)))
    first = [_cast_place(wts["w_in"][0], BIG_AXIS["w_in"], place, BF, "place_w_in"),
             _cast_place(conv_shard, 1, place, F32, "place_conv_w")]
    ride_in_proj = ["w_branch_a", "w_branch_b", "w_mix_out", "w_mem_q", "w_mem_o"]
    ride_attention = ["w_ffn_in", "w_mem_kv"]
    ride_ffn_in = ["w_ffn_out"]
    later = ride_in_proj + ride_attention + ride_ffn_in
    h1, placed_list, (w_in_buf, conv_full) = _norm_and_place(
        x, gains["norm_mix"], [wts[n][0] for n in later], [BIG_AXIS[n] for n in later], place,
        _riding_fetch(first, [BIG_AXIS["w_in"], 1], whole=[False, True]))
    placed = dict(zip(later, placed_list))
    W = {"w_in": _hand_on([w_in_buf], [BIG_AXIS["w_in"]], "gather_hand_on_first")[0]}

    def fetch(names):
        return _riding_fetch([placed[n] for n in names], [BIG_AXIS[n] for n in names])

    def hand_on(names, bufs, tag):
        W.update(zip(names, _hand_on(bufs, [BIG_AXIS[n] for n in names], "gather_hand_on" + tag)))

    proj, *bufs_a = _matmul(h1, W["w_in"], tn=1280, name="in_proj", rider=fetch(ride_in_proj))
    o_a, o_a32, *bufs_b = _sb_fwd(proj, fetch(ride_attention))
    y_b = _conv_fwd(proj, conv_full)
    hand_on(ride_in_proj + ride_attention, bufs_a + bufs_b, "")
    br_a, br_b, merged = _branches_merge(o_a, y_b, W["w_branch_a"], W["w_branch_b"], proj)
    x1 = _matmul(merged, W["w_mix_out"], tn=1024, out_dtype=F32, resid=x, name="mix_out")

    hq, q_m = _norm_matmul(x1, gains["norm_mem_q"], W["w_mem_q"], tn=1024, name="norm_mem_q")
    mn, kv = _norm_matmul(mem, gains["norm_mem_kv"], W["w_mem_kv"], tn=1024, name="norm_mem_kv")
    o_m = _mem_fwd(q_m, kv)
    x2 = _matmul(o_m, W["w_mem_o"], tn=1024, out_dtype=F32, resid=x1, name="mem_o")

    hf, gate, up, act, *bufs_c = _norm_ffn_in_swiglu(x2, gains["norm_ffn"], W["w_ffn_in"], rider=fetch(ride_ffn_in))
    hand_on(ride_ffn_in, bufs_c, "_late")

    dx3, dx3_b, dg_final, loss_part = _ffn_out_loss(act, W["w_ffn_out"], x2, gains["norm_final"], target)

    def halves_of(names):
        views = []
        for n in names:
            r, cdim = gw[n].shape
            views.append(gw[n].reshape(1, 2, r // 2, cdim) if BIG_AXIS[n] == 1
                         else gw[n].reshape(N_CHIPS, 2, r // (2 * N_CHIPS), cdim))
        return views

    def pair_adds(names, views, got):
        return [_pair_add(v, g, core, "pair_add_" + n) for n, v, g in zip(names, views, got)]

    def chip_sums(names, sums, slots):
        return [_chip_sum(sm, sl, BIG_AXIS[n], place, "chip_sum_" + n) for n, sm, sl in zip(names, sums, slots)]

    gw = {"w_ffn_out": _matmul(act, dx3_b, ta=True, tm=1408, tn=512, name="gw_ffn_out")}
    first_views = halves_of(["w_ffn_out"])
    dgu, *first_got = _d_act_swiglu(dx3_b, W["w_ffn_out"], gate, up, rider=_riding_pairs(first_views))
    gw["w_ffn_in"] = _matmul(hf, dgu, ta=True, tn=512, name="gw_ffn_in")
    first_sums = pair_adds(["w_ffn_out"], first_views, first_got)
    dx2, dx2_b, dg_ffn, *first_slots = _matmul_norm_bwd(
        dgu, W["w_ffn_in"], x2, gains["norm_ffn"], dx3, tm=256, bf_copy=True, name="d_hf_norm_bwd",
        rider=_riding_slots(first_sums, [BIG_AXIS["w_ffn_out"]]))

    do_m = _matmul(dx2_b, W["w_mem_o"], tb=True, tn=1024, name="d_o_m")
    gw["w_mem_o"] = _matmul(o_m, dx2_b, ta=True, tn=512, name="gw_mem_o")
    dq_m, dk_m, dv_m = _mem_bwd(q_m, kv, do_m)
    dkv = jnp.concatenate([dk_m, dv_m], axis=-1)
    dx1, dx1_b, dg_q = _matmul_norm_bwd(dq_m, W["w_mem_q"], x1, gains["norm_mem_q"], dx2, tm=512, bf_copy=True,
                                        name="d_hq_norm_bwd")
    gw["w_mem_q"] = _matmul(hq, dq_m, ta=True, tn=512, name="gw_mem_q")
    dmn = _matmul(dkv, W["w_mem_kv"], tb=True, tn=1024, out_dtype=F32, name="d_mn")
    gw["w_mem_kv"] = _matmul(mn, dkv, ta=True, tn=1024, name="gw_mem_kv")
    dg_kv = _norm_gain_grad(mem, dmn, "norm_mem_kv_bwd")

    gw["w_mix_out"] = _matmul(merged, dx1_b, ta=True, tn=512, name="gw_mix_out")
    mid = ["w_ffn_in", "w_mem_o", "w_mem_q", "w_mem_kv", "w_mix_out"]
    mid_views = halves_of(mid)
    dbr_a, dbr_b, dga, dgb, *mid_got = _d_merged_gates(dx1_b, W["w_mix_out"], proj, br_a, br_b,
                                                       rider=_riding_pairs(mid_views))
    do_a = _matmul(dbr_a, W["w_branch_a"], tb=True, name="d_o_a")
    gw["w_branch_a"] = _matmul(o_a, dbr_a, ta=True, tn=512, name="gw_branch_a")
    dy_b = _matmul(dbr_b, W["w_branch_b"], tb=True, name="d_y_b")
    gw["w_branch_b"] = _matmul(y_b, dbr_b, ta=True, tn=512, name="gw_branch_b")
    branches = ["w_branch_a", "w_branch_b"]
    branch_views = halves_of(branches)
    du, dgate_b, dgate_c, dconv, *branch_got = _conv_bwd(proj, conv_full, dy_b, rider=_riding_pairs(branch_views))
    early = mid + branches
    early_sums = pair_adds(mid, mid_views, mid_got) + pair_adds(branches, branch_views, branch_got)
    dq, dk, dv, *early_slots = _sb_bwd(proj, o_a32, do_a, _riding_slots(early_sums, [BIG_AXIS[n] for n in early]))

    def assemble(*parts):
        return jnp.concatenate([p.astype(BF) for p in parts], axis=-1)

    hw = dq.shape[1]
    dproj = _rowwise(assemble, [(t, hw, 0) for t in (dq, dk, dv, du, dgate_b, dgate_c)] + [(dga, d, 0), (dgb, d, 0)],
                     [(proj.shape[1], BF)], rows=s, tm=256, name="assemble_dproj")[0]
    gw["w_in"] = _matmul(h1, dproj, ta=True, tn=640, name="gw_in")
    in_views = halves_of(["w_in"])
    in_sums = pair_adds(["w_in"], in_views, _pair_exchange(in_views, "grad_pair_exchange_in"))
    grad_x, dg_mix, *in_slots = _matmul_norm_bwd(dproj, W["w_in"], x, gains["norm_mix"], dx1, tm=256, bf_copy=False,
                                                 name="d_h1_norm_bwd",
                                                 rider=_riding_slots(in_sums, [BIG_AXIS["w_in"]]))

    halves = dict(zip(early, chip_sums(early, early_sums, early_slots)))
    halves.update(zip(["w_ffn_out"], chip_sums(["w_ffn_out"], first_sums, first_slots)))
    halves.update(zip(["w_in"], chip_sums(["w_in"], in_sums, in_slots)))
    both = _half_swap([halves[n] for n in BIG])
    grads = {n: b.reshape(wts[n].shape[1:]) for n, b in zip(BIG, both)}

    small_g = [dg_mix, dg_q, dg_kv, dg_ffn, dg_final]
    pack = _pack_small(small_g, dconv[:CONV_K])
    pack = pack.at[ROW_LOSS].set(jnp.broadcast_to(loss_part[0, :1], (d,)))
    red = _allreduce_small(pack)
    loss = red[ROW_LOSS, 0]
    cw = conv_w.shape[2]
    conv_g = lax.dynamic_slice(red, (ROW_CONV, chip * cw), (CONV_K, cw))
    small_grad = _pack_small([red[i] for i in range(len(NORMS))], conv_g)
    small = [_pack_small([t[n] for n in NORMS], t["conv_w"][0]) for t in (wts, mom, var)]
    s_delta, s_m, s_v, _ = _adamw(small[0], small_grad, small[1], small[2], "adamw_small")

    out = {"grad": {}, "delta": {}, "new_m": {}, "new_v": {}}
    for n in BIG:
        shp = wts[n].shape
        dl, m2, v2, g_out = _adamw(wts[n][0], grads[n], mom[n][0], var[n][0], "adamw_" + n)
        out["grad"][n] = g_out.reshape(shp)
        out["delta"][n], out["new_m"][n], out["new_v"][n] = dl.reshape(shp), m2.reshape(shp), v2.reshape(shp)
    for key, blk in (("grad", small_grad), ("delta", s_delta), ("new_m", s_m), ("new_v", s_v)):
        for i, n in enumerate(NORMS):
            out[key][n] = blk[i].reshape(wts[n].shape)
        out[key]["conv_w"] = blk[ROW_CONV:ROW_CONV + CONV_K, :cw].reshape(conv_w.shape)

    return (loss, grad_x[None], *[out["grad"][n] for n in ORDER], *[out["delta"][n] for n in ORDER],
            *[out["new_m"][n] for n in ORDER], *[out["new_v"][n] for n in ORDER])
```
